```python
import jax, jax.numpy as jnp
from jax import lax
import numpy as np

D_MODEL = 1024
BATCH = 2
SEQ = 8192
DEPTH = 1

N_META = 16
D_CONV = D_MODEL
CONV_WIDTH = 31
D_RNN = D_MODEL
RNN_HEAD_DIM = 128
N_RNN_HEADS = D_RNN // RNN_HEAD_DIM
CHUNK = 128
CHUNK_PAD = CHUNK - N_META
N_EXPERTS = 32
TOP_K = 4
D_FF = D_MODEL
SWIGLU_LIMIT = 7.0
SWIGLU_ALPHA = 1.702
EXPERT_BLOCK = 128
RMS_EPS = 1e-6
LN_EPS = 1e-5
IN_SIZES = (D_CONV, D_CONV, D_RNN, D_RNN, D_RNN, D_RNN, D_MODEL, D_MODEL)
D_IN = sum(IN_SIZES)

kernel_name = "hybrid_conformer_hgrn2_moe_block"


def rmsnorm(x, g):
    xf = x.astype(jnp.float32)
    y = xf * lax.rsqrt(jnp.mean(xf * xf, axis=-1, keepdims=True) + RMS_EPS)
    return (y * g.astype(jnp.float32)).astype(x.dtype)


def layernorm(x, g, b):
    xf = x.astype(jnp.float32)
    mu = jnp.mean(xf, axis=-1, keepdims=True)
    xc = xf - mu
    var = jnp.mean(xc * xc, axis=-1, keepdims=True)
    y = xc * lax.rsqrt(var + LN_EPS) * g.astype(jnp.float32) + b.astype(jnp.float32)
    return y.astype(x.dtype)


def conformer_conv_branch(zv, zg, w_dw, b_dw, ln_g, ln_b, w_out, b_out):
    a = zv * jax.nn.sigmoid(zg)
    kern = w_dw.astype(a.dtype)[:, None, :]
    a = lax.conv_general_dilated(a, kern, window_strides=(1,), padding=[(CONV_WIDTH - 1, 0)],
                                 dimension_numbers=('NWC', 'WIO', 'NWC'),
                                 feature_group_count=D_CONV) + b_dw
    a = jax.nn.silu(layernorm(a, ln_g, ln_b))
    return a @ w_out + b_out


def hgrn2_chunk_scan(q, k, v, logf):
    Bsz, N, H, dk = q.shape
    dv = v.shape[-1]
    nC = N // CHUNK

    def to_chunks(t):
        return t.reshape(Bsz, nC, CHUNK, H, t.shape[-1]).transpose(1, 0, 3, 2, 4)

    causal = jnp.tril(jnp.ones((CHUNK, CHUNK), dtype=bool))[None, None, :, :, None]

    def step(S, inp):
        qc, kc, vc, lfc = inp
        b = jnp.cumsum(lfc, axis=-2)
        inter = jnp.einsum('bhtd,bhdv->bhtv', qc * jnp.exp(b), S)
        diff = b[:, :, :, None, :] - b[:, :, None, :, :]
        decay = jnp.exp(jnp.where(causal, diff, -jnp.inf))
        scores = jnp.einsum('bhtd,bhtsd,bhsd->bhts', qc, decay, kc)
        intra = jnp.einsum('bhts,bhsv->bhtv', scores, vc)
        b_last = b[:, :, -1:, :]
        S_new = jnp.exp(b_last[:, :, 0, :])[..., None] * S + \
            jnp.einsum('bhsd,bhsv->bhdv', kc * jnp.exp(b_last - b), vc)
        return S_new, inter + intra

    S0 = jnp.zeros((Bsz, H, dk, dv), jnp.float32)
    _, ys = lax.scan(step, S0, (to_chunks(q), to_chunks(k), to_chunks(v), to_chunks(logf)))
    return ys.transpose(1, 0, 3, 2, 4).reshape(Bsz, N, H, dv)


def hgrn2_branch(zq, zf, zi, zg, lb, g_onorm, w_out):
    Bsz, L = zq.shape[:2]

    def heads(t):
        return t.reshape(Bsz, L, N_RNN_HEADS, RNN_HEAD_DIM).astype(jnp.float32)

    lbh = lb.astype(jnp.float32).reshape(N_RNN_HEADS, RNN_HEAD_DIM)
    q = jax.nn.silu(heads(zq))
    f = lbh + (1.0 - lbh) * jax.nn.sigmoid(heads(zf))
    logf = jnp.log(f)
    v = heads(zi)
    pad = ((0, 0), (CHUNK_PAD, 0), (0, 0), (0, 0))
    q, v, logf = jnp.pad(q, pad), jnp.pad(v, pad), jnp.pad(logf, pad)
    k = -jnp.expm1(logf)
    o = hgrn2_chunk_scan(q, k, v, logf)[:, CHUNK_PAD:]
    o = o * lax.rsqrt(jnp.mean(o * o, axis=-1, keepdims=True) + RMS_EPS)
    o = o * g_onorm.astype(jnp.float32).reshape(N_RNN_HEADS, RNN_HEAD_DIM)
    o = o.reshape(Bsz, L, D_RNN).astype(zg.dtype) * jax.nn.silu(zg)
    return o @ w_out


def mixer_block(u, lb, w_in, w_dw, b_dw, ln_g, ln_b, w_conv_out, b_conv_out,
                g_onorm, w_rnn_out, w_o):
    z = u @ w_in
    split_points = np.cumsum(IN_SIZES)[:-1].tolist()
    zc_v, zc_g, zq, zf, zi, zg, ga, gb = jnp.split(z, split_points, axis=-1)
    a = conformer_conv_branch(zc_v, zc_g, w_dw, b_dw, ln_g, ln_b, w_conv_out, b_conv_out)
    r = hgrn2_branch(zq, zf, zi, zg, lb, g_onorm, w_rnn_out)
    y = jax.nn.sigmoid(ga) * a + jax.nn.sigmoid(gb) * r
    return y @ w_o


def moe_block(u, w_router, b_router, w_gu, b_gu, w_dn, b_dn):
    Bsz, L, D = u.shape
    T = Bsz * L
    TK = T * TOP_K
    xt = u.reshape(T, D)
    logits = (xt @ w_router + b_router).astype(jnp.float32)
    top_val, top_idx = lax.top_k(logits, TOP_K)
    gates = jax.nn.softmax(top_val, axis=-1)
    flat_e = top_idx.reshape(-1).astype(jnp.int32)
    flat_tok = jnp.arange(TK, dtype=jnp.int32) // TOP_K
    flat_w = gates.reshape(-1)
    order = jnp.argsort(flat_e)
    se, stok, sw = flat_e[order], flat_tok[order], flat_w[order]
    counts = jnp.bincount(flat_e, length=N_EXPERTS)
    starts = jnp.cumsum(counts) - counts
    padded = (counts + EXPERT_BLOCK - 1) // EXPERT_BLOCK * EXPERT_BLOCK
    pends = jnp.cumsum(padded)
    pstarts = pends - padded
    dest = pstarts[se] + (jnp.arange(TK, dtype=jnp.int32) - starts[se])
    n_blocks = -(-(TK + N_EXPERTS * (EXPERT_BLOCK - 1)) // EXPERT_BLOCK)
    n_slots = n_blocks * EXPERT_BLOCK
    slot_tok = jnp.full((n_slots,), T, jnp.int32).at[dest].set(stok)
    slot_w = jnp.zeros((n_slots,), jnp.float32).at[dest].set(sw)
    block_e = jnp.minimum(jnp.searchsorted(pends, jnp.arange(n_blocks) * EXPERT_BLOCK, side='right'),
                          N_EXPERTS - 1)
    x_pad = jnp.concatenate([xt, jnp.zeros((1, D), xt.dtype)], axis=0)
    xb = x_pad[slot_tok].reshape(n_blocks, EXPERT_BLOCK, D)

    def expert_block(args):
        xblk, e = args
        gu = xblk @ w_gu[e] + b_gu[e]
        gate, up = gu[:, :D_FF], gu[:, D_FF:]
        gate = jnp.minimum(gate, SWIGLU_LIMIT)
        up = jnp.clip(up, -SWIGLU_LIMIT, SWIGLU_LIMIT)
        hdn = (up + 1.0) * (gate * jax.nn.sigmoid(SWIGLU_ALPHA * gate))
        return hdn @ w_dn[e] + b_dn[e]

    yb = lax.map(expert_block, (xb, block_e)).reshape(n_slots, D)
    y = jnp.zeros((T + 1, D), u.dtype).at[slot_tok].add(yb * slot_w[:, None].astype(yb.dtype))
    return y[:T].reshape(Bsz, L, D)


def setup_inputs(seed: int = 0) -> dict:
    key = jax.random.key(seed)
    ks = jax.random.split(key, 24)
    f32 = jnp.float32

    def nrm(k, shape, scale):
        return jax.random.normal(k, shape, f32) * scale

    def gain(k, shape):
        return 1.0 + 0.02 * jax.random.normal(k, shape, f32)

    return {
        "x": nrm(ks[0], (BATCH, SEQ, D_MODEL), 1.0),
        "meta_tokens": nrm(ks[1], (N_META, D_MODEL), 1.0),
        "lb_logits": nrm(ks[2], (DEPTH + 1, D_RNN), 0.1),
        "g_mix": gain(ks[3], (DEPTH, D_MODEL)),
        "w_in": nrm(ks[4], (DEPTH, D_MODEL, D_IN), D_MODEL ** -0.5),
        "w_dw": nrm(ks[5], (DEPTH, CONV_WIDTH, D_CONV), CONV_WIDTH ** -0.5),
        "b_dw": nrm(ks[6], (DEPTH, D_CONV), 0.02),
        "ln_g": gain(ks[7], (DEPTH, D_CONV)),
        "ln_b": nrm(ks[8], (DEPTH, D_CONV), 0.02),
        "w_conv_out": nrm(ks[9], (DEPTH, D_CONV, D_MODEL), D_CONV ** -0.5),
        "b_conv_out": nrm(ks[10], (DEPTH, D_MODEL), 0.02),
        "g_onorm": gain(ks[11], (DEPTH, D_RNN)),
        "w_rnn_out": nrm(ks[12], (DEPTH, D_RNN, D_MODEL), D_RNN ** -0.5),
        "w_o": nrm(ks[13], (DEPTH, D_MODEL, D_MODEL), D_MODEL ** -0.5),
        "g_ffn": gain(ks[14], (DEPTH, D_MODEL)),
        "w_router": nrm(ks[15], (DEPTH, D_MODEL, N_EXPERTS), D_MODEL ** -0.5),
        "b_router": nrm(ks[16], (DEPTH, N_EXPERTS), 0.01),
        "w_gate_up": nrm(ks[17], (DEPTH, N_EXPERTS, D_MODEL, 2 * D_FF), D_MODEL ** -0.5),
        "b_gate_up": nrm(ks[18], (DEPTH, N_EXPERTS, 2 * D_FF), 0.02),
        "w_down": nrm(ks[19], (DEPTH, N_EXPERTS, D_FF, D_MODEL), D_FF ** -0.5),
        "b_down": nrm(ks[20], (DEPTH, N_EXPERTS, D_MODEL), 0.02),
        "g_final": gain(ks[21], (D_MODEL,)),
    }


def reference(x, meta_tokens, lb_logits, g_mix, w_in, w_dw, b_dw, ln_g, ln_b, w_conv_out,
              b_conv_out, g_onorm, w_rnn_out, w_o, g_ffn, w_router, b_router, w_gate_up,
              b_gate_up, w_down, b_down, g_final):
    Bsz = x.shape[0]
    meta = jnp.broadcast_to(meta_tokens.astype(x.dtype)[None], (Bsz, N_META, x.shape[-1]))
    h = jnp.concatenate([meta, x], axis=1)
    lower_bounds = jnp.cumsum(jax.nn.softmax(lb_logits.astype(jnp.float32), axis=0), axis=0)
    for l in range(DEPTH):
        h = h + mixer_block(rmsnorm(h, g_mix[l]), lower_bounds[l], w_in[l], w_dw[l], b_dw[l],
                            ln_g[l], ln_b[l], w_conv_out[l], b_conv_out[l], g_onorm[l],
                            w_rnn_out[l], w_o[l])
        h = h + moe_block(rmsnorm(h, g_ffn[l]), w_router[l], b_router[l], w_gate_up[l],
                          b_gate_up[l], w_down[l], b_down[l])
    return rmsnorm(h, g_final)[:, N_META:]
```

```python
import functools

import jax
import jax.numpy as jnp
from jax import lax
from jax.experimental import pallas as pl
from jax.experimental.pallas import tpu as pltpu

F32 = jnp.float32
BF16 = jnp.bfloat16

D_MODEL = 1024
N_META = 16
CHUNK = 128
CHUNK_PAD = CHUNK - N_META
CONV_WIDTH = 31
HEAD_DIM = 128
N_HEADS = D_MODEL // HEAD_DIM
N_EXPERTS = 32
TOP_K = 4
D_FF = D_MODEL
SWIGLU_LIMIT = 7.0
SWIGLU_ALPHA = 1.702
RMS_EPS = 1e-6
LN_EPS = 1e-5
N_IN_GROUPS = 8

LANES = 128
SUBLANES = 8
SUB_BLOCK = 32
N_SUB = CHUNK // SUB_BLOCK
HALO = 32
ROW_TILE = 640
IN_TILE = 320
CONV_ROWS = 32
EXPERT_TILE = 256
ROUTE_TILE = 128
NEG_BIG = -1e30
VMEM_LIMIT = 56 * 1024 * 1024


def _sigmoid(x):
    return 1.0 / (1.0 + jnp.exp(-x))


def _cparams(sem):
    return pltpu.CompilerParams(dimension_semantics=sem, vmem_limit_bytes=VMEM_LIMIT)


def _const_spec(shape):
    nd = len(shape)
    return pl.BlockSpec(shape, lambda *_: (0,) * nd)


def _in_proj_kernel(h_ref, g_ref, lbl_ref, w_ref, aglu_ref, q_ref, k_ref, logf_ref, v_ref,
                    og_ref, sga_ref, sgb_ref):
    d = D_MODEL
    h = h_ref[...]
    ms = jnp.mean(h * h, axis=-1, keepdims=True)
    u = (h * lax.rsqrt(ms + RMS_EPS) * g_ref[...]).astype(BF16)

    def proj(j):
        return jnp.dot(u, w_ref[:, j * d:(j + 1) * d], preferred_element_type=F32)

    aglu_ref[...] = (proj(0) * _sigmoid(proj(1))).astype(aglu_ref.dtype)
    zq = proj(2)
    q_ref[...] = (zq * _sigmoid(zq)).astype(q_ref.dtype)
    lbl = lbl_ref[...]
    e = jnp.exp(lbl - jnp.max(lbl, axis=0, keepdims=True))
    lb = e[0:1, :] / jnp.sum(e, axis=0, keepdims=True)
    s = _sigmoid(proj(3))
    logf_ref[...] = jnp.log(lb + (1.0 - lb) * s)
    k_ref[...] = ((1.0 - lb) * (1.0 - s)).astype(k_ref.dtype)
    v_ref[...] = proj(4).astype(v_ref.dtype)
    zg = proj(5)
    og_ref[...] = (zg * _sigmoid(zg)).astype(og_ref.dtype)
    sga_ref[...] = _sigmoid(proj(6)).astype(sga_ref.dtype)
    sgb_ref[...] = _sigmoid(proj(7)).astype(sgb_ref.dtype)


def _in_proj(hp, g_mix, lb_logits, w_in_bf16):
    tp, d = hp.shape
    tm = IN_TILE
    row = pl.BlockSpec((tm, d), lambda i: (i, 0))
    out_dtypes = [BF16, BF16, BF16, F32, BF16, BF16, BF16, BF16]
    return pl.pallas_call(
        _in_proj_kernel,
        grid=(tp // tm,),
        in_specs=[row, _const_spec((1, d)), _const_spec(lb_logits.shape),
                  pl.BlockSpec(w_in_bf16.shape, lambda i: (0, 0), pipeline_mode=pl.Buffered(1))],
        out_specs=[row] * 8,
        out_shape=[jax.ShapeDtypeStruct((tp, d), dt) for dt in out_dtypes],
        compiler_params=_cparams(("parallel",)),
        name="in_proj",
    )(hp, g_mix, lb_logits, w_in_bf16)


def _conv_kernel(halo_ref, cur_ref, wdw_ref, bdw_ref, lng_ref, lnb_ref, wout_ref, bout_ref, sga_ref,
                 out_ref, win_ref, acc_ref, shift_ref):
    tm = cur_ref.shape[0]
    win_ref[0:HALO, :] = halo_ref[...].astype(F32)
    win_ref[HALO:, :] = cur_ref[...].astype(F32)
    first_tap = HALO - (CONV_WIDTH - 1)
    for c in range(D_MODEL // LANES):
        lanes = slice(c * LANES, (c + 1) * LANES)
        for s in range(SUBLANES):
            n_rows = tm + HALO - (SUBLANES if s else 0)
            shift_ref[s, 0:n_rows, :] = win_ref[pl.ds(s, n_rows), lanes]
        w_c = wdw_ref[:, lanes]
        bias = jnp.broadcast_to(bdw_ref[:, lanes], (CONV_ROWS, LANES))

        def chunk(r, carry, lanes=lanes, w_c=w_c, bias=bias):
            r0 = pl.multiple_of(r * CONV_ROWS, CONV_ROWS)
            acc = bias
            for j in range(CONV_WIDTH):
                off = first_tap + j
                rows = pl.ds(r0 + off - off % SUBLANES, CONV_ROWS)
                acc = acc + w_c[j:j + 1, :] * shift_ref[off % SUBLANES, rows, :]
            acc_ref[pl.ds(r0, CONV_ROWS), lanes] = acc
            return carry

        lax.fori_loop(0, tm // CONV_ROWS, chunk, 0)
    a = acc_ref[...]
    mu = jnp.mean(a, axis=-1, keepdims=True)
    ac = a - mu
    var = jnp.mean(ac * ac, axis=-1, keepdims=True)
    y = ac * lax.rsqrt(var + LN_EPS) * lng_ref[...] + lnb_ref[...]
    y = y * _sigmoid(y)
    o = jnp.dot(y.astype(BF16), wout_ref[...], preferred_element_type=F32) + bout_ref[...]
    out_ref[...] = (sga_ref[...].astype(F32) * o).astype(out_ref.dtype)


def _conv_branch(aglu, w_dw, b_dw, ln_g, ln_b, w_out_bf16, b_out, sga):
    tp, d = aglu.shape
    tm = ROW_TILE
    per = tm // HALO
    row = pl.BlockSpec((tm, d), lambda i: (i, 0))
    halo = pl.BlockSpec((HALO, d), lambda i: (jnp.maximum(i * per - 1, 0), 0))
    vec = _const_spec((1, d))
    return pl.pallas_call(
        _conv_kernel,
        grid=(tp // tm,),
        in_specs=[halo, row, _const_spec(w_dw.shape), vec, vec, vec, _const_spec((d, d)), vec, row],
        out_specs=row,
        out_shape=jax.ShapeDtypeStruct((tp, d), BF16),
        scratch_shapes=[pltpu.VMEM((tm + HALO, d), F32), pltpu.VMEM((tm, d), F32),
                        pltpu.VMEM((SUBLANES, tm + HALO, LANES), F32)],
        compiler_params=_cparams(("parallel",)),
        name="conv_branch",
    )(aglu, aglu, w_dw, b_dw, ln_g, ln_b, w_out_bf16, b_out, sga)


def _nt_dot(a, b):
    return lax.dot_general(a, b, (((1,), (1,)), ((), ())), preferred_element_type=F32)


def _tn_dot(a, b):
    return lax.dot_general(a, b, (((0,), (0,)), ((), ())), preferred_element_type=F32)


def _scan_kernel(q_ref, k_ref, v_ref, lf_ref, og_ref, gon_ref, o_ref, st_ref):
    @pl.when(pl.program_id(1) == 0)
    def _():
        st_ref[...] = jnp.zeros_like(st_ref)

    c = CHUNK
    row = lax.broadcasted_iota(jnp.int32, (c, c), 0)
    col = lax.broadcasted_iota(jnp.int32, (c, c), 1)
    causal = col <= row
    rblk = row // SUB_BLOCK
    cblk = col // SUB_BLOCK
    diag_mask = jnp.logical_and(causal, rblk == cblk)
    bcum = jnp.dot(causal.astype(F32), lf_ref[...], preferred_element_type=F32,
                   precision=lax.Precision.HIGHEST)

    def bcast_rows(rows):
        return jnp.concatenate([jnp.broadcast_to(r, (SUB_BLOCK, HEAD_DIM)) for r in rows], axis=0)

    for h in range(N_HEADS):
        hs = slice(h * HEAD_DIM, (h + 1) * HEAD_DIM)
        b = bcum[:, hs]
        q = q_ref[:, hs].astype(F32)
        k = k_ref[:, hs].astype(F32)
        v = v_ref[:, hs]
        ends = [b[i * SUB_BLOCK + SUB_BLOCK - 1:i * SUB_BLOCK + SUB_BLOCK, :] for i in range(N_SUB)]
        mids = [b[i * SUB_BLOCK + SUB_BLOCK // 2 - 1:i * SUB_BLOCK + SUB_BLOCK // 2, :] for i in range(N_SUB)]
        mid_full = bcast_rows(mids)
        end_full = bcast_rows(ends)
        b_last = ends[-1]
        qm = (q * jnp.exp(b - mid_full)).astype(BF16)
        km = (k * jnp.exp(mid_full - b)).astype(BF16)
        scores = jnp.where(diag_mask, _nt_dot(qm, km), 0.0)
        ke = k * jnp.exp(end_full - b)
        q_parts, k_parts = [], []
        for j in range(N_SUB - 1):
            arg = jnp.where(rblk > j, b - ends[j], NEG_BIG)
            q_parts.append((q * jnp.exp(arg)).astype(BF16))
            k_parts.append(jnp.where(rblk == j, ke, 0.0).astype(BF16))
        scores = scores + _nt_dot(jnp.concatenate(q_parts, axis=1), jnp.concatenate(k_parts, axis=1))
        st = st_ref[h]
        q_in = (q * jnp.exp(b)).astype(BF16)
        o = _nt_dot(q_in, st.astype(BF16)) + jnp.dot(scores.astype(BF16), v, preferred_element_type=F32)
        k_out = (k * jnp.exp(b_last - b)).astype(BF16)
        st_ref[h] = st * jnp.exp(b_last) + _tn_dot(v, k_out)
        ms = jnp.mean(o * o, axis=-1, keepdims=True)
        on = o * lax.rsqrt(ms + RMS_EPS) * gon_ref[:, hs]
        o_ref[:, hs] = (on * og_ref[:, hs].astype(F32)).astype(o_ref.dtype)


def _hgrn2_scan(q, k, v, logf, og, g_onorm, n_batch):
    tp, d = q.shape
    n_chunks = tp // n_batch // CHUNK
    blk = pl.BlockSpec((CHUNK, d), lambda b, c: (b * n_chunks + c, 0))
    return pl.pallas_call(
        _scan_kernel,
        grid=(n_batch, n_chunks),
        in_specs=[blk, blk, blk, blk, blk, pl.BlockSpec((1, d), lambda b, c: (0, 0))],
        out_specs=blk,
        out_shape=jax.ShapeDtypeStruct((tp, d), BF16),
        scratch_shapes=[pltpu.VMEM((N_HEADS, HEAD_DIM, HEAD_DIM), F32)],
        compiler_params=_cparams(("arbitrary", "arbitrary")),
        name="hgrn2_scan",
    )(q, k, v, logf, og, g_onorm)


def _merge_route_kernel(h_ref, on_ref, ap_ref, sgb_ref, wrnn_ref, wo_ref, gffn_ref, wr_ref, br_ref,
                        h1_ref, u2_ref, info_ref, cnt_ref, tri_ref, carry_ref, colcnt_ref):
    tm = h_ref.shape[0]

    @pl.when(pl.program_id(0) == 0)
    def _():
        r_i = lax.broadcasted_iota(jnp.int32, (tm, tm), 0)
        c_i = lax.broadcasted_iota(jnp.int32, (tm, tm), 1)
        tri_ref[...] = (c_i < r_i).astype(BF16)
        carry_ref[...] = jnp.zeros_like(carry_ref)
        colcnt_ref[...] = jnp.zeros_like(colcnt_ref)

    r = jnp.dot(on_ref[...], wrnn_ref[...], preferred_element_type=F32)
    y = ap_ref[...].astype(F32) + sgb_ref[...].astype(F32) * r
    h1 = h_ref[...] + jnp.dot(y.astype(BF16), wo_ref[...], preferred_element_type=F32)
    h1_ref[...] = h1
    ms = jnp.mean(h1 * h1, axis=-1, keepdims=True)
    u2 = h1 * lax.rsqrt(ms + RMS_EPS) * gffn_ref[...]
    u2_ref[...] = u2
    logits = jnp.dot(u2, wr_ref[...], preferred_element_type=F32,
                     precision=lax.Precision.HIGHEST) + br_ref[...]
    lane = lax.broadcasted_iota(jnp.int32, (tm, LANES), 1)
    cur = jnp.where(lane < N_EXPERTS, logits, NEG_BIG)
    vals, idxs, sels = [], [], []
    for _ in range(TOP_K):
        m = jnp.max(cur, axis=-1, keepdims=True)
        idx = jnp.min(jnp.where(cur == m, lane, LANES), axis=-1, keepdims=True)
        sel = lane == idx
        cur = jnp.where(sel, 2.0 * NEG_BIG, cur)
        vals.append(m)
        idxs.append(idx)
        sels.append(sel)
    exps = [jnp.exp(vk - vals[0]) for vk in vals]
    den = exps[0] + exps[1] + exps[2] + exps[3]
    onehot = jnp.logical_or(jnp.logical_or(sels[0], sels[1]), jnp.logical_or(sels[2], sels[3]))
    onehot_bf = onehot.astype(BF16)
    rank_all = jnp.dot(tri_ref[...], onehot_bf, preferred_element_type=F32) + carry_ref[...]
    carry_ref[...] += jnp.sum(onehot.astype(F32), axis=0, keepdims=True)
    colcnt_ref[...] += _tn_dot(onehot_bf, jnp.ones((tm, LANES), BF16))
    info = jnp.zeros((tm, LANES), F32)
    for kk in range(TOP_K):
        rank_k = jnp.sum(jnp.where(sels[kk], rank_all, 0.0), axis=-1, keepdims=True)
        info = jnp.where(lane == kk, exps[kk] / den, info)
        info = jnp.where(lane == TOP_K + kk, idxs[kk].astype(F32), info)
        info = jnp.where(lane == 2 * TOP_K + kk, rank_k, info)
    info_ref[...] = info
    cnt_ref[...] = colcnt_ref[...]


def _merge_route(hp, on, ap, sgb, w_rnn_bf16, w_o_bf16, g_ffn, w_router_pad, b_router_pad):
    tp, d = hp.shape
    tm = ROW_TILE
    row = pl.BlockSpec((tm, d), lambda i: (i, 0))
    vec = _const_spec((1, d))
    return pl.pallas_call(
        _merge_route_kernel,
        grid=(tp // tm,),
        in_specs=[row, row, row, row, _const_spec((d, d)), _const_spec((d, d)), vec,
                  _const_spec((d, LANES)), _const_spec((1, LANES))],
        out_specs=[row, row, pl.BlockSpec((tm, LANES), lambda i: (i, 0)), _const_spec((LANES, LANES))],
        out_shape=[jax.ShapeDtypeStruct((tp, d), F32), jax.ShapeDtypeStruct((tp, d), F32),
                   jax.ShapeDtypeStruct((tp, LANES), F32), jax.ShapeDtypeStruct((LANES, LANES), F32)],
        scratch_shapes=[pltpu.VMEM((tm, tm), BF16), pltpu.VMEM((1, LANES), F32),
                        pltpu.VMEM((LANES, LANES), F32)],
        compiler_params=_cparams(("arbitrary",)),
        name="merge_route",
    )(hp, on, ap, sgb, w_rnn_bf16, w_o_bf16, g_ffn, w_router_pad, b_router_pad)


def _slots_kernel(info_ref, cnt_ref, dest_ref, te_ref):
    tm = info_ref.shape[0]
    n_tile_lanes = te_ref.shape[1]
    r_i = lax.broadcasted_iota(jnp.int32, (LANES, LANES), 0)
    c_i = lax.broadcasted_iota(jnp.int32, (LANES, LANES), 1)
    cnt = cnt_ref[...]
    tiles = jnp.floor((cnt + (EXPERT_TILE - 1)) / EXPERT_TILE)
    pend_col = jnp.dot((c_i <= r_i).astype(F32), tiles, preferred_element_type=F32,
                       precision=lax.Precision.HIGHEST)
    pstart_row = _tn_dot_f32(tiles, (r_i < c_i).astype(F32))
    info = info_ref[...]
    lane = lax.broadcasted_iota(jnp.int32, (tm, LANES), 1)
    dest = jnp.zeros((tm, LANES), F32)
    for kk in range(TOP_K):
        idx_k = info[:, TOP_K + kk:TOP_K + kk + 1].astype(jnp.int32)
        start_k = jnp.sum(jnp.where(lane == idx_k, pstart_row[0:1, :], 0.0), axis=-1, keepdims=True)
        slot_k = start_k * EXPERT_TILE + info[:, 2 * TOP_K + kk:2 * TOP_K + kk + 1]
        dest = jnp.where(lane == kk, slot_k, dest)
    dest_ref[...] = dest.astype(jnp.int32)
    tile_id = lax.broadcasted_iota(jnp.int32, (LANES, n_tile_lanes), 1).astype(F32)
    exp_id = lax.broadcasted_iota(jnp.int32, (LANES, n_tile_lanes), 0)
    pend_wide = jnp.concatenate([pend_col] * (n_tile_lanes // LANES), axis=1)
    below = jnp.logical_and(pend_wide <= tile_id, exp_id < N_EXPERTS)
    te = jnp.minimum(jnp.sum(below.astype(F32), axis=0, keepdims=True), N_EXPERTS - 1.0)
    n_active = pend_col[N_EXPERTS - 1:N_EXPERTS, 0:1]
    lane_t = lax.broadcasted_iota(jnp.int32, (8, n_tile_lanes), 1)
    sub_t = lax.broadcasted_iota(jnp.int32, (8, n_tile_lanes), 0)
    out = jnp.where(sub_t == 0, jnp.broadcast_to(te, (8, n_tile_lanes)),
                    jnp.broadcast_to(n_active, (8, n_tile_lanes)))
    del lane_t
    te_ref[...] = out.astype(jnp.int32)


def _tn_dot_f32(a, b):
    return lax.dot_general(a, b, (((0,), (0,)), ((), ())), preferred_element_type=F32,
                           precision=lax.Precision.HIGHEST)


def _slots(info, cnt, n_tiles_max):
    tp = info.shape[0]
    tm = ROW_TILE
    n_tile_lanes = -(-n_tiles_max // LANES) * LANES
    return pl.pallas_call(
        _slots_kernel,
        grid=(tp // tm,),
        in_specs=[pl.BlockSpec((tm, LANES), lambda i: (i, 0)), _const_spec((LANES, LANES))],
        out_specs=[pl.BlockSpec((tm, LANES), lambda i: (i, 0)), _const_spec((8, n_tile_lanes))],
        out_shape=[jax.ShapeDtypeStruct((tp, LANES), jnp.int32),
                   jax.ShapeDtypeStruct((8, n_tile_lanes), jnp.int32)],
        compiler_params=_cparams(("arbitrary",)),
        name="slots",
    )(info, cnt)


def _row_copy(src_ref, src_row, dst_ref, dst_row, sem):
    return pltpu.make_async_copy(src_ref.at[pl.ds(src_row, 1), :], dst_ref.at[pl.ds(dst_row, 1), :], sem)


def _fetch_dest(dest_hbm, dest_smem, dsem, step, slot):
    return pltpu.make_async_copy(dest_hbm.at[step], dest_smem.at[slot], dsem.at[slot])


def _dispatch_kernel(dest_hbm, u2_ref, xs_in, xs_ref, dest_smem, dsem, sem):
    del xs_in
    i = pl.program_id(0)
    n = pl.num_programs(0)
    slot = i % 2

    @pl.when(i == 0)
    def _():
        _fetch_dest(dest_hbm, dest_smem, dsem, 0, 0).start()

    _fetch_dest(dest_hbm, dest_smem, dsem, i, slot).wait()

    @pl.when(i + 1 < n)
    def _():
        _fetch_dest(dest_hbm, dest_smem, dsem, i + 1, 1 - slot).start()

    def issue(t, carry):
        for kk in range(TOP_K):
            _row_copy(u2_ref, t, xs_ref, dest_smem[slot, t * TOP_K + kk], sem).start()
        return carry

    lax.fori_loop(0, ROUTE_TILE, issue, 0)

    def drain(t, carry):
        _row_copy(u2_ref, 0, xs_ref, 0, sem).wait()
        return carry

    lax.fori_loop(0, ROUTE_TILE * TOP_K, drain, 0)


def _dispatch(dest2, u2, xs_zero):
    tp, d = u2.shape
    n_steps = tp // ROUTE_TILE
    return pl.pallas_call(
        _dispatch_kernel,
        grid=(n_steps,),
        in_specs=[pl.BlockSpec(memory_space=pl.ANY),
                  pl.BlockSpec((ROUTE_TILE, d), lambda i: (i, 0)),
                  pl.BlockSpec(memory_space=pl.ANY)],
        out_specs=pl.BlockSpec(memory_space=pl.ANY),
        out_shape=jax.ShapeDtypeStruct(xs_zero.shape, xs_zero.dtype),
        scratch_shapes=[pltpu.SMEM((2, ROUTE_TILE * TOP_K), jnp.int32),
                        pltpu.SemaphoreType.DMA((2,)), pltpu.SemaphoreType.DMA],
        input_output_aliases={2: 0},
        compiler_params=_cparams(("arbitrary",)),
        name="dispatch",
    )(dest2, u2, xs_zero)


def _expert_kernel(te_ref, nact_ref, x_ref, wgu_ref, bgu_ref, wdn_ref, bdn_ref, y_ref):
    del te_ref

    @pl.when(pl.program_id(0) < nact_ref[0])
    def _():
        x = x_ref[...].astype(BF16)
        gu = jnp.dot(x, wgu_ref[0], preferred_element_type=F32) + bgu_ref[0]
        gate = jnp.minimum(gu[:, :D_FF], SWIGLU_LIMIT)
        up = jnp.clip(gu[:, D_FF:], -SWIGLU_LIMIT, SWIGLU_LIMIT)
        hdn = (up + 1.0) * (gate * _sigmoid(SWIGLU_ALPHA * gate))
        y_ref[...] = jnp.dot(hdn.astype(BF16), wdn_ref[0], preferred_element_type=F32) + bdn_ref[0]

    @pl.when(pl.program_id(0) >= nact_ref[0])
    def _():
        y_ref[...] = jnp.zeros_like(y_ref)


def _expert_mlp(te, nact, xs, w_gu_bf16, b_gu, w_dn_bf16, b_dn):
    ns, d = xs.shape
    n_tiles = ns // EXPERT_TILE

    def tile_map(i, te_ref, nact_ref):
        return (jnp.minimum(i, nact_ref[0] - 1), 0)

    def exp_map(i, te_ref, nact_ref):
        return (te_ref[jnp.minimum(i, nact_ref[0] - 1)], 0, 0)

    grid_spec = pltpu.PrefetchScalarGridSpec(
        num_scalar_prefetch=2,
        grid=(n_tiles,),
        in_specs=[pl.BlockSpec((EXPERT_TILE, d), tile_map),
                  pl.BlockSpec((1, d, 2 * D_FF), exp_map),
                  pl.BlockSpec((1, 1, 2 * D_FF), exp_map),
                  pl.BlockSpec((1, D_FF, d), exp_map),
                  pl.BlockSpec((1, 1, d), exp_map)],
        out_specs=pl.BlockSpec((EXPERT_TILE, d), lambda i, te_ref, nact_ref: (i, 0)),
    )
    return pl.pallas_call(
        _expert_kernel,
        grid_spec=grid_spec,
        out_shape=jax.ShapeDtypeStruct((ns, d), F32),
        compiler_params=_cparams(("arbitrary",)),
        name="expert_mlp",
    )(te, nact, xs, w_gu_bf16, b_gu, w_dn_bf16, b_dn)


def _combine_kernel(dest_hbm, h1_ref, info_ref, gfin_ref, ys_ref, out_ref, dest_smem, ybuf, dsem, sem):
    c = pl.program_id(1)

    @pl.when(c > 0)
    def _():
        step = pl.program_id(0) * pl.num_programs(1) + c
        fetch = pltpu.make_async_copy(dest_hbm.at[step], dest_smem, dsem)
        fetch.start()
        fetch.wait()

        def issue(t, carry):
            for kk in range(TOP_K):
                _row_copy(ys_ref, dest_smem[t * TOP_K + kk], ybuf.at[kk], t, sem).start()
            return carry

        lax.fori_loop(0, ROUTE_TILE, issue, 0)

        def drain(t, carry):
            _row_copy(ys_ref, 0, ybuf.at[0], 0, sem).wait()
            return carry

        lax.fori_loop(0, ROUTE_TILE * TOP_K, drain, 0)
        info = info_ref[...]
        h2 = h1_ref[...]
        for kk in range(TOP_K):
            h2 = h2 + info[:, kk:kk + 1] * ybuf[kk]
        ms = jnp.mean(h2 * h2, axis=-1, keepdims=True)
        out_ref[...] = h2 * lax.rsqrt(ms + RMS_EPS) * gfin_ref[...]


def _combine(dest2, h1, info, g_final, ys, n_batch, seq):
    tp, d = h1.shape
    n_chunks = tp // n_batch // ROUTE_TILE
    out_chunks = seq // ROUTE_TILE
    blk = lambda w: pl.BlockSpec((ROUTE_TILE, w), lambda b, c: (b * n_chunks + c, 0))
    return pl.pallas_call(
        _combine_kernel,
        grid=(n_batch, n_chunks),
        in_specs=[pl.BlockSpec(memory_space=pl.ANY), blk(d), blk(LANES),
                  pl.BlockSpec((1, d), lambda b, c: (0, 0)), pl.BlockSpec(memory_space=pl.ANY)],
        out_specs=pl.BlockSpec((ROUTE_TILE, d), lambda b, c: (b * out_chunks + jnp.maximum(c - 1, 0), 0)),
        out_shape=jax.ShapeDtypeStruct((n_batch * seq, d), F32),
        scratch_shapes=[pltpu.SMEM((ROUTE_TILE * TOP_K,), jnp.int32),
                        pltpu.VMEM((TOP_K, ROUTE_TILE, d), F32),
                        pltpu.SemaphoreType.DMA, pltpu.SemaphoreType.DMA],
        compiler_params=_cparams(("arbitrary", "arbitrary")),
        name="combine",
    )(dest2, h1, info, g_final, ys)


def kernel(x, meta_tokens, lb_logits, g_mix, w_in, w_dw, b_dw, ln_g, ln_b, w_conv_out, b_conv_out,
           g_onorm, w_rnn_out, w_o, g_ffn, w_router, b_router, w_gate_up, b_gate_up, w_down, b_down,
           g_final):
    n_batch, seq, d = x.shape
    assert d == D_MODEL and w_in.shape[0] == 1, "single-layer block with D_MODEL features"
    assert CHUNK == ROUTE_TILE and seq % CHUNK == 0
    seq_pad = CHUNK + seq
    tp = n_batch * seq_pad
    assert tp % ROW_TILE == 0

    meta = jnp.broadcast_to(meta_tokens.astype(x.dtype)[None], (n_batch, N_META, d))
    hp = jnp.concatenate([jnp.zeros((n_batch, CHUNK_PAD, d), x.dtype), meta, x], axis=1).reshape(tp, d)
    vec = lambda a: a.reshape(1, -1).astype(F32)

    aglu, q, k, logf, v, og, sga, sgb = _in_proj(hp, vec(g_mix[0]), lb_logits.astype(F32), w_in[0].astype(BF16))
    ap = _conv_branch(aglu, w_dw[0].astype(F32), vec(b_dw[0]), vec(ln_g[0]), vec(ln_b[0]),
                      w_conv_out[0].astype(BF16), vec(b_conv_out[0]), sga)
    on = _hgrn2_scan(q, k, v, logf, og, vec(g_onorm[0]), n_batch)
    w_router_pad = jnp.pad(w_router[0].astype(F32), ((0, 0), (0, LANES - N_EXPERTS)))
    b_router_pad = jnp.pad(vec(b_router[0]), ((0, 0), (0, LANES - N_EXPERTS)))
    h1, u2, info, cnt = _merge_route(hp, on, ap, sgb, w_rnn_out[0].astype(BF16), w_o[0].astype(BF16),
                                     vec(g_ffn[0]), w_router_pad, b_router_pad)

    n_tiles_max = -(-(tp * TOP_K + N_EXPERTS * (EXPERT_TILE - 1)) // EXPERT_TILE)
    dest, te = _slots(info, cnt, n_tiles_max)
    dest2 = dest[:, :TOP_K].reshape(tp // ROUTE_TILE, ROUTE_TILE * TOP_K)
    xs = _dispatch(dest2, u2, jnp.zeros((n_tiles_max * EXPERT_TILE, d), F32))
    ys = _expert_mlp(te[0, :n_tiles_max], te[1, :1], xs, w_gate_up[0].astype(BF16),
                     b_gate_up[0].reshape(N_EXPERTS, 1, -1).astype(F32), w_down[0].astype(BF16),
                     b_down[0].reshape(N_EXPERTS, 1, -1).astype(F32))
    out = _combine(dest2, h1, info, vec(g_final), ys, n_batch, seq)
    return out.reshape(n_batch, seq, d)
```

```python
import functools

import jax
import jax.numpy as jnp
from jax import lax
from jax.experimental import pallas as pl
from jax.experimental.pallas import tpu as pltpu

F32 = jnp.float32
BF16 = jnp.bfloat16

D_MODEL = 1024
N_META = 16
CHUNK = 128
CHUNK_PAD = CHUNK - N_META
CONV_WIDTH = 31
HEAD_DIM = 128
N_HEADS = D_MODEL // HEAD_DIM
N_EXPERTS = 32
TOP_K = 4
D_FF = D_MODEL
SWIGLU_LIMIT = 7.0
SWIGLU_ALPHA = 1.702
RMS_EPS = 1e-6
LN_EPS = 1e-5
N_IN_GROUPS = 8

LANES = 128
SUBLANES = 8
SUB_BLOCK = 32
N_SUB = CHUNK // SUB_BLOCK
HALO = 32
ROW_TILE = 640
IN_TILE = 320
CONV_ROWS = 32
EXPERT_TILE = 256
ROUTE_TILE = 128
NEG_BIG = -1e30
VMEM_LIMIT = 56 * 1024 * 1024


def _sigmoid(x):
    return 1.0 / (1.0 + jnp.exp(-x))


def _cparams(sem):
    return pltpu.CompilerParams(dimension_semantics=sem, vmem_limit_bytes=VMEM_LIMIT)


def _const_spec(shape):
    nd = len(shape)
    return pl.BlockSpec(shape, lambda *_: (0,) * nd)


PACK_ROWS = D_MODEL // 2 // 128
HIGH_HALF = 0xFFFF0000


def _store_packed(ref, x):
    rows = x.shape[0]
    half = D_MODEL // 2
    lo = lax.bitcast_convert_type(x[:, :half].astype(BF16).astype(F32), jnp.uint32)
    hi = lax.bitcast_convert_type(x[:, half:].astype(BF16).astype(F32), jnp.uint32)
    words = (lo >> 16) | (hi & jnp.uint32(HIGH_HALF))
    for j in range(PACK_ROWS):
        ref[pl.ds(j, rows, stride=PACK_ROWS), :] = words[:, j * LANES:(j + 1) * LANES]


def _load_packed(ref, rows, lead=()):
    lo, hi = [], []
    for j in range(PACK_ROWS):
        w = ref[lead + (pl.ds(j, rows, stride=PACK_ROWS), slice(None))]
        lo.append(lax.bitcast_convert_type(w << 16, F32))
        hi.append(lax.bitcast_convert_type(w & jnp.uint32(HIGH_HALF), F32))
    return jnp.concatenate(lo + hi, axis=1)


def _in_proj_kernel(h_ref, g_ref, lbl_ref, w_ref, aglu_ref, q_ref, k_ref, logf_ref, v_ref,
                    og_ref, sga_ref, sgb_ref):
    d = D_MODEL
    h = h_ref[...]
    ms = jnp.mean(h * h, axis=-1, keepdims=True)
    u = (h * lax.rsqrt(ms + RMS_EPS) * g_ref[...]).astype(BF16)

    def proj(j):
        return jnp.dot(u, w_ref[:, j * d:(j + 1) * d], preferred_element_type=F32)

    aglu_ref[...] = (proj(0) * _sigmoid(proj(1))).astype(aglu_ref.dtype)
    zq = proj(2)
    q_ref[...] = (zq * _sigmoid(zq)).astype(q_ref.dtype)
    lbl = lbl_ref[...]
    e = jnp.exp(lbl - jnp.max(lbl, axis=0, keepdims=True))
    lb = e[0:1, :] / jnp.sum(e, axis=0, keepdims=True)
    s = _sigmoid(proj(3))
    logf_ref[...] = jnp.log(lb + (1.0 - lb) * s)
    k_ref[...] = ((1.0 - lb) * (1.0 - s)).astype(k_ref.dtype)
    v_ref[...] = proj(4).astype(v_ref.dtype)
    zg = proj(5)
    og_ref[...] = (zg * _sigmoid(zg)).astype(og_ref.dtype)
    sga_ref[...] = _sigmoid(proj(6)).astype(sga_ref.dtype)
    sgb_ref[...] = _sigmoid(proj(7)).astype(sgb_ref.dtype)


def _in_proj(hp, g_mix, lb_logits, w_in_bf16):
    tp, d = hp.shape
    tm = IN_TILE
    row = pl.BlockSpec((tm, d), lambda i: (i, 0))
    out_dtypes = [BF16, BF16, BF16, F32, BF16, BF16, BF16, BF16]
    return pl.pallas_call(
        _in_proj_kernel,
        grid=(tp // tm,),
        in_specs=[row, _const_spec((1, d)), _const_spec(lb_logits.shape),
                  pl.BlockSpec(w_in_bf16.shape, lambda i: (0, 0), pipeline_mode=pl.Buffered(1))],
        out_specs=[row] * 8,
        out_shape=[jax.ShapeDtypeStruct((tp, d), dt) for dt in out_dtypes],
        compiler_params=_cparams(("parallel",)),
        name="in_proj",
    )(hp, g_mix, lb_logits, w_in_bf16)


def _conv_kernel(halo_ref, cur_ref, wdw_ref, bdw_ref, lng_ref, lnb_ref, wout_ref, bout_ref, sga_ref,
                 out_ref, win_ref, acc_ref, shift_ref):
    tm = cur_ref.shape[0]
    win_ref[0:HALO, :] = halo_ref[...].astype(F32)
    win_ref[HALO:, :] = cur_ref[...].astype(F32)
    first_tap = HALO - (CONV_WIDTH - 1)
    for c in range(D_MODEL // LANES):
        lanes = slice(c * LANES, (c + 1) * LANES)
        for s in range(SUBLANES):
            n_rows = tm + HALO - (SUBLANES if s else 0)
            shift_ref[s, 0:n_rows, :] = win_ref[pl.ds(s, n_rows), lanes]
        w_c = wdw_ref[:, lanes]
        bias = jnp.broadcast_to(bdw_ref[:, lanes], (CONV_ROWS, LANES))

        def chunk(r, carry, lanes=lanes, w_c=w_c, bias=bias):
            r0 = pl.multiple_of(r * CONV_ROWS, CONV_ROWS)
            acc = bias
            for j in range(CONV_WIDTH):
                off = first_tap + j
                rows = pl.ds(r0 + off - off % SUBLANES, CONV_ROWS)
                acc = acc + w_c[j:j + 1, :] * shift_ref[off % SUBLANES, rows, :]
            acc_ref[pl.ds(r0, CONV_ROWS), lanes] = acc
            return carry

        lax.fori_loop(0, tm // CONV_ROWS, chunk, 0)
    a = acc_ref[...]
    mu = jnp.mean(a, axis=-1, keepdims=True)
    ac = a - mu
    var = jnp.mean(ac * ac, axis=-1, keepdims=True)
    y = ac * lax.rsqrt(var + LN_EPS) * lng_ref[...] + lnb_ref[...]
    y = y * _sigmoid(y)
    o = jnp.dot(y.astype(BF16), wout_ref[...], preferred_element_type=F32) + bout_ref[...]
    out_ref[...] = (sga_ref[...].astype(F32) * o).astype(out_ref.dtype)


def _conv_branch(aglu, w_dw, b_dw, ln_g, ln_b, w_out_bf16, b_out, sga):
    tp, d = aglu.shape
    tm = ROW_TILE
    per = tm // HALO
    row = pl.BlockSpec((tm, d), lambda i: (i, 0))
    halo = pl.BlockSpec((HALO, d), lambda i: (jnp.maximum(i * per - 1, 0), 0))
    vec = _const_spec((1, d))
    return pl.pallas_call(
        _conv_kernel,
        grid=(tp // tm,),
        in_specs=[halo, row, _const_spec(w_dw.shape), vec, vec, vec, _const_spec((d, d)), vec, row],
        out_specs=row,
        out_shape=jax.ShapeDtypeStruct((tp, d), BF16),
        scratch_shapes=[pltpu.VMEM((tm + HALO, d), F32), pltpu.VMEM((tm, d), F32),
                        pltpu.VMEM((SUBLANES, tm + HALO, LANES), F32)],
        compiler_params=_cparams(("parallel",)),
        name="conv_branch",
    )(aglu, aglu, w_dw, b_dw, ln_g, ln_b, w_out_bf16, b_out, sga)


def _nt_dot(a, b):
    return lax.dot_general(a, b, (((1,), (1,)), ((), ())), preferred_element_type=F32)


def _tn_dot(a, b):
    return lax.dot_general(a, b, (((0,), (0,)), ((), ())), preferred_element_type=F32)


def _scan_kernel(q_ref, k_ref, v_ref, lf_ref, og_ref, gon_ref, o_ref, st_ref):
    @pl.when(pl.program_id(1) == 0)
    def _():
        st_ref[...] = jnp.zeros_like(st_ref)

    c = CHUNK
    row = lax.broadcasted_iota(jnp.int32, (c, c), 0)
    col = lax.broadcasted_iota(jnp.int32, (c, c), 1)
    causal = col <= row
    rblk = row // SUB_BLOCK
    cblk = col // SUB_BLOCK
    diag_mask = jnp.logical_and(causal, rblk == cblk)
    bcum = jnp.dot(causal.astype(F32), lf_ref[...], preferred_element_type=F32,
                   precision=lax.Precision.HIGHEST)

    def bcast_rows(rows):
        return jnp.concatenate([jnp.broadcast_to(r, (SUB_BLOCK, HEAD_DIM)) for r in rows], axis=0)

    for h in range(N_HEADS):
        hs = slice(h * HEAD_DIM, (h + 1) * HEAD_DIM)
        b = bcum[:, hs]
        q = q_ref[:, hs].astype(F32)
        k = k_ref[:, hs].astype(F32)
        v = v_ref[:, hs]
        ends = [b[i * SUB_BLOCK + SUB_BLOCK - 1:i * SUB_BLOCK + SUB_BLOCK, :] for i in range(N_SUB)]
        mids = [b[i * SUB_BLOCK + SUB_BLOCK // 2 - 1:i * SUB_BLOCK + SUB_BLOCK // 2, :] for i in range(N_SUB)]
        mid_full = bcast_rows(mids)
        end_full = bcast_rows(ends)
        b_last = ends[-1]
        qm = (q * jnp.exp(b - mid_full)).astype(BF16)
        km = (k * jnp.exp(mid_full - b)).astype(BF16)
        scores = jnp.where(diag_mask, _nt_dot(qm, km), 0.0)
        ke = k * jnp.exp(end_full - b)
        q_parts, k_parts = [], []
        for j in range(N_SUB - 1):
            arg = jnp.where(rblk > j, b - ends[j], NEG_BIG)
            q_parts.append((q * jnp.exp(arg)).astype(BF16))
            k_parts.append(jnp.where(rblk == j, ke, 0.0).astype(BF16))
        scores = scores + _nt_dot(jnp.concatenate(q_parts, axis=1), jnp.concatenate(k_parts, axis=1))
        st = st_ref[h]
        q_in = (q * jnp.exp(b)).astype(BF16)
        o = _nt_dot(q_in, st.astype(BF16)) + jnp.dot(scores.astype(BF16), v, preferred_element_type=F32)
        k_out = (k * jnp.exp(b_last - b)).astype(BF16)
        st_ref[h] = st * jnp.exp(b_last) + _tn_dot(v, k_out)
        ms = jnp.mean(o * o, axis=-1, keepdims=True)
        on = o * lax.rsqrt(ms + RMS_EPS) * gon_ref[:, hs]
        o_ref[:, hs] = (on * og_ref[:, hs].astype(F32)).astype(o_ref.dtype)


def _hgrn2_scan(q, k, v, logf, og, g_onorm, n_batch):
    tp, d = q.shape
    n_chunks = tp // n_batch // CHUNK
    blk = pl.BlockSpec((CHUNK, d), lambda b, c: (b * n_chunks + c, 0))
    return pl.pallas_call(
        _scan_kernel,
        grid=(n_batch, n_chunks),
        in_specs=[blk, blk, blk, blk, blk, pl.BlockSpec((1, d), lambda b, c: (0, 0))],
        out_specs=blk,
        out_shape=jax.ShapeDtypeStruct((tp, d), BF16),
        scratch_shapes=[pltpu.VMEM((N_HEADS, HEAD_DIM, HEAD_DIM), F32)],
        compiler_params=_cparams(("arbitrary", "arbitrary")),
        name="hgrn2_scan",
    )(q, k, v, logf, og, g_onorm)


def _merge_route_kernel(h_ref, on_ref, ap_ref, sgb_ref, wrnn_ref, wo_ref, gffn_ref, wr_ref, br_ref,
                        h1_ref, u2_ref, info_ref, cnt_ref, tri_ref, carry_ref, colcnt_ref):
    tm = h_ref.shape[0]

    @pl.when(pl.program_id(0) == 0)
    def _():
        r_i = lax.broadcasted_iota(jnp.int32, (tm, tm), 0)
        c_i = lax.broadcasted_iota(jnp.int32, (tm, tm), 1)
        tri_ref[...] = (c_i < r_i).astype(BF16)
        carry_ref[...] = jnp.zeros_like(carry_ref)
        colcnt_ref[...] = jnp.zeros_like(colcnt_ref)

    r = jnp.dot(on_ref[...], wrnn_ref[...], preferred_element_type=F32)
    y = ap_ref[...].astype(F32) + sgb_ref[...].astype(F32) * r
    h1 = h_ref[...] + jnp.dot(y.astype(BF16), wo_ref[...], preferred_element_type=F32)
    h1_ref[...] = h1
    ms = jnp.mean(h1 * h1, axis=-1, keepdims=True)
    u2 = h1 * lax.rsqrt(ms + RMS_EPS) * gffn_ref[...]
    _store_packed(u2_ref, u2)
    logits = jnp.dot(u2, wr_ref[...], preferred_element_type=F32,
                     precision=lax.Precision.HIGHEST) + br_ref[...]
    lane = lax.broadcasted_iota(jnp.int32, (tm, LANES), 1)
    cur = jnp.where(lane < N_EXPERTS, logits, NEG_BIG)
    vals, idxs, sels = [], [], []
    for _ in range(TOP_K):
        m = jnp.max(cur, axis=-1, keepdims=True)
        idx = jnp.min(jnp.where(cur == m, lane, LANES), axis=-1, keepdims=True)
        sel = lane == idx
        cur = jnp.where(sel, 2.0 * NEG_BIG, cur)
        vals.append(m)
        idxs.append(idx)
        sels.append(sel)
    exps = [jnp.exp(vk - vals[0]) for vk in vals]
    den = exps[0] + exps[1] + exps[2] + exps[3]
    onehot = jnp.logical_or(jnp.logical_or(sels[0], sels[1]), jnp.logical_or(sels[2], sels[3]))
    onehot_bf = onehot.astype(BF16)
    rank_all = jnp.dot(tri_ref[...], onehot_bf, preferred_element_type=F32) + carry_ref[...]
    carry_ref[...] += jnp.sum(onehot.astype(F32), axis=0, keepdims=True)
    colcnt_ref[...] += _tn_dot(onehot_bf, jnp.ones((tm, LANES), BF16))
    info = jnp.zeros((tm, LANES), F32)
    for kk in range(TOP_K):
        rank_k = jnp.sum(jnp.where(sels[kk], rank_all, 0.0), axis=-1, keepdims=True)
        info = jnp.where(lane == kk, exps[kk] / den, info)
        info = jnp.where(lane == TOP_K + kk, idxs[kk].astype(F32), info)
        info = jnp.where(lane == 2 * TOP_K + kk, rank_k, info)
    info_ref[...] = info
    cnt_ref[...] = colcnt_ref[...]


def _merge_route(hp, on, ap, sgb, w_rnn_bf16, w_o_bf16, g_ffn, w_router_pad, b_router_pad):
    tp, d = hp.shape
    tm = ROW_TILE
    row = pl.BlockSpec((tm, d), lambda i: (i, 0))
    vec = _const_spec((1, d))
    return pl.pallas_call(
        _merge_route_kernel,
        grid=(tp // tm,),
        in_specs=[row, row, row, row, _const_spec((d, d)), _const_spec((d, d)), vec,
                  _const_spec((d, LANES)), _const_spec((1, LANES))],
        out_specs=[row, pl.BlockSpec((tm * PACK_ROWS, LANES), lambda i: (i, 0)),
                   pl.BlockSpec((tm, LANES), lambda i: (i, 0)), _const_spec((LANES, LANES))],
        out_shape=[jax.ShapeDtypeStruct((tp, d), F32), jax.ShapeDtypeStruct((tp * PACK_ROWS, LANES), jnp.uint32),
                   jax.ShapeDtypeStruct((tp, LANES), F32), jax.ShapeDtypeStruct((LANES, LANES), F32)],
        scratch_shapes=[pltpu.VMEM((tm, tm), BF16), pltpu.VMEM((1, LANES), F32),
                        pltpu.VMEM((LANES, LANES), F32)],
        compiler_params=_cparams(("arbitrary",)),
        name="merge_route",
    )(hp, on, ap, sgb, w_rnn_bf16, w_o_bf16, g_ffn, w_router_pad, b_router_pad)


def _slots_kernel(info_ref, cnt_ref, dest_ref, te_ref):
    tm = info_ref.shape[0]
    n_tile_lanes = te_ref.shape[1]
    r_i = lax.broadcasted_iota(jnp.int32, (LANES, LANES), 0)
    c_i = lax.broadcasted_iota(jnp.int32, (LANES, LANES), 1)
    cnt = cnt_ref[...]
    tiles = jnp.floor((cnt + (EXPERT_TILE - 1)) / EXPERT_TILE)
    pend_col = jnp.dot((c_i <= r_i).astype(F32), tiles, preferred_element_type=F32,
                       precision=lax.Precision.HIGHEST)
    pstart_row = _tn_dot_f32(tiles, (r_i < c_i).astype(F32))
    info = info_ref[...]
    lane = lax.broadcasted_iota(jnp.int32, (tm, LANES), 1)
    dest = jnp.zeros((tm, LANES), F32)
    for kk in range(TOP_K):
        idx_k = info[:, TOP_K + kk:TOP_K + kk + 1].astype(jnp.int32)
        start_k = jnp.sum(jnp.where(lane == idx_k, pstart_row[0:1, :], 0.0), axis=-1, keepdims=True)
        slot_k = start_k * EXPERT_TILE + info[:, 2 * TOP_K + kk:2 * TOP_K + kk + 1]
        dest = jnp.where(lane == kk, slot_k, dest)
    dest_ref[...] = dest.astype(jnp.int32)
    tile_id = lax.broadcasted_iota(jnp.int32, (LANES, n_tile_lanes), 1).astype(F32)
    exp_id = lax.broadcasted_iota(jnp.int32, (LANES, n_tile_lanes), 0)
    pend_wide = jnp.concatenate([pend_col] * (n_tile_lanes // LANES), axis=1)
    below = jnp.logical_and(pend_wide <= tile_id, exp_id < N_EXPERTS)
    te = jnp.minimum(jnp.sum(below.astype(F32), axis=0, keepdims=True), N_EXPERTS - 1.0)
    n_active = pend_col[N_EXPERTS - 1:N_EXPERTS, 0:1]
    lane_t = lax.broadcasted_iota(jnp.int32, (8, n_tile_lanes), 1)
    sub_t = lax.broadcasted_iota(jnp.int32, (8, n_tile_lanes), 0)
    pend_row = _tn_dot_f32(tiles, (r_i <= c_i).astype(F32))[0:1, :]
    pend_wide_row = jnp.concatenate([pend_row] + [jnp.zeros_like(pend_row)] * (n_tile_lanes // LANES - 1), axis=1)
    out = jnp.where(sub_t == 0, jnp.broadcast_to(te, (8, n_tile_lanes)),
                    jnp.where(sub_t == 1, jnp.broadcast_to(n_active, (8, n_tile_lanes)),
                              jnp.broadcast_to(pend_wide_row, (8, n_tile_lanes))))
    del lane_t
    te_ref[...] = out.astype(jnp.int32)


def _tn_dot_f32(a, b):
    return lax.dot_general(a, b, (((0,), (0,)), ((), ())), preferred_element_type=F32,
                           precision=lax.Precision.HIGHEST)


def _slots(info, cnt, n_tiles_max):
    tp = info.shape[0]
    tm = ROW_TILE
    n_tile_lanes = -(-n_tiles_max // LANES) * LANES
    return pl.pallas_call(
        _slots_kernel,
        grid=(tp // tm,),
        in_specs=[pl.BlockSpec((tm, LANES), lambda i: (i, 0)), _const_spec((LANES, LANES))],
        out_specs=[pl.BlockSpec((tm, LANES), lambda i: (i, 0)), _const_spec((8, n_tile_lanes))],
        out_shape=[jax.ShapeDtypeStruct((tp, LANES), jnp.int32),
                   jax.ShapeDtypeStruct((8, n_tile_lanes), jnp.int32)],
        compiler_params=_cparams(("arbitrary",)),
        name="slots",
    )(info, cnt)


def _fetch_dest(dest_hbm, dest_smem, dsem, step, slot):
    return pltpu.make_async_copy(dest_hbm.at[step], dest_smem.at[slot], dsem.at[slot])


def _slab(ref, token):
    return ref.at[pl.ds(token * PACK_ROWS, PACK_ROWS), :]


def _dispatch_kernel(pend_ref, dest_hbm, u2_hbm, xs_ref, dest_smem, zero_ref, dsem, sem, zsem):
    i = pl.program_id(0)
    n = pl.num_programs(0)
    slot = i % 2
    tile_rows = EXPERT_TILE * PACK_ROWS

    def last_tile_zero(e):
        first = pend_ref[e - 1] if e else 0
        dst = xs_ref.at[pl.ds((pend_ref[e] - 1) * tile_rows, tile_rows), :]
        return pend_ref[e] > first, pltpu.make_async_copy(zero_ref, dst, zsem)

    @pl.when(i == 0)
    def _():
        _fetch_dest(dest_hbm, dest_smem, dsem, 0, 0).start()
        zero_ref[...] = jnp.zeros_like(zero_ref)
        for e in range(N_EXPERTS):
            has_tiles, copy = last_tile_zero(e)
            pl.when(has_tiles)(copy.start)
        n_tiles = xs_ref.shape[0] // tile_rows

        def spare_tile(t):
            return pltpu.make_async_copy(zero_ref, xs_ref.at[pl.ds(t * tile_rows, tile_rows), :], zsem)

        lax.fori_loop(pend_ref[N_EXPERTS - 1], n_tiles, lambda t, c: (spare_tile(t).start(), c)[1], 0)
        lax.fori_loop(pend_ref[N_EXPERTS - 1], n_tiles, lambda t, c: (spare_tile(t).wait(), c)[1], 0)
        for e in range(N_EXPERTS):
            has_tiles, copy = last_tile_zero(e)
            pl.when(has_tiles)(copy.wait)

    _fetch_dest(dest_hbm, dest_smem, dsem, i, slot).wait()

    @pl.when(i + 1 < n)
    def _():
        _fetch_dest(dest_hbm, dest_smem, dsem, i + 1, 1 - slot).start()

    def issue(t, carry):
        src = _slab(u2_hbm, i * ROUTE_TILE + t)
        for kk in range(TOP_K):
            pltpu.make_async_copy(src, _slab(xs_ref, dest_smem[slot, t * TOP_K + kk]), sem).start()
        return carry

    lax.fori_loop(0, ROUTE_TILE, issue, 0)

    step_rows = ROUTE_TILE * TOP_K * PACK_ROWS
    whole_step = pltpu.make_async_copy(xs_ref.at[pl.ds(0, step_rows), :], xs_ref.at[pl.ds(0, step_rows), :], sem)
    pl.when(i > 0)(whole_step.wait)
    pl.when(i == n - 1)(whole_step.wait)


def _dispatch(pend, dest2, u2p, n_slots):
    n_steps = dest2.shape[0]
    grid_spec = pltpu.PrefetchScalarGridSpec(
        num_scalar_prefetch=1,
        grid=(n_steps,),
        in_specs=[pl.BlockSpec(memory_space=pl.ANY), pl.BlockSpec(memory_space=pl.ANY)],
        out_specs=pl.BlockSpec(memory_space=pl.ANY),
        scratch_shapes=[pltpu.SMEM((2, ROUTE_TILE * TOP_K), jnp.int32),
                        pltpu.VMEM((EXPERT_TILE * PACK_ROWS, LANES), jnp.uint32),
                        pltpu.SemaphoreType.DMA((2,)), pltpu.SemaphoreType.DMA, pltpu.SemaphoreType.DMA],
    )
    return pl.pallas_call(
        _dispatch_kernel,
        grid_spec=grid_spec,
        out_shape=jax.ShapeDtypeStruct((n_slots * PACK_ROWS, LANES), jnp.uint32),
        compiler_params=_cparams(("arbitrary",)),
        name="dispatch",
    )(pend, dest2, u2p)


def _expert_kernel(te_ref, nact_ref, x_ref, wgu_ref, bgu_ref, wdn_ref, bdn_ref, y_ref):
    del te_ref

    @pl.when(pl.program_id(0) < nact_ref[0])
    def _():
        x = _load_packed(x_ref, EXPERT_TILE).astype(BF16)
        gu = jnp.dot(x, wgu_ref[0], preferred_element_type=F32) + bgu_ref[0]
        gate = jnp.minimum(gu[:, :D_FF], SWIGLU_LIMIT)
        up = jnp.clip(gu[:, D_FF:], -SWIGLU_LIMIT, SWIGLU_LIMIT)
        hdn = (up + 1.0) * (gate * _sigmoid(SWIGLU_ALPHA * gate))
        y = jnp.dot(hdn.astype(BF16), wdn_ref[0], preferred_element_type=F32) + bdn_ref[0]
        _store_packed(y_ref, y)

    @pl.when(pl.program_id(0) >= nact_ref[0])
    def _():
        y_ref[...] = jnp.zeros_like(y_ref)


def _expert_mlp(te, nact, xs, w_gu_bf16, b_gu, w_dn_bf16, b_dn):
    d = D_MODEL
    tile_rows = EXPERT_TILE * PACK_ROWS
    n_tiles = xs.shape[0] // tile_rows

    def tile_map(i, te_ref, nact_ref):
        return (jnp.minimum(i, nact_ref[0] - 1), 0)

    def exp_map(i, te_ref, nact_ref):
        return (te_ref[jnp.minimum(i, nact_ref[0] - 1)], 0, 0)

    grid_spec = pltpu.PrefetchScalarGridSpec(
        num_scalar_prefetch=2,
        grid=(n_tiles,),
        in_specs=[pl.BlockSpec((tile_rows, LANES), tile_map),
                  pl.BlockSpec((1, d, 2 * D_FF), exp_map),
                  pl.BlockSpec((1, 1, 2 * D_FF), exp_map),
                  pl.BlockSpec((1, D_FF, d), exp_map),
                  pl.BlockSpec((1, 1, d), exp_map)],
        out_specs=pl.BlockSpec((tile_rows, LANES), lambda i, te_ref, nact_ref: (i, 0)),
    )
    return pl.pallas_call(
        _expert_kernel,
        grid_spec=grid_spec,
        out_shape=jax.ShapeDtypeStruct(xs.shape, jnp.uint32),
        compiler_params=_cparams(("arbitrary",)),
        name="expert_mlp",
    )(te, nact, xs, w_gu_bf16, b_gu, w_dn_bf16, b_dn)


def _combine_kernel(dest_hbm, h1_ref, info_ref, gfin_ref, ys_hbm, out_ref, dest_smem, ybuf, dsem, sem,
                    *, chunks_per_batch):
    j = pl.program_id(0)
    n = pl.num_programs(0)
    slot = j % 2

    def fetch(step, s):
        chunk = (step // chunks_per_batch) * (chunks_per_batch + 1) + step % chunks_per_batch + 1
        return pltpu.make_async_copy(dest_hbm.at[chunk], dest_smem.at[s], dsem.at[s])

    def issue(s):
        def body(t, carry):
            for kk in range(TOP_K):
                src = _slab(ys_hbm, dest_smem[s, t * TOP_K + kk])
                pltpu.make_async_copy(src, _slab(ybuf.at[s, kk], t), sem.at[s]).start()
            return carry

        lax.fori_loop(0, ROUTE_TILE, body, 0)

    @pl.when(j == 0)
    def _():
        first = fetch(0, 0)
        first.start()
        first.wait()
        issue(0)

    @pl.when(j + 1 < n)
    def _():
        fetch(j + 1, 1 - slot).start()

    for kk in range(TOP_K):
        pltpu.make_async_copy(ys_hbm.at[pl.ds(0, ROUTE_TILE * PACK_ROWS), :], ybuf.at[slot, kk], sem.at[slot]).wait()
    info = info_ref[...]
    h2 = h1_ref[...]
    for kk in range(TOP_K):
        h2 = h2 + info[:, kk:kk + 1] * _load_packed(ybuf, ROUTE_TILE, lead=(slot, kk))
    ms = jnp.mean(h2 * h2, axis=-1, keepdims=True)
    out_ref[...] = h2 * lax.rsqrt(ms + RMS_EPS) * gfin_ref[...]

    @pl.when(j + 1 < n)
    def _():
        fetch(j + 1, 1 - slot).wait()
        issue(1 - slot)


def _combine(dest2, h1, info, g_final, ys, n_batch, seq):
    d = h1.shape[1]
    cpb = seq // ROUTE_TILE

    def padded_chunk(j):
        return ((j // cpb) * (cpb + 1) + j % cpb + 1, 0)

    return pl.pallas_call(
        functools.partial(_combine_kernel, chunks_per_batch=cpb),
        grid=(n_batch * cpb,),
        in_specs=[pl.BlockSpec(memory_space=pl.ANY),
                  pl.BlockSpec((ROUTE_TILE, d), padded_chunk),
                  pl.BlockSpec((ROUTE_TILE, LANES), padded_chunk),
                  pl.BlockSpec((1, d), lambda j: (0, 0)),
                  pl.BlockSpec(memory_space=pl.ANY)],
        out_specs=pl.BlockSpec((ROUTE_TILE, d), lambda j: (j, 0)),
        out_shape=jax.ShapeDtypeStruct((n_batch * seq, d), F32),
        scratch_shapes=[pltpu.SMEM((2, ROUTE_TILE * TOP_K), jnp.int32),
                        pltpu.VMEM((2, TOP_K, ROUTE_TILE * PACK_ROWS, LANES), jnp.uint32),
                        pltpu.SemaphoreType.DMA((2,)), pltpu.SemaphoreType.DMA((2,))],
        compiler_params=_cparams(("arbitrary",)),
        name="combine",
    )(dest2, h1, info, g_final, ys)


def kernel(x, meta_tokens, lb_logits, g_mix, w_in, w_dw, b_dw, ln_g, ln_b, w_conv_out, b_conv_out,
           g_onorm, w_rnn_out, w_o, g_ffn, w_router, b_router, w_gate_up, b_gate_up, w_down, b_down,
           g_final):
    n_batch, seq, d = x.shape
    assert d == D_MODEL and w_in.shape[0] == 1, "single-layer block with D_MODEL features"
    assert CHUNK == ROUTE_TILE and seq % CHUNK == 0
    seq_pad = CHUNK + seq
    tp = n_batch * seq_pad
    assert tp % ROW_TILE == 0

    meta = jnp.broadcast_to(meta_tokens.astype(x.dtype)[None], (n_batch, N_META, d))
    hp = jnp.concatenate([jnp.zeros((n_batch, CHUNK_PAD, d), x.dtype), meta, x], axis=1).reshape(tp, d)
    vec = lambda a: a.reshape(1, -1).astype(F32)

    aglu, q, k, logf, v, og, sga, sgb = _in_proj(hp, vec(g_mix[0]), lb_logits.astype(F32), w_in[0].astype(BF16))
    ap = _conv_branch(aglu, w_dw[0].astype(F32), vec(b_dw[0]), vec(ln_g[0]), vec(ln_b[0]),
                      w_conv_out[0].astype(BF16), vec(b_conv_out[0]), sga)
    on = _hgrn2_scan(q, k, v, logf, og, vec(g_onorm[0]), n_batch)
    w_router_pad = jnp.pad(w_router[0].astype(F32), ((0, 0), (0, LANES - N_EXPERTS)))
    b_router_pad = jnp.pad(vec(b_router[0]), ((0, 0), (0, LANES - N_EXPERTS)))
    h1, u2p, info, cnt = _merge_route(hp, on, ap, sgb, w_rnn_out[0].astype(BF16), w_o[0].astype(BF16),
                                      vec(g_ffn[0]), w_router_pad, b_router_pad)

    n_tiles_max = -(-(tp * TOP_K + N_EXPERTS * (EXPERT_TILE - 1)) // EXPERT_TILE)
    dest, te = _slots(info, cnt, n_tiles_max)
    dest2 = dest[:, :TOP_K].reshape(tp // ROUTE_TILE, ROUTE_TILE * TOP_K)
    xs = _dispatch(te[2, :N_EXPERTS], dest2, u2p, n_tiles_max * EXPERT_TILE)
    ys = _expert_mlp(te[0, :n_tiles_max], te[1, :1], xs, w_gate_up[0].astype(BF16),
                     b_gate_up[0].reshape(N_EXPERTS, 1, -1).astype(F32), w_down[0].astype(BF16),
                     b_down[0].reshape(N_EXPERTS, 1, -1).astype(F32))
    out = _combine(dest2, h1, info, vec(g_final), ys, n_batch, seq)
    return out.reshape(n_batch, seq, d)
```

```python
import functools

import jax
import jax.numpy as jnp
from jax import lax
from jax.experimental import pallas as pl
from jax.experimental.pallas import tpu as pltpu

F32 = jnp.float32
BF16 = jnp.bfloat16

D_MODEL = 1024
N_META = 16
CHUNK = 128
CHUNK_PAD = CHUNK - N_META
CONV_WIDTH = 31
HEAD_DIM = 128
N_HEADS = D_MODEL // HEAD_DIM
N_EXPERTS = 32
TOP_K = 4
D_FF = D_MODEL
SWIGLU_LIMIT = 7.0
SWIGLU_ALPHA = 1.702
RMS_EPS = 1e-6
LN_EPS = 1e-5
N_IN_GROUPS = 8

LANES = 128
SUBLANES = 8
SUB_BLOCK = 32
N_SUB = CHUNK // SUB_BLOCK
HALO = 32
ROW_TILE = 640
IN_TILE = 320
CONV_ROWS = 32
EXPERT_TILE = 256
ROUTE_TILE = 128
STAGE_SLOTS = 3
NEG_BIG = -1e30
VMEM_LIMIT = 56 * 1024 * 1024


def _sigmoid(x):
    return 1.0 / (1.0 + jnp.exp(-x))


def _cparams(sem):
    return pltpu.CompilerParams(dimension_semantics=sem, vmem_limit_bytes=VMEM_LIMIT)


def _const_spec(shape):
    nd = len(shape)
    return pl.BlockSpec(shape, lambda *_: (0,) * nd)


PACK_ROWS = D_MODEL // 2 // 128
HIGH_HALF = 0xFFFF0000


def _store_packed(ref, x):
    rows = x.shape[0]
    half = D_MODEL // 2
    lo = lax.bitcast_convert_type(x[:, :half].astype(BF16).astype(F32), jnp.uint32)
    hi = lax.bitcast_convert_type(x[:, half:].astype(BF16).astype(F32), jnp.uint32)
    words = (lo >> 16) | (hi & jnp.uint32(HIGH_HALF))
    for j in range(PACK_ROWS):
        ref[pl.ds(j, rows, stride=PACK_ROWS), :] = words[:, j * LANES:(j + 1) * LANES]


def _load_packed(ref, rows, lead=()):
    lo, hi = [], []
    for j in range(PACK_ROWS):
        w = ref[lead + (pl.ds(j, rows, stride=PACK_ROWS), slice(None))]
        lo.append(lax.bitcast_convert_type(w << 16, F32))
        hi.append(lax.bitcast_convert_type(w & jnp.uint32(HIGH_HALF), F32))
    return jnp.concatenate(lo + hi, axis=1)


def _in_proj_kernel(h_ref, g_ref, lbl_ref, w_ref, aglu_ref, q_ref, k_ref, logf_ref, v_ref,
                    og_ref, sga_ref, sgb_ref):
    d = D_MODEL
    h = h_ref[...]
    ms = jnp.mean(h * h, axis=-1, keepdims=True)
    u = (h * lax.rsqrt(ms + RMS_EPS) * g_ref[...]).astype(BF16)

    def proj(j):
        return jnp.dot(u, w_ref[:, j * d:(j + 1) * d], preferred_element_type=F32)

    aglu_ref[...] = (proj(0) * _sigmoid(proj(1))).astype(aglu_ref.dtype)
    zq = proj(2)
    q_ref[...] = (zq * _sigmoid(zq)).astype(q_ref.dtype)
    lbl = lbl_ref[...]
    e = jnp.exp(lbl - jnp.max(lbl, axis=0, keepdims=True))
    lb = e[0:1, :] / jnp.sum(e, axis=0, keepdims=True)
    s = _sigmoid(proj(3))
    logf_ref[...] = jnp.log(lb + (1.0 - lb) * s)
    k_ref[...] = ((1.0 - lb) * (1.0 - s)).astype(k_ref.dtype)
    v_ref[...] = proj(4).astype(v_ref.dtype)
    zg = proj(5)
    og_ref[...] = (zg * _sigmoid(zg)).astype(og_ref.dtype)
    sga_ref[...] = _sigmoid(proj(6)).astype(sga_ref.dtype)
    sgb_ref[...] = _sigmoid(proj(7)).astype(sgb_ref.dtype)


def _in_proj(hp, g_mix, lb_logits, w_in_bf16):
    tp, d = hp.shape
    tm = IN_TILE
    row = pl.BlockSpec((tm, d), lambda i: (i, 0))
    out_dtypes = [BF16, BF16, BF16, F32, BF16, BF16, BF16, BF16]
    return pl.pallas_call(
        _in_proj_kernel,
        grid=(tp // tm,),
        in_specs=[row, _const_spec((1, d)), _const_spec(lb_logits.shape),
                  pl.BlockSpec(w_in_bf16.shape, lambda i: (0, 0), pipeline_mode=pl.Buffered(1))],
        out_specs=[row] * 8,
        out_shape=[jax.ShapeDtypeStruct((tp, d), dt) for dt in out_dtypes],
        compiler_params=_cparams(("parallel",)),
        name="in_proj",
    )(hp, g_mix, lb_logits, w_in_bf16)


def _conv_kernel(halo_ref, cur_ref, wdw_ref, bdw_ref, lng_ref, lnb_ref, wout_ref, bout_ref, sga_ref,
                 out_ref, win_ref, acc_ref, shift_ref):
    tm = cur_ref.shape[0]
    win_ref[0:HALO, :] = halo_ref[...].astype(F32)
    win_ref[HALO:, :] = cur_ref[...].astype(F32)
    first_tap = HALO - (CONV_WIDTH - 1)
    for c in range(D_MODEL // LANES):
        lanes = slice(c * LANES, (c + 1) * LANES)
        for s in range(SUBLANES):
            n_rows = tm + HALO - (SUBLANES if s else 0)
            shift_ref[s, 0:n_rows, :] = win_ref[pl.ds(s, n_rows), lanes]
        w_c = wdw_ref[:, lanes]
        bias = jnp.broadcast_to(bdw_ref[:, lanes], (CONV_ROWS, LANES))

        def chunk(r, carry, lanes=lanes, w_c=w_c, bias=bias):
            r0 = pl.multiple_of(r * CONV_ROWS, CONV_ROWS)
            acc = bias
            for j in range(CONV_WIDTH):
                off = first_tap + j
                rows = pl.ds(r0 + off - off % SUBLANES, CONV_ROWS)
                acc = acc + w_c[j:j + 1, :] * shift_ref[off % SUBLANES, rows, :]
            acc_ref[pl.ds(r0, CONV_ROWS), lanes] = acc
            return carry

        lax.fori_loop(0, tm // CONV_ROWS, chunk, 0)
    a = acc_ref[...]
    mu = jnp.mean(a, axis=-1, keepdims=True)
    ac = a - mu
    var = jnp.mean(ac * ac, axis=-1, keepdims=True)
    y = ac * lax.rsqrt(var + LN_EPS) * lng_ref[...] + lnb_ref[...]
    y = y * _sigmoid(y)
    o = jnp.dot(y.astype(BF16), wout_ref[...], preferred_element_type=F32) + bout_ref[...]
    out_ref[...] = (sga_ref[...].astype(F32) * o).astype(out_ref.dtype)


def _conv_branch(aglu, w_dw, b_dw, ln_g, ln_b, w_out_bf16, b_out, sga):
    tp, d = aglu.shape
    tm = ROW_TILE
    per = tm // HALO
    row = pl.BlockSpec((tm, d), lambda i: (i, 0))
    halo = pl.BlockSpec((HALO, d), lambda i: (jnp.maximum(i * per - 1, 0), 0))
    vec = _const_spec((1, d))
    return pl.pallas_call(
        _conv_kernel,
        grid=(tp // tm,),
        in_specs=[halo, row, _const_spec(w_dw.shape), vec, vec, vec, _const_spec((d, d)), vec, row],
        out_specs=row,
        out_shape=jax.ShapeDtypeStruct((tp, d), BF16),
        scratch_shapes=[pltpu.VMEM((tm + HALO, d), F32), pltpu.VMEM((tm, d), F32),
                        pltpu.VMEM((SUBLANES, tm + HALO, LANES), F32)],
        compiler_params=_cparams(("parallel",)),
        name="conv_branch",
    )(aglu, aglu, w_dw, b_dw, ln_g, ln_b, w_out_bf16, b_out, sga)


def _nt_dot(a, b):
    return lax.dot_general(a, b, (((1,), (1,)), ((), ())), preferred_element_type=F32)


def _tn_dot(a, b):
    return lax.dot_general(a, b, (((0,), (0,)), ((), ())), preferred_element_type=F32)


def _scan_kernel(q_ref, k_ref, v_ref, lf_ref, og_ref, gon_ref, o_ref, st_ref):
    @pl.when(pl.program_id(1) == 0)
    def _():
        st_ref[...] = jnp.zeros_like(st_ref)

    c = CHUNK
    row = lax.broadcasted_iota(jnp.int32, (c, c), 0)
    col = lax.broadcasted_iota(jnp.int32, (c, c), 1)
    causal = col <= row
    rblk = row // SUB_BLOCK
    cblk = col // SUB_BLOCK
    diag_mask = jnp.logical_and(causal, rblk == cblk)
    bcum = jnp.dot(causal.astype(F32), lf_ref[...], preferred_element_type=F32,
                   precision=lax.Precision.HIGHEST)

    def bcast_rows(rows):
        return jnp.concatenate([jnp.broadcast_to(r, (SUB_BLOCK, HEAD_DIM)) for r in rows], axis=0)

    for h in range(N_HEADS):
        hs = slice(h * HEAD_DIM, (h + 1) * HEAD_DIM)
        b = bcum[:, hs]
        q = q_ref[:, hs].astype(F32)
        k = k_ref[:, hs].astype(F32)
        v = v_ref[:, hs]
        ends = [b[i * SUB_BLOCK + SUB_BLOCK - 1:i * SUB_BLOCK + SUB_BLOCK, :] for i in range(N_SUB)]
        mids = [b[i * SUB_BLOCK + SUB_BLOCK // 2 - 1:i * SUB_BLOCK + SUB_BLOCK // 2, :] for i in range(N_SUB)]
        mid_full = bcast_rows(mids)
        end_full = bcast_rows(ends)
        b_last = ends[-1]
        qm = (q * jnp.exp(b - mid_full)).astype(BF16)
        km = (k * jnp.exp(mid_full - b)).astype(BF16)
        scores = jnp.where(diag_mask, _nt_dot(qm, km), 0.0)
        ke = k * jnp.exp(end_full - b)
        q_parts, k_parts = [], []
        for j in range(N_SUB - 1):
            arg = jnp.where(rblk > j, b - ends[j], NEG_BIG)
            q_parts.append((q * jnp.exp(arg)).astype(BF16))
            k_parts.append(jnp.where(rblk == j, ke, 0.0).astype(BF16))
        scores = scores + _nt_dot(jnp.concatenate(q_parts, axis=1), jnp.concatenate(k_parts, axis=1))
        st = st_ref[h]
        q_in = (q * jnp.exp(b)).astype(BF16)
        o = _nt_dot(q_in, st.astype(BF16)) + jnp.dot(scores.astype(BF16), v, preferred_element_type=F32)
        k_out = (k * jnp.exp(b_last - b)).astype(BF16)
        st_ref[h] = st * jnp.exp(b_last) + _tn_dot(v, k_out)
        ms = jnp.mean(o * o, axis=-1, keepdims=True)
        on = o * lax.rsqrt(ms + RMS_EPS) * gon_ref[:, hs]
        o_ref[:, hs] = (on * og_ref[:, hs].astype(F32)).astype(o_ref.dtype)


def _hgrn2_scan(q, k, v, logf, og, g_onorm, n_batch):
    tp, d = q.shape
    n_chunks = tp // n_batch // CHUNK
    blk = pl.BlockSpec((CHUNK, d), lambda b, c: (b * n_chunks + c, 0))
    return pl.pallas_call(
        _scan_kernel,
        grid=(n_batch, n_chunks),
        in_specs=[blk, blk, blk, blk, blk, pl.BlockSpec((1, d), lambda b, c: (0, 0))],
        out_specs=blk,
        out_shape=jax.ShapeDtypeStruct((tp, d), BF16),
        scratch_shapes=[pltpu.VMEM((N_HEADS, HEAD_DIM, HEAD_DIM), F32)],
        compiler_params=_cparams(("arbitrary", "arbitrary")),
        name="hgrn2_scan",
    )(q, k, v, logf, og, g_onorm)


def _merge_route_kernel(h_ref, on_ref, ap_ref, sgb_ref, wrnn_ref, wo_ref, gffn_ref, wr_ref, br_ref,
                        h1_ref, u2_ref, info_ref, cnt_ref, tri_ref, carry_ref, colcnt_ref):
    tm = h_ref.shape[0]

    @pl.when(pl.program_id(0) == 0)
    def _():
        r_i = lax.broadcasted_iota(jnp.int32, (tm, tm), 0)
        c_i = lax.broadcasted_iota(jnp.int32, (tm, tm), 1)
        tri_ref[...] = (c_i < r_i).astype(BF16)
        carry_ref[...] = jnp.zeros_like(carry_ref)
        colcnt_ref[...] = jnp.zeros_like(colcnt_ref)

    r = jnp.dot(on_ref[...], wrnn_ref[...], preferred_element_type=F32)
    y = ap_ref[...].astype(F32) + sgb_ref[...].astype(F32) * r
    h1 = h_ref[...] + jnp.dot(y.astype(BF16), wo_ref[...], preferred_element_type=F32)
    h1_ref[...] = h1
    ms = jnp.mean(h1 * h1, axis=-1, keepdims=True)
    u2 = h1 * lax.rsqrt(ms + RMS_EPS) * gffn_ref[...]
    _store_packed(u2_ref, u2)
    logits = jnp.dot(u2, wr_ref[...], preferred_element_type=F32,
                     precision=lax.Precision.HIGHEST) + br_ref[...]
    lane = lax.broadcasted_iota(jnp.int32, (tm, LANES), 1)
    cur = jnp.where(lane < N_EXPERTS, logits, NEG_BIG)
    vals, idxs, sels = [], [], []
    for _ in range(TOP_K):
        m = jnp.max(cur, axis=-1, keepdims=True)
        idx = jnp.min(jnp.where(cur == m, lane, LANES), axis=-1, keepdims=True)
        sel = lane == idx
        cur = jnp.where(sel, 2.0 * NEG_BIG, cur)
        vals.append(m)
        idxs.append(idx)
        sels.append(sel)
    exps = [jnp.exp(vk - vals[0]) for vk in vals]
    den = exps[0] + exps[1] + exps[2] + exps[3]
    onehot = jnp.logical_or(jnp.logical_or(sels[0], sels[1]), jnp.logical_or(sels[2], sels[3]))
    onehot_bf = onehot.astype(BF16)
    rank_all = jnp.dot(tri_ref[...], onehot_bf, preferred_element_type=F32) + carry_ref[...]
    carry_ref[...] += jnp.sum(onehot.astype(F32), axis=0, keepdims=True)
    colcnt_ref[...] += _tn_dot(onehot_bf, jnp.ones((tm, LANES), BF16))
    info = jnp.zeros((tm, LANES), F32)
    for kk in range(TOP_K):
        rank_k = jnp.sum(jnp.where(sels[kk], rank_all, 0.0), axis=-1, keepdims=True)
        info = jnp.where(lane == kk, exps[kk] / den, info)
        info = jnp.where(lane == TOP_K + kk, idxs[kk].astype(F32), info)
        info = jnp.where(lane == 2 * TOP_K + kk, rank_k, info)
    info_ref[...] = info
    cnt_ref[...] = colcnt_ref[...]


def _merge_route(hp, on, ap, sgb, w_rnn_bf16, w_o_bf16, g_ffn, w_router_pad, b_router_pad):
    tp, d = hp.shape
    tm = ROW_TILE
    row = pl.BlockSpec((tm, d), lambda i: (i, 0))
    vec = _const_spec((1, d))
    return pl.pallas_call(
        _merge_route_kernel,
        grid=(tp // tm,),
        in_specs=[row, row, row, row, _const_spec((d, d)), _const_spec((d, d)), vec,
                  _const_spec((d, LANES)), _const_spec((1, LANES))],
        out_specs=[row, pl.BlockSpec((tm * PACK_ROWS, LANES), lambda i: (i, 0)),
                   pl.BlockSpec((tm, LANES), lambda i: (i, 0)), _const_spec((LANES, LANES))],
        out_shape=[jax.ShapeDtypeStruct((tp, d), F32), jax.ShapeDtypeStruct((tp * PACK_ROWS, LANES), jnp.uint32),
                   jax.ShapeDtypeStruct((tp, LANES), F32), jax.ShapeDtypeStruct((LANES, LANES), F32)],
        scratch_shapes=[pltpu.VMEM((tm, tm), BF16), pltpu.VMEM((1, LANES), F32),
                        pltpu.VMEM((LANES, LANES), F32)],
        compiler_params=_cparams(("arbitrary",)),
        name="merge_route",
    )(hp, on, ap, sgb, w_rnn_bf16, w_o_bf16, g_ffn, w_router_pad, b_router_pad)


def _slots_kernel(info_ref, cnt_ref, dest_ref, te_ref):
    tm = info_ref.shape[0]
    n_tile_lanes = te_ref.shape[1]
    r_i = lax.broadcasted_iota(jnp.int32, (LANES, LANES), 0)
    c_i = lax.broadcasted_iota(jnp.int32, (LANES, LANES), 1)
    cnt = cnt_ref[...]
    tiles = jnp.floor((cnt + (EXPERT_TILE - 1)) / EXPERT_TILE)
    pend_col = jnp.dot((c_i <= r_i).astype(F32), tiles, preferred_element_type=F32,
                       precision=lax.Precision.HIGHEST)
    pstart_row = _tn_dot_f32(tiles, (r_i < c_i).astype(F32))
    info = info_ref[...]
    lane = lax.broadcasted_iota(jnp.int32, (tm, LANES), 1)
    dest = jnp.zeros((tm, LANES), F32)
    for kk in range(TOP_K):
        idx_k = info[:, TOP_K + kk:TOP_K + kk + 1].astype(jnp.int32)
        start_k = jnp.sum(jnp.where(lane == idx_k, pstart_row[0:1, :], 0.0), axis=-1, keepdims=True)
        slot_k = start_k * EXPERT_TILE + info[:, 2 * TOP_K + kk:2 * TOP_K + kk + 1]
        dest = jnp.where(lane == kk, slot_k, dest)
    dest_ref[...] = dest.astype(jnp.int32)
    tile_id = lax.broadcasted_iota(jnp.int32, (LANES, n_tile_lanes), 1).astype(F32)
    exp_id = lax.broadcasted_iota(jnp.int32, (LANES, n_tile_lanes), 0)
    pend_wide = jnp.concatenate([pend_col] * (n_tile_lanes // LANES), axis=1)
    below = jnp.logical_and(pend_wide <= tile_id, exp_id < N_EXPERTS)
    te = jnp.minimum(jnp.sum(below.astype(F32), axis=0, keepdims=True), N_EXPERTS - 1.0)
    n_active = pend_col[N_EXPERTS - 1:N_EXPERTS, 0:1]
    lane_t = lax.broadcasted_iota(jnp.int32, (8, n_tile_lanes), 1)
    sub_t = lax.broadcasted_iota(jnp.int32, (8, n_tile_lanes), 0)
    pend_row = _tn_dot_f32(tiles, (r_i <= c_i).astype(F32))[0:1, :]
    pend_wide_row = jnp.concatenate([pend_row] + [jnp.zeros_like(pend_row)] * (n_tile_lanes // LANES - 1), axis=1)
    out = jnp.where(sub_t == 0, jnp.broadcast_to(te, (8, n_tile_lanes)),
                    jnp.where(sub_t == 1, jnp.broadcast_to(n_active, (8, n_tile_lanes)),
                              jnp.broadcast_to(pend_wide_row, (8, n_tile_lanes))))
    del lane_t
    te_ref[...] = out.astype(jnp.int32)


def _tn_dot_f32(a, b):
    return lax.dot_general(a, b, (((0,), (0,)), ((), ())), preferred_element_type=F32,
                           precision=lax.Precision.HIGHEST)


def _slots(info, cnt, n_tiles_max):
    tp = info.shape[0]
    tm = ROW_TILE
    n_tile_lanes = -(-n_tiles_max // LANES) * LANES
    return pl.pallas_call(
        _slots_kernel,
        grid=(tp // tm,),
        in_specs=[pl.BlockSpec((tm, LANES), lambda i: (i, 0)), _const_spec((LANES, LANES))],
        out_specs=[pl.BlockSpec((tm, LANES), lambda i: (i, 0)), _const_spec((8, n_tile_lanes))],
        out_shape=[jax.ShapeDtypeStruct((tp, LANES), jnp.int32),
                   jax.ShapeDtypeStruct((8, n_tile_lanes), jnp.int32)],
        compiler_params=_cparams(("arbitrary",)),
        name="slots",
    )(info, cnt)


def _fetch_dest(dest_hbm, dest_smem, dsem, step, slot):
    return pltpu.make_async_copy(dest_hbm.at[step], dest_smem.at[slot], dsem.at[slot])


def _slab(ref, token):
    return ref.at[pl.ds(token * PACK_ROWS, PACK_ROWS), :]


def _dispatch_kernel(pend_ref, dest_hbm, u2_hbm, xs_ref, dest_smem, zero_ref, rows_ref, dsem, sem, zsem, rsem):
    i = pl.program_id(0)
    n = pl.num_programs(0)
    slot = i % 2
    tile_rows = EXPERT_TILE * PACK_ROWS

    def last_tile_zero(e):
        first = pend_ref[e - 1] if e else 0
        dst = xs_ref.at[pl.ds((pend_ref[e] - 1) * tile_rows, tile_rows), :]
        return pend_ref[e] > first, pltpu.make_async_copy(zero_ref, dst, zsem)

    def stage(step):
        s = step % STAGE_SLOTS
        src = u2_hbm.at[pl.ds(step * ROUTE_TILE * PACK_ROWS, ROUTE_TILE * PACK_ROWS), :]
        return pltpu.make_async_copy(src, rows_ref.at[s], rsem.at[s])

    @pl.when(i == 0)
    def _():
        _fetch_dest(dest_hbm, dest_smem, dsem, 0, 0).start()
        stage(0).start()
        zero_ref[...] = jnp.zeros_like(zero_ref)
        for e in range(N_EXPERTS):
            has_tiles, copy = last_tile_zero(e)
            pl.when(has_tiles)(copy.start)
        n_tiles = xs_ref.shape[0] // tile_rows

        def spare_tile(t):
            return pltpu.make_async_copy(zero_ref, xs_ref.at[pl.ds(t * tile_rows, tile_rows), :], zsem)

        lax.fori_loop(pend_ref[N_EXPERTS - 1], n_tiles, lambda t, c: (spare_tile(t).start(), c)[1], 0)
        lax.fori_loop(pend_ref[N_EXPERTS - 1], n_tiles, lambda t, c: (spare_tile(t).wait(), c)[1], 0)
        for e in range(N_EXPERTS):
            has_tiles, copy = last_tile_zero(e)
            pl.when(has_tiles)(copy.wait)

    _fetch_dest(dest_hbm, dest_smem, dsem, i, slot).wait()
    stage(i).wait()

    @pl.when(i + 1 < n)
    def _():
        _fetch_dest(dest_hbm, dest_smem, dsem, i + 1, 1 - slot).start()
        stage(i + 1).start()

    rows = rows_ref.at[i % STAGE_SLOTS]

    def issue(t, carry):
        for kk in range(TOP_K):
            dst = _slab(xs_ref, dest_smem[slot, t * TOP_K + kk])
            pltpu.make_async_copy(_slab(rows, t), dst, sem.at[i % STAGE_SLOTS]).start()
        return carry

    lax.fori_loop(0, ROUTE_TILE, issue, 0)

    def drain(step):
        s = step % STAGE_SLOTS
        for _ in range(TOP_K):
            pltpu.make_async_copy(rows_ref.at[s], xs_ref.at[pl.ds(0, ROUTE_TILE * PACK_ROWS), :], sem.at[s]).wait()

    pl.when(i > 0)(lambda: drain(i - 1))
    pl.when(i == n - 1)(lambda: drain(i))


def _dispatch(pend, dest2, u2p, n_slots):
    n_steps = dest2.shape[0]
    grid_spec = pltpu.PrefetchScalarGridSpec(
        num_scalar_prefetch=1,
        grid=(n_steps,),
        in_specs=[pl.BlockSpec(memory_space=pl.ANY), pl.BlockSpec(memory_space=pl.ANY)],
        out_specs=pl.BlockSpec(memory_space=pl.ANY),
        scratch_shapes=[pltpu.SMEM((2, ROUTE_TILE * TOP_K), jnp.int32),
                        pltpu.VMEM((EXPERT_TILE * PACK_ROWS, LANES), jnp.uint32),
                        pltpu.VMEM((STAGE_SLOTS, ROUTE_TILE * PACK_ROWS, LANES), jnp.uint32),
                        pltpu.SemaphoreType.DMA((2,)), pltpu.SemaphoreType.DMA((STAGE_SLOTS,)),
                        pltpu.SemaphoreType.DMA, pltpu.SemaphoreType.DMA((STAGE_SLOTS,))],
    )
    return pl.pallas_call(
        _dispatch_kernel,
        grid_spec=grid_spec,
        out_shape=jax.ShapeDtypeStruct((n_slots * PACK_ROWS, LANES), jnp.uint32),
        compiler_params=_cparams(("arbitrary",)),
        name="dispatch",
    )(pend, dest2, u2p)


def _expert_kernel(te_ref, nact_ref, x_ref, wgu_ref, bgu_ref, wdn_ref, bdn_ref, y_ref):
    del te_ref

    @pl.when(pl.program_id(0) < nact_ref[0])
    def _():
        x = _load_packed(x_ref, EXPERT_TILE).astype(BF16)
        gu = jnp.dot(x, wgu_ref[0].astype(BF16), preferred_element_type=F32) + bgu_ref[0]
        gate = jnp.minimum(gu[:, :D_FF], SWIGLU_LIMIT)
        up = jnp.clip(gu[:, D_FF:], -SWIGLU_LIMIT, SWIGLU_LIMIT)
        hdn = (up + 1.0) * (gate * _sigmoid(SWIGLU_ALPHA * gate))
        y = jnp.dot(hdn.astype(BF16), wdn_ref[0].astype(BF16), preferred_element_type=F32) + bdn_ref[0]
        _store_packed(y_ref, y)

    @pl.when(pl.program_id(0) >= nact_ref[0])
    def _():
        y_ref[...] = jnp.zeros_like(y_ref)


def _expert_mlp(te, nact, xs, w_gu, b_gu, w_dn, b_dn):
    d = D_MODEL
    tile_rows = EXPERT_TILE * PACK_ROWS
    n_tiles = xs.shape[0] // tile_rows

    def tile_map(i, te_ref, nact_ref):
        return (jnp.minimum(i, nact_ref[0] - 1), 0)

    def exp_map(i, te_ref, nact_ref):
        return (te_ref[jnp.minimum(i, nact_ref[0] - 1)], 0, 0)

    grid_spec = pltpu.PrefetchScalarGridSpec(
        num_scalar_prefetch=2,
        grid=(n_tiles,),
        in_specs=[pl.BlockSpec((tile_rows, LANES), tile_map),
                  pl.BlockSpec((1, d, 2 * D_FF), exp_map),
                  pl.BlockSpec((1, 1, 2 * D_FF), exp_map),
                  pl.BlockSpec((1, D_FF, d), exp_map),
                  pl.BlockSpec((1, 1, d), exp_map)],
        out_specs=pl.BlockSpec((tile_rows, LANES), lambda i, te_ref, nact_ref: (i, 0)),
    )
    return pl.pallas_call(
        _expert_kernel,
        grid_spec=grid_spec,
        out_shape=jax.ShapeDtypeStruct(xs.shape, jnp.uint32),
        compiler_params=_cparams(("arbitrary",)),
        name="expert_mlp",
    )(te, nact, xs, w_gu, b_gu, w_dn, b_dn)


def _combine_kernel(dest_hbm, h1_ref, info_ref, gfin_ref, ys_hbm, out_ref, dest_smem, ybuf, dsem, sem,
                    *, chunks_per_batch):
    j = pl.program_id(0)
    n = pl.num_programs(0)
    slot = j % 2

    def fetch(step, s):
        chunk = (step // chunks_per_batch) * (chunks_per_batch + 1) + step % chunks_per_batch + 1
        return pltpu.make_async_copy(dest_hbm.at[chunk], dest_smem.at[s], dsem.at[s])

    def issue(s):
        def body(t, carry):
            for kk in range(TOP_K):
                src = _slab(ys_hbm, dest_smem[s, t * TOP_K + kk])
                pltpu.make_async_copy(src, _slab(ybuf.at[s, kk], t), sem.at[s]).start()
            return carry

        lax.fori_loop(0, ROUTE_TILE, body, 0)

    @pl.when(j == 0)
    def _():
        first = fetch(0, 0)
        first.start()
        first.wait()
        issue(0)

    @pl.when(j + 1 < n)
    def _():
        fetch(j + 1, 1 - slot).start()

    for kk in range(TOP_K):
        pltpu.make_async_copy(ys_hbm.at[pl.ds(0, ROUTE_TILE * PACK_ROWS), :], ybuf.at[slot, kk], sem.at[slot]).wait()
    info = info_ref[...]
    h2 = h1_ref[...]
    for kk in range(TOP_K):
        h2 = h2 + info[:, kk:kk + 1] * _load_packed(ybuf, ROUTE_TILE, lead=(slot, kk))
    ms = jnp.mean(h2 * h2, axis=-1, keepdims=True)
    out_ref[...] = h2 * lax.rsqrt(ms + RMS_EPS) * gfin_ref[...]

    @pl.when(j + 1 < n)
    def _():
        fetch(j + 1, 1 - slot).wait()
        issue(1 - slot)


def _combine(dest2, h1, info, g_final, ys, n_batch, seq):
    d = h1.shape[1]
    cpb = seq // ROUTE_TILE

    def padded_chunk(j):
        return ((j // cpb) * (cpb + 1) + j % cpb + 1, 0)

    return pl.pallas_call(
        functools.partial(_combine_kernel, chunks_per_batch=cpb),
        grid=(n_batch * cpb,),
        in_specs=[pl.BlockSpec(memory_space=pl.ANY),
                  pl.BlockSpec((ROUTE_TILE, d), padded_chunk),
                  pl.BlockSpec((ROUTE_TILE, LANES), padded_chunk),
                  pl.BlockSpec((1, d), lambda j: (0, 0)),
                  pl.BlockSpec(memory_space=pl.ANY)],
        out_specs=pl.BlockSpec((ROUTE_TILE, d), lambda j: (j, 0)),
        out_shape=jax.ShapeDtypeStruct((n_batch * seq, d), F32),
        scratch_shapes=[pltpu.SMEM((2, ROUTE_TILE * TOP_K), jnp.int32),
                        pltpu.VMEM((2, TOP_K, ROUTE_TILE * PACK_ROWS, LANES), jnp.uint32),
                        pltpu.SemaphoreType.DMA((2,)), pltpu.SemaphoreType.DMA((2,))],
        compiler_params=_cparams(("arbitrary",)),
        name="combine",
    )(dest2, h1, info, g_final, ys)


def kernel(x, meta_tokens, lb_logits, g_mix, w_in, w_dw, b_dw, ln_g, ln_b, w_conv_out, b_conv_out,
           g_onorm, w_rnn_out, w_o, g_ffn, w_router, b_router, w_gate_up, b_gate_up, w_down, b_down,
           g_final):
    n_batch, seq, d = x.shape
    assert d == D_MODEL and w_in.shape[0] == 1, "single-layer block with D_MODEL features"
    assert CHUNK == ROUTE_TILE and seq % CHUNK == 0
    seq_pad = CHUNK + seq
    tp = n_batch * seq_pad
    assert tp % ROW_TILE == 0

    meta = jnp.broadcast_to(meta_tokens.astype(x.dtype)[None], (n_batch, N_META, d))
    hp = jnp.concatenate([jnp.zeros((n_batch, CHUNK_PAD, d), x.dtype), meta, x], axis=1).reshape(tp, d)
    vec = lambda a: a.reshape(1, -1).astype(F32)

    aglu, q, k, logf, v, og, sga, sgb = _in_proj(hp, vec(g_mix[0]), lb_logits.astype(F32), w_in[0].astype(BF16))
    ap = _conv_branch(aglu, w_dw[0].astype(F32), vec(b_dw[0]), vec(ln_g[0]), vec(ln_b[0]),
                      w_conv_out[0].astype(BF16), vec(b_conv_out[0]), sga)
    on = _hgrn2_scan(q, k, v, logf, og, vec(g_onorm[0]), n_batch)
    w_router_pad = jnp.pad(w_router[0].astype(F32), ((0, 0), (0, LANES - N_EXPERTS)))
    b_router_pad = jnp.pad(vec(b_router[0]), ((0, 0), (0, LANES - N_EXPERTS)))
    h1, u2p, info, cnt = _merge_route(hp, on, ap, sgb, w_rnn_out[0].astype(BF16), w_o[0].astype(BF16),
                                      vec(g_ffn[0]), w_router_pad, b_router_pad)

    n_tiles_max = -(-(tp * TOP_K + N_EXPERTS * (EXPERT_TILE - 1)) // EXPERT_TILE)
    dest, te = _slots(info, cnt, n_tiles_max)
    dest2 = dest[:, :TOP_K].reshape(tp // ROUTE_TILE, ROUTE_TILE * TOP_K)
    xs = _dispatch(te[2, :N_EXPERTS], dest2, u2p, n_tiles_max * EXPERT_TILE)
    ys = _expert_mlp(te[0, :n_tiles_max], te[1, :1], xs, w_gate_up[0],
                     b_gate_up[0].reshape(N_EXPERTS, 1, -1).astype(F32), w_down[0],
                     b_down[0].reshape(N_EXPERTS, 1, -1).astype(F32))
    out = _combine(dest2, h1, info, vec(g_final), ys, n_batch, seq)
    return out.reshape(n_batch, seq, d)
```

```python
import functools

import jax
import jax.numpy as jnp
from jax import lax
from jax.experimental import pallas as pl
from jax.experimental.pallas import tpu as pltpu

F32 = jnp.float32
BF16 = jnp.bfloat16

D_MODEL = 1024
N_META = 16
CHUNK = 128
CHUNK_PAD = CHUNK - N_META
CONV_WIDTH = 31
HEAD_DIM = 128
N_HEADS = D_MODEL // HEAD_DIM
N_EXPERTS = 32
TOP_K = 4
D_FF = D_MODEL
SWIGLU_LIMIT = 7.0
SWIGLU_ALPHA = 1.702
RMS_EPS = 1e-6
LN_EPS = 1e-5
N_IN_GROUPS = 8

LANES = 128
SUBLANES = 8
SUB_BLOCK = 32
N_SUB = CHUNK // SUB_BLOCK
HALO = 32
ROW_TILE = 640
IN_TILE = 320
CONV_ROWS = 32
EXPERT_TILE = 512
ROUTE_TILE = 128
STAGE_SLOTS = 3
NEG_BIG = -1e30
VMEM_LIMIT = 56 * 1024 * 1024


def _sigmoid(x):
    return 1.0 / (1.0 + jnp.exp(-x))


def _cparams(sem):
    return pltpu.CompilerParams(dimension_semantics=sem, vmem_limit_bytes=VMEM_LIMIT)


def _const_spec(shape):
    nd = len(shape)
    return pl.BlockSpec(shape, lambda *_: (0,) * nd)


PACK_ROWS = D_MODEL // 2 // 128
HIGH_HALF = 0xFFFF0000


def _store_packed(ref, x):
    rows = x.shape[0]
    half = D_MODEL // 2
    lo = lax.bitcast_convert_type(x[:, :half].astype(BF16).astype(F32), jnp.uint32)
    hi = lax.bitcast_convert_type(x[:, half:].astype(BF16).astype(F32), jnp.uint32)
    words = (lo >> 16) | (hi & jnp.uint32(HIGH_HALF))
    for j in range(PACK_ROWS):
        ref[pl.ds(j, rows, stride=PACK_ROWS), :] = words[:, j * LANES:(j + 1) * LANES]


def _load_packed(ref, rows, lead=()):
    lo, hi = [], []
    for j in range(PACK_ROWS):
        w = ref[lead + (pl.ds(j, rows, stride=PACK_ROWS), slice(None))]
        lo.append(lax.bitcast_convert_type(w << 16, F32))
        hi.append(lax.bitcast_convert_type(w & jnp.uint32(HIGH_HALF), F32))
    return jnp.concatenate(lo + hi, axis=1)


def _in_proj_kernel(h_ref, g_ref, lbl_ref, w_ref, aglu_ref, q_ref, k_ref, logf_ref, v_ref,
                    og_ref, sga_ref, sgb_ref):
    d = D_MODEL
    h = h_ref[...]
    ms = jnp.mean(h * h, axis=-1, keepdims=True)
    u = (h * lax.rsqrt(ms + RMS_EPS) * g_ref[...]).astype(BF16)

    def proj(j):
        return jnp.dot(u, w_ref[:, j * d:(j + 1) * d], preferred_element_type=F32)

    aglu_ref[...] = (proj(0) * _sigmoid(proj(1))).astype(aglu_ref.dtype)
    zq = proj(2)
    q_ref[...] = (zq * _sigmoid(zq)).astype(q_ref.dtype)
    lbl = lbl_ref[...]
    e = jnp.exp(lbl - jnp.max(lbl, axis=0, keepdims=True))
    lb = e[0:1, :] / jnp.sum(e, axis=0, keepdims=True)
    s = _sigmoid(proj(3))
    logf_ref[...] = jnp.log(lb + (1.0 - lb) * s)
    k_ref[...] = ((1.0 - lb) * (1.0 - s)).astype(k_ref.dtype)
    v_ref[...] = proj(4).astype(v_ref.dtype)
    zg = proj(5)
    og_ref[...] = (zg * _sigmoid(zg)).astype(og_ref.dtype)
    sga_ref[...] = _sigmoid(proj(6)).astype(sga_ref.dtype)
    sgb_ref[...] = _sigmoid(proj(7)).astype(sgb_ref.dtype)


def _in_proj(hp, g_mix, lb_logits, w_in_bf16):
    tp, d = hp.shape
    tm = IN_TILE
    row = pl.BlockSpec((tm, d), lambda i: (i, 0))
    out_dtypes = [BF16, BF16, BF16, F32, BF16, BF16, BF16, BF16]
    return pl.pallas_call(
        _in_proj_kernel,
        grid=(tp // tm,),
        in_specs=[row, _const_spec((1, d)), _const_spec(lb_logits.shape),
                  pl.BlockSpec(w_in_bf16.shape, lambda i: (0, 0), pipeline_mode=pl.Buffered(1))],
        out_specs=[row] * 8,
        out_shape=[jax.ShapeDtypeStruct((tp, d), dt) for dt in out_dtypes],
        compiler_params=_cparams(("parallel",)),
        name="in_proj",
    )(hp, g_mix, lb_logits, w_in_bf16)


def _conv_kernel(halo_ref, cur_ref, wdw_ref, bdw_ref, lng_ref, lnb_ref, wout_ref, bout_ref, sga_ref,
                 out_ref, win_ref, acc_ref, shift_ref):
    tm = cur_ref.shape[0]
    win_ref[0:HALO, :] = halo_ref[...].astype(F32)
    win_ref[HALO:, :] = cur_ref[...].astype(F32)
    first_tap = HALO - (CONV_WIDTH - 1)
    for c in range(D_MODEL // LANES):
        lanes = slice(c * LANES, (c + 1) * LANES)
        for s in range(SUBLANES):
            n_rows = tm + HALO - (SUBLANES if s else 0)
            shift_ref[s, 0:n_rows, :] = win_ref[pl.ds(s, n_rows), lanes]
        w_c = wdw_ref[:, lanes]
        bias = jnp.broadcast_to(bdw_ref[:, lanes], (CONV_ROWS, LANES))

        def chunk(r, carry, lanes=lanes, w_c=w_c, bias=bias):
            r0 = pl.multiple_of(r * CONV_ROWS, CONV_ROWS)
            acc = bias
            for j in range(CONV_WIDTH):
                off = first_tap + j
                rows = pl.ds(r0 + off - off % SUBLANES, CONV_ROWS)
                acc = acc + w_c[j:j + 1, :] * shift_ref[off % SUBLANES, rows, :]
            acc_ref[pl.ds(r0, CONV_ROWS), lanes] = acc
            return carry

        lax.fori_loop(0, tm // CONV_ROWS, chunk, 0)
    a = acc_ref[...]
    mu = jnp.mean(a, axis=-1, keepdims=True)
    ac = a - mu
    var = jnp.mean(ac * ac, axis=-1, keepdims=True)
    y = ac * lax.rsqrt(var + LN_EPS) * lng_ref[...] + lnb_ref[...]
    y = y * _sigmoid(y)
    o = jnp.dot(y.astype(BF16), wout_ref[...], preferred_element_type=F32) + bout_ref[...]
    out_ref[...] = (sga_ref[...].astype(F32) * o).astype(out_ref.dtype)


def _conv_branch(aglu, w_dw, b_dw, ln_g, ln_b, w_out_bf16, b_out, sga):
    tp, d = aglu.shape
    tm = ROW_TILE
    per = tm // HALO
    row = pl.BlockSpec((tm, d), lambda i: (i, 0))
    halo = pl.BlockSpec((HALO, d), lambda i: (jnp.maximum(i * per - 1, 0), 0))
    vec = _const_spec((1, d))
    return pl.pallas_call(
        _conv_kernel,
        grid=(tp // tm,),
        in_specs=[halo, row, _const_spec(w_dw.shape), vec, vec, vec, _const_spec((d, d)), vec, row],
        out_specs=row,
        out_shape=jax.ShapeDtypeStruct((tp, d), BF16),
        scratch_shapes=[pltpu.VMEM((tm + HALO, d), F32), pltpu.VMEM((tm, d), F32),
                        pltpu.VMEM((SUBLANES, tm + HALO, LANES), F32)],
        compiler_params=_cparams(("parallel",)),
        name="conv_branch",
    )(aglu, aglu, w_dw, b_dw, ln_g, ln_b, w_out_bf16, b_out, sga)


def _nt_dot(a, b):
    return lax.dot_general(a, b, (((1,), (1,)), ((), ())), preferred_element_type=F32)


def _tn_dot(a, b):
    return lax.dot_general(a, b, (((0,), (0,)), ((), ())), preferred_element_type=F32)


def _scan_kernel(q_ref, k_ref, v_ref, lf_ref, og_ref, gon_ref, o_ref, st_ref):
    @pl.when(pl.program_id(1) == 0)
    def _():
        st_ref[...] = jnp.zeros_like(st_ref)

    c = CHUNK
    row = lax.broadcasted_iota(jnp.int32, (c, c), 0)
    col = lax.broadcasted_iota(jnp.int32, (c, c), 1)
    causal = col <= row
    rblk = row // SUB_BLOCK
    cblk = col // SUB_BLOCK
    diag_mask = jnp.logical_and(causal, rblk == cblk)
    bcum = jnp.dot(causal.astype(F32), lf_ref[...], preferred_element_type=F32,
                   precision=lax.Precision.HIGHEST)

    def bcast_rows(rows):
        return jnp.concatenate([jnp.broadcast_to(r, (SUB_BLOCK, HEAD_DIM)) for r in rows], axis=0)

    for h in range(N_HEADS):
        hs = slice(h * HEAD_DIM, (h + 1) * HEAD_DIM)
        b = bcum[:, hs]
        q = q_ref[:, hs].astype(F32)
        k = k_ref[:, hs].astype(F32)
        v = v_ref[:, hs]
        ends = [b[i * SUB_BLOCK + SUB_BLOCK - 1:i * SUB_BLOCK + SUB_BLOCK, :] for i in range(N_SUB)]
        mids = [b[i * SUB_BLOCK + SUB_BLOCK // 2 - 1:i * SUB_BLOCK + SUB_BLOCK // 2, :] for i in range(N_SUB)]
        mid_full = bcast_rows(mids)
        end_full = bcast_rows(ends)
        b_last = ends[-1]
        qm = (q * jnp.exp(b - mid_full)).astype(BF16)
        km = (k * jnp.exp(mid_full - b)).astype(BF16)
        scores = jnp.where(diag_mask, _nt_dot(qm, km), 0.0)
        ke = k * jnp.exp(end_full - b)
        q_parts, k_parts = [], []
        for j in range(N_SUB - 1):
            arg = jnp.where(rblk > j, b - ends[j], NEG_BIG)
            q_parts.append((q * jnp.exp(arg)).astype(BF16))
            k_parts.append(jnp.where(rblk == j, ke, 0.0).astype(BF16))
        scores = scores + _nt_dot(jnp.concatenate(q_parts, axis=1), jnp.concatenate(k_parts, axis=1))
        st = st_ref[h]
        q_in = (q * jnp.exp(b)).astype(BF16)
        o = _nt_dot(q_in, st.astype(BF16)) + jnp.dot(scores.astype(BF16), v, preferred_element_type=F32)
        k_out = (k * jnp.exp(b_last - b)).astype(BF16)
        st_ref[h] = st * jnp.exp(b_last) + _tn_dot(v, k_out)
        ms = jnp.mean(o * o, axis=-1, keepdims=True)
        on = o * lax.rsqrt(ms + RMS_EPS) * gon_ref[:, hs]
        o_ref[:, hs] = (on * og_ref[:, hs].astype(F32)).astype(o_ref.dtype)


def _hgrn2_scan(q, k, v, logf, og, g_onorm, n_batch):
    tp, d = q.shape
    n_chunks = tp // n_batch // CHUNK
    blk = pl.BlockSpec((CHUNK, d), lambda b, c: (b * n_chunks + c, 0))
    return pl.pallas_call(
        _scan_kernel,
        grid=(n_batch, n_chunks),
        in_specs=[blk, blk, blk, blk, blk, pl.BlockSpec((1, d), lambda b, c: (0, 0))],
        out_specs=blk,
        out_shape=jax.ShapeDtypeStruct((tp, d), BF16),
        scratch_shapes=[pltpu.VMEM((N_HEADS, HEAD_DIM, HEAD_DIM), F32)],
        compiler_params=_cparams(("arbitrary", "arbitrary")),
        name="hgrn2_scan",
    )(q, k, v, logf, og, g_onorm)


def _merge_route_kernel(h_ref, on_ref, ap_ref, sgb_ref, wrnn_ref, wo_ref, gffn_ref, wr_ref, br_ref,
                        h1_ref, u2_ref, info_ref, cnt_ref, tri_ref, carry_ref, colcnt_ref):
    tm = h_ref.shape[0]

    @pl.when(pl.program_id(0) == 0)
    def _():
        r_i = lax.broadcasted_iota(jnp.int32, (tm, tm), 0)
        c_i = lax.broadcasted_iota(jnp.int32, (tm, tm), 1)
        tri_ref[...] = (c_i < r_i).astype(BF16)
        carry_ref[...] = jnp.zeros_like(carry_ref)
        colcnt_ref[...] = jnp.zeros_like(colcnt_ref)

    r = jnp.dot(on_ref[...], wrnn_ref[...], preferred_element_type=F32)
    y = ap_ref[...].astype(F32) + sgb_ref[...].astype(F32) * r
    h1 = h_ref[...] + jnp.dot(y.astype(BF16), wo_ref[...], preferred_element_type=F32)
    h1_ref[...] = h1
    ms = jnp.mean(h1 * h1, axis=-1, keepdims=True)
    u2 = h1 * lax.rsqrt(ms + RMS_EPS) * gffn_ref[...]
    _store_packed(u2_ref, u2)
    logits = jnp.dot(u2, wr_ref[...], preferred_element_type=F32,
                     precision=lax.Precision.HIGHEST) + br_ref[...]
    lane = lax.broadcasted_iota(jnp.int32, (tm, LANES), 1)
    cur = jnp.where(lane < N_EXPERTS, logits, NEG_BIG)
    vals, idxs, sels = [], [], []
    for _ in range(TOP_K):
        m = jnp.max(cur, axis=-1, keepdims=True)
        idx = jnp.min(jnp.where(cur == m, lane, LANES), axis=-1, keepdims=True)
        sel = lane == idx
        cur = jnp.where(sel, 2.0 * NEG_BIG, cur)
        vals.append(m)
        idxs.append(idx)
        sels.append(sel)
    exps = [jnp.exp(vk - vals[0]) for vk in vals]
    den = exps[0] + exps[1] + exps[2] + exps[3]
    onehot = jnp.logical_or(jnp.logical_or(sels[0], sels[1]), jnp.logical_or(sels[2], sels[3]))
    onehot_bf = onehot.astype(BF16)
    rank_all = jnp.dot(tri_ref[...], onehot_bf, preferred_element_type=F32) + carry_ref[...]
    carry_ref[...] += jnp.sum(onehot.astype(F32), axis=0, keepdims=True)
    colcnt_ref[...] += _tn_dot(onehot_bf, jnp.ones((tm, LANES), BF16))
    info = jnp.zeros((tm, LANES), F32)
    for kk in range(TOP_K):
        rank_k = jnp.sum(jnp.where(sels[kk], rank_all, 0.0), axis=-1, keepdims=True)
        info = jnp.where(lane == kk, exps[kk] / den, info)
        info = jnp.where(lane == TOP_K + kk, idxs[kk].astype(F32), info)
        info = jnp.where(lane == 2 * TOP_K + kk, rank_k, info)
    info_ref[...] = info
    cnt_ref[...] = colcnt_ref[...]


def _merge_route(hp, on, ap, sgb, w_rnn_bf16, w_o_bf16, g_ffn, w_router_pad, b_router_pad):
    tp, d = hp.shape
    tm = ROW_TILE
    row = pl.BlockSpec((tm, d), lambda i: (i, 0))
    vec = _const_spec((1, d))
    return pl.pallas_call(
        _merge_route_kernel,
        grid=(tp // tm,),
        in_specs=[row, row, row, row, _const_spec((d, d)), _const_spec((d, d)), vec,
                  _const_spec((d, LANES)), _const_spec((1, LANES))],
        out_specs=[row, pl.BlockSpec((tm * PACK_ROWS, LANES), lambda i: (i, 0)),
                   pl.BlockSpec((tm, LANES), lambda i: (i, 0)), _const_spec((LANES, LANES))],
        out_shape=[jax.ShapeDtypeStruct((tp, d), F32), jax.ShapeDtypeStruct((tp * PACK_ROWS, LANES), jnp.uint32),
                   jax.ShapeDtypeStruct((tp, LANES), F32), jax.ShapeDtypeStruct((LANES, LANES), F32)],
        scratch_shapes=[pltpu.VMEM((tm, tm), BF16), pltpu.VMEM((1, LANES), F32),
                        pltpu.VMEM((LANES, LANES), F32)],
        compiler_params=_cparams(("arbitrary",)),
        name="merge_route",
    )(hp, on, ap, sgb, w_rnn_bf16, w_o_bf16, g_ffn, w_router_pad, b_router_pad)


def _slots_kernel(info_ref, cnt_ref, dest_ref, te_ref):
    tm = info_ref.shape[0]
    n_tile_lanes = te_ref.shape[1]
    r_i = lax.broadcasted_iota(jnp.int32, (LANES, LANES), 0)
    c_i = lax.broadcasted_iota(jnp.int32, (LANES, LANES), 1)
    cnt = cnt_ref[...]
    tiles = jnp.floor((cnt + (EXPERT_TILE - 1)) / EXPERT_TILE)
    pend_col = jnp.dot((c_i <= r_i).astype(F32), tiles, preferred_element_type=F32,
                       precision=lax.Precision.HIGHEST)
    pstart_row = _tn_dot_f32(tiles, (r_i < c_i).astype(F32))
    info = info_ref[...]
    lane = lax.broadcasted_iota(jnp.int32, (tm, LANES), 1)
    dest = jnp.zeros((tm, LANES), F32)
    for kk in range(TOP_K):
        idx_k = info[:, TOP_K + kk:TOP_K + kk + 1].astype(jnp.int32)
        start_k = jnp.sum(jnp.where(lane == idx_k, pstart_row[0:1, :], 0.0), axis=-1, keepdims=True)
        slot_k = start_k * EXPERT_TILE + info[:, 2 * TOP_K + kk:2 * TOP_K + kk + 1]
        dest = jnp.where(lane == kk, slot_k, dest)
    dest_ref[...] = dest.astype(jnp.int32)
    tile_id = lax.broadcasted_iota(jnp.int32, (LANES, n_tile_lanes), 1).astype(F32)
    exp_id = lax.broadcasted_iota(jnp.int32, (LANES, n_tile_lanes), 0)
    pend_wide = jnp.concatenate([pend_col] * (n_tile_lanes // LANES), axis=1)
    below = jnp.logical_and(pend_wide <= tile_id, exp_id < N_EXPERTS)
    te = jnp.minimum(jnp.sum(below.astype(F32), axis=0, keepdims=True), N_EXPERTS - 1.0)
    n_active = pend_col[N_EXPERTS - 1:N_EXPERTS, 0:1]
    lane_t = lax.broadcasted_iota(jnp.int32, (8, n_tile_lanes), 1)
    sub_t = lax.broadcasted_iota(jnp.int32, (8, n_tile_lanes), 0)
    pend_row = _tn_dot_f32(tiles, (r_i <= c_i).astype(F32))[0:1, :]
    pend_wide_row = jnp.concatenate([pend_row] + [jnp.zeros_like(pend_row)] * (n_tile_lanes // LANES - 1), axis=1)
    out = jnp.where(sub_t == 0, jnp.broadcast_to(te, (8, n_tile_lanes)),
                    jnp.where(sub_t == 1, jnp.broadcast_to(n_active, (8, n_tile_lanes)),
                              jnp.broadcast_to(pend_wide_row, (8, n_tile_lanes))))
    del lane_t
    te_ref[...] = out.astype(jnp.int32)


def _tn_dot_f32(a, b):
    return lax.dot_general(a, b, (((0,), (0,)), ((), ())), preferred_element_type=F32,
                           precision=lax.Precision.HIGHEST)


def _slots(info, cnt, n_tiles_max):
    tp = info.shape[0]
    tm = ROW_TILE
    n_tile_lanes = -(-n_tiles_max // LANES) * LANES
    return pl.pallas_call(
        _slots_kernel,
        grid=(tp // tm,),
        in_specs=[pl.BlockSpec((tm, LANES), lambda i: (i, 0)), _const_spec((LANES, LANES))],
        out_specs=[pl.BlockSpec((tm, LANES), lambda i: (i, 0)), _const_spec((8, n_tile_lanes))],
        out_shape=[jax.ShapeDtypeStruct((tp, LANES), jnp.int32),
                   jax.ShapeDtypeStruct((8, n_tile_lanes), jnp.int32)],
        compiler_params=_cparams(("arbitrary",)),
        name="slots",
    )(info, cnt)


def _fetch_dest(dest_hbm, dest_smem, dsem, step, slot):
    return pltpu.make_async_copy(dest_hbm.at[step], dest_smem.at[slot], dsem.at[slot])


def _slab(ref, token):
    return ref.at[pl.ds(token * PACK_ROWS, PACK_ROWS), :]


def _dispatch_kernel(pend_ref, dest_hbm, u2_hbm, xs_ref, dest_smem, zero_ref, rows_ref, dsem, sem, zsem, rsem):
    i = pl.program_id(0)
    n = pl.num_programs(0)
    slot = i % 2
    tile_rows = EXPERT_TILE * PACK_ROWS

    def last_tile_zero(e):
        first = pend_ref[e - 1] if e else 0
        dst = xs_ref.at[pl.ds((pend_ref[e] - 1) * tile_rows, tile_rows), :]
        return pend_ref[e] > first, pltpu.make_async_copy(zero_ref, dst, zsem)

    def stage(step):
        s = step % STAGE_SLOTS
        src = u2_hbm.at[pl.ds(step * ROUTE_TILE * PACK_ROWS, ROUTE_TILE * PACK_ROWS), :]
        return pltpu.make_async_copy(src, rows_ref.at[s], rsem.at[s])

    @pl.when(i == 0)
    def _():
        _fetch_dest(dest_hbm, dest_smem, dsem, 0, 0).start()
        stage(0).start()
        zero_ref[...] = jnp.zeros_like(zero_ref)
        for e in range(N_EXPERTS):
            has_tiles, copy = last_tile_zero(e)
            pl.when(has_tiles)(copy.start)
        n_tiles = xs_ref.shape[0] // tile_rows

        def spare_tile(t):
            return pltpu.make_async_copy(zero_ref, xs_ref.at[pl.ds(t * tile_rows, tile_rows), :], zsem)

        lax.fori_loop(pend_ref[N_EXPERTS - 1], n_tiles, lambda t, c: (spare_tile(t).start(), c)[1], 0)
        lax.fori_loop(pend_ref[N_EXPERTS - 1], n_tiles, lambda t, c: (spare_tile(t).wait(), c)[1], 0)
        for e in range(N_EXPERTS):
            has_tiles, copy = last_tile_zero(e)
            pl.when(has_tiles)(copy.wait)

    _fetch_dest(dest_hbm, dest_smem, dsem, i, slot).wait()
    stage(i).wait()

    @pl.when(i + 1 < n)
    def _():
        _fetch_dest(dest_hbm, dest_smem, dsem, i + 1, 1 - slot).start()
        stage(i + 1).start()

    rows = rows_ref.at[i % STAGE_SLOTS]

    def issue(t, carry):
        for kk in range(TOP_K):
            dst = _slab(xs_ref, dest_smem[slot, t * TOP_K + kk])
            pltpu.make_async_copy(_slab(rows, t), dst, sem.at[i % STAGE_SLOTS]).start(priority=kk % 2)
        return carry

    lax.fori_loop(0, ROUTE_TILE, issue, 0)

    def drain(step):
        s = step % STAGE_SLOTS
        for _ in range(TOP_K):
            pltpu.make_async_copy(rows_ref.at[s], xs_ref.at[pl.ds(0, ROUTE_TILE * PACK_ROWS), :], sem.at[s]).wait()

    pl.when(i > 0)(lambda: drain(i - 1))
    pl.when(i == n - 1)(lambda: drain(i))


def _dispatch(pend, dest2, u2p, n_slots):
    n_steps = dest2.shape[0]
    grid_spec = pltpu.PrefetchScalarGridSpec(
        num_scalar_prefetch=1,
        grid=(n_steps,),
        in_specs=[pl.BlockSpec(memory_space=pl.ANY), pl.BlockSpec(memory_space=pl.ANY)],
        out_specs=pl.BlockSpec(memory_space=pl.ANY),
        scratch_shapes=[pltpu.SMEM((2, ROUTE_TILE * TOP_K), jnp.int32),
                        pltpu.VMEM((EXPERT_TILE * PACK_ROWS, LANES), jnp.uint32),
                        pltpu.VMEM((STAGE_SLOTS, ROUTE_TILE * PACK_ROWS, LANES), jnp.uint32),
                        pltpu.SemaphoreType.DMA((2,)), pltpu.SemaphoreType.DMA((STAGE_SLOTS,)),
                        pltpu.SemaphoreType.DMA, pltpu.SemaphoreType.DMA((STAGE_SLOTS,))],
    )
    return pl.pallas_call(
        _dispatch_kernel,
        grid_spec=grid_spec,
        out_shape=jax.ShapeDtypeStruct((n_slots * PACK_ROWS, LANES), jnp.uint32),
        compiler_params=_cparams(("arbitrary",)),
        name="dispatch",
    )(pend, dest2, u2p)


def _expert_kernel(te_ref, nact_ref, x_ref, wgu_ref, bgu_ref, wdn_ref, bdn_ref, y_ref):
    del te_ref

    @pl.when(pl.program_id(0) < nact_ref[0])
    def _():
        x = _load_packed(x_ref, EXPERT_TILE).astype(BF16)
        gu = jnp.dot(x, wgu_ref[0].astype(BF16), preferred_element_type=F32) + bgu_ref[0]
        gate = jnp.minimum(gu[:, :D_FF], SWIGLU_LIMIT)
        up = jnp.clip(gu[:, D_FF:], -SWIGLU_LIMIT, SWIGLU_LIMIT)
        hdn = (up + 1.0) * (gate * _sigmoid(SWIGLU_ALPHA * gate))
        y = jnp.dot(hdn.astype(BF16), wdn_ref[0].astype(BF16), preferred_element_type=F32) + bdn_ref[0]
        _store_packed(y_ref, y)

    @pl.when(pl.program_id(0) >= nact_ref[0])
    def _():
        y_ref[...] = jnp.zeros_like(y_ref)


def _expert_mlp(te, nact, xs, w_gu, b_gu, w_dn, b_dn):
    d = D_MODEL
    tile_rows = EXPERT_TILE * PACK_ROWS
    n_tiles = xs.shape[0] // tile_rows

    def tile_map(i, te_ref, nact_ref):
        return (jnp.minimum(i, nact_ref[0] - 1), 0)

    def exp_map(i, te_ref, nact_ref):
        return (te_ref[jnp.minimum(i, nact_ref[0] - 1)], 0, 0)

    grid_spec = pltpu.PrefetchScalarGridSpec(
        num_scalar_prefetch=2,
        grid=(n_tiles,),
        in_specs=[pl.BlockSpec((tile_rows, LANES), tile_map),
                  pl.BlockSpec((1, d, 2 * D_FF), exp_map),
                  pl.BlockSpec((1, 1, 2 * D_FF), exp_map),
                  pl.BlockSpec((1, D_FF, d), exp_map),
                  pl.BlockSpec((1, 1, d), exp_map)],
        out_specs=pl.BlockSpec((tile_rows, LANES), lambda i, te_ref, nact_ref: (i, 0)),
    )
    return pl.pallas_call(
        _expert_kernel,
        grid_spec=grid_spec,
        out_shape=jax.ShapeDtypeStruct(xs.shape, jnp.uint32),
        compiler_params=_cparams(("arbitrary",)),
        name="expert_mlp",
    )(te, nact, xs, w_gu, b_gu, w_dn, b_dn)


def _combine_kernel(dest_hbm, h1_ref, info_ref, gfin_ref, ys_hbm, out_ref, dest_smem, ybuf, dsem, sem,
                    *, chunks_per_batch):
    j = pl.program_id(0)
    n = pl.num_programs(0)
    slot = j % 2

    def fetch(step):
        chunk = (step // chunks_per_batch) * (chunks_per_batch + 1) + step % chunks_per_batch + 1
        return pltpu.make_async_copy(dest_hbm.at[chunk], dest_smem.at[step % 2], dsem.at[step % 2])

    def issue(step):
        s = step % 2

        def body(t, carry):
            for kk in range(TOP_K):
                src = _slab(ys_hbm, dest_smem[s, t * TOP_K + kk])
                pltpu.make_async_copy(src, _slab(ybuf.at[s, kk], t), sem.at[s]).start(priority=kk % 2)
            return carry

        lax.fori_loop(0, ROUTE_TILE, body, 0)

    @pl.when(j == 0)
    def _():
        fetch(0).start()
        fetch(0).wait()
        issue(0)
        fetch(1).start()

    @pl.when(j + 1 < n)
    def _():
        fetch(j + 1).wait()
        issue(j + 1)

    @pl.when(j + 2 < n)
    def _():
        fetch(j + 2).start()

    for kk in range(TOP_K):
        pltpu.make_async_copy(ys_hbm.at[pl.ds(0, ROUTE_TILE * PACK_ROWS), :], ybuf.at[slot, kk], sem.at[slot]).wait()
    info = info_ref[...]
    h2 = h1_ref[...]
    for kk in range(TOP_K):
        h2 = h2 + info[:, kk:kk + 1] * _load_packed(ybuf, ROUTE_TILE, lead=(slot, kk))
    ms = jnp.mean(h2 * h2, axis=-1, keepdims=True)
    out_ref[...] = h2 * lax.rsqrt(ms + RMS_EPS) * gfin_ref[...]


def _combine(dest2, h1, info, g_final, ys, n_batch, seq):
    d = h1.shape[1]
    cpb = seq // ROUTE_TILE

    def padded_chunk(j):
        return ((j // cpb) * (cpb + 1) + j % cpb + 1, 0)

    return pl.pallas_call(
        functools.partial(_combine_kernel, chunks_per_batch=cpb),
        grid=(n_batch * cpb,),
        in_specs=[pl.BlockSpec(memory_space=pl.ANY),
                  pl.BlockSpec((ROUTE_TILE, d), padded_chunk),
                  pl.BlockSpec((ROUTE_TILE, LANES), padded_chunk),
                  pl.BlockSpec((1, d), lambda j: (0, 0)),
                  pl.BlockSpec(memory_space=pl.ANY)],
        out_specs=pl.BlockSpec((ROUTE_TILE, d), lambda j: (j, 0)),
        out_shape=jax.ShapeDtypeStruct((n_batch * seq, d), F32),
        scratch_shapes=[pltpu.SMEM((2, ROUTE_TILE * TOP_K), jnp.int32),
                        pltpu.VMEM((2, TOP_K, ROUTE_TILE * PACK_ROWS, LANES), jnp.uint32),
                        pltpu.SemaphoreType.DMA((2,)), pltpu.SemaphoreType.DMA((2,))],
        compiler_params=_cparams(("arbitrary",)),
        name="combine",
    )(dest2, h1, info, g_final, ys)


def kernel(x, meta_tokens, lb_logits, g_mix, w_in, w_dw, b_dw, ln_g, ln_b, w_conv_out, b_conv_out,
           g_onorm, w_rnn_out, w_o, g_ffn, w_router, b_router, w_gate_up, b_gate_up, w_down, b_down,
           g_final):
    n_batch, seq, d = x.shape
    assert d == D_MODEL and w_in.shape[0] == 1, "single-layer block with D_MODEL features"
    assert CHUNK == ROUTE_TILE and seq % CHUNK == 0
    seq_pad = CHUNK + seq
    tp = n_batch * seq_pad
    assert tp % ROW_TILE == 0

    meta = jnp.broadcast_to(meta_tokens.astype(x.dtype)[None], (n_batch, N_META, d))
    hp = jnp.concatenate([jnp.zeros((n_batch, CHUNK_PAD, d), x.dtype), meta, x], axis=1).reshape(tp, d)
    vec = lambda a: a.reshape(1, -1).astype(F32)

    aglu, q, k, logf, v, og, sga, sgb = _in_proj(hp, vec(g_mix[0]), lb_logits.astype(F32), w_in[0].astype(BF16))
    ap = _conv_branch(aglu, w_dw[0].astype(F32), vec(b_dw[0]), vec(ln_g[0]), vec(ln_b[0]),
                      w_conv_out[0].astype(BF16), vec(b_conv_out[0]), sga)
    on = _hgrn2_scan(q, k, v, logf, og, vec(g_onorm[0]), n_batch)
    w_router_pad = jnp.pad(w_router[0].astype(F32), ((0, 0), (0, LANES - N_EXPERTS)))
    b_router_pad = jnp.pad(vec(b_router[0]), ((0, 0), (0, LANES - N_EXPERTS)))
    h1, u2p, info, cnt = _merge_route(hp, on, ap, sgb, w_rnn_out[0].astype(BF16), w_o[0].astype(BF16),
                                      vec(g_ffn[0]), w_router_pad, b_router_pad)

    n_tiles_max = -(-(tp * TOP_K + N_EXPERTS * (EXPERT_TILE - 1)) // EXPERT_TILE)
    dest, te = _slots(info, cnt, n_tiles_max)
    dest2 = dest[:, :TOP_K].reshape(tp // ROUTE_TILE, ROUTE_TILE * TOP_K)
    xs = _dispatch(te[2, :N_EXPERTS], dest2, u2p, n_tiles_max * EXPERT_TILE)
    ys = _expert_mlp(te[0, :n_tiles_max], te[1, :1], xs, w_gate_up[0],
                     b_gate_up[0].reshape(N_EXPERTS, 1, -1).astype(F32), w_down[0],
                     b_down[0].reshape(N_EXPERTS, 1, -1).astype(F32))
    out = _combine(dest2, h1, info, vec(g_final), ys, n_batch, seq)
    return out.reshape(n_batch, seq, d)
```

```python
import functools

import jax
import jax.numpy as jnp
from jax import lax
from jax.experimental import pallas as pl
from jax.experimental.pallas import tpu as pltpu

F32 = jnp.float32
BF16 = jnp.bfloat16

D_MODEL = 1024
N_META = 16
CHUNK = 128
CHUNK_PAD = CHUNK - N_META
CONV_WIDTH = 31
HEAD_DIM = 128
N_HEADS = D_MODEL // HEAD_DIM
N_EXPERTS = 32
TOP_K = 4
D_FF = D_MODEL
SWIGLU_LIMIT = 7.0
SWIGLU_ALPHA = 1.702
RMS_EPS = 1e-6
LN_EPS = 1e-5
N_IN_GROUPS = 8

LANES = 128
SUBLANES = 8
SUB_BLOCK = 32
N_SUB = CHUNK // SUB_BLOCK
HALO = 32
ROW_TILE = 640
IN_TILE = 320
CONV_ROWS = 32
EXPERT_TILE = 512
ROUTE_TILE = 128
STAGE_SLOTS = 3
NEG_BIG = -1e30
VMEM_LIMIT = 56 * 1024 * 1024


def _sigmoid(x):
    return 1.0 / (1.0 + jnp.exp(-x))


def _cparams(sem):
    return pltpu.CompilerParams(dimension_semantics=sem, vmem_limit_bytes=VMEM_LIMIT)


def _const_spec(shape):
    nd = len(shape)
    return pl.BlockSpec(shape, lambda *_: (0,) * nd)


PACK_ROWS = D_MODEL // 2 // 128
HIGH_HALF = 0xFFFF0000


def _store_packed(ref, x):
    rows = x.shape[0]
    half = D_MODEL // 2
    lo = lax.bitcast_convert_type(x[:, :half].astype(BF16).astype(F32), jnp.uint32)
    hi = lax.bitcast_convert_type(x[:, half:].astype(BF16).astype(F32), jnp.uint32)
    words = (lo >> 16) | (hi & jnp.uint32(HIGH_HALF))
    for j in range(PACK_ROWS):
        ref[pl.ds(j, rows, stride=PACK_ROWS), :] = words[:, j * LANES:(j + 1) * LANES]


def _load_packed(ref, rows, lead=()):
    lo, hi = [], []
    for j in range(PACK_ROWS):
        w = ref[lead + (pl.ds(j, rows, stride=PACK_ROWS), slice(None))]
        lo.append(lax.bitcast_convert_type(w << 16, F32))
        hi.append(lax.bitcast_convert_type(w & jnp.uint32(HIGH_HALF), F32))
    return jnp.concatenate(lo + hi, axis=1)


def _in_proj_kernel(h_ref, g_ref, lbl_ref, w_ref, aglu_ref, q_ref, k_ref, logf_ref, v_ref,
                    og_ref, sga_ref, sgb_ref):
    d = D_MODEL
    h = h_ref[...]
    ms = jnp.mean(h * h, axis=-1, keepdims=True)
    u = (h * lax.rsqrt(ms + RMS_EPS) * g_ref[...]).astype(BF16)

    def proj(j):
        return jnp.dot(u, w_ref[:, j * d:(j + 1) * d], preferred_element_type=F32)

    aglu_ref[...] = (proj(0) * _sigmoid(proj(1))).astype(aglu_ref.dtype)
    zq = proj(2)
    q_ref[...] = (zq * _sigmoid(zq)).astype(q_ref.dtype)
    lbl = lbl_ref[...]
    e = jnp.exp(lbl - jnp.max(lbl, axis=0, keepdims=True))
    lb = e[0:1, :] / jnp.sum(e, axis=0, keepdims=True)
    s = _sigmoid(proj(3))
    logf_ref[...] = jnp.log(lb + (1.0 - lb) * s)
    k_ref[...] = ((1.0 - lb) * (1.0 - s)).astype(k_ref.dtype)
    v_ref[...] = proj(4).astype(v_ref.dtype)
    zg = proj(5)
    og_ref[...] = (zg * _sigmoid(zg)).astype(og_ref.dtype)
    sga_ref[...] = _sigmoid(proj(6)).astype(sga_ref.dtype)
    sgb_ref[...] = _sigmoid(proj(7)).astype(sgb_ref.dtype)


def _in_proj(hp, g_mix, lb_logits, w_in_bf16):
    tp, d = hp.shape
    tm = IN_TILE
    row = pl.BlockSpec((tm, d), lambda i: (i, 0))
    out_dtypes = [BF16, BF16, BF16, F32, BF16, BF16, BF16, BF16]
    return pl.pallas_call(
        _in_proj_kernel,
        grid=(tp // tm,),
        in_specs=[row, _const_spec((1, d)), _const_spec(lb_logits.shape),
                  pl.BlockSpec(w_in_bf16.shape, lambda i: (0, 0), pipeline_mode=pl.Buffered(1))],
        out_specs=[row] * 8,
        out_shape=[jax.ShapeDtypeStruct((tp, d), dt) for dt in out_dtypes],
        compiler_params=_cparams(("parallel",)),
        name="in_proj",
    )(hp, g_mix, lb_logits, w_in_bf16)


def _conv_kernel(halo_ref, cur_ref, wdw_ref, bdw_ref, lng_ref, lnb_ref, wout_ref, bout_ref, sga_ref,
                 out_ref, win_ref, acc_ref, shift_ref):
    tm = cur_ref.shape[0]
    win_ref[0:HALO, :] = halo_ref[...].astype(F32)
    win_ref[HALO:, :] = cur_ref[...].astype(F32)
    first_tap = HALO - (CONV_WIDTH - 1)
    for c in range(D_MODEL // LANES):
        lanes = slice(c * LANES, (c + 1) * LANES)
        for s in range(SUBLANES):
            n_rows = tm + HALO - (SUBLANES if s else 0)
            shift_ref[s, 0:n_rows, :] = win_ref[pl.ds(s, n_rows), lanes]
        w_c = wdw_ref[:, lanes]
        bias = jnp.broadcast_to(bdw_ref[:, lanes], (CONV_ROWS, LANES))

        def chunk(r, carry, lanes=lanes, w_c=w_c, bias=bias):
            r0 = pl.multiple_of(r * CONV_ROWS, CONV_ROWS)
            acc = bias
            for j in range(CONV_WIDTH):
                off = first_tap + j
                rows = pl.ds(r0 + off - off % SUBLANES, CONV_ROWS)
                acc = acc + w_c[j:j + 1, :] * shift_ref[off % SUBLANES, rows, :]
            acc_ref[pl.ds(r0, CONV_ROWS), lanes] = acc
            return carry

        lax.fori_loop(0, tm // CONV_ROWS, chunk, 0)
    a = acc_ref[...]
    mu = jnp.mean(a, axis=-1, keepdims=True)
    ac = a - mu
    var = jnp.mean(ac * ac, axis=-1, keepdims=True)
    y = ac * lax.rsqrt(var + LN_EPS) * lng_ref[...] + lnb_ref[...]
    y = y * _sigmoid(y)
    o = jnp.dot(y.astype(BF16), wout_ref[...], preferred_element_type=F32) + bout_ref[...]
    out_ref[...] = (sga_ref[...].astype(F32) * o).astype(out_ref.dtype)


def _conv_branch(aglu, w_dw, b_dw, ln_g, ln_b, w_out_bf16, b_out, sga):
    tp, d = aglu.shape
    tm = ROW_TILE
    per = tm // HALO
    row = pl.BlockSpec((tm, d), lambda i: (i, 0))
    halo = pl.BlockSpec((HALO, d), lambda i: (jnp.maximum(i * per - 1, 0), 0))
    vec = _const_spec((1, d))
    return pl.pallas_call(
        _conv_kernel,
        grid=(tp // tm,),
        in_specs=[halo, row, _const_spec(w_dw.shape), vec, vec, vec, _const_spec((d, d)), vec, row],
        out_specs=row,
        out_shape=jax.ShapeDtypeStruct((tp, d), BF16),
        scratch_shapes=[pltpu.VMEM((tm + HALO, d), F32), pltpu.VMEM((tm, d), F32),
                        pltpu.VMEM((SUBLANES, tm + HALO, LANES), F32)],
        compiler_params=_cparams(("parallel",)),
        name="conv_branch",
    )(aglu, aglu, w_dw, b_dw, ln_g, ln_b, w_out_bf16, b_out, sga)


def _nt_dot(a, b):
    return lax.dot_general(a, b, (((1,), (1,)), ((), ())), preferred_element_type=F32)


def _tn_dot(a, b):
    return lax.dot_general(a, b, (((0,), (0,)), ((), ())), preferred_element_type=F32)


def _scan_kernel(q_ref, k_ref, v_ref, lf_ref, og_ref, gon_ref, o_ref, st_ref):
    @pl.when(pl.program_id(1) == 0)
    def _():
        st_ref[...] = jnp.zeros_like(st_ref)

    c = CHUNK
    row = lax.broadcasted_iota(jnp.int32, (c, c), 0)
    col = lax.broadcasted_iota(jnp.int32, (c, c), 1)
    causal = col <= row
    rblk = row // SUB_BLOCK
    cblk = col // SUB_BLOCK
    diag_mask = jnp.logical_and(causal, rblk == cblk)
    bcum = jnp.dot(causal.astype(F32), lf_ref[...], preferred_element_type=F32,
                   precision=lax.Precision.HIGHEST)

    def bcast_rows(rows):
        return jnp.concatenate([jnp.broadcast_to(r, (SUB_BLOCK, HEAD_DIM)) for r in rows], axis=0)

    for h in range(N_HEADS):
        hs = slice(h * HEAD_DIM, (h + 1) * HEAD_DIM)
        b = bcum[:, hs]
        q = q_ref[:, hs].astype(F32)
        k = k_ref[:, hs].astype(F32)
        v = v_ref[:, hs]
        ends = [b[i * SUB_BLOCK + SUB_BLOCK - 1:i * SUB_BLOCK + SUB_BLOCK, :] for i in range(N_SUB)]
        mids = [b[i * SUB_BLOCK + SUB_BLOCK // 2 - 1:i * SUB_BLOCK + SUB_BLOCK // 2, :] for i in range(N_SUB)]
        mid_full = bcast_rows(mids)
        end_full = bcast_rows(ends)
        b_last = ends[-1]
        qm = (q * jnp.exp(b - mid_full)).astype(BF16)
        km = (k * jnp.exp(mid_full - b)).astype(BF16)
        scores = jnp.where(diag_mask, _nt_dot(qm, km), 0.0)
        ke = k * jnp.exp(end_full - b)
        q_parts, k_parts = [], []
        for j in range(N_SUB - 1):
            arg = jnp.where(rblk > j, b - ends[j], NEG_BIG)
            q_parts.append((q * jnp.exp(arg)).astype(BF16))
            k_parts.append(jnp.where(rblk == j, ke, 0.0).astype(BF16))
        scores = scores + _nt_dot(jnp.concatenate(q_parts, axis=1), jnp.concatenate(k_parts, axis=1))
        st = st_ref[h]
        q_in = (q * jnp.exp(b)).astype(BF16)
        o = _nt_dot(q_in, st.astype(BF16)) + jnp.dot(scores.astype(BF16), v, preferred_element_type=F32)
        k_out = (k * jnp.exp(b_last - b)).astype(BF16)
        st_ref[h] = st * jnp.exp(b_last) + _tn_dot(v, k_out)
        ms = jnp.mean(o * o, axis=-1, keepdims=True)
        on = o * lax.rsqrt(ms + RMS_EPS) * gon_ref[:, hs]
        o_ref[:, hs] = (on * og_ref[:, hs].astype(F32)).astype(o_ref.dtype)


def _hgrn2_scan(q, k, v, logf, og, g_onorm, n_batch):
    tp, d = q.shape
    n_chunks = tp // n_batch // CHUNK
    blk = pl.BlockSpec((CHUNK, d), lambda b, c: (b * n_chunks + c, 0))
    return pl.pallas_call(
        _scan_kernel,
        grid=(n_batch, n_chunks),
        in_specs=[blk, blk, blk, blk, blk, pl.BlockSpec((1, d), lambda b, c: (0, 0))],
        out_specs=blk,
        out_shape=jax.ShapeDtypeStruct((tp, d), BF16),
        scratch_shapes=[pltpu.VMEM((N_HEADS, HEAD_DIM, HEAD_DIM), F32)],
        compiler_params=_cparams(("arbitrary", "arbitrary")),
        name="hgrn2_scan",
    )(q, k, v, logf, og, g_onorm)


def _merge_route_kernel(h_ref, on_ref, ap_ref, sgb_ref, wrnn_ref, wo_ref, gffn_ref, wr_ref, br_ref,
                        h1_ref, u2_ref, info_ref, cnt_ref, tri_ref, carry_ref, colcnt_ref):
    tm = h_ref.shape[0]

    @pl.when(pl.program_id(0) == 0)
    def _():
        r_i = lax.broadcasted_iota(jnp.int32, (tm, tm), 0)
        c_i = lax.broadcasted_iota(jnp.int32, (tm, tm), 1)
        tri_ref[...] = (c_i < r_i).astype(BF16)
        carry_ref[...] = jnp.zeros_like(carry_ref)
        colcnt_ref[...] = jnp.zeros_like(colcnt_ref)

    r = jnp.dot(on_ref[...], wrnn_ref[...], preferred_element_type=F32)
    y = ap_ref[...].astype(F32) + sgb_ref[...].astype(F32) * r
    h1 = h_ref[...] + jnp.dot(y.astype(BF16), wo_ref[...], preferred_element_type=F32)
    h1_ref[...] = h1
    ms = jnp.mean(h1 * h1, axis=-1, keepdims=True)
    u2 = h1 * lax.rsqrt(ms + RMS_EPS) * gffn_ref[...]
    _store_packed(u2_ref, u2)
    logits = jnp.dot(u2, wr_ref[...], preferred_element_type=F32,
                     precision=lax.Precision.HIGHEST) + br_ref[...]
    lane = lax.broadcasted_iota(jnp.int32, (tm, LANES), 1)
    cur = jnp.where(lane < N_EXPERTS, logits, NEG_BIG)
    vals, idxs, sels = [], [], []
    for _ in range(TOP_K):
        m = jnp.max(cur, axis=-1, keepdims=True)
        idx = jnp.min(jnp.where(cur == m, lane, LANES), axis=-1, keepdims=True)
        sel = lane == idx
        cur = jnp.where(sel, 2.0 * NEG_BIG, cur)
        vals.append(m)
        idxs.append(idx)
        sels.append(sel)
    exps = [jnp.exp(vk - vals[0]) for vk in vals]
    den = exps[0] + exps[1] + exps[2] + exps[3]
    onehot = jnp.logical_or(jnp.logical_or(sels[0], sels[1]), jnp.logical_or(sels[2], sels[3]))
    onehot_bf = onehot.astype(BF16)
    rank_all = jnp.dot(tri_ref[...], onehot_bf, preferred_element_type=F32) + carry_ref[...]
    carry_ref[...] += jnp.sum(onehot.astype(F32), axis=0, keepdims=True)
    colcnt_ref[...] += _tn_dot(onehot_bf, jnp.ones((tm, LANES), BF16))
    info = jnp.zeros((tm, LANES), F32)
    for kk in range(TOP_K):
        rank_k = jnp.sum(jnp.where(sels[kk], rank_all, 0.0), axis=-1, keepdims=True)
        info = jnp.where(lane == kk, exps[kk] / den, info)
        info = jnp.where(lane == TOP_K + kk, idxs[kk].astype(F32), info)
        info = jnp.where(lane == 2 * TOP_K + kk, rank_k, info)
    info_ref[...] = info
    cnt_ref[...] = colcnt_ref[...]


def _merge_route(hp, on, ap, sgb, w_rnn_bf16, w_o_bf16, g_ffn, w_router_pad, b_router_pad):
    tp, d = hp.shape
    tm = ROW_TILE
    row = pl.BlockSpec((tm, d), lambda i: (i, 0))
    vec = _const_spec((1, d))
    return pl.pallas_call(
        _merge_route_kernel,
        grid=(tp // tm,),
        in_specs=[row, row, row, row, _const_spec((d, d)), _const_spec((d, d)), vec,
                  _const_spec((d, LANES)), _const_spec((1, LANES))],
        out_specs=[row, pl.BlockSpec((tm * PACK_ROWS, LANES), lambda i: (i, 0)),
                   pl.BlockSpec((tm, LANES), lambda i: (i, 0)), _const_spec((LANES, LANES))],
        out_shape=[jax.ShapeDtypeStruct((tp, d), F32), jax.ShapeDtypeStruct((tp * PACK_ROWS, LANES), jnp.uint32),
                   jax.ShapeDtypeStruct((tp, LANES), F32), jax.ShapeDtypeStruct((LANES, LANES), F32)],
        scratch_shapes=[pltpu.VMEM((tm, tm), BF16), pltpu.VMEM((1, LANES), F32),
                        pltpu.VMEM((LANES, LANES), F32)],
        compiler_params=_cparams(("arbitrary",)),
        name="merge_route",
    )(hp, on, ap, sgb, w_rnn_bf16, w_o_bf16, g_ffn, w_router_pad, b_router_pad)


def _slots_kernel(info_ref, cnt_ref, dest_ref, te_ref):
    tm = info_ref.shape[0]
    n_tile_lanes = te_ref.shape[1]
    r_i = lax.broadcasted_iota(jnp.int32, (LANES, LANES), 0)
    c_i = lax.broadcasted_iota(jnp.int32, (LANES, LANES), 1)
    cnt = cnt_ref[...]
    tiles = jnp.floor((cnt + (EXPERT_TILE - 1)) / EXPERT_TILE)
    pend_col = jnp.dot((c_i <= r_i).astype(F32), tiles, preferred_element_type=F32,
                       precision=lax.Precision.HIGHEST)
    pstart_row = _tn_dot_f32(tiles, (r_i < c_i).astype(F32))
    info = info_ref[...]
    lane = lax.broadcasted_iota(jnp.int32, (tm, LANES), 1)
    dest = jnp.zeros((tm, LANES), F32)
    for kk in range(TOP_K):
        idx_k = info[:, TOP_K + kk:TOP_K + kk + 1].astype(jnp.int32)
        start_k = jnp.sum(jnp.where(lane == idx_k, pstart_row[0:1, :], 0.0), axis=-1, keepdims=True)
        slot_k = start_k * EXPERT_TILE + info[:, 2 * TOP_K + kk:2 * TOP_K + kk + 1]
        dest = jnp.where(lane == kk, slot_k, dest)
    dest_ref[...] = dest.astype(jnp.int32)
    tile_id = lax.broadcasted_iota(jnp.int32, (LANES, n_tile_lanes), 1).astype(F32)
    exp_id = lax.broadcasted_iota(jnp.int32, (LANES, n_tile_lanes), 0)
    pend_wide = jnp.concatenate([pend_col] * (n_tile_lanes // LANES), axis=1)
    below = jnp.logical_and(pend_wide <= tile_id, exp_id < N_EXPERTS)
    te = jnp.minimum(jnp.sum(below.astype(F32), axis=0, keepdims=True), N_EXPERTS - 1.0)
    n_active = pend_col[N_EXPERTS - 1:N_EXPERTS, 0:1]
    lane_t = lax.broadcasted_iota(jnp.int32, (8, n_tile_lanes), 1)
    sub_t = lax.broadcasted_iota(jnp.int32, (8, n_tile_lanes), 0)
    pend_row = _tn_dot_f32(tiles, (r_i <= c_i).astype(F32))[0:1, :]
    pend_wide_row = jnp.concatenate([pend_row] + [jnp.zeros_like(pend_row)] * (n_tile_lanes // LANES - 1), axis=1)
    out = jnp.where(sub_t == 0, jnp.broadcast_to(te, (8, n_tile_lanes)),
                    jnp.where(sub_t == 1, jnp.broadcast_to(n_active, (8, n_tile_lanes)),
                              jnp.broadcast_to(pend_wide_row, (8, n_tile_lanes))))
    del lane_t
    te_ref[...] = out.astype(jnp.int32)


def _tn_dot_f32(a, b):
    return lax.dot_general(a, b, (((0,), (0,)), ((), ())), preferred_element_type=F32,
                           precision=lax.Precision.HIGHEST)


def _slots(info, cnt, n_tiles_max):
    tp = info.shape[0]
    tm = ROW_TILE
    n_tile_lanes = -(-n_tiles_max // LANES) * LANES
    return pl.pallas_call(
        _slots_kernel,
        grid=(tp // tm,),
        in_specs=[pl.BlockSpec((tm, LANES), lambda i: (i, 0)), _const_spec((LANES, LANES))],
        out_specs=[pl.BlockSpec((tm, LANES), lambda i: (i, 0)), _const_spec((8, n_tile_lanes))],
        out_shape=[jax.ShapeDtypeStruct((tp, LANES), jnp.int32),
                   jax.ShapeDtypeStruct((8, n_tile_lanes), jnp.int32)],
        compiler_params=_cparams(("arbitrary",)),
        name="slots",
    )(info, cnt)


def _dest_slot(dest_smem, slot):
    return dest_smem.at[pl.ds(pl.multiple_of(slot * (ROUTE_TILE * TOP_K), ROUTE_TILE * TOP_K), ROUTE_TILE * TOP_K)]


def _fetch_dest(dest_hbm, dest_smem, dsem, step, slot):
    return pltpu.make_async_copy(dest_hbm.at[step], _dest_slot(dest_smem, slot), dsem.at[slot])


def _slab(ref, token):
    return ref.at[pl.ds(token * PACK_ROWS, PACK_ROWS), :]


def _dispatch_kernel(pend_ref, dest_hbm, u2_hbm, xs_ref, dest_smem, zero_ref, rows_ref, dsem, sem, zsem, rsem):
    i = pl.program_id(0)
    n = pl.num_programs(0)
    slot = i % 2
    tile_rows = EXPERT_TILE * PACK_ROWS

    def last_tile_zero(e):
        first = pend_ref[e - 1] if e else 0
        dst = xs_ref.at[pl.ds((pend_ref[e] - 1) * tile_rows, tile_rows), :]
        return pend_ref[e] > first, pltpu.make_async_copy(zero_ref, dst, zsem)

    def stage(step):
        s = step % STAGE_SLOTS
        src = u2_hbm.at[pl.ds(step * ROUTE_TILE * PACK_ROWS, ROUTE_TILE * PACK_ROWS), :]
        return pltpu.make_async_copy(src, rows_ref.at[s], rsem.at[s])

    @pl.when(i == 0)
    def _():
        _fetch_dest(dest_hbm, dest_smem, dsem, 0, 0).start()
        stage(0).start()
        zero_ref[...] = jnp.zeros_like(zero_ref)
        for e in range(N_EXPERTS):
            has_tiles, copy = last_tile_zero(e)
            pl.when(has_tiles)(copy.start)
        n_tiles = xs_ref.shape[0] // tile_rows

        def spare_tile(t):
            return pltpu.make_async_copy(zero_ref, xs_ref.at[pl.ds(t * tile_rows, tile_rows), :], zsem)

        lax.fori_loop(pend_ref[N_EXPERTS - 1], n_tiles, lambda t, c: (spare_tile(t).start(), c)[1], 0)
        lax.fori_loop(pend_ref[N_EXPERTS - 1], n_tiles, lambda t, c: (spare_tile(t).wait(), c)[1], 0)
        for e in range(N_EXPERTS):
            has_tiles, copy = last_tile_zero(e)
            pl.when(has_tiles)(copy.wait)

    _fetch_dest(dest_hbm, dest_smem, dsem, i, slot).wait()
    stage(i).wait()

    @pl.when(i + 1 < n)
    def _():
        _fetch_dest(dest_hbm, dest_smem, dsem, i + 1, 1 - slot).start()
        stage(i + 1).start()

    rows = rows_ref.at[i % STAGE_SLOTS]

    dest_base = slot * (ROUTE_TILE * TOP_K)

    def issue(t, carry):
        for kk in range(TOP_K):
            dst = _slab(xs_ref, dest_smem[dest_base + t * TOP_K + kk])
            pltpu.make_async_copy(_slab(rows, t), dst, sem.at[i % STAGE_SLOTS]).start(priority=kk % 2)
        return carry

    lax.fori_loop(0, ROUTE_TILE, issue, 0, unroll=2)

    def drain(step):
        s = step % STAGE_SLOTS
        for _ in range(TOP_K):
            pltpu.make_async_copy(rows_ref.at[s], xs_ref.at[pl.ds(0, ROUTE_TILE * PACK_ROWS), :], sem.at[s]).wait()

    pl.when(i > 0)(lambda: drain(i - 1))
    pl.when(i == n - 1)(lambda: drain(i))


def _dispatch(pend, dest2, u2p, n_slots):
    n_steps = dest2.shape[0]
    grid_spec = pltpu.PrefetchScalarGridSpec(
        num_scalar_prefetch=1,
        grid=(n_steps,),
        in_specs=[pl.BlockSpec(memory_space=pl.ANY), pl.BlockSpec(memory_space=pl.ANY)],
        out_specs=pl.BlockSpec(memory_space=pl.ANY),
        scratch_shapes=[pltpu.SMEM((2 * ROUTE_TILE * TOP_K,), jnp.int32),
                        pltpu.VMEM((EXPERT_TILE * PACK_ROWS, LANES), jnp.uint32),
                        pltpu.VMEM((STAGE_SLOTS, ROUTE_TILE * PACK_ROWS, LANES), jnp.uint32),
                        pltpu.SemaphoreType.DMA((2,)), pltpu.SemaphoreType.DMA((STAGE_SLOTS,)),
                        pltpu.SemaphoreType.DMA, pltpu.SemaphoreType.DMA((STAGE_SLOTS,))],
    )
    return pl.pallas_call(
        _dispatch_kernel,
        grid_spec=grid_spec,
        out_shape=jax.ShapeDtypeStruct((n_slots * PACK_ROWS, LANES), jnp.uint32),
        compiler_params=_cparams(("arbitrary",)),
        name="dispatch",
    )(pend, dest2, u2p)


def _expert_kernel(te_ref, nact_ref, x_ref, wgu_ref, bgu_ref, wdn_ref, bdn_ref, y_ref):
    del te_ref

    @pl.when(pl.program_id(0) < nact_ref[0])
    def _():
        x = _load_packed(x_ref, EXPERT_TILE).astype(BF16)
        gu = jnp.dot(x, wgu_ref[0].astype(BF16), preferred_element_type=F32) + bgu_ref[0]
        gate = jnp.minimum(gu[:, :D_FF], SWIGLU_LIMIT)
        up = jnp.clip(gu[:, D_FF:], -SWIGLU_LIMIT, SWIGLU_LIMIT)
        hdn = (up + 1.0) * (gate * _sigmoid(SWIGLU_ALPHA * gate))
        y = jnp.dot(hdn.astype(BF16), wdn_ref[0].astype(BF16), preferred_element_type=F32) + bdn_ref[0]
        _store_packed(y_ref, y)

    @pl.when(pl.program_id(0) >= nact_ref[0])
    def _():
        y_ref[...] = jnp.zeros_like(y_ref)


def _expert_mlp(te, nact, xs, w_gu, b_gu, w_dn, b_dn):
    d = D_MODEL
    tile_rows = EXPERT_TILE * PACK_ROWS
    n_tiles = xs.shape[0] // tile_rows

    def tile_map(i, te_ref, nact_ref):
        return (jnp.minimum(i, nact_ref[0] - 1), 0)

    def exp_map(i, te_ref, nact_ref):
        return (te_ref[jnp.minimum(i, nact_ref[0] - 1)], 0, 0)

    grid_spec = pltpu.PrefetchScalarGridSpec(
        num_scalar_prefetch=2,
        grid=(n_tiles,),
        in_specs=[pl.BlockSpec((tile_rows, LANES), tile_map),
                  pl.BlockSpec((1, d, 2 * D_FF), exp_map),
                  pl.BlockSpec((1, 1, 2 * D_FF), exp_map),
                  pl.BlockSpec((1, D_FF, d), exp_map),
                  pl.BlockSpec((1, 1, d), exp_map)],
        out_specs=pl.BlockSpec((tile_rows, LANES), lambda i, te_ref, nact_ref: (i, 0)),
    )
    return pl.pallas_call(
        _expert_kernel,
        grid_spec=grid_spec,
        out_shape=jax.ShapeDtypeStruct(xs.shape, jnp.uint32),
        compiler_params=_cparams(("arbitrary",)),
        name="expert_mlp",
    )(te, nact, xs, w_gu, b_gu, w_dn, b_dn)


def _combine_kernel(dest_hbm, h1_ref, info_ref, gfin_ref, ys_hbm, out_ref, dest_smem, ybuf, dsem, sem,
                    *, chunks_per_batch):
    j = pl.program_id(0)
    n = pl.num_programs(0)
    slot = j % 2

    def fetch(step):
        chunk = (step // chunks_per_batch) * (chunks_per_batch + 1) + step % chunks_per_batch + 1
        return pltpu.make_async_copy(dest_hbm.at[chunk], _dest_slot(dest_smem, step % 2), dsem.at[step % 2])

    def issue(step):
        s = step % 2
        dest_base = s * (ROUTE_TILE * TOP_K)

        def body(t, carry):
            for kk in range(TOP_K):
                src = _slab(ys_hbm, dest_smem[dest_base + t * TOP_K + kk])
                pltpu.make_async_copy(src, _slab(ybuf.at[s, kk], t), sem.at[s]).start(priority=kk % 2)
            return carry

        lax.fori_loop(0, ROUTE_TILE, body, 0, unroll=2)

    @pl.when(j == 0)
    def _():
        fetch(0).start()
        fetch(0).wait()
        issue(0)
        fetch(1).start()

    @pl.when(j + 1 < n)
    def _():
        fetch(j + 1).wait()
        issue(j + 1)

    @pl.when(j + 2 < n)
    def _():
        fetch(j + 2).start()

    for kk in range(TOP_K):
        pltpu.make_async_copy(ys_hbm.at[pl.ds(0, ROUTE_TILE * PACK_ROWS), :], ybuf.at[slot, kk], sem.at[slot]).wait()
    info = info_ref[...]
    h2 = h1_ref[...]
    for kk in range(TOP_K):
        h2 = h2 + info[:, kk:kk + 1] * _load_packed(ybuf, ROUTE_TILE, lead=(slot, kk))
    ms = jnp.mean(h2 * h2, axis=-1, keepdims=True)
    out_ref[...] = h2 * lax.rsqrt(ms + RMS_EPS) * gfin_ref[...]


def _combine(dest2, h1, info, g_final, ys, n_batch, seq):
    d = h1.shape[1]
    cpb = seq // ROUTE_TILE

    def padded_chunk(j):
        return ((j // cpb) * (cpb + 1) + j % cpb + 1, 0)

    return pl.pallas_call(
        functools.partial(_combine_kernel, chunks_per_batch=cpb),
        grid=(n_batch * cpb,),
        in_specs=[pl.BlockSpec(memory_space=pl.ANY),
                  pl.BlockSpec((ROUTE_TILE, d), padded_chunk),
                  pl.BlockSpec((ROUTE_TILE, LANES), padded_chunk),
                  pl.BlockSpec((1, d), lambda j: (0, 0)),
                  pl.BlockSpec(memory_space=pl.ANY)],
        out_specs=pl.BlockSpec((ROUTE_TILE, d), lambda j: (j, 0)),
        out_shape=jax.ShapeDtypeStruct((n_batch * seq, d), F32),
        scratch_shapes=[pltpu.SMEM((2 * ROUTE_TILE * TOP_K,), jnp.int32),
                        pltpu.VMEM((2, TOP_K, ROUTE_TILE * PACK_ROWS, LANES), jnp.uint32),
                        pltpu.SemaphoreType.DMA((2,)), pltpu.SemaphoreType.DMA((2,))],
        compiler_params=_cparams(("arbitrary",)),
        name="combine",
    )(dest2, h1, info, g_final, ys)


def kernel(x, meta_tokens, lb_logits, g_mix, w_in, w_dw, b_dw, ln_g, ln_b, w_conv_out, b_conv_out,
           g_onorm, w_rnn_out, w_o, g_ffn, w_router, b_router, w_gate_up, b_gate_up, w_down, b_down,
           g_final):
    n_batch, seq, d = x.shape
    assert d == D_MODEL and w_in.shape[0] == 1, "single-layer block with D_MODEL features"
    assert CHUNK == ROUTE_TILE and seq % CHUNK == 0
    seq_pad = CHUNK + seq
    tp = n_batch * seq_pad
    assert tp % ROW_TILE == 0

    meta = jnp.broadcast_to(meta_tokens.astype(x.dtype)[None], (n_batch, N_META, d))
    hp = jnp.concatenate([jnp.zeros((n_batch, CHUNK_PAD, d), x.dtype), meta, x], axis=1).reshape(tp, d)
    vec = lambda a: a.reshape(1, -1).astype(F32)

    aglu, q, k, logf, v, og, sga, sgb = _in_proj(hp, vec(g_mix[0]), lb_logits.astype(F32), w_in[0].astype(BF16))
    ap = _conv_branch(aglu, w_dw[0].astype(F32), vec(b_dw[0]), vec(ln_g[0]), vec(ln_b[0]),
                      w_conv_out[0].astype(BF16), vec(b_conv_out[0]), sga)
    on = _hgrn2_scan(q, k, v, logf, og, vec(g_onorm[0]), n_batch)
    w_router_pad = jnp.pad(w_router[0].astype(F32), ((0, 0), (0, LANES - N_EXPERTS)))
    b_router_pad = jnp.pad(vec(b_router[0]), ((0, 0), (0, LANES - N_EXPERTS)))
    h1, u2p, info, cnt = _merge_route(hp, on, ap, sgb, w_rnn_out[0].astype(BF16), w_o[0].astype(BF16),
                                      vec(g_ffn[0]), w_router_pad, b_router_pad)

    n_tiles_max = -(-(tp * TOP_K + N_EXPERTS * (EXPERT_TILE - 1)) // EXPERT_TILE)
    dest, te = _slots(info, cnt, n_tiles_max)
    dest2 = dest[:, :TOP_K].reshape(tp // ROUTE_TILE, ROUTE_TILE * TOP_K)
    xs = _dispatch(te[2, :N_EXPERTS], dest2, u2p, n_tiles_max * EXPERT_TILE)
    ys = _expert_mlp(te[0, :n_tiles_max], te[1, :1], xs, w_gate_up[0],
                     b_gate_up[0].reshape(N_EXPERTS, 1, -1).astype(F32), w_down[0],
                     b_down[0].reshape(N_EXPERTS, 1, -1).astype(F32))
    out = _combine(dest2, h1, info, vec(g_final), ys, n_batch, seq)
    return out.reshape(n_batch, seq, d)
```

```python
import functools

import jax
import jax.numpy as jnp
from jax import lax
from jax.experimental import pallas as pl
from jax.experimental.pallas import tpu as pltpu

F32 = jnp.float32
BF16 = jnp.bfloat16

D_MODEL = 1024
N_META = 16
CHUNK = 128
CHUNK_PAD = CHUNK - N_META
CONV_WIDTH = 31
HEAD_DIM = 128
N_HEADS = D_MODEL // HEAD_DIM
N_EXPERTS = 32
TOP_K = 4
D_FF = D_MODEL
SWIGLU_LIMIT = 7.0
SWIGLU_ALPHA = 1.702
RMS_EPS = 1e-6
LN_EPS = 1e-5
N_IN_GROUPS = 8

LANES = 128
SUBLANES = 8
SUB_BLOCK = 32
N_SUB = CHUNK // SUB_BLOCK
HALO = 32
ROW_TILE = 640
IN_TILE = 320
CONV_ROWS = 64
EXPERT_TILE = 512
ROUTE_TILE = 128
STAGE_SLOTS = 3
NEG_BIG = -1e30
VMEM_LIMIT = 56 * 1024 * 1024


def _sigmoid(x):
    return 1.0 / (1.0 + jnp.exp(-x))


def _cparams(sem):
    return pltpu.CompilerParams(dimension_semantics=sem, vmem_limit_bytes=VMEM_LIMIT)


def _const_spec(shape):
    nd = len(shape)
    return pl.BlockSpec(shape, lambda *_: (0,) * nd)


PACK_ROWS = D_MODEL // 2 // 128
HIGH_HALF = 0xFFFF0000


def _store_packed(ref, x):
    rows = x.shape[0]
    half = D_MODEL // 2
    lo = lax.bitcast_convert_type(x[:, :half].astype(BF16).astype(F32), jnp.uint32)
    hi = lax.bitcast_convert_type(x[:, half:].astype(BF16).astype(F32), jnp.uint32)
    words = (lo >> 16) | (hi & jnp.uint32(HIGH_HALF))
    for j in range(PACK_ROWS):
        ref[pl.ds(j, rows, stride=PACK_ROWS), :] = words[:, j * LANES:(j + 1) * LANES]


def _load_packed(ref, rows, lead=()):
    lo, hi = [], []
    for j in range(PACK_ROWS):
        w = ref[lead + (pl.ds(j, rows, stride=PACK_ROWS), slice(None))]
        lo.append(lax.bitcast_convert_type(w << 16, F32))
        hi.append(lax.bitcast_convert_type(w & jnp.uint32(HIGH_HALF), F32))
    return jnp.concatenate(lo + hi, axis=1)


def _in_proj_kernel(h_ref, g_ref, lbl_ref, w_ref, aglu_ref, q_ref, k_ref, logf_ref, v_ref,
                    og_ref, sga_ref, sgb_ref):
    d = D_MODEL
    h = h_ref[...]
    ms = jnp.mean(h * h, axis=-1, keepdims=True)
    u = (h * lax.rsqrt(ms + RMS_EPS) * g_ref[...]).astype(BF16)

    def proj(j):
        return jnp.dot(u, w_ref[:, j * d:(j + 1) * d], preferred_element_type=F32)

    aglu_ref[...] = (proj(0) * _sigmoid(proj(1))).astype(aglu_ref.dtype)
    zq = proj(2)
    q_ref[...] = (zq * _sigmoid(zq)).astype(q_ref.dtype)
    lbl = lbl_ref[...]
    e = jnp.exp(lbl - jnp.max(lbl, axis=0, keepdims=True))
    lb = e[0:1, :] / jnp.sum(e, axis=0, keepdims=True)
    s = _sigmoid(proj(3))
    logf_ref[...] = jnp.log(lb + (1.0 - lb) * s)
    k_ref[...] = ((1.0 - lb) * (1.0 - s)).astype(k_ref.dtype)
    v_ref[...] = proj(4).astype(v_ref.dtype)
    zg = proj(5)
    og_ref[...] = (zg * _sigmoid(zg)).astype(og_ref.dtype)
    sga_ref[...] = _sigmoid(proj(6)).astype(sga_ref.dtype)
    sgb_ref[...] = _sigmoid(proj(7)).astype(sgb_ref.dtype)


def _in_proj(hp, g_mix, lb_logits, w_in_bf16):
    tp, d = hp.shape
    tm = IN_TILE
    row = pl.BlockSpec((tm, d), lambda i: (i, 0))
    out_dtypes = [BF16, BF16, BF16, F32, BF16, BF16, BF16, BF16]
    return pl.pallas_call(
        _in_proj_kernel,
        grid=(tp // tm,),
        in_specs=[row, _const_spec((1, d)), _const_spec(lb_logits.shape),
                  pl.BlockSpec(w_in_bf16.shape, lambda i: (0, 0), pipeline_mode=pl.Buffered(1))],
        out_specs=[row] * 8,
        out_shape=[jax.ShapeDtypeStruct((tp, d), dt) for dt in out_dtypes],
        compiler_params=_cparams(("parallel",)),
        name="in_proj",
    )(hp, g_mix, lb_logits, w_in_bf16)


def _conv_kernel(halo_ref, cur_ref, wdw_ref, bdw_ref, lng_ref, lnb_ref, wout_ref, bout_ref, sga_ref,
                 out_ref, win_ref, acc_ref, shift_ref):
    tm = cur_ref.shape[0]
    win_ref[0:HALO, :] = halo_ref[...].astype(F32)
    win_ref[HALO:, :] = cur_ref[...].astype(F32)
    first_tap = HALO - (CONV_WIDTH - 1)
    for c in range(D_MODEL // LANES):
        lanes = slice(c * LANES, (c + 1) * LANES)
        for s in range(SUBLANES):
            n_rows = tm + HALO - (SUBLANES if s else 0)
            shift_ref[s, 0:n_rows, :] = win_ref[pl.ds(s, n_rows), lanes]
        w_c = wdw_ref[:, lanes]
        bias = jnp.broadcast_to(bdw_ref[:, lanes], (CONV_ROWS, LANES))

        def chunk(r, carry, lanes=lanes, w_c=w_c, bias=bias):
            r0 = pl.multiple_of(r * CONV_ROWS, CONV_ROWS)
            acc = bias
            for j in range(CONV_WIDTH):
                off = first_tap + j
                rows = pl.ds(r0 + off - off % SUBLANES, CONV_ROWS)
                acc = acc + w_c[j:j + 1, :] * shift_ref[off % SUBLANES, rows, :]
            acc_ref[pl.ds(r0, CONV_ROWS), lanes] = acc
            return carry

        lax.fori_loop(0, tm // CONV_ROWS, chunk, 0)
    a = acc_ref[...]
    mu = jnp.mean(a, axis=-1, keepdims=True)
    ac = a - mu
    var = jnp.mean(ac * ac, axis=-1, keepdims=True)
    y = ac * lax.rsqrt(var + LN_EPS) * lng_ref[...] + lnb_ref[...]
    y = y * _sigmoid(y)
    o = jnp.dot(y.astype(BF16), wout_ref[...], preferred_element_type=F32) + bout_ref[...]
    out_ref[...] = (sga_ref[...].astype(F32) * o).astype(out_ref.dtype)


def _conv_branch(aglu, w_dw, b_dw, ln_g, ln_b, w_out_bf16, b_out, sga):
    tp, d = aglu.shape
    tm = ROW_TILE
    per = tm // HALO
    row = pl.BlockSpec((tm, d), lambda i: (i, 0))
    halo = pl.BlockSpec((HALO, d), lambda i: (jnp.maximum(i * per - 1, 0), 0))
    vec = _const_spec((1, d))
    return pl.pallas_call(
        _conv_kernel,
        grid=(tp // tm,),
        in_specs=[halo, row, _const_spec(w_dw.shape), vec, vec, vec, _const_spec((d, d)), vec, row],
        out_specs=row,
        out_shape=jax.ShapeDtypeStruct((tp, d), BF16),
        scratch_shapes=[pltpu.VMEM((tm + HALO, d), F32), pltpu.VMEM((tm, d), F32),
                        pltpu.VMEM((SUBLANES, tm + HALO, LANES), F32)],
        compiler_params=_cparams(("parallel",)),
        name="conv_branch",
    )(aglu, aglu, w_dw, b_dw, ln_g, ln_b, w_out_bf16, b_out, sga)


def _split_bf16(x):
    hi = x.astype(BF16)
    return hi, (x - hi.astype(F32)).astype(BF16)


def _nt_dot(a, b):
    return lax.dot_general(a, b, (((1,), (1,)), ((), ())), preferred_element_type=F32)


def _tn_dot(a, b):
    return lax.dot_general(a, b, (((0,), (0,)), ((), ())), preferred_element_type=F32)


def _scan_kernel(q_ref, k_ref, v_ref, lf_ref, og_ref, gon_ref, o_ref, st_ref):
    @pl.when(pl.program_id(1) == 0)
    def _():
        st_ref[...] = jnp.zeros_like(st_ref)

    c = CHUNK
    row = lax.broadcasted_iota(jnp.int32, (c, c), 0)
    col = lax.broadcasted_iota(jnp.int32, (c, c), 1)
    causal = col <= row
    rblk = row // SUB_BLOCK
    cblk = col // SUB_BLOCK
    diag_mask = jnp.logical_and(causal, rblk == cblk)
    bcum = jnp.dot(causal.astype(F32), lf_ref[...], preferred_element_type=F32,
                   precision=lax.Precision.HIGHEST)

    def bcast_rows(rows):
        return jnp.concatenate([jnp.broadcast_to(r, (SUB_BLOCK, HEAD_DIM)) for r in rows], axis=0)

    for h in range(N_HEADS):
        hs = slice(h * HEAD_DIM, (h + 1) * HEAD_DIM)
        b = bcum[:, hs]
        q = q_ref[:, hs].astype(F32)
        k = k_ref[:, hs].astype(F32)
        v = v_ref[:, hs]
        ends = [b[i * SUB_BLOCK + SUB_BLOCK - 1:i * SUB_BLOCK + SUB_BLOCK, :] for i in range(N_SUB)]
        mids = [b[i * SUB_BLOCK + SUB_BLOCK // 2 - 1:i * SUB_BLOCK + SUB_BLOCK // 2, :] for i in range(N_SUB)]
        mid_full = bcast_rows(mids)
        end_full = bcast_rows(ends)
        b_last = ends[-1]
        qm = (q * jnp.exp(b - mid_full)).astype(BF16)
        km = (k * jnp.exp(mid_full - b)).astype(BF16)
        scores = jnp.where(diag_mask, _nt_dot(qm, km), 0.0)
        ke = k * jnp.exp(end_full - b)
        q_parts, k_parts = [], []
        for j in range(N_SUB - 1):
            arg = jnp.where(rblk > j, b - ends[j], NEG_BIG)
            q_parts.append((q * jnp.exp(arg)).astype(BF16))
            k_parts.append(jnp.where(rblk == j, ke, 0.0).astype(BF16))
        scores = scores + _nt_dot(jnp.concatenate(q_parts, axis=1), jnp.concatenate(k_parts, axis=1))
        st = st_ref[h]
        q_in = (q * jnp.exp(b)).astype(BF16)
        o = _nt_dot(q_in, st.astype(BF16)) + jnp.dot(scores.astype(BF16), v, preferred_element_type=F32)
        k_out = (k * jnp.exp(b_last - b)).astype(BF16)
        st_ref[h] = st * jnp.exp(b_last) + _tn_dot(v, k_out)
        ms = jnp.mean(o * o, axis=-1, keepdims=True)
        on = o * lax.rsqrt(ms + RMS_EPS) * gon_ref[:, hs]
        o_ref[:, hs] = (on * og_ref[:, hs].astype(F32)).astype(o_ref.dtype)


def _hgrn2_scan(q, k, v, logf, og, g_onorm, n_batch):
    tp, d = q.shape
    n_chunks = tp // n_batch // CHUNK
    blk = pl.BlockSpec((CHUNK, d), lambda b, c: (b * n_chunks + c, 0))
    return pl.pallas_call(
        _scan_kernel,
        grid=(n_batch, n_chunks),
        in_specs=[blk, blk, blk, blk, blk, pl.BlockSpec((1, d), lambda b, c: (0, 0))],
        out_specs=blk,
        out_shape=jax.ShapeDtypeStruct((tp, d), BF16),
        scratch_shapes=[pltpu.VMEM((N_HEADS, HEAD_DIM, HEAD_DIM), F32)],
        compiler_params=_cparams(("arbitrary", "arbitrary")),
        name="hgrn2_scan",
    )(q, k, v, logf, og, g_onorm)


def _merge_route_kernel(h_ref, on_ref, ap_ref, sgb_ref, wrnn_ref, wo_ref, gffn_ref, wr_ref, br_ref,
                        h1_ref, u2_ref, info_ref, cnt_ref, tri_ref, carry_ref, colcnt_ref):
    tm = h_ref.shape[0]

    @pl.when(pl.program_id(0) == 0)
    def _():
        r_i = lax.broadcasted_iota(jnp.int32, (tm, tm), 0)
        c_i = lax.broadcasted_iota(jnp.int32, (tm, tm), 1)
        tri_ref[...] = (c_i < r_i).astype(BF16)
        carry_ref[...] = jnp.zeros_like(carry_ref)
        colcnt_ref[...] = jnp.zeros_like(colcnt_ref)

    r = jnp.dot(on_ref[...], wrnn_ref[...], preferred_element_type=F32)
    y = ap_ref[...].astype(F32) + sgb_ref[...].astype(F32) * r
    h1 = h_ref[...] + jnp.dot(y.astype(BF16), wo_ref[...], preferred_element_type=F32)
    h1_ref[...] = h1
    ms = jnp.mean(h1 * h1, axis=-1, keepdims=True)
    u2 = h1 * lax.rsqrt(ms + RMS_EPS) * gffn_ref[...]
    _store_packed(u2_ref, u2)
    u_hi, u_lo = _split_bf16(u2)
    w_hi, w_lo = _split_bf16(wr_ref[...])
    logits = (jnp.dot(u_hi, w_hi, preferred_element_type=F32)
              + (jnp.dot(u_lo, w_hi, preferred_element_type=F32) + jnp.dot(u_hi, w_lo, preferred_element_type=F32))
              + br_ref[...])
    lane = lax.broadcasted_iota(jnp.int32, (tm, LANES), 1)
    cur = jnp.where(lane < N_EXPERTS, logits, NEG_BIG)
    vals, idxs, sels = [], [], []
    for _ in range(TOP_K):
        m = jnp.max(cur, axis=-1, keepdims=True)
        idx = jnp.min(jnp.where(cur == m, lane, LANES), axis=-1, keepdims=True)
        sel = lane == idx
        cur = jnp.where(sel, 2.0 * NEG_BIG, cur)
        vals.append(m)
        idxs.append(idx)
        sels.append(sel)
    exps = [jnp.exp(vk - vals[0]) for vk in vals]
    den = exps[0] + exps[1] + exps[2] + exps[3]
    onehot = jnp.logical_or(jnp.logical_or(sels[0], sels[1]), jnp.logical_or(sels[2], sels[3]))
    onehot_bf = onehot.astype(BF16)
    rank_all = jnp.dot(tri_ref[...], onehot_bf, preferred_element_type=F32) + carry_ref[...]
    carry_ref[...] += jnp.sum(onehot.astype(F32), axis=0, keepdims=True)
    colcnt_ref[...] += _tn_dot(onehot_bf, jnp.ones((tm, LANES), BF16))
    info = jnp.zeros((tm, LANES), F32)
    for kk in range(TOP_K):
        rank_k = jnp.sum(jnp.where(sels[kk], rank_all, 0.0), axis=-1, keepdims=True)
        info = jnp.where(lane == kk, exps[kk] / den, info)
        info = jnp.where(lane == TOP_K + kk, idxs[kk].astype(F32), info)
        info = jnp.where(lane == 2 * TOP_K + kk, rank_k, info)
    info_ref[...] = info
    cnt_ref[...] = colcnt_ref[...]


def _merge_route(hp, on, ap, sgb, w_rnn_bf16, w_o_bf16, g_ffn, w_router_pad, b_router_pad):
    tp, d = hp.shape
    tm = ROW_TILE
    row = pl.BlockSpec((tm, d), lambda i: (i, 0))
    vec = _const_spec((1, d))
    return pl.pallas_call(
        _merge_route_kernel,
        grid=(tp // tm,),
        in_specs=[row, row, row, row, _const_spec((d, d)), _const_spec((d, d)), vec,
                  _const_spec((d, LANES)), _const_spec((1, LANES))],
        out_specs=[row, pl.BlockSpec((tm * PACK_ROWS, LANES), lambda i: (i, 0)),
                   pl.BlockSpec((tm, LANES), lambda i: (i, 0)), _const_spec((LANES, LANES))],
        out_shape=[jax.ShapeDtypeStruct((tp, d), F32), jax.ShapeDtypeStruct((tp * PACK_ROWS, LANES), jnp.uint32),
                   jax.ShapeDtypeStruct((tp, LANES), F32), jax.ShapeDtypeStruct((LANES, LANES), F32)],
        scratch_shapes=[pltpu.VMEM((tm, tm), BF16), pltpu.VMEM((1, LANES), F32),
                        pltpu.VMEM((LANES, LANES), F32)],
        compiler_params=_cparams(("arbitrary",)),
        name="merge_route",
    )(hp, on, ap, sgb, w_rnn_bf16, w_o_bf16, g_ffn, w_router_pad, b_router_pad)


def _slots_kernel(info_ref, cnt_ref, dest_ref, te_ref, pstart_ref):
    tm = info_ref.shape[0]
    n_tile_lanes = te_ref.shape[1]

    @pl.when(pl.program_id(0) == 0)
    def _():
        r_i = lax.broadcasted_iota(jnp.int32, (LANES, LANES), 0)
        c_i = lax.broadcasted_iota(jnp.int32, (LANES, LANES), 1)
        cnt = cnt_ref[...]
        tiles = jnp.floor((cnt + (EXPERT_TILE - 1)) / EXPERT_TILE)
        pend_col = jnp.dot((c_i <= r_i).astype(F32), tiles, preferred_element_type=F32,
                           precision=lax.Precision.HIGHEST)
        pstart_ref[...] = _tn_dot_f32(tiles, (r_i < c_i).astype(F32))[0:SUBLANES, :]
        pend_row = _tn_dot_f32(tiles, (r_i <= c_i).astype(F32))[0:1, :]
        tile_id = lax.broadcasted_iota(jnp.int32, (LANES, n_tile_lanes), 1).astype(F32)
        exp_id = lax.broadcasted_iota(jnp.int32, (LANES, n_tile_lanes), 0)
        pend_wide = jnp.concatenate([pend_col] * (n_tile_lanes // LANES), axis=1)
        below = jnp.logical_and(pend_wide <= tile_id, exp_id < N_EXPERTS)
        te = jnp.minimum(jnp.sum(below.astype(F32), axis=0, keepdims=True), N_EXPERTS - 1.0)
        n_active = pend_col[N_EXPERTS - 1:N_EXPERTS, 0:1]
        sub_t = lax.broadcasted_iota(jnp.int32, (SUBLANES, n_tile_lanes), 0)
        pend_wide_row = jnp.concatenate([pend_row] + [jnp.zeros_like(pend_row)] * (n_tile_lanes // LANES - 1), axis=1)
        out = jnp.where(sub_t == 0, jnp.broadcast_to(te, (SUBLANES, n_tile_lanes)),
                        jnp.where(sub_t == 1, jnp.broadcast_to(n_active, (SUBLANES, n_tile_lanes)),
                                  jnp.broadcast_to(pend_wide_row, (SUBLANES, n_tile_lanes))))
        te_ref[...] = out.astype(jnp.int32)

    info = info_ref[...]
    pstart_row = pstart_ref[0:1, :]
    lane = lax.broadcasted_iota(jnp.int32, (tm, LANES), 1)
    dest = jnp.zeros((tm, LANES), F32)
    for kk in range(TOP_K):
        idx_k = info[:, TOP_K + kk:TOP_K + kk + 1].astype(jnp.int32)
        start_k = jnp.sum(jnp.where(lane == idx_k, pstart_row, 0.0), axis=-1, keepdims=True)
        slot_k = start_k * EXPERT_TILE + info[:, 2 * TOP_K + kk:2 * TOP_K + kk + 1]
        dest = jnp.where(lane == kk, slot_k, dest)
    dest_ref[...] = dest.astype(jnp.int32)


def _tn_dot_f32(a, b):
    return lax.dot_general(a, b, (((0,), (0,)), ((), ())), preferred_element_type=F32,
                           precision=lax.Precision.HIGHEST)


def _slots(info, cnt, n_tiles_max):
    tp = info.shape[0]
    tm = ROW_TILE
    n_tile_lanes = -(-n_tiles_max // LANES) * LANES
    return pl.pallas_call(
        _slots_kernel,
        grid=(tp // tm,),
        in_specs=[pl.BlockSpec((tm, LANES), lambda i: (i, 0)), _const_spec((LANES, LANES))],
        out_specs=[pl.BlockSpec((tm, LANES), lambda i: (i, 0)), _const_spec((8, n_tile_lanes))],
        out_shape=[jax.ShapeDtypeStruct((tp, LANES), jnp.int32),
                   jax.ShapeDtypeStruct((8, n_tile_lanes), jnp.int32)],
        scratch_shapes=[pltpu.VMEM((SUBLANES, LANES), F32)],
        compiler_params=_cparams(("arbitrary",)),
        name="slots",
    )(info, cnt)


def _dest_slot(dest_smem, slot):
    return dest_smem.at[pl.ds(pl.multiple_of(slot * (ROUTE_TILE * TOP_K), ROUTE_TILE * TOP_K), ROUTE_TILE * TOP_K)]


def _fetch_dest(dest_hbm, dest_smem, dsem, step, slot):
    return pltpu.make_async_copy(dest_hbm.at[step], _dest_slot(dest_smem, slot), dsem.at[slot])


def _slab(ref, token):
    return ref.at[pl.ds(token * PACK_ROWS, PACK_ROWS), :]


def _dispatch_kernel(pend_ref, dest_hbm, u2_hbm, xs_ref, dest_smem, zero_ref, rows_ref, dsem, sem, zsem, rsem):
    i = pl.program_id(0)
    n = pl.num_programs(0)
    slot = i % 2
    tile_rows = EXPERT_TILE * PACK_ROWS

    def last_tile_zero(e):
        first = pend_ref[e - 1] if e else 0
        dst = xs_ref.at[pl.ds((pend_ref[e] - 1) * tile_rows, tile_rows), :]
        return pend_ref[e] > first, pltpu.make_async_copy(zero_ref, dst, zsem)

    def stage(step):
        s = step % STAGE_SLOTS
        src = u2_hbm.at[pl.ds(step * ROUTE_TILE * PACK_ROWS, ROUTE_TILE * PACK_ROWS), :]
        return pltpu.make_async_copy(src, rows_ref.at[s], rsem.at[s])

    @pl.when(i == 0)
    def _():
        _fetch_dest(dest_hbm, dest_smem, dsem, 0, 0).start()
        stage(0).start()
        zero_ref[...] = jnp.zeros_like(zero_ref)
        for e in range(N_EXPERTS):
            has_tiles, copy = last_tile_zero(e)
            pl.when(has_tiles)(copy.start)
        n_tiles = xs_ref.shape[0] // tile_rows

        def spare_tile(t):
            return pltpu.make_async_copy(zero_ref, xs_ref.at[pl.ds(t * tile_rows, tile_rows), :], zsem)

        lax.fori_loop(pend_ref[N_EXPERTS - 1], n_tiles, lambda t, c: (spare_tile(t).start(), c)[1], 0)
        lax.fori_loop(pend_ref[N_EXPERTS - 1], n_tiles, lambda t, c: (spare_tile(t).wait(), c)[1], 0)
        for e in range(N_EXPERTS):
            has_tiles, copy = last_tile_zero(e)
            pl.when(has_tiles)(copy.wait)

    _fetch_dest(dest_hbm, dest_smem, dsem, i, slot).wait()
    stage(i).wait()

    @pl.when(i + 1 < n)
    def _():
        _fetch_dest(dest_hbm, dest_smem, dsem, i + 1, 1 - slot).start()
        stage(i + 1).start()

    rows = rows_ref.at[i % STAGE_SLOTS]

    dest_base = slot * (ROUTE_TILE * TOP_K)

    def issue(t, carry):
        for kk in range(TOP_K):
            dst = _slab(xs_ref, dest_smem[dest_base + t * TOP_K + kk])
            pltpu.make_async_copy(_slab(rows, t), dst, sem.at[i % STAGE_SLOTS]).start(priority=kk % 2)
        return carry

    lax.fori_loop(0, ROUTE_TILE, issue, 0, unroll=4)

    def drain(step):
        s = step % STAGE_SLOTS
        for _ in range(TOP_K):
            pltpu.make_async_copy(rows_ref.at[s], xs_ref.at[pl.ds(0, ROUTE_TILE * PACK_ROWS), :], sem.at[s]).wait()

    pl.when(i > 0)(lambda: drain(i - 1))
    pl.when(i == n - 1)(lambda: drain(i))


def _dispatch(pend, dest2, u2p, n_slots):
    n_steps = dest2.shape[0]
    grid_spec = pltpu.PrefetchScalarGridSpec(
        num_scalar_prefetch=1,
        grid=(n_steps,),
        in_specs=[pl.BlockSpec(memory_space=pl.ANY), pl.BlockSpec(memory_space=pl.ANY)],
        out_specs=pl.BlockSpec(memory_space=pl.ANY),
        scratch_shapes=[pltpu.SMEM((2 * ROUTE_TILE * TOP_K,), jnp.int32),
                        pltpu.VMEM((EXPERT_TILE * PACK_ROWS, LANES), jnp.uint32),
                        pltpu.VMEM((STAGE_SLOTS, ROUTE_TILE * PACK_ROWS, LANES), jnp.uint32),
                        pltpu.SemaphoreType.DMA((2,)), pltpu.SemaphoreType.DMA((STAGE_SLOTS,)),
                        pltpu.SemaphoreType.DMA, pltpu.SemaphoreType.DMA((STAGE_SLOTS,))],
    )
    return pl.pallas_call(
        _dispatch_kernel,
        grid_spec=grid_spec,
        out_shape=jax.ShapeDtypeStruct((n_slots * PACK_ROWS, LANES), jnp.uint32),
        compiler_params=_cparams(("arbitrary",)),
        name="dispatch",
    )(pend, dest2, u2p)


def _expert_kernel(te_ref, nact_ref, x_ref, wgu_ref, bgu_ref, wdn_ref, bdn_ref, y_ref):
    del te_ref

    @pl.when(pl.program_id(0) < nact_ref[0])
    def _():
        x = _load_packed(x_ref, EXPERT_TILE).astype(BF16)
        gu = jnp.dot(x, wgu_ref[0].astype(BF16), preferred_element_type=F32) + bgu_ref[0]
        gate = jnp.minimum(gu[:, :D_FF], SWIGLU_LIMIT)
        up = jnp.clip(gu[:, D_FF:], -SWIGLU_LIMIT, SWIGLU_LIMIT)
        hdn = (up + 1.0) * (gate * _sigmoid(SWIGLU_ALPHA * gate))
        y = jnp.dot(hdn.astype(BF16), wdn_ref[0].astype(BF16), preferred_element_type=F32) + bdn_ref[0]
        _store_packed(y_ref, y)

    @pl.when(pl.program_id(0) >= nact_ref[0])
    def _():
        y_ref[...] = jnp.zeros_like(y_ref)


def _expert_mlp(te, nact, xs, w_gu, b_gu, w_dn, b_dn):
    d = D_MODEL
    tile_rows = EXPERT_TILE * PACK_ROWS
    n_tiles = xs.shape[0] // tile_rows

    def tile_map(i, te_ref, nact_ref):
        return (jnp.minimum(i, nact_ref[0] - 1), 0)

    def exp_map(i, te_ref, nact_ref):
        return (te_ref[jnp.minimum(i, nact_ref[0] - 1)], 0, 0)

    grid_spec = pltpu.PrefetchScalarGridSpec(
        num_scalar_prefetch=2,
        grid=(n_tiles,),
        in_specs=[pl.BlockSpec((tile_rows, LANES), tile_map),
                  pl.BlockSpec((1, d, 2 * D_FF), exp_map),
                  pl.BlockSpec((1, 1, 2 * D_FF), exp_map),
                  pl.BlockSpec((1, D_FF, d), exp_map),
                  pl.BlockSpec((1, 1, d), exp_map)],
        out_specs=pl.BlockSpec((tile_rows, LANES), lambda i, te_ref, nact_ref: (i, 0)),
    )
    return pl.pallas_call(
        _expert_kernel,
        grid_spec=grid_spec,
        out_shape=jax.ShapeDtypeStruct(xs.shape, jnp.uint32),
        compiler_params=_cparams(("arbitrary",)),
        name="expert_mlp",
    )(te, nact, xs, w_gu, b_gu, w_dn, b_dn)


def _combine_kernel(dest_hbm, h1_ref, info_ref, gfin_ref, ys_hbm, out_ref, dest_smem, ybuf, dsem, sem,
                    *, chunks_per_batch):
    j = pl.program_id(0)
    n = pl.num_programs(0)
    slot = j % 2

    def fetch(step):
        chunk = (step // chunks_per_batch) * (chunks_per_batch + 1) + step % chunks_per_batch + 1
        return pltpu.make_async_copy(dest_hbm.at[chunk], _dest_slot(dest_smem, step % 2), dsem.at[step % 2])

    def issue(step):
        s = step % 2
        dest_base = s * (ROUTE_TILE * TOP_K)

        def body(t, carry):
            for kk in range(TOP_K):
                src = _slab(ys_hbm, dest_smem[dest_base + t * TOP_K + kk])
                pltpu.make_async_copy(src, _slab(ybuf.at[s, kk], t), sem.at[s]).start(priority=kk % 2)
            return carry

        lax.fori_loop(0, ROUTE_TILE, body, 0, unroll=4)

    @pl.when(j == 0)
    def _():
        fetch(0).start()
        fetch(0).wait()
        issue(0)
        fetch(1).start()

    @pl.when(j + 1 < n)
    def _():
        fetch(j + 1).wait()
        issue(j + 1)

    @pl.when(j + 2 < n)
    def _():
        fetch(j + 2).start()

    for kk in range(TOP_K):
        pltpu.make_async_copy(ys_hbm.at[pl.ds(0, ROUTE_TILE * PACK_ROWS), :], ybuf.at[slot, kk], sem.at[slot]).wait()
    info = info_ref[...]
    h2 = h1_ref[...]
    for kk in range(TOP_K):
        h2 = h2 + info[:, kk:kk + 1] * _load_packed(ybuf, ROUTE_TILE, lead=(slot, kk))
    ms = jnp.mean(h2 * h2, axis=-1, keepdims=True)
    out_ref[...] = h2 * lax.rsqrt(ms + RMS_EPS) * gfin_ref[...]


def _combine(dest2, h1, info, g_final, ys, n_batch, seq):
    d = h1.shape[1]
    cpb = seq // ROUTE_TILE

    def padded_chunk(j):
        return ((j // cpb) * (cpb + 1) + j % cpb + 1, 0)

    return pl.pallas_call(
        functools.partial(_combine_kernel, chunks_per_batch=cpb),
        grid=(n_batch * cpb,),
        in_specs=[pl.BlockSpec(memory_space=pl.ANY),
                  pl.BlockSpec((ROUTE_TILE, d), padded_chunk),
                  pl.BlockSpec((ROUTE_TILE, LANES), padded_chunk),
                  pl.BlockSpec((1, d), lambda j: (0, 0)),
                  pl.BlockSpec(memory_space=pl.ANY)],
        out_specs=pl.BlockSpec((ROUTE_TILE, d), lambda j: (j, 0)),
        out_shape=jax.ShapeDtypeStruct((n_batch * seq, d), F32),
        scratch_shapes=[pltpu.SMEM((2 * ROUTE_TILE * TOP_K,), jnp.int32),
                        pltpu.VMEM((2, TOP_K, ROUTE_TILE * PACK_ROWS, LANES), jnp.uint32),
                        pltpu.SemaphoreType.DMA((2,)), pltpu.SemaphoreType.DMA((2,))],
        compiler_params=_cparams(("arbitrary",)),
        name="combine",
    )(dest2, h1, info, g_final, ys)


def kernel(x, meta_tokens, lb_logits, g_mix, w_in, w_dw, b_dw, ln_g, ln_b, w_conv_out, b_conv_out,
           g_onorm, w_rnn_out, w_o, g_ffn, w_router, b_router, w_gate_up, b_gate_up, w_down, b_down,
           g_final):
    n_batch, seq, d = x.shape
    assert d == D_MODEL and w_in.shape[0] == 1, "single-layer block with D_MODEL features"
    assert CHUNK == ROUTE_TILE and seq % CHUNK == 0
    seq_pad = CHUNK + seq
    tp = n_batch * seq_pad
    assert tp % ROW_TILE == 0

    meta = jnp.broadcast_to(meta_tokens.astype(x.dtype)[None], (n_batch, N_META, d))
    hp = jnp.concatenate([jnp.zeros((n_batch, CHUNK_PAD, d), x.dtype), meta, x], axis=1).reshape(tp, d)
    vec = lambda a: a.reshape(1, -1).astype(F32)

    aglu, q, k, logf, v, og, sga, sgb = _in_proj(hp, vec(g_mix[0]), lb_logits.astype(F32), w_in[0].astype(BF16))
    ap = _conv_branch(aglu, w_dw[0].astype(F32), vec(b_dw[0]), vec(ln_g[0]), vec(ln_b[0]),
                      w_conv_out[0].astype(BF16), vec(b_conv_out[0]), sga)
    on = _hgrn2_scan(q, k, v, logf, og, vec(g_onorm[0]), n_batch)
    w_router_pad = jnp.pad(w_router[0].astype(F32), ((0, 0), (0, LANES - N_EXPERTS)))
    b_router_pad = jnp.pad(vec(b_router[0]), ((0, 0), (0, LANES - N_EXPERTS)))
    h1, u2p, info, cnt = _merge_route(hp, on, ap, sgb, w_rnn_out[0].astype(BF16), w_o[0].astype(BF16),
                                      vec(g_ffn[0]), w_router_pad, b_router_pad)

    n_tiles_max = -(-(tp * TOP_K + N_EXPERTS * (EXPERT_TILE - 1)) // EXPERT_TILE)
    dest, te = _slots(info, cnt, n_tiles_max)
    dest2 = dest[:, :TOP_K].reshape(tp // ROUTE_TILE, ROUTE_TILE * TOP_K)
    xs = _dispatch(te[2, :N_EXPERTS], dest2, u2p, n_tiles_max * EXPERT_TILE)
    ys = _expert_mlp(te[0, :n_tiles_max], te[1, :1], xs, w_gate_up[0],
                     b_gate_up[0].reshape(N_EXPERTS, 1, -1).astype(F32), w_down[0],
                     b_down[0].reshape(N_EXPERTS, 1, -1).astype(F32))
    out = _combine(dest2, h1, info, vec(g_final), ys, n_batch, seq)
    return out.reshape(n_batch, seq, d)
```

```python
import functools

import jax
import jax.numpy as jnp
from jax import lax
from jax.experimental import pallas as pl
from jax.experimental.pallas import tpu as pltpu

F32 = jnp.float32
BF16 = jnp.bfloat16

D_MODEL = 1024
N_META = 16
CHUNK = 128
CHUNK_PAD = CHUNK - N_META
CONV_WIDTH = 31
HEAD_DIM = 128
N_HEADS = D_MODEL // HEAD_DIM
N_EXPERTS = 32
TOP_K = 4
D_FF = D_MODEL
SWIGLU_LIMIT = 7.0
SWIGLU_ALPHA = 1.702
RMS_EPS = 1e-6
LN_EPS = 1e-5
N_IN_GROUPS = 8

LANES = 128
SUBLANES = 8
SUB_BLOCK = 32
N_SUB = CHUNK // SUB_BLOCK
HALO = 32
ROW_TILE = 640
IN_TILE = 640
SLOTS_STEPS = 5
CONV_ROWS = 64
EXPERT_TILE = 512
ROUTE_TILE = 128
STAGE_SLOTS = 3
NEG_BIG = -1e30
VMEM_LIMIT = 56 * 1024 * 1024


def _sigmoid(x):
    return 1.0 / (1.0 + jnp.exp(-x))


def _cparams(sem):
    return pltpu.CompilerParams(dimension_semantics=sem, vmem_limit_bytes=VMEM_LIMIT)


def _const_spec(shape):
    nd = len(shape)
    return pl.BlockSpec(shape, lambda *_: (0,) * nd)


PACK_ROWS = D_MODEL // 2 // 128
HIGH_HALF = 0xFFFF0000


def _store_packed(ref, x):
    rows = x.shape[0]
    half = D_MODEL // 2
    lo = lax.bitcast_convert_type(x[:, :half].astype(BF16).astype(F32), jnp.uint32)
    hi = lax.bitcast_convert_type(x[:, half:].astype(BF16).astype(F32), jnp.uint32)
    words = (lo >> 16) | (hi & jnp.uint32(HIGH_HALF))
    for j in range(PACK_ROWS):
        ref[pl.ds(j, rows, stride=PACK_ROWS), :] = words[:, j * LANES:(j + 1) * LANES]


def _load_packed(ref, rows, lead=()):
    lo, hi = [], []
    for j in range(PACK_ROWS):
        w = ref[lead + (pl.ds(j, rows, stride=PACK_ROWS), slice(None))]
        lo.append(lax.bitcast_convert_type(w << 16, F32))
        hi.append(lax.bitcast_convert_type(w & jnp.uint32(HIGH_HALF), F32))
    return jnp.concatenate(lo + hi, axis=1)


def _in_proj_kernel(h_ref, g_ref, lbl_ref, w_ref, aglu_ref, q_ref, k_ref, logf_ref, v_ref,
                    og_ref, sga_ref, sgb_ref):
    d = D_MODEL
    h = h_ref[...]
    ms = jnp.mean(h * h, axis=-1, keepdims=True)
    u = (h * lax.rsqrt(ms + RMS_EPS) * g_ref[...]).astype(BF16)

    def proj(j):
        return jnp.dot(u, w_ref[:, j * d:(j + 1) * d], preferred_element_type=F32)

    aglu_ref[...] = (proj(0) * _sigmoid(proj(1))).astype(aglu_ref.dtype)
    zq = proj(2)
    q_ref[...] = (zq * _sigmoid(zq)).astype(q_ref.dtype)
    lbl = lbl_ref[...]
    e = jnp.exp(lbl - jnp.max(lbl, axis=0, keepdims=True))
    lb = e[0:1, :] / jnp.sum(e, axis=0, keepdims=True)
    s = _sigmoid(proj(3))
    logf_ref[...] = jnp.log(lb + (1.0 - lb) * s)
    k_ref[...] = ((1.0 - lb) * (1.0 - s)).astype(k_ref.dtype)
    v_ref[...] = proj(4).astype(v_ref.dtype)
    zg = proj(5)
    og_ref[...] = (zg * _sigmoid(zg)).astype(og_ref.dtype)
    sga_ref[...] = _sigmoid(proj(6)).astype(sga_ref.dtype)
    sgb_ref[...] = _sigmoid(proj(7)).astype(sgb_ref.dtype)


def _in_proj(hp, g_mix, lb_logits, w_in_bf16):
    tp, d = hp.shape
    tm = IN_TILE
    row = pl.BlockSpec((tm, d), lambda i: (i, 0))
    out_dtypes = [BF16, BF16, BF16, F32, BF16, BF16, BF16, BF16]
    return pl.pallas_call(
        _in_proj_kernel,
        grid=(tp // tm,),
        in_specs=[row, _const_spec((1, d)), _const_spec(lb_logits.shape),
                  pl.BlockSpec(w_in_bf16.shape, lambda i: (0, 0), pipeline_mode=pl.Buffered(1))],
        out_specs=[row] * 8,
        out_shape=[jax.ShapeDtypeStruct((tp, d), dt) for dt in out_dtypes],
        compiler_params=_cparams(("parallel",)),
        name="in_proj",
    )(hp, g_mix, lb_logits, w_in_bf16)


def _conv_kernel(halo_ref, cur_ref, wdw_ref, bdw_ref, lng_ref, lnb_ref, wout_ref, bout_ref, sga_ref,
                 out_ref, win_ref, acc_ref, shift_ref):
    tm = cur_ref.shape[0]
    win_ref[0:HALO, :] = halo_ref[...].astype(F32)
    win_ref[HALO:, :] = cur_ref[...].astype(F32)
    first_tap = HALO - (CONV_WIDTH - 1)
    for c in range(D_MODEL // LANES):
        lanes = slice(c * LANES, (c + 1) * LANES)
        for s in range(SUBLANES):
            n_rows = tm + HALO - (SUBLANES if s else 0)
            shift_ref[s, 0:n_rows, :] = win_ref[pl.ds(s, n_rows), lanes]
        w_c = wdw_ref[:, lanes]
        bias = jnp.broadcast_to(bdw_ref[:, lanes], (CONV_ROWS, LANES))

        def chunk(r, carry, lanes=lanes, w_c=w_c, bias=bias):
            r0 = pl.multiple_of(r * CONV_ROWS, CONV_ROWS)
            acc = bias
            for j in range(CONV_WIDTH):
                off = first_tap + j
                rows = pl.ds(r0 + off - off % SUBLANES, CONV_ROWS)
                acc = acc + w_c[j:j + 1, :] * shift_ref[off % SUBLANES, rows, :]
            acc_ref[pl.ds(r0, CONV_ROWS), lanes] = acc
            return carry

        lax.fori_loop(0, tm // CONV_ROWS, chunk, 0)
    a = acc_ref[...]
    mu = jnp.mean(a, axis=-1, keepdims=True)
    ac = a - mu
    var = jnp.mean(ac * ac, axis=-1, keepdims=True)
    y = ac * lax.rsqrt(var + LN_EPS) * lng_ref[...] + lnb_ref[...]
    y = y * _sigmoid(y)
    o = jnp.dot(y.astype(BF16), wout_ref[...], preferred_element_type=F32) + bout_ref[...]
    out_ref[...] = (sga_ref[...].astype(F32) * o).astype(out_ref.dtype)


def _conv_branch(aglu, w_dw, b_dw, ln_g, ln_b, w_out_bf16, b_out, sga):
    tp, d = aglu.shape
    tm = ROW_TILE
    per = tm // HALO
    row = pl.BlockSpec((tm, d), lambda i: (i, 0))
    halo = pl.BlockSpec((HALO, d), lambda i: (jnp.maximum(i * per - 1, 0), 0))
    vec = _const_spec((1, d))
    return pl.pallas_call(
        _conv_kernel,
        grid=(tp // tm,),
        in_specs=[halo, row, _const_spec(w_dw.shape), vec, vec, vec, _const_spec((d, d)), vec, row],
        out_specs=row,
        out_shape=jax.ShapeDtypeStruct((tp, d), BF16),
        scratch_shapes=[pltpu.VMEM((tm + HALO, d), F32), pltpu.VMEM((tm, d), F32),
                        pltpu.VMEM((SUBLANES, tm + HALO, LANES), F32)],
        compiler_params=_cparams(("parallel",)),
        name="conv_branch",
    )(aglu, aglu, w_dw, b_dw, ln_g, ln_b, w_out_bf16, b_out, sga)


def _split_bf16(x):
    hi = x.astype(BF16)
    return hi, (x - hi.astype(F32)).astype(BF16)


def _nt_dot(a, b):
    return lax.dot_general(a, b, (((1,), (1,)), ((), ())), preferred_element_type=F32)


def _tn_dot(a, b):
    return lax.dot_general(a, b, (((0,), (0,)), ((), ())), preferred_element_type=F32)


def _scan_kernel(q_ref, k_ref, v_ref, lf_ref, og_ref, gon_ref, o_ref, st_ref):
    @pl.when(pl.program_id(1) == 0)
    def _():
        st_ref[...] = jnp.zeros_like(st_ref)

    c = CHUNK
    row = lax.broadcasted_iota(jnp.int32, (c, c), 0)
    col = lax.broadcasted_iota(jnp.int32, (c, c), 1)
    causal = col <= row
    diag_mask = jnp.logical_and(causal, row // SUB_BLOCK == col // SUB_BLOCK)
    lf = lf_ref[...]
    lf_hi = lf.astype(BF16)
    lf_mid, lf_lo = _split_bf16(lf - lf_hi.astype(F32))
    tri = causal.astype(BF16)
    bcum = (jnp.dot(tri, lf_hi, preferred_element_type=F32)
            + (jnp.dot(tri, lf_mid, preferred_element_type=F32) + jnp.dot(tri, lf_lo, preferred_element_type=F32)))

    def bcast_rows(rows):
        return jnp.concatenate([jnp.broadcast_to(r, (SUB_BLOCK, HEAD_DIM)) for r in rows], axis=0)

    for h in range(N_HEADS):
        hs = slice(h * HEAD_DIM, (h + 1) * HEAD_DIM)
        b = bcum[:, hs]
        q = q_ref[:, hs].astype(F32)
        k = k_ref[:, hs].astype(F32)
        v = v_ref[:, hs]
        ends = [b[i * SUB_BLOCK + SUB_BLOCK - 1:i * SUB_BLOCK + SUB_BLOCK, :] for i in range(N_SUB)]
        mids = [b[i * SUB_BLOCK + SUB_BLOCK // 2 - 1:i * SUB_BLOCK + SUB_BLOCK // 2, :] for i in range(N_SUB)]
        mid_full = bcast_rows(mids)
        b_last = ends[-1]
        zero_row = jnp.zeros_like(b_last)
        qm = q * jnp.exp(b - mid_full)
        km = k * jnp.exp(mid_full - b)
        scores = jnp.where(diag_mask, _nt_dot(qm.astype(BF16), km.astype(BF16)), 0.0)
        q_parts, k_parts = [], []
        for j in range(N_SUB - 1):
            q_rows = [jnp.exp(mids[i] - ends[j]) if i > j else zero_row for i in range(N_SUB)]
            k_rows = [jnp.exp(ends[j] - mids[j]) if i == j else zero_row for i in range(N_SUB)]
            q_parts.append((qm * bcast_rows(q_rows)).astype(BF16))
            k_parts.append((km * bcast_rows(k_rows)).astype(BF16))
        scores = scores + _nt_dot(jnp.concatenate(q_parts, axis=1), jnp.concatenate(k_parts, axis=1))
        st = st_ref[h]
        q_in = (qm * bcast_rows([jnp.exp(m) for m in mids])).astype(BF16)
        o = _nt_dot(q_in, st.astype(BF16)) + jnp.dot(scores.astype(BF16), v, preferred_element_type=F32)
        k_out = (km * bcast_rows([jnp.exp(b_last - m) for m in mids])).astype(BF16)
        st_ref[h] = st * jnp.exp(b_last) + _tn_dot(v, k_out)
        ms = jnp.mean(o * o, axis=-1, keepdims=True)
        on = o * lax.rsqrt(ms + RMS_EPS) * gon_ref[:, hs]
        o_ref[:, hs] = (on * og_ref[:, hs].astype(F32)).astype(o_ref.dtype)


def _hgrn2_scan(q, k, v, logf, og, g_onorm, n_batch):
    tp, d = q.shape
    n_chunks = tp // n_batch // CHUNK
    blk = pl.BlockSpec((CHUNK, d), lambda b, c: (b * n_chunks + c, 0))
    return pl.pallas_call(
        _scan_kernel,
        grid=(n_batch, n_chunks),
        in_specs=[blk, blk, blk, blk, blk, pl.BlockSpec((1, d), lambda b, c: (0, 0))],
        out_specs=blk,
        out_shape=jax.ShapeDtypeStruct((tp, d), BF16),
        scratch_shapes=[pltpu.VMEM((N_HEADS, HEAD_DIM, HEAD_DIM), F32)],
        compiler_params=_cparams(("arbitrary", "arbitrary")),
        name="hgrn2_scan",
    )(q, k, v, logf, og, g_onorm)


def _merge_route_kernel(h_ref, on_ref, ap_ref, sgb_ref, wrnn_ref, wo_ref, gffn_ref, wr_ref, br_ref,
                        h1_ref, u2_ref, info_ref, cnt_ref, tri_ref, carry_ref, colcnt_ref):
    tm = h_ref.shape[0]

    @pl.when(pl.program_id(0) == 0)
    def _():
        r_i = lax.broadcasted_iota(jnp.int32, (tm, tm), 0)
        c_i = lax.broadcasted_iota(jnp.int32, (tm, tm), 1)
        tri_ref[...] = (c_i < r_i).astype(BF16)
        carry_ref[...] = jnp.zeros_like(carry_ref)
        colcnt_ref[...] = jnp.zeros_like(colcnt_ref)

    r = jnp.dot(on_ref[...], wrnn_ref[...], preferred_element_type=F32)
    y = ap_ref[...].astype(F32) + sgb_ref[...].astype(F32) * r
    h1 = h_ref[...] + jnp.dot(y.astype(BF16), wo_ref[...], preferred_element_type=F32)
    h1_ref[...] = h1
    ms = jnp.mean(h1 * h1, axis=-1, keepdims=True)
    u2 = h1 * lax.rsqrt(ms + RMS_EPS) * gffn_ref[...]
    _store_packed(u2_ref, u2)
    u_hi, u_lo = _split_bf16(u2)
    w_hi, w_lo = _split_bf16(wr_ref[...])
    logits = (jnp.dot(u_hi, w_hi, preferred_element_type=F32)
              + (jnp.dot(u_lo, w_hi, preferred_element_type=F32) + jnp.dot(u_hi, w_lo, preferred_element_type=F32))
              + br_ref[...])
    lane = lax.broadcasted_iota(jnp.int32, (tm, LANES), 1)
    cur = jnp.where(lane < N_EXPERTS, logits, NEG_BIG)
    vals, idxs, sels = [], [], []
    for _ in range(TOP_K):
        m = jnp.max(cur, axis=-1, keepdims=True)
        idx = jnp.min(jnp.where(cur == m, lane, LANES), axis=-1, keepdims=True)
        sel = lane == idx
        cur = jnp.where(sel, 2.0 * NEG_BIG, cur)
        vals.append(m)
        idxs.append(idx)
        sels.append(sel)
    exps = [jnp.exp(vk - vals[0]) for vk in vals]
    den = exps[0] + exps[1] + exps[2] + exps[3]
    onehot = jnp.logical_or(jnp.logical_or(sels[0], sels[1]), jnp.logical_or(sels[2], sels[3]))
    onehot_bf = onehot.astype(BF16)
    rank_all = jnp.dot(tri_ref[...], onehot_bf, preferred_element_type=F32) + carry_ref[...]
    carry_ref[...] += jnp.sum(onehot.astype(F32), axis=0, keepdims=True)
    colcnt_ref[...] += _tn_dot(onehot_bf, jnp.ones((tm, LANES), BF16))
    info = jnp.zeros((tm, LANES), F32)
    for kk in range(TOP_K):
        rank_k = jnp.sum(jnp.where(sels[kk], rank_all, 0.0), axis=-1, keepdims=True)
        info = jnp.where(lane == kk, exps[kk] / den, info)
        info = jnp.where(lane == TOP_K + kk, idxs[kk].astype(F32), info)
        info = jnp.where(lane == 2 * TOP_K + kk, rank_k, info)
    info_ref[...] = info
    cnt_ref[...] = colcnt_ref[...]


def _merge_route(hp, on, ap, sgb, w_rnn_bf16, w_o_bf16, g_ffn, w_router_pad, b_router_pad):
    tp, d = hp.shape
    tm = ROW_TILE
    row = pl.BlockSpec((tm, d), lambda i: (i, 0))
    vec = _const_spec((1, d))
    return pl.pallas_call(
        _merge_route_kernel,
        grid=(tp // tm,),
        in_specs=[row, row, row, row, _const_spec((d, d)), _const_spec((d, d)), vec,
                  _const_spec((d, LANES)), _const_spec((1, LANES))],
        out_specs=[row, pl.BlockSpec((tm * PACK_ROWS, LANES), lambda i: (i, 0)),
                   pl.BlockSpec((tm, LANES), lambda i: (i, 0)), _const_spec((LANES, LANES))],
        out_shape=[jax.ShapeDtypeStruct((tp, d), F32), jax.ShapeDtypeStruct((tp * PACK_ROWS, LANES), jnp.uint32),
                   jax.ShapeDtypeStruct((tp, LANES), F32), jax.ShapeDtypeStruct((LANES, LANES), F32)],
        scratch_shapes=[pltpu.VMEM((tm, tm), BF16), pltpu.VMEM((1, LANES), F32),
                        pltpu.VMEM((LANES, LANES), F32)],
        compiler_params=_cparams(("arbitrary",)),
        name="merge_route",
    )(hp, on, ap, sgb, w_rnn_bf16, w_o_bf16, g_ffn, w_router_pad, b_router_pad)


def _slots_kernel(info_ref, cnt_ref, dest_ref, te_ref, pstart_ref):
    tm = info_ref.shape[0]
    n_tile_lanes = te_ref.shape[1]

    @pl.when(pl.program_id(0) == 0)
    def _():
        r_i = lax.broadcasted_iota(jnp.int32, (LANES, LANES), 0)
        c_i = lax.broadcasted_iota(jnp.int32, (LANES, LANES), 1)
        cnt = cnt_ref[...]
        tiles = jnp.floor((cnt + (EXPERT_TILE - 1)) / EXPERT_TILE)
        pend_col = jnp.dot((c_i <= r_i).astype(F32), tiles, preferred_element_type=F32,
                           precision=lax.Precision.HIGHEST)
        pstart_ref[...] = _tn_dot_f32(tiles, (r_i < c_i).astype(F32))[0:SUBLANES, :]
        pend_row = _tn_dot_f32(tiles, (r_i <= c_i).astype(F32))[0:1, :]
        tile_id = lax.broadcasted_iota(jnp.int32, (LANES, n_tile_lanes), 1).astype(F32)
        exp_id = lax.broadcasted_iota(jnp.int32, (LANES, n_tile_lanes), 0)
        pend_wide = jnp.concatenate([pend_col] * (n_tile_lanes // LANES), axis=1)
        below = jnp.logical_and(pend_wide <= tile_id, exp_id < N_EXPERTS)
        te = jnp.minimum(jnp.sum(below.astype(F32), axis=0, keepdims=True), N_EXPERTS - 1.0)
        n_active = pend_col[N_EXPERTS - 1:N_EXPERTS, 0:1]
        sub_t = lax.broadcasted_iota(jnp.int32, (SUBLANES, n_tile_lanes), 0)
        pend_wide_row = jnp.concatenate([pend_row] + [jnp.zeros_like(pend_row)] * (n_tile_lanes // LANES - 1), axis=1)
        out = jnp.where(sub_t == 0, jnp.broadcast_to(te, (SUBLANES, n_tile_lanes)),
                        jnp.where(sub_t == 1, jnp.broadcast_to(n_active, (SUBLANES, n_tile_lanes)),
                                  jnp.broadcast_to(pend_wide_row, (SUBLANES, n_tile_lanes))))
        te_ref[...] = out.astype(jnp.int32)

    info = info_ref[...]
    pstart_row = pstart_ref[0:1, :]
    lane = lax.broadcasted_iota(jnp.int32, (tm, LANES), 1)
    dest = jnp.zeros((tm, LANES), F32)
    for kk in range(TOP_K):
        idx_k = info[:, TOP_K + kk:TOP_K + kk + 1].astype(jnp.int32)
        start_k = jnp.sum(jnp.where(lane == idx_k, pstart_row, 0.0), axis=-1, keepdims=True)
        slot_k = start_k * EXPERT_TILE + info[:, 2 * TOP_K + kk:2 * TOP_K + kk + 1]
        dest = jnp.where(lane == kk, slot_k, dest)
    dest_ref[...] = dest.astype(jnp.int32)


def _tn_dot_f32(a, b):
    return lax.dot_general(a, b, (((0,), (0,)), ((), ())), preferred_element_type=F32,
                           precision=lax.Precision.HIGHEST)


def _slots(info, cnt, n_tiles_max):
    tp = info.shape[0]
    tm = tp // SLOTS_STEPS
    n_tile_lanes = -(-n_tiles_max // LANES) * LANES
    return pl.pallas_call(
        _slots_kernel,
        grid=(tp // tm,),
        in_specs=[pl.BlockSpec((tm, LANES), lambda i: (i, 0)), _const_spec((LANES, LANES))],
        out_specs=[pl.BlockSpec((tm, LANES), lambda i: (i, 0)), _const_spec((8, n_tile_lanes))],
        out_shape=[jax.ShapeDtypeStruct((tp, LANES), jnp.int32),
                   jax.ShapeDtypeStruct((8, n_tile_lanes), jnp.int32)],
        scratch_shapes=[pltpu.VMEM((SUBLANES, LANES), F32)],
        compiler_params=_cparams(("arbitrary",)),
        name="slots",
    )(info, cnt)


def _dest_slot(dest_smem, slot):
    return dest_smem.at[pl.ds(pl.multiple_of(slot * (ROUTE_TILE * TOP_K), ROUTE_TILE * TOP_K), ROUTE_TILE * TOP_K)]


def _fetch_dest(dest_hbm, dest_smem, dsem, step, slot):
    return pltpu.make_async_copy(dest_hbm.at[step], _dest_slot(dest_smem, slot), dsem.at[slot])


def _slab(ref, token):
    return ref.at[pl.ds(token * PACK_ROWS, PACK_ROWS), :]


def _dispatch_kernel(pend_ref, dest_hbm, u2_hbm, xs_ref, dest_smem, zero_ref, rows_ref, dsem, sem, zsem, rsem):
    i = pl.program_id(0)
    n = pl.num_programs(0)
    slot = i % 2
    tile_rows = EXPERT_TILE * PACK_ROWS

    def last_tile_zero(e):
        first = pend_ref[e - 1] if e else 0
        dst = xs_ref.at[pl.ds((pend_ref[e] - 1) * tile_rows, tile_rows), :]
        return pend_ref[e] > first, pltpu.make_async_copy(zero_ref, dst, zsem)

    def stage(step):
        s = step % STAGE_SLOTS
        src = u2_hbm.at[pl.ds(step * ROUTE_TILE * PACK_ROWS, ROUTE_TILE * PACK_ROWS), :]
        return pltpu.make_async_copy(src, rows_ref.at[s], rsem.at[s])

    @pl.when(i == 0)
    def _():
        _fetch_dest(dest_hbm, dest_smem, dsem, 0, 0).start()
        stage(0).start()
        zero_ref[...] = jnp.zeros_like(zero_ref)
        for e in range(N_EXPERTS):
            has_tiles, copy = last_tile_zero(e)
            pl.when(has_tiles)(copy.start)
        n_tiles = xs_ref.shape[0] // tile_rows

        def spare_tile(t):
            return pltpu.make_async_copy(zero_ref, xs_ref.at[pl.ds(t * tile_rows, tile_rows), :], zsem)

        lax.fori_loop(pend_ref[N_EXPERTS - 1], n_tiles, lambda t, c: (spare_tile(t).start(), c)[1], 0)
        lax.fori_loop(pend_ref[N_EXPERTS - 1], n_tiles, lambda t, c: (spare_tile(t).wait(), c)[1], 0)
        for e in range(N_EXPERTS):
            has_tiles, copy = last_tile_zero(e)
            pl.when(has_tiles)(copy.wait)

    _fetch_dest(dest_hbm, dest_smem, dsem, i, slot).wait()
    stage(i).wait()

    @pl.when(i + 1 < n)
    def _():
        _fetch_dest(dest_hbm, dest_smem, dsem, i + 1, 1 - slot).start()
        stage(i + 1).start()

    rows = rows_ref.at[i % STAGE_SLOTS]

    dest_base = slot * (ROUTE_TILE * TOP_K)

    def issue(t, carry):
        for kk in range(TOP_K):
            dst = _slab(xs_ref, dest_smem[dest_base + t * TOP_K + kk])
            pltpu.make_async_copy(_slab(rows, t), dst, sem.at[i % STAGE_SLOTS]).start(priority=kk % 2)
        return carry

    lax.fori_loop(0, ROUTE_TILE, issue, 0, unroll=4)

    def drain(step):
        s = step % STAGE_SLOTS
        for _ in range(TOP_K):
            pltpu.make_async_copy(rows_ref.at[s], xs_ref.at[pl.ds(0, ROUTE_TILE * PACK_ROWS), :], sem.at[s]).wait()

    pl.when(i > 0)(lambda: drain(i - 1))
    pl.when(i == n - 1)(lambda: drain(i))


def _dispatch(pend, dest2, u2p, n_slots):
    n_steps = dest2.shape[0]
    grid_spec = pltpu.PrefetchScalarGridSpec(
        num_scalar_prefetch=1,
        grid=(n_steps,),
        in_specs=[pl.BlockSpec(memory_space=pl.ANY), pl.BlockSpec(memory_space=pl.ANY)],
        out_specs=pl.BlockSpec(memory_space=pl.ANY),
        scratch_shapes=[pltpu.SMEM((2 * ROUTE_TILE * TOP_K,), jnp.int32),
                        pltpu.VMEM((EXPERT_TILE * PACK_ROWS, LANES), jnp.uint32),
                        pltpu.VMEM((STAGE_SLOTS, ROUTE_TILE * PACK_ROWS, LANES), jnp.uint32),
                        pltpu.SemaphoreType.DMA((2,)), pltpu.SemaphoreType.DMA((STAGE_SLOTS,)),
                        pltpu.SemaphoreType.DMA, pltpu.SemaphoreType.DMA((STAGE_SLOTS,))],
    )
    return pl.pallas_call(
        _dispatch_kernel,
        grid_spec=grid_spec,
        out_shape=jax.ShapeDtypeStruct((n_slots * PACK_ROWS, LANES), jnp.uint32),
        compiler_params=_cparams(("arbitrary",)),
        name="dispatch",
    )(pend, dest2, u2p)


def _expert_kernel(te_ref, nact_ref, x_ref, wgu_ref, bgu_ref, wdn_ref, bdn_ref, y_ref):
    del te_ref

    @pl.when(pl.program_id(0) < nact_ref[0])
    def _():
        x = _load_packed(x_ref, EXPERT_TILE).astype(BF16)
        gu = jnp.dot(x, wgu_ref[0].astype(BF16), preferred_element_type=F32) + bgu_ref[0]
        gate = jnp.minimum(gu[:, :D_FF], SWIGLU_LIMIT)
        up = jnp.clip(gu[:, D_FF:], -SWIGLU_LIMIT, SWIGLU_LIMIT)
        hdn = (up + 1.0) * (gate * _sigmoid(SWIGLU_ALPHA * gate))
        y = jnp.dot(hdn.astype(BF16), wdn_ref[0].astype(BF16), preferred_element_type=F32) + bdn_ref[0]
        _store_packed(y_ref, y)

    @pl.when(pl.program_id(0) >= nact_ref[0])
    def _():
        y_ref[...] = jnp.zeros_like(y_ref)


def _expert_mlp(te, nact, xs, w_gu, b_gu, w_dn, b_dn):
    d = D_MODEL
    tile_rows = EXPERT_TILE * PACK_ROWS
    n_tiles = xs.shape[0] // tile_rows

    def tile_map(i, te_ref, nact_ref):
        return (jnp.minimum(i, nact_ref[0] - 1), 0)

    def exp_map(i, te_ref, nact_ref):
        return (te_ref[jnp.minimum(i, nact_ref[0] - 1)], 0, 0)

    grid_spec = pltpu.PrefetchScalarGridSpec(
        num_scalar_prefetch=2,
        grid=(n_tiles,),
        in_specs=[pl.BlockSpec((tile_rows, LANES), tile_map),
                  pl.BlockSpec((1, d, 2 * D_FF), exp_map),
                  pl.BlockSpec((1, 1, 2 * D_FF), exp_map),
                  pl.BlockSpec((1, D_FF, d), exp_map),
                  pl.BlockSpec((1, 1, d), exp_map)],
        out_specs=pl.BlockSpec((tile_rows, LANES), lambda i, te_ref, nact_ref: (i, 0)),
    )
    return pl.pallas_call(
        _expert_kernel,
        grid_spec=grid_spec,
        out_shape=jax.ShapeDtypeStruct(xs.shape, jnp.uint32),
        compiler_params=_cparams(("arbitrary",)),
        name="expert_mlp",
    )(te, nact, xs, w_gu, b_gu, w_dn, b_dn)


def _combine_kernel(dest_hbm, h1_ref, info_ref, gfin_ref, ys_hbm, out_ref, dest_smem, ybuf, dsem, sem,
                    *, chunks_per_batch):
    j = pl.program_id(0)
    n = pl.num_programs(0)
    slot = j % 2

    def fetch(step):
        chunk = (step // chunks_per_batch) * (chunks_per_batch + 1) + step % chunks_per_batch + 1
        return pltpu.make_async_copy(dest_hbm.at[chunk], _dest_slot(dest_smem, step % 2), dsem.at[step % 2])

    def issue(step):
        s = step % 2
        dest_base = s * (ROUTE_TILE * TOP_K)

        def body(t, carry):
            for kk in range(TOP_K):
                src = _slab(ys_hbm, dest_smem[dest_base + t * TOP_K + kk])
                pltpu.make_async_copy(src, _slab(ybuf.at[s, kk], t), sem.at[s]).start(priority=kk % 2)
            return carry

        lax.fori_loop(0, ROUTE_TILE, body, 0, unroll=4)

    @pl.when(j == 0)
    def _():
        fetch(0).start()
        fetch(0).wait()
        issue(0)
        fetch(1).start()

    @pl.when(j + 1 < n)
    def _():
        fetch(j + 1).wait()
        issue(j + 1)

    @pl.when(j + 2 < n)
    def _():
        fetch(j + 2).start()

    for kk in range(TOP_K):
        pltpu.make_async_copy(ys_hbm.at[pl.ds(0, ROUTE_TILE * PACK_ROWS), :], ybuf.at[slot, kk], sem.at[slot]).wait()
    info = info_ref[...]
    h2 = h1_ref[...]
    for kk in range(TOP_K):
        h2 = h2 + info[:, kk:kk + 1] * _load_packed(ybuf, ROUTE_TILE, lead=(slot, kk))
    ms = jnp.mean(h2 * h2, axis=-1, keepdims=True)
    out_ref[...] = h2 * lax.rsqrt(ms + RMS_EPS) * gfin_ref[...]


def _combine(dest2, h1, info, g_final, ys, n_batch, seq):
    d = h1.shape[1]
    cpb = seq // ROUTE_TILE

    def padded_chunk(j):
        return ((j // cpb) * (cpb + 1) + j % cpb + 1, 0)

    return pl.pallas_call(
        functools.partial(_combine_kernel, chunks_per_batch=cpb),
        grid=(n_batch * cpb,),
        in_specs=[pl.BlockSpec(memory_space=pl.ANY),
                  pl.BlockSpec((ROUTE_TILE, d), padded_chunk),
                  pl.BlockSpec((ROUTE_TILE, LANES), padded_chunk),
                  pl.BlockSpec((1, d), lambda j: (0, 0)),
                  pl.BlockSpec(memory_space=pl.ANY)],
        out_specs=pl.BlockSpec((ROUTE_TILE, d), lambda j: (j, 0)),
        out_shape=jax.ShapeDtypeStruct((n_batch * seq, d), F32),
        scratch_shapes=[pltpu.SMEM((2 * ROUTE_TILE * TOP_K,), jnp.int32),
                        pltpu.VMEM((2, TOP_K, ROUTE_TILE * PACK_ROWS, LANES), jnp.uint32),
                        pltpu.SemaphoreType.DMA((2,)), pltpu.SemaphoreType.DMA((2,))],
        compiler_params=_cparams(("arbitrary",)),
        name="combine",
    )(dest2, h1, info, g_final, ys)


def kernel(x, meta_tokens, lb_logits, g_mix, w_in, w_dw, b_dw, ln_g, ln_b, w_conv_out, b_conv_out,
           g_onorm, w_rnn_out, w_o, g_ffn, w_router, b_router, w_gate_up, b_gate_up, w_down, b_down,
           g_final):
    n_batch, seq, d = x.shape
    assert d == D_MODEL and w_in.shape[0] == 1, "single-layer block with D_MODEL features"
    assert CHUNK == ROUTE_TILE and seq % CHUNK == 0
    seq_pad = CHUNK + seq
    tp = n_batch * seq_pad
    assert tp % ROW_TILE == 0

    meta = jnp.broadcast_to(meta_tokens.astype(x.dtype)[None], (n_batch, N_META, d))
    hp = jnp.concatenate([jnp.zeros((n_batch, CHUNK_PAD, d), x.dtype), meta, x], axis=1).reshape(tp, d)
    vec = lambda a: a.reshape(1, -1).astype(F32)

    aglu, q, k, logf, v, og, sga, sgb = _in_proj(hp, vec(g_mix[0]), lb_logits.astype(F32), w_in[0].astype(BF16))
    ap = _conv_branch(aglu, w_dw[0].astype(F32), vec(b_dw[0]), vec(ln_g[0]), vec(ln_b[0]),
                      w_conv_out[0].astype(BF16), vec(b_conv_out[0]), sga)
    on = _hgrn2_scan(q, k, v, logf, og, vec(g_onorm[0]), n_batch)
    w_router_pad = jnp.pad(w_router[0].astype(F32), ((0, 0), (0, LANES - N_EXPERTS)))
    b_router_pad = jnp.pad(vec(b_router[0]), ((0, 0), (0, LANES - N_EXPERTS)))
    h1, u2p, info, cnt = _merge_route(hp, on, ap, sgb, w_rnn_out[0].astype(BF16), w_o[0].astype(BF16),
                                      vec(g_ffn[0]), w_router_pad, b_router_pad)

    n_tiles_max = -(-(tp * TOP_K + N_EXPERTS * (EXPERT_TILE - 1)) // EXPERT_TILE)
    dest, te = _slots(info, cnt, n_tiles_max)
    dest2 = dest[:, :TOP_K].reshape(tp // ROUTE_TILE, ROUTE_TILE * TOP_K)
    xs = _dispatch(te[2, :N_EXPERTS], dest2, u2p, n_tiles_max * EXPERT_TILE)
    ys = _expert_mlp(te[0, :n_tiles_max], te[1, :1], xs, w_gate_up[0],
                     b_gate_up[0].reshape(N_EXPERTS, 1, -1).astype(F32), w_down[0],
                     b_down[0].reshape(N_EXPERTS, 1, -1).astype(F32))
    out = _combine(dest2, h1, info, vec(g_final), ys, n_batch, seq)
    return out.reshape(n_batch, seq, d)
```

```python
import functools

import jax
import jax.numpy as jnp
from jax import lax
from jax.experimental import pallas as pl
from jax.experimental.pallas import tpu as pltpu

F32 = jnp.float32
BF16 = jnp.bfloat16

D_MODEL = 1024
N_META = 16
CHUNK = 128
CHUNK_PAD = CHUNK - N_META
CONV_WIDTH = 31
HEAD_DIM = 128
N_HEADS = D_MODEL // HEAD_DIM
N_EXPERTS = 32
TOP_K = 4
D_FF = D_MODEL
SWIGLU_LIMIT = 7.0
SWIGLU_ALPHA = 1.702
RMS_EPS = 1e-6
LN_EPS = 1e-5
N_IN_GROUPS = 8

LANES = 128
SUBLANES = 8
SUB_BLOCK = 32
N_SUB = CHUNK // SUB_BLOCK
HALO = 32
ROW_TILE = 640
IN_TILE = 320
SLOTS_STEPS = 5
CONV_ROWS = 64
EXPERT_TILE = 512
ROUTE_TILE = 128
STAGE_SLOTS = 3
NEG_BIG = -1e30
VMEM_LIMIT = 56 * 1024 * 1024


def _sigmoid(x):
    return 1.0 / (1.0 + jnp.exp(-x))


def _cparams(sem):
    return pltpu.CompilerParams(dimension_semantics=sem, vmem_limit_bytes=VMEM_LIMIT)


def _const_spec(shape):
    nd = len(shape)
    return pl.BlockSpec(shape, lambda *_: (0,) * nd)


PACK_ROWS = D_MODEL // 2 // 128
HIGH_HALF = 0xFFFF0000


def _store_packed(ref, x):
    rows = x.shape[0]
    half = D_MODEL // 2
    lo = lax.bitcast_convert_type(x[:, :half].astype(BF16).astype(F32), jnp.uint32)
    hi = lax.bitcast_convert_type(x[:, half:].astype(BF16).astype(F32), jnp.uint32)
    words = (lo >> 16) | (hi & jnp.uint32(HIGH_HALF))
    for j in range(PACK_ROWS):
        ref[pl.ds(j, rows, stride=PACK_ROWS), :] = words[:, j * LANES:(j + 1) * LANES]


def _load_packed(ref, rows, lead=()):
    lo, hi = [], []
    for j in range(PACK_ROWS):
        w = ref[lead + (pl.ds(j, rows, stride=PACK_ROWS), slice(None))]
        lo.append(lax.bitcast_convert_type(w << 16, F32))
        hi.append(lax.bitcast_convert_type(w & jnp.uint32(HIGH_HALF), F32))
    return jnp.concatenate(lo + hi, axis=1)


def _in_proj_kernel(h_ref, g_ref, lbl_ref, w_ref, aglu_ref, q_ref, k_ref, logf_ref, v_ref,
                    og_ref, sga_ref, sgb_ref):
    d = D_MODEL
    h = h_ref[...]
    ms = jnp.mean(h * h, axis=-1, keepdims=True)
    u = (h * lax.rsqrt(ms + RMS_EPS) * g_ref[...]).astype(BF16)

    def proj(j):
        return jnp.dot(u, w_ref[:, j * d:(j + 1) * d], preferred_element_type=F32)

    aglu_ref[...] = (proj(0) * _sigmoid(proj(1))).astype(aglu_ref.dtype)
    zq = proj(2)
    q_ref[...] = (zq * _sigmoid(zq)).astype(q_ref.dtype)
    lbl = lbl_ref[...]
    e = jnp.exp(lbl - jnp.max(lbl, axis=0, keepdims=True))
    lb = e[0:1, :] / jnp.sum(e, axis=0, keepdims=True)
    s = _sigmoid(proj(3))
    logf_ref[...] = jnp.log(lb + (1.0 - lb) * s)
    k_ref[...] = ((1.0 - lb) * (1.0 - s)).astype(k_ref.dtype)
    v_ref[...] = proj(4).astype(v_ref.dtype)
    zg = proj(5)
    og_ref[...] = (zg * _sigmoid(zg)).astype(og_ref.dtype)
    sga_ref[...] = _sigmoid(proj(6)).astype(sga_ref.dtype)
    sgb_ref[...] = _sigmoid(proj(7)).astype(sgb_ref.dtype)


def _in_proj(hp, g_mix, lb_logits, w_in_bf16):
    tp, d = hp.shape
    tm = IN_TILE
    row = pl.BlockSpec((tm, d), lambda i: (i, 0))
    out_dtypes = [BF16, BF16, BF16, F32, BF16, BF16, BF16, BF16]
    return pl.pallas_call(
        _in_proj_kernel,
        grid=(tp // tm,),
        in_specs=[row, _const_spec((1, d)), _const_spec(lb_logits.shape),
                  pl.BlockSpec(w_in_bf16.shape, lambda i: (0, 0), pipeline_mode=pl.Buffered(1))],
        out_specs=[row] * 8,
        out_shape=[jax.ShapeDtypeStruct((tp, d), dt) for dt in out_dtypes],
        compiler_params=_cparams(("parallel",)),
        name="in_proj",
    )(hp, g_mix, lb_logits, w_in_bf16)


def _conv_kernel(halo_ref, cur_ref, wdw_ref, bdw_ref, lng_ref, lnb_ref, wout_ref, bout_ref, sga_ref,
                 out_ref, win_ref, acc_ref, shift_ref):
    tm = cur_ref.shape[0]
    win_ref[0:HALO, :] = halo_ref[...].astype(F32)
    win_ref[HALO:, :] = cur_ref[...].astype(F32)
    first_tap = HALO - (CONV_WIDTH - 1)
    for c in range(D_MODEL // LANES):
        lanes = slice(c * LANES, (c + 1) * LANES)
        for s in range(SUBLANES):
            n_rows = tm + HALO - (SUBLANES if s else 0)
            shift_ref[s, 0:n_rows, :] = win_ref[pl.ds(s, n_rows), lanes]
        w_c = wdw_ref[:, lanes]
        bias = jnp.broadcast_to(bdw_ref[:, lanes], (CONV_ROWS, LANES))

        def chunk(r, carry, lanes=lanes, w_c=w_c, bias=bias):
            r0 = pl.multiple_of(r * CONV_ROWS, CONV_ROWS)
            acc = bias
            for j in range(CONV_WIDTH):
                off = first_tap + j
                rows = pl.ds(r0 + off - off % SUBLANES, CONV_ROWS)
                acc = acc + w_c[j:j + 1, :] * shift_ref[off % SUBLANES, rows, :]
            acc_ref[pl.ds(r0, CONV_ROWS), lanes] = acc
            return carry

        lax.fori_loop(0, tm // CONV_ROWS, chunk, 0)
    a = acc_ref[...]
    mu = jnp.mean(a, axis=-1, keepdims=True)
    ac = a - mu
    var = jnp.mean(ac * ac, axis=-1, keepdims=True)
    y = ac * lax.rsqrt(var + LN_EPS) * lng_ref[...] + lnb_ref[...]
    y = y * _sigmoid(y)
    o = jnp.dot(y.astype(BF16), wout_ref[...], preferred_element_type=F32) + bout_ref[...]
    out_ref[...] = (sga_ref[...].astype(F32) * o).astype(out_ref.dtype)


def _conv_branch(aglu, w_dw, b_dw, ln_g, ln_b, w_out_bf16, b_out, sga):
    tp, d = aglu.shape
    tm = ROW_TILE
    per = tm // HALO
    row = pl.BlockSpec((tm, d), lambda i: (i, 0))
    halo = pl.BlockSpec((HALO, d), lambda i: (jnp.maximum(i * per - 1, 0), 0))
    vec = _const_spec((1, d))
    return pl.pallas_call(
        _conv_kernel,
        grid=(tp // tm,),
        in_specs=[halo, row, _const_spec(w_dw.shape), vec, vec, vec, _const_spec((d, d)), vec, row],
        out_specs=row,
        out_shape=jax.ShapeDtypeStruct((tp, d), BF16),
        scratch_shapes=[pltpu.VMEM((tm + HALO, d), F32), pltpu.VMEM((tm, d), F32),
                        pltpu.VMEM((SUBLANES, tm + HALO, LANES), F32)],
        compiler_params=_cparams(("parallel",)),
        name="conv_branch",
    )(aglu, aglu, w_dw, b_dw, ln_g, ln_b, w_out_bf16, b_out, sga)


def _split_bf16(x):
    hi = x.astype(BF16)
    return hi, (x - hi.astype(F32)).astype(BF16)


def _nt_dot(a, b):
    return lax.dot_general(a, b, (((1,), (1,)), ((), ())), preferred_element_type=F32)


def _tn_dot(a, b):
    return lax.dot_general(a, b, (((0,), (0,)), ((), ())), preferred_element_type=F32)


def _scan_kernel(q_ref, k_ref, v_ref, lf_ref, og_ref, gon_ref, o_ref, st_ref):
    @pl.when(pl.program_id(1) == 0)
    def _():
        st_ref[...] = jnp.zeros_like(st_ref)

    c = CHUNK
    row = lax.broadcasted_iota(jnp.int32, (c, c), 0)
    col = lax.broadcasted_iota(jnp.int32, (c, c), 1)
    causal = col <= row
    diag_mask = jnp.logical_and(causal, row // SUB_BLOCK == col // SUB_BLOCK)
    lf = lf_ref[...]
    lf_hi = lf.astype(BF16)
    lf_mid, lf_lo = _split_bf16(lf - lf_hi.astype(F32))
    tri = causal.astype(BF16)
    bcum = (jnp.dot(tri, lf_hi, preferred_element_type=F32)
            + (jnp.dot(tri, lf_mid, preferred_element_type=F32) + jnp.dot(tri, lf_lo, preferred_element_type=F32)))

    def bcast_rows(rows):
        return jnp.concatenate([jnp.broadcast_to(r, (SUB_BLOCK, HEAD_DIM)) for r in rows], axis=0)

    heads = [slice(h * HEAD_DIM, (h + 1) * HEAD_DIM) for h in range(N_HEADS)]
    operands = []
    for hs in heads:
        b = bcum[:, hs]
        q = q_ref[:, hs].astype(F32)
        k = k_ref[:, hs].astype(F32)
        ends = [b[i * SUB_BLOCK + SUB_BLOCK - 1:i * SUB_BLOCK + SUB_BLOCK, :] for i in range(N_SUB)]
        mids = [b[i * SUB_BLOCK + SUB_BLOCK // 2 - 1:i * SUB_BLOCK + SUB_BLOCK // 2, :] for i in range(N_SUB)]
        mid_full = bcast_rows(mids)
        b_last = ends[-1]
        zero_row = jnp.zeros_like(b_last)
        qm = q * jnp.exp(b - mid_full)
        km = k * jnp.exp(mid_full - b)
        q_parts, k_parts = [], []
        for j in range(N_SUB - 1):
            q_rows = [jnp.exp(mids[i] - ends[j]) if i > j else zero_row for i in range(N_SUB)]
            k_rows = [jnp.exp(ends[j] - mids[j]) if i == j else zero_row for i in range(N_SUB)]
            q_parts.append((qm * bcast_rows(q_rows)).astype(BF16))
            k_parts.append((km * bcast_rows(k_rows)).astype(BF16))
        q_in = (qm * bcast_rows([jnp.exp(m) for m in mids])).astype(BF16)
        k_out = (km * bcast_rows([jnp.exp(b_last - m) for m in mids])).astype(BF16)
        operands.append((qm.astype(BF16), km.astype(BF16), jnp.concatenate(q_parts, axis=1),
                         jnp.concatenate(k_parts, axis=1), q_in, k_out, jnp.exp(b_last)))
    products = []
    for h, hs in enumerate(heads):
        qm, km, q_cat, k_cat, q_in, k_out, _ = operands[h]
        products.append((_nt_dot(qm, km), _nt_dot(q_cat, k_cat), _nt_dot(q_in, st_ref[h].astype(BF16)),
                         _tn_dot(v_ref[:, hs], k_out)))
    outs = []
    for h, hs in enumerate(heads):
        same_block, earlier_blocks, from_state, state_update = products[h]
        scores = jnp.where(diag_mask, same_block, 0.0) + earlier_blocks
        outs.append(from_state + jnp.dot(scores.astype(BF16), v_ref[:, hs], preferred_element_type=F32))
        st_ref[h] = st_ref[h] * operands[h][6] + state_update
    for h, hs in enumerate(heads):
        o = outs[h]
        ms = jnp.mean(o * o, axis=-1, keepdims=True)
        on = o * lax.rsqrt(ms + RMS_EPS) * gon_ref[:, hs]
        o_ref[:, hs] = (on * og_ref[:, hs].astype(F32)).astype(o_ref.dtype)


def _hgrn2_scan(q, k, v, logf, og, g_onorm, n_batch):
    tp, d = q.shape
    n_chunks = tp // n_batch // CHUNK
    blk = pl.BlockSpec((CHUNK, d), lambda b, c: (b * n_chunks + c, 0))
    return pl.pallas_call(
        _scan_kernel,
        grid=(n_batch, n_chunks),
        in_specs=[blk, blk, blk, blk, blk, pl.BlockSpec((1, d), lambda b, c: (0, 0))],
        out_specs=blk,
        out_shape=jax.ShapeDtypeStruct((tp, d), BF16),
        scratch_shapes=[pltpu.VMEM((N_HEADS, HEAD_DIM, HEAD_DIM), F32)],
        compiler_params=_cparams(("arbitrary", "arbitrary")),
        name="hgrn2_scan",
    )(q, k, v, logf, og, g_onorm)


def _merge_route_kernel(h_ref, on_ref, ap_ref, sgb_ref, wrnn_ref, wo_ref, gffn_ref, wr_ref, br_ref,
                        h1_ref, u2_ref, info_ref, cnt_ref, tri_ref, carry_ref, colcnt_ref):
    tm = h_ref.shape[0]

    @pl.when(pl.program_id(0) == 0)
    def _():
        r_i = lax.broadcasted_iota(jnp.int32, (tm, tm), 0)
        c_i = lax.broadcasted_iota(jnp.int32, (tm, tm), 1)
        tri_ref[...] = (c_i < r_i).astype(BF16)
        carry_ref[...] = jnp.zeros_like(carry_ref)
        colcnt_ref[...] = jnp.zeros_like(colcnt_ref)

    r = jnp.dot(on_ref[...], wrnn_ref[...], preferred_element_type=F32)
    y = ap_ref[...].astype(F32) + sgb_ref[...].astype(F32) * r
    h1 = h_ref[...] + jnp.dot(y.astype(BF16), wo_ref[...], preferred_element_type=F32)
    h1_ref[...] = h1
    ms = jnp.mean(h1 * h1, axis=-1, keepdims=True)
    u2 = h1 * lax.rsqrt(ms + RMS_EPS) * gffn_ref[...]
    _store_packed(u2_ref, u2)
    u_hi, u_lo = _split_bf16(u2)
    w_hi, w_lo = _split_bf16(wr_ref[...])
    logits = (jnp.dot(u_hi, w_hi, preferred_element_type=F32)
              + (jnp.dot(u_lo, w_hi, preferred_element_type=F32) + jnp.dot(u_hi, w_lo, preferred_element_type=F32))
              + br_ref[...])
    lane = lax.broadcasted_iota(jnp.int32, (tm, LANES), 1)
    cur = jnp.where(lane < N_EXPERTS, logits, NEG_BIG)
    vals, idxs, sels = [], [], []
    for _ in range(TOP_K):
        m = jnp.max(cur, axis=-1, keepdims=True)
        idx = jnp.min(jnp.where(cur == m, lane, LANES), axis=-1, keepdims=True)
        sel = lane == idx
        cur = jnp.where(sel, 2.0 * NEG_BIG, cur)
        vals.append(m)
        idxs.append(idx)
        sels.append(sel)
    exps = [jnp.exp(vk - vals[0]) for vk in vals]
    den = exps[0] + exps[1] + exps[2] + exps[3]
    onehot = jnp.logical_or(jnp.logical_or(sels[0], sels[1]), jnp.logical_or(sels[2], sels[3]))
    onehot_bf = onehot.astype(BF16)
    rank_all = jnp.dot(tri_ref[...], onehot_bf, preferred_element_type=F32) + carry_ref[...]
    carry_ref[...] += jnp.sum(onehot.astype(F32), axis=0, keepdims=True)
    colcnt_ref[...] += _tn_dot(onehot_bf, jnp.ones((tm, LANES), BF16))
    info = jnp.zeros((tm, LANES), F32)
    for kk in range(TOP_K):
        rank_k = jnp.sum(jnp.where(sels[kk], rank_all, 0.0), axis=-1, keepdims=True)
        info = jnp.where(lane == kk, exps[kk] / den, info)
        info = jnp.where(lane == TOP_K + kk, idxs[kk].astype(F32), info)
        info = jnp.where(lane == 2 * TOP_K + kk, rank_k, info)
    info_ref[...] = info
    cnt_ref[...] = colcnt_ref[...]


def _merge_route(hp, on, ap, sgb, w_rnn_bf16, w_o_bf16, g_ffn, w_router_pad, b_router_pad):
    tp, d = hp.shape
    tm = ROW_TILE
    row = pl.BlockSpec((tm, d), lambda i: (i, 0))
    vec = _const_spec((1, d))
    return pl.pallas_call(
        _merge_route_kernel,
        grid=(tp // tm,),
        in_specs=[row, row, row, row, _const_spec((d, d)), _const_spec((d, d)), vec,
                  _const_spec((d, LANES)), _const_spec((1, LANES))],
        out_specs=[row, pl.BlockSpec((tm * PACK_ROWS, LANES), lambda i: (i, 0)),
                   pl.BlockSpec((tm, LANES), lambda i: (i, 0)), _const_spec((LANES, LANES))],
        out_shape=[jax.ShapeDtypeStruct((tp, d), F32), jax.ShapeDtypeStruct((tp * PACK_ROWS, LANES), jnp.uint32),
                   jax.ShapeDtypeStruct((tp, LANES), F32), jax.ShapeDtypeStruct((LANES, LANES), F32)],
        scratch_shapes=[pltpu.VMEM((tm, tm), BF16), pltpu.VMEM((1, LANES), F32),
                        pltpu.VMEM((LANES, LANES), F32)],
        compiler_params=_cparams(("arbitrary",)),
        name="merge_route",
    )(hp, on, ap, sgb, w_rnn_bf16, w_o_bf16, g_ffn, w_router_pad, b_router_pad)


def _slots_kernel(info_ref, cnt_ref, dest_ref, te_ref, pstart_ref):
    tm = info_ref.shape[0]
    n_tile_lanes = te_ref.shape[1]

    @pl.when(pl.program_id(0) == 0)
    def _():
        r_i = lax.broadcasted_iota(jnp.int32, (LANES, LANES), 0)
        c_i = lax.broadcasted_iota(jnp.int32, (LANES, LANES), 1)
        cnt = cnt_ref[...]
        tiles = jnp.floor((cnt + (EXPERT_TILE - 1)) / EXPERT_TILE)
        pend_col = jnp.dot((c_i <= r_i).astype(F32), tiles, preferred_element_type=F32,
                           precision=lax.Precision.HIGHEST)
        pstart_ref[...] = _tn_dot_f32(tiles, (r_i < c_i).astype(F32))[0:SUBLANES, :]
        pend_row = _tn_dot_f32(tiles, (r_i <= c_i).astype(F32))[0:1, :]
        tile_id = lax.broadcasted_iota(jnp.int32, (LANES, n_tile_lanes), 1).astype(F32)
        exp_id = lax.broadcasted_iota(jnp.int32, (LANES, n_tile_lanes), 0)
        pend_wide = jnp.concatenate([pend_col] * (n_tile_lanes // LANES), axis=1)
        below = jnp.logical_and(pend_wide <= tile_id, exp_id < N_EXPERTS)
        te = jnp.minimum(jnp.sum(below.astype(F32), axis=0, keepdims=True), N_EXPERTS - 1.0)
        n_active = pend_col[N_EXPERTS - 1:N_EXPERTS, 0:1]
        sub_t = lax.broadcasted_iota(jnp.int32, (SUBLANES, n_tile_lanes), 0)
        pend_wide_row = jnp.concatenate([pend_row] + [jnp.zeros_like(pend_row)] * (n_tile_lanes // LANES - 1), axis=1)
        out = jnp.where(sub_t == 0, jnp.broadcast_to(te, (SUBLANES, n_tile_lanes)),
                        jnp.where(sub_t == 1, jnp.broadcast_to(n_active, (SUBLANES, n_tile_lanes)),
                                  jnp.broadcast_to(pend_wide_row, (SUBLANES, n_tile_lanes))))
        te_ref[...] = out.astype(jnp.int32)

    info = info_ref[...]
    pstart_row = pstart_ref[0:1, :]
    lane = lax.broadcasted_iota(jnp.int32, (tm, LANES), 1)
    dest = jnp.zeros((tm, LANES), F32)
    for kk in range(TOP_K):
        idx_k = info[:, TOP_K + kk:TOP_K + kk + 1].astype(jnp.int32)
        start_k = jnp.sum(jnp.where(lane == idx_k, pstart_row, 0.0), axis=-1, keepdims=True)
        slot_k = start_k * EXPERT_TILE + info[:, 2 * TOP_K + kk:2 * TOP_K + kk + 1]
        dest = jnp.where(lane == kk, slot_k, dest)
    dest_ref[...] = dest.astype(jnp.int32)


def _tn_dot_f32(a, b):
    return lax.dot_general(a, b, (((0,), (0,)), ((), ())), preferred_element_type=F32,
                           precision=lax.Precision.HIGHEST)


def _slots(info, cnt, n_tiles_max):
    tp = info.shape[0]
    tm = tp // SLOTS_STEPS
    n_tile_lanes = -(-n_tiles_max // LANES) * LANES
    return pl.pallas_call(
        _slots_kernel,
        grid=(tp // tm,),
        in_specs=[pl.BlockSpec((tm, LANES), lambda i: (i, 0)), _const_spec((LANES, LANES))],
        out_specs=[pl.BlockSpec((tm, LANES), lambda i: (i, 0)), _const_spec((8, n_tile_lanes))],
        out_shape=[jax.ShapeDtypeStruct((tp, LANES), jnp.int32),
                   jax.ShapeDtypeStruct((8, n_tile_lanes), jnp.int32)],
        scratch_shapes=[pltpu.VMEM((SUBLANES, LANES), F32)],
        compiler_params=_cparams(("arbitrary",)),
        name="slots",
    )(info, cnt)


def _dest_slot(dest_smem, slot):
    return dest_smem.at[pl.ds(pl.multiple_of(slot * (ROUTE_TILE * TOP_K), ROUTE_TILE * TOP_K), ROUTE_TILE * TOP_K)]


def _fetch_dest(dest_hbm, dest_smem, dsem, step, slot):
    return pltpu.make_async_copy(dest_hbm.at[step], _dest_slot(dest_smem, slot), dsem.at[slot])


def _slab(ref, token):
    return ref.at[pl.ds(token * PACK_ROWS, PACK_ROWS), :]


def _dispatch_kernel(pend_ref, dest_hbm, u2_hbm, xs_ref, dest_smem, zero_ref, rows_ref, dsem, sem, zsem, rsem):
    i = pl.program_id(0)
    n = pl.num_programs(0)
    slot = i % 2
    tile_rows = EXPERT_TILE * PACK_ROWS

    def last_tile_zero(e):
        first = pend_ref[e - 1] if e else 0
        dst = xs_ref.at[pl.ds((pend_ref[e] - 1) * tile_rows, tile_rows), :]
        return pend_ref[e] > first, pltpu.make_async_copy(zero_ref, dst, zsem)

    def stage(step):
        s = step % STAGE_SLOTS
        src = u2_hbm.at[pl.ds(step * ROUTE_TILE * PACK_ROWS, ROUTE_TILE * PACK_ROWS), :]
        return pltpu.make_async_copy(src, rows_ref.at[s], rsem.at[s])

    @pl.when(i == 0)
    def _():
        _fetch_dest(dest_hbm, dest_smem, dsem, 0, 0).start()
        stage(0).start()
        zero_ref[...] = jnp.zeros_like(zero_ref)
        for e in range(N_EXPERTS):
            has_tiles, copy = last_tile_zero(e)
            pl.when(has_tiles)(copy.start)
        n_tiles = xs_ref.shape[0] // tile_rows

        def spare_tile(t):
            return pltpu.make_async_copy(zero_ref, xs_ref.at[pl.ds(t * tile_rows, tile_rows), :], zsem)

        lax.fori_loop(pend_ref[N_EXPERTS - 1], n_tiles, lambda t, c: (spare_tile(t).start(), c)[1], 0)
        lax.fori_loop(pend_ref[N_EXPERTS - 1], n_tiles, lambda t, c: (spare_tile(t).wait(), c)[1], 0)
        for e in range(N_EXPERTS):
            has_tiles, copy = last_tile_zero(e)
            pl.when(has_tiles)(copy.wait)

    _fetch_dest(dest_hbm, dest_smem, dsem, i, slot).wait()
    stage(i).wait()

    @pl.when(i + 1 < n)
    def _():
        _fetch_dest(dest_hbm, dest_smem, dsem, i + 1, 1 - slot).start()
        stage(i + 1).start()

    rows = rows_ref.at[i % STAGE_SLOTS]

    dest_base = slot * (ROUTE_TILE * TOP_K)

    def issue(t, carry):
        for kk in range(TOP_K):
            dst = _slab(xs_ref, dest_smem[dest_base + t * TOP_K + kk])
            pltpu.make_async_copy(_slab(rows, t), dst, sem.at[i % STAGE_SLOTS]).start(priority=kk % 2)
        return carry

    lax.fori_loop(0, ROUTE_TILE, issue, 0, unroll=4)

    def drain(step):
        s = step % STAGE_SLOTS
        for _ in range(TOP_K):
            pltpu.make_async_copy(rows_ref.at[s], xs_ref.at[pl.ds(0, ROUTE_TILE * PACK_ROWS), :], sem.at[s]).wait()

    pl.when(i > 0)(lambda: drain(i - 1))
    pl.when(i == n - 1)(lambda: drain(i))


def _dispatch(pend, dest2, u2p, n_slots):
    n_steps = dest2.shape[0]
    grid_spec = pltpu.PrefetchScalarGridSpec(
        num_scalar_prefetch=1,
        grid=(n_steps,),
        in_specs=[pl.BlockSpec(memory_space=pl.ANY), pl.BlockSpec(memory_space=pl.ANY)],
        out_specs=pl.BlockSpec(memory_space=pl.ANY),
        scratch_shapes=[pltpu.SMEM((2 * ROUTE_TILE * TOP_K,), jnp.int32),
                        pltpu.VMEM((EXPERT_TILE * PACK_ROWS, LANES), jnp.uint32),
                        pltpu.VMEM((STAGE_SLOTS, ROUTE_TILE * PACK_ROWS, LANES), jnp.uint32),
                        pltpu.SemaphoreType.DMA((2,)), pltpu.SemaphoreType.DMA((STAGE_SLOTS,)),
                        pltpu.SemaphoreType.DMA, pltpu.SemaphoreType.DMA((STAGE_SLOTS,))],
    )
    return pl.pallas_call(
        _dispatch_kernel,
        grid_spec=grid_spec,
        out_shape=jax.ShapeDtypeStruct((n_slots * PACK_ROWS, LANES), jnp.uint32),
        compiler_params=_cparams(("arbitrary",)),
        name="dispatch",
    )(pend, dest2, u2p)


def _expert_kernel(te_ref, nact_ref, x_ref, wgu_ref, bgu_ref, wdn_ref, bdn_ref, y_ref):
    del te_ref

    @pl.when(pl.program_id(0) < nact_ref[0])
    def _():
        x = _load_packed(x_ref, EXPERT_TILE).astype(BF16)
        gu = jnp.dot(x, wgu_ref[0].astype(BF16), preferred_element_type=F32) + bgu_ref[0]
        gate = jnp.minimum(gu[:, :D_FF], SWIGLU_LIMIT)
        up = jnp.clip(gu[:, D_FF:], -SWIGLU_LIMIT, SWIGLU_LIMIT)
        hdn = (up + 1.0) * (gate * _sigmoid(SWIGLU_ALPHA * gate))
        y = jnp.dot(hdn.astype(BF16), wdn_ref[0].astype(BF16), preferred_element_type=F32) + bdn_ref[0]
        _store_packed(y_ref, y)

    @pl.when(pl.program_id(0) >= nact_ref[0])
    def _():
        y_ref[...] = jnp.zeros_like(y_ref)


def _expert_mlp(te, nact, xs, w_gu, b_gu, w_dn, b_dn):
    d = D_MODEL
    tile_rows = EXPERT_TILE * PACK_ROWS
    n_tiles = xs.shape[0] // tile_rows

    def tile_map(i, te_ref, nact_ref):
        return (jnp.minimum(i, nact_ref[0] - 1), 0)

    def exp_map(i, te_ref, nact_ref):
        return (te_ref[jnp.minimum(i, nact_ref[0] - 1)], 0, 0)

    grid_spec = pltpu.PrefetchScalarGridSpec(
        num_scalar_prefetch=2,
        grid=(n_tiles,),
        in_specs=[pl.BlockSpec((tile_rows, LANES), tile_map),
                  pl.BlockSpec((1, d, 2 * D_FF), exp_map),
                  pl.BlockSpec((1, 1, 2 * D_FF), exp_map),
                  pl.BlockSpec((1, D_FF, d), exp_map),
                  pl.BlockSpec((1, 1, d), exp_map)],
        out_specs=pl.BlockSpec((tile_rows, LANES), lambda i, te_ref, nact_ref: (i, 0)),
    )
    return pl.pallas_call(
        _expert_kernel,
        grid_spec=grid_spec,
        out_shape=jax.ShapeDtypeStruct(xs.shape, jnp.uint32),
        compiler_params=_cparams(("arbitrary",)),
        name="expert_mlp",
    )(te, nact, xs, w_gu, b_gu, w_dn, b_dn)


def _combine_kernel(dest_hbm, h1_ref, info_ref, gfin_ref, ys_hbm, out_ref, dest_smem, ybuf, dsem, sem,
                    *, chunks_per_batch):
    j = pl.program_id(0)
    n = pl.num_programs(0)
    slot = j % 2

    def fetch(step):
        chunk = (step // chunks_per_batch) * (chunks_per_batch + 1) + step % chunks_per_batch + 1
        return pltpu.make_async_copy(dest_hbm.at[chunk], _dest_slot(dest_smem, step % 2), dsem.at[step % 2])

    def issue(step):
        s = step % 2
        dest_base = s * (ROUTE_TILE * TOP_K)

        def body(t, carry):
            for kk in range(TOP_K):
                src = _slab(ys_hbm, dest_smem[dest_base + t * TOP_K + kk])
                pltpu.make_async_copy(src, _slab(ybuf.at[s, kk], t), sem.at[s]).start(priority=kk % 2)
            return carry

        lax.fori_loop(0, ROUTE_TILE, body, 0, unroll=4)

    @pl.when(j == 0)
    def _():
        fetch(0).start()
        fetch(0).wait()
        issue(0)
        fetch(1).start()

    @pl.when(j + 1 < n)
    def _():
        fetch(j + 1).wait()
        issue(j + 1)

    @pl.when(j + 2 < n)
    def _():
        fetch(j + 2).start()

    for kk in range(TOP_K):
        pltpu.make_async_copy(ys_hbm.at[pl.ds(0, ROUTE_TILE * PACK_ROWS), :], ybuf.at[slot, kk], sem.at[slot]).wait()
    info = info_ref[...]
    h2 = h1_ref[...]
    for kk in range(TOP_K):
        h2 = h2 + info[:, kk:kk + 1] * _load_packed(ybuf, ROUTE_TILE, lead=(slot, kk))
    ms = jnp.mean(h2 * h2, axis=-1, keepdims=True)
    out_ref[...] = h2 * lax.rsqrt(ms + RMS_EPS) * gfin_ref[...]


def _combine(dest2, h1, info, g_final, ys, n_batch, seq):
    d = h1.shape[1]
    cpb = seq // ROUTE_TILE

    def padded_chunk(j):
        return ((j // cpb) * (cpb + 1) + j % cpb + 1, 0)

    return pl.pallas_call(
        functools.partial(_combine_kernel, chunks_per_batch=cpb),
        grid=(n_batch * cpb,),
        in_specs=[pl.BlockSpec(memory_space=pl.ANY),
                  pl.BlockSpec((ROUTE_TILE, d), padded_chunk),
                  pl.BlockSpec((ROUTE_TILE, LANES), padded_chunk),
                  pl.BlockSpec((1, d), lambda j: (0, 0)),
                  pl.BlockSpec(memory_space=pl.ANY)],
        out_specs=pl.BlockSpec((ROUTE_TILE, d), lambda j: (j, 0)),
        out_shape=jax.ShapeDtypeStruct((n_batch * seq, d), F32),
        scratch_shapes=[pltpu.SMEM((2 * ROUTE_TILE * TOP_K,), jnp.int32),
                        pltpu.VMEM((2, TOP_K, ROUTE_TILE * PACK_ROWS, LANES), jnp.uint32),
                        pltpu.SemaphoreType.DMA((2,)), pltpu.SemaphoreType.DMA((2,))],
        compiler_params=_cparams(("arbitrary",)),
        name="combine",
    )(dest2, h1, info, g_final, ys)


def kernel(x, meta_tokens, lb_logits, g_mix, w_in, w_dw, b_dw, ln_g, ln_b, w_conv_out, b_conv_out,
           g_onorm, w_rnn_out, w_o, g_ffn, w_router, b_router, w_gate_up, b_gate_up, w_down, b_down,
           g_final):
    n_batch, seq, d = x.shape
    assert d == D_MODEL and w_in.shape[0] == 1, "single-layer block with D_MODEL features"
    assert CHUNK == ROUTE_TILE and seq % CHUNK == 0
    seq_pad = CHUNK + seq
    tp = n_batch * seq_pad
    assert tp % ROW_TILE == 0

    meta = jnp.broadcast_to(meta_tokens.astype(x.dtype)[None], (n_batch, N_META, d))
    hp = jnp.concatenate([jnp.zeros((n_batch, CHUNK_PAD, d), x.dtype), meta, x], axis=1).reshape(tp, d)
    vec = lambda a: a.reshape(1, -1).astype(F32)

    aglu, q, k, logf, v, og, sga, sgb = _in_proj(hp, vec(g_mix[0]), lb_logits.astype(F32), w_in[0].astype(BF16))
    ap = _conv_branch(aglu, w_dw[0].astype(F32), vec(b_dw[0]), vec(ln_g[0]), vec(ln_b[0]),
                      w_conv_out[0].astype(BF16), vec(b_conv_out[0]), sga)
    on = _hgrn2_scan(q, k, v, logf, og, vec(g_onorm[0]), n_batch)
    w_router_pad = jnp.pad(w_router[0].astype(F32), ((0, 0), (0, LANES - N_EXPERTS)))
    b_router_pad = jnp.pad(vec(b_router[0]), ((0, 0), (0, LANES - N_EXPERTS)))
    h1, u2p, info, cnt = _merge_route(hp, on, ap, sgb, w_rnn_out[0].astype(BF16), w_o[0].astype(BF16),
                                      vec(g_ffn[0]), w_router_pad, b_router_pad)

    n_tiles_max = -(-(tp * TOP_K + N_EXPERTS * (EXPERT_TILE - 1)) // EXPERT_TILE)
    dest, te = _slots(info, cnt, n_tiles_max)
    dest2 = dest[:, :TOP_K].reshape(tp // ROUTE_TILE, ROUTE_TILE * TOP_K)
    xs = _dispatch(te[2, :N_EXPERTS], dest2, u2p, n_tiles_max * EXPERT_TILE)
    ys = _expert_mlp(te[0, :n_tiles_max], te[1, :1], xs, w_gate_up[0],
                     b_gate_up[0].reshape(N_EXPERTS, 1, -1).astype(F32), w_down[0],
                     b_down[0].reshape(N_EXPERTS, 1, -1).astype(F32))
    out = _combine(dest2, h1, info, vec(g_final), ys, n_batch, seq)
    return out.reshape(n_batch, seq, d)
```

```python
import functools

import jax
import jax.numpy as jnp
from jax import lax
from jax.experimental import pallas as pl
from jax.experimental.pallas import tpu as pltpu
from jax.experimental.pallas import tpu_sc as plsc

F32 = jnp.float32
BF16 = jnp.bfloat16

D_MODEL = 1024
N_META = 16
CHUNK = 128
CHUNK_PAD = CHUNK - N_META
CONV_WIDTH = 31
HEAD_DIM = 128
N_HEADS = D_MODEL // HEAD_DIM
N_EXPERTS = 32
TOP_K = 4
D_FF = D_MODEL
SWIGLU_LIMIT = 7.0
SWIGLU_ALPHA = 1.702
RMS_EPS = 1e-6
LN_EPS = 1e-5

LANES = 128
SUBLANES = 8
SUB_BLOCK = 32
N_SUB = CHUNK // SUB_BLOCK
HALO = 32
ROW_TILE = 640
IN_TILE = 320
SLOTS_STEPS = 5
CONV_ROWS = 64
EXPERT_TILE = 512
SC_CHUNK = 104
COMBINE_TILE = 128
NEG_BIG = -1e30
VMEM_LIMIT = 56 * 1024 * 1024


def _sigmoid(x):
    return 1.0 / (1.0 + jnp.exp(-x))


def _cparams(sem):
    return pltpu.CompilerParams(dimension_semantics=sem, vmem_limit_bytes=VMEM_LIMIT)


def _const_spec(shape):
    nd = len(shape)
    return pl.BlockSpec(shape, lambda *_: (0,) * nd)


PACK_WORDS = D_MODEL // 2
HIGH_HALF = 0xFFFF0000


def _pack_rows(x):
    lo = lax.bitcast_convert_type(x[:, :PACK_WORDS].astype(BF16).astype(F32), jnp.uint32)
    hi = lax.bitcast_convert_type(x[:, PACK_WORDS:].astype(BF16).astype(F32), jnp.uint32)
    return (lo >> 16) | (hi & jnp.uint32(HIGH_HALF))


def _unpack_rows(w):
    lo = lax.bitcast_convert_type(w << 16, F32)
    hi = lax.bitcast_convert_type(w & jnp.uint32(HIGH_HALF), F32)
    return jnp.concatenate([lo, hi], axis=1)


def _in_proj_kernel(h_ref, g_ref, lbl_ref, w_ref, aglu_ref, q_ref, k_ref, logf_ref, v_ref,
                    og_ref, sga_ref, sgb_ref):
    d = D_MODEL
    h = h_ref[...]
    ms = jnp.mean(h * h, axis=-1, keepdims=True)
    u = (h * lax.rsqrt(ms + RMS_EPS) * g_ref[...]).astype(BF16)

    def proj(j):
        return jnp.dot(u, w_ref[:, j * d:(j + 1) * d], preferred_element_type=F32)

    aglu_ref[...] = (proj(0) * _sigmoid(proj(1))).astype(aglu_ref.dtype)
    zq = proj(2)
    q_ref[...] = (zq * _sigmoid(zq)).astype(q_ref.dtype)
    lbl = lbl_ref[...]
    e = jnp.exp(lbl - jnp.max(lbl, axis=0, keepdims=True))
    lb = e[0:1, :] / jnp.sum(e, axis=0, keepdims=True)
    s = _sigmoid(proj(3))
    logf_ref[...] = jnp.log(lb + (1.0 - lb) * s)
    k_ref[...] = ((1.0 - lb) * (1.0 - s)).astype(k_ref.dtype)
    v_ref[...] = proj(4).astype(v_ref.dtype)
    zg = proj(5)
    og_ref[...] = (zg * _sigmoid(zg)).astype(og_ref.dtype)
    sga_ref[...] = _sigmoid(proj(6)).astype(sga_ref.dtype)
    sgb_ref[...] = _sigmoid(proj(7)).astype(sgb_ref.dtype)


def _in_proj(hp, g_mix, lb_logits, w_in_bf16):
    tp, d = hp.shape
    tm = IN_TILE
    row = pl.BlockSpec((tm, d), lambda i: (i, 0))
    out_dtypes = [BF16, BF16, BF16, F32, BF16, BF16, BF16, BF16]
    return pl.pallas_call(
        _in_proj_kernel,
        grid=(tp // tm,),
        in_specs=[row, _const_spec((1, d)), _const_spec(lb_logits.shape),
                  pl.BlockSpec(w_in_bf16.shape, lambda i: (0, 0), pipeline_mode=pl.Buffered(1))],
        out_specs=[row] * 8,
        out_shape=[jax.ShapeDtypeStruct((tp, d), dt) for dt in out_dtypes],
        compiler_params=_cparams(("parallel",)),
        name="in_proj",
    )(hp, g_mix, lb_logits, w_in_bf16)


def _conv_kernel(halo_ref, cur_ref, wdw_ref, bdw_ref, lng_ref, lnb_ref, wout_ref, bout_ref, sga_ref,
                 out_ref, win_ref, acc_ref, shift_ref):
    tm = cur_ref.shape[0]
    win_ref[0:HALO, :] = halo_ref[...].astype(F32)
    win_ref[HALO:, :] = cur_ref[...].astype(F32)
    first_tap = HALO - (CONV_WIDTH - 1)
    for c in range(D_MODEL // LANES):
        lanes = slice(c * LANES, (c + 1) * LANES)
        for s in range(SUBLANES):
            n_rows = tm + HALO - (SUBLANES if s else 0)
            shift_ref[s, 0:n_rows, :] = win_ref[pl.ds(s, n_rows), lanes]
        w_c = wdw_ref[:, lanes]
        bias = jnp.broadcast_to(bdw_ref[:, lanes], (CONV_ROWS, LANES))

        def chunk(r, carry, lanes=lanes, w_c=w_c, bias=bias):
            r0 = pl.multiple_of(r * CONV_ROWS, CONV_ROWS)
            acc = bias
            for j in range(CONV_WIDTH):
                off = first_tap + j
                rows = pl.ds(r0 + off - off % SUBLANES, CONV_ROWS)
                acc = acc + w_c[j:j + 1, :] * shift_ref[off % SUBLANES, rows, :]
            acc_ref[pl.ds(r0, CONV_ROWS), lanes] = acc
            return carry

        lax.fori_loop(0, tm // CONV_ROWS, chunk, 0)
    a = acc_ref[...]
    mu = jnp.mean(a, axis=-1, keepdims=True)
    ac = a - mu
    var = jnp.mean(ac * ac, axis=-1, keepdims=True)
    y = ac * lax.rsqrt(var + LN_EPS) * lng_ref[...] + lnb_ref[...]
    y = y * _sigmoid(y)
    o = jnp.dot(y.astype(BF16), wout_ref[...], preferred_element_type=F32) + bout_ref[...]
    out_ref[...] = (sga_ref[...].astype(F32) * o).astype(out_ref.dtype)


def _conv_branch(aglu, w_dw, b_dw, ln_g, ln_b, w_out_bf16, b_out, sga):
    tp, d = aglu.shape
    tm = ROW_TILE
    per = tm // HALO
    row = pl.BlockSpec((tm, d), lambda i: (i, 0))
    halo = pl.BlockSpec((HALO, d), lambda i: (jnp.maximum(i * per - 1, 0), 0))
    vec = _const_spec((1, d))
    return pl.pallas_call(
        _conv_kernel,
        grid=(tp // tm,),
        in_specs=[halo, row, _const_spec(w_dw.shape), vec, vec, vec, _const_spec((d, d)), vec, row],
        out_specs=row,
        out_shape=jax.ShapeDtypeStruct((tp, d), BF16),
        scratch_shapes=[pltpu.VMEM((tm + HALO, d), F32), pltpu.VMEM((tm, d), F32),
                        pltpu.VMEM((SUBLANES, tm + HALO, LANES), F32)],
        compiler_params=_cparams(("parallel",)),
        name="conv_branch",
    )(aglu, aglu, w_dw, b_dw, ln_g, ln_b, w_out_bf16, b_out, sga)


def _split_bf16(x):
    hi = x.astype(BF16)
    return hi, (x - hi.astype(F32)).astype(BF16)


def _nt_dot(a, b):
    return lax.dot_general(a, b, (((1,), (1,)), ((), ())), preferred_element_type=F32)


def _tn_dot(a, b):
    return lax.dot_general(a, b, (((0,), (0,)), ((), ())), preferred_element_type=F32)


def _scan_kernel(q_ref, k_ref, v_ref, lf_ref, og_ref, gon_ref, o_ref, st_ref):
    @pl.when(pl.program_id(1) == 0)
    def _():
        st_ref[...] = jnp.zeros_like(st_ref)

    c = CHUNK
    row = lax.broadcasted_iota(jnp.int32, (c, c), 0)
    col = lax.broadcasted_iota(jnp.int32, (c, c), 1)
    causal = col <= row
    diag_mask = jnp.logical_and(causal, row // SUB_BLOCK == col // SUB_BLOCK)
    lf = lf_ref[...]
    lf_hi = lf.astype(BF16)
    lf_mid, lf_lo = _split_bf16(lf - lf_hi.astype(F32))
    tri = causal.astype(BF16)
    bcum = (jnp.dot(tri, lf_hi, preferred_element_type=F32)
            + (jnp.dot(tri, lf_mid, preferred_element_type=F32) + jnp.dot(tri, lf_lo, preferred_element_type=F32)))

    def bcast_rows(rows):
        return jnp.concatenate([jnp.broadcast_to(r, (SUB_BLOCK, HEAD_DIM)) for r in rows], axis=0)

    heads = [slice(h * HEAD_DIM, (h + 1) * HEAD_DIM) for h in range(N_HEADS)]
    operands = []
    for hs in heads:
        b = bcum[:, hs]
        q = q_ref[:, hs].astype(F32)
        k = k_ref[:, hs].astype(F32)
        ends = [b[i * SUB_BLOCK + SUB_BLOCK - 1:i * SUB_BLOCK + SUB_BLOCK, :] for i in range(N_SUB)]
        mids = [b[i * SUB_BLOCK + SUB_BLOCK // 2 - 1:i * SUB_BLOCK + SUB_BLOCK // 2, :] for i in range(N_SUB)]
        mid_full = bcast_rows(mids)
        b_last = ends[-1]
        zero_row = jnp.zeros_like(b_last)
        qm = q * jnp.exp(b - mid_full)
        km = k * jnp.exp(mid_full - b)
        q_parts, k_parts = [], []
        for j in range(N_SUB - 1):
            q_rows = [jnp.exp(mids[i] - ends[j]) if i > j else zero_row for i in range(N_SUB)]
            k_rows = [jnp.exp(ends[j] - mids[j]) if i == j else zero_row for i in range(N_SUB)]
            q_parts.append((qm * bcast_rows(q_rows)).astype(BF16))
            k_parts.append((km * bcast_rows(k_rows)).astype(BF16))
        q_in = (qm * bcast_rows([jnp.exp(m) for m in mids])).astype(BF16)
        k_out = (km * bcast_rows([jnp.exp(b_last - m) for m in mids])).astype(BF16)
        operands.append((qm.astype(BF16), km.astype(BF16), jnp.concatenate(q_parts, axis=1),
                         jnp.concatenate(k_parts, axis=1), q_in, k_out, jnp.exp(b_last)))
    products = []
    for h, hs in enumerate(heads):
        qm, km, q_cat, k_cat, q_in, k_out, _ = operands[h]
        products.append((_nt_dot(qm, km), _nt_dot(q_cat, k_cat), _nt_dot(q_in, st_ref[h].astype(BF16)),
                         _tn_dot(v_ref[:, hs], k_out)))
    outs = []
    for h, hs in enumerate(heads):
        same_block, earlier_blocks, from_state, state_update = products[h]
        scores = jnp.where(diag_mask, same_block, 0.0) + earlier_blocks
        outs.append(from_state + jnp.dot(scores.astype(BF16), v_ref[:, hs], preferred_element_type=F32))
        st_ref[h] = st_ref[h] * operands[h][6] + state_update
    for h, hs in enumerate(heads):
        o = outs[h]
        ms = jnp.mean(o * o, axis=-1, keepdims=True)
        on = o * lax.rsqrt(ms + RMS_EPS) * gon_ref[:, hs]
        o_ref[:, hs] = (on * og_ref[:, hs].astype(F32)).astype(o_ref.dtype)


def _hgrn2_scan(q, k, v, logf, og, g_onorm, n_batch):
    tp, d = q.shape
    n_chunks = tp // n_batch // CHUNK
    blk = pl.BlockSpec((CHUNK, d), lambda b, c: (b * n_chunks + c, 0))
    return pl.pallas_call(
        _scan_kernel,
        grid=(n_batch, n_chunks),
        in_specs=[blk, blk, blk, blk, blk, pl.BlockSpec((1, d), lambda b, c: (0, 0))],
        out_specs=blk,
        out_shape=jax.ShapeDtypeStruct((tp, d), BF16),
        scratch_shapes=[pltpu.VMEM((N_HEADS, HEAD_DIM, HEAD_DIM), F32)],
        compiler_params=_cparams(("arbitrary", "arbitrary")),
        name="hgrn2_scan",
    )(q, k, v, logf, og, g_onorm)


def _merge_route_kernel(h_ref, on_ref, ap_ref, sgb_ref, wrnn_ref, wo_ref, gffn_ref, wr_ref, br_ref,
                        h1_ref, u2_ref, info_ref, cnt_ref, tri_ref, carry_ref, colcnt_ref):
    tm = h_ref.shape[0]

    @pl.when(pl.program_id(0) == 0)
    def _():
        r_i = lax.broadcasted_iota(jnp.int32, (tm, tm), 0)
        c_i = lax.broadcasted_iota(jnp.int32, (tm, tm), 1)
        tri_ref[...] = (c_i < r_i).astype(BF16)
        carry_ref[...] = jnp.zeros_like(carry_ref)
        colcnt_ref[...] = jnp.zeros_like(colcnt_ref)

    r = jnp.dot(on_ref[...], wrnn_ref[...], preferred_element_type=F32)
    y = ap_ref[...].astype(F32) + sgb_ref[...].astype(F32) * r
    h1 = h_ref[...] + jnp.dot(y.astype(BF16), wo_ref[...], preferred_element_type=F32)
    h1_ref[...] = h1
    ms = jnp.mean(h1 * h1, axis=-1, keepdims=True)
    u2 = h1 * lax.rsqrt(ms + RMS_EPS) * gffn_ref[...]
    u2_ref[...] = _pack_rows(u2)
    u_hi, u_lo = _split_bf16(u2)
    w_hi, w_lo = _split_bf16(wr_ref[...])
    logits = (jnp.dot(u_hi, w_hi, preferred_element_type=F32)
              + (jnp.dot(u_lo, w_hi, preferred_element_type=F32) + jnp.dot(u_hi, w_lo, preferred_element_type=F32))
              + br_ref[...])
    lane = lax.broadcasted_iota(jnp.int32, (tm, LANES), 1)
    cur = jnp.where(lane < N_EXPERTS, logits, NEG_BIG)
    vals, idxs, sels = [], [], []
    for _ in range(TOP_K):
        m = jnp.max(cur, axis=-1, keepdims=True)
        idx = jnp.min(jnp.where(cur == m, lane, LANES), axis=-1, keepdims=True)
        sel = lane == idx
        cur = jnp.where(sel, 2.0 * NEG_BIG, cur)
        vals.append(m)
        idxs.append(idx)
        sels.append(sel)
    exps = [jnp.exp(vk - vals[0]) for vk in vals]
    den = exps[0] + exps[1] + exps[2] + exps[3]
    onehot = jnp.logical_or(jnp.logical_or(sels[0], sels[1]), jnp.logical_or(sels[2], sels[3]))
    onehot_bf = onehot.astype(BF16)
    rank_all = jnp.dot(tri_ref[...], onehot_bf, preferred_element_type=F32) + carry_ref[...]
    carry_ref[...] += jnp.sum(onehot.astype(F32), axis=0, keepdims=True)
    colcnt_ref[...] += _tn_dot(onehot_bf, jnp.ones((tm, LANES), BF16))
    info = jnp.zeros((tm, LANES), F32)
    for kk in range(TOP_K):
        rank_k = jnp.sum(jnp.where(sels[kk], rank_all, 0.0), axis=-1, keepdims=True)
        info = jnp.where(lane == kk, exps[kk] / den, info)
        info = jnp.where(lane == TOP_K + kk, idxs[kk].astype(F32), info)
        info = jnp.where(lane == 2 * TOP_K + kk, rank_k, info)
    info_ref[...] = info
    cnt_ref[...] = colcnt_ref[...]


def _merge_route(hp, on, ap, sgb, w_rnn_bf16, w_o_bf16, g_ffn, w_router_pad, b_router_pad):
    tp, d = hp.shape
    tm = ROW_TILE
    row = pl.BlockSpec((tm, d), lambda i: (i, 0))
    vec = _const_spec((1, d))
    return pl.pallas_call(
        _merge_route_kernel,
        grid=(tp // tm,),
        in_specs=[row, row, row, row, _const_spec((d, d)), _const_spec((d, d)), vec,
                  _const_spec((d, LANES)), _const_spec((1, LANES))],
        out_specs=[row, pl.BlockSpec((tm, PACK_WORDS), lambda i: (i, 0)),
                   pl.BlockSpec((tm, LANES), lambda i: (i, 0)), _const_spec((LANES, LANES))],
        out_shape=[jax.ShapeDtypeStruct((tp, d), F32), jax.ShapeDtypeStruct((tp, PACK_WORDS), jnp.uint32),
                   jax.ShapeDtypeStruct((tp, LANES), F32), jax.ShapeDtypeStruct((LANES, LANES), F32)],
        scratch_shapes=[pltpu.VMEM((tm, tm), BF16), pltpu.VMEM((1, LANES), F32),
                        pltpu.VMEM((LANES, LANES), F32)],
        compiler_params=_cparams(("arbitrary",)),
        name="merge_route",
    )(hp, on, ap, sgb, w_rnn_bf16, w_o_bf16, g_ffn, w_router_pad, b_router_pad)


def _slots_kernel(info_ref, cnt_ref, dest_ref, te_ref, pstart_ref):
    tm = info_ref.shape[0]
    n_tile_lanes = te_ref.shape[1]

    @pl.when(pl.program_id(0) == 0)
    def _():
        r_i = lax.broadcasted_iota(jnp.int32, (LANES, LANES), 0)
        c_i = lax.broadcasted_iota(jnp.int32, (LANES, LANES), 1)
        cnt = cnt_ref[...]
        tiles = jnp.floor((cnt + (EXPERT_TILE - 1)) / EXPERT_TILE)
        pend_col = jnp.dot((c_i <= r_i).astype(F32), tiles, preferred_element_type=F32,
                           precision=lax.Precision.HIGHEST)
        pstart_ref[...] = _tn_dot_f32(tiles, (r_i < c_i).astype(F32))[0:SUBLANES, :]
        tile_id = lax.broadcasted_iota(jnp.int32, (LANES, n_tile_lanes), 1).astype(F32)
        exp_id = lax.broadcasted_iota(jnp.int32, (LANES, n_tile_lanes), 0)
        pend_wide = jnp.concatenate([pend_col] * (n_tile_lanes // LANES), axis=1)
        below = jnp.logical_and(pend_wide <= tile_id, exp_id < N_EXPERTS)
        te = jnp.minimum(jnp.sum(below.astype(F32), axis=0, keepdims=True), N_EXPERTS - 1.0)
        n_active = pend_col[N_EXPERTS - 1:N_EXPERTS, 0:1]
        sub_t = lax.broadcasted_iota(jnp.int32, (SUBLANES, n_tile_lanes), 0)
        out = jnp.where(sub_t == 0, jnp.broadcast_to(te, (SUBLANES, n_tile_lanes)),
                        jnp.broadcast_to(n_active, (SUBLANES, n_tile_lanes)))
        te_ref[...] = out.astype(jnp.int32)

    info = info_ref[...]
    pstart_row = pstart_ref[0:1, :]
    lane = lax.broadcasted_iota(jnp.int32, (tm, LANES), 1)
    dest = jnp.zeros((tm, LANES), F32)
    for kk in range(TOP_K):
        idx_k = info[:, TOP_K + kk:TOP_K + kk + 1].astype(jnp.int32)
        start_k = jnp.sum(jnp.where(lane == idx_k, pstart_row, 0.0), axis=-1, keepdims=True)
        slot_k = start_k * EXPERT_TILE + info[:, 2 * TOP_K + kk:2 * TOP_K + kk + 1]
        dest = jnp.where(lane == kk, slot_k, dest)
    dest_ref[...] = dest.astype(jnp.int32)


def _tn_dot_f32(a, b):
    return lax.dot_general(a, b, (((0,), (0,)), ((), ())), preferred_element_type=F32,
                           precision=lax.Precision.HIGHEST)


def _slots(info, cnt, n_tiles_max):
    tp = info.shape[0]
    tm = tp // SLOTS_STEPS
    n_tile_lanes = -(-n_tiles_max // LANES) * LANES
    return pl.pallas_call(
        _slots_kernel,
        grid=(tp // tm,),
        in_specs=[pl.BlockSpec((tm, LANES), lambda i: (i, 0)), _const_spec((LANES, LANES))],
        out_specs=[pl.BlockSpec((tm, LANES), lambda i: (i, 0)), _const_spec((8, n_tile_lanes))],
        out_shape=[jax.ShapeDtypeStruct((tp, LANES), jnp.int32),
                   jax.ShapeDtypeStruct((8, n_tile_lanes), jnp.int32)],
        scratch_shapes=[pltpu.VMEM((SUBLANES, LANES), F32)],
        compiler_params=_cparams(("arbitrary",)),
        name="slots",
    )(info, cnt)


def _sc_workers():
    sc = plsc.get_sparse_core_info()
    return sc.num_cores, sc.num_cores * sc.num_subcores


def _sc_token_chunks(tp, body):
    n_cores, n_workers = _sc_workers()
    per_worker = tp // n_workers
    assert per_worker * n_workers == tp and per_worker % SC_CHUNK == 0 and SC_CHUNK % SUBLANES == 0
    base = (lax.axis_index("s") * n_cores + lax.axis_index("c")) * per_worker

    @pl.loop(0, per_worker // SC_CHUNK)
    def _(ci):
        body(pl.multiple_of(base + ci * SC_CHUNK, SUBLANES))


def _sc_kernel(out_type):
    return functools.partial(
        pl.kernel, mesh=plsc.VectorSubcoreMesh(core_axis_name="c", subcore_axis_name="s"), out_type=out_type,
        scratch_types=[pltpu.VMEM((SC_CHUNK,), jnp.int32), pltpu.VMEM((SC_CHUNK, PACK_WORDS), jnp.uint32)])


def _dispatch(dest_flat, u2p, n_slots):
    tp = u2p.shape[0]

    @_sc_kernel(jax.ShapeDtypeStruct((n_slots, PACK_WORDS), jnp.uint32))
    def scatter_rows(u2_hbm, dest_hbm, xs_hbm, idx_v, rows_v):
        def chunk(off):
            pltpu.sync_copy(u2_hbm.at[pl.ds(off, SC_CHUNK)], rows_v)
            for kk in range(TOP_K):
                pltpu.sync_copy(dest_hbm.at[pl.ds(kk * tp + off, SC_CHUNK)], idx_v)
                pltpu.sync_copy(rows_v, xs_hbm.at[idx_v])

        _sc_token_chunks(tp, chunk)

    return scatter_rows(u2p, dest_flat)


def _gather_expert_rows(dest_flat, ys):
    tp = dest_flat.shape[0] // TOP_K

    @_sc_kernel(jax.ShapeDtypeStruct((TOP_K, tp, PACK_WORDS), jnp.uint32))
    def gather_rows(ys_hbm, dest_hbm, out_hbm, idx_v, rows_v):
        def chunk(off):
            for kk in range(TOP_K):
                pltpu.sync_copy(dest_hbm.at[pl.ds(kk * tp + off, SC_CHUNK)], idx_v)
                pltpu.sync_copy(ys_hbm.at[idx_v], rows_v)
                pltpu.sync_copy(rows_v, out_hbm.at[kk, pl.ds(off, SC_CHUNK)])

        _sc_token_chunks(tp, chunk)

    return gather_rows(ys, dest_flat)


def _expert_kernel(te_ref, nact_ref, x_ref, wgu_ref, bgu_ref, wdn_ref, bdn_ref, y_ref):
    del te_ref

    @pl.when(pl.program_id(0) < nact_ref[0])
    def _():
        x = _unpack_rows(x_ref[...]).astype(BF16)
        gu = jnp.dot(x, wgu_ref[0].astype(BF16), preferred_element_type=F32) + bgu_ref[0]
        gate = jnp.minimum(gu[:, :D_FF], SWIGLU_LIMIT)
        up = jnp.clip(gu[:, D_FF:], -SWIGLU_LIMIT, SWIGLU_LIMIT)
        hdn = (up + 1.0) * (gate * _sigmoid(SWIGLU_ALPHA * gate))
        y = jnp.dot(hdn.astype(BF16), wdn_ref[0].astype(BF16), preferred_element_type=F32) + bdn_ref[0]
        y_ref[...] = _pack_rows(y)

    @pl.when(pl.program_id(0) >= nact_ref[0])
    def _():
        y_ref[...] = jnp.zeros_like(y_ref)


def _expert_mlp(te, nact, xs, w_gu, b_gu, w_dn, b_dn):
    d = D_MODEL
    n_tiles = xs.shape[0] // EXPERT_TILE

    def tile_map(i, te_ref, nact_ref):
        return (jnp.minimum(i, nact_ref[0] - 1), 0)

    def exp_map(i, te_ref, nact_ref):
        return (te_ref[jnp.minimum(i, nact_ref[0] - 1)], 0, 0)

    grid_spec = pltpu.PrefetchScalarGridSpec(
        num_scalar_prefetch=2,
        grid=(n_tiles,),
        in_specs=[pl.BlockSpec((EXPERT_TILE, PACK_WORDS), tile_map),
                  pl.BlockSpec((1, d, 2 * D_FF), exp_map),
                  pl.BlockSpec((1, 1, 2 * D_FF), exp_map),
                  pl.BlockSpec((1, D_FF, d), exp_map),
                  pl.BlockSpec((1, 1, d), exp_map)],
        out_specs=pl.BlockSpec((EXPERT_TILE, PACK_WORDS), lambda i, te_ref, nact_ref: (i, 0)),
    )
    return pl.pallas_call(
        _expert_kernel,
        grid_spec=grid_spec,
        out_shape=jax.ShapeDtypeStruct(xs.shape, jnp.uint32),
        compiler_params=_cparams(("arbitrary",)),
        name="expert_mlp",
    )(te, nact, xs, w_gu, b_gu, w_dn, b_dn)


def _combine_kernel(h1_ref, info_ref, gfin_ref, yg_ref, out_ref):
    info = info_ref[...]
    h2 = h1_ref[...]
    for kk in range(TOP_K):
        h2 = h2 + info[:, kk:kk + 1] * _unpack_rows(yg_ref[kk])
    ms = jnp.mean(h2 * h2, axis=-1, keepdims=True)
    out_ref[...] = h2 * lax.rsqrt(ms + RMS_EPS) * gfin_ref[...]


def _combine(h1, info, g_final, yg, n_batch, seq):
    d = h1.shape[1]
    tm = COMBINE_TILE
    tiles = seq // tm
    skip = CHUNK // tm

    def padded(j):
        return (j // tiles) * (tiles + skip) + j % tiles + skip

    return pl.pallas_call(
        _combine_kernel,
        grid=(n_batch * tiles,),
        in_specs=[pl.BlockSpec((tm, d), lambda j: (padded(j), 0)),
                  pl.BlockSpec((tm, LANES), lambda j: (padded(j), 0)),
                  pl.BlockSpec((1, d), lambda j: (0, 0)),
                  pl.BlockSpec((TOP_K, tm, PACK_WORDS), lambda j: (0, padded(j), 0))],
        out_specs=pl.BlockSpec((tm, d), lambda j: (j, 0)),
        out_shape=jax.ShapeDtypeStruct((n_batch * seq, d), F32),
        compiler_params=_cparams(("parallel",)),
        name="combine",
    )(h1, info, g_final, yg)


def kernel(x, meta_tokens, lb_logits, g_mix, w_in, w_dw, b_dw, ln_g, ln_b, w_conv_out, b_conv_out,
           g_onorm, w_rnn_out, w_o, g_ffn, w_router, b_router, w_gate_up, b_gate_up, w_down, b_down,
           g_final):
    n_batch, seq, d = x.shape
    assert d == D_MODEL and w_in.shape[0] == 1, "single-layer block with D_MODEL features"
    assert seq % CHUNK == 0 and CHUNK % COMBINE_TILE == 0
    seq_pad = CHUNK + seq
    tp = n_batch * seq_pad
    assert tp % ROW_TILE == 0

    meta = jnp.broadcast_to(meta_tokens.astype(x.dtype)[None], (n_batch, N_META, d))
    hp = jnp.concatenate([jnp.zeros((n_batch, CHUNK_PAD, d), x.dtype), meta, x], axis=1).reshape(tp, d)
    vec = lambda a: a.reshape(1, -1).astype(F32)

    aglu, q, k, logf, v, og, sga, sgb = _in_proj(hp, vec(g_mix[0]), lb_logits.astype(F32), w_in[0].astype(BF16))
    ap = _conv_branch(aglu, w_dw[0].astype(F32), vec(b_dw[0]), vec(ln_g[0]), vec(ln_b[0]),
                      w_conv_out[0].astype(BF16), vec(b_conv_out[0]), sga)
    on = _hgrn2_scan(q, k, v, logf, og, vec(g_onorm[0]), n_batch)
    w_router_pad = jnp.pad(w_router[0].astype(F32), ((0, 0), (0, LANES - N_EXPERTS)))
    b_router_pad = jnp.pad(vec(b_router[0]), ((0, 0), (0, LANES - N_EXPERTS)))
    h1, u2p, info, cnt = _merge_route(hp, on, ap, sgb, w_rnn_out[0].astype(BF16), w_o[0].astype(BF16),
                                      vec(g_ffn[0]), w_router_pad, b_router_pad)

    n_tiles_max = -(-(tp * TOP_K + N_EXPERTS * (EXPERT_TILE - 1)) // EXPERT_TILE)
    dest, te = _slots(info, cnt, n_tiles_max)
    dest_flat = dest[:, :TOP_K].T.reshape(-1)
    xs = _dispatch(dest_flat, u2p, n_tiles_max * EXPERT_TILE)
    ys = _expert_mlp(te[0, :n_tiles_max], te[1, :1], xs, w_gate_up[0],
                     b_gate_up[0].reshape(N_EXPERTS, 1, -1).astype(F32), w_down[0],
                     b_down[0].reshape(N_EXPERTS, 1, -1).astype(F32))
    yg = _gather_expert_rows(dest_flat, ys)
    out = _combine(h1, info, vec(g_final), yg, n_batch, seq)
    return out.reshape(n_batch, seq, d)
```

```python
import functools

import jax
import jax.numpy as jnp
from jax import lax
from jax.experimental import pallas as pl
from jax.experimental.pallas import tpu as pltpu
from jax.experimental.pallas import tpu_sc as plsc

F32 = jnp.float32
BF16 = jnp.bfloat16

D_MODEL = 1024
N_META = 16
CHUNK = 128
CHUNK_PAD = CHUNK - N_META
CONV_WIDTH = 31
HEAD_DIM = 128
N_HEADS = D_MODEL // HEAD_DIM
N_EXPERTS = 32
TOP_K = 4
D_FF = D_MODEL
SWIGLU_LIMIT = 7.0
SWIGLU_ALPHA = 1.702
RMS_EPS = 1e-6
LN_EPS = 1e-5

LANES = 128
SUBLANES = 8
SUB_BLOCK = 32
N_SUB = CHUNK // SUB_BLOCK
HALO = 32
ROW_TILE = 640
IN_TILE = 320
SLOTS_STEPS = 5
CONV_ROWS = 64
EXPERT_TILE = 512
SC_SCATTER_CHUNK = 104
SC_GATHER_CHUNK = 40
COMBINE_SUB = 4
NEG_BIG = -1e30
VMEM_LIMIT = 56 * 1024 * 1024


def _sigmoid(x):
    return 1.0 / (1.0 + jnp.exp(-x))


def _cparams(sem):
    return pltpu.CompilerParams(dimension_semantics=sem, vmem_limit_bytes=VMEM_LIMIT)


def _const_spec(shape):
    nd = len(shape)
    return pl.BlockSpec(shape, lambda *_: (0,) * nd)


PACK_WORDS = D_MODEL // 2
HIGH_HALF = 0xFFFF0000


def _pack_rows(x):
    lo = lax.bitcast_convert_type(x[:, :PACK_WORDS].astype(BF16).astype(F32), jnp.uint32)
    hi = lax.bitcast_convert_type(x[:, PACK_WORDS:].astype(BF16).astype(F32), jnp.uint32)
    return (lo >> 16) | (hi & jnp.uint32(HIGH_HALF))


def _unpack_rows(w):
    lo = lax.bitcast_convert_type(w << 16, F32)
    hi = lax.bitcast_convert_type(w & jnp.uint32(HIGH_HALF), F32)
    return jnp.concatenate([lo, hi], axis=1)


def _in_proj_kernel(h_ref, g_ref, lbl_ref, w_ref, aglu_ref, q_ref, k_ref, logf_ref, v_ref,
                    og_ref, sga_ref, sgb_ref):
    d = D_MODEL
    h = h_ref[...]
    ms = jnp.mean(h * h, axis=-1, keepdims=True)
    u = (h * lax.rsqrt(ms + RMS_EPS) * g_ref[...]).astype(BF16)

    def proj(j):
        return jnp.dot(u, w_ref[:, j * d:(j + 1) * d], preferred_element_type=F32)

    aglu_ref[...] = (proj(0) * _sigmoid(proj(1))).astype(aglu_ref.dtype)
    zq = proj(2)
    q_ref[...] = (zq * _sigmoid(zq)).astype(q_ref.dtype)
    lbl = lbl_ref[...]
    e = jnp.exp(lbl - jnp.max(lbl, axis=0, keepdims=True))
    lb = e[0:1, :] / jnp.sum(e, axis=0, keepdims=True)
    s = _sigmoid(proj(3))
    logf_ref[...] = jnp.log(lb + (1.0 - lb) * s)
    k_ref[...] = ((1.0 - lb) * (1.0 - s)).astype(k_ref.dtype)
    v_ref[...] = proj(4).astype(v_ref.dtype)
    zg = proj(5)
    og_ref[...] = (zg * _sigmoid(zg)).astype(og_ref.dtype)
    sga_ref[...] = _sigmoid(proj(6)).astype(sga_ref.dtype)
    sgb_ref[...] = _sigmoid(proj(7)).astype(sgb_ref.dtype)


def _in_proj(hp, g_mix, lb_logits, w_in_bf16):
    tp, d = hp.shape
    tm = IN_TILE
    row = pl.BlockSpec((tm, d), lambda i: (i, 0))
    out_dtypes = [BF16, BF16, BF16, F32, BF16, BF16, BF16, BF16]
    return pl.pallas_call(
        _in_proj_kernel,
        grid=(tp // tm,),
        in_specs=[row, _const_spec((1, d)), _const_spec(lb_logits.shape),
                  pl.BlockSpec(w_in_bf16.shape, lambda i: (0, 0), pipeline_mode=pl.Buffered(1))],
        out_specs=[row] * 8,
        out_shape=[jax.ShapeDtypeStruct((tp, d), dt) for dt in out_dtypes],
        compiler_params=_cparams(("parallel",)),
        name="in_proj",
    )(hp, g_mix, lb_logits, w_in_bf16)


def _conv_kernel(halo_ref, cur_ref, wdw_ref, bdw_ref, lng_ref, lnb_ref, wout_ref, bout_ref, sga_ref,
                 out_ref, win_ref, acc_ref, shift_ref):
    tm = cur_ref.shape[0]
    win_ref[0:HALO, :] = halo_ref[...].astype(F32)
    win_ref[HALO:, :] = cur_ref[...].astype(F32)
    first_tap = HALO - (CONV_WIDTH - 1)
    for c in range(D_MODEL // LANES):
        lanes = slice(c * LANES, (c + 1) * LANES)
        for s in range(SUBLANES):
            n_rows = tm + HALO - (SUBLANES if s else 0)
            shift_ref[s, 0:n_rows, :] = win_ref[pl.ds(s, n_rows), lanes]
        w_c = wdw_ref[:, lanes]
        bias = jnp.broadcast_to(bdw_ref[:, lanes], (CONV_ROWS, LANES))

        def chunk(r, carry, lanes=lanes, w_c=w_c, bias=bias):
            r0 = pl.multiple_of(r * CONV_ROWS, CONV_ROWS)
            acc = bias
            for j in range(CONV_WIDTH):
                off = first_tap + j
                rows = pl.ds(r0 + off - off % SUBLANES, CONV_ROWS)
                acc = acc + w_c[j:j + 1, :] * shift_ref[off % SUBLANES, rows, :]
            acc_ref[pl.ds(r0, CONV_ROWS), lanes] = acc
            return carry

        lax.fori_loop(0, tm // CONV_ROWS, chunk, 0)
    a = acc_ref[...]
    mu = jnp.mean(a, axis=-1, keepdims=True)
    ac = a - mu
    var = jnp.mean(ac * ac, axis=-1, keepdims=True)
    y = ac * lax.rsqrt(var + LN_EPS) * lng_ref[...] + lnb_ref[...]
    y = y * _sigmoid(y)
    o = jnp.dot(y.astype(BF16), wout_ref[...], preferred_element_type=F32) + bout_ref[...]
    out_ref[...] = (sga_ref[...].astype(F32) * o).astype(out_ref.dtype)


def _conv_branch(aglu, w_dw, b_dw, ln_g, ln_b, w_out_bf16, b_out, sga):
    tp, d = aglu.shape
    tm = ROW_TILE
    per = tm // HALO
    row = pl.BlockSpec((tm, d), lambda i: (i, 0))
    halo = pl.BlockSpec((HALO, d), lambda i: (jnp.maximum(i * per - 1, 0), 0))
    vec = _const_spec((1, d))
    return pl.pallas_call(
        _conv_kernel,
        grid=(tp // tm,),
        in_specs=[halo, row, _const_spec(w_dw.shape), vec, vec, vec, _const_spec((d, d)), vec, row],
        out_specs=row,
        out_shape=jax.ShapeDtypeStruct((tp, d), BF16),
        scratch_shapes=[pltpu.VMEM((tm + HALO, d), F32), pltpu.VMEM((tm, d), F32),
                        pltpu.VMEM((SUBLANES, tm + HALO, LANES), F32)],
        compiler_params=_cparams(("parallel",)),
        name="conv_branch",
    )(aglu, aglu, w_dw, b_dw, ln_g, ln_b, w_out_bf16, b_out, sga)


def _split_bf16(x):
    hi = x.astype(BF16)
    return hi, (x - hi.astype(F32)).astype(BF16)


def _nt_dot(a, b):
    return lax.dot_general(a, b, (((1,), (1,)), ((), ())), preferred_element_type=F32)


def _tn_dot(a, b):
    return lax.dot_general(a, b, (((0,), (0,)), ((), ())), preferred_element_type=F32)


def _scan_kernel(q_ref, k_ref, v_ref, lf_ref, og_ref, gon_ref, o_ref, st_ref):
    @pl.when(pl.program_id(1) == 0)
    def _():
        st_ref[...] = jnp.zeros_like(st_ref)

    c = CHUNK
    row = lax.broadcasted_iota(jnp.int32, (c, c), 0)
    col = lax.broadcasted_iota(jnp.int32, (c, c), 1)
    causal = col <= row
    diag_mask = jnp.logical_and(causal, row // SUB_BLOCK == col // SUB_BLOCK)
    lf = lf_ref[...]
    lf_hi = lf.astype(BF16)
    lf_mid, lf_lo = _split_bf16(lf - lf_hi.astype(F32))
    tri = causal.astype(BF16)
    bcum = (jnp.dot(tri, lf_hi, preferred_element_type=F32)
            + (jnp.dot(tri, lf_mid, preferred_element_type=F32) + jnp.dot(tri, lf_lo, preferred_element_type=F32)))

    def bcast_rows(rows):
        return jnp.concatenate([jnp.broadcast_to(r, (SUB_BLOCK, HEAD_DIM)) for r in rows], axis=0)

    heads = [slice(h * HEAD_DIM, (h + 1) * HEAD_DIM) for h in range(N_HEADS)]
    operands = []
    for hs in heads:
        b = bcum[:, hs]
        q = q_ref[:, hs].astype(F32)
        k = k_ref[:, hs].astype(F32)
        ends = [b[i * SUB_BLOCK + SUB_BLOCK - 1:i * SUB_BLOCK + SUB_BLOCK, :] for i in range(N_SUB)]
        mids = [b[i * SUB_BLOCK + SUB_BLOCK // 2 - 1:i * SUB_BLOCK + SUB_BLOCK // 2, :] for i in range(N_SUB)]
        mid_full = bcast_rows(mids)
        b_last = ends[-1]
        zero_row = jnp.zeros_like(b_last)
        qm = q * jnp.exp(b - mid_full)
        km = k * jnp.exp(mid_full - b)
        q_parts, k_parts = [], []
        for j in range(N_SUB - 1):
            q_rows = [jnp.exp(mids[i] - ends[j]) if i > j else zero_row for i in range(N_SUB)]
            k_rows = [jnp.exp(ends[j] - mids[j]) if i == j else zero_row for i in range(N_SUB)]
            q_parts.append((qm * bcast_rows(q_rows)).astype(BF16))
            k_parts.append((km * bcast_rows(k_rows)).astype(BF16))
        q_in = (qm * bcast_rows([jnp.exp(m) for m in mids])).astype(BF16)
        k_out = (km * bcast_rows([jnp.exp(b_last - m) for m in mids])).astype(BF16)
        operands.append((qm.astype(BF16), km.astype(BF16), jnp.concatenate(q_parts, axis=1),
                         jnp.concatenate(k_parts, axis=1), q_in, k_out, jnp.exp(b_last)))
    products = []
    for h, hs in enumerate(heads):
        qm, km, q_cat, k_cat, q_in, k_out, _ = operands[h]
        products.append((_nt_dot(qm, km), _nt_dot(q_cat, k_cat), _nt_dot(q_in, st_ref[h].astype(BF16)),
                         _tn_dot(v_ref[:, hs], k_out)))
    outs = []
    for h, hs in enumerate(heads):
        same_block, earlier_blocks, from_state, state_update = products[h]
        scores = jnp.where(diag_mask, same_block, 0.0) + earlier_blocks
        outs.append(from_state + jnp.dot(scores.astype(BF16), v_ref[:, hs], preferred_element_type=F32))
        st_ref[h] = st_ref[h] * operands[h][6] + state_update
    for h, hs in enumerate(heads):
        o = outs[h]
        ms = jnp.mean(o * o, axis=-1, keepdims=True)
        on = o * lax.rsqrt(ms + RMS_EPS) * gon_ref[:, hs]
        o_ref[:, hs] = (on * og_ref[:, hs].astype(F32)).astype(o_ref.dtype)


def _hgrn2_scan(q, k, v, logf, og, g_onorm, n_batch):
    tp, d = q.shape
    n_chunks = tp // n_batch // CHUNK
    blk = pl.BlockSpec((CHUNK, d), lambda b, c: (b * n_chunks + c, 0))
    return pl.pallas_call(
        _scan_kernel,
        grid=(n_batch, n_chunks),
        in_specs=[blk, blk, blk, blk, blk, pl.BlockSpec((1, d), lambda b, c: (0, 0))],
        out_specs=blk,
        out_shape=jax.ShapeDtypeStruct((tp, d), BF16),
        scratch_shapes=[pltpu.VMEM((N_HEADS, HEAD_DIM, HEAD_DIM), F32)],
        compiler_params=_cparams(("arbitrary", "arbitrary")),
        name="hgrn2_scan",
    )(q, k, v, logf, og, g_onorm)


def _merge_route_kernel(h_ref, on_ref, ap_ref, sgb_ref, wrnn_ref, wo_ref, gffn_ref, wr_ref, br_ref,
                        h1_ref, u2_ref, info_ref, cnt_ref, tri_ref, carry_ref, colcnt_ref):
    tm = h_ref.shape[0]

    @pl.when(pl.program_id(0) == 0)
    def _():
        r_i = lax.broadcasted_iota(jnp.int32, (tm, tm), 0)
        c_i = lax.broadcasted_iota(jnp.int32, (tm, tm), 1)
        tri_ref[...] = (c_i < r_i).astype(BF16)
        carry_ref[...] = jnp.zeros_like(carry_ref)
        colcnt_ref[...] = jnp.zeros_like(colcnt_ref)

    r = jnp.dot(on_ref[...], wrnn_ref[...], preferred_element_type=F32)
    y = ap_ref[...].astype(F32) + sgb_ref[...].astype(F32) * r
    h1 = h_ref[...] + jnp.dot(y.astype(BF16), wo_ref[...], preferred_element_type=F32)
    h1_ref[...] = h1
    ms = jnp.mean(h1 * h1, axis=-1, keepdims=True)
    u2 = h1 * lax.rsqrt(ms + RMS_EPS) * gffn_ref[...]
    u2_ref[...] = _pack_rows(u2)
    u_hi, u_lo = _split_bf16(u2)
    w_hi, w_lo = _split_bf16(wr_ref[...])
    logits = (jnp.dot(u_hi, w_hi, preferred_element_type=F32)
              + (jnp.dot(u_lo, w_hi, preferred_element_type=F32) + jnp.dot(u_hi, w_lo, preferred_element_type=F32))
              + br_ref[...])
    lane = lax.broadcasted_iota(jnp.int32, (tm, LANES), 1)
    cur = jnp.where(lane < N_EXPERTS, logits, NEG_BIG)
    vals, idxs, sels = [], [], []
    for _ in range(TOP_K):
        m = jnp.max(cur, axis=-1, keepdims=True)
        idx = jnp.min(jnp.where(cur == m, lane, LANES), axis=-1, keepdims=True)
        sel = lane == idx
        cur = jnp.where(sel, 2.0 * NEG_BIG, cur)
        vals.append(m)
        idxs.append(idx)
        sels.append(sel)
    exps = [jnp.exp(vk - vals[0]) for vk in vals]
    den = exps[0] + exps[1] + exps[2] + exps[3]
    onehot = jnp.logical_or(jnp.logical_or(sels[0], sels[1]), jnp.logical_or(sels[2], sels[3]))
    onehot_bf = onehot.astype(BF16)
    rank_all = jnp.dot(tri_ref[...], onehot_bf, preferred_element_type=F32) + carry_ref[...]
    carry_ref[...] += jnp.sum(onehot.astype(F32), axis=0, keepdims=True)
    colcnt_ref[...] += _tn_dot(onehot_bf, jnp.ones((tm, LANES), BF16))
    info = jnp.zeros((tm, LANES), F32)
    for kk in range(TOP_K):
        rank_k = jnp.sum(jnp.where(sels[kk], rank_all, 0.0), axis=-1, keepdims=True)
        info = jnp.where(lane == kk, exps[kk] / den, info)
        info = jnp.where(lane == TOP_K + kk, idxs[kk].astype(F32), info)
        info = jnp.where(lane == 2 * TOP_K + kk, rank_k, info)
    info_ref[...] = info
    cnt_ref[...] = colcnt_ref[...]


def _merge_route(hp, on, ap, sgb, w_rnn_bf16, w_o_bf16, g_ffn, w_router_pad, b_router_pad):
    tp, d = hp.shape
    tm = ROW_TILE
    row = pl.BlockSpec((tm, d), lambda i: (i, 0))
    vec = _const_spec((1, d))
    return pl.pallas_call(
        _merge_route_kernel,
        grid=(tp // tm,),
        in_specs=[row, row, row, row, _const_spec((d, d)), _const_spec((d, d)), vec,
                  _const_spec((d, LANES)), _const_spec((1, LANES))],
        out_specs=[row, pl.BlockSpec((tm, PACK_WORDS), lambda i: (i, 0)),
                   pl.BlockSpec((tm, LANES), lambda i: (i, 0)), _const_spec((LANES, LANES))],
        out_shape=[jax.ShapeDtypeStruct((tp, d), F32), jax.ShapeDtypeStruct((tp, PACK_WORDS), jnp.uint32),
                   jax.ShapeDtypeStruct((tp, LANES), F32), jax.ShapeDtypeStruct((LANES, LANES), F32)],
        scratch_shapes=[pltpu.VMEM((tm, tm), BF16), pltpu.VMEM((1, LANES), F32),
                        pltpu.VMEM((LANES, LANES), F32)],
        compiler_params=_cparams(("arbitrary",)),
        name="merge_route",
    )(hp, on, ap, sgb, w_rnn_bf16, w_o_bf16, g_ffn, w_router_pad, b_router_pad)


def _slots_kernel(info_ref, cnt_ref, dest_ref, te_ref, pstart_ref):
    tm = info_ref.shape[0]
    n_tile_lanes = te_ref.shape[1]

    @pl.when(pl.program_id(0) == 0)
    def _():
        r_i = lax.broadcasted_iota(jnp.int32, (LANES, LANES), 0)
        c_i = lax.broadcasted_iota(jnp.int32, (LANES, LANES), 1)
        cnt = cnt_ref[...]
        tiles = jnp.floor((cnt + (EXPERT_TILE - 1)) / EXPERT_TILE)
        pend_col = jnp.dot((c_i <= r_i).astype(F32), tiles, preferred_element_type=F32,
                           precision=lax.Precision.HIGHEST)
        pstart_ref[...] = _tn_dot_f32(tiles, (r_i < c_i).astype(F32))[0:SUBLANES, :]
        tile_id = lax.broadcasted_iota(jnp.int32, (LANES, n_tile_lanes), 1).astype(F32)
        exp_id = lax.broadcasted_iota(jnp.int32, (LANES, n_tile_lanes), 0)
        pend_wide = jnp.concatenate([pend_col] * (n_tile_lanes // LANES), axis=1)
        below = jnp.logical_and(pend_wide <= tile_id, exp_id < N_EXPERTS)
        te = jnp.minimum(jnp.sum(below.astype(F32), axis=0, keepdims=True), N_EXPERTS - 1.0)
        n_active = pend_col[N_EXPERTS - 1:N_EXPERTS, 0:1]
        sub_t = lax.broadcasted_iota(jnp.int32, (SUBLANES, n_tile_lanes), 0)
        out = jnp.where(sub_t == 0, jnp.broadcast_to(te, (SUBLANES, n_tile_lanes)),
                        jnp.broadcast_to(n_active, (SUBLANES, n_tile_lanes)))
        te_ref[...] = out.astype(jnp.int32)

    info = info_ref[...]
    pstart_row = pstart_ref[0:1, :]
    lane = lax.broadcasted_iota(jnp.int32, (tm, LANES), 1)
    dest = jnp.zeros((tm, LANES), F32)
    for kk in range(TOP_K):
        idx_k = info[:, TOP_K + kk:TOP_K + kk + 1].astype(jnp.int32)
        start_k = jnp.sum(jnp.where(lane == idx_k, pstart_row, 0.0), axis=-1, keepdims=True)
        slot_k = start_k * EXPERT_TILE + info[:, 2 * TOP_K + kk:2 * TOP_K + kk + 1]
        dest = jnp.where(lane == kk, slot_k, dest)
    dest_ref[...] = dest.astype(jnp.int32)


def _tn_dot_f32(a, b):
    return lax.dot_general(a, b, (((0,), (0,)), ((), ())), preferred_element_type=F32,
                           precision=lax.Precision.HIGHEST)


def _slots(info, cnt, n_tiles_max):
    tp = info.shape[0]
    tm = tp // SLOTS_STEPS
    n_tile_lanes = -(-n_tiles_max // LANES) * LANES
    return pl.pallas_call(
        _slots_kernel,
        grid=(tp // tm,),
        in_specs=[pl.BlockSpec((tm, LANES), lambda i: (i, 0)), _const_spec((LANES, LANES))],
        out_specs=[pl.BlockSpec((tm, LANES), lambda i: (i, 0)), _const_spec((8, n_tile_lanes))],
        out_shape=[jax.ShapeDtypeStruct((tp, LANES), jnp.int32),
                   jax.ShapeDtypeStruct((8, n_tile_lanes), jnp.int32)],
        scratch_shapes=[pltpu.VMEM((SUBLANES, LANES), F32)],
        compiler_params=_cparams(("arbitrary",)),
        name="slots",
    )(info, cnt)


def _sc_workers():
    sc = plsc.get_sparse_core_info()
    return sc.num_cores, sc.num_cores * sc.num_subcores


def _sc_token_chunks(tp, chunk, body):
    n_cores, n_workers = _sc_workers()
    per_worker = tp // n_workers
    assert per_worker * n_workers == tp and per_worker % chunk == 0 and chunk % SUBLANES == 0
    base = (lax.axis_index("s") * n_cores + lax.axis_index("c")) * per_worker

    @pl.loop(0, per_worker // chunk)
    def _(ci):
        body(pl.multiple_of(base + ci * chunk, SUBLANES))


def _sc_kernel(out_type, chunk, n_row_bufs):
    return functools.partial(
        pl.kernel, mesh=plsc.VectorSubcoreMesh(core_axis_name="c", subcore_axis_name="s"), out_type=out_type,
        scratch_types=[pltpu.VMEM((TOP_K, chunk), jnp.int32), pltpu.VMEM((n_row_bufs, chunk, PACK_WORDS), jnp.uint32),
                       pltpu.SemaphoreType.DMA, pltpu.SemaphoreType.DMA])


def _wait_all(copies):
    for c in copies:
        c.wait()


def _dispatch(dest_flat, u2p, n_slots):
    tp = u2p.shape[0]
    ch = SC_SCATTER_CHUNK

    @_sc_kernel(jax.ShapeDtypeStruct((n_slots, PACK_WORDS), jnp.uint32), ch, 1)
    def scatter_rows(u2_hbm, dest_hbm, xs_hbm, idx_v, rows_v, sem_a, sem_b):
        def chunk(off):
            loads = [pltpu.async_copy(u2_hbm.at[pl.ds(off, ch)], rows_v.at[0], sem_a)]
            loads += [pltpu.async_copy(dest_hbm.at[pl.ds(kk * tp + off, ch)], idx_v.at[kk], sem_a)
                      for kk in range(TOP_K)]
            _wait_all(loads)
            _wait_all([pltpu.async_copy(rows_v.at[0], xs_hbm.at[idx_v.at[kk]], sem_b) for kk in range(TOP_K)])

        _sc_token_chunks(tp, ch, chunk)

    return scatter_rows(u2p, dest_flat)


def _gather_expert_rows(dest_flat, ys):
    tp = dest_flat.shape[0] // TOP_K
    ch = SC_GATHER_CHUNK

    @_sc_kernel(jax.ShapeDtypeStruct((TOP_K, tp, PACK_WORDS), jnp.uint32), ch, TOP_K)
    def gather_rows(ys_hbm, dest_hbm, out_hbm, idx_v, rows_v, sem_a, sem_b):
        def chunk(off):
            _wait_all([pltpu.async_copy(dest_hbm.at[pl.ds(kk * tp + off, ch)], idx_v.at[kk], sem_a)
                       for kk in range(TOP_K)])
            _wait_all([pltpu.async_copy(ys_hbm.at[idx_v.at[kk]], rows_v.at[kk], sem_b) for kk in range(TOP_K)])
            _wait_all([pltpu.async_copy(rows_v.at[kk], out_hbm.at[kk, pl.ds(off, ch)], sem_a)
                       for kk in range(TOP_K)])

        _sc_token_chunks(tp, ch, chunk)

    return gather_rows(ys, dest_flat)


def _expert_kernel(te_ref, nact_ref, x_ref, wgu_ref, bgu_ref, wdn_ref, bdn_ref, y_ref):
    del te_ref

    @pl.when(pl.program_id(0) < nact_ref[0])
    def _():
        x = _unpack_rows(x_ref[...]).astype(BF16)
        gu = jnp.dot(x, wgu_ref[0].astype(BF16), preferred_element_type=F32) + bgu_ref[0]
        gate = jnp.minimum(gu[:, :D_FF], SWIGLU_LIMIT)
        up = jnp.clip(gu[:, D_FF:], -SWIGLU_LIMIT, SWIGLU_LIMIT)
        hdn = (up + 1.0) * (gate * _sigmoid(SWIGLU_ALPHA * gate))
        y = jnp.dot(hdn.astype(BF16), wdn_ref[0].astype(BF16), preferred_element_type=F32) + bdn_ref[0]
        y_ref[...] = _pack_rows(y)

    @pl.when(pl.program_id(0) >= nact_ref[0])
    def _():
        y_ref[...] = jnp.zeros_like(y_ref)


def _expert_mlp(te, nact, xs, w_gu, b_gu, w_dn, b_dn):
    d = D_MODEL
    n_tiles = xs.shape[0] // EXPERT_TILE

    def tile_map(i, te_ref, nact_ref):
        return (jnp.minimum(i, nact_ref[0] - 1), 0)

    def exp_map(i, te_ref, nact_ref):
        return (te_ref[jnp.minimum(i, nact_ref[0] - 1)], 0, 0)

    grid_spec = pltpu.PrefetchScalarGridSpec(
        num_scalar_prefetch=2,
        grid=(n_tiles,),
        in_specs=[pl.BlockSpec((EXPERT_TILE, PACK_WORDS), tile_map),
                  pl.BlockSpec((1, d, 2 * D_FF), exp_map),
                  pl.BlockSpec((1, 1, 2 * D_FF), exp_map),
                  pl.BlockSpec((1, D_FF, d), exp_map),
                  pl.BlockSpec((1, 1, d), exp_map)],
        out_specs=pl.BlockSpec((EXPERT_TILE, PACK_WORDS), lambda i, te_ref, nact_ref: (i, 0)),
    )
    return pl.pallas_call(
        _expert_kernel,
        grid_spec=grid_spec,
        out_shape=jax.ShapeDtypeStruct(xs.shape, jnp.uint32),
        compiler_params=_cparams(("arbitrary",)),
        name="expert_mlp",
    )(te, nact, xs, w_gu, b_gu, w_dn, b_dn)


def _combine_kernel(gfin_ref, *refs):
    out_ref = refs[-1]
    for s in range(COMBINE_SUB):
        h1_ref, info_ref, yg_ref = refs[3 * s:3 * s + 3]
        info = info_ref[...]
        h2 = h1_ref[...]
        for kk in range(TOP_K):
            h2 = h2 + info[:, kk:kk + 1] * _unpack_rows(yg_ref[kk])
        ms = jnp.mean(h2 * h2, axis=-1, keepdims=True)
        out_ref[s * CHUNK:(s + 1) * CHUNK, :] = h2 * lax.rsqrt(ms + RMS_EPS) * gfin_ref[...]


def _combine(h1, info, g_final, yg, n_batch, seq):
    d = h1.shape[1]
    chunks = seq // CHUNK

    def padded(s):
        def index(j):
            c = j * COMBINE_SUB + s
            return (c // chunks) * (chunks + 1) + c % chunks + 1
        return index

    in_specs = [pl.BlockSpec((1, d), lambda j: (0, 0))]
    operands = [g_final]
    for s in range(COMBINE_SUB):
        chunk_of = padded(s)
        in_specs += [pl.BlockSpec((CHUNK, d), lambda j, f=chunk_of: (f(j), 0)),
                     pl.BlockSpec((CHUNK, LANES), lambda j, f=chunk_of: (f(j), 0)),
                     pl.BlockSpec((TOP_K, CHUNK, PACK_WORDS), lambda j, f=chunk_of: (0, f(j), 0))]
        operands += [h1, info, yg]
    return pl.pallas_call(
        _combine_kernel,
        grid=(n_batch * chunks // COMBINE_SUB,),
        in_specs=in_specs,
        out_specs=pl.BlockSpec((COMBINE_SUB * CHUNK, d), lambda j: (j, 0)),
        out_shape=jax.ShapeDtypeStruct((n_batch * seq, d), F32),
        compiler_params=_cparams(("parallel",)),
        name="combine",
    )(*operands)


def kernel(x, meta_tokens, lb_logits, g_mix, w_in, w_dw, b_dw, ln_g, ln_b, w_conv_out, b_conv_out,
           g_onorm, w_rnn_out, w_o, g_ffn, w_router, b_router, w_gate_up, b_gate_up, w_down, b_down,
           g_final):
    n_batch, seq, d = x.shape
    assert d == D_MODEL and w_in.shape[0] == 1, "single-layer block with D_MODEL features"
    assert seq % (CHUNK * COMBINE_SUB) == 0
    seq_pad = CHUNK + seq
    tp = n_batch * seq_pad
    assert tp % ROW_TILE == 0

    meta = jnp.broadcast_to(meta_tokens.astype(x.dtype)[None], (n_batch, N_META, d))
    hp = jnp.concatenate([jnp.zeros((n_batch, CHUNK_PAD, d), x.dtype), meta, x], axis=1).reshape(tp, d)
    vec = lambda a: a.reshape(1, -1).astype(F32)

    aglu, q, k, logf, v, og, sga, sgb = _in_proj(hp, vec(g_mix[0]), lb_logits.astype(F32), w_in[0].astype(BF16))
    ap = _conv_branch(aglu, w_dw[0].astype(F32), vec(b_dw[0]), vec(ln_g[0]), vec(ln_b[0]),
                      w_conv_out[0].astype(BF16), vec(b_conv_out[0]), sga)
    on = _hgrn2_scan(q, k, v, logf, og, vec(g_onorm[0]), n_batch)
    w_router_pad = jnp.pad(w_router[0].astype(F32), ((0, 0), (0, LANES - N_EXPERTS)))
    b_router_pad = jnp.pad(vec(b_router[0]), ((0, 0), (0, LANES - N_EXPERTS)))
    h1, u2p, info, cnt = _merge_route(hp, on, ap, sgb, w_rnn_out[0].astype(BF16), w_o[0].astype(BF16),
                                      vec(g_ffn[0]), w_router_pad, b_router_pad)

    n_tiles_max = -(-(tp * TOP_K + N_EXPERTS * (EXPERT_TILE - 1)) // EXPERT_TILE)
    dest, te = _slots(info, cnt, n_tiles_max)
    dest_flat = dest[:, :TOP_K].T.reshape(-1)
    xs = _dispatch(dest_flat, u2p, n_tiles_max * EXPERT_TILE)
    ys = _expert_mlp(te[0, :n_tiles_max], te[1, :1], xs, w_gate_up[0],
                     b_gate_up[0].reshape(N_EXPERTS, 1, -1).astype(F32), w_down[0],
                     b_down[0].reshape(N_EXPERTS, 1, -1).astype(F32))
    yg = _gather_expert_rows(dest_flat, ys)
    out = _combine(h1, info, vec(g_final), yg, n_batch, seq)
    return out.reshape(n_batch, seq, d)
```

```python
import functools

import jax
import jax.numpy as jnp
from jax import lax
from jax.experimental import pallas as pl
from jax.experimental.pallas import tpu as pltpu
from jax.experimental.pallas import tpu_sc as plsc

F32 = jnp.float32
BF16 = jnp.bfloat16

D_MODEL = 1024
N_META = 16
CHUNK = 128
CHUNK_PAD = CHUNK - N_META
CONV_WIDTH = 31
HEAD_DIM = 128
N_HEADS = D_MODEL // HEAD_DIM
N_EXPERTS = 32
TOP_K = 4
D_FF = D_MODEL
SWIGLU_LIMIT = 7.0
SWIGLU_ALPHA = 1.702
RMS_EPS = 1e-6
LN_EPS = 1e-5

LANES = 128
SUBLANES = 8
SUB_BLOCK = 32
N_SUB = CHUNK // SUB_BLOCK
HALO = 32
ROW_TILE = 640
IN_TILE = 320
SLOTS_STEPS = 5
CONV_ROWS = 64
EXPERT_TILE = 512
SC_SCATTER_CHUNK = 104
SC_GATHER_CHUNK = 40
COMBINE_SUB = 4
NEG_BIG = -1e30
VMEM_LIMIT = 56 * 1024 * 1024


def _sigmoid(x):
    return 1.0 / (1.0 + jnp.exp(-x))


def _cparams(sem, fuse_inputs=None):
    return pltpu.CompilerParams(dimension_semantics=sem, vmem_limit_bytes=VMEM_LIMIT, allow_input_fusion=fuse_inputs)


def _const_spec(shape):
    nd = len(shape)
    return pl.BlockSpec(shape, lambda *_: (0,) * nd)


PACK_WORDS = D_MODEL // 2
HIGH_HALF = 0xFFFF0000


def _pack_rows(x):
    lo = lax.bitcast_convert_type(x[:, :PACK_WORDS].astype(BF16).astype(F32), jnp.uint32)
    hi = lax.bitcast_convert_type(x[:, PACK_WORDS:].astype(BF16).astype(F32), jnp.uint32)
    return (lo >> 16) | (hi & jnp.uint32(HIGH_HALF))


def _unpack_rows(w):
    lo = lax.bitcast_convert_type(w << 16, F32)
    hi = lax.bitcast_convert_type(w & jnp.uint32(HIGH_HALF), F32)
    return jnp.concatenate([lo, hi], axis=1)


def _in_proj_kernel(h_ref, g_ref, lbl_ref, w_ref, aglu_ref, q_ref, k_ref, logf_ref, v_ref,
                    og_ref, sga_ref, sgb_ref):
    d = D_MODEL
    h = h_ref[...]
    ms = jnp.mean(h * h, axis=-1, keepdims=True)
    u = (h * lax.rsqrt(ms + RMS_EPS) * g_ref[...]).astype(BF16)

    def proj(j):
        return jnp.dot(u, w_ref[:, j * d:(j + 1) * d], preferred_element_type=F32)

    aglu_ref[...] = (proj(0) * _sigmoid(proj(1))).astype(aglu_ref.dtype)
    zq = proj(2)
    q_ref[...] = (zq * _sigmoid(zq)).astype(q_ref.dtype)
    lbl = lbl_ref[...]
    e = jnp.exp(lbl - jnp.max(lbl, axis=0, keepdims=True))
    lb = e[0:1, :] / jnp.sum(e, axis=0, keepdims=True)
    s = _sigmoid(proj(3))
    logf_ref[...] = jnp.log(lb + (1.0 - lb) * s)
    k_ref[...] = ((1.0 - lb) * (1.0 - s)).astype(k_ref.dtype)
    v_ref[...] = proj(4).astype(v_ref.dtype)
    zg = proj(5)
    og_ref[...] = (zg * _sigmoid(zg)).astype(og_ref.dtype)
    sga_ref[...] = _sigmoid(proj(6)).astype(sga_ref.dtype)
    sgb_ref[...] = _sigmoid(proj(7)).astype(sgb_ref.dtype)


def _in_proj(hp, g_mix, lb_logits, w_in_bf16):
    tp, d = hp.shape
    tm = IN_TILE
    row = pl.BlockSpec((tm, d), lambda i: (i, 0))
    out_dtypes = [BF16, BF16, BF16, F32, BF16, BF16, BF16, BF16]
    return pl.pallas_call(
        _in_proj_kernel,
        grid=(tp // tm,),
        in_specs=[row, _const_spec((1, d)), _const_spec(lb_logits.shape),
                  pl.BlockSpec(w_in_bf16.shape, lambda i: (0, 0), pipeline_mode=pl.Buffered(1))],
        out_specs=[row] * 8,
        out_shape=[jax.ShapeDtypeStruct((tp, d), dt) for dt in out_dtypes],
        compiler_params=_cparams(("parallel",), [True, False, False, True]),
        name="in_proj",
    )(hp, g_mix, lb_logits, w_in_bf16)


def _conv_kernel(halo_ref, cur_ref, wdw_ref, bdw_ref, lng_ref, lnb_ref, wout_ref, bout_ref, sga_ref,
                 out_ref, win_ref, acc_ref, shift_ref):
    tm = cur_ref.shape[0]
    win_ref[0:HALO, :] = halo_ref[...].astype(F32)
    win_ref[HALO:, :] = cur_ref[...].astype(F32)
    first_tap = HALO - (CONV_WIDTH - 1)
    for c in range(D_MODEL // LANES):
        lanes = slice(c * LANES, (c + 1) * LANES)
        for s in range(SUBLANES):
            n_rows = tm + HALO - (SUBLANES if s else 0)
            shift_ref[s, 0:n_rows, :] = win_ref[pl.ds(s, n_rows), lanes]
        w_c = wdw_ref[:, lanes]
        bias = jnp.broadcast_to(bdw_ref[:, lanes], (CONV_ROWS, LANES))

        def chunk(r, carry, lanes=lanes, w_c=w_c, bias=bias):
            r0 = pl.multiple_of(r * CONV_ROWS, CONV_ROWS)
            acc = bias
            for j in range(CONV_WIDTH):
                off = first_tap + j
                rows = pl.ds(r0 + off - off % SUBLANES, CONV_ROWS)
                acc = acc + w_c[j:j + 1, :] * shift_ref[off % SUBLANES, rows, :]
            acc_ref[pl.ds(r0, CONV_ROWS), lanes] = acc
            return carry

        lax.fori_loop(0, tm // CONV_ROWS, chunk, 0)
    a = acc_ref[...]
    mu = jnp.mean(a, axis=-1, keepdims=True)
    ac = a - mu
    var = jnp.mean(ac * ac, axis=-1, keepdims=True)
    y = ac * lax.rsqrt(var + LN_EPS) * lng_ref[...] + lnb_ref[...]
    y = y * _sigmoid(y)
    o = jnp.dot(y.astype(BF16), wout_ref[...], preferred_element_type=F32) + bout_ref[...]
    out_ref[...] = (sga_ref[...].astype(F32) * o).astype(out_ref.dtype)


def _conv_branch(aglu, w_dw, b_dw, ln_g, ln_b, w_out_bf16, b_out, sga):
    tp, d = aglu.shape
    tm = ROW_TILE
    per = tm // HALO
    row = pl.BlockSpec((tm, d), lambda i: (i, 0))
    halo = pl.BlockSpec((HALO, d), lambda i: (jnp.maximum(i * per - 1, 0), 0))
    vec = _const_spec((1, d))
    return pl.pallas_call(
        _conv_kernel,
        grid=(tp // tm,),
        in_specs=[halo, row, _const_spec(w_dw.shape), vec, vec, vec, _const_spec((d, d)), vec, row],
        out_specs=row,
        out_shape=jax.ShapeDtypeStruct((tp, d), BF16),
        scratch_shapes=[pltpu.VMEM((tm + HALO, d), F32), pltpu.VMEM((tm, d), F32),
                        pltpu.VMEM((SUBLANES, tm + HALO, LANES), F32)],
        compiler_params=_cparams(("parallel",)),
        name="conv_branch",
    )(aglu, aglu, w_dw, b_dw, ln_g, ln_b, w_out_bf16, b_out, sga)


def _split_bf16(x):
    hi = x.astype(BF16)
    return hi, (x - hi.astype(F32)).astype(BF16)


def _nt_dot(a, b):
    return lax.dot_general(a, b, (((1,), (1,)), ((), ())), preferred_element_type=F32)


def _tn_dot(a, b):
    return lax.dot_general(a, b, (((0,), (0,)), ((), ())), preferred_element_type=F32)


def _scan_kernel(q_ref, k_ref, v_ref, lf_ref, og_ref, gon_ref, o_ref, st_ref):
    @pl.when(pl.program_id(1) == 0)
    def _():
        st_ref[...] = jnp.zeros_like(st_ref)

    c = CHUNK
    row = lax.broadcasted_iota(jnp.int32, (c, c), 0)
    col = lax.broadcasted_iota(jnp.int32, (c, c), 1)
    causal = col <= row
    diag_mask = jnp.logical_and(causal, row // SUB_BLOCK == col // SUB_BLOCK)
    lf = lf_ref[...]
    lf_hi = lf.astype(BF16)
    lf_mid, lf_lo = _split_bf16(lf - lf_hi.astype(F32))
    tri = causal.astype(BF16)
    bcum = (jnp.dot(tri, lf_hi, preferred_element_type=F32)
            + (jnp.dot(tri, lf_mid, preferred_element_type=F32) + jnp.dot(tri, lf_lo, preferred_element_type=F32)))

    def bcast_rows(rows):
        return jnp.concatenate([jnp.broadcast_to(r, (SUB_BLOCK, HEAD_DIM)) for r in rows], axis=0)

    heads = [slice(h * HEAD_DIM, (h + 1) * HEAD_DIM) for h in range(N_HEADS)]
    operands = []
    for hs in heads:
        b = bcum[:, hs]
        q = q_ref[:, hs].astype(F32)
        k = k_ref[:, hs].astype(F32)
        ends = [b[i * SUB_BLOCK + SUB_BLOCK - 1:i * SUB_BLOCK + SUB_BLOCK, :] for i in range(N_SUB)]
        mids = [b[i * SUB_BLOCK + SUB_BLOCK // 2 - 1:i * SUB_BLOCK + SUB_BLOCK // 2, :] for i in range(N_SUB)]
        mid_full = bcast_rows(mids)
        b_last = ends[-1]
        zero_row = jnp.zeros_like(b_last)
        qm = q * jnp.exp(b - mid_full)
        km = k * jnp.exp(mid_full - b)
        q_parts, k_parts = [], []
        for j in range(N_SUB - 1):
            q_rows = [jnp.exp(mids[i] - ends[j]) if i > j else zero_row for i in range(N_SUB)]
            k_rows = [jnp.exp(ends[j] - mids[j]) if i == j else zero_row for i in range(N_SUB)]
            q_parts.append((qm * bcast_rows(q_rows)).astype(BF16))
            k_parts.append((km * bcast_rows(k_rows)).astype(BF16))
        q_in = (qm * bcast_rows([jnp.exp(m) for m in mids])).astype(BF16)
        k_out = (km * bcast_rows([jnp.exp(b_last - m) for m in mids])).astype(BF16)
        operands.append((qm.astype(BF16), km.astype(BF16), jnp.concatenate(q_parts, axis=1),
                         jnp.concatenate(k_parts, axis=1), q_in, k_out, jnp.exp(b_last)))
    products = []
    for h, hs in enumerate(heads):
        qm, km, q_cat, k_cat, q_in, k_out, _ = operands[h]
        products.append((_nt_dot(qm, km), _nt_dot(q_cat, k_cat), _nt_dot(q_in, st_ref[h].astype(BF16)),
                         _tn_dot(v_ref[:, hs], k_out)))
    outs = []
    for h, hs in enumerate(heads):
        same_block, earlier_blocks, from_state, state_update = products[h]
        scores = jnp.where(diag_mask, same_block, 0.0) + earlier_blocks
        outs.append(from_state + jnp.dot(scores.astype(BF16), v_ref[:, hs], preferred_element_type=F32))
        st_ref[h] = st_ref[h] * operands[h][6] + state_update
    for h, hs in enumerate(heads):
        o = outs[h]
        ms = jnp.mean(o * o, axis=-1, keepdims=True)
        on = o * lax.rsqrt(ms + RMS_EPS) * gon_ref[:, hs]
        o_ref[:, hs] = (on * og_ref[:, hs].astype(F32)).astype(o_ref.dtype)


def _hgrn2_scan(q, k, v, logf, og, g_onorm, n_batch):
    tp, d = q.shape
    n_chunks = tp // n_batch // CHUNK
    blk = pl.BlockSpec((CHUNK, d), lambda b, c: (b * n_chunks + c, 0))
    return pl.pallas_call(
        _scan_kernel,
        grid=(n_batch, n_chunks),
        in_specs=[blk, blk, blk, blk, blk, pl.BlockSpec((1, d), lambda b, c: (0, 0))],
        out_specs=blk,
        out_shape=jax.ShapeDtypeStruct((tp, d), BF16),
        scratch_shapes=[pltpu.VMEM((N_HEADS, HEAD_DIM, HEAD_DIM), F32)],
        compiler_params=_cparams(("arbitrary", "arbitrary")),
        name="hgrn2_scan",
    )(q, k, v, logf, og, g_onorm)


def _merge_route_kernel(h_ref, on_ref, ap_ref, sgb_ref, wrnn_ref, wo_ref, gffn_ref, wr_ref, br_ref,
                        h1_ref, u2_ref, info_ref, cnt_ref, tri_ref, carry_ref, colcnt_ref):
    tm = h_ref.shape[0]

    @pl.when(pl.program_id(0) == 0)
    def _():
        r_i = lax.broadcasted_iota(jnp.int32, (tm, tm), 0)
        c_i = lax.broadcasted_iota(jnp.int32, (tm, tm), 1)
        tri_ref[...] = (c_i < r_i).astype(BF16)
        carry_ref[...] = jnp.zeros_like(carry_ref)
        colcnt_ref[...] = jnp.zeros_like(colcnt_ref)

    r = jnp.dot(on_ref[...], wrnn_ref[...], preferred_element_type=F32)
    y = ap_ref[...].astype(F32) + sgb_ref[...].astype(F32) * r
    h1 = h_ref[...] + jnp.dot(y.astype(BF16), wo_ref[...], preferred_element_type=F32)
    h1_ref[...] = h1
    ms = jnp.mean(h1 * h1, axis=-1, keepdims=True)
    u2 = h1 * lax.rsqrt(ms + RMS_EPS) * gffn_ref[...]
    u2_ref[...] = _pack_rows(u2)
    u_hi, u_lo = _split_bf16(u2)
    w_hi, w_lo = _split_bf16(wr_ref[...])
    logits = (jnp.dot(u_hi, w_hi, preferred_element_type=F32)
              + (jnp.dot(u_lo, w_hi, preferred_element_type=F32) + jnp.dot(u_hi, w_lo, preferred_element_type=F32))
              + br_ref[...])
    lane = lax.broadcasted_iota(jnp.int32, (tm, LANES), 1)
    cur = jnp.where(lane < N_EXPERTS, logits, NEG_BIG)
    vals, idxs, sels = [], [], []
    for _ in range(TOP_K):
        m = jnp.max(cur, axis=-1, keepdims=True)
        idx = jnp.min(jnp.where(cur == m, lane, LANES), axis=-1, keepdims=True)
        sel = lane == idx
        cur = jnp.where(sel, 2.0 * NEG_BIG, cur)
        vals.append(m)
        idxs.append(idx)
        sels.append(sel)
    exps = [jnp.exp(vk - vals[0]) for vk in vals]
    den = exps[0] + exps[1] + exps[2] + exps[3]
    onehot = jnp.logical_or(jnp.logical_or(sels[0], sels[1]), jnp.logical_or(sels[2], sels[3]))
    onehot_bf = onehot.astype(BF16)
    rank_all = jnp.dot(tri_ref[...], onehot_bf, preferred_element_type=F32) + carry_ref[...]
    carry_ref[...] += jnp.sum(onehot.astype(F32), axis=0, keepdims=True)
    colcnt_ref[...] += _tn_dot(onehot_bf, jnp.ones((tm, LANES), BF16))
    info = jnp.zeros((tm, LANES), F32)
    for kk in range(TOP_K):
        rank_k = jnp.sum(jnp.where(sels[kk], rank_all, 0.0), axis=-1, keepdims=True)
        info = jnp.where(lane == kk, exps[kk] / den, info)
        info = jnp.where(lane == TOP_K + kk, idxs[kk].astype(F32), info)
        info = jnp.where(lane == 2 * TOP_K + kk, rank_k, info)
    info_ref[...] = info
    cnt_ref[...] = colcnt_ref[...]


def _merge_route(hp, on, ap, sgb, w_rnn_bf16, w_o_bf16, g_ffn, w_router_pad, b_router_pad):
    tp, d = hp.shape
    tm = ROW_TILE
    row = pl.BlockSpec((tm, d), lambda i: (i, 0))
    vec = _const_spec((1, d))
    return pl.pallas_call(
        _merge_route_kernel,
        grid=(tp // tm,),
        in_specs=[row, row, row, row, _const_spec((d, d)), _const_spec((d, d)), vec,
                  _const_spec((d, LANES)), _const_spec((1, LANES))],
        out_specs=[row, pl.BlockSpec((tm, PACK_WORDS), lambda i: (i, 0)),
                   pl.BlockSpec((tm, LANES), lambda i: (i, 0)), _const_spec((LANES, LANES))],
        out_shape=[jax.ShapeDtypeStruct((tp, d), F32), jax.ShapeDtypeStruct((tp, PACK_WORDS), jnp.uint32),
                   jax.ShapeDtypeStruct((tp, LANES), F32), jax.ShapeDtypeStruct((LANES, LANES), F32)],
        scratch_shapes=[pltpu.VMEM((tm, tm), BF16), pltpu.VMEM((1, LANES), F32),
                        pltpu.VMEM((LANES, LANES), F32)],
        compiler_params=_cparams(("arbitrary",), [True] + [False] * 8),
        name="merge_route",
    )(hp, on, ap, sgb, w_rnn_bf16, w_o_bf16, g_ffn, w_router_pad, b_router_pad)


def _slots_kernel(info_ref, cnt_ref, dest_ref, te_ref, pstart_ref):
    tm = info_ref.shape[0]
    n_tile_lanes = te_ref.shape[1]

    @pl.when(pl.program_id(0) == 0)
    def _():
        r_i = lax.broadcasted_iota(jnp.int32, (LANES, LANES), 0)
        c_i = lax.broadcasted_iota(jnp.int32, (LANES, LANES), 1)
        cnt = cnt_ref[...]
        tiles = jnp.floor((cnt + (EXPERT_TILE - 1)) / EXPERT_TILE)
        pend_col = jnp.dot((c_i <= r_i).astype(F32), tiles, preferred_element_type=F32,
                           precision=lax.Precision.HIGHEST)
        pstart_ref[...] = _tn_dot_f32(tiles, (r_i < c_i).astype(F32))[0:SUBLANES, :]
        tile_id = lax.broadcasted_iota(jnp.int32, (LANES, n_tile_lanes), 1).astype(F32)
        exp_id = lax.broadcasted_iota(jnp.int32, (LANES, n_tile_lanes), 0)
        pend_wide = jnp.concatenate([pend_col] * (n_tile_lanes // LANES), axis=1)
        below = jnp.logical_and(pend_wide <= tile_id, exp_id < N_EXPERTS)
        te = jnp.minimum(jnp.sum(below.astype(F32), axis=0, keepdims=True), N_EXPERTS - 1.0)
        n_active = pend_col[N_EXPERTS - 1:N_EXPERTS, 0:1]
        sub_t = lax.broadcasted_iota(jnp.int32, (SUBLANES, n_tile_lanes), 0)
        out = jnp.where(sub_t == 0, jnp.broadcast_to(te, (SUBLANES, n_tile_lanes)),
                        jnp.broadcast_to(n_active, (SUBLANES, n_tile_lanes)))
        te_ref[...] = out.astype(jnp.int32)

    info = info_ref[...]
    pstart_row = pstart_ref[0:1, :]
    lane = lax.broadcasted_iota(jnp.int32, (tm, LANES), 1)
    dest = jnp.zeros((tm, LANES), F32)
    for kk in range(TOP_K):
        idx_k = info[:, TOP_K + kk:TOP_K + kk + 1].astype(jnp.int32)
        start_k = jnp.sum(jnp.where(lane == idx_k, pstart_row, 0.0), axis=-1, keepdims=True)
        slot_k = start_k * EXPERT_TILE + info[:, 2 * TOP_K + kk:2 * TOP_K + kk + 1]
        dest = jnp.where(lane == kk, slot_k, dest)
    dest_ref[...] = dest.astype(jnp.int32)


def _tn_dot_f32(a, b):
    return lax.dot_general(a, b, (((0,), (0,)), ((), ())), preferred_element_type=F32,
                           precision=lax.Precision.HIGHEST)


def _slots(info, cnt, n_tiles_max):
    tp = info.shape[0]
    tm = tp // SLOTS_STEPS
    n_tile_lanes = -(-n_tiles_max // LANES) * LANES
    return pl.pallas_call(
        _slots_kernel,
        grid=(tp // tm,),
        in_specs=[pl.BlockSpec((tm, LANES), lambda i: (i, 0)), _const_spec((LANES, LANES))],
        out_specs=[pl.BlockSpec((tm, LANES), lambda i: (i, 0)), _const_spec((8, n_tile_lanes))],
        out_shape=[jax.ShapeDtypeStruct((tp, LANES), jnp.int32),
                   jax.ShapeDtypeStruct((8, n_tile_lanes), jnp.int32)],
        scratch_shapes=[pltpu.VMEM((SUBLANES, LANES), F32)],
        compiler_params=_cparams(("arbitrary",)),
        name="slots",
    )(info, cnt)


def _sc_workers():
    sc = plsc.get_sparse_core_info()
    return sc.num_cores, sc.num_cores * sc.num_subcores


def _sc_token_chunks(tp, chunk, body):
    n_cores, n_workers = _sc_workers()
    per_worker = tp // n_workers
    assert per_worker * n_workers == tp and per_worker % chunk == 0 and chunk % SUBLANES == 0
    base = (lax.axis_index("s") * n_cores + lax.axis_index("c")) * per_worker

    @pl.loop(0, per_worker // chunk)
    def _(ci):
        body(pl.multiple_of(base + ci * chunk, SUBLANES))


def _sc_kernel(out_type, chunk, n_row_bufs):
    return functools.partial(
        pl.kernel, mesh=plsc.VectorSubcoreMesh(core_axis_name="c", subcore_axis_name="s"), out_type=out_type,
        scratch_types=[pltpu.VMEM((TOP_K, chunk), jnp.int32), pltpu.VMEM((n_row_bufs, chunk, PACK_WORDS), jnp.uint32),
                       pltpu.SemaphoreType.DMA, pltpu.SemaphoreType.DMA])


def _wait_all(copies):
    for c in copies:
        c.wait()


def _dispatch(dest_flat, u2p, n_slots):
    tp = u2p.shape[0]
    ch = SC_SCATTER_CHUNK

    @_sc_kernel(jax.ShapeDtypeStruct((n_slots, PACK_WORDS), jnp.uint32), ch, 1)
    def scatter_rows(u2_hbm, dest_hbm, xs_hbm, idx_v, rows_v, sem_a, sem_b):
        def chunk(off):
            loads = [pltpu.async_copy(u2_hbm.at[pl.ds(off, ch)], rows_v.at[0], sem_a)]
            loads += [pltpu.async_copy(dest_hbm.at[pl.ds(kk * tp + off, ch)], idx_v.at[kk], sem_a)
                      for kk in range(TOP_K)]
            _wait_all(loads)
            _wait_all([pltpu.async_copy(rows_v.at[0], xs_hbm.at[idx_v.at[kk]], sem_b) for kk in range(TOP_K)])

        _sc_token_chunks(tp, ch, chunk)

    return scatter_rows(u2p, dest_flat)


def _gather_expert_rows(dest_flat, ys):
    tp = dest_flat.shape[0] // TOP_K
    ch = SC_GATHER_CHUNK

    @_sc_kernel(jax.ShapeDtypeStruct((TOP_K, tp, PACK_WORDS), jnp.uint32), ch, TOP_K)
    def gather_rows(ys_hbm, dest_hbm, out_hbm, idx_v, rows_v, sem_a, sem_b):
        def chunk(off):
            _wait_all([pltpu.async_copy(dest_hbm.at[pl.ds(kk * tp + off, ch)], idx_v.at[kk], sem_a)
                       for kk in range(TOP_K)])
            _wait_all([pltpu.async_copy(ys_hbm.at[idx_v.at[kk]], rows_v.at[kk], sem_b) for kk in range(TOP_K)])
            _wait_all([pltpu.async_copy(rows_v.at[kk], out_hbm.at[kk, pl.ds(off, ch)], sem_a)
                       for kk in range(TOP_K)])

        _sc_token_chunks(tp, ch, chunk)

    return gather_rows(ys, dest_flat)


def _expert_kernel(te_ref, nact_ref, x_ref, wgu_ref, bgu_ref, wdn_ref, bdn_ref, y_ref):
    del te_ref

    @pl.when(pl.program_id(0) < nact_ref[0])
    def _():
        x = _unpack_rows(x_ref[...]).astype(BF16)
        gu = jnp.dot(x, wgu_ref[0].astype(BF16), preferred_element_type=F32) + bgu_ref[0]
        gate = jnp.minimum(gu[:, :D_FF], SWIGLU_LIMIT)
        up = jnp.clip(gu[:, D_FF:], -SWIGLU_LIMIT, SWIGLU_LIMIT)
        hdn = (up + 1.0) * (gate * _sigmoid(SWIGLU_ALPHA * gate))
        y = jnp.dot(hdn.astype(BF16), wdn_ref[0].astype(BF16), preferred_element_type=F32) + bdn_ref[0]
        y_ref[...] = _pack_rows(y)

    @pl.when(pl.program_id(0) >= nact_ref[0])
    def _():
        y_ref[...] = jnp.zeros_like(y_ref)


def _expert_mlp(te, nact, xs, w_gu, b_gu, w_dn, b_dn):
    d = D_MODEL
    n_tiles = xs.shape[0] // EXPERT_TILE

    def tile_map(i, te_ref, nact_ref):
        return (jnp.minimum(i, nact_ref[0] - 1), 0)

    def exp_map(i, te_ref, nact_ref):
        return (te_ref[jnp.minimum(i, nact_ref[0] - 1)], 0, 0)

    grid_spec = pltpu.PrefetchScalarGridSpec(
        num_scalar_prefetch=2,
        grid=(n_tiles,),
        in_specs=[pl.BlockSpec((EXPERT_TILE, PACK_WORDS), tile_map),
                  pl.BlockSpec((1, d, 2 * D_FF), exp_map),
                  pl.BlockSpec((1, 1, 2 * D_FF), exp_map),
                  pl.BlockSpec((1, D_FF, d), exp_map),
                  pl.BlockSpec((1, 1, d), exp_map)],
        out_specs=pl.BlockSpec((EXPERT_TILE, PACK_WORDS), lambda i, te_ref, nact_ref: (i, 0)),
    )
    return pl.pallas_call(
        _expert_kernel,
        grid_spec=grid_spec,
        out_shape=jax.ShapeDtypeStruct(xs.shape, jnp.uint32),
        compiler_params=_cparams(("arbitrary",)),
        name="expert_mlp",
    )(te, nact, xs, w_gu, b_gu, w_dn, b_dn)


def _combine_kernel(gfin_ref, *refs):
    out_ref = refs[-1]
    for s in range(COMBINE_SUB):
        h1_ref, info_ref, yg_ref = refs[3 * s:3 * s + 3]
        info = info_ref[...]
        h2 = h1_ref[...]
        for kk in range(TOP_K):
            h2 = h2 + info[:, kk:kk + 1] * _unpack_rows(yg_ref[kk])
        ms = jnp.mean(h2 * h2, axis=-1, keepdims=True)
        out_ref[s * CHUNK:(s + 1) * CHUNK, :] = h2 * lax.rsqrt(ms + RMS_EPS) * gfin_ref[...]


def _combine(h1, info, g_final, yg, n_batch, seq):
    d = h1.shape[1]
    chunks = seq // CHUNK

    def padded(s):
        def index(j):
            c = j * COMBINE_SUB + s
            return (c // chunks) * (chunks + 1) + c % chunks + 1
        return index

    in_specs = [pl.BlockSpec((1, d), lambda j: (0, 0))]
    operands = [g_final]
    for s in range(COMBINE_SUB):
        chunk_of = padded(s)
        in_specs += [pl.BlockSpec((CHUNK, d), lambda j, f=chunk_of: (f(j), 0)),
                     pl.BlockSpec((CHUNK, LANES), lambda j, f=chunk_of: (f(j), 0)),
                     pl.BlockSpec((TOP_K, CHUNK, PACK_WORDS), lambda j, f=chunk_of: (0, f(j), 0))]
        operands += [h1, info, yg]
    return pl.pallas_call(
        _combine_kernel,
        grid=(n_batch * chunks // COMBINE_SUB,),
        in_specs=in_specs,
        out_specs=pl.BlockSpec((COMBINE_SUB * CHUNK, d), lambda j: (j, 0)),
        out_shape=jax.ShapeDtypeStruct((n_batch * seq, d), F32),
        compiler_params=_cparams(("parallel",)),
        name="combine",
    )(*operands)


def kernel(x, meta_tokens, lb_logits, g_mix, w_in, w_dw, b_dw, ln_g, ln_b, w_conv_out, b_conv_out,
           g_onorm, w_rnn_out, w_o, g_ffn, w_router, b_router, w_gate_up, b_gate_up, w_down, b_down,
           g_final):
    n_batch, seq, d = x.shape
    assert d == D_MODEL and w_in.shape[0] == 1, "single-layer block with D_MODEL features"
    assert seq % (CHUNK * COMBINE_SUB) == 0
    seq_pad = CHUNK + seq
    tp = n_batch * seq_pad
    assert tp % ROW_TILE == 0

    meta = jnp.broadcast_to(meta_tokens.astype(x.dtype)[None], (n_batch, N_META, d))
    hp = jnp.concatenate([jnp.zeros((n_batch, CHUNK_PAD, d), x.dtype), meta, x], axis=1).reshape(tp, d)
    vec = lambda a: a.reshape(1, -1).astype(F32)

    aglu, q, k, logf, v, og, sga, sgb = _in_proj(hp, vec(g_mix[0]), lb_logits.astype(F32), w_in[0].astype(BF16))
    ap = _conv_branch(aglu, w_dw[0].astype(F32), vec(b_dw[0]), vec(ln_g[0]), vec(ln_b[0]),
                      w_conv_out[0].astype(BF16), vec(b_conv_out[0]), sga)
    on = _hgrn2_scan(q, k, v, logf, og, vec(g_onorm[0]), n_batch)
    w_router_pad = jnp.pad(w_router[0].astype(F32), ((0, 0), (0, LANES - N_EXPERTS)))
    b_router_pad = jnp.pad(vec(b_router[0]), ((0, 0), (0, LANES - N_EXPERTS)))
    h1, u2p, info, cnt = _merge_route(hp, on, ap, sgb, w_rnn_out[0].astype(BF16), w_o[0].astype(BF16),
                                      vec(g_ffn[0]), w_router_pad, b_router_pad)

    n_tiles_max = -(-(tp * TOP_K + N_EXPERTS * (EXPERT_TILE - 1)) // EXPERT_TILE)
    dest, te = _slots(info, cnt, n_tiles_max)
    dest_flat = dest[:, :TOP_K].T.reshape(-1)
    xs = _dispatch(dest_flat, u2p, n_tiles_max * EXPERT_TILE)
    ys = _expert_mlp(te[0, :n_tiles_max], te[1, :1], xs, w_gate_up[0],
                     b_gate_up[0].reshape(N_EXPERTS, 1, -1).astype(F32), w_down[0],
                     b_down[0].reshape(N_EXPERTS, 1, -1).astype(F32))
    yg = _gather_expert_rows(dest_flat, ys)
    out = _combine(h1, info, vec(g_final), yg, n_batch, seq)
    return out.reshape(n_batch, seq, d)
```

```python
import functools

import jax
import jax.numpy as jnp
from jax import lax
from jax.experimental import pallas as pl
from jax.experimental.pallas import tpu as pltpu
from jax.experimental.pallas import tpu_sc as plsc

F32 = jnp.float32
BF16 = jnp.bfloat16

D_MODEL = 1024
N_META = 16
CHUNK = 128
CHUNK_PAD = CHUNK - N_META
CONV_WIDTH = 31
HEAD_DIM = 128
N_HEADS = D_MODEL // HEAD_DIM
N_EXPERTS = 32
TOP_K = 4
D_FF = D_MODEL
SWIGLU_LIMIT = 7.0
SWIGLU_ALPHA = 1.702
RMS_EPS = 1e-6
LN_EPS = 1e-5

LANES = 128
SUBLANES = 8
SUB_BLOCK = 32
N_SUB = CHUNK // SUB_BLOCK
HALO = 32
ROW_TILE = 640
IN_TILE = 320
SLOTS_STEPS = 5
CONV_ROWS = 64
EXPERT_TILE = 512
SC_SCATTER_CHUNK = 104
SC_GATHER_CHUNK = 40
COMBINE_SUB = 4
NEG_BIG = -1e30
VMEM_LIMIT = 56 * 1024 * 1024


def _sigmoid(x):
    return 1.0 / (1.0 + jnp.exp(-x))


def _cparams(sem):
    return pltpu.CompilerParams(dimension_semantics=sem, vmem_limit_bytes=VMEM_LIMIT)


def _const_spec(shape):
    nd = len(shape)
    return pl.BlockSpec(shape, lambda *_: (0,) * nd)


PACK_WORDS = D_MODEL // 2
HIGH_HALF = 0xFFFF0000


def _pack_rows(x):
    lo = lax.bitcast_convert_type(x[:, :PACK_WORDS].astype(BF16).astype(F32), jnp.uint32)
    hi = lax.bitcast_convert_type(x[:, PACK_WORDS:].astype(BF16).astype(F32), jnp.uint32)
    return (lo >> 16) | (hi & jnp.uint32(HIGH_HALF))


def _unpack_rows(w):
    lo = lax.bitcast_convert_type(w << 16, F32)
    hi = lax.bitcast_convert_type(w & jnp.uint32(HIGH_HALF), F32)
    return jnp.concatenate([lo, hi], axis=1)


def _in_proj_kernel(h_ref, g_ref, lbl_ref, w_ref, aglu_ref, q_ref, k_ref, logf_ref, v_ref,
                    og_ref, sga_ref, sgb_ref):
    d = D_MODEL
    h = h_ref[...]
    ms = jnp.mean(h * h, axis=-1, keepdims=True)
    u = (h * lax.rsqrt(ms + RMS_EPS) * g_ref[...]).astype(BF16)

    def proj(j):
        return jnp.dot(u, w_ref[:, j * d:(j + 1) * d], preferred_element_type=F32)

    aglu_ref[...] = (proj(0) * _sigmoid(proj(1))).astype(aglu_ref.dtype)
    zq = proj(2)
    q_ref[...] = (zq * _sigmoid(zq)).astype(q_ref.dtype)
    lbl = lbl_ref[...]
    e = jnp.exp(lbl - jnp.max(lbl, axis=0, keepdims=True))
    lb = e[0:1, :] / jnp.sum(e, axis=0, keepdims=True)
    s = _sigmoid(proj(3))
    logf_ref[...] = jnp.log(lb + (1.0 - lb) * s)
    k_ref[...] = ((1.0 - lb) * (1.0 - s)).astype(k_ref.dtype)
    v_ref[...] = proj(4).astype(v_ref.dtype)
    zg = proj(5)
    og_ref[...] = (zg * _sigmoid(zg)).astype(og_ref.dtype)
    sga_ref[...] = _sigmoid(proj(6)).astype(sga_ref.dtype)
    sgb_ref[...] = _sigmoid(proj(7)).astype(sgb_ref.dtype)


def _in_proj(hp, g_mix, lb_logits, w_in_bf16):
    tp, d = hp.shape
    tm = IN_TILE
    row = pl.BlockSpec((tm, d), lambda i: (i, 0))
    out_dtypes = [BF16, BF16, BF16, F32, BF16, BF16, BF16, BF16]
    return pl.pallas_call(
        _in_proj_kernel,
        grid=(tp // tm,),
        in_specs=[row, _const_spec((1, d)), _const_spec(lb_logits.shape),
                  pl.BlockSpec(w_in_bf16.shape, lambda i: (0, 0), pipeline_mode=pl.Buffered(1))],
        out_specs=[row] * 8,
        out_shape=[jax.ShapeDtypeStruct((tp, d), dt) for dt in out_dtypes],
        compiler_params=_cparams(("parallel",)),
        name="in_proj",
    )(hp, g_mix, lb_logits, w_in_bf16)


def _conv_kernel(halo_ref, cur_ref, wdw_ref, bdw_ref, lng_ref, lnb_ref, wout_ref, bout_ref, sga_ref,
                 out_ref, win_ref, acc_ref, shift_ref):
    tm = cur_ref.shape[0]
    win_ref[0:HALO, :] = halo_ref[...].astype(F32)
    win_ref[HALO:, :] = cur_ref[...].astype(F32)
    first_tap = HALO - (CONV_WIDTH - 1)
    for c in range(D_MODEL // LANES):
        lanes = slice(c * LANES, (c + 1) * LANES)
        for s in range(SUBLANES):
            n_rows = tm + HALO - (SUBLANES if s else 0)
            shift_ref[s, 0:n_rows, :] = win_ref[pl.ds(s, n_rows), lanes]
        w_c = wdw_ref[:, lanes]
        bias = jnp.broadcast_to(bdw_ref[:, lanes], (CONV_ROWS, LANES))

        def chunk(r, carry, lanes=lanes, w_c=w_c, bias=bias):
            r0 = pl.multiple_of(r * CONV_ROWS, CONV_ROWS)
            acc = bias
            for j in range(CONV_WIDTH):
                off = first_tap + j
                rows = pl.ds(r0 + off - off % SUBLANES, CONV_ROWS)
                acc = acc + w_c[j:j + 1, :] * shift_ref[off % SUBLANES, rows, :]
            acc_ref[pl.ds(r0, CONV_ROWS), lanes] = acc
            return carry

        lax.fori_loop(0, tm // CONV_ROWS, chunk, 0)
    a = acc_ref[...]
    mu = jnp.mean(a, axis=-1, keepdims=True)
    ac = a - mu
    var = jnp.mean(ac * ac, axis=-1, keepdims=True)
    y = ac * lax.rsqrt(var + LN_EPS) * lng_ref[...] + lnb_ref[...]
    y = y * _sigmoid(y)
    o = jnp.dot(y.astype(BF16), wout_ref[...], preferred_element_type=F32) + bout_ref[...]
    out_ref[...] = (sga_ref[...].astype(F32) * o).astype(out_ref.dtype)


def _conv_branch(aglu, w_dw, b_dw, ln_g, ln_b, w_out_bf16, b_out, sga):
    tp, d = aglu.shape
    tm = ROW_TILE
    per = tm // HALO
    row = pl.BlockSpec((tm, d), lambda i: (i, 0))
    halo = pl.BlockSpec((HALO, d), lambda i: (jnp.maximum(i * per - 1, 0), 0))
    vec = _const_spec((1, d))
    return pl.pallas_call(
        _conv_kernel,
        grid=(tp // tm,),
        in_specs=[halo, row, _const_spec(w_dw.shape), vec, vec, vec, _const_spec((d, d)), vec, row],
        out_specs=row,
        out_shape=jax.ShapeDtypeStruct((tp, d), BF16),
        scratch_shapes=[pltpu.VMEM((tm + HALO, d), F32), pltpu.VMEM((tm, d), F32),
                        pltpu.VMEM((SUBLANES, tm + HALO, LANES), F32)],
        compiler_params=_cparams(("parallel",)),
        name="conv_branch",
    )(aglu, aglu, w_dw, b_dw, ln_g, ln_b, w_out_bf16, b_out, sga)


def _split_bf16(x):
    hi = x.astype(BF16)
    return hi, (x - hi.astype(F32)).astype(BF16)


def _nt_dot(a, b):
    return lax.dot_general(a, b, (((1,), (1,)), ((), ())), preferred_element_type=F32)


def _tn_dot(a, b):
    return lax.dot_general(a, b, (((0,), (0,)), ((), ())), preferred_element_type=F32)


def _scan_kernel(q_ref, k_ref, v_ref, lf_ref, og_ref, gon_ref, o_ref, st_ref):
    @pl.when(pl.program_id(1) == 0)
    def _():
        st_ref[...] = jnp.zeros_like(st_ref)

    c = CHUNK
    row = lax.broadcasted_iota(jnp.int32, (c, c), 0)
    col = lax.broadcasted_iota(jnp.int32, (c, c), 1)
    causal = col <= row
    diag_mask = jnp.logical_and(causal, row // SUB_BLOCK == col // SUB_BLOCK)
    lf = lf_ref[...]
    lf_hi = lf.astype(BF16)
    lf_mid, lf_lo = _split_bf16(lf - lf_hi.astype(F32))
    tri = causal.astype(BF16)
    bcum = (jnp.dot(tri, lf_hi, preferred_element_type=F32)
            + (jnp.dot(tri, lf_mid, preferred_element_type=F32) + jnp.dot(tri, lf_lo, preferred_element_type=F32)))

    def bcast_rows(rows):
        return jnp.concatenate([jnp.broadcast_to(r, (SUB_BLOCK, HEAD_DIM)) for r in rows], axis=0)

    heads = [slice(h * HEAD_DIM, (h + 1) * HEAD_DIM) for h in range(N_HEADS)]
    operands = []
    for hs in heads:
        b = bcum[:, hs]
        q = q_ref[:, hs].astype(F32)
        k = k_ref[:, hs].astype(F32)
        ends = [b[i * SUB_BLOCK + SUB_BLOCK - 1:i * SUB_BLOCK + SUB_BLOCK, :] for i in range(N_SUB)]
        mids = [b[i * SUB_BLOCK + SUB_BLOCK // 2 - 1:i * SUB_BLOCK + SUB_BLOCK // 2, :] for i in range(N_SUB)]
        mid_full = bcast_rows(mids)
        b_last = ends[-1]
        zero_row = jnp.zeros_like(b_last)
        qm = q * jnp.exp(b - mid_full)
        km = k * jnp.exp(mid_full - b)
        q_parts, k_parts = [], []
        for j in range(N_SUB - 1):
            q_rows = [jnp.exp(mids[i] - ends[j]) if i > j else zero_row for i in range(N_SUB)]
            k_rows = [jnp.exp(ends[j] - mids[j]) if i == j else zero_row for i in range(N_SUB)]
            q_parts.append((qm * bcast_rows(q_rows)).astype(BF16))
            k_parts.append((km * bcast_rows(k_rows)).astype(BF16))
        q_in = (qm * bcast_rows([jnp.exp(m) for m in mids])).astype(BF16)
        k_out = (km * bcast_rows([jnp.exp(b_last - m) for m in mids])).astype(BF16)
        operands.append((qm.astype(BF16), km.astype(BF16), jnp.concatenate(q_parts, axis=1),
                         jnp.concatenate(k_parts, axis=1), q_in, k_out, jnp.exp(b_last)))
    products = []
    for h, hs in enumerate(heads):
        qm, km, q_cat, k_cat, q_in, k_out, _ = operands[h]
        products.append((_nt_dot(qm, km), _nt_dot(q_cat, k_cat), _nt_dot(q_in, st_ref[h].astype(BF16)),
                         _tn_dot(v_ref[:, hs], k_out)))
    outs = []
    for h, hs in enumerate(heads):
        same_block, earlier_blocks, from_state, state_update = products[h]
        scores = jnp.where(diag_mask, same_block, 0.0) + earlier_blocks
        outs.append(from_state + jnp.dot(scores.astype(BF16), v_ref[:, hs], preferred_element_type=F32))
        st_ref[h] = st_ref[h] * operands[h][6] + state_update
    for h, hs in enumerate(heads):
        o = outs[h]
        ms = jnp.mean(o * o, axis=-1, keepdims=True)
        on = o * lax.rsqrt(ms + RMS_EPS) * gon_ref[:, hs]
        o_ref[:, hs] = (on * og_ref[:, hs].astype(F32)).astype(o_ref.dtype)


def _hgrn2_scan(q, k, v, logf, og, g_onorm, n_batch):
    tp, d = q.shape
    n_chunks = tp // n_batch // CHUNK
    blk = pl.BlockSpec((CHUNK, d), lambda b, c: (b * n_chunks + c, 0))
    return pl.pallas_call(
        _scan_kernel,
        grid=(n_batch, n_chunks),
        in_specs=[blk, blk, blk, blk, blk, pl.BlockSpec((1, d), lambda b, c: (0, 0))],
        out_specs=blk,
        out_shape=jax.ShapeDtypeStruct((tp, d), BF16),
        scratch_shapes=[pltpu.VMEM((N_HEADS, HEAD_DIM, HEAD_DIM), F32)],
        compiler_params=_cparams(("arbitrary", "arbitrary")),
        name="hgrn2_scan",
    )(q, k, v, logf, og, g_onorm)


def _merge_route_kernel(h_ref, on_ref, ap_ref, sgb_ref, wrnn_ref, wo_ref, gffn_ref, wr_ref, br_ref,
                        h1_ref, u2_ref, info_ref, cnt_ref, tri_ref, carry_ref, colcnt_ref):
    tm = h_ref.shape[0]

    @pl.when(pl.program_id(0) == 0)
    def _():
        r_i = lax.broadcasted_iota(jnp.int32, (tm, tm), 0)
        c_i = lax.broadcasted_iota(jnp.int32, (tm, tm), 1)
        tri_ref[...] = (c_i < r_i).astype(BF16)
        carry_ref[...] = jnp.zeros_like(carry_ref)
        colcnt_ref[...] = jnp.zeros_like(colcnt_ref)

    r = jnp.dot(on_ref[...], wrnn_ref[...], preferred_element_type=F32)
    y = ap_ref[...].astype(F32) + sgb_ref[...].astype(F32) * r
    h1 = h_ref[...] + jnp.dot(y.astype(BF16), wo_ref[...], preferred_element_type=F32)
    h1_ref[...] = h1
    ms = jnp.mean(h1 * h1, axis=-1, keepdims=True)
    u2 = h1 * lax.rsqrt(ms + RMS_EPS) * gffn_ref[...]
    u2_ref[...] = _pack_rows(u2)
    u_hi, u_lo = _split_bf16(u2)
    w_hi, w_lo = _split_bf16(wr_ref[...])
    logits = (jnp.dot(u_hi, w_hi, preferred_element_type=F32)
              + (jnp.dot(u_lo, w_hi, preferred_element_type=F32) + jnp.dot(u_hi, w_lo, preferred_element_type=F32))
              + br_ref[...])
    lane = lax.broadcasted_iota(jnp.int32, (tm, LANES), 1)
    cur = jnp.where(lane < N_EXPERTS, logits, NEG_BIG)
    vals, idxs, sels = [], [], []
    for _ in range(TOP_K):
        m = jnp.max(cur, axis=-1, keepdims=True)
        idx = jnp.min(jnp.where(cur == m, lane, LANES), axis=-1, keepdims=True)
        sel = lane == idx
        cur = jnp.where(sel, 2.0 * NEG_BIG, cur)
        vals.append(m)
        idxs.append(idx)
        sels.append(sel)
    exps = [jnp.exp(vk - vals[0]) for vk in vals]
    den = exps[0] + exps[1] + exps[2] + exps[3]
    onehot = jnp.logical_or(jnp.logical_or(sels[0], sels[1]), jnp.logical_or(sels[2], sels[3]))
    onehot_bf = onehot.astype(BF16)
    rank_all = jnp.dot(tri_ref[...], onehot_bf, preferred_element_type=F32) + carry_ref[...]
    carry_ref[...] += jnp.sum(onehot.astype(F32), axis=0, keepdims=True)
    colcnt_ref[...] += _tn_dot(onehot_bf, jnp.ones((tm, LANES), BF16))
    info = jnp.zeros((tm, LANES), F32)
    for kk in range(TOP_K):
        rank_k = jnp.sum(jnp.where(sels[kk], rank_all, 0.0), axis=-1, keepdims=True)
        info = jnp.where(lane == kk, exps[kk] / den, info)
        info = jnp.where(lane == TOP_K + kk, idxs[kk].astype(F32), info)
        info = jnp.where(lane == 2 * TOP_K + kk, rank_k, info)
    info_ref[...] = info
    cnt_ref[...] = colcnt_ref[...]


def _merge_route(hp, on, ap, sgb, w_rnn_bf16, w_o_bf16, g_ffn, w_router_pad, b_router_pad):
    tp, d = hp.shape
    tm = ROW_TILE
    row = pl.BlockSpec((tm, d), lambda i: (i, 0))
    vec = _const_spec((1, d))
    return pl.pallas_call(
        _merge_route_kernel,
        grid=(tp // tm,),
        in_specs=[row, row, row, row, _const_spec((d, d)), _const_spec((d, d)), vec,
                  _const_spec((d, LANES)), _const_spec((1, LANES))],
        out_specs=[row, pl.BlockSpec((tm, PACK_WORDS), lambda i: (i, 0)),
                   pl.BlockSpec((tm, LANES), lambda i: (i, 0)), _const_spec((LANES, LANES))],
        out_shape=[jax.ShapeDtypeStruct((tp, d), F32), jax.ShapeDtypeStruct((tp, PACK_WORDS), jnp.uint32),
                   jax.ShapeDtypeStruct((tp, LANES), F32), jax.ShapeDtypeStruct((LANES, LANES), F32)],
        scratch_shapes=[pltpu.VMEM((tm, tm), BF16), pltpu.VMEM((1, LANES), F32),
                        pltpu.VMEM((LANES, LANES), F32)],
        compiler_params=_cparams(("arbitrary",)),
        name="merge_route",
    )(hp, on, ap, sgb, w_rnn_bf16, w_o_bf16, g_ffn, w_router_pad, b_router_pad)


def _slots_kernel(info_ref, cnt_ref, dest_ref, te_ref, pstart_ref):
    tm = info_ref.shape[0]
    n_tile_lanes = te_ref.shape[1]

    @pl.when(pl.program_id(0) == 0)
    def _():
        r_i = lax.broadcasted_iota(jnp.int32, (LANES, LANES), 0)
        c_i = lax.broadcasted_iota(jnp.int32, (LANES, LANES), 1)
        cnt = cnt_ref[...]
        tiles = jnp.floor((cnt + (EXPERT_TILE - 1)) / EXPERT_TILE)
        pend_col = jnp.dot((c_i <= r_i).astype(F32), tiles, preferred_element_type=F32,
                           precision=lax.Precision.HIGHEST)
        pstart_ref[...] = _tn_dot_f32(tiles, (r_i < c_i).astype(F32))[0:SUBLANES, :]
        tile_id = lax.broadcasted_iota(jnp.int32, (LANES, n_tile_lanes), 1).astype(F32)
        exp_id = lax.broadcasted_iota(jnp.int32, (LANES, n_tile_lanes), 0)
        pend_wide = jnp.concatenate([pend_col] * (n_tile_lanes // LANES), axis=1)
        below = jnp.logical_and(pend_wide <= tile_id, exp_id < N_EXPERTS)
        te = jnp.minimum(jnp.sum(below.astype(F32), axis=0, keepdims=True), N_EXPERTS - 1.0)
        n_active = pend_col[N_EXPERTS - 1:N_EXPERTS, 0:1]
        sub_t = lax.broadcasted_iota(jnp.int32, (SUBLANES, n_tile_lanes), 0)
        out = jnp.where(sub_t == 0, jnp.broadcast_to(te, (SUBLANES, n_tile_lanes)),
                        jnp.broadcast_to(n_active, (SUBLANES, n_tile_lanes)))
        te_ref[...] = out.astype(jnp.int32)

    info = info_ref[...]
    pstart_row = pstart_ref[0:1, :]
    lane = lax.broadcasted_iota(jnp.int32, (tm, LANES), 1)
    dest = jnp.zeros((tm, LANES), F32)
    for kk in range(TOP_K):
        idx_k = info[:, TOP_K + kk:TOP_K + kk + 1].astype(jnp.int32)
        start_k = jnp.sum(jnp.where(lane == idx_k, pstart_row, 0.0), axis=-1, keepdims=True)
        slot_k = start_k * EXPERT_TILE + info[:, 2 * TOP_K + kk:2 * TOP_K + kk + 1]
        dest = jnp.where(lane == kk, slot_k, dest)
    dest_ref[...] = dest.astype(jnp.int32)


def _tn_dot_f32(a, b):
    return lax.dot_general(a, b, (((0,), (0,)), ((), ())), preferred_element_type=F32,
                           precision=lax.Precision.HIGHEST)


def _slots(info, cnt, n_tiles_max):
    tp = info.shape[0]
    tm = tp // SLOTS_STEPS
    n_tile_lanes = -(-n_tiles_max // LANES) * LANES
    return pl.pallas_call(
        _slots_kernel,
        grid=(tp // tm,),
        in_specs=[pl.BlockSpec((tm, LANES), lambda i: (i, 0)), _const_spec((LANES, LANES))],
        out_specs=[pl.BlockSpec((tm, LANES), lambda i: (i, 0)), _const_spec((8, n_tile_lanes))],
        out_shape=[jax.ShapeDtypeStruct((tp, LANES), jnp.int32),
                   jax.ShapeDtypeStruct((8, n_tile_lanes), jnp.int32)],
        scratch_shapes=[pltpu.VMEM((SUBLANES, LANES), F32)],
        compiler_params=_cparams(("arbitrary",)),
        name="slots",
    )(info, cnt)


def _sc_workers():
    sc = plsc.get_sparse_core_info()
    return sc.num_cores, sc.num_cores * sc.num_subcores


def _sc_token_chunks(tp, chunk, body):
    n_cores, n_workers = _sc_workers()
    per_worker = tp // n_workers
    assert per_worker * n_workers == tp and per_worker % chunk == 0 and chunk % SUBLANES == 0
    base = (lax.axis_index("s") * n_cores + lax.axis_index("c")) * per_worker

    @pl.loop(0, per_worker // chunk)
    def _(ci):
        body(pl.multiple_of(base + ci * chunk, SUBLANES))


def _sc_kernel(out_type, chunk, n_row_bufs):
    return functools.partial(
        pl.kernel, mesh=plsc.VectorSubcoreMesh(core_axis_name="c", subcore_axis_name="s"), out_type=out_type,
        scratch_types=[pltpu.VMEM((TOP_K, chunk), jnp.int32), pltpu.VMEM((n_row_bufs, chunk, PACK_WORDS), jnp.uint32),
                       pltpu.SemaphoreType.DMA, pltpu.SemaphoreType.DMA])


def _wait_all(copies):
    for c in copies:
        c.wait()


def _dispatch(dest_flat, u2p, n_slots):
    tp = u2p.shape[0]
    ch = SC_SCATTER_CHUNK

    @_sc_kernel(jax.ShapeDtypeStruct((n_slots, PACK_WORDS), jnp.uint32), ch, 1)
    def scatter_rows(u2_hbm, dest_hbm, xs_hbm, idx_v, rows_v, sem_a, sem_b):
        def chunk(off):
            loads = [pltpu.async_copy(u2_hbm.at[pl.ds(off, ch)], rows_v.at[0], sem_a)]
            loads += [pltpu.async_copy(dest_hbm.at[pl.ds(kk * tp + off, ch)], idx_v.at[kk], sem_a)
                      for kk in range(TOP_K)]
            _wait_all(loads)
            _wait_all([pltpu.async_copy(rows_v.at[0], xs_hbm.at[idx_v.at[kk]], sem_b) for kk in range(TOP_K)])

        _sc_token_chunks(tp, ch, chunk)

    return scatter_rows(u2p, dest_flat)


def _gather_expert_rows(dest_flat, ys):
    tp = dest_flat.shape[0] // TOP_K
    ch = SC_GATHER_CHUNK

    @_sc_kernel(jax.ShapeDtypeStruct((TOP_K, tp, PACK_WORDS), jnp.uint32), ch, TOP_K)
    def gather_rows(ys_hbm, dest_hbm, out_hbm, idx_v, rows_v, sem_a, sem_b):
        def chunk(off):
            _wait_all([pltpu.async_copy(dest_hbm.at[pl.ds(kk * tp + off, ch)], idx_v.at[kk], sem_a)
                       for kk in range(TOP_K)])
            _wait_all([pltpu.async_copy(ys_hbm.at[idx_v.at[kk]], rows_v.at[kk], sem_b) for kk in range(TOP_K)])
            _wait_all([pltpu.async_copy(rows_v.at[kk], out_hbm.at[kk, pl.ds(off, ch)], sem_a)
                       for kk in range(TOP_K)])

        _sc_token_chunks(tp, ch, chunk)

    return gather_rows(ys, dest_flat)


def _expert_kernel(te_ref, nact_ref, x_ref, wgu_hbm, bgu_ref, wdn_hbm, bdn_ref, y_ref,
                   wgu_buf, wdn_buf, slot_ref, sem):
    i = pl.program_id(0)
    n_active = nact_ref[0]
    last_tile = pl.num_programs(0) - 1

    def fetch(expert, slot):
        return (pltpu.make_async_copy(wgu_hbm.at[expert], wgu_buf.at[slot], sem.at[slot, 0]),
                pltpu.make_async_copy(wdn_hbm.at[expert], wdn_buf.at[slot], sem.at[slot, 1]))

    @pl.when(i == 0)
    def _():
        slot_ref[0] = 1
        for copy in fetch(te_ref[0], 0):
            copy.start()

    @pl.when(i < n_active)
    def _():
        expert = te_ref[i]
        first_tile_of_expert = jnp.logical_or(i == 0, expert != te_ref[jnp.maximum(i - 1, 0)])

        @pl.when(first_tile_of_expert)
        def _():
            slot = 1 - slot_ref[0]
            slot_ref[0] = slot
            for copy in fetch(expert, slot):
                copy.wait()
            nxt = lax.while_loop(
                lambda j: jnp.logical_and(j < n_active, te_ref[jnp.minimum(j, last_tile)] == expert),
                lambda j: j + 1, i + 1)

            @pl.when(nxt < n_active)
            def _():
                for copy in fetch(te_ref[jnp.minimum(nxt, last_tile)], 1 - slot):
                    copy.start()

        slot = slot_ref[0]
        x = _unpack_rows(x_ref[...]).astype(BF16)
        gu = jnp.dot(x, wgu_buf[slot].astype(BF16), preferred_element_type=F32) + bgu_ref[0]
        gate = jnp.minimum(gu[:, :D_FF], SWIGLU_LIMIT)
        up = jnp.clip(gu[:, D_FF:], -SWIGLU_LIMIT, SWIGLU_LIMIT)
        hdn = (up + 1.0) * (gate * _sigmoid(SWIGLU_ALPHA * gate))
        y = jnp.dot(hdn.astype(BF16), wdn_buf[slot].astype(BF16), preferred_element_type=F32) + bdn_ref[0]
        y_ref[...] = _pack_rows(y)

    @pl.when(i >= n_active)
    def _():
        y_ref[...] = jnp.zeros_like(y_ref)


def _expert_mlp(te, nact, xs, w_gu, b_gu, w_dn, b_dn):
    d = D_MODEL
    n_tiles = xs.shape[0] // EXPERT_TILE

    def tile_map(i, te_ref, nact_ref):
        return (jnp.minimum(i, nact_ref[0] - 1), 0)

    def exp_map(i, te_ref, nact_ref):
        return (te_ref[jnp.minimum(i, nact_ref[0] - 1)], 0, 0)

    grid_spec = pltpu.PrefetchScalarGridSpec(
        num_scalar_prefetch=2,
        grid=(n_tiles,),
        in_specs=[pl.BlockSpec((EXPERT_TILE, PACK_WORDS), tile_map),
                  pl.BlockSpec(memory_space=pl.ANY),
                  pl.BlockSpec((1, 1, 2 * D_FF), exp_map),
                  pl.BlockSpec(memory_space=pl.ANY),
                  pl.BlockSpec((1, 1, d), exp_map)],
        out_specs=pl.BlockSpec((EXPERT_TILE, PACK_WORDS), lambda i, te_ref, nact_ref: (i, 0)),
        scratch_shapes=[pltpu.VMEM((2, d, 2 * D_FF), F32), pltpu.VMEM((2, D_FF, d), F32),
                        pltpu.SMEM((1,), jnp.int32), pltpu.SemaphoreType.DMA((2, 2))],
    )
    return pl.pallas_call(
        _expert_kernel,
        grid_spec=grid_spec,
        out_shape=jax.ShapeDtypeStruct(xs.shape, jnp.uint32),
        compiler_params=_cparams(("arbitrary",)),
        name="expert_mlp",
    )(te, nact, xs, w_gu, b_gu, w_dn, b_dn)


def _combine_kernel(gfin_ref, *refs):
    out_ref = refs[-1]
    for s in range(COMBINE_SUB):
        h1_ref, info_ref, yg_ref = refs[3 * s:3 * s + 3]
        info = info_ref[...]
        h2 = h1_ref[...]
        for kk in range(TOP_K):
            h2 = h2 + info[:, kk:kk + 1] * _unpack_rows(yg_ref[kk])
        ms = jnp.mean(h2 * h2, axis=-1, keepdims=True)
        out_ref[s * CHUNK:(s + 1) * CHUNK, :] = h2 * lax.rsqrt(ms + RMS_EPS) * gfin_ref[...]


def _combine(h1, info, g_final, yg, n_batch, seq):
    d = h1.shape[1]
    chunks = seq // CHUNK

    def padded(s):
        def index(j):
            c = j * COMBINE_SUB + s
            return (c // chunks) * (chunks + 1) + c % chunks + 1
        return index

    in_specs = [pl.BlockSpec((1, d), lambda j: (0, 0))]
    operands = [g_final]
    for s in range(COMBINE_SUB):
        chunk_of = padded(s)
        in_specs += [pl.BlockSpec((CHUNK, d), lambda j, f=chunk_of: (f(j), 0)),
                     pl.BlockSpec((CHUNK, LANES), lambda j, f=chunk_of: (f(j), 0)),
                     pl.BlockSpec((TOP_K, CHUNK, PACK_WORDS), lambda j, f=chunk_of: (0, f(j), 0))]
        operands += [h1, info, yg]
    return pl.pallas_call(
        _combine_kernel,
        grid=(n_batch * chunks // COMBINE_SUB,),
        in_specs=in_specs,
        out_specs=pl.BlockSpec((COMBINE_SUB * CHUNK, d), lambda j: (j, 0)),
        out_shape=jax.ShapeDtypeStruct((n_batch * seq, d), F32),
        compiler_params=_cparams(("parallel",)),
        name="combine",
    )(*operands)


def kernel(x, meta_tokens, lb_logits, g_mix, w_in, w_dw, b_dw, ln_g, ln_b, w_conv_out, b_conv_out,
           g_onorm, w_rnn_out, w_o, g_ffn, w_router, b_router, w_gate_up, b_gate_up, w_down, b_down,
           g_final):
    n_batch, seq, d = x.shape
    assert d == D_MODEL and w_in.shape[0] == 1, "single-layer block with D_MODEL features"
    assert seq % (CHUNK * COMBINE_SUB) == 0
    seq_pad = CHUNK + seq
    tp = n_batch * seq_pad
    assert tp % ROW_TILE == 0

    meta = jnp.broadcast_to(meta_tokens.astype(x.dtype)[None], (n_batch, N_META, d))
    hp = jnp.concatenate([jnp.zeros((n_batch, CHUNK_PAD, d), x.dtype), meta, x], axis=1).reshape(tp, d)
    vec = lambda a: a.reshape(1, -1).astype(F32)

    aglu, q, k, logf, v, og, sga, sgb = _in_proj(hp, vec(g_mix[0]), lb_logits.astype(F32), w_in[0].astype(BF16))
    ap = _conv_branch(aglu, w_dw[0].astype(F32), vec(b_dw[0]), vec(ln_g[0]), vec(ln_b[0]),
                      w_conv_out[0].astype(BF16), vec(b_conv_out[0]), sga)
    on = _hgrn2_scan(q, k, v, logf, og, vec(g_onorm[0]), n_batch)
    w_router_pad = jnp.pad(w_router[0].astype(F32), ((0, 0), (0, LANES - N_EXPERTS)))
    b_router_pad = jnp.pad(vec(b_router[0]), ((0, 0), (0, LANES - N_EXPERTS)))
    h1, u2p, info, cnt = _merge_route(hp, on, ap, sgb, w_rnn_out[0].astype(BF16), w_o[0].astype(BF16),
                                      vec(g_ffn[0]), w_router_pad, b_router_pad)

    n_tiles_max = -(-(tp * TOP_K + N_EXPERTS * (EXPERT_TILE - 1)) // EXPERT_TILE)
    dest, te = _slots(info, cnt, n_tiles_max)
    dest_flat = dest[:, :TOP_K].T.reshape(-1)
    xs = _dispatch(dest_flat, u2p, n_tiles_max * EXPERT_TILE)
    ys = _expert_mlp(te[0, :n_tiles_max], te[1, :1], xs, w_gate_up[0],
                     b_gate_up[0].reshape(N_EXPERTS, 1, -1).astype(F32), w_down[0],
                     b_down[0].reshape(N_EXPERTS, 1, -1).astype(F32))
    yg = _gather_expert_rows(dest_flat, ys)
    out = _combine(h1, info, vec(g_final), yg, n_batch, seq)
    return out.reshape(n_batch, seq, d)
```

```python
import functools

import jax
import jax.numpy as jnp
from jax import lax
from jax.experimental import pallas as pl
from jax.experimental.pallas import tpu as pltpu
from jax.experimental.pallas import tpu_sc as plsc

F32 = jnp.float32
BF16 = jnp.bfloat16

D_MODEL = 1024
N_META = 16
CHUNK = 128
CHUNK_PAD = CHUNK - N_META
CONV_WIDTH = 31
HEAD_DIM = 128
N_HEADS = D_MODEL // HEAD_DIM
N_EXPERTS = 32
TOP_K = 4
D_FF = D_MODEL
SWIGLU_LIMIT = 7.0
SWIGLU_ALPHA = 1.702
RMS_EPS = 1e-6
LN_EPS = 1e-5

LANES = 128
SUBLANES = 8
SUB_BLOCK = 32
N_SUB = CHUNK // SUB_BLOCK
HALO = 32
ROW_TILE = 640
IN_TILE = 320
MERGE_PARTS = 2
SLOTS_STEPS = 5
CONV_ROWS = 64
EXPERT_TILE = 512
SC_SCATTER_CHUNK = 104
SC_GATHER_CHUNK = 40
COMBINE_SUB = 4
NEG_BIG = -1e30
VMEM_LIMIT = 56 * 1024 * 1024


def _sigmoid(x):
    return 1.0 / (1.0 + jnp.exp(-x))


def _cparams(sem):
    return pltpu.CompilerParams(dimension_semantics=sem, vmem_limit_bytes=VMEM_LIMIT)


def _const_spec(shape):
    nd = len(shape)
    return pl.BlockSpec(shape, lambda *_: (0,) * nd)


PACK_WORDS = D_MODEL // 2
HIGH_HALF = 0xFFFF0000


def _pack_rows(x):
    lo = lax.bitcast_convert_type(x[:, :PACK_WORDS].astype(BF16).astype(F32), jnp.uint32)
    hi = lax.bitcast_convert_type(x[:, PACK_WORDS:].astype(BF16).astype(F32), jnp.uint32)
    return (lo >> 16) | (hi & jnp.uint32(HIGH_HALF))


def _unpack_rows(w):
    lo = lax.bitcast_convert_type(w << 16, F32)
    hi = lax.bitcast_convert_type(w & jnp.uint32(HIGH_HALF), F32)
    return jnp.concatenate([lo, hi], axis=1)


def _in_proj_kernel(h_ref, g_ref, lbl_ref, w_ref, aglu_ref, q_ref, k_ref, logf_ref, v_ref,
                    og_ref, sga_ref, sgb_ref):
    d = D_MODEL
    h = h_ref[...]
    ms = jnp.mean(h * h, axis=-1, keepdims=True)
    u = (h * lax.rsqrt(ms + RMS_EPS) * g_ref[...]).astype(BF16)

    def proj(j):
        return jnp.dot(u, w_ref[:, j * d:(j + 1) * d], preferred_element_type=F32)

    aglu_ref[...] = (proj(0) * _sigmoid(proj(1))).astype(aglu_ref.dtype)
    zq = proj(2)
    q_ref[...] = (zq * _sigmoid(zq)).astype(q_ref.dtype)
    lbl = lbl_ref[...]
    e = jnp.exp(lbl - jnp.max(lbl, axis=0, keepdims=True))
    lb = e[0:1, :] / jnp.sum(e, axis=0, keepdims=True)
    s = _sigmoid(proj(3))
    logf_ref[...] = jnp.log(lb + (1.0 - lb) * s)
    k_ref[...] = ((1.0 - lb) * (1.0 - s)).astype(k_ref.dtype)
    v_ref[...] = proj(4).astype(v_ref.dtype)
    zg = proj(5)
    og_ref[...] = (zg * _sigmoid(zg)).astype(og_ref.dtype)
    sga_ref[...] = _sigmoid(proj(6)).astype(sga_ref.dtype)
    sgb_ref[...] = _sigmoid(proj(7)).astype(sgb_ref.dtype)


def _in_proj(hp, g_mix, lb_logits, w_in_bf16):
    tp, d = hp.shape
    tm = IN_TILE
    row = pl.BlockSpec((tm, d), lambda i: (i, 0))
    out_dtypes = [BF16, BF16, BF16, F32, BF16, BF16, BF16, BF16]
    return pl.pallas_call(
        _in_proj_kernel,
        grid=(tp // tm,),
        in_specs=[row, _const_spec((1, d)), _const_spec(lb_logits.shape),
                  pl.BlockSpec(w_in_bf16.shape, lambda i: (0, 0), pipeline_mode=pl.Buffered(1))],
        out_specs=[row] * 8,
        out_shape=[jax.ShapeDtypeStruct((tp, d), dt) for dt in out_dtypes],
        compiler_params=_cparams(("parallel",)),
        name="in_proj",
    )(hp, g_mix, lb_logits, w_in_bf16)


def _conv_kernel(halo_ref, cur_ref, wdw_ref, bdw_ref, lng_ref, lnb_ref, wout_ref, bout_ref, sga_ref,
                 out_ref, win_ref, acc_ref, shift_ref):
    tm = cur_ref.shape[0]
    win_ref[0:HALO, :] = halo_ref[...].astype(F32)
    win_ref[HALO:, :] = cur_ref[...].astype(F32)
    first_tap = HALO - (CONV_WIDTH - 1)
    for c in range(D_MODEL // LANES):
        lanes = slice(c * LANES, (c + 1) * LANES)
        for s in range(SUBLANES):
            n_rows = tm + HALO - (SUBLANES if s else 0)
            shift_ref[s, 0:n_rows, :] = win_ref[pl.ds(s, n_rows), lanes]
        w_c = wdw_ref[:, lanes]
        bias = jnp.broadcast_to(bdw_ref[:, lanes], (CONV_ROWS, LANES))

        def chunk(r, carry, lanes=lanes, w_c=w_c, bias=bias):
            r0 = pl.multiple_of(r * CONV_ROWS, CONV_ROWS)
            acc = bias
            for j in range(CONV_WIDTH):
                off = first_tap + j
                rows = pl.ds(r0 + off - off % SUBLANES, CONV_ROWS)
                acc = acc + w_c[j:j + 1, :] * shift_ref[off % SUBLANES, rows, :]
            acc_ref[pl.ds(r0, CONV_ROWS), lanes] = acc
            return carry

        lax.fori_loop(0, tm // CONV_ROWS, chunk, 0)
    a = acc_ref[...]
    mu = jnp.mean(a, axis=-1, keepdims=True)
    ac = a - mu
    var = jnp.mean(ac * ac, axis=-1, keepdims=True)
    y = ac * lax.rsqrt(var + LN_EPS) * lng_ref[...] + lnb_ref[...]
    y = y * _sigmoid(y)
    o = jnp.dot(y.astype(BF16), wout_ref[...], preferred_element_type=F32) + bout_ref[...]
    out_ref[...] = (sga_ref[...].astype(F32) * o).astype(out_ref.dtype)


def _conv_branch(aglu, w_dw, b_dw, ln_g, ln_b, w_out_bf16, b_out, sga):
    tp, d = aglu.shape
    tm = ROW_TILE
    per = tm // HALO
    row = pl.BlockSpec((tm, d), lambda i: (i, 0))
    halo = pl.BlockSpec((HALO, d), lambda i: (jnp.maximum(i * per - 1, 0), 0))
    vec = _const_spec((1, d))
    return pl.pallas_call(
        _conv_kernel,
        grid=(tp // tm,),
        in_specs=[halo, row, _const_spec(w_dw.shape), vec, vec, vec, _const_spec((d, d)), vec, row],
        out_specs=row,
        out_shape=jax.ShapeDtypeStruct((tp, d), BF16),
        scratch_shapes=[pltpu.VMEM((tm + HALO, d), F32), pltpu.VMEM((tm, d), F32),
                        pltpu.VMEM((SUBLANES, tm + HALO, LANES), F32)],
        compiler_params=_cparams(("parallel",)),
        name="conv_branch",
    )(aglu, aglu, w_dw, b_dw, ln_g, ln_b, w_out_bf16, b_out, sga)


def _split_bf16(x):
    hi = x.astype(BF16)
    return hi, (x - hi.astype(F32)).astype(BF16)


def _nt_dot(a, b):
    return lax.dot_general(a, b, (((1,), (1,)), ((), ())), preferred_element_type=F32)


def _tn_dot(a, b):
    return lax.dot_general(a, b, (((0,), (0,)), ((), ())), preferred_element_type=F32)


def _scan_kernel(q_ref, k_ref, v_ref, lf_ref, og_ref, gon_ref, o_ref, st_ref):
    @pl.when(pl.program_id(1) == 0)
    def _():
        st_ref[...] = jnp.zeros_like(st_ref)

    c = CHUNK
    row = lax.broadcasted_iota(jnp.int32, (c, c), 0)
    col = lax.broadcasted_iota(jnp.int32, (c, c), 1)
    causal = col <= row
    diag_mask = jnp.logical_and(causal, row // SUB_BLOCK == col // SUB_BLOCK)
    lf = lf_ref[...]
    lf_hi = lf.astype(BF16)
    lf_mid, lf_lo = _split_bf16(lf - lf_hi.astype(F32))
    tri = causal.astype(BF16)
    bcum = (jnp.dot(tri, lf_hi, preferred_element_type=F32)
            + (jnp.dot(tri, lf_mid, preferred_element_type=F32) + jnp.dot(tri, lf_lo, preferred_element_type=F32)))

    def bcast_rows(rows):
        return jnp.concatenate([jnp.broadcast_to(r, (SUB_BLOCK, HEAD_DIM)) for r in rows], axis=0)

    heads = [slice(h * HEAD_DIM, (h + 1) * HEAD_DIM) for h in range(N_HEADS)]
    operands = []
    for hs in heads:
        b = bcum[:, hs]
        q = q_ref[:, hs].astype(F32)
        k = k_ref[:, hs].astype(F32)
        ends = [b[i * SUB_BLOCK + SUB_BLOCK - 1:i * SUB_BLOCK + SUB_BLOCK, :] for i in range(N_SUB)]
        mids = [b[i * SUB_BLOCK + SUB_BLOCK // 2 - 1:i * SUB_BLOCK + SUB_BLOCK // 2, :] for i in range(N_SUB)]
        mid_full = bcast_rows(mids)
        b_last = ends[-1]
        zero_row = jnp.zeros_like(b_last)
        qm = q * jnp.exp(b - mid_full)
        km = k * jnp.exp(mid_full - b)
        q_parts, k_parts = [], []
        for j in range(N_SUB - 1):
            q_rows = [jnp.exp(mids[i] - ends[j]) if i > j else zero_row for i in range(N_SUB)]
            k_rows = [jnp.exp(ends[j] - mids[j]) if i == j else zero_row for i in range(N_SUB)]
            q_parts.append((qm * bcast_rows(q_rows)).astype(BF16))
            k_parts.append((km * bcast_rows(k_rows)).astype(BF16))
        q_in = (qm * bcast_rows([jnp.exp(m) for m in mids])).astype(BF16)
        k_out = (km * bcast_rows([jnp.exp(b_last - m) for m in mids])).astype(BF16)
        operands.append((qm.astype(BF16), km.astype(BF16), jnp.concatenate(q_parts, axis=1),
                         jnp.concatenate(k_parts, axis=1), q_in, k_out, jnp.exp(b_last)))
    products = []
    for h, hs in enumerate(heads):
        qm, km, q_cat, k_cat, q_in, k_out, _ = operands[h]
        products.append((_nt_dot(qm, km), _nt_dot(q_cat, k_cat), _nt_dot(q_in, st_ref[h].astype(BF16)),
                         _tn_dot(v_ref[:, hs], k_out)))
    outs = []
    for h, hs in enumerate(heads):
        same_block, earlier_blocks, from_state, state_update = products[h]
        scores = jnp.where(diag_mask, same_block, 0.0) + earlier_blocks
        outs.append(from_state + jnp.dot(scores.astype(BF16), v_ref[:, hs], preferred_element_type=F32))
        st_ref[h] = st_ref[h] * operands[h][6] + state_update
    for h, hs in enumerate(heads):
        o = outs[h]
        ms = jnp.mean(o * o, axis=-1, keepdims=True)
        on = o * lax.rsqrt(ms + RMS_EPS) * gon_ref[:, hs]
        o_ref[:, hs] = (on * og_ref[:, hs].astype(F32)).astype(o_ref.dtype)


def _hgrn2_scan(q, k, v, logf, og, g_onorm, n_batch):
    tp, d = q.shape
    n_chunks = tp // n_batch // CHUNK
    blk = pl.BlockSpec((CHUNK, d), lambda b, c: (b * n_chunks + c, 0))
    return pl.pallas_call(
        _scan_kernel,
        grid=(n_batch, n_chunks),
        in_specs=[blk, blk, blk, blk, blk, pl.BlockSpec((1, d), lambda b, c: (0, 0))],
        out_specs=blk,
        out_shape=jax.ShapeDtypeStruct((tp, d), BF16),
        scratch_shapes=[pltpu.VMEM((N_HEADS, HEAD_DIM, HEAD_DIM), F32)],
        compiler_params=_cparams(("arbitrary", "arbitrary")),
        name="hgrn2_scan",
    )(q, k, v, logf, og, g_onorm)


def _merge_route_kernel(h_ref, on_ref, ap_ref, sgb_ref, wrnn_ref, wo_ref, gffn_ref, wr_ref, br_ref,
                        h1_ref, u2_ref, info_ref, cnt_ref, tri_ref, carry_ref, colcnt_ref, logits_ref):
    tm = h_ref.shape[0]
    i = pl.program_id(0)

    @pl.when(i == 0)
    def _():
        r_i = lax.broadcasted_iota(jnp.int32, (tm, tm), 0)
        c_i = lax.broadcasted_iota(jnp.int32, (tm, tm), 1)
        tri_ref[...] = (c_i < r_i).astype(BF16)
        carry_ref[...] = jnp.zeros_like(carry_ref)
        colcnt_ref[...] = jnp.zeros_like(colcnt_ref)
        logits_ref[...] = jnp.zeros_like(logits_ref)

    routed = i > 0
    lane = lax.broadcasted_iota(jnp.int32, (tm, LANES), 1)
    cur = jnp.where(lane < N_EXPERTS, logits_ref[...], NEG_BIG)
    vals, idxs, sels = [], [], []

    def topk_round(cur):
        m = jnp.max(cur, axis=-1, keepdims=True)
        idx = jnp.min(jnp.where(cur == m, lane, LANES), axis=-1, keepdims=True)
        sel = lane == idx
        vals.append(m)
        idxs.append(idx)
        sels.append(sel)
        return jnp.where(sel, 2.0 * NEG_BIG, cur)

    w_hi, w_lo = _split_bf16(wr_ref[...])
    part = tm // MERGE_PARTS
    rows = [slice(p * part, (p + 1) * part) for p in range(MERGE_PARTS)]
    rs = [jnp.dot(on_ref[rw, :], wrnn_ref[...], preferred_element_type=F32) for rw in rows]
    cur = topk_round(cur)
    ys = [(ap_ref[rw, :].astype(F32) + sgb_ref[rw, :].astype(F32) * r).astype(BF16) for rw, r in zip(rows, rs)]
    h1s = [h_ref[rw, :] + jnp.dot(y, wo_ref[...], preferred_element_type=F32) for rw, y in zip(rows, ys)]
    cur = topk_round(cur)
    u2s = []
    for rw, h1 in zip(rows, h1s):
        h1_ref[rw, :] = h1
        ms = jnp.mean(h1 * h1, axis=-1, keepdims=True)
        u2 = h1 * lax.rsqrt(ms + RMS_EPS) * gffn_ref[...]
        u2_ref[rw, :] = _pack_rows(u2)
        u2s.append(u2)
    cur = topk_round(cur)
    new_logits = []
    for u2 in u2s:
        u_hi, u_lo = _split_bf16(u2)
        new_logits.append(jnp.dot(u_hi, w_hi, preferred_element_type=F32)
                          + (jnp.dot(u_lo, w_hi, preferred_element_type=F32)
                             + jnp.dot(u_hi, w_lo, preferred_element_type=F32))
                          + br_ref[...])
    cur = topk_round(cur)
    assert len(vals) == TOP_K
    exps = [jnp.exp(vk - vals[0]) for vk in vals]
    den = exps[0] + exps[1] + exps[2] + exps[3]
    onehot = jnp.logical_or(jnp.logical_or(sels[0], sels[1]), jnp.logical_or(sels[2], sels[3]))
    onehot = jnp.logical_and(onehot, routed)
    onehot_bf = onehot.astype(BF16)
    rank_all = jnp.dot(tri_ref[...], onehot_bf, preferred_element_type=F32) + carry_ref[...]
    carry_ref[...] += jnp.sum(onehot.astype(F32), axis=0, keepdims=True)
    colcnt_ref[...] += _tn_dot(onehot_bf, jnp.ones((tm, LANES), BF16))
    info = jnp.zeros((tm, LANES), F32)
    for kk in range(TOP_K):
        rank_k = jnp.sum(jnp.where(sels[kk], rank_all, 0.0), axis=-1, keepdims=True)
        info = jnp.where(lane == kk, exps[kk] / den, info)
        info = jnp.where(lane == TOP_K + kk, idxs[kk].astype(F32), info)
        info = jnp.where(lane == 2 * TOP_K + kk, rank_k, info)
    info_ref[...] = info
    cnt_ref[...] = colcnt_ref[...]
    logits_ref[...] = jnp.concatenate(new_logits, axis=0)


def _merge_route(hp, on, ap, sgb, w_rnn_bf16, w_o_bf16, g_ffn, w_router_pad, b_router_pad):
    tp, d = hp.shape
    tm = ROW_TILE
    n_tiles = tp // tm
    this_tile = lambda i: (jnp.minimum(i, n_tiles - 1), 0)
    prev_tile = lambda i: (jnp.maximum(i - 1, 0), 0)
    row = pl.BlockSpec((tm, d), this_tile)
    vec = _const_spec((1, d))
    return pl.pallas_call(
        _merge_route_kernel,
        grid=(n_tiles + 1,),
        in_specs=[row, row, row, row, _const_spec((d, d)), _const_spec((d, d)), vec,
                  _const_spec((d, LANES)), _const_spec((1, LANES))],
        out_specs=[row, pl.BlockSpec((tm, PACK_WORDS), this_tile),
                   pl.BlockSpec((tm, LANES), prev_tile), _const_spec((LANES, LANES))],
        out_shape=[jax.ShapeDtypeStruct((tp, d), F32), jax.ShapeDtypeStruct((tp, PACK_WORDS), jnp.uint32),
                   jax.ShapeDtypeStruct((tp, LANES), F32), jax.ShapeDtypeStruct((LANES, LANES), F32)],
        scratch_shapes=[pltpu.VMEM((tm, tm), BF16), pltpu.VMEM((1, LANES), F32),
                        pltpu.VMEM((LANES, LANES), F32), pltpu.VMEM((tm, LANES), F32)],
        compiler_params=_cparams(("arbitrary",)),
        name="merge_route",
    )(hp, on, ap, sgb, w_rnn_bf16, w_o_bf16, g_ffn, w_router_pad, b_router_pad)


def _slots_kernel(info_ref, cnt_ref, dest_ref, te_ref, pstart_ref):
    tm = info_ref.shape[0]
    n_tile_lanes = te_ref.shape[1]

    @pl.when(pl.program_id(0) == 0)
    def _():
        r_i = lax.broadcasted_iota(jnp.int32, (LANES, LANES), 0)
        c_i = lax.broadcasted_iota(jnp.int32, (LANES, LANES), 1)
        cnt = cnt_ref[...]
        tiles = jnp.floor((cnt + (EXPERT_TILE - 1)) / EXPERT_TILE)
        pend_col = jnp.dot((c_i <= r_i).astype(F32), tiles, preferred_element_type=F32,
                           precision=lax.Precision.HIGHEST)
        pstart_ref[...] = _tn_dot_f32(tiles, (r_i < c_i).astype(F32))[0:SUBLANES, :]
        tile_id = lax.broadcasted_iota(jnp.int32, (LANES, n_tile_lanes), 1).astype(F32)
        exp_id = lax.broadcasted_iota(jnp.int32, (LANES, n_tile_lanes), 0)
        pend_wide = jnp.concatenate([pend_col] * (n_tile_lanes // LANES), axis=1)
        below = jnp.logical_and(pend_wide <= tile_id, exp_id < N_EXPERTS)
        te = jnp.minimum(jnp.sum(below.astype(F32), axis=0, keepdims=True), N_EXPERTS - 1.0)
        n_active = pend_col[N_EXPERTS - 1:N_EXPERTS, 0:1]
        sub_t = lax.broadcasted_iota(jnp.int32, (SUBLANES, n_tile_lanes), 0)
        out = jnp.where(sub_t == 0, jnp.broadcast_to(te, (SUBLANES, n_tile_lanes)),
                        jnp.broadcast_to(n_active, (SUBLANES, n_tile_lanes)))
        te_ref[...] = out.astype(jnp.int32)

    info = info_ref[...]
    pstart_row = pstart_ref[0:1, :]
    lane = lax.broadcasted_iota(jnp.int32, (tm, LANES), 1)
    dest = jnp.zeros((tm, LANES), F32)
    for kk in range(TOP_K):
        idx_k = info[:, TOP_K + kk:TOP_K + kk + 1].astype(jnp.int32)
        start_k = jnp.sum(jnp.where(lane == idx_k, pstart_row, 0.0), axis=-1, keepdims=True)
        slot_k = start_k * EXPERT_TILE + info[:, 2 * TOP_K + kk:2 * TOP_K + kk + 1]
        dest = jnp.where(lane == kk, slot_k, dest)
    dest_ref[...] = dest.astype(jnp.int32)


def _tn_dot_f32(a, b):
    return lax.dot_general(a, b, (((0,), (0,)), ((), ())), preferred_element_type=F32,
                           precision=lax.Precision.HIGHEST)


def _slots(info, cnt, n_tiles_max):
    tp = info.shape[0]
    tm = tp // SLOTS_STEPS
    n_tile_lanes = -(-n_tiles_max // LANES) * LANES
    return pl.pallas_call(
        _slots_kernel,
        grid=(tp // tm,),
        in_specs=[pl.BlockSpec((tm, LANES), lambda i: (i, 0)), _const_spec((LANES, LANES))],
        out_specs=[pl.BlockSpec((tm, LANES), lambda i: (i, 0)), _const_spec((8, n_tile_lanes))],
        out_shape=[jax.ShapeDtypeStruct((tp, LANES), jnp.int32),
                   jax.ShapeDtypeStruct((8, n_tile_lanes), jnp.int32)],
        scratch_shapes=[pltpu.VMEM((SUBLANES, LANES), F32)],
        compiler_params=_cparams(("arbitrary",)),
        name="slots",
    )(info, cnt)


def _sc_workers():
    sc = plsc.get_sparse_core_info()
    return sc.num_cores, sc.num_cores * sc.num_subcores


def _sc_token_chunks(tp, chunk, body):
    n_cores, n_workers = _sc_workers()
    per_worker = tp // n_workers
    assert per_worker * n_workers == tp and per_worker % chunk == 0 and chunk % SUBLANES == 0
    base = (lax.axis_index("s") * n_cores + lax.axis_index("c")) * per_worker

    @pl.loop(0, per_worker // chunk)
    def _(ci):
        body(pl.multiple_of(base + ci * chunk, SUBLANES))


def _sc_kernel(out_type, chunk, n_row_bufs):
    return functools.partial(
        pl.kernel, mesh=plsc.VectorSubcoreMesh(core_axis_name="c", subcore_axis_name="s"), out_type=out_type,
        scratch_types=[pltpu.VMEM((TOP_K, chunk), jnp.int32), pltpu.VMEM((n_row_bufs, chunk, PACK_WORDS), jnp.uint32),
                       pltpu.SemaphoreType.DMA, pltpu.SemaphoreType.DMA])


def _wait_all(copies):
    for c in copies:
        c.wait()


def _dispatch(dest_flat, u2p, n_slots):
    tp = u2p.shape[0]
    ch = SC_SCATTER_CHUNK

    @_sc_kernel(jax.ShapeDtypeStruct((n_slots, PACK_WORDS), jnp.uint32), ch, 1)
    def scatter_rows(u2_hbm, dest_hbm, xs_hbm, idx_v, rows_v, sem_a, sem_b):
        def chunk(off):
            loads = [pltpu.async_copy(u2_hbm.at[pl.ds(off, ch)], rows_v.at[0], sem_a)]
            loads += [pltpu.async_copy(dest_hbm.at[pl.ds(kk * tp + off, ch)], idx_v.at[kk], sem_a)
                      for kk in range(TOP_K)]
            _wait_all(loads)
            _wait_all([pltpu.async_copy(rows_v.at[0], xs_hbm.at[idx_v.at[kk]], sem_b) for kk in range(TOP_K)])

        _sc_token_chunks(tp, ch, chunk)

    return scatter_rows(u2p, dest_flat)


def _gather_expert_rows(dest_flat, ys):
    tp = dest_flat.shape[0] // TOP_K
    ch = SC_GATHER_CHUNK

    @_sc_kernel(jax.ShapeDtypeStruct((TOP_K, tp, PACK_WORDS), jnp.uint32), ch, TOP_K)
    def gather_rows(ys_hbm, dest_hbm, out_hbm, idx_v, rows_v, sem_a, sem_b):
        def chunk(off):
            _wait_all([pltpu.async_copy(dest_hbm.at[pl.ds(kk * tp + off, ch)], idx_v.at[kk], sem_a)
                       for kk in range(TOP_K)])
            _wait_all([pltpu.async_copy(ys_hbm.at[idx_v.at[kk]], rows_v.at[kk], sem_b) for kk in range(TOP_K)])
            _wait_all([pltpu.async_copy(rows_v.at[kk], out_hbm.at[kk, pl.ds(off, ch)], sem_a)
                       for kk in range(TOP_K)])

        _sc_token_chunks(tp, ch, chunk)

    return gather_rows(ys, dest_flat)


def _expert_kernel(te_ref, nact_ref, x_ref, wgu_hbm, bgu_ref, wdn_hbm, bdn_ref, y_ref,
                   wgu_buf, wdn_buf, slot_ref, sem):
    i = pl.program_id(0)
    n_active = nact_ref[0]
    last_tile = pl.num_programs(0) - 1

    def fetch(expert, slot):
        return (pltpu.make_async_copy(wgu_hbm.at[expert], wgu_buf.at[slot], sem.at[slot, 0]),
                pltpu.make_async_copy(wdn_hbm.at[expert], wdn_buf.at[slot], sem.at[slot, 1]))

    @pl.when(i == 0)
    def _():
        slot_ref[0] = 1
        for copy in fetch(te_ref[0], 0):
            copy.start()

    @pl.when(i < n_active)
    def _():
        expert = te_ref[i]
        first_tile_of_expert = jnp.logical_or(i == 0, expert != te_ref[jnp.maximum(i - 1, 0)])

        @pl.when(first_tile_of_expert)
        def _():
            slot = 1 - slot_ref[0]
            slot_ref[0] = slot
            for copy in fetch(expert, slot):
                copy.wait()
            nxt = lax.while_loop(
                lambda j: jnp.logical_and(j < n_active, te_ref[jnp.minimum(j, last_tile)] == expert),
                lambda j: j + 1, i + 1)

            @pl.when(nxt < n_active)
            def _():
                for copy in fetch(te_ref[jnp.minimum(nxt, last_tile)], 1 - slot):
                    copy.start()

        slot = slot_ref[0]
        x = _unpack_rows(x_ref[...]).astype(BF16)
        gu = jnp.dot(x, wgu_buf[slot].astype(BF16), preferred_element_type=F32) + bgu_ref[0]
        gate = jnp.minimum(gu[:, :D_FF], SWIGLU_LIMIT)
        up = jnp.clip(gu[:, D_FF:], -SWIGLU_LIMIT, SWIGLU_LIMIT)
        hdn = (up + 1.0) * (gate * _sigmoid(SWIGLU_ALPHA * gate))
        y = jnp.dot(hdn.astype(BF16), wdn_buf[slot].astype(BF16), preferred_element_type=F32) + bdn_ref[0]
        y_ref[...] = _pack_rows(y)

    @pl.when(i >= n_active)
    def _():
        y_ref[...] = jnp.zeros_like(y_ref)


def _expert_mlp(te, nact, xs, w_gu, b_gu, w_dn, b_dn):
    d = D_MODEL
    n_tiles = xs.shape[0] // EXPERT_TILE

    def tile_map(i, te_ref, nact_ref):
        return (jnp.minimum(i, nact_ref[0] - 1), 0)

    def exp_map(i, te_ref, nact_ref):
        return (te_ref[jnp.minimum(i, nact_ref[0] - 1)], 0, 0)

    grid_spec = pltpu.PrefetchScalarGridSpec(
        num_scalar_prefetch=2,
        grid=(n_tiles,),
        in_specs=[pl.BlockSpec((EXPERT_TILE, PACK_WORDS), tile_map),
                  pl.BlockSpec(memory_space=pl.ANY),
                  pl.BlockSpec((1, 1, 2 * D_FF), exp_map),
                  pl.BlockSpec(memory_space=pl.ANY),
                  pl.BlockSpec((1, 1, d), exp_map)],
        out_specs=pl.BlockSpec((EXPERT_TILE, PACK_WORDS), lambda i, te_ref, nact_ref: (i, 0)),
        scratch_shapes=[pltpu.VMEM((2, d, 2 * D_FF), F32), pltpu.VMEM((2, D_FF, d), F32),
                        pltpu.SMEM((1,), jnp.int32), pltpu.SemaphoreType.DMA((2, 2))],
    )
    return pl.pallas_call(
        _expert_kernel,
        grid_spec=grid_spec,
        out_shape=jax.ShapeDtypeStruct(xs.shape, jnp.uint32),
        compiler_params=_cparams(("arbitrary",)),
        name="expert_mlp",
    )(te, nact, xs, w_gu, b_gu, w_dn, b_dn)


def _combine_kernel(gfin_ref, *refs):
    out_ref = refs[-1]
    for s in range(COMBINE_SUB):
        h1_ref, info_ref, yg_ref = refs[3 * s:3 * s + 3]
        info = info_ref[...]
        h2 = h1_ref[...]
        for kk in range(TOP_K):
            h2 = h2 + info[:, kk:kk + 1] * _unpack_rows(yg_ref[kk])
        ms = jnp.mean(h2 * h2, axis=-1, keepdims=True)
        out_ref[s * CHUNK:(s + 1) * CHUNK, :] = h2 * lax.rsqrt(ms + RMS_EPS) * gfin_ref[...]


def _combine(h1, info, g_final, yg, n_batch, seq):
    d = h1.shape[1]
    chunks = seq // CHUNK

    def padded(s):
        def index(j):
            c = j * COMBINE_SUB + s
            return (c // chunks) * (chunks + 1) + c % chunks + 1
        return index

    in_specs = [pl.BlockSpec((1, d), lambda j: (0, 0))]
    operands = [g_final]
    for s in range(COMBINE_SUB):
        chunk_of = padded(s)
        in_specs += [pl.BlockSpec((CHUNK, d), lambda j, f=chunk_of: (f(j), 0)),
                     pl.BlockSpec((CHUNK, LANES), lambda j, f=chunk_of: (f(j), 0)),
                     pl.BlockSpec((TOP_K, CHUNK, PACK_WORDS), lambda j, f=chunk_of: (0, f(j), 0))]
        operands += [h1, info, yg]
    return pl.pallas_call(
        _combine_kernel,
        grid=(n_batch * chunks // COMBINE_SUB,),
        in_specs=in_specs,
        out_specs=pl.BlockSpec((COMBINE_SUB * CHUNK, d), lambda j: (j, 0)),
        out_shape=jax.ShapeDtypeStruct((n_batch * seq, d), F32),
        compiler_params=_cparams(("parallel",)),
        name="combine",
    )(*operands)


def kernel(x, meta_tokens, lb_logits, g_mix, w_in, w_dw, b_dw, ln_g, ln_b, w_conv_out, b_conv_out,
           g_onorm, w_rnn_out, w_o, g_ffn, w_router, b_router, w_gate_up, b_gate_up, w_down, b_down,
           g_final):
    n_batch, seq, d = x.shape
    assert d == D_MODEL and w_in.shape[0] == 1, "single-layer block with D_MODEL features"
    assert seq % (CHUNK * COMBINE_SUB) == 0
    seq_pad = CHUNK + seq
    tp = n_batch * seq_pad
    assert tp % ROW_TILE == 0

    meta = jnp.broadcast_to(meta_tokens.astype(x.dtype)[None], (n_batch, N_META, d))
    hp = jnp.concatenate([jnp.zeros((n_batch, CHUNK_PAD, d), x.dtype), meta, x], axis=1).reshape(tp, d)
    vec = lambda a: a.reshape(1, -1).astype(F32)

    aglu, q, k, logf, v, og, sga, sgb = _in_proj(hp, vec(g_mix[0]), lb_logits.astype(F32), w_in[0].astype(BF16))
    ap = _conv_branch(aglu, w_dw[0].astype(F32), vec(b_dw[0]), vec(ln_g[0]), vec(ln_b[0]),
                      w_conv_out[0].astype(BF16), vec(b_conv_out[0]), sga)
    on = _hgrn2_scan(q, k, v, logf, og, vec(g_onorm[0]), n_batch)
    w_router_pad = jnp.pad(w_router[0].astype(F32), ((0, 0), (0, LANES - N_EXPERTS)))
    b_router_pad = jnp.pad(vec(b_router[0]), ((0, 0), (0, LANES - N_EXPERTS)))
    h1, u2p, info, cnt = _merge_route(hp, on, ap, sgb, w_rnn_out[0].astype(BF16), w_o[0].astype(BF16),
                                      vec(g_ffn[0]), w_router_pad, b_router_pad)

    n_tiles_max = -(-(tp * TOP_K + N_EXPERTS * (EXPERT_TILE - 1)) // EXPERT_TILE)
    dest, te = _slots(info, cnt, n_tiles_max)
    dest_flat = dest[:, :TOP_K].T.reshape(-1)
    xs = _dispatch(dest_flat, u2p, n_tiles_max * EXPERT_TILE)
    ys = _expert_mlp(te[0, :n_tiles_max], te[1, :1], xs, w_gate_up[0],
                     b_gate_up[0].reshape(N_EXPERTS, 1, -1).astype(F32), w_down[0],
                     b_down[0].reshape(N_EXPERTS, 1, -1).astype(F32))
    yg = _gather_expert_rows(dest_flat, ys)
    out = _combine(h1, info, vec(g_final), yg, n_batch, seq)
    return out.reshape(n_batch, seq, d)
```

```python
import functools

import jax
import jax.numpy as jnp
from jax import lax
from jax.experimental import pallas as pl
from jax.experimental.pallas import tpu as pltpu
from jax.experimental.pallas import tpu_sc as plsc

F32 = jnp.float32
BF16 = jnp.bfloat16

D_MODEL = 1024
N_META = 16
CHUNK = 128
CHUNK_PAD = CHUNK - N_META
CONV_WIDTH = 31
HEAD_DIM = 128
N_HEADS = D_MODEL // HEAD_DIM
N_EXPERTS = 32
TOP_K = 4
D_FF = D_MODEL
SWIGLU_LIMIT = 7.0
SWIGLU_ALPHA = 1.702
RMS_EPS = 1e-6
LN_EPS = 1e-5

LANES = 128
SUBLANES = 8
SUB_BLOCK = 32
N_SUB = CHUNK // SUB_BLOCK
HALO = 32
ROW_TILE = 640
IN_TILE = 320
SCAN_CHUNKS = 5
MERGE_PARTS = 2
SLOTS_STEPS = 5
CONV_ROWS = 64
EXPERT_TILE = 512
SC_SCATTER_CHUNK = 104
SC_GATHER_CHUNK = 40
COMBINE_SUB = 4
NEG_BIG = -1e30
VMEM_LIMIT = 56 * 1024 * 1024


def _sigmoid(x):
    return 1.0 / (1.0 + jnp.exp(-x))


def _cparams(sem):
    return pltpu.CompilerParams(dimension_semantics=sem, vmem_limit_bytes=VMEM_LIMIT)


def _const_spec(shape):
    nd = len(shape)
    return pl.BlockSpec(shape, lambda *_: (0,) * nd)


PACK_WORDS = D_MODEL // 2
HIGH_HALF = 0xFFFF0000


def _pack_rows(x):
    lo = lax.bitcast_convert_type(x[:, :PACK_WORDS].astype(BF16).astype(F32), jnp.uint32)
    hi = lax.bitcast_convert_type(x[:, PACK_WORDS:].astype(BF16).astype(F32), jnp.uint32)
    return (lo >> 16) | (hi & jnp.uint32(HIGH_HALF))


def _unpack_rows(w):
    lo = lax.bitcast_convert_type(w << 16, F32)
    hi = lax.bitcast_convert_type(w & jnp.uint32(HIGH_HALF), F32)
    return jnp.concatenate([lo, hi], axis=1)


def _in_proj_kernel(h_ref, g_ref, lbl_ref, w_ref, aglu_ref, q_ref, k_ref, logf_ref, v_ref,
                    og_ref, sga_ref, sgb_ref):
    d = D_MODEL
    h = h_ref[...]
    ms = jnp.mean(h * h, axis=-1, keepdims=True)
    u = (h * lax.rsqrt(ms + RMS_EPS) * g_ref[...]).astype(BF16)

    def proj(j):
        return jnp.dot(u, w_ref[:, j * d:(j + 1) * d], preferred_element_type=F32)

    aglu_ref[...] = (proj(0) * _sigmoid(proj(1))).astype(aglu_ref.dtype)
    zq = proj(2)
    q_ref[...] = (zq * _sigmoid(zq)).astype(q_ref.dtype)
    lbl = lbl_ref[...]
    e = jnp.exp(lbl - jnp.max(lbl, axis=0, keepdims=True))
    lb = e[0:1, :] / jnp.sum(e, axis=0, keepdims=True)
    s = _sigmoid(proj(3))
    logf_ref[...] = jnp.log(lb + (1.0 - lb) * s)
    k_ref[...] = ((1.0 - lb) * (1.0 - s)).astype(k_ref.dtype)
    v_ref[...] = proj(4).astype(v_ref.dtype)
    zg = proj(5)
    og_ref[...] = (zg * _sigmoid(zg)).astype(og_ref.dtype)
    sga_ref[...] = _sigmoid(proj(6)).astype(sga_ref.dtype)
    sgb_ref[...] = _sigmoid(proj(7)).astype(sgb_ref.dtype)


def _in_proj(hp, g_mix, lb_logits, w_in_bf16):
    tp, d = hp.shape
    tm = IN_TILE
    row = pl.BlockSpec((tm, d), lambda i: (i, 0))
    out_dtypes = [BF16, BF16, BF16, F32, BF16, BF16, BF16, BF16]
    return pl.pallas_call(
        _in_proj_kernel,
        grid=(tp // tm,),
        in_specs=[row, _const_spec((1, d)), _const_spec(lb_logits.shape),
                  pl.BlockSpec(w_in_bf16.shape, lambda i: (0, 0), pipeline_mode=pl.Buffered(1))],
        out_specs=[row] * 8,
        out_shape=[jax.ShapeDtypeStruct((tp, d), dt) for dt in out_dtypes],
        compiler_params=_cparams(("parallel",)),
        name="in_proj",
    )(hp, g_mix, lb_logits, w_in_bf16)


def _conv_kernel(halo_ref, cur_ref, wdw_ref, bdw_ref, lng_ref, lnb_ref, wout_ref, bout_ref, sga_ref,
                 out_ref, win_ref, acc_ref, shift_ref):
    tm = cur_ref.shape[0]
    win_ref[0:HALO, :] = halo_ref[...].astype(F32)
    win_ref[HALO:, :] = cur_ref[...].astype(F32)
    first_tap = HALO - (CONV_WIDTH - 1)
    for c in range(D_MODEL // LANES):
        lanes = slice(c * LANES, (c + 1) * LANES)
        for s in range(SUBLANES):
            n_rows = tm + HALO - (SUBLANES if s else 0)
            shift_ref[s, 0:n_rows, :] = win_ref[pl.ds(s, n_rows), lanes]
        w_c = wdw_ref[:, lanes]
        bias = jnp.broadcast_to(bdw_ref[:, lanes], (CONV_ROWS, LANES))

        def chunk(r, carry, lanes=lanes, w_c=w_c, bias=bias):
            r0 = pl.multiple_of(r * CONV_ROWS, CONV_ROWS)
            acc = bias
            for j in range(CONV_WIDTH):
                off = first_tap + j
                rows = pl.ds(r0 + off - off % SUBLANES, CONV_ROWS)
                acc = acc + w_c[j:j + 1, :] * shift_ref[off % SUBLANES, rows, :]
            acc_ref[pl.ds(r0, CONV_ROWS), lanes] = acc
            return carry

        lax.fori_loop(0, tm // CONV_ROWS, chunk, 0)
    a = acc_ref[...]
    mu = jnp.mean(a, axis=-1, keepdims=True)
    ac = a - mu
    var = jnp.mean(ac * ac, axis=-1, keepdims=True)
    y = ac * lax.rsqrt(var + LN_EPS) * lng_ref[...] + lnb_ref[...]
    y = y * _sigmoid(y)
    o = jnp.dot(y.astype(BF16), wout_ref[...], preferred_element_type=F32) + bout_ref[...]
    out_ref[...] = (sga_ref[...].astype(F32) * o).astype(out_ref.dtype)


def _conv_branch(aglu, w_dw, b_dw, ln_g, ln_b, w_out_bf16, b_out, sga):
    tp, d = aglu.shape
    tm = ROW_TILE
    per = tm // HALO
    row = pl.BlockSpec((tm, d), lambda i: (i, 0))
    halo = pl.BlockSpec((HALO, d), lambda i: (jnp.maximum(i * per - 1, 0), 0))
    vec = _const_spec((1, d))
    return pl.pallas_call(
        _conv_kernel,
        grid=(tp // tm,),
        in_specs=[halo, row, _const_spec(w_dw.shape), vec, vec, vec, _const_spec((d, d)), vec, row],
        out_specs=row,
        out_shape=jax.ShapeDtypeStruct((tp, d), BF16),
        scratch_shapes=[pltpu.VMEM((tm + HALO, d), F32), pltpu.VMEM((tm, d), F32),
                        pltpu.VMEM((SUBLANES, tm + HALO, LANES), F32)],
        compiler_params=_cparams(("parallel",)),
        name="conv_branch",
    )(aglu, aglu, w_dw, b_dw, ln_g, ln_b, w_out_bf16, b_out, sga)


def _split_bf16(x):
    hi = x.astype(BF16)
    return hi, (x - hi.astype(F32)).astype(BF16)


def _nt_dot(a, b):
    return lax.dot_general(a, b, (((1,), (1,)), ((), ())), preferred_element_type=F32)


def _tn_dot(a, b):
    return lax.dot_general(a, b, (((0,), (0,)), ((), ())), preferred_element_type=F32)


def _scan_kernel(q_ref, k_ref, v_ref, lf_ref, og_ref, gon_ref, o_ref, st_ref):
    @pl.when(pl.program_id(1) == 0)
    def _():
        st_ref[...] = jnp.zeros_like(st_ref)

    c = CHUNK
    row = lax.broadcasted_iota(jnp.int32, (c, c), 0)
    col = lax.broadcasted_iota(jnp.int32, (c, c), 1)
    causal = col <= row
    diag_mask = jnp.logical_and(causal, row // SUB_BLOCK == col // SUB_BLOCK)
    tri = causal.astype(BF16)

    def bcast_rows(rows):
        return jnp.concatenate([jnp.broadcast_to(r, (SUB_BLOCK, HEAD_DIM)) for r in rows], axis=0)

    heads = [slice(h * HEAD_DIM, (h + 1) * HEAD_DIM) for h in range(N_HEADS)]
    chunks = [slice(ci * c, (ci + 1) * c) for ci in range(SCAN_CHUNKS)]

    bcums = []
    for cs in chunks:
        lf = lf_ref[cs, :]
        lf_hi = lf.astype(BF16)
        lf_mid, lf_lo = _split_bf16(lf - lf_hi.astype(F32))
        bcums.append(jnp.dot(tri, lf_hi, preferred_element_type=F32)
                     + (jnp.dot(tri, lf_mid, preferred_element_type=F32)
                        + jnp.dot(tri, lf_lo, preferred_element_type=F32)))
    operands = {}
    for ci, cs in enumerate(chunks):
        for h, hs in enumerate(heads):
            b = bcums[ci][:, hs]
            q = q_ref[cs, hs].astype(F32)
            k = k_ref[cs, hs].astype(F32)
            ends = [b[i * SUB_BLOCK + SUB_BLOCK - 1:i * SUB_BLOCK + SUB_BLOCK, :] for i in range(N_SUB)]
            mids = [b[i * SUB_BLOCK + SUB_BLOCK // 2 - 1:i * SUB_BLOCK + SUB_BLOCK // 2, :] for i in range(N_SUB)]
            mid_full = bcast_rows(mids)
            b_last = ends[-1]
            zero_row = jnp.zeros_like(b_last)
            qm = q * jnp.exp(b - mid_full)
            km = k * jnp.exp(mid_full - b)
            q_parts, k_parts = [], []
            for j in range(N_SUB - 1):
                q_rows = [jnp.exp(mids[i] - ends[j]) if i > j else zero_row for i in range(N_SUB)]
                k_rows = [jnp.exp(ends[j] - mids[j]) if i == j else zero_row for i in range(N_SUB)]
                q_parts.append((qm * bcast_rows(q_rows)).astype(BF16))
                k_parts.append((km * bcast_rows(k_rows)).astype(BF16))
            q_in = (qm * bcast_rows([jnp.exp(m) for m in mids])).astype(BF16)
            k_out = (km * bcast_rows([jnp.exp(b_last - m) for m in mids])).astype(BF16)
            operands[ci, h] = (qm.astype(BF16), km.astype(BF16), jnp.concatenate(q_parts, axis=1),
                               jnp.concatenate(k_parts, axis=1), q_in, k_out, jnp.exp(b_last))
    products = {}
    for ci, cs in enumerate(chunks):
        for h, hs in enumerate(heads):
            qm, km, q_cat, k_cat, _, k_out, _ = operands[ci, h]
            products[ci, h] = (_nt_dot(qm, km), _nt_dot(q_cat, k_cat), _tn_dot(v_ref[cs, hs], k_out))
    within = {}
    for ci, cs in enumerate(chunks):
        for h, hs in enumerate(heads):
            same_block, earlier_blocks, _ = products[ci, h]
            scores = jnp.where(diag_mask, same_block, 0.0) + earlier_blocks
            within[ci, h] = jnp.dot(scores.astype(BF16), v_ref[cs, hs], preferred_element_type=F32)
    states = [st_ref[h] for h in range(N_HEADS)]
    outs = {}
    for ci in range(SCAN_CHUNKS):
        for h in range(N_HEADS):
            outs[ci, h] = within[ci, h] + _nt_dot(operands[ci, h][4], states[h].astype(BF16))
            states[h] = states[h] * operands[ci, h][6] + products[ci, h][2]
    for h in range(N_HEADS):
        st_ref[h] = states[h]
    for ci, cs in enumerate(chunks):
        for h, hs in enumerate(heads):
            o = outs[ci, h]
            ms = jnp.mean(o * o, axis=-1, keepdims=True)
            on = o * lax.rsqrt(ms + RMS_EPS) * gon_ref[:, hs]
            o_ref[cs, hs] = (on * og_ref[cs, hs].astype(F32)).astype(o_ref.dtype)


def _hgrn2_scan(q, k, v, logf, og, g_onorm, n_batch):
    tp, d = q.shape
    rows = SCAN_CHUNKS * CHUNK
    n_chunks = tp // n_batch // rows
    assert n_chunks * rows * n_batch == tp
    blk = pl.BlockSpec((rows, d), lambda b, c: (b * n_chunks + c, 0))
    return pl.pallas_call(
        _scan_kernel,
        grid=(n_batch, n_chunks),
        in_specs=[blk, blk, blk, blk, blk, pl.BlockSpec((1, d), lambda b, c: (0, 0))],
        out_specs=blk,
        out_shape=jax.ShapeDtypeStruct((tp, d), BF16),
        scratch_shapes=[pltpu.VMEM((N_HEADS, HEAD_DIM, HEAD_DIM), F32)],
        compiler_params=_cparams(("arbitrary", "arbitrary")),
        name="hgrn2_scan",
    )(q, k, v, logf, og, g_onorm)


def _merge_route_kernel(h_ref, on_ref, ap_ref, sgb_ref, wrnn_ref, wo_ref, gffn_ref, wr_ref, br_ref,
                        h1_ref, u2_ref, info_ref, cnt_ref, tri_ref, carry_ref, colcnt_ref, logits_ref):
    tm = h_ref.shape[0]
    i = pl.program_id(0)

    @pl.when(i == 0)
    def _():
        r_i = lax.broadcasted_iota(jnp.int32, (tm, tm), 0)
        c_i = lax.broadcasted_iota(jnp.int32, (tm, tm), 1)
        tri_ref[...] = (c_i < r_i).astype(BF16)
        carry_ref[...] = jnp.zeros_like(carry_ref)
        colcnt_ref[...] = jnp.zeros_like(colcnt_ref)
        logits_ref[...] = jnp.zeros_like(logits_ref)

    routed = i > 0
    lane = lax.broadcasted_iota(jnp.int32, (tm, LANES), 1)
    cur = jnp.where(lane < N_EXPERTS, logits_ref[...], NEG_BIG)
    vals, idxs, sels = [], [], []

    def topk_round(cur):
        m = jnp.max(cur, axis=-1, keepdims=True)
        idx = jnp.min(jnp.where(cur == m, lane, LANES), axis=-1, keepdims=True)
        sel = lane == idx
        vals.append(m)
        idxs.append(idx)
        sels.append(sel)
        return jnp.where(sel, 2.0 * NEG_BIG, cur)

    w_hi, w_lo = _split_bf16(wr_ref[...])
    part = tm // MERGE_PARTS
    rows = [slice(p * part, (p + 1) * part) for p in range(MERGE_PARTS)]
    rs = [jnp.dot(on_ref[rw, :], wrnn_ref[...], preferred_element_type=F32) for rw in rows]
    cur = topk_round(cur)
    ys = [(ap_ref[rw, :].astype(F32) + sgb_ref[rw, :].astype(F32) * r).astype(BF16) for rw, r in zip(rows, rs)]
    h1s = [h_ref[rw, :] + jnp.dot(y, wo_ref[...], preferred_element_type=F32) for rw, y in zip(rows, ys)]
    cur = topk_round(cur)
    u2s = []
    for rw, h1 in zip(rows, h1s):
        h1_ref[rw, :] = h1
        ms = jnp.mean(h1 * h1, axis=-1, keepdims=True)
        u2 = h1 * lax.rsqrt(ms + RMS_EPS) * gffn_ref[...]
        u2_ref[rw, :] = _pack_rows(u2)
        u2s.append(u2)
    cur = topk_round(cur)
    new_logits = []
    for u2 in u2s:
        u_hi, u_lo = _split_bf16(u2)
        new_logits.append(jnp.dot(u_hi, w_hi, preferred_element_type=F32)
                          + (jnp.dot(u_lo, w_hi, preferred_element_type=F32)
                             + jnp.dot(u_hi, w_lo, preferred_element_type=F32))
                          + br_ref[...])
    cur = topk_round(cur)
    assert len(vals) == TOP_K
    exps = [jnp.exp(vk - vals[0]) for vk in vals]
    den = exps[0] + exps[1] + exps[2] + exps[3]
    onehot = jnp.logical_or(jnp.logical_or(sels[0], sels[1]), jnp.logical_or(sels[2], sels[3]))
    onehot = jnp.logical_and(onehot, routed)
    onehot_bf = onehot.astype(BF16)
    rank_all = jnp.dot(tri_ref[...], onehot_bf, preferred_element_type=F32) + carry_ref[...]
    carry_ref[...] += jnp.sum(onehot.astype(F32), axis=0, keepdims=True)
    colcnt_ref[...] += _tn_dot(onehot_bf, jnp.ones((tm, LANES), BF16))
    info = jnp.zeros((tm, LANES), F32)
    for kk in range(TOP_K):
        rank_k = jnp.sum(jnp.where(sels[kk], rank_all, 0.0), axis=-1, keepdims=True)
        info = jnp.where(lane == kk, exps[kk] / den, info)
        info = jnp.where(lane == TOP_K + kk, idxs[kk].astype(F32), info)
        info = jnp.where(lane == 2 * TOP_K + kk, rank_k, info)
    info_ref[...] = info
    cnt_ref[...] = colcnt_ref[...]
    logits_ref[...] = jnp.concatenate(new_logits, axis=0)


def _merge_route(hp, on, ap, sgb, w_rnn_bf16, w_o_bf16, g_ffn, w_router_pad, b_router_pad):
    tp, d = hp.shape
    tm = ROW_TILE
    n_tiles = tp // tm
    this_tile = lambda i: (jnp.minimum(i, n_tiles - 1), 0)
    prev_tile = lambda i: (jnp.maximum(i - 1, 0), 0)
    row = pl.BlockSpec((tm, d), this_tile)
    vec = _const_spec((1, d))
    return pl.pallas_call(
        _merge_route_kernel,
        grid=(n_tiles + 1,),
        in_specs=[row, row, row, row, _const_spec((d, d)), _const_spec((d, d)), vec,
                  _const_spec((d, LANES)), _const_spec((1, LANES))],
        out_specs=[row, pl.BlockSpec((tm, PACK_WORDS), this_tile),
                   pl.BlockSpec((tm, LANES), prev_tile), _const_spec((LANES, LANES))],
        out_shape=[jax.ShapeDtypeStruct((tp, d), F32), jax.ShapeDtypeStruct((tp, PACK_WORDS), jnp.uint32),
                   jax.ShapeDtypeStruct((tp, LANES), F32), jax.ShapeDtypeStruct((LANES, LANES), F32)],
        scratch_shapes=[pltpu.VMEM((tm, tm), BF16), pltpu.VMEM((1, LANES), F32),
                        pltpu.VMEM((LANES, LANES), F32), pltpu.VMEM((tm, LANES), F32)],
        compiler_params=_cparams(("arbitrary",)),
        name="merge_route",
    )(hp, on, ap, sgb, w_rnn_bf16, w_o_bf16, g_ffn, w_router_pad, b_router_pad)


def _slots_kernel(info_ref, cnt_ref, dest_ref, te_ref, pstart_ref):
    tm = info_ref.shape[0]
    n_tile_lanes = te_ref.shape[1]

    @pl.when(pl.program_id(0) == 0)
    def _():
        r_i = lax.broadcasted_iota(jnp.int32, (LANES, LANES), 0)
        c_i = lax.broadcasted_iota(jnp.int32, (LANES, LANES), 1)
        cnt = cnt_ref[...]
        tiles = jnp.floor((cnt + (EXPERT_TILE - 1)) / EXPERT_TILE)
        pend_col = jnp.dot((c_i <= r_i).astype(F32), tiles, preferred_element_type=F32,
                           precision=lax.Precision.HIGHEST)
        pstart_ref[...] = _tn_dot_f32(tiles, (r_i < c_i).astype(F32))[0:SUBLANES, :]
        tile_id = lax.broadcasted_iota(jnp.int32, (LANES, n_tile_lanes), 1).astype(F32)
        exp_id = lax.broadcasted_iota(jnp.int32, (LANES, n_tile_lanes), 0)
        pend_wide = jnp.concatenate([pend_col] * (n_tile_lanes // LANES), axis=1)
        below = jnp.logical_and(pend_wide <= tile_id, exp_id < N_EXPERTS)
        te = jnp.minimum(jnp.sum(below.astype(F32), axis=0, keepdims=True), N_EXPERTS - 1.0)
        n_active = pend_col[N_EXPERTS - 1:N_EXPERTS, 0:1]
        sub_t = lax.broadcasted_iota(jnp.int32, (SUBLANES, n_tile_lanes), 0)
        out = jnp.where(sub_t == 0, jnp.broadcast_to(te, (SUBLANES, n_tile_lanes)),
                        jnp.broadcast_to(n_active, (SUBLANES, n_tile_lanes)))
        te_ref[...] = out.astype(jnp.int32)

    info = info_ref[...]
    pstart_row = pstart_ref[0:1, :]
    lane = lax.broadcasted_iota(jnp.int32, (tm, LANES), 1)
    dest = jnp.zeros((tm, LANES), F32)
    for kk in range(TOP_K):
        idx_k = info[:, TOP_K + kk:TOP_K + kk + 1].astype(jnp.int32)
        start_k = jnp.sum(jnp.where(lane == idx_k, pstart_row, 0.0), axis=-1, keepdims=True)
        slot_k = start_k * EXPERT_TILE + info[:, 2 * TOP_K + kk:2 * TOP_K + kk + 1]
        dest = jnp.where(lane == kk, slot_k, dest)
    dest_ref[...] = dest.astype(jnp.int32)


def _tn_dot_f32(a, b):
    return lax.dot_general(a, b, (((0,), (0,)), ((), ())), preferred_element_type=F32,
                           precision=lax.Precision.HIGHEST)


def _slots(info, cnt, n_tiles_max):
    tp = info.shape[0]
    tm = tp // SLOTS_STEPS
    n_tile_lanes = -(-n_tiles_max // LANES) * LANES
    return pl.pallas_call(
        _slots_kernel,
        grid=(tp // tm,),
        in_specs=[pl.BlockSpec((tm, LANES), lambda i: (i, 0)), _const_spec((LANES, LANES))],
        out_specs=[pl.BlockSpec((tm, LANES), lambda i: (i, 0)), _const_spec((8, n_tile_lanes))],
        out_shape=[jax.ShapeDtypeStruct((tp, LANES), jnp.int32),
                   jax.ShapeDtypeStruct((8, n_tile_lanes), jnp.int32)],
        scratch_shapes=[pltpu.VMEM((SUBLANES, LANES), F32)],
        compiler_params=_cparams(("arbitrary",)),
        name="slots",
    )(info, cnt)


def _sc_workers():
    sc = plsc.get_sparse_core_info()
    return sc.num_cores, sc.num_cores * sc.num_subcores


def _sc_token_chunks(tp, chunk, body):
    n_cores, n_workers = _sc_workers()
    per_worker = tp // n_workers
    assert per_worker * n_workers == tp and per_worker % chunk == 0 and chunk % SUBLANES == 0
    base = (lax.axis_index("s") * n_cores + lax.axis_index("c")) * per_worker

    @pl.loop(0, per_worker // chunk)
    def _(ci):
        body(pl.multiple_of(base + ci * chunk, SUBLANES))


def _sc_kernel(out_type, chunk, n_row_bufs):
    return functools.partial(
        pl.kernel, mesh=plsc.VectorSubcoreMesh(core_axis_name="c", subcore_axis_name="s"), out_type=out_type,
        scratch_types=[pltpu.VMEM((TOP_K, chunk), jnp.int32), pltpu.VMEM((n_row_bufs, chunk, PACK_WORDS), jnp.uint32),
                       pltpu.SemaphoreType.DMA, pltpu.SemaphoreType.DMA])


def _wait_all(copies):
    for c in copies:
        c.wait()


def _dispatch(dest_flat, u2p, n_slots):
    tp = u2p.shape[0]
    ch = SC_SCATTER_CHUNK

    @_sc_kernel(jax.ShapeDtypeStruct((n_slots, PACK_WORDS), jnp.uint32), ch, 1)
    def scatter_rows(u2_hbm, dest_hbm, xs_hbm, idx_v, rows_v, sem_a, sem_b):
        def chunk(off):
            loads = [pltpu.async_copy(u2_hbm.at[pl.ds(off, ch)], rows_v.at[0], sem_a)]
            loads += [pltpu.async_copy(dest_hbm.at[pl.ds(kk * tp + off, ch)], idx_v.at[kk], sem_a)
                      for kk in range(TOP_K)]
            _wait_all(loads)
            _wait_all([pltpu.async_copy(rows_v.at[0], xs_hbm.at[idx_v.at[kk]], sem_b) for kk in range(TOP_K)])

        _sc_token_chunks(tp, ch, chunk)

    return scatter_rows(u2p, dest_flat)


def _gather_expert_rows(dest_flat, ys):
    tp = dest_flat.shape[0] // TOP_K
    ch = SC_GATHER_CHUNK

    @_sc_kernel(jax.ShapeDtypeStruct((TOP_K, tp, PACK_WORDS), jnp.uint32), ch, TOP_K)
    def gather_rows(ys_hbm, dest_hbm, out_hbm, idx_v, rows_v, sem_a, sem_b):
        def chunk(off):
            _wait_all([pltpu.async_copy(dest_hbm.at[pl.ds(kk * tp + off, ch)], idx_v.at[kk], sem_a)
                       for kk in range(TOP_K)])
            _wait_all([pltpu.async_copy(ys_hbm.at[idx_v.at[kk]], rows_v.at[kk], sem_b) for kk in range(TOP_K)])
            _wait_all([pltpu.async_copy(rows_v.at[kk], out_hbm.at[kk, pl.ds(off, ch)], sem_a)
                       for kk in range(TOP_K)])

        _sc_token_chunks(tp, ch, chunk)

    return gather_rows(ys, dest_flat)


def _expert_kernel(te_ref, nact_ref, x_ref, wgu_hbm, bgu_ref, wdn_hbm, bdn_ref, y_ref,
                   wgu_buf, wdn_buf, slot_ref, sem):
    i = pl.program_id(0)
    n_active = nact_ref[0]
    last_tile = pl.num_programs(0) - 1

    def fetch(expert, slot):
        return (pltpu.make_async_copy(wgu_hbm.at[expert], wgu_buf.at[slot], sem.at[slot, 0]),
                pltpu.make_async_copy(wdn_hbm.at[expert], wdn_buf.at[slot], sem.at[slot, 1]))

    @pl.when(i == 0)
    def _():
        slot_ref[0] = 1
        for copy in fetch(te_ref[0], 0):
            copy.start()

    @pl.when(i < n_active)
    def _():
        expert = te_ref[i]
        first_tile_of_expert = jnp.logical_or(i == 0, expert != te_ref[jnp.maximum(i - 1, 0)])

        @pl.when(first_tile_of_expert)
        def _():
            slot = 1 - slot_ref[0]
            slot_ref[0] = slot
            for copy in fetch(expert, slot):
                copy.wait()
            nxt = lax.while_loop(
                lambda j: jnp.logical_and(j < n_active, te_ref[jnp.minimum(j, last_tile)] == expert),
                lambda j: j + 1, i + 1)

            @pl.when(nxt < n_active)
            def _():
                for copy in fetch(te_ref[jnp.minimum(nxt, last_tile)], 1 - slot):
                    copy.start()

        slot = slot_ref[0]
        x = _unpack_rows(x_ref[...]).astype(BF16)
        gu = jnp.dot(x, wgu_buf[slot].astype(BF16), preferred_element_type=F32) + bgu_ref[0]
        gate = jnp.minimum(gu[:, :D_FF], SWIGLU_LIMIT)
        up = jnp.clip(gu[:, D_FF:], -SWIGLU_LIMIT, SWIGLU_LIMIT)
        hdn = (up + 1.0) * (gate * _sigmoid(SWIGLU_ALPHA * gate))
        y = jnp.dot(hdn.astype(BF16), wdn_buf[slot].astype(BF16), preferred_element_type=F32) + bdn_ref[0]
        y_ref[...] = _pack_rows(y)

    @pl.when(i >= n_active)
    def _():
        y_ref[...] = jnp.zeros_like(y_ref)


def _expert_mlp(te, nact, xs, w_gu, b_gu, w_dn, b_dn):
    d = D_MODEL
    n_tiles = xs.shape[0] // EXPERT_TILE

    def tile_map(i, te_ref, nact_ref):
        return (jnp.minimum(i, nact_ref[0] - 1), 0)

    def exp_map(i, te_ref, nact_ref):
        return (te_ref[jnp.minimum(i, nact_ref[0] - 1)], 0, 0)

    grid_spec = pltpu.PrefetchScalarGridSpec(
        num_scalar_prefetch=2,
        grid=(n_tiles,),
        in_specs=[pl.BlockSpec((EXPERT_TILE, PACK_WORDS), tile_map),
                  pl.BlockSpec(memory_space=pl.ANY),
                  pl.BlockSpec((1, 1, 2 * D_FF), exp_map),
                  pl.BlockSpec(memory_space=pl.ANY),
                  pl.BlockSpec((1, 1, d), exp_map)],
        out_specs=pl.BlockSpec((EXPERT_TILE, PACK_WORDS), lambda i, te_ref, nact_ref: (i, 0)),
        scratch_shapes=[pltpu.VMEM((2, d, 2 * D_FF), F32), pltpu.VMEM((2, D_FF, d), F32),
                        pltpu.SMEM((1,), jnp.int32), pltpu.SemaphoreType.DMA((2, 2))],
    )
    return pl.pallas_call(
        _expert_kernel,
        grid_spec=grid_spec,
        out_shape=jax.ShapeDtypeStruct(xs.shape, jnp.uint32),
        compiler_params=_cparams(("arbitrary",)),
        name="expert_mlp",
    )(te, nact, xs, w_gu, b_gu, w_dn, b_dn)


def _combine_kernel(gfin_ref, *refs):
    out_ref = refs[-1]
    for s in range(COMBINE_SUB):
        h1_ref, info_ref, yg_ref = refs[3 * s:3 * s + 3]
        info = info_ref[...]
        h2 = h1_ref[...]
        for kk in range(TOP_K):
            h2 = h2 + info[:, kk:kk + 1] * _unpack_rows(yg_ref[kk])
        ms = jnp.mean(h2 * h2, axis=-1, keepdims=True)
        out_ref[s * CHUNK:(s + 1) * CHUNK, :] = h2 * lax.rsqrt(ms + RMS_EPS) * gfin_ref[...]


def _combine(h1, info, g_final, yg, n_batch, seq):
    d = h1.shape[1]
    chunks = seq // CHUNK

    def padded(s):
        def index(j):
            c = j * COMBINE_SUB + s
            return (c // chunks) * (chunks + 1) + c % chunks + 1
        return index

    in_specs = [pl.BlockSpec((1, d), lambda j: (0, 0))]
    operands = [g_final]
    for s in range(COMBINE_SUB):
        chunk_of = padded(s)
        in_specs += [pl.BlockSpec((CHUNK, d), lambda j, f=chunk_of: (f(j), 0)),
                     pl.BlockSpec((CHUNK, LANES), lambda j, f=chunk_of: (f(j), 0)),
                     pl.BlockSpec((TOP_K, CHUNK, PACK_WORDS), lambda j, f=chunk_of: (0, f(j), 0))]
        operands += [h1, info, yg]
    return pl.pallas_call(
        _combine_kernel,
        grid=(n_batch * chunks // COMBINE_SUB,),
        in_specs=in_specs,
        out_specs=pl.BlockSpec((COMBINE_SUB * CHUNK, d), lambda j: (j, 0)),
        out_shape=jax.ShapeDtypeStruct((n_batch * seq, d), F32),
        compiler_params=_cparams(("parallel",)),
        name="combine",
    )(*operands)


def kernel(x, meta_tokens, lb_logits, g_mix, w_in, w_dw, b_dw, ln_g, ln_b, w_conv_out, b_conv_out,
           g_onorm, w_rnn_out, w_o, g_ffn, w_router, b_router, w_gate_up, b_gate_up, w_down, b_down,
           g_final):
    n_batch, seq, d = x.shape
    assert d == D_MODEL and w_in.shape[0] == 1, "single-layer block with D_MODEL features"
    assert seq % (CHUNK * COMBINE_SUB) == 0
    seq_pad = CHUNK + seq
    tp = n_batch * seq_pad
    assert tp % ROW_TILE == 0

    meta = jnp.broadcast_to(meta_tokens.astype(x.dtype)[None], (n_batch, N_META, d))
    hp = jnp.concatenate([jnp.zeros((n_batch, CHUNK_PAD, d), x.dtype), meta, x], axis=1).reshape(tp, d)
    vec = lambda a: a.reshape(1, -1).astype(F32)

    aglu, q, k, logf, v, og, sga, sgb = _in_proj(hp, vec(g_mix[0]), lb_logits.astype(F32), w_in[0].astype(BF16))
    ap = _conv_branch(aglu, w_dw[0].astype(F32), vec(b_dw[0]), vec(ln_g[0]), vec(ln_b[0]),
                      w_conv_out[0].astype(BF16), vec(b_conv_out[0]), sga)
    on = _hgrn2_scan(q, k, v, logf, og, vec(g_onorm[0]), n_batch)
    w_router_pad = jnp.pad(w_router[0].astype(F32), ((0, 0), (0, LANES - N_EXPERTS)))
    b_router_pad = jnp.pad(vec(b_router[0]), ((0, 0), (0, LANES - N_EXPERTS)))
    h1, u2p, info, cnt = _merge_route(hp, on, ap, sgb, w_rnn_out[0].astype(BF16), w_o[0].astype(BF16),
                                      vec(g_ffn[0]), w_router_pad, b_router_pad)

    n_tiles_max = -(-(tp * TOP_K + N_EXPERTS * (EXPERT_TILE - 1)) // EXPERT_TILE)
    dest, te = _slots(info, cnt, n_tiles_max)
    dest_flat = dest[:, :TOP_K].T.reshape(-1)
    xs = _dispatch(dest_flat, u2p, n_tiles_max * EXPERT_TILE)
    ys = _expert_mlp(te[0, :n_tiles_max], te[1, :1], xs, w_gate_up[0],
                     b_gate_up[0].reshape(N_EXPERTS, 1, -1).astype(F32), w_down[0],
                     b_down[0].reshape(N_EXPERTS, 1, -1).astype(F32))
    yg = _gather_expert_rows(dest_flat, ys)
    out = _combine(h1, info, vec(g_final), yg, n_batch, seq)
    return out.reshape(n_batch, seq, d)
```

```python
import functools

import jax
import jax.numpy as jnp
from jax import lax
from jax.experimental import pallas as pl
from jax.experimental.pallas import tpu as pltpu
from jax.experimental.pallas import tpu_sc as plsc

F32 = jnp.float32
BF16 = jnp.bfloat16

D_MODEL = 1024
N_META = 16
CHUNK = 128
CHUNK_PAD = CHUNK - N_META
CONV_WIDTH = 31
HEAD_DIM = 128
N_HEADS = D_MODEL // HEAD_DIM
N_EXPERTS = 32
TOP_K = 4
D_FF = D_MODEL
SWIGLU_LIMIT = 7.0
SWIGLU_ALPHA = 1.702
RMS_EPS = 1e-6
LN_EPS = 1e-5

LANES = 128
SUBLANES = 8
SUB_BLOCK = 32
N_SUB = CHUNK // SUB_BLOCK
HALO = 32
ROW_TILE = 640
IN_TILE = 320
SCAN_CHUNKS = 5
MERGE_PARTS = 2
SLOTS_STEPS = 5
CONV_ROWS = 64
EXPERT_TILE = 512
SC_SCATTER_CHUNK = 104
SC_GATHER_CHUNK = 40
COMBINE_SUB = 4
NEG_BIG = -1e30
VMEM_LIMIT = 56 * 1024 * 1024


def _sigmoid(x):
    return 1.0 / (1.0 + jnp.exp(-x))


def _cparams(sem):
    return pltpu.CompilerParams(dimension_semantics=sem, vmem_limit_bytes=VMEM_LIMIT)


def _const_spec(shape):
    nd = len(shape)
    return pl.BlockSpec(shape, lambda *_: (0,) * nd)


PACK_WORDS = D_MODEL // 2
HIGH_HALF = 0xFFFF0000


def _pack_rows(x):
    lo = lax.bitcast_convert_type(x[:, :PACK_WORDS].astype(BF16).astype(F32), jnp.uint32)
    hi = lax.bitcast_convert_type(x[:, PACK_WORDS:].astype(BF16).astype(F32), jnp.uint32)
    return (lo >> 16) | (hi & jnp.uint32(HIGH_HALF))


def _unpack_rows(w):
    lo = lax.bitcast_convert_type(w << 16, F32)
    hi = lax.bitcast_convert_type(w & jnp.uint32(HIGH_HALF), F32)
    return jnp.concatenate([lo, hi], axis=1)


def _residual_tile(x_ref, meta_ref, is_first):
    x = x_ref[...]
    tm, d = x.shape
    prefix = jnp.concatenate([jnp.zeros((CHUNK_PAD, d), x.dtype), meta_ref[...]], axis=0)
    first = jnp.concatenate([prefix, x[:tm - CHUNK, :]], axis=0)
    return jnp.where(is_first, first, x)


def _residual_spec(tm, d, seq, tiles_per_batch, n_tiles):
    def start(i):
        i = jnp.minimum(i, n_tiles - 1)
        b, t = i // tiles_per_batch, i % tiles_per_batch
        return (pl.multiple_of(b * seq + jnp.maximum(t * tm - CHUNK, 0), SUBLANES), 0)
    return pl.BlockSpec((pl.Element(tm), pl.Element(d)), start)


def _in_proj_kernel(x_ref, meta_ref, g_ref, lbl_ref, w_ref, aglu_ref, q_ref, k_ref, logf_ref, v_ref,
                    og_ref, sga_ref, sgb_ref, *, tiles_per_batch):
    d = D_MODEL
    h = _residual_tile(x_ref, meta_ref, pl.program_id(0) % tiles_per_batch == 0)
    ms = jnp.mean(h * h, axis=-1, keepdims=True)
    u = (h * lax.rsqrt(ms + RMS_EPS) * g_ref[...]).astype(BF16)

    def proj(j):
        return jnp.dot(u, w_ref[:, j * d:(j + 1) * d], preferred_element_type=F32)

    aglu_ref[...] = (proj(0) * _sigmoid(proj(1))).astype(aglu_ref.dtype)
    zq = proj(2)
    q_ref[...] = (zq * _sigmoid(zq)).astype(q_ref.dtype)
    lbl = lbl_ref[...]
    e = jnp.exp(lbl - jnp.max(lbl, axis=0, keepdims=True))
    lb = e[0:1, :] / jnp.sum(e, axis=0, keepdims=True)
    s = _sigmoid(proj(3))
    logf_ref[...] = jnp.log(lb + (1.0 - lb) * s)
    k_ref[...] = ((1.0 - lb) * (1.0 - s)).astype(k_ref.dtype)
    v_ref[...] = proj(4).astype(v_ref.dtype)
    zg = proj(5)
    og_ref[...] = (zg * _sigmoid(zg)).astype(og_ref.dtype)
    sga_ref[...] = _sigmoid(proj(6)).astype(sga_ref.dtype)
    sgb_ref[...] = _sigmoid(proj(7)).astype(sgb_ref.dtype)


def _in_proj(x2, meta, seq, g_mix, lb_logits, w_in_bf16):
    d = x2.shape[1]
    n_batch = x2.shape[0] // seq
    tm = IN_TILE
    tpb = (seq + CHUNK) // tm
    tp = n_batch * (seq + CHUNK)
    assert tpb * tm == seq + CHUNK and tm > CHUNK
    row = pl.BlockSpec((tm, d), lambda i: (i, 0))
    out_dtypes = [BF16, BF16, BF16, F32, BF16, BF16, BF16, BF16]
    return pl.pallas_call(
        functools.partial(_in_proj_kernel, tiles_per_batch=tpb),
        grid=(tp // tm,),
        in_specs=[_residual_spec(tm, d, seq, tpb, tp // tm), _const_spec(meta.shape), _const_spec((1, d)),
                  _const_spec(lb_logits.shape),
                  pl.BlockSpec(w_in_bf16.shape, lambda i: (0, 0), pipeline_mode=pl.Buffered(1))],
        out_specs=[row] * 8,
        out_shape=[jax.ShapeDtypeStruct((tp, d), dt) for dt in out_dtypes],
        compiler_params=_cparams(("parallel",)),
        name="in_proj",
    )(x2, meta, g_mix, lb_logits, w_in_bf16)


def _conv_kernel(halo_ref, cur_ref, wdw_ref, bdw_ref, lng_ref, lnb_ref, wout_ref, bout_ref, sga_ref,
                 out_ref, win_ref, acc_ref, shift_ref):
    tm = cur_ref.shape[0]
    win_ref[0:HALO, :] = halo_ref[...].astype(F32)
    win_ref[HALO:, :] = cur_ref[...].astype(F32)
    first_tap = HALO - (CONV_WIDTH - 1)
    for c in range(D_MODEL // LANES):
        lanes = slice(c * LANES, (c + 1) * LANES)
        for s in range(SUBLANES):
            n_rows = tm + HALO - (SUBLANES if s else 0)
            shift_ref[s, 0:n_rows, :] = win_ref[pl.ds(s, n_rows), lanes]
        w_c = wdw_ref[:, lanes]
        bias = jnp.broadcast_to(bdw_ref[:, lanes], (CONV_ROWS, LANES))

        def chunk(r, carry, lanes=lanes, w_c=w_c, bias=bias):
            r0 = pl.multiple_of(r * CONV_ROWS, CONV_ROWS)
            acc = bias
            for j in range(CONV_WIDTH):
                off = first_tap + j
                rows = pl.ds(r0 + off - off % SUBLANES, CONV_ROWS)
                acc = acc + w_c[j:j + 1, :] * shift_ref[off % SUBLANES, rows, :]
            acc_ref[pl.ds(r0, CONV_ROWS), lanes] = acc
            return carry

        lax.fori_loop(0, tm // CONV_ROWS, chunk, 0)
    a = acc_ref[...]
    mu = jnp.mean(a, axis=-1, keepdims=True)
    ac = a - mu
    var = jnp.mean(ac * ac, axis=-1, keepdims=True)
    y = ac * lax.rsqrt(var + LN_EPS) * lng_ref[...] + lnb_ref[...]
    y = y * _sigmoid(y)
    o = jnp.dot(y.astype(BF16), wout_ref[...], preferred_element_type=F32) + bout_ref[...]
    out_ref[...] = (sga_ref[...].astype(F32) * o).astype(out_ref.dtype)


def _conv_branch(aglu, w_dw, b_dw, ln_g, ln_b, w_out_bf16, b_out, sga):
    tp, d = aglu.shape
    tm = ROW_TILE
    per = tm // HALO
    row = pl.BlockSpec((tm, d), lambda i: (i, 0))
    halo = pl.BlockSpec((HALO, d), lambda i: (jnp.maximum(i * per - 1, 0), 0))
    vec = _const_spec((1, d))
    return pl.pallas_call(
        _conv_kernel,
        grid=(tp // tm,),
        in_specs=[halo, row, _const_spec(w_dw.shape), vec, vec, vec, _const_spec((d, d)), vec, row],
        out_specs=row,
        out_shape=jax.ShapeDtypeStruct((tp, d), BF16),
        scratch_shapes=[pltpu.VMEM((tm + HALO, d), F32), pltpu.VMEM((tm, d), F32),
                        pltpu.VMEM((SUBLANES, tm + HALO, LANES), F32)],
        compiler_params=_cparams(("parallel",)),
        name="conv_branch",
    )(aglu, aglu, w_dw, b_dw, ln_g, ln_b, w_out_bf16, b_out, sga)


def _split_bf16(x):
    hi = x.astype(BF16)
    return hi, (x - hi.astype(F32)).astype(BF16)


def _nt_dot(a, b):
    return lax.dot_general(a, b, (((1,), (1,)), ((), ())), preferred_element_type=F32)


def _tn_dot(a, b):
    return lax.dot_general(a, b, (((0,), (0,)), ((), ())), preferred_element_type=F32)


def _scan_kernel(q_ref, k_ref, v_ref, lf_ref, og_ref, gon_ref, o_ref, st_ref):
    @pl.when(pl.program_id(1) == 0)
    def _():
        st_ref[...] = jnp.zeros_like(st_ref)

    c = CHUNK
    row = lax.broadcasted_iota(jnp.int32, (c, c), 0)
    col = lax.broadcasted_iota(jnp.int32, (c, c), 1)
    causal = col <= row
    diag_mask = jnp.logical_and(causal, row // SUB_BLOCK == col // SUB_BLOCK)
    tri = causal.astype(BF16)

    def bcast_rows(rows):
        return jnp.concatenate([jnp.broadcast_to(r, (SUB_BLOCK, HEAD_DIM)) for r in rows], axis=0)

    heads = [slice(h * HEAD_DIM, (h + 1) * HEAD_DIM) for h in range(N_HEADS)]
    chunks = [slice(ci * c, (ci + 1) * c) for ci in range(SCAN_CHUNKS)]

    bcums = []
    for cs in chunks:
        lf = lf_ref[cs, :]
        lf_hi = lf.astype(BF16)
        lf_mid, lf_lo = _split_bf16(lf - lf_hi.astype(F32))
        bcums.append(jnp.dot(tri, lf_hi, preferred_element_type=F32)
                     + (jnp.dot(tri, lf_mid, preferred_element_type=F32)
                        + jnp.dot(tri, lf_lo, preferred_element_type=F32)))
    operands = {}
    for ci, cs in enumerate(chunks):
        for h, hs in enumerate(heads):
            b = bcums[ci][:, hs]
            q = q_ref[cs, hs].astype(F32)
            k = k_ref[cs, hs].astype(F32)
            ends = [b[i * SUB_BLOCK + SUB_BLOCK - 1:i * SUB_BLOCK + SUB_BLOCK, :] for i in range(N_SUB)]
            mids = [b[i * SUB_BLOCK + SUB_BLOCK // 2 - 1:i * SUB_BLOCK + SUB_BLOCK // 2, :] for i in range(N_SUB)]
            mid_full = bcast_rows(mids)
            b_last = ends[-1]
            zero_row = jnp.zeros_like(b_last)
            qm = q * jnp.exp(b - mid_full)
            km = k * jnp.exp(mid_full - b)
            q_parts, k_parts = [], []
            for j in range(N_SUB - 1):
                q_rows = [jnp.exp(mids[i] - ends[j]) if i > j else zero_row for i in range(N_SUB)]
                k_rows = [jnp.exp(ends[j] - mids[j]) if i == j else zero_row for i in range(N_SUB)]
                q_parts.append((qm * bcast_rows(q_rows)).astype(BF16))
                k_parts.append((km * bcast_rows(k_rows)).astype(BF16))
            q_in = (qm * bcast_rows([jnp.exp(m) for m in mids])).astype(BF16)
            k_out = (km * bcast_rows([jnp.exp(b_last - m) for m in mids])).astype(BF16)
            operands[ci, h] = (qm.astype(BF16), km.astype(BF16), jnp.concatenate(q_parts, axis=1),
                               jnp.concatenate(k_parts, axis=1), q_in, k_out, jnp.exp(b_last))
    products = {}
    for ci, cs in enumerate(chunks):
        for h, hs in enumerate(heads):
            qm, km, q_cat, k_cat, _, k_out, _ = operands[ci, h]
            products[ci, h] = (_nt_dot(qm, km), _nt_dot(q_cat, k_cat), _tn_dot(v_ref[cs, hs], k_out))
    within = {}
    for ci, cs in enumerate(chunks):
        for h, hs in enumerate(heads):
            same_block, earlier_blocks, _ = products[ci, h]
            scores = jnp.where(diag_mask, same_block, 0.0) + earlier_blocks
            within[ci, h] = jnp.dot(scores.astype(BF16), v_ref[cs, hs], preferred_element_type=F32)
    states = [st_ref[h] for h in range(N_HEADS)]
    outs = {}
    for ci in range(SCAN_CHUNKS):
        for h in range(N_HEADS):
            outs[ci, h] = within[ci, h] + _nt_dot(operands[ci, h][4], states[h].astype(BF16))
            states[h] = states[h] * operands[ci, h][6] + products[ci, h][2]
    for h in range(N_HEADS):
        st_ref[h] = states[h]
    for ci, cs in enumerate(chunks):
        for h, hs in enumerate(heads):
            o = outs[ci, h]
            ms = jnp.mean(o * o, axis=-1, keepdims=True)
            on = o * lax.rsqrt(ms + RMS_EPS) * gon_ref[:, hs]
            o_ref[cs, hs] = (on * og_ref[cs, hs].astype(F32)).astype(o_ref.dtype)


def _hgrn2_scan(q, k, v, logf, og, g_onorm, n_batch):
    tp, d = q.shape
    rows = SCAN_CHUNKS * CHUNK
    n_chunks = tp // n_batch // rows
    assert n_chunks * rows * n_batch == tp
    blk = pl.BlockSpec((rows, d), lambda b, c: (b * n_chunks + c, 0))
    return pl.pallas_call(
        _scan_kernel,
        grid=(n_batch, n_chunks),
        in_specs=[blk, blk, blk, blk, blk, pl.BlockSpec((1, d), lambda b, c: (0, 0))],
        out_specs=blk,
        out_shape=jax.ShapeDtypeStruct((tp, d), BF16),
        scratch_shapes=[pltpu.VMEM((N_HEADS, HEAD_DIM, HEAD_DIM), F32)],
        compiler_params=_cparams(("arbitrary", "arbitrary")),
        name="hgrn2_scan",
    )(q, k, v, logf, og, g_onorm)


def _merge_route_kernel(x_ref, meta_ref, on_ref, ap_ref, sgb_ref, wrnn_ref, wo_ref, gffn_ref, wr_ref, br_ref,
                        h1_ref, u2_ref, info_ref, cnt_ref, tri_ref, carry_ref, colcnt_ref, logits_ref, hres_ref,
                        *, tiles_per_batch, n_tiles):
    tm = on_ref.shape[0]
    i = pl.program_id(0)
    hres_ref[...] = _residual_tile(x_ref, meta_ref, jnp.minimum(i, n_tiles - 1) % tiles_per_batch == 0)

    @pl.when(i == 0)
    def _():
        r_i = lax.broadcasted_iota(jnp.int32, (tm, tm), 0)
        c_i = lax.broadcasted_iota(jnp.int32, (tm, tm), 1)
        tri_ref[...] = (c_i < r_i).astype(BF16)
        carry_ref[...] = jnp.zeros_like(carry_ref)
        colcnt_ref[...] = jnp.zeros_like(colcnt_ref)
        logits_ref[...] = jnp.zeros_like(logits_ref)

    routed = i > 0
    lane = lax.broadcasted_iota(jnp.int32, (tm, LANES), 1)
    cur = jnp.where(lane < N_EXPERTS, logits_ref[...], NEG_BIG)
    vals, idxs, sels = [], [], []

    def topk_round(cur):
        m = jnp.max(cur, axis=-1, keepdims=True)
        idx = jnp.min(jnp.where(cur == m, lane, LANES), axis=-1, keepdims=True)
        sel = lane == idx
        vals.append(m)
        idxs.append(idx)
        sels.append(sel)
        return jnp.where(sel, 2.0 * NEG_BIG, cur)

    w_hi, w_lo = _split_bf16(wr_ref[...])
    part = tm // MERGE_PARTS
    rows = [slice(p * part, (p + 1) * part) for p in range(MERGE_PARTS)]
    rs = [jnp.dot(on_ref[rw, :], wrnn_ref[...], preferred_element_type=F32) for rw in rows]
    cur = topk_round(cur)
    ys = [(ap_ref[rw, :].astype(F32) + sgb_ref[rw, :].astype(F32) * r).astype(BF16) for rw, r in zip(rows, rs)]
    h1s = [hres_ref[rw, :] + jnp.dot(y, wo_ref[...], preferred_element_type=F32) for rw, y in zip(rows, ys)]
    cur = topk_round(cur)
    u2s = []
    for rw, h1 in zip(rows, h1s):
        h1_ref[rw, :] = h1
        ms = jnp.mean(h1 * h1, axis=-1, keepdims=True)
        u2 = h1 * lax.rsqrt(ms + RMS_EPS) * gffn_ref[...]
        u2_ref[rw, :] = _pack_rows(u2)
        u2s.append(u2)
    cur = topk_round(cur)
    new_logits = []
    for u2 in u2s:
        u_hi, u_lo = _split_bf16(u2)
        new_logits.append(jnp.dot(u_hi, w_hi, preferred_element_type=F32)
                          + (jnp.dot(u_lo, w_hi, preferred_element_type=F32)
                             + jnp.dot(u_hi, w_lo, preferred_element_type=F32))
                          + br_ref[...])
    cur = topk_round(cur)
    assert len(vals) == TOP_K
    exps = [jnp.exp(vk - vals[0]) for vk in vals]
    den = exps[0] + exps[1] + exps[2] + exps[3]
    onehot = jnp.logical_or(jnp.logical_or(sels[0], sels[1]), jnp.logical_or(sels[2], sels[3]))
    onehot = jnp.logical_and(onehot, routed)
    onehot_bf = onehot.astype(BF16)
    rank_all = jnp.dot(tri_ref[...], onehot_bf, preferred_element_type=F32) + carry_ref[...]
    carry_ref[...] += jnp.sum(onehot.astype(F32), axis=0, keepdims=True)
    colcnt_ref[...] += _tn_dot(onehot_bf, jnp.ones((tm, LANES), BF16))
    info = jnp.zeros((tm, LANES), F32)
    for kk in range(TOP_K):
        rank_k = jnp.sum(jnp.where(sels[kk], rank_all, 0.0), axis=-1, keepdims=True)
        info = jnp.where(lane == kk, exps[kk] / den, info)
        info = jnp.where(lane == TOP_K + kk, idxs[kk].astype(F32), info)
        info = jnp.where(lane == 2 * TOP_K + kk, rank_k, info)
    info_ref[...] = info
    cnt_ref[...] = colcnt_ref[...]
    logits_ref[...] = jnp.concatenate(new_logits, axis=0)


def _merge_route(x2, meta, seq, on, ap, sgb, w_rnn_bf16, w_o_bf16, g_ffn, w_router_pad, b_router_pad):
    tp, d = on.shape
    tm = ROW_TILE
    n_tiles = tp // tm
    tpb = (seq + CHUNK) // tm
    assert tpb * tm == seq + CHUNK
    this_tile = lambda i: (jnp.minimum(i, n_tiles - 1), 0)
    prev_tile = lambda i: (jnp.maximum(i - 1, 0), 0)
    row = pl.BlockSpec((tm, d), this_tile)
    vec = _const_spec((1, d))
    return pl.pallas_call(
        functools.partial(_merge_route_kernel, tiles_per_batch=tpb, n_tiles=n_tiles),
        grid=(n_tiles + 1,),
        in_specs=[_residual_spec(tm, d, seq, tpb, n_tiles), _const_spec(meta.shape), row, row, row,
                  _const_spec((d, d)), _const_spec((d, d)), vec, _const_spec((d, LANES)), _const_spec((1, LANES))],
        out_specs=[row, pl.BlockSpec((tm, PACK_WORDS), this_tile),
                   pl.BlockSpec((tm, LANES), prev_tile), _const_spec((LANES, LANES))],
        out_shape=[jax.ShapeDtypeStruct((tp, d), F32), jax.ShapeDtypeStruct((tp, PACK_WORDS), jnp.uint32),
                   jax.ShapeDtypeStruct((tp, LANES), F32), jax.ShapeDtypeStruct((LANES, LANES), F32)],
        scratch_shapes=[pltpu.VMEM((tm, tm), BF16), pltpu.VMEM((1, LANES), F32),
                        pltpu.VMEM((LANES, LANES), F32), pltpu.VMEM((tm, LANES), F32), pltpu.VMEM((tm, d), F32)],
        compiler_params=_cparams(("arbitrary",)),
        name="merge_route",
    )(x2, meta, on, ap, sgb, w_rnn_bf16, w_o_bf16, g_ffn, w_router_pad, b_router_pad)


def _slots_kernel(info_ref, cnt_ref, dest_ref, te_ref, pstart_ref):
    tm = info_ref.shape[0]
    n_tile_lanes = te_ref.shape[1]

    @pl.when(pl.program_id(0) == 0)
    def _():
        r_i = lax.broadcasted_iota(jnp.int32, (LANES, LANES), 0)
        c_i = lax.broadcasted_iota(jnp.int32, (LANES, LANES), 1)
        cnt = cnt_ref[...]
        tiles = jnp.floor((cnt + (EXPERT_TILE - 1)) / EXPERT_TILE)
        pend_col = jnp.dot((c_i <= r_i).astype(F32), tiles, preferred_element_type=F32,
                           precision=lax.Precision.HIGHEST)
        pstart_ref[...] = _tn_dot_f32(tiles, (r_i < c_i).astype(F32))[0:SUBLANES, :]
        tile_id = lax.broadcasted_iota(jnp.int32, (LANES, n_tile_lanes), 1).astype(F32)
        exp_id = lax.broadcasted_iota(jnp.int32, (LANES, n_tile_lanes), 0)
        pend_wide = jnp.concatenate([pend_col] * (n_tile_lanes // LANES), axis=1)
        below = jnp.logical_and(pend_wide <= tile_id, exp_id < N_EXPERTS)
        te = jnp.minimum(jnp.sum(below.astype(F32), axis=0, keepdims=True), N_EXPERTS - 1.0)
        n_active = pend_col[N_EXPERTS - 1:N_EXPERTS, 0:1]
        sub_t = lax.broadcasted_iota(jnp.int32, (SUBLANES, n_tile_lanes), 0)
        out = jnp.where(sub_t == 0, jnp.broadcast_to(te, (SUBLANES, n_tile_lanes)),
                        jnp.broadcast_to(n_active, (SUBLANES, n_tile_lanes)))
        te_ref[...] = out.astype(jnp.int32)

    info = info_ref[...]
    pstart_row = pstart_ref[0:1, :]
    lane = lax.broadcasted_iota(jnp.int32, (tm, LANES), 1)
    dest = jnp.zeros((tm, LANES), F32)
    for kk in range(TOP_K):
        idx_k = info[:, TOP_K + kk:TOP_K + kk + 1].astype(jnp.int32)
        start_k = jnp.sum(jnp.where(lane == idx_k, pstart_row, 0.0), axis=-1, keepdims=True)
        slot_k = start_k * EXPERT_TILE + info[:, 2 * TOP_K + kk:2 * TOP_K + kk + 1]
        dest = jnp.where(lane == kk, slot_k, dest)
    dest_ref[...] = dest.astype(jnp.int32)


def _tn_dot_f32(a, b):
    return lax.dot_general(a, b, (((0,), (0,)), ((), ())), preferred_element_type=F32,
                           precision=lax.Precision.HIGHEST)


def _slots(info, cnt, n_tiles_max):
    tp = info.shape[0]
    tm = tp // SLOTS_STEPS
    n_tile_lanes = -(-n_tiles_max // LANES) * LANES
    return pl.pallas_call(
        _slots_kernel,
        grid=(tp // tm,),
        in_specs=[pl.BlockSpec((tm, LANES), lambda i: (i, 0)), _const_spec((LANES, LANES))],
        out_specs=[pl.BlockSpec((tm, LANES), lambda i: (i, 0)), _const_spec((8, n_tile_lanes))],
        out_shape=[jax.ShapeDtypeStruct((tp, LANES), jnp.int32),
                   jax.ShapeDtypeStruct((8, n_tile_lanes), jnp.int32)],
        scratch_shapes=[pltpu.VMEM((SUBLANES, LANES), F32)],
        compiler_params=_cparams(("arbitrary",)),
        name="slots",
    )(info, cnt)


def _sc_workers():
    sc = plsc.get_sparse_core_info()
    return sc.num_cores, sc.num_cores * sc.num_subcores


def _sc_token_chunks(tp, chunk, body):
    n_cores, n_workers = _sc_workers()
    per_worker = tp // n_workers
    assert per_worker * n_workers == tp and per_worker % chunk == 0 and chunk % SUBLANES == 0
    base = (lax.axis_index("s") * n_cores + lax.axis_index("c")) * per_worker

    @pl.loop(0, per_worker // chunk)
    def _(ci):
        body(pl.multiple_of(base + ci * chunk, SUBLANES))


def _sc_kernel(out_type, chunk, n_row_bufs):
    return functools.partial(
        pl.kernel, mesh=plsc.VectorSubcoreMesh(core_axis_name="c", subcore_axis_name="s"), out_type=out_type,
        scratch_types=[pltpu.VMEM((TOP_K, chunk), jnp.int32), pltpu.VMEM((n_row_bufs, chunk, PACK_WORDS), jnp.uint32),
                       pltpu.SemaphoreType.DMA, pltpu.SemaphoreType.DMA])


def _wait_all(copies):
    for c in copies:
        c.wait()


def _dispatch(dest_flat, u2p, n_slots):
    tp = u2p.shape[0]
    ch = SC_SCATTER_CHUNK

    @_sc_kernel(jax.ShapeDtypeStruct((n_slots, PACK_WORDS), jnp.uint32), ch, 1)
    def scatter_rows(u2_hbm, dest_hbm, xs_hbm, idx_v, rows_v, sem_a, sem_b):
        def chunk(off):
            loads = [pltpu.async_copy(u2_hbm.at[pl.ds(off, ch)], rows_v.at[0], sem_a)]
            loads += [pltpu.async_copy(dest_hbm.at[pl.ds(kk * tp + off, ch)], idx_v.at[kk], sem_a)
                      for kk in range(TOP_K)]
            _wait_all(loads)
            _wait_all([pltpu.async_copy(rows_v.at[0], xs_hbm.at[idx_v.at[kk]], sem_b) for kk in range(TOP_K)])

        _sc_token_chunks(tp, ch, chunk)

    return scatter_rows(u2p, dest_flat)


def _gather_expert_rows(dest_flat, ys):
    tp = dest_flat.shape[0] // TOP_K
    ch = SC_GATHER_CHUNK

    @_sc_kernel(jax.ShapeDtypeStruct((TOP_K, tp, PACK_WORDS), jnp.uint32), ch, TOP_K)
    def gather_rows(ys_hbm, dest_hbm, out_hbm, idx_v, rows_v, sem_a, sem_b):
        def chunk(off):
            _wait_all([pltpu.async_copy(dest_hbm.at[pl.ds(kk * tp + off, ch)], idx_v.at[kk], sem_a)
                       for kk in range(TOP_K)])
            _wait_all([pltpu.async_copy(ys_hbm.at[idx_v.at[kk]], rows_v.at[kk], sem_b) for kk in range(TOP_K)])
            _wait_all([pltpu.async_copy(rows_v.at[kk], out_hbm.at[kk, pl.ds(off, ch)], sem_a)
                       for kk in range(TOP_K)])

        _sc_token_chunks(tp, ch, chunk)

    return gather_rows(ys, dest_flat)


def _expert_kernel(te_ref, nact_ref, x_ref, wgu_hbm, bgu_ref, wdn_hbm, bdn_ref, y_ref,
                   wgu_buf, wdn_buf, slot_ref, sem):
    i = pl.program_id(0)
    n_active = nact_ref[0]
    last_tile = pl.num_programs(0) - 1

    def fetch(expert, slot):
        return (pltpu.make_async_copy(wgu_hbm.at[expert], wgu_buf.at[slot], sem.at[slot, 0]),
                pltpu.make_async_copy(wdn_hbm.at[expert], wdn_buf.at[slot], sem.at[slot, 1]))

    @pl.when(i == 0)
    def _():
        slot_ref[0] = 1
        for copy in fetch(te_ref[0], 0):
            copy.start()

    @pl.when(i < n_active)
    def _():
        expert = te_ref[i]
        first_tile_of_expert = jnp.logical_or(i == 0, expert != te_ref[jnp.maximum(i - 1, 0)])

        @pl.when(first_tile_of_expert)
        def _():
            slot = 1 - slot_ref[0]
            slot_ref[0] = slot
            for copy in fetch(expert, slot):
                copy.wait()
            nxt = lax.while_loop(
                lambda j: jnp.logical_and(j < n_active, te_ref[jnp.minimum(j, last_tile)] == expert),
                lambda j: j + 1, i + 1)

            @pl.when(nxt < n_active)
            def _():
                for copy in fetch(te_ref[jnp.minimum(nxt, last_tile)], 1 - slot):
                    copy.start()

        slot = slot_ref[0]
        x = _unpack_rows(x_ref[...]).astype(BF16)
        gu = jnp.dot(x, wgu_buf[slot].astype(BF16), preferred_element_type=F32) + bgu_ref[0]
        gate = jnp.minimum(gu[:, :D_FF], SWIGLU_LIMIT)
        up = jnp.clip(gu[:, D_FF:], -SWIGLU_LIMIT, SWIGLU_LIMIT)
        hdn = (up + 1.0) * (gate * _sigmoid(SWIGLU_ALPHA * gate))
        y = jnp.dot(hdn.astype(BF16), wdn_buf[slot].astype(BF16), preferred_element_type=F32) + bdn_ref[0]
        y_ref[...] = _pack_rows(y)

    @pl.when(i >= n_active)
    def _():
        y_ref[...] = jnp.zeros_like(y_ref)


def _expert_mlp(te, nact, xs, w_gu, b_gu, w_dn, b_dn):
    d = D_MODEL
    n_tiles = xs.shape[0] // EXPERT_TILE

    def tile_map(i, te_ref, nact_ref):
        return (jnp.minimum(i, nact_ref[0] - 1), 0)

    def exp_map(i, te_ref, nact_ref):
        return (te_ref[jnp.minimum(i, nact_ref[0] - 1)], 0, 0)

    grid_spec = pltpu.PrefetchScalarGridSpec(
        num_scalar_prefetch=2,
        grid=(n_tiles,),
        in_specs=[pl.BlockSpec((EXPERT_TILE, PACK_WORDS), tile_map),
                  pl.BlockSpec(memory_space=pl.ANY),
                  pl.BlockSpec((1, 1, 2 * D_FF), exp_map),
                  pl.BlockSpec(memory_space=pl.ANY),
                  pl.BlockSpec((1, 1, d), exp_map)],
        out_specs=pl.BlockSpec((EXPERT_TILE, PACK_WORDS), lambda i, te_ref, nact_ref: (i, 0)),
        scratch_shapes=[pltpu.VMEM((2, d, 2 * D_FF), F32), pltpu.VMEM((2, D_FF, d), F32),
                        pltpu.SMEM((1,), jnp.int32), pltpu.SemaphoreType.DMA((2, 2))],
    )
    return pl.pallas_call(
        _expert_kernel,
        grid_spec=grid_spec,
        out_shape=jax.ShapeDtypeStruct(xs.shape, jnp.uint32),
        compiler_params=_cparams(("arbitrary",)),
        name="expert_mlp",
    )(te, nact, xs, w_gu, b_gu, w_dn, b_dn)


def _combine_kernel(gfin_ref, *refs):
    out_ref = refs[-1]
    for s in range(COMBINE_SUB):
        h1_ref, info_ref, yg_ref = refs[3 * s:3 * s + 3]
        info = info_ref[...]
        h2 = h1_ref[...]
        for kk in range(TOP_K):
            h2 = h2 + info[:, kk:kk + 1] * _unpack_rows(yg_ref[kk])
        ms = jnp.mean(h2 * h2, axis=-1, keepdims=True)
        out_ref[s * CHUNK:(s + 1) * CHUNK, :] = h2 * lax.rsqrt(ms + RMS_EPS) * gfin_ref[...]


def _combine(h1, info, g_final, yg, n_batch, seq):
    d = h1.shape[1]
    chunks = seq // CHUNK

    def padded(s):
        def index(j):
            c = j * COMBINE_SUB + s
            return (c // chunks) * (chunks + 1) + c % chunks + 1
        return index

    in_specs = [pl.BlockSpec((1, d), lambda j: (0, 0))]
    operands = [g_final]
    for s in range(COMBINE_SUB):
        chunk_of = padded(s)
        in_specs += [pl.BlockSpec((CHUNK, d), lambda j, f=chunk_of: (f(j), 0)),
                     pl.BlockSpec((CHUNK, LANES), lambda j, f=chunk_of: (f(j), 0)),
                     pl.BlockSpec((TOP_K, CHUNK, PACK_WORDS), lambda j, f=chunk_of: (0, f(j), 0))]
        operands += [h1, info, yg]
    return pl.pallas_call(
        _combine_kernel,
        grid=(n_batch * chunks // COMBINE_SUB,),
        in_specs=in_specs,
        out_specs=pl.BlockSpec((COMBINE_SUB * CHUNK, d), lambda j: (j, 0)),
        out_shape=jax.ShapeDtypeStruct((n_batch * seq, d), F32),
        compiler_params=_cparams(("parallel",)),
        name="combine",
    )(*operands)


def kernel(x, meta_tokens, lb_logits, g_mix, w_in, w_dw, b_dw, ln_g, ln_b, w_conv_out, b_conv_out,
           g_onorm, w_rnn_out, w_o, g_ffn, w_router, b_router, w_gate_up, b_gate_up, w_down, b_down,
           g_final):
    n_batch, seq, d = x.shape
    assert d == D_MODEL and w_in.shape[0] == 1, "single-layer block with D_MODEL features"
    assert seq % (CHUNK * COMBINE_SUB) == 0
    seq_pad = CHUNK + seq
    tp = n_batch * seq_pad
    assert tp % ROW_TILE == 0

    x2 = x.reshape(n_batch * seq, d)
    meta = meta_tokens.astype(x.dtype)
    vec = lambda a: a.reshape(1, -1).astype(F32)

    aglu, q, k, logf, v, og, sga, sgb = _in_proj(x2, meta, seq, vec(g_mix[0]), lb_logits.astype(F32),
                                                 w_in[0].astype(BF16))
    ap = _conv_branch(aglu, w_dw[0].astype(F32), vec(b_dw[0]), vec(ln_g[0]), vec(ln_b[0]),
                      w_conv_out[0].astype(BF16), vec(b_conv_out[0]), sga)
    on = _hgrn2_scan(q, k, v, logf, og, vec(g_onorm[0]), n_batch)
    w_router_pad = jnp.pad(w_router[0].astype(F32), ((0, 0), (0, LANES - N_EXPERTS)))
    b_router_pad = jnp.pad(vec(b_router[0]), ((0, 0), (0, LANES - N_EXPERTS)))
    h1, u2p, info, cnt = _merge_route(x2, meta, seq, on, ap, sgb, w_rnn_out[0].astype(BF16), w_o[0].astype(BF16),
                                      vec(g_ffn[0]), w_router_pad, b_router_pad)

    n_tiles_max = -(-(tp * TOP_K + N_EXPERTS * (EXPERT_TILE - 1)) // EXPERT_TILE)
    dest, te = _slots(info, cnt, n_tiles_max)
    dest_flat = dest[:, :TOP_K].T.reshape(-1)
    xs = _dispatch(dest_flat, u2p, n_tiles_max * EXPERT_TILE)
    ys = _expert_mlp(te[0, :n_tiles_max], te[1, :1], xs, w_gate_up[0],
                     b_gate_up[0].reshape(N_EXPERTS, 1, -1).astype(F32), w_down[0],
                     b_down[0].reshape(N_EXPERTS, 1, -1).astype(F32))
    yg = _gather_expert_rows(dest_flat, ys)
    out = _combine(h1, info, vec(g_final), yg, n_batch, seq)
    return out.reshape(n_batch, seq, d)
```

```python
import functools

import jax
import jax.numpy as jnp
from jax import lax
from jax.experimental import pallas as pl
from jax.experimental.pallas import tpu as pltpu
from jax.experimental.pallas import tpu_sc as plsc

F32 = jnp.float32
BF16 = jnp.bfloat16

D_MODEL = 1024
N_META = 16
CHUNK = 128
CHUNK_PAD = CHUNK - N_META
CONV_WIDTH = 31
HEAD_DIM = 128
N_HEADS = D_MODEL // HEAD_DIM
N_EXPERTS = 32
TOP_K = 4
D_FF = D_MODEL
SWIGLU_LIMIT = 7.0
SWIGLU_ALPHA = 1.702
RMS_EPS = 1e-6
LN_EPS = 1e-5

LANES = 128
SUBLANES = 8
SUB_BLOCK = 32
N_SUB = CHUNK // SUB_BLOCK
HALO = 32
ROW_TILE = 640
IN_TILE = 320
SCAN_CHUNKS = 5
MERGE_PARTS = 2
SLOTS_STEPS = 5
CONV_ROWS = 64
EXPERT_TILE = 512
SC_SCATTER_CHUNK = 104
SC_GATHER_CHUNK = 40
COMBINE_SUB = 4
NEG_BIG = -1e30
VMEM_LIMIT = 56 * 1024 * 1024


def _sigmoid(x):
    return 1.0 / (1.0 + jnp.exp(-x))


def _cparams(sem):
    return pltpu.CompilerParams(dimension_semantics=sem, vmem_limit_bytes=VMEM_LIMIT)


def _const_spec(shape):
    nd = len(shape)
    return pl.BlockSpec(shape, lambda *_: (0,) * nd)


PACK_WORDS = D_MODEL // 2
HIGH_HALF = 0xFFFF0000


def _pack_rows(x):
    lo = lax.bitcast_convert_type(x[:, :PACK_WORDS].astype(BF16).astype(F32), jnp.uint32)
    hi = lax.bitcast_convert_type(x[:, PACK_WORDS:].astype(BF16).astype(F32), jnp.uint32)
    return (lo >> 16) | (hi & jnp.uint32(HIGH_HALF))


def _unpack_rows(w):
    lo = lax.bitcast_convert_type(w << 16, F32)
    hi = lax.bitcast_convert_type(w & jnp.uint32(HIGH_HALF), F32)
    return jnp.concatenate([lo, hi], axis=1)


def _residual_tile(x_ref, meta_ref, is_first):
    x = x_ref[...]
    tm, d = x.shape
    prefix = jnp.concatenate([jnp.zeros((CHUNK_PAD, d), x.dtype), meta_ref[...]], axis=0)
    first = jnp.concatenate([prefix, x[:tm - CHUNK, :]], axis=0)
    return jnp.where(is_first, first, x)


def _residual_spec(tm, d, seq, tiles_per_batch, n_tiles):
    def start(i):
        i = jnp.minimum(i, n_tiles - 1)
        b, t = i // tiles_per_batch, i % tiles_per_batch
        return (pl.multiple_of(b * seq + jnp.maximum(t * tm - CHUNK, 0), SUBLANES), 0)
    return pl.BlockSpec((pl.Element(tm), pl.Element(d)), start)


def _in_proj_kernel(x_ref, meta_ref, g_ref, lbl_ref, w_ref, aglu_ref, q_ref, k_ref, logf_ref, v_ref,
                    og_ref, sga_ref, sgb_ref, *, tiles_per_batch):
    d = D_MODEL
    h = _residual_tile(x_ref, meta_ref, pl.program_id(0) % tiles_per_batch == 0)
    ms = jnp.mean(h * h, axis=-1, keepdims=True)
    u = (h * lax.rsqrt(ms + RMS_EPS) * g_ref[...]).astype(BF16)

    def proj(j):
        return jnp.dot(u, w_ref[:, j * d:(j + 1) * d], preferred_element_type=F32)

    aglu_ref[...] = (proj(0) * _sigmoid(proj(1))).astype(aglu_ref.dtype)
    zq = proj(2)
    q_ref[...] = (zq * _sigmoid(zq)).astype(q_ref.dtype)
    lbl = lbl_ref[...]
    e = jnp.exp(lbl - jnp.max(lbl, axis=0, keepdims=True))
    lb = e[0:1, :] / jnp.sum(e, axis=0, keepdims=True)
    s = _sigmoid(proj(3))
    logf_ref[...] = jnp.log(lb + (1.0 - lb) * s)
    k_ref[...] = ((1.0 - lb) * (1.0 - s)).astype(k_ref.dtype)
    v_ref[...] = proj(4).astype(v_ref.dtype)
    zg = proj(5)
    og_ref[...] = (zg * _sigmoid(zg)).astype(og_ref.dtype)
    sga_ref[...] = _sigmoid(proj(6)).astype(sga_ref.dtype)
    sgb_ref[...] = _sigmoid(proj(7)).astype(sgb_ref.dtype)


def _in_proj(x2, meta, seq, g_mix, lb_logits, w_in_bf16):
    d = x2.shape[1]
    n_batch = x2.shape[0] // seq
    tm = IN_TILE
    tpb = (seq + CHUNK) // tm
    tp = n_batch * (seq + CHUNK)
    assert tpb * tm == seq + CHUNK and tm > CHUNK
    row = pl.BlockSpec((tm, d), lambda i: (i, 0))
    out_dtypes = [BF16, BF16, BF16, F32, BF16, BF16, BF16, BF16]
    return pl.pallas_call(
        functools.partial(_in_proj_kernel, tiles_per_batch=tpb),
        grid=(tp // tm,),
        in_specs=[_residual_spec(tm, d, seq, tpb, tp // tm), _const_spec(meta.shape), _const_spec((1, d)),
                  _const_spec(lb_logits.shape),
                  pl.BlockSpec(w_in_bf16.shape, lambda i: (0, 0), pipeline_mode=pl.Buffered(1))],
        out_specs=[row] * 8,
        out_shape=[jax.ShapeDtypeStruct((tp, d), dt) for dt in out_dtypes],
        compiler_params=_cparams(("parallel",)),
        name="in_proj",
    )(x2, meta, g_mix, lb_logits, w_in_bf16)


def _conv_kernel(halo_ref, cur_ref, wdw_ref, bdw_ref, lng_ref, lnb_ref, wout_ref, bout_ref, sga_ref,
                 out_ref, win_ref, acc_ref, shift_ref):
    tm = cur_ref.shape[0]
    win_ref[0:HALO, :] = halo_ref[...].astype(F32)
    win_ref[HALO:, :] = cur_ref[...].astype(F32)
    first_tap = HALO - (CONV_WIDTH - 1)
    for c in range(D_MODEL // LANES):
        lanes = slice(c * LANES, (c + 1) * LANES)
        for s in range(SUBLANES):
            n_rows = tm + HALO - (SUBLANES if s else 0)
            shift_ref[s, 0:n_rows, :] = win_ref[pl.ds(s, n_rows), lanes]
        w_c = wdw_ref[:, lanes]
        bias = jnp.broadcast_to(bdw_ref[:, lanes], (CONV_ROWS, LANES))

        def chunk(r, carry, lanes=lanes, w_c=w_c, bias=bias):
            r0 = pl.multiple_of(r * CONV_ROWS, CONV_ROWS)
            acc = bias
            for j in range(CONV_WIDTH):
                off = first_tap + j
                rows = pl.ds(r0 + off - off % SUBLANES, CONV_ROWS)
                acc = acc + w_c[j:j + 1, :] * shift_ref[off % SUBLANES, rows, :]
            acc_ref[pl.ds(r0, CONV_ROWS), lanes] = acc
            return carry

        lax.fori_loop(0, tm // CONV_ROWS, chunk, 0)
    a = acc_ref[...]
    mu = jnp.mean(a, axis=-1, keepdims=True)
    ac = a - mu
    var = jnp.mean(ac * ac, axis=-1, keepdims=True)
    y = ac * lax.rsqrt(var + LN_EPS) * lng_ref[...] + lnb_ref[...]
    y = y * _sigmoid(y)
    o = jnp.dot(y.astype(BF16), wout_ref[...], preferred_element_type=F32) + bout_ref[...]
    out_ref[...] = (sga_ref[...].astype(F32) * o).astype(out_ref.dtype)


def _conv_branch(aglu, w_dw, b_dw, ln_g, ln_b, w_out_bf16, b_out, sga):
    tp, d = aglu.shape
    tm = ROW_TILE
    per = tm // HALO
    row = pl.BlockSpec((tm, d), lambda i: (i, 0))
    halo = pl.BlockSpec((HALO, d), lambda i: (jnp.maximum(i * per - 1, 0), 0))
    vec = _const_spec((1, d))
    return pl.pallas_call(
        _conv_kernel,
        grid=(tp // tm,),
        in_specs=[halo, row, _const_spec(w_dw.shape), vec, vec, vec, _const_spec((d, d)), vec, row],
        out_specs=row,
        out_shape=jax.ShapeDtypeStruct((tp, d), BF16),
        scratch_shapes=[pltpu.VMEM((tm + HALO, d), F32), pltpu.VMEM((tm, d), F32),
                        pltpu.VMEM((SUBLANES, tm + HALO, LANES), F32)],
        compiler_params=_cparams(("parallel",)),
        name="conv_branch",
    )(aglu, aglu, w_dw, b_dw, ln_g, ln_b, w_out_bf16, b_out, sga)


def _split_bf16(x):
    hi = x.astype(BF16)
    return hi, (x - hi.astype(F32)).astype(BF16)


def _nt_dot(a, b):
    return lax.dot_general(a, b, (((1,), (1,)), ((), ())), preferred_element_type=F32)


def _tn_dot(a, b):
    return lax.dot_general(a, b, (((0,), (0,)), ((), ())), preferred_element_type=F32)


def _scan_kernel(q_ref, k_ref, v_ref, lf_ref, og_ref, gon_ref, o_ref, st_ref):
    @pl.when(pl.program_id(1) == 0)
    def _():
        st_ref[...] = jnp.zeros_like(st_ref)

    c = CHUNK
    row = lax.broadcasted_iota(jnp.int32, (c, c), 0)
    col = lax.broadcasted_iota(jnp.int32, (c, c), 1)
    causal = col <= row
    diag_mask = jnp.logical_and(causal, row // SUB_BLOCK == col // SUB_BLOCK)
    tri = causal.astype(BF16)

    def bcast_rows(rows):
        return jnp.concatenate([jnp.broadcast_to(r, (SUB_BLOCK, HEAD_DIM)) for r in rows], axis=0)

    heads = [slice(h * HEAD_DIM, (h + 1) * HEAD_DIM) for h in range(N_HEADS)]
    chunks = [slice(ci * c, (ci + 1) * c) for ci in range(SCAN_CHUNKS)]

    bcums = []
    for cs in chunks:
        lf = lf_ref[cs, :]
        lf_hi = lf.astype(BF16)
        lf_mid, lf_lo = _split_bf16(lf - lf_hi.astype(F32))
        bcums.append(jnp.dot(tri, lf_hi, preferred_element_type=F32)
                     + (jnp.dot(tri, lf_mid, preferred_element_type=F32)
                        + jnp.dot(tri, lf_lo, preferred_element_type=F32)))
    operands = {}
    for ci, cs in enumerate(chunks):
        for h, hs in enumerate(heads):
            b = bcums[ci][:, hs]
            q = q_ref[cs, hs].astype(F32)
            k = k_ref[cs, hs].astype(F32)
            ends = [b[i * SUB_BLOCK + SUB_BLOCK - 1:i * SUB_BLOCK + SUB_BLOCK, :] for i in range(N_SUB)]
            mids = [b[i * SUB_BLOCK + SUB_BLOCK // 2 - 1:i * SUB_BLOCK + SUB_BLOCK // 2, :] for i in range(N_SUB)]
            mid_full = bcast_rows(mids)
            b_last = ends[-1]
            zero_row = jnp.zeros_like(b_last)
            qm = q * jnp.exp(b - mid_full)
            km = k * jnp.exp(mid_full - b)
            q_parts, k_parts = [], []
            for j in range(N_SUB - 1):
                q_rows = [jnp.exp(mids[i] - ends[j]) if i > j else zero_row for i in range(N_SUB)]
                k_rows = [jnp.exp(ends[j] - mids[j]) if i == j else zero_row for i in range(N_SUB)]
                q_parts.append((qm * bcast_rows(q_rows)).astype(BF16))
                k_parts.append((km * bcast_rows(k_rows)).astype(BF16))
            q_in = (qm * bcast_rows([jnp.exp(m) for m in mids])).astype(BF16)
            k_out = (km * bcast_rows([jnp.exp(b_last - m) for m in mids])).astype(BF16)
            operands[ci, h] = (qm.astype(BF16), km.astype(BF16), jnp.concatenate(q_parts, axis=1),
                               jnp.concatenate(k_parts, axis=1), q_in, k_out, jnp.exp(b_last))
    products = {}
    for ci, cs in enumerate(chunks):
        for h, hs in enumerate(heads):
            qm, km, q_cat, k_cat, _, k_out, _ = operands[ci, h]
            products[ci, h] = (_nt_dot(qm, km), _nt_dot(q_cat, k_cat), _tn_dot(v_ref[cs, hs], k_out))
    within = {}
    for ci, cs in enumerate(chunks):
        for h, hs in enumerate(heads):
            same_block, earlier_blocks, _ = products[ci, h]
            scores = jnp.where(diag_mask, same_block, 0.0) + earlier_blocks
            within[ci, h] = jnp.dot(scores.astype(BF16), v_ref[cs, hs], preferred_element_type=F32)
    states = [st_ref[h] for h in range(N_HEADS)]
    outs = {}
    for ci in range(SCAN_CHUNKS):
        for h in range(N_HEADS):
            outs[ci, h] = within[ci, h] + _nt_dot(operands[ci, h][4], states[h].astype(BF16))
            states[h] = states[h] * operands[ci, h][6] + products[ci, h][2]
    for h in range(N_HEADS):
        st_ref[h] = states[h]
    for ci, cs in enumerate(chunks):
        for h, hs in enumerate(heads):
            o = outs[ci, h]
            ms = jnp.mean(o * o, axis=-1, keepdims=True)
            on = o * lax.rsqrt(ms + RMS_EPS) * gon_ref[:, hs]
            o_ref[cs, hs] = (on * og_ref[cs, hs].astype(F32)).astype(o_ref.dtype)


def _hgrn2_scan(q, k, v, logf, og, g_onorm, n_batch):
    tp, d = q.shape
    rows = SCAN_CHUNKS * CHUNK
    n_chunks = tp // n_batch // rows
    assert n_chunks * rows * n_batch == tp
    blk = pl.BlockSpec((rows, d), lambda b, c: (b * n_chunks + c, 0))
    return pl.pallas_call(
        _scan_kernel,
        grid=(n_batch, n_chunks),
        in_specs=[blk, blk, blk, blk, blk, pl.BlockSpec((1, d), lambda b, c: (0, 0))],
        out_specs=blk,
        out_shape=jax.ShapeDtypeStruct((tp, d), BF16),
        scratch_shapes=[pltpu.VMEM((N_HEADS, HEAD_DIM, HEAD_DIM), F32)],
        compiler_params=_cparams(("arbitrary", "arbitrary")),
        name="hgrn2_scan",
    )(q, k, v, logf, og, g_onorm)


def _merge_route_kernel(x_ref, meta_ref, on_ref, ap_ref, sgb_ref, wrnn_ref, wo_ref, gffn_ref, wr_ref, br_ref,
                        h1_ref, u2_ref, info_ref, cnt_ref, tri_ref, carry_ref, colcnt_ref, logits_ref, hres_ref,
                        *, tiles_per_batch, n_tiles):
    tm = on_ref.shape[0]
    i = pl.program_id(0)
    hres_ref[...] = _residual_tile(x_ref, meta_ref, jnp.minimum(i, n_tiles - 1) % tiles_per_batch == 0)

    @pl.when(i == 0)
    def _():
        r_i = lax.broadcasted_iota(jnp.int32, (tm, tm), 0)
        c_i = lax.broadcasted_iota(jnp.int32, (tm, tm), 1)
        tri_ref[...] = (c_i < r_i).astype(BF16)
        carry_ref[...] = jnp.zeros_like(carry_ref)
        colcnt_ref[...] = jnp.zeros_like(colcnt_ref)
        logits_ref[...] = jnp.zeros_like(logits_ref)

    routed = i > 0
    lane = lax.broadcasted_iota(jnp.int32, (tm, LANES), 1)
    cur = jnp.where(lane < N_EXPERTS, logits_ref[...], NEG_BIG)
    vals, idxs, sels = [], [], []

    def topk_round(cur):
        m = jnp.max(cur, axis=-1, keepdims=True)
        idx = jnp.min(jnp.where(cur == m, lane, LANES), axis=-1, keepdims=True)
        sel = lane == idx
        vals.append(m)
        idxs.append(idx)
        sels.append(sel)
        return jnp.where(sel, 2.0 * NEG_BIG, cur)

    w_hi, w_lo = _split_bf16(wr_ref[...])
    part = tm // MERGE_PARTS
    rows = [slice(p * part, (p + 1) * part) for p in range(MERGE_PARTS)]
    rs = [jnp.dot(on_ref[rw, :], wrnn_ref[...], preferred_element_type=F32) for rw in rows]
    cur = topk_round(cur)
    ys = [(ap_ref[rw, :].astype(F32) + sgb_ref[rw, :].astype(F32) * r).astype(BF16) for rw, r in zip(rows, rs)]
    h1s = [hres_ref[rw, :] + jnp.dot(y, wo_ref[...], preferred_element_type=F32) for rw, y in zip(rows, ys)]
    cur = topk_round(cur)
    u2s = []
    for rw, h1 in zip(rows, h1s):
        h1_ref[rw, :] = h1
        ms = jnp.mean(h1 * h1, axis=-1, keepdims=True)
        u2 = h1 * lax.rsqrt(ms + RMS_EPS) * gffn_ref[...]
        u2_ref[rw, :] = _pack_rows(u2)
        u2s.append(u2)
    cur = topk_round(cur)
    new_logits = []
    for u2 in u2s:
        u_hi, u_lo = _split_bf16(u2)
        new_logits.append(jnp.dot(u_hi, w_hi, preferred_element_type=F32)
                          + (jnp.dot(u_lo, w_hi, preferred_element_type=F32)
                             + jnp.dot(u_hi, w_lo, preferred_element_type=F32))
                          + br_ref[...])
    cur = topk_round(cur)
    assert len(vals) == TOP_K
    exps = [jnp.exp(vk - vals[0]) for vk in vals]
    den = exps[0] + exps[1] + exps[2] + exps[3]
    onehot = jnp.logical_or(jnp.logical_or(sels[0], sels[1]), jnp.logical_or(sels[2], sels[3]))
    onehot = jnp.logical_and(onehot, routed)
    onehot_bf = onehot.astype(BF16)
    rank_all = jnp.dot(tri_ref[...], onehot_bf, preferred_element_type=F32) + carry_ref[...]
    carry_ref[...] += jnp.sum(onehot.astype(F32), axis=0, keepdims=True)
    colcnt_ref[...] += _tn_dot(onehot_bf, jnp.ones((tm, LANES), BF16))
    info = jnp.zeros((tm, LANES), F32)
    for kk in range(TOP_K):
        rank_k = jnp.sum(jnp.where(sels[kk], rank_all, 0.0), axis=-1, keepdims=True)
        info = jnp.where(lane == kk, exps[kk] / den, info)
        info = jnp.where(lane == TOP_K + kk, idxs[kk].astype(F32), info)
        info = jnp.where(lane == 2 * TOP_K + kk, rank_k, info)
    info_ref[...] = info
    cnt_ref[...] = colcnt_ref[...]
    logits_ref[...] = jnp.concatenate(new_logits, axis=0)


def _merge_route(x2, meta, seq, on, ap, sgb, w_rnn_bf16, w_o_bf16, g_ffn, w_router_pad, b_router_pad):
    tp, d = on.shape
    tm = ROW_TILE
    n_tiles = tp // tm
    tpb = (seq + CHUNK) // tm
    assert tpb * tm == seq + CHUNK
    this_tile = lambda i: (jnp.minimum(i, n_tiles - 1), 0)
    prev_tile = lambda i: (jnp.maximum(i - 1, 0), 0)
    row = pl.BlockSpec((tm, d), this_tile)
    vec = _const_spec((1, d))
    return pl.pallas_call(
        functools.partial(_merge_route_kernel, tiles_per_batch=tpb, n_tiles=n_tiles),
        grid=(n_tiles + 1,),
        in_specs=[_residual_spec(tm, d, seq, tpb, n_tiles), _const_spec(meta.shape), row, row, row,
                  _const_spec((d, d)), _const_spec((d, d)), vec, _const_spec((d, LANES)), _const_spec((1, LANES))],
        out_specs=[row, pl.BlockSpec((tm, PACK_WORDS), this_tile),
                   pl.BlockSpec((tm, LANES), prev_tile), _const_spec((LANES, LANES))],
        out_shape=[jax.ShapeDtypeStruct((tp, d), F32), jax.ShapeDtypeStruct((tp, PACK_WORDS), jnp.uint32),
                   jax.ShapeDtypeStruct((tp, LANES), F32), jax.ShapeDtypeStruct((LANES, LANES), F32)],
        scratch_shapes=[pltpu.VMEM((tm, tm), BF16), pltpu.VMEM((1, LANES), F32),
                        pltpu.VMEM((LANES, LANES), F32), pltpu.VMEM((tm, LANES), F32), pltpu.VMEM((tm, d), F32)],
        compiler_params=_cparams(("arbitrary",)),
        name="merge_route",
    )(x2, meta, on, ap, sgb, w_rnn_bf16, w_o_bf16, g_ffn, w_router_pad, b_router_pad)


def _slots_kernel(info_ref, cnt_ref, dest_ref, te_ref, pstart_ref):
    tm = info_ref.shape[0]
    n_tile_lanes = te_ref.shape[1]

    @pl.when(pl.program_id(0) == 0)
    def _():
        r_i = lax.broadcasted_iota(jnp.int32, (LANES, LANES), 0)
        c_i = lax.broadcasted_iota(jnp.int32, (LANES, LANES), 1)
        cnt = cnt_ref[...]
        tiles = jnp.floor((cnt + (EXPERT_TILE - 1)) / EXPERT_TILE)
        pend_col = jnp.dot((c_i <= r_i).astype(F32), tiles, preferred_element_type=F32,
                           precision=lax.Precision.HIGHEST)
        pstart_ref[...] = _tn_dot_f32(tiles, (r_i < c_i).astype(F32))[0:SUBLANES, :]
        tile_id = lax.broadcasted_iota(jnp.int32, (LANES, n_tile_lanes), 1).astype(F32)
        exp_id = lax.broadcasted_iota(jnp.int32, (LANES, n_tile_lanes), 0)
        pend_wide = jnp.concatenate([pend_col] * (n_tile_lanes // LANES), axis=1)
        below = jnp.logical_and(pend_wide <= tile_id, exp_id < N_EXPERTS)
        te = jnp.minimum(jnp.sum(below.astype(F32), axis=0, keepdims=True), N_EXPERTS - 1.0)
        n_active = pend_col[N_EXPERTS - 1:N_EXPERTS, 0:1]
        owner = exp_id.astype(F32) == te
        cnt_wide = jnp.concatenate([cnt] * (n_tile_lanes // LANES), axis=1)
        first_tile = pend_wide - jnp.concatenate([tiles] * (n_tile_lanes // LANES), axis=1)
        left = jnp.sum(jnp.where(owner, cnt_wide - (tile_id - first_tile) * EXPERT_TILE, 0.0), axis=0, keepdims=True)
        valid = jnp.clip(left, 0.0, float(EXPERT_TILE))
        sub_t = lax.broadcasted_iota(jnp.int32, (SUBLANES, n_tile_lanes), 0)
        out = jnp.where(sub_t == 0, jnp.broadcast_to(te, (SUBLANES, n_tile_lanes)),
                        jnp.where(sub_t == 1, jnp.broadcast_to(n_active, (SUBLANES, n_tile_lanes)),
                                  jnp.broadcast_to(valid, (SUBLANES, n_tile_lanes))))
        te_ref[...] = out.astype(jnp.int32)

    info = info_ref[...]
    pstart_row = pstart_ref[0:1, :]
    lane = lax.broadcasted_iota(jnp.int32, (tm, LANES), 1)
    dest = jnp.zeros((tm, LANES), F32)
    for kk in range(TOP_K):
        idx_k = info[:, TOP_K + kk:TOP_K + kk + 1].astype(jnp.int32)
        start_k = jnp.sum(jnp.where(lane == idx_k, pstart_row, 0.0), axis=-1, keepdims=True)
        slot_k = start_k * EXPERT_TILE + info[:, 2 * TOP_K + kk:2 * TOP_K + kk + 1]
        dest = jnp.where(lane == kk, slot_k, dest)
    dest_ref[...] = dest.astype(jnp.int32)


def _tn_dot_f32(a, b):
    return lax.dot_general(a, b, (((0,), (0,)), ((), ())), preferred_element_type=F32,
                           precision=lax.Precision.HIGHEST)


def _slots(info, cnt, n_tiles_max):
    tp = info.shape[0]
    tm = tp // SLOTS_STEPS
    n_tile_lanes = -(-n_tiles_max // LANES) * LANES
    return pl.pallas_call(
        _slots_kernel,
        grid=(tp // tm,),
        in_specs=[pl.BlockSpec((tm, LANES), lambda i: (i, 0)), _const_spec((LANES, LANES))],
        out_specs=[pl.BlockSpec((tm, LANES), lambda i: (i, 0)), _const_spec((8, n_tile_lanes))],
        out_shape=[jax.ShapeDtypeStruct((tp, LANES), jnp.int32),
                   jax.ShapeDtypeStruct((8, n_tile_lanes), jnp.int32)],
        scratch_shapes=[pltpu.VMEM((SUBLANES, LANES), F32)],
        compiler_params=_cparams(("arbitrary",)),
        name="slots",
    )(info, cnt)


def _sc_workers():
    sc = plsc.get_sparse_core_info()
    return sc.num_cores, sc.num_cores * sc.num_subcores


def _sc_token_chunks(tp, chunk, body):
    n_cores, n_workers = _sc_workers()
    per_worker = tp // n_workers
    assert per_worker * n_workers == tp and per_worker % chunk == 0 and chunk % SUBLANES == 0
    base = (lax.axis_index("s") * n_cores + lax.axis_index("c")) * per_worker

    @pl.loop(0, per_worker // chunk)
    def _(ci):
        body(pl.multiple_of(base + ci * chunk, SUBLANES))


def _sc_kernel(out_type, chunk, n_row_bufs):
    return functools.partial(
        pl.kernel, mesh=plsc.VectorSubcoreMesh(core_axis_name="c", subcore_axis_name="s"), out_type=out_type,
        scratch_types=[pltpu.VMEM((TOP_K, chunk), jnp.int32), pltpu.VMEM((n_row_bufs, chunk, PACK_WORDS), jnp.uint32),
                       pltpu.SemaphoreType.DMA, pltpu.SemaphoreType.DMA])


def _wait_all(copies):
    for c in copies:
        c.wait()


def _dispatch(dest_flat, u2p, n_slots):
    tp = u2p.shape[0]
    ch = SC_SCATTER_CHUNK

    @_sc_kernel(jax.ShapeDtypeStruct((n_slots, PACK_WORDS), jnp.uint32), ch, 1)
    def scatter_rows(u2_hbm, dest_hbm, xs_hbm, idx_v, rows_v, sem_a, sem_b):
        def chunk(off):
            loads = [pltpu.async_copy(u2_hbm.at[pl.ds(off, ch)], rows_v.at[0], sem_a)]
            loads += [pltpu.async_copy(dest_hbm.at[pl.ds(kk * tp + off, ch)], idx_v.at[kk], sem_a)
                      for kk in range(TOP_K)]
            _wait_all(loads)
            _wait_all([pltpu.async_copy(rows_v.at[0], xs_hbm.at[idx_v.at[kk]], sem_b) for kk in range(TOP_K)])

        _sc_token_chunks(tp, ch, chunk)

    return scatter_rows(u2p, dest_flat)


def _gather_expert_rows(dest_flat, ys):
    tp = dest_flat.shape[0] // TOP_K
    ch = SC_GATHER_CHUNK

    @_sc_kernel(jax.ShapeDtypeStruct((TOP_K, tp, PACK_WORDS), jnp.uint32), ch, TOP_K)
    def gather_rows(ys_hbm, dest_hbm, out_hbm, idx_v, rows_v, sem_a, sem_b):
        def chunk(off):
            _wait_all([pltpu.async_copy(dest_hbm.at[pl.ds(kk * tp + off, ch)], idx_v.at[kk], sem_a)
                       for kk in range(TOP_K)])
            _wait_all([pltpu.async_copy(ys_hbm.at[idx_v.at[kk]], rows_v.at[kk], sem_b) for kk in range(TOP_K)])
            _wait_all([pltpu.async_copy(rows_v.at[kk], out_hbm.at[kk, pl.ds(off, ch)], sem_a)
                       for kk in range(TOP_K)])

        _sc_token_chunks(tp, ch, chunk)

    return gather_rows(ys, dest_flat)


def _expert_kernel(te_ref, nact_ref, valid_ref, x_ref, wgu_hbm, bgu_ref, wdn_hbm, bdn_ref, y_ref,
                   wgu_buf, wdn_buf, slot_ref, sem):
    i = pl.program_id(0)
    n_active = nact_ref[0]
    last_tile = pl.num_programs(0) - 1

    def fetch(expert, slot):
        return (pltpu.make_async_copy(wgu_hbm.at[expert], wgu_buf.at[slot], sem.at[slot, 0]),
                pltpu.make_async_copy(wdn_hbm.at[expert], wdn_buf.at[slot], sem.at[slot, 1]))

    @pl.when(i == 0)
    def _():
        slot_ref[0] = 1
        for copy in fetch(te_ref[0], 0):
            copy.start()

    @pl.when(i < n_active)
    def _():
        expert = te_ref[i]
        first_tile_of_expert = jnp.logical_or(i == 0, expert != te_ref[jnp.maximum(i - 1, 0)])

        @pl.when(first_tile_of_expert)
        def _():
            slot = 1 - slot_ref[0]
            slot_ref[0] = slot
            for copy in fetch(expert, slot):
                copy.wait()
            nxt = lax.while_loop(
                lambda j: jnp.logical_and(j < n_active, te_ref[jnp.minimum(j, last_tile)] == expert),
                lambda j: j + 1, i + 1)

            @pl.when(nxt < n_active)
            def _():
                for copy in fetch(te_ref[jnp.minimum(nxt, last_tile)], 1 - slot):
                    copy.start()

        slot = slot_ref[0]

        def mlp(rows):
            x = _unpack_rows(x_ref[0:rows, :]).astype(BF16)
            gu = jnp.dot(x, wgu_buf[slot].astype(BF16), preferred_element_type=F32) + bgu_ref[0]
            gate = jnp.minimum(gu[:, :D_FF], SWIGLU_LIMIT)
            up = jnp.clip(gu[:, D_FF:], -SWIGLU_LIMIT, SWIGLU_LIMIT)
            hdn = (up + 1.0) * (gate * _sigmoid(SWIGLU_ALPHA * gate))
            y = jnp.dot(hdn.astype(BF16), wdn_buf[slot].astype(BF16), preferred_element_type=F32) + bdn_ref[0]
            y_ref[0:rows, :] = _pack_rows(y)

        half = EXPERT_TILE // 2
        half_empty = valid_ref[i] <= half
        pl.when(jnp.logical_not(half_empty))(lambda: mlp(EXPERT_TILE))

        @pl.when(half_empty)
        def _():
            mlp(half)
            y_ref[half:, :] = jnp.zeros((EXPERT_TILE - half, PACK_WORDS), y_ref.dtype)

    @pl.when(i >= n_active)
    def _():
        y_ref[...] = jnp.zeros_like(y_ref)


def _expert_mlp(te, nact, valid, xs, w_gu, b_gu, w_dn, b_dn):
    d = D_MODEL
    n_tiles = xs.shape[0] // EXPERT_TILE

    def tile_map(i, te_ref, nact_ref, valid_ref):
        return (jnp.minimum(i, nact_ref[0] - 1), 0)

    def exp_map(i, te_ref, nact_ref, valid_ref):
        return (te_ref[jnp.minimum(i, nact_ref[0] - 1)], 0, 0)

    grid_spec = pltpu.PrefetchScalarGridSpec(
        num_scalar_prefetch=3,
        grid=(n_tiles,),
        in_specs=[pl.BlockSpec((EXPERT_TILE, PACK_WORDS), tile_map),
                  pl.BlockSpec(memory_space=pl.ANY),
                  pl.BlockSpec((1, 1, 2 * D_FF), exp_map),
                  pl.BlockSpec(memory_space=pl.ANY),
                  pl.BlockSpec((1, 1, d), exp_map)],
        out_specs=pl.BlockSpec((EXPERT_TILE, PACK_WORDS), lambda i, te_ref, nact_ref, valid_ref: (i, 0)),
        scratch_shapes=[pltpu.VMEM((2, d, 2 * D_FF), F32), pltpu.VMEM((2, D_FF, d), F32),
                        pltpu.SMEM((1,), jnp.int32), pltpu.SemaphoreType.DMA((2, 2))],
    )
    return pl.pallas_call(
        _expert_kernel,
        grid_spec=grid_spec,
        out_shape=jax.ShapeDtypeStruct(xs.shape, jnp.uint32),
        compiler_params=_cparams(("arbitrary",)),
        name="expert_mlp",
    )(te, nact, valid, xs, w_gu, b_gu, w_dn, b_dn)


def _combine_kernel(gfin_ref, *refs):
    out_ref = refs[-1]
    for s in range(COMBINE_SUB):
        h1_ref, info_ref, yg_ref = refs[3 * s:3 * s + 3]
        info = info_ref[...]
        h2 = h1_ref[...]
        for kk in range(TOP_K):
            h2 = h2 + info[:, kk:kk + 1] * _unpack_rows(yg_ref[kk])
        ms = jnp.mean(h2 * h2, axis=-1, keepdims=True)
        out_ref[s * CHUNK:(s + 1) * CHUNK, :] = h2 * lax.rsqrt(ms + RMS_EPS) * gfin_ref[...]


def _combine(h1, info, g_final, yg, n_batch, seq):
    d = h1.shape[1]
    chunks = seq // CHUNK

    def padded(s):
        def index(j):
            c = j * COMBINE_SUB + s
            return (c // chunks) * (chunks + 1) + c % chunks + 1
        return index

    in_specs = [pl.BlockSpec((1, d), lambda j: (0, 0))]
    operands = [g_final]
    for s in range(COMBINE_SUB):
        chunk_of = padded(s)
        in_specs += [pl.BlockSpec((CHUNK, d), lambda j, f=chunk_of: (f(j), 0)),
                     pl.BlockSpec((CHUNK, LANES), lambda j, f=chunk_of: (f(j), 0)),
                     pl.BlockSpec((TOP_K, CHUNK, PACK_WORDS), lambda j, f=chunk_of: (0, f(j), 0))]
        operands += [h1, info, yg]
    return pl.pallas_call(
        _combine_kernel,
        grid=(n_batch * chunks // COMBINE_SUB,),
        in_specs=in_specs,
        out_specs=pl.BlockSpec((COMBINE_SUB * CHUNK, d), lambda j: (j, 0)),
        out_shape=jax.ShapeDtypeStruct((n_batch * seq, d), F32),
        compiler_params=_cparams(("parallel",)),
        name="combine",
    )(*operands)


def kernel(x, meta_tokens, lb_logits, g_mix, w_in, w_dw, b_dw, ln_g, ln_b, w_conv_out, b_conv_out,
           g_onorm, w_rnn_out, w_o, g_ffn, w_router, b_router, w_gate_up, b_gate_up, w_down, b_down,
           g_final):
    n_batch, seq, d = x.shape
    assert d == D_MODEL and w_in.shape[0] == 1, "single-layer block with D_MODEL features"
    assert seq % (CHUNK * COMBINE_SUB) == 0
    seq_pad = CHUNK + seq
    tp = n_batch * seq_pad
    assert tp % ROW_TILE == 0

    x2 = x.reshape(n_batch * seq, d)
    meta = meta_tokens.astype(x.dtype)
    vec = lambda a: a.reshape(1, -1).astype(F32)

    aglu, q, k, logf, v, og, sga, sgb = _in_proj(x2, meta, seq, vec(g_mix[0]), lb_logits.astype(F32),
                                                 w_in[0].astype(BF16))
    ap = _conv_branch(aglu, w_dw[0].astype(F32), vec(b_dw[0]), vec(ln_g[0]), vec(ln_b[0]),
                      w_conv_out[0].astype(BF16), vec(b_conv_out[0]), sga)
    on = _hgrn2_scan(q, k, v, logf, og, vec(g_onorm[0]), n_batch)
    w_router_pad = jnp.pad(w_router[0].astype(F32), ((0, 0), (0, LANES - N_EXPERTS)))
    b_router_pad = jnp.pad(vec(b_router[0]), ((0, 0), (0, LANES - N_EXPERTS)))
    h1, u2p, info, cnt = _merge_route(x2, meta, seq, on, ap, sgb, w_rnn_out[0].astype(BF16), w_o[0].astype(BF16),
                                      vec(g_ffn[0]), w_router_pad, b_router_pad)

    n_tiles_max = -(-(tp * TOP_K + N_EXPERTS * (EXPERT_TILE - 1)) // EXPERT_TILE)
    dest, te = _slots(info, cnt, n_tiles_max)
    dest_flat = dest[:, :TOP_K].T.reshape(-1)
    xs = _dispatch(dest_flat, u2p, n_tiles_max * EXPERT_TILE)
    ys = _expert_mlp(te[0, :n_tiles_max], te[1, :1], te[2, :n_tiles_max], xs, w_gate_up[0],
                     b_gate_up[0].reshape(N_EXPERTS, 1, -1).astype(F32), w_down[0],
                     b_down[0].reshape(N_EXPERTS, 1, -1).astype(F32))
    yg = _gather_expert_rows(dest_flat, ys)
    out = _combine(h1, info, vec(g_final), yg, n_batch, seq)
    return out.reshape(n_batch, seq, d)
```

```python
import functools

import jax
import jax.numpy as jnp
from jax import lax
from jax.experimental import pallas as pl
from jax.experimental.pallas import tpu as pltpu
from jax.experimental.pallas import tpu_sc as plsc

F32 = jnp.float32
BF16 = jnp.bfloat16

D_MODEL = 1024
N_META = 16
CHUNK = 128
CHUNK_PAD = CHUNK - N_META
CONV_WIDTH = 31
HEAD_DIM = 128
N_HEADS = D_MODEL // HEAD_DIM
N_EXPERTS = 32
TOP_K = 4
D_FF = D_MODEL
SWIGLU_LIMIT = 7.0
SWIGLU_ALPHA = 1.702
RMS_EPS = 1e-6
LN_EPS = 1e-5

LANES = 128
SUBLANES = 8
SUB_BLOCK = 32
N_SUB = CHUNK // SUB_BLOCK
HALO = 32
ROW_TILE = 640
IN_TILE = 320
SCAN_CHUNKS = 5
MERGE_PARTS = 2
SLOTS_STEPS = 5
CONV_ROWS = 64
EXPERT_TILE = 512
SC_SCATTER_CHUNK = 104
SC_GATHER_CHUNK = 40
COMBINE_SUB = 4
NEG_BIG = -1e30
VMEM_LIMIT = 56 * 1024 * 1024


def _sigmoid(x):
    return 1.0 / (1.0 + jnp.exp(-x))


def _cparams(sem):
    return pltpu.CompilerParams(dimension_semantics=sem, vmem_limit_bytes=VMEM_LIMIT)


def _const_spec(shape):
    nd = len(shape)
    return pl.BlockSpec(shape, lambda *_: (0,) * nd)


PACK_WORDS = D_MODEL // 2
HIGH_HALF = 0xFFFF0000


def _pack_rows(x):
    lo = lax.bitcast_convert_type(x[:, :PACK_WORDS].astype(BF16).astype(F32), jnp.uint32)
    hi = lax.bitcast_convert_type(x[:, PACK_WORDS:].astype(BF16).astype(F32), jnp.uint32)
    return (lo >> 16) | (hi & jnp.uint32(HIGH_HALF))


def _unpack_rows(w):
    lo = lax.bitcast_convert_type(w << 16, F32)
    hi = lax.bitcast_convert_type(w & jnp.uint32(HIGH_HALF), F32)
    return jnp.concatenate([lo, hi], axis=1)


def _residual_tile(x_ref, meta_ref, is_first):
    x = x_ref[...]
    tm, d = x.shape
    prefix = jnp.concatenate([jnp.zeros((CHUNK_PAD, d), x.dtype), meta_ref[...]], axis=0)
    first = jnp.concatenate([prefix, x[:tm - CHUNK, :]], axis=0)
    return jnp.where(is_first, first, x)


def _residual_spec(tm, d, seq, tiles_per_batch, n_tiles):
    def start(i):
        i = jnp.minimum(i, n_tiles - 1)
        b, t = i // tiles_per_batch, i % tiles_per_batch
        return (pl.multiple_of(b * seq + jnp.maximum(t * tm - CHUNK, 0), SUBLANES), 0)
    return pl.BlockSpec((pl.Element(tm), pl.Element(d)), start)


def _in_proj_kernel(x_ref, meta_ref, g_ref, lbl_ref, w_ref, aglu_ref, q_ref, k_ref, logf_ref, v_ref,
                    og_ref, sga_ref, sgb_ref, *, tiles_per_batch):
    d = D_MODEL
    h = _residual_tile(x_ref, meta_ref, pl.program_id(0) % tiles_per_batch == 0)
    ms = jnp.mean(h * h, axis=-1, keepdims=True)
    u = (h * lax.rsqrt(ms + RMS_EPS) * g_ref[...]).astype(BF16)

    def proj(j):
        return jnp.dot(u, w_ref[:, j * d:(j + 1) * d], preferred_element_type=F32)

    aglu_ref[...] = (proj(0) * _sigmoid(proj(1))).astype(aglu_ref.dtype)
    zq = proj(2)
    q_ref[...] = (zq * _sigmoid(zq)).astype(q_ref.dtype)
    lbl = lbl_ref[...]
    e = jnp.exp(lbl - jnp.max(lbl, axis=0, keepdims=True))
    lb = e[0:1, :] / jnp.sum(e, axis=0, keepdims=True)
    s = _sigmoid(proj(3))
    logf_ref[...] = jnp.log(lb + (1.0 - lb) * s)
    k_ref[...] = ((1.0 - lb) * (1.0 - s)).astype(k_ref.dtype)
    v_ref[...] = proj(4).astype(v_ref.dtype)
    zg = proj(5)
    og_ref[...] = (zg * _sigmoid(zg)).astype(og_ref.dtype)
    sga_ref[...] = _sigmoid(proj(6)).astype(sga_ref.dtype)
    sgb_ref[...] = _sigmoid(proj(7)).astype(sgb_ref.dtype)


def _in_proj(x2, meta, seq, g_mix, lb_logits, w_in_bf16):
    d = x2.shape[1]
    n_batch = x2.shape[0] // seq
    tm = IN_TILE
    tpb = (seq + CHUNK) // tm
    tp = n_batch * (seq + CHUNK)
    assert tpb * tm == seq + CHUNK and tm > CHUNK
    row = pl.BlockSpec((tm, d), lambda i: (i, 0))
    out_dtypes = [BF16, BF16, BF16, F32, BF16, BF16, BF16, BF16]
    return pl.pallas_call(
        functools.partial(_in_proj_kernel, tiles_per_batch=tpb),
        grid=(tp // tm,),
        in_specs=[_residual_spec(tm, d, seq, tpb, tp // tm), _const_spec(meta.shape), _const_spec((1, d)),
                  _const_spec(lb_logits.shape),
                  pl.BlockSpec(w_in_bf16.shape, lambda i: (0, 0), pipeline_mode=pl.Buffered(1))],
        out_specs=[row] * 8,
        out_shape=[jax.ShapeDtypeStruct((tp, d), dt) for dt in out_dtypes],
        compiler_params=_cparams(("parallel",)),
        name="in_proj",
    )(x2, meta, g_mix, lb_logits, w_in_bf16)


def _conv_kernel(halo_ref, cur_ref, wdw_ref, bdw_ref, lng_ref, lnb_ref, wout_ref, bout_ref, sga_ref,
                 out_ref, win_ref, acc_ref, shift_ref):
    tm = cur_ref.shape[0]
    win_ref[0:HALO, :] = halo_ref[...].astype(F32)
    win_ref[HALO:, :] = cur_ref[...].astype(F32)
    first_tap = HALO - (CONV_WIDTH - 1)
    for c in range(D_MODEL // LANES):
        lanes = slice(c * LANES, (c + 1) * LANES)
        for s in range(SUBLANES):
            n_rows = tm + HALO - (SUBLANES if s else 0)
            shift_ref[s, 0:n_rows, :] = win_ref[pl.ds(s, n_rows), lanes]
        w_c = wdw_ref[:, lanes]
        bias = jnp.broadcast_to(bdw_ref[:, lanes], (CONV_ROWS, LANES))

        def chunk(r, carry, lanes=lanes, w_c=w_c, bias=bias):
            r0 = pl.multiple_of(r * CONV_ROWS, CONV_ROWS)
            acc = bias
            for j in range(CONV_WIDTH):
                off = first_tap + j
                rows = pl.ds(r0 + off - off % SUBLANES, CONV_ROWS)
                acc = acc + w_c[j:j + 1, :] * shift_ref[off % SUBLANES, rows, :]
            acc_ref[pl.ds(r0, CONV_ROWS), lanes] = acc
            return carry

        lax.fori_loop(0, tm // CONV_ROWS, chunk, 0)
    a = acc_ref[...]
    mu = jnp.mean(a, axis=-1, keepdims=True)
    ac = a - mu
    var = jnp.mean(ac * ac, axis=-1, keepdims=True)
    y = ac * lax.rsqrt(var + LN_EPS) * lng_ref[...] + lnb_ref[...]
    y = y * _sigmoid(y)
    o = jnp.dot(y.astype(BF16), wout_ref[...], preferred_element_type=F32) + bout_ref[...]
    out_ref[...] = (sga_ref[...].astype(F32) * o).astype(out_ref.dtype)


def _conv_branch(aglu, w_dw, b_dw, ln_g, ln_b, w_out_bf16, b_out, sga):
    tp, d = aglu.shape
    tm = ROW_TILE
    per = tm // HALO
    row = pl.BlockSpec((tm, d), lambda i: (i, 0))
    halo = pl.BlockSpec((HALO, d), lambda i: (jnp.maximum(i * per - 1, 0), 0))
    vec = _const_spec((1, d))
    return pl.pallas_call(
        _conv_kernel,
        grid=(tp // tm,),
        in_specs=[halo, row, _const_spec(w_dw.shape), vec, vec, vec, _const_spec((d, d)), vec, row],
        out_specs=row,
        out_shape=jax.ShapeDtypeStruct((tp, d), BF16),
        scratch_shapes=[pltpu.VMEM((tm + HALO, d), F32), pltpu.VMEM((tm, d), F32),
                        pltpu.VMEM((SUBLANES, tm + HALO, LANES), F32)],
        compiler_params=_cparams(("parallel",)),
        name="conv_branch",
    )(aglu, aglu, w_dw, b_dw, ln_g, ln_b, w_out_bf16, b_out, sga)


def _split_bf16(x):
    hi = x.astype(BF16)
    return hi, (x - hi.astype(F32)).astype(BF16)


def _nt_dot(a, b):
    return lax.dot_general(a, b, (((1,), (1,)), ((), ())), preferred_element_type=F32)


def _tn_dot(a, b):
    return lax.dot_general(a, b, (((0,), (0,)), ((), ())), preferred_element_type=F32)


def _scan_kernel(q_ref, k_ref, v_ref, lf_ref, og_ref, gon_ref, o_ref, st_ref):
    @pl.when(pl.program_id(1) == 0)
    def _():
        st_ref[...] = jnp.zeros_like(st_ref)

    c = CHUNK
    row = lax.broadcasted_iota(jnp.int32, (c, c), 0)
    col = lax.broadcasted_iota(jnp.int32, (c, c), 1)
    causal = col <= row
    diag_mask = jnp.logical_and(causal, row // SUB_BLOCK == col // SUB_BLOCK)
    tri = causal.astype(BF16)

    def bcast_rows(rows):
        return jnp.concatenate([jnp.broadcast_to(r, (SUB_BLOCK, HEAD_DIM)) for r in rows], axis=0)

    heads = [slice(h * HEAD_DIM, (h + 1) * HEAD_DIM) for h in range(N_HEADS)]
    chunks = [slice(ci * c, (ci + 1) * c) for ci in range(SCAN_CHUNKS)]

    bcums = []
    for cs in chunks:
        lf = lf_ref[cs, :]
        lf_hi = lf.astype(BF16)
        lf_mid, lf_lo = _split_bf16(lf - lf_hi.astype(F32))
        bcums.append(jnp.dot(tri, lf_hi, preferred_element_type=F32)
                     + (jnp.dot(tri, lf_mid, preferred_element_type=F32)
                        + jnp.dot(tri, lf_lo, preferred_element_type=F32)))
    operands = {}
    for ci, cs in enumerate(chunks):
        for h, hs in enumerate(heads):
            b = bcums[ci][:, hs]
            q = q_ref[cs, hs].astype(F32)
            k = k_ref[cs, hs].astype(F32)
            ends = [b[i * SUB_BLOCK + SUB_BLOCK - 1:i * SUB_BLOCK + SUB_BLOCK, :] for i in range(N_SUB)]
            mids = [b[i * SUB_BLOCK + SUB_BLOCK // 2 - 1:i * SUB_BLOCK + SUB_BLOCK // 2, :] for i in range(N_SUB)]
            mid_full = bcast_rows(mids)
            b_last = ends[-1]
            zero_row = jnp.zeros_like(b_last)
            qm = q * jnp.exp(b - mid_full)
            km = k * jnp.exp(mid_full - b)
            q_parts, k_parts = [], []
            for j in range(N_SUB - 1):
                q_rows = [jnp.exp(mids[i] - ends[j]) if i > j else zero_row for i in range(N_SUB)]
                k_rows = [jnp.exp(ends[j] - mids[j]) if i == j else zero_row for i in range(N_SUB)]
                q_parts.append((qm * bcast_rows(q_rows)).astype(BF16))
                k_parts.append((km * bcast_rows(k_rows)).astype(BF16))
            q_in = (qm * bcast_rows([jnp.exp(m) for m in mids])).astype(BF16)
            k_out = (km * bcast_rows([jnp.exp(b_last - m) for m in mids])).astype(BF16)
            operands[ci, h] = (qm.astype(BF16), km.astype(BF16), jnp.concatenate(q_parts, axis=1),
                               jnp.concatenate(k_parts, axis=1), q_in, k_out, jnp.exp(b_last))
    products = {}
    for ci, cs in enumerate(chunks):
        for h, hs in enumerate(heads):
            qm, km, q_cat, k_cat, _, k_out, _ = operands[ci, h]
            products[ci, h] = (_nt_dot(qm, km), _nt_dot(q_cat, k_cat), _tn_dot(v_ref[cs, hs], k_out))
    within = {}
    for ci, cs in enumerate(chunks):
        for h, hs in enumerate(heads):
            same_block, earlier_blocks, _ = products[ci, h]
            scores = jnp.where(diag_mask, same_block, 0.0) + earlier_blocks
            within[ci, h] = jnp.dot(scores.astype(BF16), v_ref[cs, hs], preferred_element_type=F32)
    states = [st_ref[h] for h in range(N_HEADS)]
    outs = {}
    for ci in range(SCAN_CHUNKS):
        for h in range(N_HEADS):
            outs[ci, h] = within[ci, h] + _nt_dot(operands[ci, h][4], states[h].astype(BF16))
            states[h] = states[h] * operands[ci, h][6] + products[ci, h][2]
    for h in range(N_HEADS):
        st_ref[h] = states[h]
    for ci, cs in enumerate(chunks):
        for h, hs in enumerate(heads):
            o = outs[ci, h]
            ms = jnp.mean(o * o, axis=-1, keepdims=True)
            on = o * lax.rsqrt(ms + RMS_EPS) * gon_ref[:, hs]
            o_ref[cs, hs] = (on * og_ref[cs, hs].astype(F32)).astype(o_ref.dtype)


def _hgrn2_scan(q, k, v, logf, og, g_onorm, n_batch):
    tp, d = q.shape
    rows = SCAN_CHUNKS * CHUNK
    n_chunks = tp // n_batch // rows
    assert n_chunks * rows * n_batch == tp
    blk = pl.BlockSpec((rows, d), lambda b, c: (b * n_chunks + c, 0))
    return pl.pallas_call(
        _scan_kernel,
        grid=(n_batch, n_chunks),
        in_specs=[blk, blk, blk, blk, blk, pl.BlockSpec((1, d), lambda b, c: (0, 0))],
        out_specs=blk,
        out_shape=jax.ShapeDtypeStruct((tp, d), BF16),
        scratch_shapes=[pltpu.VMEM((N_HEADS, HEAD_DIM, HEAD_DIM), F32)],
        compiler_params=_cparams(("arbitrary", "arbitrary")),
        name="hgrn2_scan",
    )(q, k, v, logf, og, g_onorm)


def _merge_route_kernel(x_ref, meta_ref, on_ref, ap_ref, sgb_ref, wrnn_ref, wo_ref, gffn_ref, wr_ref, br_ref,
                        h1_ref, u2_ref, info_ref, cnt_ref, tri_ref, carry_ref, colcnt_ref, logits_ref, hres_ref,
                        *, tiles_per_batch, n_tiles):
    tm = on_ref.shape[0]
    i = pl.program_id(0)
    hres_ref[...] = _residual_tile(x_ref, meta_ref, jnp.minimum(i, n_tiles - 1) % tiles_per_batch == 0)

    @pl.when(i == 0)
    def _():
        r_i = lax.broadcasted_iota(jnp.int32, (tm, tm), 0)
        c_i = lax.broadcasted_iota(jnp.int32, (tm, tm), 1)
        tri_ref[...] = (c_i < r_i).astype(BF16)
        carry_ref[...] = jnp.zeros_like(carry_ref)
        colcnt_ref[...] = jnp.zeros_like(colcnt_ref)
        logits_ref[...] = jnp.zeros_like(logits_ref)

    routed = i > 0
    lane = lax.broadcasted_iota(jnp.int32, (tm, LANES), 1)
    cur = jnp.where(lane < N_EXPERTS, logits_ref[...], NEG_BIG)
    vals, idxs, sels = [], [], []

    def topk_round(cur):
        m = jnp.max(cur, axis=-1, keepdims=True)
        idx = jnp.min(jnp.where(cur == m, lane, LANES), axis=-1, keepdims=True)
        sel = lane == idx
        vals.append(m)
        idxs.append(idx)
        sels.append(sel)
        return jnp.where(sel, 2.0 * NEG_BIG, cur)

    w_hi, w_lo = _split_bf16(wr_ref[...])
    part = tm // MERGE_PARTS
    rows = [slice(p * part, (p + 1) * part) for p in range(MERGE_PARTS)]
    rs = [jnp.dot(on_ref[rw, :], wrnn_ref[...], preferred_element_type=F32) for rw in rows]
    cur = topk_round(cur)
    ys = [(ap_ref[rw, :].astype(F32) + sgb_ref[rw, :].astype(F32) * r).astype(BF16) for rw, r in zip(rows, rs)]
    h1s = [hres_ref[rw, :] + jnp.dot(y, wo_ref[...], preferred_element_type=F32) for rw, y in zip(rows, ys)]
    cur = topk_round(cur)
    u2s = []
    for rw, h1 in zip(rows, h1s):
        h1_ref[rw, :] = h1
        ms = jnp.mean(h1 * h1, axis=-1, keepdims=True)
        u2 = h1 * lax.rsqrt(ms + RMS_EPS) * gffn_ref[...]
        u2_ref[rw, :] = _pack_rows(u2)
        u2s.append(u2)
    cur = topk_round(cur)
    new_logits = []
    for u2 in u2s:
        u_hi, u_lo = _split_bf16(u2)
        new_logits.append(jnp.dot(u_hi, w_hi, preferred_element_type=F32)
                          + (jnp.dot(u_lo, w_hi, preferred_element_type=F32)
                             + jnp.dot(u_hi, w_lo, preferred_element_type=F32))
                          + br_ref[...])
    cur = topk_round(cur)
    assert len(vals) == TOP_K
    exps = [jnp.exp(vk - vals[0]) for vk in vals]
    den = exps[0] + exps[1] + exps[2] + exps[3]
    onehot = jnp.logical_or(jnp.logical_or(sels[0], sels[1]), jnp.logical_or(sels[2], sels[3]))
    onehot = jnp.logical_and(onehot, routed)
    onehot_bf = onehot.astype(BF16)
    rank_all = jnp.dot(tri_ref[...], onehot_bf, preferred_element_type=F32) + carry_ref[...]
    carry_ref[...] += jnp.sum(onehot.astype(F32), axis=0, keepdims=True)
    colcnt_ref[...] += _tn_dot(onehot_bf, jnp.ones((tm, LANES), BF16))
    info = jnp.zeros((tm, LANES), F32)
    for kk in range(TOP_K):
        rank_k = jnp.sum(jnp.where(sels[kk], rank_all, 0.0), axis=-1, keepdims=True)
        info = jnp.where(lane == kk, exps[kk] / den, info)
        info = jnp.where(lane == TOP_K + kk, idxs[kk].astype(F32), info)
        info = jnp.where(lane == 2 * TOP_K + kk, rank_k, info)
    info_ref[...] = info
    cnt_ref[...] = colcnt_ref[...]
    logits_ref[...] = jnp.concatenate(new_logits, axis=0)


def _merge_route(x2, meta, seq, on, ap, sgb, w_rnn_bf16, w_o_bf16, g_ffn, w_router_pad, b_router_pad):
    tp, d = on.shape
    tm = ROW_TILE
    n_tiles = tp // tm
    tpb = (seq + CHUNK) // tm
    assert tpb * tm == seq + CHUNK
    this_tile = lambda i: (jnp.minimum(i, n_tiles - 1), 0)
    prev_tile = lambda i: (jnp.maximum(i - 1, 0), 0)
    row = pl.BlockSpec((tm, d), this_tile)
    vec = _const_spec((1, d))
    return pl.pallas_call(
        functools.partial(_merge_route_kernel, tiles_per_batch=tpb, n_tiles=n_tiles),
        grid=(n_tiles + 1,),
        in_specs=[_residual_spec(tm, d, seq, tpb, n_tiles), _const_spec(meta.shape), row, row, row,
                  _const_spec((d, d)), _const_spec((d, d)), vec, _const_spec((d, LANES)), _const_spec((1, LANES))],
        out_specs=[row, pl.BlockSpec((tm, PACK_WORDS), this_tile),
                   pl.BlockSpec((tm, LANES), prev_tile), _const_spec((LANES, LANES))],
        out_shape=[jax.ShapeDtypeStruct((tp, d), F32), jax.ShapeDtypeStruct((tp, PACK_WORDS), jnp.uint32),
                   jax.ShapeDtypeStruct((tp, LANES), F32), jax.ShapeDtypeStruct((LANES, LANES), F32)],
        scratch_shapes=[pltpu.VMEM((tm, tm), BF16), pltpu.VMEM((1, LANES), F32),
                        pltpu.VMEM((LANES, LANES), F32), pltpu.VMEM((tm, LANES), F32), pltpu.VMEM((tm, d), F32)],
        compiler_params=_cparams(("arbitrary",)),
        name="merge_route",
    )(x2, meta, on, ap, sgb, w_rnn_bf16, w_o_bf16, g_ffn, w_router_pad, b_router_pad)


def _slots_kernel(info_ref, cnt_ref, dest_ref, te_ref, pstart_ref):
    tm = info_ref.shape[0]
    n_tile_lanes = te_ref.shape[1]

    @pl.when(pl.program_id(0) == 0)
    def _():
        r_i = lax.broadcasted_iota(jnp.int32, (LANES, LANES), 0)
        c_i = lax.broadcasted_iota(jnp.int32, (LANES, LANES), 1)
        cnt = cnt_ref[...]
        tiles = jnp.floor((cnt + (EXPERT_TILE - 1)) / EXPERT_TILE)
        pend_col = jnp.dot((c_i <= r_i).astype(F32), tiles, preferred_element_type=F32,
                           precision=lax.Precision.HIGHEST)
        pstart_ref[...] = _tn_dot_f32(tiles, (r_i < c_i).astype(F32))[0:SUBLANES, :]
        tile_id = lax.broadcasted_iota(jnp.int32, (LANES, n_tile_lanes), 1).astype(F32)
        exp_id = lax.broadcasted_iota(jnp.int32, (LANES, n_tile_lanes), 0)
        pend_wide = jnp.concatenate([pend_col] * (n_tile_lanes // LANES), axis=1)
        below = jnp.logical_and(pend_wide <= tile_id, exp_id < N_EXPERTS)
        te = jnp.minimum(jnp.sum(below.astype(F32), axis=0, keepdims=True), N_EXPERTS - 1.0)
        n_active = pend_col[N_EXPERTS - 1:N_EXPERTS, 0:1]
        owner = exp_id.astype(F32) == te
        cnt_wide = jnp.concatenate([cnt] * (n_tile_lanes // LANES), axis=1)
        first_tile = pend_wide - jnp.concatenate([tiles] * (n_tile_lanes // LANES), axis=1)
        left = jnp.sum(jnp.where(owner, cnt_wide - (tile_id - first_tile) * EXPERT_TILE, 0.0), axis=0, keepdims=True)
        valid = jnp.clip(left, 0.0, float(EXPERT_TILE))
        sub_t = lax.broadcasted_iota(jnp.int32, (SUBLANES, n_tile_lanes), 0)
        out = jnp.where(sub_t == 0, jnp.broadcast_to(te, (SUBLANES, n_tile_lanes)),
                        jnp.where(sub_t == 1, jnp.broadcast_to(n_active, (SUBLANES, n_tile_lanes)),
                                  jnp.broadcast_to(valid, (SUBLANES, n_tile_lanes))))
        te_ref[...] = out.astype(jnp.int32)

    info = info_ref[...]
    pstart_row = pstart_ref[0:1, :]
    lane = lax.broadcasted_iota(jnp.int32, (tm, LANES), 1)
    dest = jnp.zeros((tm, LANES), F32)
    for kk in range(TOP_K):
        idx_k = info[:, TOP_K + kk:TOP_K + kk + 1].astype(jnp.int32)
        start_k = jnp.sum(jnp.where(lane == idx_k, pstart_row, 0.0), axis=-1, keepdims=True)
        slot_k = start_k * EXPERT_TILE + info[:, 2 * TOP_K + kk:2 * TOP_K + kk + 1]
        dest = jnp.where(lane == kk, slot_k, dest)
    dest_ref[...] = dest.astype(jnp.int32)


def _tn_dot_f32(a, b):
    return lax.dot_general(a, b, (((0,), (0,)), ((), ())), preferred_element_type=F32,
                           precision=lax.Precision.HIGHEST)


def _slots(info, cnt, n_tiles_max):
    tp = info.shape[0]
    tm = tp // SLOTS_STEPS
    n_tile_lanes = -(-n_tiles_max // LANES) * LANES
    return pl.pallas_call(
        _slots_kernel,
        grid=(tp // tm,),
        in_specs=[pl.BlockSpec((tm, LANES), lambda i: (i, 0)), _const_spec((LANES, LANES))],
        out_specs=[pl.BlockSpec((tm, LANES), lambda i: (i, 0)), _const_spec((8, n_tile_lanes))],
        out_shape=[jax.ShapeDtypeStruct((tp, LANES), jnp.int32),
                   jax.ShapeDtypeStruct((8, n_tile_lanes), jnp.int32)],
        scratch_shapes=[pltpu.VMEM((SUBLANES, LANES), F32)],
        compiler_params=_cparams(("arbitrary",)),
        name="slots",
    )(info, cnt)


def _sc_workers():
    sc = plsc.get_sparse_core_info()
    return sc.num_cores, sc.num_cores * sc.num_subcores


def _sc_token_chunks(tp, chunk, body):
    n_cores, n_workers = _sc_workers()
    per_worker = tp // n_workers
    assert per_worker * n_workers == tp and per_worker % chunk == 0 and chunk % SUBLANES == 0
    base = (lax.axis_index("s") * n_cores + lax.axis_index("c")) * per_worker

    @pl.loop(0, per_worker // chunk)
    def _(ci):
        body(pl.multiple_of(base + ci * chunk, SUBLANES))


def _sc_kernel(out_type, chunk, n_row_bufs):
    return functools.partial(
        pl.kernel, mesh=plsc.VectorSubcoreMesh(core_axis_name="c", subcore_axis_name="s"), out_type=out_type,
        scratch_types=[pltpu.VMEM((TOP_K, chunk), jnp.int32), pltpu.VMEM((n_row_bufs, chunk, PACK_WORDS), jnp.uint32),
                       pltpu.SemaphoreType.DMA, pltpu.SemaphoreType.DMA])


def _wait_all(copies):
    for c in copies:
        c.wait()


def _dispatch(dest_flat, u2p, n_slots):
    tp = u2p.shape[0]
    ch = SC_SCATTER_CHUNK

    @_sc_kernel(jax.ShapeDtypeStruct((n_slots, PACK_WORDS), jnp.uint32), ch, 1)
    def scatter_rows(u2_hbm, dest_hbm, xs_hbm, idx_v, rows_v, sem_a, sem_b):
        def chunk(off):
            loads = [pltpu.async_copy(u2_hbm.at[pl.ds(off, ch)], rows_v.at[0], sem_a)]
            loads += [pltpu.async_copy(dest_hbm.at[pl.ds(kk * tp + off, ch)], idx_v.at[kk], sem_a)
                      for kk in range(TOP_K)]
            _wait_all(loads)
            _wait_all([pltpu.async_copy(rows_v.at[0], xs_hbm.at[idx_v.at[kk]], sem_b) for kk in range(TOP_K)])

        _sc_token_chunks(tp, ch, chunk)

    return scatter_rows(u2p, dest_flat)


def _gather_expert_rows(dest_flat, ys, tok0, n_tok):
    tp = dest_flat.shape[0] // TOP_K
    per_worker = n_tok // _sc_workers()[1]
    ch = max(c for c in range(SUBLANES, SC_GATHER_CHUNK + 1, SUBLANES) if per_worker % c == 0)

    @_sc_kernel(jax.ShapeDtypeStruct((TOP_K, n_tok, PACK_WORDS), jnp.uint32), ch, TOP_K)
    def gather_rows(ys_hbm, dest_hbm, out_hbm, idx_v, rows_v, sem_a, sem_b):
        def chunk(off):
            _wait_all([pltpu.async_copy(dest_hbm.at[pl.ds(kk * tp + tok0 + off, ch)], idx_v.at[kk], sem_a)
                       for kk in range(TOP_K)])
            _wait_all([pltpu.async_copy(ys_hbm.at[idx_v.at[kk]], rows_v.at[kk], sem_b) for kk in range(TOP_K)])
            _wait_all([pltpu.async_copy(rows_v.at[kk], out_hbm.at[kk, pl.ds(off, ch)], sem_a)
                       for kk in range(TOP_K)])

        _sc_token_chunks(n_tok, ch, chunk)

    return gather_rows(ys, dest_flat)


def _expert_kernel(te_ref, nact_ref, valid_ref, x_ref, wgu_hbm, bgu_ref, wdn_hbm, bdn_ref, y_ref,
                   wgu_buf, wdn_buf, slot_ref, sem):
    i = pl.program_id(0)
    n_active = nact_ref[0]
    last_tile = pl.num_programs(0) - 1

    def fetch(expert, slot):
        return (pltpu.make_async_copy(wgu_hbm.at[expert], wgu_buf.at[slot], sem.at[slot, 0]),
                pltpu.make_async_copy(wdn_hbm.at[expert], wdn_buf.at[slot], sem.at[slot, 1]))

    @pl.when(i == 0)
    def _():
        slot_ref[0] = 1
        for copy in fetch(te_ref[0], 0):
            copy.start()

    @pl.when(i < n_active)
    def _():
        expert = te_ref[i]
        first_tile_of_expert = jnp.logical_or(i == 0, expert != te_ref[jnp.maximum(i - 1, 0)])

        @pl.when(first_tile_of_expert)
        def _():
            slot = 1 - slot_ref[0]
            slot_ref[0] = slot
            for copy in fetch(expert, slot):
                copy.wait()
            nxt = lax.while_loop(
                lambda j: jnp.logical_and(j < n_active, te_ref[jnp.minimum(j, last_tile)] == expert),
                lambda j: j + 1, i + 1)

            @pl.when(nxt < n_active)
            def _():
                for copy in fetch(te_ref[jnp.minimum(nxt, last_tile)], 1 - slot):
                    copy.start()

        slot = slot_ref[0]

        def mlp(rows):
            x = _unpack_rows(x_ref[0:rows, :]).astype(BF16)
            gu = jnp.dot(x, wgu_buf[slot].astype(BF16), preferred_element_type=F32) + bgu_ref[0]
            gate = jnp.minimum(gu[:, :D_FF], SWIGLU_LIMIT)
            up = jnp.clip(gu[:, D_FF:], -SWIGLU_LIMIT, SWIGLU_LIMIT)
            hdn = (up + 1.0) * (gate * _sigmoid(SWIGLU_ALPHA * gate))
            y = jnp.dot(hdn.astype(BF16), wdn_buf[slot].astype(BF16), preferred_element_type=F32) + bdn_ref[0]
            y_ref[0:rows, :] = _pack_rows(y)

        half = EXPERT_TILE // 2
        half_empty = valid_ref[i] <= half
        pl.when(jnp.logical_not(half_empty))(lambda: mlp(EXPERT_TILE))

        @pl.when(half_empty)
        def _():
            mlp(half)
            y_ref[half:, :] = jnp.zeros((EXPERT_TILE - half, PACK_WORDS), y_ref.dtype)

    @pl.when(i >= n_active)
    def _():
        y_ref[...] = jnp.zeros_like(y_ref)


def _expert_mlp(te, nact, valid, xs, w_gu, b_gu, w_dn, b_dn):
    d = D_MODEL
    n_tiles = xs.shape[0] // EXPERT_TILE

    def tile_map(i, te_ref, nact_ref, valid_ref):
        return (jnp.minimum(i, nact_ref[0] - 1), 0)

    def exp_map(i, te_ref, nact_ref, valid_ref):
        return (te_ref[jnp.minimum(i, nact_ref[0] - 1)], 0, 0)

    grid_spec = pltpu.PrefetchScalarGridSpec(
        num_scalar_prefetch=3,
        grid=(n_tiles,),
        in_specs=[pl.BlockSpec((EXPERT_TILE, PACK_WORDS), tile_map),
                  pl.BlockSpec(memory_space=pl.ANY),
                  pl.BlockSpec((1, 1, 2 * D_FF), exp_map),
                  pl.BlockSpec(memory_space=pl.ANY),
                  pl.BlockSpec((1, 1, d), exp_map)],
        out_specs=pl.BlockSpec((EXPERT_TILE, PACK_WORDS), lambda i, te_ref, nact_ref, valid_ref: (i, 0)),
        scratch_shapes=[pltpu.VMEM((2, d, 2 * D_FF), F32), pltpu.VMEM((2, D_FF, d), F32),
                        pltpu.SMEM((1,), jnp.int32), pltpu.SemaphoreType.DMA((2, 2))],
    )
    return pl.pallas_call(
        _expert_kernel,
        grid_spec=grid_spec,
        out_shape=jax.ShapeDtypeStruct(xs.shape, jnp.uint32),
        compiler_params=_cparams(("arbitrary",)),
        name="expert_mlp",
    )(te, nact, valid, xs, w_gu, b_gu, w_dn, b_dn)


def _combine_kernel(gfin_ref, *refs):
    out_ref = refs[-1]
    for s in range(COMBINE_SUB):
        h1_ref, info_ref, yg_ref = refs[3 * s:3 * s + 3]
        info = info_ref[...]
        h2 = h1_ref[...]
        for kk in range(TOP_K):
            h2 = h2 + info[:, kk:kk + 1] * _unpack_rows(yg_ref[kk])
        ms = jnp.mean(h2 * h2, axis=-1, keepdims=True)
        out_ref[s * CHUNK:(s + 1) * CHUNK, :] = h2 * lax.rsqrt(ms + RMS_EPS) * gfin_ref[...]


def _combine(h1, info, g_final, yg, batch, n_batch, seq, yg_chunk0, out_prev):
    d = h1.shape[1]
    chunks = seq // CHUNK
    steps = chunks // COMBINE_SUB

    in_specs = [pl.BlockSpec((1, d), lambda j: (0, 0))]
    operands = [g_final]
    for s in range(COMBINE_SUB):
        padded = lambda j, s=s: batch * (chunks + 1) + 1 + j * COMBINE_SUB + s
        local = lambda j, s=s: yg_chunk0 + j * COMBINE_SUB + s
        in_specs += [pl.BlockSpec((CHUNK, d), lambda j, f=padded: (f(j), 0)),
                     pl.BlockSpec((CHUNK, LANES), lambda j, f=padded: (f(j), 0)),
                     pl.BlockSpec((TOP_K, CHUNK, PACK_WORDS), lambda j, f=local: (0, f(j), 0))]
        operands += [h1, info, yg]
    aliases = {}
    kernel_fn = _combine_kernel
    if out_prev is not None:
        in_specs.append(pl.BlockSpec(memory_space=pl.ANY))
        operands.append(out_prev)
        aliases = {len(operands) - 1: 0}
        kernel_fn = lambda gfin_ref, *refs: _combine_kernel(gfin_ref, *refs[:-2], refs[-1])
    return pl.pallas_call(
        kernel_fn,
        grid=(steps,),
        in_specs=in_specs,
        out_specs=pl.BlockSpec((COMBINE_SUB * CHUNK, d), lambda j: (batch * steps + j, 0)),
        out_shape=jax.ShapeDtypeStruct((n_batch * seq, d), F32),
        input_output_aliases=aliases,
        compiler_params=_cparams(("parallel",)),
        name="combine",
    )(*operands)


def kernel(x, meta_tokens, lb_logits, g_mix, w_in, w_dw, b_dw, ln_g, ln_b, w_conv_out, b_conv_out,
           g_onorm, w_rnn_out, w_o, g_ffn, w_router, b_router, w_gate_up, b_gate_up, w_down, b_down,
           g_final):
    n_batch, seq, d = x.shape
    assert d == D_MODEL and w_in.shape[0] == 1, "single-layer block with D_MODEL features"
    assert seq % (CHUNK * COMBINE_SUB) == 0
    seq_pad = CHUNK + seq
    tp = n_batch * seq_pad
    assert tp % ROW_TILE == 0

    x2 = x.reshape(n_batch * seq, d)
    meta = meta_tokens.astype(x.dtype)
    vec = lambda a: a.reshape(1, -1).astype(F32)

    aglu, q, k, logf, v, og, sga, sgb = _in_proj(x2, meta, seq, vec(g_mix[0]), lb_logits.astype(F32),
                                                 w_in[0].astype(BF16))
    ap = _conv_branch(aglu, w_dw[0].astype(F32), vec(b_dw[0]), vec(ln_g[0]), vec(ln_b[0]),
                      w_conv_out[0].astype(BF16), vec(b_conv_out[0]), sga)
    on = _hgrn2_scan(q, k, v, logf, og, vec(g_onorm[0]), n_batch)
    w_router_pad = jnp.pad(w_router[0].astype(F32), ((0, 0), (0, LANES - N_EXPERTS)))
    b_router_pad = jnp.pad(vec(b_router[0]), ((0, 0), (0, LANES - N_EXPERTS)))
    h1, u2p, info, cnt = _merge_route(x2, meta, seq, on, ap, sgb, w_rnn_out[0].astype(BF16), w_o[0].astype(BF16),
                                      vec(g_ffn[0]), w_router_pad, b_router_pad)

    n_tiles_max = -(-(tp * TOP_K + N_EXPERTS * (EXPERT_TILE - 1)) // EXPERT_TILE)
    dest, te = _slots(info, cnt, n_tiles_max)
    dest_flat = dest[:, :TOP_K].T.reshape(-1)
    xs = _dispatch(dest_flat, u2p, n_tiles_max * EXPERT_TILE)
    ys = _expert_mlp(te[0, :n_tiles_max], te[1, :1], te[2, :n_tiles_max], xs, w_gate_up[0],
                     b_gate_up[0].reshape(N_EXPERTS, 1, -1).astype(F32), w_down[0],
                     b_down[0].reshape(N_EXPERTS, 1, -1).astype(F32))
    grain = _sc_workers()[1] * SUBLANES
    starts = [-(-(b * seq_pad) // grain) * grain for b in range(n_batch)] + [tp]
    out = None
    for b in range(n_batch):
        first_row = b * seq_pad + CHUNK
        assert starts[b] <= first_row and (first_row - starts[b]) % CHUNK == 0 and starts[b + 1] >= (b + 1) * seq_pad
        yg = _gather_expert_rows(dest_flat, ys, starts[b], starts[b + 1] - starts[b])
        out = _combine(h1, info, vec(g_final), yg, b, n_batch, seq, (first_row - starts[b]) // CHUNK, out)
    return out.reshape(n_batch, seq, d)
```

```python
import functools

import jax
import jax.numpy as jnp
from jax import lax
from jax.experimental import pallas as pl
from jax.experimental.pallas import tpu as pltpu
from jax.experimental.pallas import tpu_sc as plsc

F32 = jnp.float32
BF16 = jnp.bfloat16

D_MODEL = 1024
N_META = 16
CHUNK = 128
CHUNK_PAD = CHUNK - N_META
CONV_WIDTH = 31
HEAD_DIM = 128
N_HEADS = D_MODEL // HEAD_DIM
N_EXPERTS = 32
TOP_K = 4
D_FF = D_MODEL
SWIGLU_LIMIT = 7.0
SWIGLU_ALPHA = 1.702
RMS_EPS = 1e-6
LN_EPS = 1e-5

LANES = 128
SUBLANES = 8
SUB_BLOCK = 32
N_SUB = CHUNK // SUB_BLOCK
HALO = 32
ROW_TILE = 640
IN_TILE = 320
SCAN_CHUNKS = 5
MERGE_PARTS = 2
SLOTS_STEPS = 5
CONV_ROWS = 64
EXPERT_TILE = 512
SC_SCATTER_CHUNK = 104
SC_GATHER_CHUNK = 40
COMBINE_SUB = 4
NEG_BIG = -1e30
VMEM_LIMIT = 56 * 1024 * 1024


def _sigmoid(x):
    return 1.0 / (1.0 + jnp.exp(-x))


def _cparams(sem):
    return pltpu.CompilerParams(dimension_semantics=sem, vmem_limit_bytes=VMEM_LIMIT)


def _const_spec(shape):
    nd = len(shape)
    return pl.BlockSpec(shape, lambda *_: (0,) * nd)


PACK_WORDS = D_MODEL // 2
HIGH_HALF = 0xFFFF0000


def _pack_rows(x):
    lo = lax.bitcast_convert_type(x[:, :PACK_WORDS].astype(BF16).astype(F32), jnp.uint32)
    hi = lax.bitcast_convert_type(x[:, PACK_WORDS:].astype(BF16).astype(F32), jnp.uint32)
    return (lo >> 16) | (hi & jnp.uint32(HIGH_HALF))


def _unpack_rows(w):
    lo = lax.bitcast_convert_type(w << 16, F32)
    hi = lax.bitcast_convert_type(w & jnp.uint32(HIGH_HALF), F32)
    return jnp.concatenate([lo, hi], axis=1)


def _residual_tile(x_ref, meta_ref, is_first):
    x = x_ref[...]
    tm, d = x.shape
    prefix = jnp.concatenate([jnp.zeros((CHUNK_PAD, d), x.dtype), meta_ref[...]], axis=0)
    first = jnp.concatenate([prefix, x[:tm - CHUNK, :]], axis=0)
    return jnp.where(is_first, first, x)


def _residual_spec(tm, d, seq, tiles_per_batch, n_tiles):
    def start(i):
        i = jnp.minimum(i, n_tiles - 1)
        b, t = i // tiles_per_batch, i % tiles_per_batch
        return (pl.multiple_of(b * seq + jnp.maximum(t * tm - CHUNK, 0), SUBLANES), 0)
    return pl.BlockSpec((pl.Element(tm), pl.Element(d)), start)


def _in_proj_kernel(x_ref, meta_ref, g_ref, lbl_ref, w_ref, aglu_ref, q_ref, k_ref, logf_ref, v_ref,
                    og_ref, sga_ref, sgb_ref, *, tiles_per_batch):
    d = D_MODEL
    h = _residual_tile(x_ref, meta_ref, pl.program_id(0) % tiles_per_batch == 0)
    ms = jnp.mean(h * h, axis=-1, keepdims=True)
    u = (h * lax.rsqrt(ms + RMS_EPS) * g_ref[...]).astype(BF16)

    def proj(j):
        return jnp.dot(u, w_ref[:, j * d:(j + 1) * d], preferred_element_type=F32)

    aglu_ref[...] = (proj(0) * _sigmoid(proj(1))).astype(aglu_ref.dtype)
    zq = proj(2)
    q_ref[...] = (zq * _sigmoid(zq)).astype(q_ref.dtype)
    lbl = lbl_ref[...]
    e = jnp.exp(lbl - jnp.max(lbl, axis=0, keepdims=True))
    lb = e[0:1, :] / jnp.sum(e, axis=0, keepdims=True)
    s = _sigmoid(proj(3))
    logf_ref[...] = jnp.log(lb + (1.0 - lb) * s)
    k_ref[...] = ((1.0 - lb) * (1.0 - s)).astype(k_ref.dtype)
    v_ref[...] = proj(4).astype(v_ref.dtype)
    zg = proj(5)
    og_ref[...] = (zg * _sigmoid(zg)).astype(og_ref.dtype)
    sga_ref[...] = _sigmoid(proj(6)).astype(sga_ref.dtype)
    sgb_ref[...] = _sigmoid(proj(7)).astype(sgb_ref.dtype)


def _in_proj(x2, meta, seq, g_mix, lb_logits, w_in_bf16):
    d = x2.shape[1]
    n_batch = x2.shape[0] // seq
    tm = IN_TILE
    tpb = (seq + CHUNK) // tm
    tp = n_batch * (seq + CHUNK)
    assert tpb * tm == seq + CHUNK and tm > CHUNK
    row = pl.BlockSpec((tm, d), lambda i: (i, 0))
    out_dtypes = [BF16, BF16, BF16, F32, BF16, BF16, BF16, BF16]
    return pl.pallas_call(
        functools.partial(_in_proj_kernel, tiles_per_batch=tpb),
        grid=(tp // tm,),
        in_specs=[_residual_spec(tm, d, seq, tpb, tp // tm), _const_spec(meta.shape), _const_spec((1, d)),
                  _const_spec(lb_logits.shape),
                  pl.BlockSpec(w_in_bf16.shape, lambda i: (0, 0), pipeline_mode=pl.Buffered(1))],
        out_specs=[row] * 8,
        out_shape=[jax.ShapeDtypeStruct((tp, d), dt) for dt in out_dtypes],
        compiler_params=_cparams(("parallel",)),
        name="in_proj",
    )(x2, meta, g_mix, lb_logits, w_in_bf16)


def _conv_kernel(halo_ref, cur_ref, wdw_ref, bdw_ref, lng_ref, lnb_ref, wout_ref, bout_ref, sga_ref,
                 out_ref, win_ref, acc_ref, shift_ref):
    tm = cur_ref.shape[0]
    win_ref[0:HALO, :] = halo_ref[...].astype(F32)
    win_ref[HALO:, :] = cur_ref[...].astype(F32)
    first_tap = HALO - (CONV_WIDTH - 1)
    for c in range(D_MODEL // LANES):
        lanes = slice(c * LANES, (c + 1) * LANES)
        n_rows = tm + HALO - SUBLANES
        for s in range(1, SUBLANES):
            shift_ref[s - 1, 0:n_rows, :] = win_ref[pl.ds(s, n_rows), lanes]
        bias = jnp.broadcast_to(bdw_ref[:, lanes], (CONV_ROWS, LANES))
        groups = CONV_ROWS // SUBLANES

        def chunk(r, carry, c=c, lanes=lanes, bias=bias):
            r0 = pl.multiple_of(r * CONV_ROWS, CONV_ROWS)
            accs = [bias.reshape(groups, SUBLANES, LANES), None]
            for j in range(CONV_WIDTH):
                off = first_tap + j
                rows = pl.ds(r0 + off - off % SUBLANES, CONV_ROWS)
                w_j = wdw_ref[c, pl.ds(j, SUBLANES, stride=0), :]
                window = shift_ref[off % SUBLANES - 1, rows, :] if off % SUBLANES else win_ref[rows, lanes]
                term = w_j[None] * window.reshape(groups, SUBLANES, LANES)
                accs[j % 2] = term if accs[j % 2] is None else accs[j % 2] + term
            acc_ref[pl.ds(r0, CONV_ROWS), lanes] = (accs[0] + accs[1]).reshape(CONV_ROWS, LANES)
            return carry

        lax.fori_loop(0, tm // CONV_ROWS, chunk, 0)
    a = acc_ref[...]
    mu = jnp.mean(a, axis=-1, keepdims=True)
    ac = a - mu
    var = jnp.mean(ac * ac, axis=-1, keepdims=True)
    y = ac * lax.rsqrt(var + LN_EPS) * lng_ref[...] + lnb_ref[...]
    y = y * _sigmoid(y)
    o = jnp.dot(y.astype(BF16), wout_ref[...], preferred_element_type=F32) + bout_ref[...]
    out_ref[...] = (sga_ref[...].astype(F32) * o).astype(out_ref.dtype)


def _conv_branch(aglu, w_dw, b_dw, ln_g, ln_b, w_out_bf16, b_out, sga):
    tp, d = aglu.shape
    tm = ROW_TILE
    per = tm // HALO
    row = pl.BlockSpec((tm, d), lambda i: (i, 0))
    halo = pl.BlockSpec((HALO, d), lambda i: (jnp.maximum(i * per - 1, 0), 0))
    vec = _const_spec((1, d))
    return pl.pallas_call(
        _conv_kernel,
        grid=(tp // tm,),
        in_specs=[halo, row, _const_spec(w_dw.shape), vec, vec, vec, _const_spec((d, d)), vec, row],
        out_specs=row,
        out_shape=jax.ShapeDtypeStruct((tp, d), BF16),
        scratch_shapes=[pltpu.VMEM((tm + HALO, d), F32), pltpu.VMEM((tm, d), F32),
                        pltpu.VMEM((SUBLANES - 1, tm + HALO - SUBLANES, LANES), F32)],
        compiler_params=_cparams(("parallel",)),
        name="conv_branch",
    )(aglu, aglu, w_dw, b_dw, ln_g, ln_b, w_out_bf16, b_out, sga)


def _split_bf16(x):
    hi = x.astype(BF16)
    return hi, (x - hi.astype(F32)).astype(BF16)


def _nt_dot(a, b):
    return lax.dot_general(a, b, (((1,), (1,)), ((), ())), preferred_element_type=F32)


def _tn_dot(a, b):
    return lax.dot_general(a, b, (((0,), (0,)), ((), ())), preferred_element_type=F32)


def _scan_kernel(q_ref, k_ref, v_ref, lf_ref, og_ref, gon_ref, o_ref, st_ref):
    @pl.when(pl.program_id(1) == 0)
    def _():
        st_ref[...] = jnp.zeros_like(st_ref)

    c = CHUNK
    row = lax.broadcasted_iota(jnp.int32, (c, c), 0)
    col = lax.broadcasted_iota(jnp.int32, (c, c), 1)
    causal = col <= row
    diag_mask = jnp.logical_and(causal, row // SUB_BLOCK == col // SUB_BLOCK)
    tri = causal.astype(BF16)

    def bcast_rows(rows):
        return jnp.concatenate([jnp.broadcast_to(r, (SUB_BLOCK, HEAD_DIM)) for r in rows], axis=0)

    heads = [slice(h * HEAD_DIM, (h + 1) * HEAD_DIM) for h in range(N_HEADS)]
    chunks = [slice(ci * c, (ci + 1) * c) for ci in range(SCAN_CHUNKS)]

    bcums = []
    for cs in chunks:
        lf = lf_ref[cs, :]
        lf_hi = lf.astype(BF16)
        lf_mid, lf_lo = _split_bf16(lf - lf_hi.astype(F32))
        bcums.append(jnp.dot(tri, lf_hi, preferred_element_type=F32)
                     + (jnp.dot(tri, lf_mid, preferred_element_type=F32)
                        + jnp.dot(tri, lf_lo, preferred_element_type=F32)))
    operands = {}
    for ci, cs in enumerate(chunks):
        for h, hs in enumerate(heads):
            b = bcums[ci][:, hs]
            q = q_ref[cs, hs].astype(F32)
            k = k_ref[cs, hs].astype(F32)
            ends = [b[i * SUB_BLOCK + SUB_BLOCK - 1:i * SUB_BLOCK + SUB_BLOCK, :] for i in range(N_SUB)]
            mids = [b[i * SUB_BLOCK + SUB_BLOCK // 2 - 1:i * SUB_BLOCK + SUB_BLOCK // 2, :] for i in range(N_SUB)]
            mid_full = bcast_rows(mids)
            b_last = ends[-1]
            zero_row = jnp.zeros_like(b_last)
            qm = q * jnp.exp(b - mid_full)
            km = k * jnp.exp(mid_full - b)
            q_parts, k_parts = [], []
            for j in range(N_SUB - 1):
                q_rows = [jnp.exp(mids[i] - ends[j]) if i > j else zero_row for i in range(N_SUB)]
                k_rows = [jnp.exp(ends[j] - mids[j]) if i == j else zero_row for i in range(N_SUB)]
                q_parts.append((qm * bcast_rows(q_rows)).astype(BF16))
                k_parts.append((km * bcast_rows(k_rows)).astype(BF16))
            q_in = (qm * bcast_rows([jnp.exp(m) for m in mids])).astype(BF16)
            k_out = (km * bcast_rows([jnp.exp(b_last - m) for m in mids])).astype(BF16)
            operands[ci, h] = (qm.astype(BF16), km.astype(BF16), jnp.concatenate(q_parts, axis=1),
                               jnp.concatenate(k_parts, axis=1), q_in, k_out, jnp.exp(b_last))
    products = {}
    for ci, cs in enumerate(chunks):
        for h, hs in enumerate(heads):
            qm, km, q_cat, k_cat, _, k_out, _ = operands[ci, h]
            products[ci, h] = (_nt_dot(qm, km), _nt_dot(q_cat, k_cat), _tn_dot(v_ref[cs, hs], k_out))
    within = {}
    for ci, cs in enumerate(chunks):
        for h, hs in enumerate(heads):
            same_block, earlier_blocks, _ = products[ci, h]
            scores = jnp.where(diag_mask, same_block, 0.0) + earlier_blocks
            within[ci, h] = jnp.dot(scores.astype(BF16), v_ref[cs, hs], preferred_element_type=F32)
    states = [st_ref[h] for h in range(N_HEADS)]
    outs = {}
    for ci in range(SCAN_CHUNKS):
        for h in range(N_HEADS):
            outs[ci, h] = within[ci, h] + _nt_dot(operands[ci, h][4], states[h].astype(BF16))
            states[h] = states[h] * operands[ci, h][6] + products[ci, h][2]
    for h in range(N_HEADS):
        st_ref[h] = states[h]
    for ci, cs in enumerate(chunks):
        for h, hs in enumerate(heads):
            o = outs[ci, h]
            ms = jnp.mean(o * o, axis=-1, keepdims=True)
            on = o * lax.rsqrt(ms + RMS_EPS) * gon_ref[:, hs]
            o_ref[cs, hs] = (on * og_ref[cs, hs].astype(F32)).astype(o_ref.dtype)


def _hgrn2_scan(q, k, v, logf, og, g_onorm, n_batch):
    tp, d = q.shape
    rows = SCAN_CHUNKS * CHUNK
    n_chunks = tp // n_batch // rows
    assert n_chunks * rows * n_batch == tp
    blk = pl.BlockSpec((rows, d), lambda b, c: (b * n_chunks + c, 0))
    return pl.pallas_call(
        _scan_kernel,
        grid=(n_batch, n_chunks),
        in_specs=[blk, blk, blk, blk, blk, pl.BlockSpec((1, d), lambda b, c: (0, 0))],
        out_specs=blk,
        out_shape=jax.ShapeDtypeStruct((tp, d), BF16),
        scratch_shapes=[pltpu.VMEM((N_HEADS, HEAD_DIM, HEAD_DIM), F32)],
        compiler_params=_cparams(("arbitrary", "arbitrary")),
        name="hgrn2_scan",
    )(q, k, v, logf, og, g_onorm)


def _merge_route_kernel(x_ref, meta_ref, on_ref, ap_ref, sgb_ref, wrnn_ref, wo_ref, gffn_ref, wr_ref, br_ref,
                        h1_ref, u2_ref, info_ref, cnt_ref, tri_ref, carry_ref, colcnt_ref, logits_ref, hres_ref,
                        *, tiles_per_batch, n_tiles):
    tm = on_ref.shape[0]
    i = pl.program_id(0)
    hres_ref[...] = _residual_tile(x_ref, meta_ref, jnp.minimum(i, n_tiles - 1) % tiles_per_batch == 0)

    @pl.when(i == 0)
    def _():
        r_i = lax.broadcasted_iota(jnp.int32, (tm, tm), 0)
        c_i = lax.broadcasted_iota(jnp.int32, (tm, tm), 1)
        tri_ref[...] = (c_i < r_i).astype(BF16)
        carry_ref[...] = jnp.zeros_like(carry_ref)
        colcnt_ref[...] = jnp.zeros_like(colcnt_ref)
        logits_ref[...] = jnp.zeros_like(logits_ref)

    routed = i > 0
    lane = lax.broadcasted_iota(jnp.int32, (tm, LANES), 1)
    cur = jnp.where(lane < N_EXPERTS, logits_ref[...], NEG_BIG)
    vals, idxs, sels = [], [], []

    def topk_round(cur):
        m = jnp.max(cur, axis=-1, keepdims=True)
        idx = jnp.min(jnp.where(cur == m, lane, LANES), axis=-1, keepdims=True)
        sel = lane == idx
        vals.append(m)
        idxs.append(idx)
        sels.append(sel)
        return jnp.where(sel, 2.0 * NEG_BIG, cur)

    w_hi, w_lo = _split_bf16(wr_ref[...])
    part = tm // MERGE_PARTS
    rows = [slice(p * part, (p + 1) * part) for p in range(MERGE_PARTS)]
    rs = [jnp.dot(on_ref[rw, :], wrnn_ref[...], preferred_element_type=F32) for rw in rows]
    cur = topk_round(cur)
    ys = [(ap_ref[rw, :].astype(F32) + sgb_ref[rw, :].astype(F32) * r).astype(BF16) for rw, r in zip(rows, rs)]
    h1s = [hres_ref[rw, :] + jnp.dot(y, wo_ref[...], preferred_element_type=F32) for rw, y in zip(rows, ys)]
    cur = topk_round(cur)
    u2s = []
    for rw, h1 in zip(rows, h1s):
        h1_ref[rw, :] = h1
        ms = jnp.mean(h1 * h1, axis=-1, keepdims=True)
        u2 = h1 * lax.rsqrt(ms + RMS_EPS) * gffn_ref[...]
        u2_ref[rw, :] = _pack_rows(u2)
        u2s.append(u2)
    cur = topk_round(cur)
    new_logits = []
    for u2 in u2s:
        u_hi, u_lo = _split_bf16(u2)
        new_logits.append(jnp.dot(u_hi, w_hi, preferred_element_type=F32)
                          + (jnp.dot(u_lo, w_hi, preferred_element_type=F32)
                             + jnp.dot(u_hi, w_lo, preferred_element_type=F32))
                          + br_ref[...])
    cur = topk_round(cur)
    assert len(vals) == TOP_K
    exps = [jnp.exp(vk - vals[0]) for vk in vals]
    den = exps[0] + exps[1] + exps[2] + exps[3]
    onehot = jnp.logical_or(jnp.logical_or(sels[0], sels[1]), jnp.logical_or(sels[2], sels[3]))
    onehot = jnp.logical_and(onehot, routed)
    onehot_bf = onehot.astype(BF16)
    rank_all = jnp.dot(tri_ref[...], onehot_bf, preferred_element_type=F32) + carry_ref[...]
    carry_ref[...] += jnp.sum(onehot.astype(F32), axis=0, keepdims=True)
    colcnt_ref[...] += _tn_dot(onehot_bf, jnp.ones((tm, LANES), BF16))
    info = jnp.zeros((tm, LANES), F32)
    for kk in range(TOP_K):
        rank_k = jnp.sum(jnp.where(sels[kk], rank_all, 0.0), axis=-1, keepdims=True)
        info = jnp.where(lane == kk, exps[kk] / den, info)
        info = jnp.where(lane == TOP_K + kk, idxs[kk].astype(F32), info)
        info = jnp.where(lane == 2 * TOP_K + kk, rank_k, info)
    info_ref[...] = info
    cnt_ref[...] = colcnt_ref[...]
    logits_ref[...] = jnp.concatenate(new_logits, axis=0)


def _merge_route(x2, meta, seq, on, ap, sgb, w_rnn_bf16, w_o_bf16, g_ffn, w_router_pad, b_router_pad):
    tp, d = on.shape
    tm = ROW_TILE
    n_tiles = tp // tm
    tpb = (seq + CHUNK) // tm
    assert tpb * tm == seq + CHUNK
    this_tile = lambda i: (jnp.minimum(i, n_tiles - 1), 0)
    prev_tile = lambda i: (jnp.maximum(i - 1, 0), 0)
    row = pl.BlockSpec((tm, d), this_tile)
    vec = _const_spec((1, d))
    return pl.pallas_call(
        functools.partial(_merge_route_kernel, tiles_per_batch=tpb, n_tiles=n_tiles),
        grid=(n_tiles + 1,),
        in_specs=[_residual_spec(tm, d, seq, tpb, n_tiles), _const_spec(meta.shape), row, row, row,
                  _const_spec((d, d)), _const_spec((d, d)), vec, _const_spec((d, LANES)), _const_spec((1, LANES))],
        out_specs=[row, pl.BlockSpec((tm, PACK_WORDS), this_tile),
                   pl.BlockSpec((tm, LANES), prev_tile), _const_spec((LANES, LANES))],
        out_shape=[jax.ShapeDtypeStruct((tp, d), F32), jax.ShapeDtypeStruct((tp, PACK_WORDS), jnp.uint32),
                   jax.ShapeDtypeStruct((tp, LANES), F32), jax.ShapeDtypeStruct((LANES, LANES), F32)],
        scratch_shapes=[pltpu.VMEM((tm, tm), BF16), pltpu.VMEM((1, LANES), F32),
                        pltpu.VMEM((LANES, LANES), F32), pltpu.VMEM((tm, LANES), F32), pltpu.VMEM((tm, d), F32)],
        compiler_params=_cparams(("arbitrary",)),
        name="merge_route",
    )(x2, meta, on, ap, sgb, w_rnn_bf16, w_o_bf16, g_ffn, w_router_pad, b_router_pad)


def _slots_kernel(info_ref, cnt_ref, dest_ref, te_ref, pstart_ref):
    tm = info_ref.shape[0]
    n_tile_lanes = te_ref.shape[1]

    @pl.when(pl.program_id(0) == 0)
    def _():
        r_i = lax.broadcasted_iota(jnp.int32, (LANES, LANES), 0)
        c_i = lax.broadcasted_iota(jnp.int32, (LANES, LANES), 1)
        cnt = cnt_ref[...]
        tiles = jnp.floor((cnt + (EXPERT_TILE - 1)) / EXPERT_TILE)
        pend_col = jnp.dot((c_i <= r_i).astype(F32), tiles, preferred_element_type=F32,
                           precision=lax.Precision.HIGHEST)
        pstart_ref[...] = _tn_dot_f32(tiles, (r_i < c_i).astype(F32))[0:SUBLANES, :]
        tile_id = lax.broadcasted_iota(jnp.int32, (LANES, n_tile_lanes), 1).astype(F32)
        exp_id = lax.broadcasted_iota(jnp.int32, (LANES, n_tile_lanes), 0)
        pend_wide = jnp.concatenate([pend_col] * (n_tile_lanes // LANES), axis=1)
        below = jnp.logical_and(pend_wide <= tile_id, exp_id < N_EXPERTS)
        te = jnp.minimum(jnp.sum(below.astype(F32), axis=0, keepdims=True), N_EXPERTS - 1.0)
        n_active = pend_col[N_EXPERTS - 1:N_EXPERTS, 0:1]
        owner = exp_id.astype(F32) == te
        cnt_wide = jnp.concatenate([cnt] * (n_tile_lanes // LANES), axis=1)
        first_tile = pend_wide - jnp.concatenate([tiles] * (n_tile_lanes // LANES), axis=1)
        left = jnp.sum(jnp.where(owner, cnt_wide - (tile_id - first_tile) * EXPERT_TILE, 0.0), axis=0, keepdims=True)
        valid = jnp.clip(left, 0.0, float(EXPERT_TILE))
        sub_t = lax.broadcasted_iota(jnp.int32, (SUBLANES, n_tile_lanes), 0)
        out = jnp.where(sub_t == 0, jnp.broadcast_to(te, (SUBLANES, n_tile_lanes)),
                        jnp.where(sub_t == 1, jnp.broadcast_to(n_active, (SUBLANES, n_tile_lanes)),
                                  jnp.broadcast_to(valid, (SUBLANES, n_tile_lanes))))
        te_ref[...] = out.astype(jnp.int32)

    info = info_ref[...]
    pstart_row = pstart_ref[0:1, :]
    lane = lax.broadcasted_iota(jnp.int32, (tm, LANES), 1)
    dest = jnp.zeros((tm, LANES), F32)
    for kk in range(TOP_K):
        idx_k = info[:, TOP_K + kk:TOP_K + kk + 1].astype(jnp.int32)
        start_k = jnp.sum(jnp.where(lane == idx_k, pstart_row, 0.0), axis=-1, keepdims=True)
        slot_k = start_k * EXPERT_TILE + info[:, 2 * TOP_K + kk:2 * TOP_K + kk + 1]
        dest = jnp.where(lane == kk, slot_k, dest)
    dest_ref[...] = dest.astype(jnp.int32)


def _tn_dot_f32(a, b):
    return lax.dot_general(a, b, (((0,), (0,)), ((), ())), preferred_element_type=F32,
                           precision=lax.Precision.HIGHEST)


def _slots(info, cnt, n_tiles_max):
    tp = info.shape[0]
    tm = tp // SLOTS_STEPS
    n_tile_lanes = -(-n_tiles_max // LANES) * LANES
    return pl.pallas_call(
        _slots_kernel,
        grid=(tp // tm,),
        in_specs=[pl.BlockSpec((tm, LANES), lambda i: (i, 0)), _const_spec((LANES, LANES))],
        out_specs=[pl.BlockSpec((tm, LANES), lambda i: (i, 0)), _const_spec((8, n_tile_lanes))],
        out_shape=[jax.ShapeDtypeStruct((tp, LANES), jnp.int32),
                   jax.ShapeDtypeStruct((8, n_tile_lanes), jnp.int32)],
        scratch_shapes=[pltpu.VMEM((SUBLANES, LANES), F32)],
        compiler_params=_cparams(("arbitrary",)),
        name="slots",
    )(info, cnt)


def _sc_workers():
    sc = plsc.get_sparse_core_info()
    return sc.num_cores, sc.num_cores * sc.num_subcores


def _sc_token_chunks(tp, chunk, body):
    n_cores, n_workers = _sc_workers()
    per_worker = tp // n_workers
    assert per_worker * n_workers == tp and per_worker % chunk == 0 and chunk % SUBLANES == 0
    base = (lax.axis_index("s") * n_cores + lax.axis_index("c")) * per_worker

    @pl.loop(0, per_worker // chunk)
    def _(ci):
        body(pl.multiple_of(base + ci * chunk, SUBLANES))


def _sc_kernel(out_type, chunk, n_row_bufs):
    return functools.partial(
        pl.kernel, mesh=plsc.VectorSubcoreMesh(core_axis_name="c", subcore_axis_name="s"), out_type=out_type,
        scratch_types=[pltpu.VMEM((TOP_K, chunk), jnp.int32), pltpu.VMEM((n_row_bufs, chunk, PACK_WORDS), jnp.uint32),
                       pltpu.SemaphoreType.DMA, pltpu.SemaphoreType.DMA])


def _wait_all(copies):
    for c in copies:
        c.wait()


def _dispatch(dest_flat, u2p, n_slots):
    tp = u2p.shape[0]
    ch = SC_SCATTER_CHUNK

    @_sc_kernel(jax.ShapeDtypeStruct((n_slots, PACK_WORDS), jnp.uint32), ch, 1)
    def scatter_rows(u2_hbm, dest_hbm, xs_hbm, idx_v, rows_v, sem_a, sem_b):
        def chunk(off):
            loads = [pltpu.async_copy(u2_hbm.at[pl.ds(off, ch)], rows_v.at[0], sem_a)]
            loads += [pltpu.async_copy(dest_hbm.at[pl.ds(kk * tp + off, ch)], idx_v.at[kk], sem_a)
                      for kk in range(TOP_K)]
            _wait_all(loads)
            _wait_all([pltpu.async_copy(rows_v.at[0], xs_hbm.at[idx_v.at[kk]], sem_b) for kk in range(TOP_K)])

        _sc_token_chunks(tp, ch, chunk)

    return scatter_rows(u2p, dest_flat)


def _gather_expert_rows(dest_flat, ys):
    tp = dest_flat.shape[0] // TOP_K
    ch = SC_GATHER_CHUNK

    @_sc_kernel(jax.ShapeDtypeStruct((TOP_K, tp, PACK_WORDS), jnp.uint32), ch, TOP_K)
    def gather_rows(ys_hbm, dest_hbm, out_hbm, idx_v, rows_v, sem_a, sem_b):
        def chunk(off):
            _wait_all([pltpu.async_copy(dest_hbm.at[pl.ds(kk * tp + off, ch)], idx_v.at[kk], sem_a)
                       for kk in range(TOP_K)])
            _wait_all([pltpu.async_copy(ys_hbm.at[idx_v.at[kk]], rows_v.at[kk], sem_b) for kk in range(TOP_K)])
            _wait_all([pltpu.async_copy(rows_v.at[kk], out_hbm.at[kk, pl.ds(off, ch)], sem_a)
                       for kk in range(TOP_K)])

        _sc_token_chunks(tp, ch, chunk)

    return gather_rows(ys, dest_flat)


def _expert_kernel(te_ref, nact_ref, valid_ref, x_ref, wgu_hbm, bgu_ref, wdn_hbm, bdn_ref, y_ref,
                   wgu_buf, wdn_buf, slot_ref, sem):
    i = pl.program_id(0)
    n_active = nact_ref[0]
    last_tile = pl.num_programs(0) - 1

    def fetch(expert, slot):
        return (pltpu.make_async_copy(wgu_hbm.at[expert], wgu_buf.at[slot], sem.at[slot, 0]),
                pltpu.make_async_copy(wdn_hbm.at[expert], wdn_buf.at[slot], sem.at[slot, 1]))

    @pl.when(i == 0)
    def _():
        slot_ref[0] = 1
        for copy in fetch(te_ref[0], 0):
            copy.start()

    @pl.when(i < n_active)
    def _():
        expert = te_ref[i]
        first_tile_of_expert = jnp.logical_or(i == 0, expert != te_ref[jnp.maximum(i - 1, 0)])

        @pl.when(first_tile_of_expert)
        def _():
            slot = 1 - slot_ref[0]
            slot_ref[0] = slot
            for copy in fetch(expert, slot):
                copy.wait()
            nxt = lax.while_loop(
                lambda j: jnp.logical_and(j < n_active, te_ref[jnp.minimum(j, last_tile)] == expert),
                lambda j: j + 1, i + 1)

            @pl.when(nxt < n_active)
            def _():
                for copy in fetch(te_ref[jnp.minimum(nxt, last_tile)], 1 - slot):
                    copy.start()

        slot = slot_ref[0]

        def mlp(rows):
            x = _unpack_rows(x_ref[0:rows, :]).astype(BF16)
            gu = jnp.dot(x, wgu_buf[slot].astype(BF16), preferred_element_type=F32) + bgu_ref[0]
            gate = jnp.minimum(gu[:, :D_FF], SWIGLU_LIMIT)
            up = jnp.clip(gu[:, D_FF:], -SWIGLU_LIMIT, SWIGLU_LIMIT)
            hdn = (up + 1.0) * (gate * _sigmoid(SWIGLU_ALPHA * gate))
            y = jnp.dot(hdn.astype(BF16), wdn_buf[slot].astype(BF16), preferred_element_type=F32) + bdn_ref[0]
            y_ref[0:rows, :] = _pack_rows(y)

        half = EXPERT_TILE // 2
        half_empty = valid_ref[i] <= half
        pl.when(jnp.logical_not(half_empty))(lambda: mlp(EXPERT_TILE))

        @pl.when(half_empty)
        def _():
            mlp(half)
            y_ref[half:, :] = jnp.zeros((EXPERT_TILE - half, PACK_WORDS), y_ref.dtype)

    @pl.when(i >= n_active)
    def _():
        y_ref[...] = jnp.zeros_like(y_ref)


def _expert_mlp(te, nact, valid, xs, w_gu, b_gu, w_dn, b_dn):
    d = D_MODEL
    n_tiles = xs.shape[0] // EXPERT_TILE

    def tile_map(i, te_ref, nact_ref, valid_ref):
        return (jnp.minimum(i, nact_ref[0] - 1), 0)

    def exp_map(i, te_ref, nact_ref, valid_ref):
        return (te_ref[jnp.minimum(i, nact_ref[0] - 1)], 0, 0)

    grid_spec = pltpu.PrefetchScalarGridSpec(
        num_scalar_prefetch=3,
        grid=(n_tiles,),
        in_specs=[pl.BlockSpec((EXPERT_TILE, PACK_WORDS), tile_map),
                  pl.BlockSpec(memory_space=pl.ANY),
                  pl.BlockSpec((1, 1, 2 * D_FF), exp_map),
                  pl.BlockSpec(memory_space=pl.ANY),
                  pl.BlockSpec((1, 1, d), exp_map)],
        out_specs=pl.BlockSpec((EXPERT_TILE, PACK_WORDS), lambda i, te_ref, nact_ref, valid_ref: (i, 0)),
        scratch_shapes=[pltpu.VMEM((2, d, 2 * D_FF), F32), pltpu.VMEM((2, D_FF, d), F32),
                        pltpu.SMEM((1,), jnp.int32), pltpu.SemaphoreType.DMA((2, 2))],
    )
    return pl.pallas_call(
        _expert_kernel,
        grid_spec=grid_spec,
        out_shape=jax.ShapeDtypeStruct(xs.shape, jnp.uint32),
        compiler_params=_cparams(("arbitrary",)),
        name="expert_mlp",
    )(te, nact, valid, xs, w_gu, b_gu, w_dn, b_dn)


def _combine_kernel(gfin_ref, *refs):
    out_ref = refs[-1]
    for s in range(COMBINE_SUB):
        h1_ref, info_ref, yg_ref = refs[3 * s:3 * s + 3]
        info = info_ref[...]
        h2 = h1_ref[...]
        for kk in range(TOP_K):
            h2 = h2 + info[:, kk:kk + 1] * _unpack_rows(yg_ref[kk])
        ms = jnp.mean(h2 * h2, axis=-1, keepdims=True)
        out_ref[s * CHUNK:(s + 1) * CHUNK, :] = h2 * lax.rsqrt(ms + RMS_EPS) * gfin_ref[...]


def _combine(h1, info, g_final, yg, n_batch, seq):
    d = h1.shape[1]
    chunks = seq // CHUNK

    def padded(s):
        def index(j):
            c = j * COMBINE_SUB + s
            return (c // chunks) * (chunks + 1) + c % chunks + 1
        return index

    in_specs = [pl.BlockSpec((1, d), lambda j: (0, 0))]
    operands = [g_final]
    for s in range(COMBINE_SUB):
        chunk_of = padded(s)
        in_specs += [pl.BlockSpec((CHUNK, d), lambda j, f=chunk_of: (f(j), 0)),
                     pl.BlockSpec((CHUNK, LANES), lambda j, f=chunk_of: (f(j), 0)),
                     pl.BlockSpec((TOP_K, CHUNK, PACK_WORDS), lambda j, f=chunk_of: (0, f(j), 0))]
        operands += [h1, info, yg]
    return pl.pallas_call(
        _combine_kernel,
        grid=(n_batch * chunks // COMBINE_SUB,),
        in_specs=in_specs,
        out_specs=pl.BlockSpec((COMBINE_SUB * CHUNK, d), lambda j: (j, 0)),
        out_shape=jax.ShapeDtypeStruct((n_batch * seq, d), F32),
        compiler_params=_cparams(("parallel",)),
        name="combine",
    )(*operands)


def kernel(x, meta_tokens, lb_logits, g_mix, w_in, w_dw, b_dw, ln_g, ln_b, w_conv_out, b_conv_out,
           g_onorm, w_rnn_out, w_o, g_ffn, w_router, b_router, w_gate_up, b_gate_up, w_down, b_down,
           g_final):
    n_batch, seq, d = x.shape
    assert d == D_MODEL and w_in.shape[0] == 1, "single-layer block with D_MODEL features"
    assert seq % (CHUNK * COMBINE_SUB) == 0
    seq_pad = CHUNK + seq
    tp = n_batch * seq_pad
    assert tp % ROW_TILE == 0

    x2 = x.reshape(n_batch * seq, d)
    meta = meta_tokens.astype(x.dtype)
    vec = lambda a: a.reshape(1, -1).astype(F32)

    aglu, q, k, logf, v, og, sga, sgb = _in_proj(x2, meta, seq, vec(g_mix[0]), lb_logits.astype(F32),
                                                 w_in[0].astype(BF16))
    w_taps = w_dw[0].astype(F32).reshape(CONV_WIDTH, d // LANES, LANES).transpose(1, 0, 2)
    ap = _conv_branch(aglu, w_taps, vec(b_dw[0]), vec(ln_g[0]), vec(ln_b[0]),
                      w_conv_out[0].astype(BF16), vec(b_conv_out[0]), sga)
    on = _hgrn2_scan(q, k, v, logf, og, vec(g_onorm[0]), n_batch)
    w_router_pad = jnp.pad(w_router[0].astype(F32), ((0, 0), (0, LANES - N_EXPERTS)))
    b_router_pad = jnp.pad(vec(b_router[0]), ((0, 0), (0, LANES - N_EXPERTS)))
    h1, u2p, info, cnt = _merge_route(x2, meta, seq, on, ap, sgb, w_rnn_out[0].astype(BF16), w_o[0].astype(BF16),
                                      vec(g_ffn[0]), w_router_pad, b_router_pad)

    n_tiles_max = -(-(tp * TOP_K + N_EXPERTS * (EXPERT_TILE - 1)) // EXPERT_TILE)
    dest, te = _slots(info, cnt, n_tiles_max)
    dest_flat = dest[:, :TOP_K].T.reshape(-1)
    xs = _dispatch(dest_flat, u2p, n_tiles_max * EXPERT_TILE)
    ys = _expert_mlp(te[0, :n_tiles_max], te[1, :1], te[2, :n_tiles_max], xs, w_gate_up[0],
                     b_gate_up[0].reshape(N_EXPERTS, 1, -1).astype(F32), w_down[0],
                     b_down[0].reshape(N_EXPERTS, 1, -1).astype(F32))
    yg = _gather_expert_rows(dest_flat, ys)
    out = _combine(h1, info, vec(g_final), yg, n_batch, seq)
    return out.reshape(n_batch, seq, d)
```

```python
import functools

import jax
import jax.numpy as jnp
from jax import lax
from jax.experimental import pallas as pl
from jax.experimental.pallas import tpu as pltpu
from jax.experimental.pallas import tpu_sc as plsc

F32 = jnp.float32
BF16 = jnp.bfloat16

D_MODEL = 1024
N_META = 16
CHUNK = 128
CHUNK_PAD = CHUNK - N_META
CONV_WIDTH = 31
HEAD_DIM = 128
N_HEADS = D_MODEL // HEAD_DIM
N_EXPERTS = 32
TOP_K = 4
D_FF = D_MODEL
SWIGLU_LIMIT = 7.0
SWIGLU_ALPHA = 1.702
RMS_EPS = 1e-6
LN_EPS = 1e-5

LANES = 128
SUBLANES = 8
SUB_BLOCK = 32
N_SUB = CHUNK // SUB_BLOCK
HALO = 32
ROW_TILE = 640
IN_TILE = 320
SCAN_CHUNKS = 5
MERGE_PARTS = 2
SLOTS_STEPS = 5
CONV_ROWS = 64
CONV_SUMS = 4
EXPERT_TILE = 512
SC_SCATTER_CHUNK = 104
SC_GATHER_CHUNK = 40
COMBINE_SUB = 4
NEG_BIG = -1e30
VMEM_LIMIT = 56 * 1024 * 1024


def _sigmoid(x):
    return 1.0 / (1.0 + jnp.exp(-x))


def _cparams(sem):
    return pltpu.CompilerParams(dimension_semantics=sem, vmem_limit_bytes=VMEM_LIMIT)


def _const_spec(shape):
    nd = len(shape)
    return pl.BlockSpec(shape, lambda *_: (0,) * nd)


PACK_WORDS = D_MODEL // 2
HIGH_HALF = 0xFFFF0000


def _pack_rows(x):
    lo = lax.bitcast_convert_type(x[:, :PACK_WORDS].astype(BF16).astype(F32), jnp.uint32)
    hi = lax.bitcast_convert_type(x[:, PACK_WORDS:].astype(BF16).astype(F32), jnp.uint32)
    return (lo >> 16) | (hi & jnp.uint32(HIGH_HALF))


def _unpack_rows(w):
    lo = lax.bitcast_convert_type(w << 16, F32)
    hi = lax.bitcast_convert_type(w & jnp.uint32(HIGH_HALF), F32)
    return jnp.concatenate([lo, hi], axis=1)


def _residual_tile(x_ref, meta_ref, is_first):
    x = x_ref[...]
    tm, d = x.shape
    prefix = jnp.concatenate([jnp.zeros((CHUNK_PAD, d), x.dtype), meta_ref[...]], axis=0)
    first = jnp.concatenate([prefix, x[:tm - CHUNK, :]], axis=0)
    return jnp.where(is_first, first, x)


def _residual_spec(tm, d, seq, tiles_per_batch, n_tiles):
    def start(i):
        i = jnp.minimum(i, n_tiles - 1)
        b, t = i // tiles_per_batch, i % tiles_per_batch
        return (pl.multiple_of(b * seq + jnp.maximum(t * tm - CHUNK, 0), SUBLANES), 0)
    return pl.BlockSpec((pl.Element(tm), pl.Element(d)), start)


def _in_proj_kernel(x_ref, meta_ref, g_ref, lbl_ref, w_ref, aglu_ref, q_ref, k_ref, logf_ref, v_ref,
                    og_ref, sga_ref, sgb_ref, *, tiles_per_batch):
    d = D_MODEL
    h = _residual_tile(x_ref, meta_ref, pl.program_id(0) % tiles_per_batch == 0)
    ms = jnp.mean(h * h, axis=-1, keepdims=True)
    u = (h * lax.rsqrt(ms + RMS_EPS) * g_ref[...]).astype(BF16)

    def proj(j):
        return jnp.dot(u, w_ref[:, j * d:(j + 1) * d], preferred_element_type=F32)

    aglu_ref[...] = (proj(0) * _sigmoid(proj(1))).astype(aglu_ref.dtype)
    zq = proj(2)
    q_ref[...] = (zq * _sigmoid(zq)).astype(q_ref.dtype)
    lbl = lbl_ref[...]
    e = jnp.exp(lbl - jnp.max(lbl, axis=0, keepdims=True))
    lb = e[0:1, :] / jnp.sum(e, axis=0, keepdims=True)
    s = _sigmoid(proj(3))
    logf_ref[...] = jnp.log(lb + (1.0 - lb) * s)
    k_ref[...] = ((1.0 - lb) * (1.0 - s)).astype(k_ref.dtype)
    v_ref[...] = proj(4).astype(v_ref.dtype)
    zg = proj(5)
    og_ref[...] = (zg * _sigmoid(zg)).astype(og_ref.dtype)
    sga_ref[...] = _sigmoid(proj(6)).astype(sga_ref.dtype)
    sgb_ref[...] = _sigmoid(proj(7)).astype(sgb_ref.dtype)


def _in_proj(x2, meta, seq, g_mix, lb_logits, w_in_bf16):
    d = x2.shape[1]
    n_batch = x2.shape[0] // seq
    tm = IN_TILE
    tpb = (seq + CHUNK) // tm
    tp = n_batch * (seq + CHUNK)
    assert tpb * tm == seq + CHUNK and tm > CHUNK
    row = pl.BlockSpec((tm, d), lambda i: (i, 0))
    out_dtypes = [BF16, BF16, BF16, F32, BF16, BF16, BF16, BF16]
    return pl.pallas_call(
        functools.partial(_in_proj_kernel, tiles_per_batch=tpb),
        grid=(tp // tm,),
        in_specs=[_residual_spec(tm, d, seq, tpb, tp // tm), _const_spec(meta.shape), _const_spec((1, d)),
                  _const_spec(lb_logits.shape),
                  pl.BlockSpec(w_in_bf16.shape, lambda i: (0, 0), pipeline_mode=pl.Buffered(1))],
        out_specs=[row] * 8,
        out_shape=[jax.ShapeDtypeStruct((tp, d), dt) for dt in out_dtypes],
        compiler_params=_cparams(("parallel",)),
        name="in_proj",
    )(x2, meta, g_mix, lb_logits, w_in_bf16)


def _conv_kernel(halo_ref, cur_ref, wdw_ref, bdw_ref, lng_ref, lnb_ref, wout_ref, bout_ref, sga_ref,
                 out_ref, win_ref, acc_ref, shift_ref):
    tm = cur_ref.shape[0]
    win_ref[0:HALO, :] = halo_ref[...].astype(F32)
    win_ref[HALO:, :] = cur_ref[...].astype(F32)
    first_tap = HALO - (CONV_WIDTH - 1)
    for c in range(D_MODEL // LANES):
        lanes = slice(c * LANES, (c + 1) * LANES)
        n_rows = tm + HALO - SUBLANES
        for s in range(1, SUBLANES):
            shift_ref[s - 1, 0:n_rows, :] = win_ref[pl.ds(s, n_rows), lanes]
        bias = jnp.broadcast_to(bdw_ref[:, lanes], (CONV_ROWS, LANES))
        groups = CONV_ROWS // SUBLANES

        def chunk(r, carry, c=c, lanes=lanes, bias=bias):
            r0 = pl.multiple_of(r * CONV_ROWS, CONV_ROWS)
            accs = [bias.reshape(groups, SUBLANES, LANES)] + [None] * (CONV_SUMS - 1)
            for j in range(CONV_WIDTH):
                off = first_tap + j
                rows = pl.ds(r0 + off - off % SUBLANES, CONV_ROWS)
                w_j = wdw_ref[c, pl.ds(j, SUBLANES, stride=0), :]
                window = shift_ref[off % SUBLANES - 1, rows, :] if off % SUBLANES else win_ref[rows, lanes]
                term = w_j[None] * window.reshape(groups, SUBLANES, LANES)
                accs[j % CONV_SUMS] = term if accs[j % CONV_SUMS] is None else accs[j % CONV_SUMS] + term
            while len(accs) > 1:
                accs = [a + b for a, b in zip(accs[0::2], accs[1::2])]
            acc_ref[pl.ds(r0, CONV_ROWS), lanes] = accs[0].reshape(CONV_ROWS, LANES)
            return carry

        lax.fori_loop(0, tm // CONV_ROWS, chunk, 0)
    a = acc_ref[...]
    mu = jnp.mean(a, axis=-1, keepdims=True)
    ac = a - mu
    var = jnp.mean(ac * ac, axis=-1, keepdims=True)
    y = ac * lax.rsqrt(var + LN_EPS) * lng_ref[...] + lnb_ref[...]
    y = y * _sigmoid(y)
    o = jnp.dot(y.astype(BF16), wout_ref[...], preferred_element_type=F32) + bout_ref[...]
    out_ref[...] = (sga_ref[...].astype(F32) * o).astype(out_ref.dtype)


def _conv_branch(aglu, w_dw, b_dw, ln_g, ln_b, w_out_bf16, b_out, sga):
    tp, d = aglu.shape
    tm = ROW_TILE
    per = tm // HALO
    row = pl.BlockSpec((tm, d), lambda i: (i, 0))
    halo = pl.BlockSpec((HALO, d), lambda i: (jnp.maximum(i * per - 1, 0), 0))
    vec = _const_spec((1, d))
    return pl.pallas_call(
        _conv_kernel,
        grid=(tp // tm,),
        in_specs=[halo, row, _const_spec(w_dw.shape), vec, vec, vec, _const_spec((d, d)), vec, row],
        out_specs=row,
        out_shape=jax.ShapeDtypeStruct((tp, d), BF16),
        scratch_shapes=[pltpu.VMEM((tm + HALO, d), F32), pltpu.VMEM((tm, d), F32),
                        pltpu.VMEM((SUBLANES - 1, tm + HALO - SUBLANES, LANES), F32)],
        compiler_params=_cparams(("parallel",)),
        name="conv_branch",
    )(aglu, aglu, w_dw, b_dw, ln_g, ln_b, w_out_bf16, b_out, sga)


def _split_bf16(x):
    hi = x.astype(BF16)
    return hi, (x - hi.astype(F32)).astype(BF16)


def _nt_dot(a, b):
    return lax.dot_general(a, b, (((1,), (1,)), ((), ())), preferred_element_type=F32)


def _tn_dot(a, b):
    return lax.dot_general(a, b, (((0,), (0,)), ((), ())), preferred_element_type=F32)


def _scan_kernel(q_ref, k_ref, v_ref, lf_ref, og_ref, gon_ref, o_ref, st_ref):
    @pl.when(pl.program_id(1) == 0)
    def _():
        st_ref[...] = jnp.zeros_like(st_ref)

    c = CHUNK
    row = lax.broadcasted_iota(jnp.int32, (c, c), 0)
    col = lax.broadcasted_iota(jnp.int32, (c, c), 1)
    causal = col <= row
    diag_mask = jnp.logical_and(causal, row // SUB_BLOCK == col // SUB_BLOCK)
    tri = causal.astype(BF16)

    def bcast_rows(rows):
        return jnp.concatenate([jnp.broadcast_to(r, (SUB_BLOCK, HEAD_DIM)) for r in rows], axis=0)

    heads = [slice(h * HEAD_DIM, (h + 1) * HEAD_DIM) for h in range(N_HEADS)]
    chunks = [slice(ci * c, (ci + 1) * c) for ci in range(SCAN_CHUNKS)]

    bcums = []
    for cs in chunks:
        lf = lf_ref[cs, :]
        lf_hi = lf.astype(BF16)
        lf_mid, lf_lo = _split_bf16(lf - lf_hi.astype(F32))
        bcums.append(jnp.dot(tri, lf_hi, preferred_element_type=F32)
                     + (jnp.dot(tri, lf_mid, preferred_element_type=F32)
                        + jnp.dot(tri, lf_lo, preferred_element_type=F32)))
    operands = {}
    for ci, cs in enumerate(chunks):
        for h, hs in enumerate(heads):
            b = bcums[ci][:, hs]
            q = q_ref[cs, hs].astype(F32)
            k = k_ref[cs, hs].astype(F32)
            ends = [b[i * SUB_BLOCK + SUB_BLOCK - 1:i * SUB_BLOCK + SUB_BLOCK, :] for i in range(N_SUB)]
            mids = [b[i * SUB_BLOCK + SUB_BLOCK // 2 - 1:i * SUB_BLOCK + SUB_BLOCK // 2, :] for i in range(N_SUB)]
            mid_full = bcast_rows(mids)
            b_last = ends[-1]
            zero_row = jnp.zeros_like(b_last)
            qm = q * jnp.exp(b - mid_full)
            km = k * jnp.exp(mid_full - b)
            q_parts, k_parts = [], []
            for j in range(N_SUB - 1):
                q_rows = [jnp.exp(mids[i] - ends[j]) if i > j else zero_row for i in range(N_SUB)]
                k_rows = [jnp.exp(ends[j] - mids[j]) if i == j else zero_row for i in range(N_SUB)]
                q_parts.append((qm * bcast_rows(q_rows)).astype(BF16))
                k_parts.append((km * bcast_rows(k_rows)).astype(BF16))
            q_in = (qm * bcast_rows([jnp.exp(m) for m in mids])).astype(BF16)
            k_out = (km * bcast_rows([jnp.exp(b_last - m) for m in mids])).astype(BF16)
            operands[ci, h] = (qm.astype(BF16), km.astype(BF16), jnp.concatenate(q_parts, axis=1),
                               jnp.concatenate(k_parts, axis=1), q_in, k_out, jnp.exp(b_last))
    products = {}
    for ci, cs in enumerate(chunks):
        for h, hs in enumerate(heads):
            qm, km, q_cat, k_cat, _, k_out, _ = operands[ci, h]
            products[ci, h] = (_nt_dot(qm, km), _nt_dot(q_cat, k_cat), _tn_dot(v_ref[cs, hs], k_out))
    within = {}
    for ci, cs in enumerate(chunks):
        for h, hs in enumerate(heads):
            same_block, earlier_blocks, _ = products[ci, h]
            scores = jnp.where(diag_mask, same_block, 0.0) + earlier_blocks
            within[ci, h] = jnp.dot(scores.astype(BF16), v_ref[cs, hs], preferred_element_type=F32)
    states = [st_ref[h] for h in range(N_HEADS)]
    outs = {}
    for ci in range(SCAN_CHUNKS):
        for h in range(N_HEADS):
            outs[ci, h] = within[ci, h] + _nt_dot(operands[ci, h][4], states[h].astype(BF16))
            states[h] = states[h] * operands[ci, h][6] + products[ci, h][2]
    for h in range(N_HEADS):
        st_ref[h] = states[h]
    for ci, cs in enumerate(chunks):
        for h, hs in enumerate(heads):
            o = outs[ci, h]
            ms = jnp.mean(o * o, axis=-1, keepdims=True)
            on = o * lax.rsqrt(ms + RMS_EPS) * gon_ref[:, hs]
            o_ref[cs, hs] = (on * og_ref[cs, hs].astype(F32)).astype(o_ref.dtype)


def _hgrn2_scan(q, k, v, logf, og, g_onorm, n_batch):
    tp, d = q.shape
    rows = SCAN_CHUNKS * CHUNK
    n_chunks = tp // n_batch // rows
    assert n_chunks * rows * n_batch == tp
    blk = pl.BlockSpec((rows, d), lambda b, c: (b * n_chunks + c, 0))
    return pl.pallas_call(
        _scan_kernel,
        grid=(n_batch, n_chunks),
        in_specs=[blk, blk, blk, blk, blk, pl.BlockSpec((1, d), lambda b, c: (0, 0))],
        out_specs=blk,
        out_shape=jax.ShapeDtypeStruct((tp, d), BF16),
        scratch_shapes=[pltpu.VMEM((N_HEADS, HEAD_DIM, HEAD_DIM), F32)],
        compiler_params=_cparams(("arbitrary", "arbitrary")),
        name="hgrn2_scan",
    )(q, k, v, logf, og, g_onorm)


def _merge_route_kernel(x_ref, meta_ref, on_ref, ap_ref, sgb_ref, wrnn_ref, wo_ref, gffn_ref, wr_ref, br_ref,
                        h1_ref, u2_ref, info_ref, cnt_ref, tri_ref, carry_ref, colcnt_ref, logits_ref, hres_ref,
                        *, tiles_per_batch, n_tiles):
    tm = on_ref.shape[0]
    i = pl.program_id(0)
    hres_ref[...] = _residual_tile(x_ref, meta_ref, jnp.minimum(i, n_tiles - 1) % tiles_per_batch == 0)

    @pl.when(i == 0)
    def _():
        r_i = lax.broadcasted_iota(jnp.int32, (tm, tm), 0)
        c_i = lax.broadcasted_iota(jnp.int32, (tm, tm), 1)
        tri_ref[...] = (c_i < r_i).astype(BF16)
        carry_ref[...] = jnp.zeros_like(carry_ref)
        colcnt_ref[...] = jnp.zeros_like(colcnt_ref)
        logits_ref[...] = jnp.zeros_like(logits_ref)

    routed = i > 0
    lane = lax.broadcasted_iota(jnp.int32, (tm, LANES), 1)
    cur = jnp.where(lane < N_EXPERTS, logits_ref[...], NEG_BIG)
    vals, idxs, sels = [], [], []

    def topk_round(cur):
        m = jnp.max(cur, axis=-1, keepdims=True)
        idx = jnp.min(jnp.where(cur == m, lane, LANES), axis=-1, keepdims=True)
        sel = lane == idx
        vals.append(m)
        idxs.append(idx)
        sels.append(sel)
        return jnp.where(sel, 2.0 * NEG_BIG, cur)

    w_hi, w_lo = _split_bf16(wr_ref[...])
    part = tm // MERGE_PARTS
    rows = [slice(p * part, (p + 1) * part) for p in range(MERGE_PARTS)]
    rs = [jnp.dot(on_ref[rw, :], wrnn_ref[...], preferred_element_type=F32) for rw in rows]
    cur = topk_round(cur)
    ys = [(ap_ref[rw, :].astype(F32) + sgb_ref[rw, :].astype(F32) * r).astype(BF16) for rw, r in zip(rows, rs)]
    h1s = [hres_ref[rw, :] + jnp.dot(y, wo_ref[...], preferred_element_type=F32) for rw, y in zip(rows, ys)]
    cur = topk_round(cur)
    u2s = []
    for rw, h1 in zip(rows, h1s):
        h1_ref[rw, :] = h1
        ms = jnp.mean(h1 * h1, axis=-1, keepdims=True)
        u2 = h1 * lax.rsqrt(ms + RMS_EPS) * gffn_ref[...]
        u2_ref[rw, :] = _pack_rows(u2)
        u2s.append(u2)
    cur = topk_round(cur)
    new_logits = []
    for u2 in u2s:
        u_hi, u_lo = _split_bf16(u2)
        new_logits.append(jnp.dot(u_hi, w_hi, preferred_element_type=F32)
                          + (jnp.dot(u_lo, w_hi, preferred_element_type=F32)
                             + jnp.dot(u_hi, w_lo, preferred_element_type=F32))
                          + br_ref[...])
    cur = topk_round(cur)
    assert len(vals) == TOP_K
    exps = [jnp.exp(vk - vals[0]) for vk in vals]
    den = exps[0] + exps[1] + exps[2] + exps[3]
    onehot = jnp.logical_or(jnp.logical_or(sels[0], sels[1]), jnp.logical_or(sels[2], sels[3]))
    onehot = jnp.logical_and(onehot, routed)
    onehot_bf = onehot.astype(BF16)
    rank_all = jnp.dot(tri_ref[...], onehot_bf, preferred_element_type=F32) + carry_ref[...]
    carry_ref[...] += jnp.sum(onehot.astype(F32), axis=0, keepdims=True)
    colcnt_ref[...] += _tn_dot(onehot_bf, jnp.ones((tm, LANES), BF16))
    info = jnp.zeros((tm, LANES), F32)
    for kk in range(TOP_K):
        rank_k = jnp.sum(jnp.where(sels[kk], rank_all, 0.0), axis=-1, keepdims=True)
        info = jnp.where(lane == kk, exps[kk] / den, info)
        info = jnp.where(lane == TOP_K + kk, idxs[kk].astype(F32), info)
        info = jnp.where(lane == 2 * TOP_K + kk, rank_k, info)
    info_ref[...] = info
    cnt_ref[...] = colcnt_ref[...]
    logits_ref[...] = jnp.concatenate(new_logits, axis=0)


def _merge_route(x2, meta, seq, on, ap, sgb, w_rnn_bf16, w_o_bf16, g_ffn, w_router_pad, b_router_pad):
    tp, d = on.shape
    tm = ROW_TILE
    n_tiles = tp // tm
    tpb = (seq + CHUNK) // tm
    assert tpb * tm == seq + CHUNK
    this_tile = lambda i: (jnp.minimum(i, n_tiles - 1), 0)
    prev_tile = lambda i: (jnp.maximum(i - 1, 0), 0)
    row = pl.BlockSpec((tm, d), this_tile)
    vec = _const_spec((1, d))
    return pl.pallas_call(
        functools.partial(_merge_route_kernel, tiles_per_batch=tpb, n_tiles=n_tiles),
        grid=(n_tiles + 1,),
        in_specs=[_residual_spec(tm, d, seq, tpb, n_tiles), _const_spec(meta.shape), row, row, row,
                  _const_spec((d, d)), _const_spec((d, d)), vec, _const_spec((d, LANES)), _const_spec((1, LANES))],
        out_specs=[row, pl.BlockSpec((tm, PACK_WORDS), this_tile),
                   pl.BlockSpec((tm, LANES), prev_tile), _const_spec((LANES, LANES))],
        out_shape=[jax.ShapeDtypeStruct((tp, d), F32), jax.ShapeDtypeStruct((tp, PACK_WORDS), jnp.uint32),
                   jax.ShapeDtypeStruct((tp, LANES), F32), jax.ShapeDtypeStruct((LANES, LANES), F32)],
        scratch_shapes=[pltpu.VMEM((tm, tm), BF16), pltpu.VMEM((1, LANES), F32),
                        pltpu.VMEM((LANES, LANES), F32), pltpu.VMEM((tm, LANES), F32), pltpu.VMEM((tm, d), F32)],
        compiler_params=_cparams(("arbitrary",)),
        name="merge_route",
    )(x2, meta, on, ap, sgb, w_rnn_bf16, w_o_bf16, g_ffn, w_router_pad, b_router_pad)


def _slots_kernel(info_ref, cnt_ref, dest_ref, te_ref, pstart_ref):
    tm = info_ref.shape[0]
    n_tile_lanes = te_ref.shape[1]

    @pl.when(pl.program_id(0) == 0)
    def _():
        r_i = lax.broadcasted_iota(jnp.int32, (LANES, LANES), 0)
        c_i = lax.broadcasted_iota(jnp.int32, (LANES, LANES), 1)
        cnt = cnt_ref[...]
        tiles = jnp.floor((cnt + (EXPERT_TILE - 1)) / EXPERT_TILE)
        pend_col = jnp.dot((c_i <= r_i).astype(F32), tiles, preferred_element_type=F32,
                           precision=lax.Precision.HIGHEST)
        pstart_ref[...] = _tn_dot_f32(tiles, (r_i < c_i).astype(F32))[0:SUBLANES, :]
        tile_id = lax.broadcasted_iota(jnp.int32, (LANES, n_tile_lanes), 1).astype(F32)
        exp_id = lax.broadcasted_iota(jnp.int32, (LANES, n_tile_lanes), 0)
        pend_wide = jnp.concatenate([pend_col] * (n_tile_lanes // LANES), axis=1)
        below = jnp.logical_and(pend_wide <= tile_id, exp_id < N_EXPERTS)
        te = jnp.minimum(jnp.sum(below.astype(F32), axis=0, keepdims=True), N_EXPERTS - 1.0)
        n_active = pend_col[N_EXPERTS - 1:N_EXPERTS, 0:1]
        owner = exp_id.astype(F32) == te
        cnt_wide = jnp.concatenate([cnt] * (n_tile_lanes // LANES), axis=1)
        first_tile = pend_wide - jnp.concatenate([tiles] * (n_tile_lanes // LANES), axis=1)
        left = jnp.sum(jnp.where(owner, cnt_wide - (tile_id - first_tile) * EXPERT_TILE, 0.0), axis=0, keepdims=True)
        valid = jnp.clip(left, 0.0, float(EXPERT_TILE))
        sub_t = lax.broadcasted_iota(jnp.int32, (SUBLANES, n_tile_lanes), 0)
        out = jnp.where(sub_t == 0, jnp.broadcast_to(te, (SUBLANES, n_tile_lanes)),
                        jnp.where(sub_t == 1, jnp.broadcast_to(n_active, (SUBLANES, n_tile_lanes)),
                                  jnp.broadcast_to(valid, (SUBLANES, n_tile_lanes))))
        te_ref[...] = out.astype(jnp.int32)

    info = info_ref[...]
    pstart_row = pstart_ref[0:1, :]
    lane = lax.broadcasted_iota(jnp.int32, (tm, LANES), 1)
    dest = jnp.zeros((tm, LANES), F32)
    for kk in range(TOP_K):
        idx_k = info[:, TOP_K + kk:TOP_K + kk + 1].astype(jnp.int32)
        start_k = jnp.sum(jnp.where(lane == idx_k, pstart_row, 0.0), axis=-1, keepdims=True)
        slot_k = start_k * EXPERT_TILE + info[:, 2 * TOP_K + kk:2 * TOP_K + kk + 1]
        dest = jnp.where(lane == kk, slot_k, dest)
    dest_ref[...] = dest.astype(jnp.int32)


def _tn_dot_f32(a, b):
    return lax.dot_general(a, b, (((0,), (0,)), ((), ())), preferred_element_type=F32,
                           precision=lax.Precision.HIGHEST)


def _slots(info, cnt, n_tiles_max):
    tp = info.shape[0]
    tm = tp // SLOTS_STEPS
    n_tile_lanes = -(-n_tiles_max // LANES) * LANES
    return pl.pallas_call(
        _slots_kernel,
        grid=(tp // tm,),
        in_specs=[pl.BlockSpec((tm, LANES), lambda i: (i, 0)), _const_spec((LANES, LANES))],
        out_specs=[pl.BlockSpec((tm, LANES), lambda i: (i, 0)), _const_spec((8, n_tile_lanes))],
        out_shape=[jax.ShapeDtypeStruct((tp, LANES), jnp.int32),
                   jax.ShapeDtypeStruct((8, n_tile_lanes), jnp.int32)],
        scratch_shapes=[pltpu.VMEM((SUBLANES, LANES), F32)],
        compiler_params=_cparams(("arbitrary",)),
        name="slots",
    )(info, cnt)


def _sc_workers():
    sc = plsc.get_sparse_core_info()
    return sc.num_cores, sc.num_cores * sc.num_subcores


def _sc_token_chunks(tp, chunk, body):
    n_cores, n_workers = _sc_workers()
    per_worker = tp // n_workers
    assert per_worker * n_workers == tp and per_worker % chunk == 0 and chunk % SUBLANES == 0
    base = (lax.axis_index("s") * n_cores + lax.axis_index("c")) * per_worker

    @pl.loop(0, per_worker // chunk)
    def _(ci):
        body(pl.multiple_of(base + ci * chunk, SUBLANES))


def _sc_kernel(out_type, chunk, n_row_bufs):
    return functools.partial(
        pl.kernel, mesh=plsc.VectorSubcoreMesh(core_axis_name="c", subcore_axis_name="s"), out_type=out_type,
        scratch_types=[pltpu.VMEM((TOP_K, chunk), jnp.int32), pltpu.VMEM((n_row_bufs, chunk, PACK_WORDS), jnp.uint32),
                       pltpu.SemaphoreType.DMA, pltpu.SemaphoreType.DMA])


def _wait_all(copies):
    for c in copies:
        c.wait()


def _dispatch(dest_flat, u2p, n_slots):
    tp = u2p.shape[0]
    ch = SC_SCATTER_CHUNK

    @_sc_kernel(jax.ShapeDtypeStruct((n_slots, PACK_WORDS), jnp.uint32), ch, 1)
    def scatter_rows(u2_hbm, dest_hbm, xs_hbm, idx_v, rows_v, sem_a, sem_b):
        def chunk(off):
            loads = [pltpu.async_copy(u2_hbm.at[pl.ds(off, ch)], rows_v.at[0], sem_a)]
            loads += [pltpu.async_copy(dest_hbm.at[pl.ds(kk * tp + off, ch)], idx_v.at[kk], sem_a)
                      for kk in range(TOP_K)]
            _wait_all(loads)
            _wait_all([pltpu.async_copy(rows_v.at[0], xs_hbm.at[idx_v.at[kk]], sem_b) for kk in range(TOP_K)])

        _sc_token_chunks(tp, ch, chunk)

    return scatter_rows(u2p, dest_flat)


def _gather_expert_rows(dest_flat, ys):
    tp = dest_flat.shape[0] // TOP_K
    ch = SC_GATHER_CHUNK

    @_sc_kernel(jax.ShapeDtypeStruct((TOP_K, tp, PACK_WORDS), jnp.uint32), ch, TOP_K)
    def gather_rows(ys_hbm, dest_hbm, out_hbm, idx_v, rows_v, sem_a, sem_b):
        def chunk(off):
            _wait_all([pltpu.async_copy(dest_hbm.at[pl.ds(kk * tp + off, ch)], idx_v.at[kk], sem_a)
                       for kk in range(TOP_K)])
            _wait_all([pltpu.async_copy(ys_hbm.at[idx_v.at[kk]], rows_v.at[kk], sem_b) for kk in range(TOP_K)])
            _wait_all([pltpu.async_copy(rows_v.at[kk], out_hbm.at[kk, pl.ds(off, ch)], sem_a)
                       for kk in range(TOP_K)])

        _sc_token_chunks(tp, ch, chunk)

    return gather_rows(ys, dest_flat)


def _expert_kernel(te_ref, nact_ref, valid_ref, x_ref, wgu_hbm, bgu_ref, wdn_hbm, bdn_ref, y_ref,
                   wgu_buf, wdn_buf, slot_ref, sem):
    i = pl.program_id(0)
    n_active = nact_ref[0]
    last_tile = pl.num_programs(0) - 1

    def fetch(expert, slot):
        return (pltpu.make_async_copy(wgu_hbm.at[expert], wgu_buf.at[slot], sem.at[slot, 0]),
                pltpu.make_async_copy(wdn_hbm.at[expert], wdn_buf.at[slot], sem.at[slot, 1]))

    @pl.when(i == 0)
    def _():
        slot_ref[0] = 1
        for copy in fetch(te_ref[0], 0):
            copy.start()

    @pl.when(i < n_active)
    def _():
        expert = te_ref[i]
        first_tile_of_expert = jnp.logical_or(i == 0, expert != te_ref[jnp.maximum(i - 1, 0)])

        @pl.when(first_tile_of_expert)
        def _():
            slot = 1 - slot_ref[0]
            slot_ref[0] = slot
            for copy in fetch(expert, slot):
                copy.wait()
            nxt = lax.while_loop(
                lambda j: jnp.logical_and(j < n_active, te_ref[jnp.minimum(j, last_tile)] == expert),
                lambda j: j + 1, i + 1)

            @pl.when(nxt < n_active)
            def _():
                for copy in fetch(te_ref[jnp.minimum(nxt, last_tile)], 1 - slot):
                    copy.start()

        slot = slot_ref[0]

        def mlp(rows):
            x = _unpack_rows(x_ref[0:rows, :]).astype(BF16)
            gu = jnp.dot(x, wgu_buf[slot].astype(BF16), preferred_element_type=F32) + bgu_ref[0]
            gate = jnp.minimum(gu[:, :D_FF], SWIGLU_LIMIT)
            up = jnp.clip(gu[:, D_FF:], -SWIGLU_LIMIT, SWIGLU_LIMIT)
            hdn = (up + 1.0) * (gate * _sigmoid(SWIGLU_ALPHA * gate))
            y = jnp.dot(hdn.astype(BF16), wdn_buf[slot].astype(BF16), preferred_element_type=F32) + bdn_ref[0]
            y_ref[0:rows, :] = _pack_rows(y)

        half = EXPERT_TILE // 2
        half_empty = valid_ref[i] <= half
        pl.when(jnp.logical_not(half_empty))(lambda: mlp(EXPERT_TILE))

        @pl.when(half_empty)
        def _():
            mlp(half)
            y_ref[half:, :] = jnp.zeros((EXPERT_TILE - half, PACK_WORDS), y_ref.dtype)

    @pl.when(i >= n_active)
    def _():
        y_ref[...] = jnp.zeros_like(y_ref)


def _expert_mlp(te, nact, valid, xs, w_gu, b_gu, w_dn, b_dn):
    d = D_MODEL
    n_tiles = xs.shape[0] // EXPERT_TILE

    def tile_map(i, te_ref, nact_ref, valid_ref):
        return (jnp.minimum(i, nact_ref[0] - 1), 0)

    def exp_map(i, te_ref, nact_ref, valid_ref):
        return (te_ref[jnp.minimum(i, nact_ref[0] - 1)], 0, 0)

    grid_spec = pltpu.PrefetchScalarGridSpec(
        num_scalar_prefetch=3,
        grid=(n_tiles,),
        in_specs=[pl.BlockSpec((EXPERT_TILE, PACK_WORDS), tile_map),
                  pl.BlockSpec(memory_space=pl.ANY),
                  pl.BlockSpec((1, 1, 2 * D_FF), exp_map),
                  pl.BlockSpec(memory_space=pl.ANY),
                  pl.BlockSpec((1, 1, d), exp_map)],
        out_specs=pl.BlockSpec((EXPERT_TILE, PACK_WORDS), lambda i, te_ref, nact_ref, valid_ref: (i, 0)),
        scratch_shapes=[pltpu.VMEM((2, d, 2 * D_FF), F32), pltpu.VMEM((2, D_FF, d), F32),
                        pltpu.SMEM((1,), jnp.int32), pltpu.SemaphoreType.DMA((2, 2))],
    )
    return pl.pallas_call(
        _expert_kernel,
        grid_spec=grid_spec,
        out_shape=jax.ShapeDtypeStruct(xs.shape, jnp.uint32),
        compiler_params=_cparams(("arbitrary",)),
        name="expert_mlp",
    )(te, nact, valid, xs, w_gu, b_gu, w_dn, b_dn)


def _combine_kernel(gfin_ref, *refs):
    out_ref = refs[-1]
    for s in range(COMBINE_SUB):
        h1_ref, info_ref, yg_ref = refs[3 * s:3 * s + 3]
        info = info_ref[...]
        h2 = h1_ref[...]
        for kk in range(TOP_K):
            h2 = h2 + info[:, kk:kk + 1] * _unpack_rows(yg_ref[kk])
        ms = jnp.mean(h2 * h2, axis=-1, keepdims=True)
        out_ref[s * CHUNK:(s + 1) * CHUNK, :] = h2 * lax.rsqrt(ms + RMS_EPS) * gfin_ref[...]


def _combine(h1, info, g_final, yg, n_batch, seq):
    d = h1.shape[1]
    chunks = seq // CHUNK

    def padded(s):
        def index(j):
            c = j * COMBINE_SUB + s
            return (c // chunks) * (chunks + 1) + c % chunks + 1
        return index

    in_specs = [pl.BlockSpec((1, d), lambda j: (0, 0))]
    operands = [g_final]
    for s in range(COMBINE_SUB):
        chunk_of = padded(s)
        in_specs += [pl.BlockSpec((CHUNK, d), lambda j, f=chunk_of: (f(j), 0)),
                     pl.BlockSpec((CHUNK, LANES), lambda j, f=chunk_of: (f(j), 0)),
                     pl.BlockSpec((TOP_K, CHUNK, PACK_WORDS), lambda j, f=chunk_of: (0, f(j), 0))]
        operands += [h1, info, yg]
    return pl.pallas_call(
        _combine_kernel,
        grid=(n_batch * chunks // COMBINE_SUB,),
        in_specs=in_specs,
        out_specs=pl.BlockSpec((COMBINE_SUB * CHUNK, d), lambda j: (j, 0)),
        out_shape=jax.ShapeDtypeStruct((n_batch * seq, d), F32),
        compiler_params=_cparams(("parallel",)),
        name="combine",
    )(*operands)


def kernel(x, meta_tokens, lb_logits, g_mix, w_in, w_dw, b_dw, ln_g, ln_b, w_conv_out, b_conv_out,
           g_onorm, w_rnn_out, w_o, g_ffn, w_router, b_router, w_gate_up, b_gate_up, w_down, b_down,
           g_final):
    n_batch, seq, d = x.shape
    assert d == D_MODEL and w_in.shape[0] == 1, "single-layer block with D_MODEL features"
    assert seq % (CHUNK * COMBINE_SUB) == 0
    seq_pad = CHUNK + seq
    tp = n_batch * seq_pad
    assert tp % ROW_TILE == 0

    x2 = x.reshape(n_batch * seq, d)
    meta = meta_tokens.astype(x.dtype)
    vec = lambda a: a.reshape(1, -1).astype(F32)

    aglu, q, k, logf, v, og, sga, sgb = _in_proj(x2, meta, seq, vec(g_mix[0]), lb_logits.astype(F32),
                                                 w_in[0].astype(BF16))
    w_taps = w_dw[0].astype(F32).reshape(CONV_WIDTH, d // LANES, LANES).transpose(1, 0, 2)
    ap = _conv_branch(aglu, w_taps, vec(b_dw[0]), vec(ln_g[0]), vec(ln_b[0]),
                      w_conv_out[0].astype(BF16), vec(b_conv_out[0]), sga)
    on = _hgrn2_scan(q, k, v, logf, og, vec(g_onorm[0]), n_batch)
    w_router_pad = jnp.pad(w_router[0].astype(F32), ((0, 0), (0, LANES - N_EXPERTS)))
    b_router_pad = jnp.pad(vec(b_router[0]), ((0, 0), (0, LANES - N_EXPERTS)))
    h1, u2p, info, cnt = _merge_route(x2, meta, seq, on, ap, sgb, w_rnn_out[0].astype(BF16), w_o[0].astype(BF16),
                                      vec(g_ffn[0]), w_router_pad, b_router_pad)

    n_tiles_max = -(-(tp * TOP_K + N_EXPERTS * (EXPERT_TILE - 1)) // EXPERT_TILE)
    dest, te = _slots(info, cnt, n_tiles_max)
    dest_flat = dest[:, :TOP_K].T.reshape(-1)
    xs = _dispatch(dest_flat, u2p, n_tiles_max * EXPERT_TILE)
    ys = _expert_mlp(te[0, :n_tiles_max], te[1, :1], te[2, :n_tiles_max], xs, w_gate_up[0],
                     b_gate_up[0].reshape(N_EXPERTS, 1, -1).astype(F32), w_down[0],
                     b_down[0].reshape(N_EXPERTS, 1, -1).astype(F32))
    yg = _gather_expert_rows(dest_flat, ys)
    out = _combine(h1, info, vec(g_final), yg, n_batch, seq)
    return out.reshape(n_batch, seq, d)
```

```python
import functools

import jax
import jax.numpy as jnp
from jax import lax
from jax.experimental import pallas as pl
from jax.experimental.pallas import tpu as pltpu
from jax.experimental.pallas import tpu_sc as plsc

F32 = jnp.float32
BF16 = jnp.bfloat16

D_MODEL = 1024
N_META = 16
CHUNK = 128
CHUNK_PAD = CHUNK - N_META
CONV_WIDTH = 31
HEAD_DIM = 128
N_HEADS = D_MODEL // HEAD_DIM
N_EXPERTS = 32
TOP_K = 4
D_FF = D_MODEL
SWIGLU_LIMIT = 7.0
SWIGLU_ALPHA = 1.702
RMS_EPS = 1e-6
LN_EPS = 1e-5

ACT_AGLU, ACT_Q, ACT_K, ACT_V, ACT_OG, ACT_SGA, ACT_SGB = range(7)
N_ACTS = 7

LANES = 128
SUBLANES = 8
SUB_BLOCK = 32
N_SUB = CHUNK // SUB_BLOCK
HALO = 32
ROW_TILE = 640
IN_TILE = 320
SCAN_CHUNKS = 5
MERGE_PARTS = 2
SLOTS_STEPS = 5
CONV_ROWS = 64
CONV_SUMS = 4
EXPERT_TILE = 512
SC_SCATTER_CHUNK = 104
SC_GATHER_CHUNK = 40
COMBINE_SUB = 4
NEG_BIG = -1e30
VMEM_LIMIT = 56 * 1024 * 1024


def _sigmoid(x):
    return 1.0 / (1.0 + jnp.exp(-x))


def _cparams(sem):
    return pltpu.CompilerParams(dimension_semantics=sem, vmem_limit_bytes=VMEM_LIMIT)


def _const_spec(shape):
    nd = len(shape)
    return pl.BlockSpec(shape, lambda *_: (0,) * nd)


PACK_WORDS = D_MODEL // 2
HIGH_HALF = 0xFFFF0000


def _pack_rows(x):
    lo = lax.bitcast_convert_type(x[:, :PACK_WORDS].astype(BF16).astype(F32), jnp.uint32)
    hi = lax.bitcast_convert_type(x[:, PACK_WORDS:].astype(BF16).astype(F32), jnp.uint32)
    return (lo >> 16) | (hi & jnp.uint32(HIGH_HALF))


def _unpack_rows(w):
    lo = lax.bitcast_convert_type(w << 16, F32)
    hi = lax.bitcast_convert_type(w & jnp.uint32(HIGH_HALF), F32)
    return jnp.concatenate([lo, hi], axis=1)


def _residual_tile(x_ref, meta_ref, is_first):
    x = x_ref[...]
    tm, d = x.shape
    prefix = jnp.concatenate([jnp.zeros((CHUNK_PAD, d), x.dtype), meta_ref[...]], axis=0)
    first = jnp.concatenate([prefix, x[:tm - CHUNK, :]], axis=0)
    return jnp.where(is_first, first, x)


def _residual_spec(tm, d, seq, tiles_per_batch, n_tiles):
    def start(i):
        i = jnp.minimum(i, n_tiles - 1)
        b, t = i // tiles_per_batch, i % tiles_per_batch
        return (pl.multiple_of(b * seq + jnp.maximum(t * tm - CHUNK, 0), SUBLANES), 0)
    return pl.BlockSpec((pl.Element(tm), pl.Element(d)), start)


def _in_proj_kernel(x_ref, meta_ref, g_ref, lbl_ref, w_ref, acts_ref, logf_ref, *, tiles_per_batch):
    d = D_MODEL

    def put(group, value):
        acts_ref[:, group * d:(group + 1) * d] = value.astype(acts_ref.dtype)

    h = _residual_tile(x_ref, meta_ref, pl.program_id(0) % tiles_per_batch == 0)
    ms = jnp.mean(h * h, axis=-1, keepdims=True)
    u = (h * lax.rsqrt(ms + RMS_EPS) * g_ref[...]).astype(BF16)

    def proj(j):
        return jnp.dot(u, w_ref[:, j * d:(j + 1) * d], preferred_element_type=F32)

    put(ACT_AGLU, proj(0) * _sigmoid(proj(1)))
    zq = proj(2)
    put(ACT_Q, zq * _sigmoid(zq))
    lbl = lbl_ref[...]
    e = jnp.exp(lbl - jnp.max(lbl, axis=0, keepdims=True))
    lb = e[0:1, :] / jnp.sum(e, axis=0, keepdims=True)
    s = _sigmoid(proj(3))
    logf_ref[...] = jnp.log(lb + (1.0 - lb) * s)
    put(ACT_K, (1.0 - lb) * (1.0 - s))
    put(ACT_V, proj(4))
    zg = proj(5)
    put(ACT_OG, zg * _sigmoid(zg))
    put(ACT_SGA, _sigmoid(proj(6)))
    put(ACT_SGB, _sigmoid(proj(7)))


def _in_proj(x2, meta, seq, g_mix, lb_logits, w_in_bf16):
    d = x2.shape[1]
    n_batch = x2.shape[0] // seq
    tm = IN_TILE
    tpb = (seq + CHUNK) // tm
    tp = n_batch * (seq + CHUNK)
    assert tpb * tm == seq + CHUNK and tm > CHUNK
    row = pl.BlockSpec((tm, d), lambda i: (i, 0))
    return pl.pallas_call(
        functools.partial(_in_proj_kernel, tiles_per_batch=tpb),
        grid=(tp // tm,),
        in_specs=[_residual_spec(tm, d, seq, tpb, tp // tm), _const_spec(meta.shape), _const_spec((1, d)),
                  _const_spec(lb_logits.shape),
                  pl.BlockSpec(w_in_bf16.shape, lambda i: (0, 0), pipeline_mode=pl.Buffered(1))],
        out_specs=[pl.BlockSpec((tm, N_ACTS * d), lambda i: (i, 0)), row],
        out_shape=[jax.ShapeDtypeStruct((tp, N_ACTS * d), BF16), jax.ShapeDtypeStruct((tp, d), F32)],
        compiler_params=_cparams(("parallel",)),
        name="in_proj",
    )(x2, meta, g_mix, lb_logits, w_in_bf16)


def _conv_kernel(halo_ref, cur_ref, wdw_ref, bdw_ref, lng_ref, lnb_ref, wout_ref, bout_ref, sga_ref,
                 out_ref, win_ref, acc_ref, shift_ref):
    tm = cur_ref.shape[0]
    win_ref[0:HALO, :] = halo_ref[...].astype(F32)
    win_ref[HALO:, :] = cur_ref[...].astype(F32)
    first_tap = HALO - (CONV_WIDTH - 1)
    for c in range(D_MODEL // LANES):
        lanes = slice(c * LANES, (c + 1) * LANES)
        n_rows = tm + HALO - SUBLANES
        for s in range(1, SUBLANES):
            shift_ref[s - 1, 0:n_rows, :] = win_ref[pl.ds(s, n_rows), lanes]
        bias = jnp.broadcast_to(bdw_ref[:, lanes], (CONV_ROWS, LANES))
        groups = CONV_ROWS // SUBLANES

        def chunk(r, carry, c=c, lanes=lanes, bias=bias):
            r0 = pl.multiple_of(r * CONV_ROWS, CONV_ROWS)
            accs = [bias.reshape(groups, SUBLANES, LANES)] + [None] * (CONV_SUMS - 1)
            for j in range(CONV_WIDTH):
                off = first_tap + j
                rows = pl.ds(r0 + off - off % SUBLANES, CONV_ROWS)
                w_j = wdw_ref[c, pl.ds(j, SUBLANES, stride=0), :]
                window = shift_ref[off % SUBLANES - 1, rows, :] if off % SUBLANES else win_ref[rows, lanes]
                term = w_j[None] * window.reshape(groups, SUBLANES, LANES)
                accs[j % CONV_SUMS] = term if accs[j % CONV_SUMS] is None else accs[j % CONV_SUMS] + term
            while len(accs) > 1:
                accs = [a + b for a, b in zip(accs[0::2], accs[1::2])]
            acc_ref[pl.ds(r0, CONV_ROWS), lanes] = accs[0].reshape(CONV_ROWS, LANES)
            return carry

        lax.fori_loop(0, tm // CONV_ROWS, chunk, 0)
    a = acc_ref[...]
    mu = jnp.mean(a, axis=-1, keepdims=True)
    ac = a - mu
    var = jnp.mean(ac * ac, axis=-1, keepdims=True)
    y = ac * lax.rsqrt(var + LN_EPS) * lng_ref[...] + lnb_ref[...]
    y = y * _sigmoid(y)
    o = jnp.dot(y.astype(BF16), wout_ref[...], preferred_element_type=F32) + bout_ref[...]
    out_ref[...] = (sga_ref[...].astype(F32) * o).astype(out_ref.dtype)


def _conv_branch(acts, w_dw, b_dw, ln_g, ln_b, w_out_bf16, b_out):
    tp, d = acts.shape[0], D_MODEL
    tm = ROW_TILE
    per = tm // HALO
    row = pl.BlockSpec((tm, d), lambda i: (i, 0))
    glu = pl.BlockSpec((tm, d), lambda i: (i, ACT_AGLU))
    gate = pl.BlockSpec((tm, d), lambda i: (i, ACT_SGA))
    halo = pl.BlockSpec((HALO, d), lambda i: (jnp.maximum(i * per - 1, 0), ACT_AGLU))
    vec = _const_spec((1, d))
    return pl.pallas_call(
        _conv_kernel,
        grid=(tp // tm,),
        in_specs=[halo, glu, _const_spec(w_dw.shape), vec, vec, vec, _const_spec((d, d)), vec, gate],
        out_specs=row,
        out_shape=jax.ShapeDtypeStruct((tp, d), BF16),
        scratch_shapes=[pltpu.VMEM((tm + HALO, d), F32), pltpu.VMEM((tm, d), F32),
                        pltpu.VMEM((SUBLANES - 1, tm + HALO - SUBLANES, LANES), F32)],
        compiler_params=_cparams(("parallel",)),
        name="conv_branch",
    )(acts, acts, w_dw, b_dw, ln_g, ln_b, w_out_bf16, b_out, acts)


def _split_bf16(x):
    hi = x.astype(BF16)
    return hi, (x - hi.astype(F32)).astype(BF16)


def _nt_dot(a, b):
    return lax.dot_general(a, b, (((1,), (1,)), ((), ())), preferred_element_type=F32)


def _tn_dot(a, b):
    return lax.dot_general(a, b, (((0,), (0,)), ((), ())), preferred_element_type=F32)


def _scan_kernel(q_ref, k_ref, v_ref, lf_ref, og_ref, gon_ref, o_ref, st_ref):
    @pl.when(pl.program_id(1) == 0)
    def _():
        st_ref[...] = jnp.zeros_like(st_ref)

    c = CHUNK
    row = lax.broadcasted_iota(jnp.int32, (c, c), 0)
    col = lax.broadcasted_iota(jnp.int32, (c, c), 1)
    causal = col <= row
    diag_mask = jnp.logical_and(causal, row // SUB_BLOCK == col // SUB_BLOCK)
    tri = causal.astype(BF16)

    def bcast_rows(rows):
        return jnp.concatenate([jnp.broadcast_to(r, (SUB_BLOCK, HEAD_DIM)) for r in rows], axis=0)

    heads = [slice(h * HEAD_DIM, (h + 1) * HEAD_DIM) for h in range(N_HEADS)]
    chunks = [slice(ci * c, (ci + 1) * c) for ci in range(SCAN_CHUNKS)]

    bcums = []
    for cs in chunks:
        lf = lf_ref[cs, :]
        lf_hi = lf.astype(BF16)
        lf_mid, lf_lo = _split_bf16(lf - lf_hi.astype(F32))
        bcums.append(jnp.dot(tri, lf_hi, preferred_element_type=F32)
                     + (jnp.dot(tri, lf_mid, preferred_element_type=F32)
                        + jnp.dot(tri, lf_lo, preferred_element_type=F32)))
    operands = {}
    for ci, cs in enumerate(chunks):
        for h, hs in enumerate(heads):
            b = bcums[ci][:, hs]
            q = q_ref[cs, hs].astype(F32)
            k = k_ref[cs, hs].astype(F32)
            ends = [b[i * SUB_BLOCK + SUB_BLOCK - 1:i * SUB_BLOCK + SUB_BLOCK, :] for i in range(N_SUB)]
            mids = [b[i * SUB_BLOCK + SUB_BLOCK // 2 - 1:i * SUB_BLOCK + SUB_BLOCK // 2, :] for i in range(N_SUB)]
            mid_full = bcast_rows(mids)
            b_last = ends[-1]
            zero_row = jnp.zeros_like(b_last)
            qm = q * jnp.exp(b - mid_full)
            km = k * jnp.exp(mid_full - b)
            q_parts, k_parts = [], []
            for j in range(N_SUB - 1):
                q_rows = [jnp.exp(mids[i] - ends[j]) if i > j else zero_row for i in range(N_SUB)]
                k_rows = [jnp.exp(ends[j] - mids[j]) if i == j else zero_row for i in range(N_SUB)]
                q_parts.append((qm * bcast_rows(q_rows)).astype(BF16))
                k_parts.append((km * bcast_rows(k_rows)).astype(BF16))
            q_in = (qm * bcast_rows([jnp.exp(m) for m in mids])).astype(BF16)
            k_out = (km * bcast_rows([jnp.exp(b_last - m) for m in mids])).astype(BF16)
            operands[ci, h] = (qm.astype(BF16), km.astype(BF16), jnp.concatenate(q_parts, axis=1),
                               jnp.concatenate(k_parts, axis=1), q_in, k_out, jnp.exp(b_last))
    products = {}
    for ci, cs in enumerate(chunks):
        for h, hs in enumerate(heads):
            qm, km, q_cat, k_cat, _, k_out, _ = operands[ci, h]
            products[ci, h] = (_nt_dot(qm, km), _nt_dot(q_cat, k_cat), _tn_dot(v_ref[cs, hs], k_out))
    within = {}
    for ci, cs in enumerate(chunks):
        for h, hs in enumerate(heads):
            same_block, earlier_blocks, _ = products[ci, h]
            scores = jnp.where(diag_mask, same_block, 0.0) + earlier_blocks
            within[ci, h] = jnp.dot(scores.astype(BF16), v_ref[cs, hs], preferred_element_type=F32)
    states = [st_ref[h] for h in range(N_HEADS)]
    outs = {}
    for ci in range(SCAN_CHUNKS):
        for h in range(N_HEADS):
            outs[ci, h] = within[ci, h] + _nt_dot(operands[ci, h][4], states[h].astype(BF16))
            states[h] = states[h] * operands[ci, h][6] + products[ci, h][2]
    for h in range(N_HEADS):
        st_ref[h] = states[h]
    for ci, cs in enumerate(chunks):
        for h, hs in enumerate(heads):
            o = outs[ci, h]
            ms = jnp.mean(o * o, axis=-1, keepdims=True)
            on = o * lax.rsqrt(ms + RMS_EPS) * gon_ref[:, hs]
            o_ref[cs, hs] = (on * og_ref[cs, hs].astype(F32)).astype(o_ref.dtype)


def _hgrn2_scan(acts, logf, g_onorm, n_batch):
    tp, d = logf.shape
    rows = SCAN_CHUNKS * CHUNK
    n_chunks = tp // n_batch // rows
    assert n_chunks * rows * n_batch == tp
    col = lambda group: pl.BlockSpec((rows, d), lambda b, c: (b * n_chunks + c, group))
    blk = col(0)
    return pl.pallas_call(
        _scan_kernel,
        grid=(n_batch, n_chunks),
        in_specs=[col(ACT_Q), col(ACT_K), col(ACT_V), blk, col(ACT_OG), pl.BlockSpec((1, d), lambda b, c: (0, 0))],
        out_specs=blk,
        out_shape=jax.ShapeDtypeStruct((tp, d), BF16),
        scratch_shapes=[pltpu.VMEM((N_HEADS, HEAD_DIM, HEAD_DIM), F32)],
        compiler_params=_cparams(("arbitrary", "arbitrary")),
        name="hgrn2_scan",
    )(acts, acts, acts, logf, acts, g_onorm)


def _merge_route_kernel(x_ref, meta_ref, on_ref, ap_ref, sgb_ref, wrnn_ref, wo_ref, gffn_ref, wr_ref, br_ref,
                        h1_ref, u2_ref, info_ref, cnt_ref, tri_ref, carry_ref, colcnt_ref, logits_ref, hres_ref,
                        *, tiles_per_batch, n_tiles):
    tm = on_ref.shape[0]
    i = pl.program_id(0)
    hres_ref[...] = _residual_tile(x_ref, meta_ref, jnp.minimum(i, n_tiles - 1) % tiles_per_batch == 0)

    @pl.when(i == 0)
    def _():
        r_i = lax.broadcasted_iota(jnp.int32, (tm, tm), 0)
        c_i = lax.broadcasted_iota(jnp.int32, (tm, tm), 1)
        tri_ref[...] = (c_i < r_i).astype(BF16)
        carry_ref[...] = jnp.zeros_like(carry_ref)
        colcnt_ref[...] = jnp.zeros_like(colcnt_ref)
        logits_ref[...] = jnp.zeros_like(logits_ref)

    routed = i > 0
    lane = lax.broadcasted_iota(jnp.int32, (tm, LANES), 1)
    cur = jnp.where(lane < N_EXPERTS, logits_ref[...], NEG_BIG)
    vals, idxs, sels = [], [], []

    def topk_round(cur):
        m = jnp.max(cur, axis=-1, keepdims=True)
        idx = jnp.min(jnp.where(cur == m, lane, LANES), axis=-1, keepdims=True)
        sel = lane == idx
        vals.append(m)
        idxs.append(idx)
        sels.append(sel)
        return jnp.where(sel, 2.0 * NEG_BIG, cur)

    w_hi, w_lo = _split_bf16(wr_ref[...])
    part = tm // MERGE_PARTS
    rows = [slice(p * part, (p + 1) * part) for p in range(MERGE_PARTS)]
    rs = [jnp.dot(on_ref[rw, :], wrnn_ref[...], preferred_element_type=F32) for rw in rows]
    cur = topk_round(cur)
    ys = [(ap_ref[rw, :].astype(F32) + sgb_ref[rw, :].astype(F32) * r).astype(BF16) for rw, r in zip(rows, rs)]
    h1s = [hres_ref[rw, :] + jnp.dot(y, wo_ref[...], preferred_element_type=F32) for rw, y in zip(rows, ys)]
    cur = topk_round(cur)
    u2s = []
    for rw, h1 in zip(rows, h1s):
        h1_ref[rw, :] = h1
        ms = jnp.mean(h1 * h1, axis=-1, keepdims=True)
        u2 = h1 * lax.rsqrt(ms + RMS_EPS) * gffn_ref[...]
        u2_ref[rw, :] = _pack_rows(u2)
        u2s.append(u2)
    cur = topk_round(cur)
    new_logits = []
    for u2 in u2s:
        u_hi, u_lo = _split_bf16(u2)
        new_logits.append(jnp.dot(u_hi, w_hi, preferred_element_type=F32)
                          + (jnp.dot(u_lo, w_hi, preferred_element_type=F32)
                             + jnp.dot(u_hi, w_lo, preferred_element_type=F32))
                          + br_ref[...])
    cur = topk_round(cur)
    assert len(vals) == TOP_K
    exps = [jnp.exp(vk - vals[0]) for vk in vals]
    den = exps[0] + exps[1] + exps[2] + exps[3]
    onehot = jnp.logical_or(jnp.logical_or(sels[0], sels[1]), jnp.logical_or(sels[2], sels[3]))
    onehot = jnp.logical_and(onehot, routed)
    onehot_bf = onehot.astype(BF16)
    rank_all = jnp.dot(tri_ref[...], onehot_bf, preferred_element_type=F32) + carry_ref[...]
    carry_ref[...] += jnp.sum(onehot.astype(F32), axis=0, keepdims=True)
    colcnt_ref[...] += _tn_dot(onehot_bf, jnp.ones((tm, LANES), BF16))
    info = jnp.zeros((tm, LANES), F32)
    for kk in range(TOP_K):
        rank_k = jnp.sum(jnp.where(sels[kk], rank_all, 0.0), axis=-1, keepdims=True)
        info = jnp.where(lane == kk, exps[kk] / den, info)
        info = jnp.where(lane == TOP_K + kk, idxs[kk].astype(F32), info)
        info = jnp.where(lane == 2 * TOP_K + kk, rank_k, info)
    info_ref[...] = info
    cnt_ref[...] = colcnt_ref[...]
    logits_ref[...] = jnp.concatenate(new_logits, axis=0)


def _merge_route(x2, meta, seq, on, ap, acts, w_rnn_bf16, w_o_bf16, g_ffn, w_router_pad, b_router_pad):
    tp, d = on.shape
    tm = ROW_TILE
    n_tiles = tp // tm
    tpb = (seq + CHUNK) // tm
    assert tpb * tm == seq + CHUNK
    this_tile = lambda i: (jnp.minimum(i, n_tiles - 1), 0)
    prev_tile = lambda i: (jnp.maximum(i - 1, 0), 0)
    row = pl.BlockSpec((tm, d), this_tile)
    vec = _const_spec((1, d))
    return pl.pallas_call(
        functools.partial(_merge_route_kernel, tiles_per_batch=tpb, n_tiles=n_tiles),
        grid=(n_tiles + 1,),
        in_specs=[_residual_spec(tm, d, seq, tpb, n_tiles), _const_spec(meta.shape), row, row,
                  pl.BlockSpec((tm, d), lambda i: (jnp.minimum(i, n_tiles - 1), ACT_SGB)),
                  _const_spec((d, d)), _const_spec((d, d)), vec, _const_spec((d, LANES)), _const_spec((1, LANES))],
        out_specs=[row, pl.BlockSpec((tm, PACK_WORDS), this_tile),
                   pl.BlockSpec((tm, LANES), prev_tile), _const_spec((LANES, LANES))],
        out_shape=[jax.ShapeDtypeStruct((tp, d), F32), jax.ShapeDtypeStruct((tp, PACK_WORDS), jnp.uint32),
                   jax.ShapeDtypeStruct((tp, LANES), F32), jax.ShapeDtypeStruct((LANES, LANES), F32)],
        scratch_shapes=[pltpu.VMEM((tm, tm), BF16), pltpu.VMEM((1, LANES), F32),
                        pltpu.VMEM((LANES, LANES), F32), pltpu.VMEM((tm, LANES), F32), pltpu.VMEM((tm, d), F32)],
        compiler_params=_cparams(("arbitrary",)),
        name="merge_route",
    )(x2, meta, on, ap, acts, w_rnn_bf16, w_o_bf16, g_ffn, w_router_pad, b_router_pad)


def _slots_kernel(info_ref, cnt_ref, dest_ref, te_ref, pstart_ref):
    tm = info_ref.shape[0]
    n_tile_lanes = te_ref.shape[1]

    @pl.when(pl.program_id(0) == 0)
    def _():
        r_i = lax.broadcasted_iota(jnp.int32, (LANES, LANES), 0)
        c_i = lax.broadcasted_iota(jnp.int32, (LANES, LANES), 1)
        cnt = cnt_ref[...]
        tiles = jnp.floor((cnt + (EXPERT_TILE - 1)) / EXPERT_TILE)
        pend_col = jnp.dot((c_i <= r_i).astype(F32), tiles, preferred_element_type=F32,
                           precision=lax.Precision.HIGHEST)
        pstart_ref[...] = _tn_dot_f32(tiles, (r_i < c_i).astype(F32))[0:SUBLANES, :]
        tile_id = lax.broadcasted_iota(jnp.int32, (LANES, n_tile_lanes), 1).astype(F32)
        exp_id = lax.broadcasted_iota(jnp.int32, (LANES, n_tile_lanes), 0)
        pend_wide = jnp.concatenate([pend_col] * (n_tile_lanes // LANES), axis=1)
        below = jnp.logical_and(pend_wide <= tile_id, exp_id < N_EXPERTS)
        te = jnp.minimum(jnp.sum(below.astype(F32), axis=0, keepdims=True), N_EXPERTS - 1.0)
        n_active = pend_col[N_EXPERTS - 1:N_EXPERTS, 0:1]
        owner = exp_id.astype(F32) == te
        cnt_wide = jnp.concatenate([cnt] * (n_tile_lanes // LANES), axis=1)
        first_tile = pend_wide - jnp.concatenate([tiles] * (n_tile_lanes // LANES), axis=1)
        left = jnp.sum(jnp.where(owner, cnt_wide - (tile_id - first_tile) * EXPERT_TILE, 0.0), axis=0, keepdims=True)
        valid = jnp.clip(left, 0.0, float(EXPERT_TILE))
        sub_t = lax.broadcasted_iota(jnp.int32, (SUBLANES, n_tile_lanes), 0)
        out = jnp.where(sub_t == 0, jnp.broadcast_to(te, (SUBLANES, n_tile_lanes)),
                        jnp.where(sub_t == 1, jnp.broadcast_to(n_active, (SUBLANES, n_tile_lanes)),
                                  jnp.broadcast_to(valid, (SUBLANES, n_tile_lanes))))
        te_ref[...] = out.astype(jnp.int32)

    info = info_ref[...]
    pstart_row = pstart_ref[0:1, :]
    lane = lax.broadcasted_iota(jnp.int32, (tm, LANES), 1)
    dest = jnp.zeros((tm, LANES), F32)
    for kk in range(TOP_K):
        idx_k = info[:, TOP_K + kk:TOP_K + kk + 1].astype(jnp.int32)
        start_k = jnp.sum(jnp.where(lane == idx_k, pstart_row, 0.0), axis=-1, keepdims=True)
        slot_k = start_k * EXPERT_TILE + info[:, 2 * TOP_K + kk:2 * TOP_K + kk + 1]
        dest = jnp.where(lane == kk, slot_k, dest)
    dest_ref[...] = dest.astype(jnp.int32)


def _tn_dot_f32(a, b):
    return lax.dot_general(a, b, (((0,), (0,)), ((), ())), preferred_element_type=F32,
                           precision=lax.Precision.HIGHEST)


def _slots(info, cnt, n_tiles_max):
    tp = info.shape[0]
    tm = tp // SLOTS_STEPS
    n_tile_lanes = -(-n_tiles_max // LANES) * LANES
    return pl.pallas_call(
        _slots_kernel,
        grid=(tp // tm,),
        in_specs=[pl.BlockSpec((tm, LANES), lambda i: (i, 0)), _const_spec((LANES, LANES))],
        out_specs=[pl.BlockSpec((tm, LANES), lambda i: (i, 0)), _const_spec((8, n_tile_lanes))],
        out_shape=[jax.ShapeDtypeStruct((tp, LANES), jnp.int32),
                   jax.ShapeDtypeStruct((8, n_tile_lanes), jnp.int32)],
        scratch_shapes=[pltpu.VMEM((SUBLANES, LANES), F32)],
        compiler_params=_cparams(("arbitrary",)),
        name="slots",
    )(info, cnt)


def _sc_workers():
    sc = plsc.get_sparse_core_info()
    return sc.num_cores, sc.num_cores * sc.num_subcores


def _sc_token_chunks(tp, chunk, body):
    n_cores, n_workers = _sc_workers()
    per_worker = tp // n_workers
    assert per_worker * n_workers == tp and per_worker % chunk == 0 and chunk % SUBLANES == 0
    base = (lax.axis_index("s") * n_cores + lax.axis_index("c")) * per_worker

    @pl.loop(0, per_worker // chunk)
    def _(ci):
        body(pl.multiple_of(base + ci * chunk, SUBLANES))


def _sc_kernel(out_type, chunk, n_row_bufs):
    return functools.partial(
        pl.kernel, mesh=plsc.VectorSubcoreMesh(core_axis_name="c", subcore_axis_name="s"), out_type=out_type,
        scratch_types=[pltpu.VMEM((TOP_K, chunk), jnp.int32), pltpu.VMEM((n_row_bufs, chunk, PACK_WORDS), jnp.uint32),
                       pltpu.SemaphoreType.DMA, pltpu.SemaphoreType.DMA])


def _wait_all(copies):
    for c in copies:
        c.wait()


def _dispatch(dest_flat, u2p, n_slots):
    tp = u2p.shape[0]
    ch = SC_SCATTER_CHUNK

    @_sc_kernel(jax.ShapeDtypeStruct((n_slots, PACK_WORDS), jnp.uint32), ch, 1)
    def scatter_rows(u2_hbm, dest_hbm, xs_hbm, idx_v, rows_v, sem_a, sem_b):
        def chunk(off):
            loads = [pltpu.async_copy(u2_hbm.at[pl.ds(off, ch)], rows_v.at[0], sem_a)]
            loads += [pltpu.async_copy(dest_hbm.at[pl.ds(kk * tp + off, ch)], idx_v.at[kk], sem_a)
                      for kk in range(TOP_K)]
            _wait_all(loads)
            _wait_all([pltpu.async_copy(rows_v.at[0], xs_hbm.at[idx_v.at[kk]], sem_b) for kk in range(TOP_K)])

        _sc_token_chunks(tp, ch, chunk)

    return scatter_rows(u2p, dest_flat)


def _gather_expert_rows(dest_flat, ys):
    tp = dest_flat.shape[0] // TOP_K
    ch = SC_GATHER_CHUNK

    @_sc_kernel(jax.ShapeDtypeStruct((TOP_K, tp, PACK_WORDS), jnp.uint32), ch, TOP_K)
    def gather_rows(ys_hbm, dest_hbm, out_hbm, idx_v, rows_v, sem_a, sem_b):
        def chunk(off):
            _wait_all([pltpu.async_copy(dest_hbm.at[pl.ds(kk * tp + off, ch)], idx_v.at[kk], sem_a)
                       for kk in range(TOP_K)])
            _wait_all([pltpu.async_copy(ys_hbm.at[idx_v.at[kk]], rows_v.at[kk], sem_b) for kk in range(TOP_K)])
            _wait_all([pltpu.async_copy(rows_v.at[kk], out_hbm.at[kk, pl.ds(off, ch)], sem_a)
                       for kk in range(TOP_K)])

        _sc_token_chunks(tp, ch, chunk)

    return gather_rows(ys, dest_flat)


def _expert_kernel(te_ref, nact_ref, valid_ref, x_ref, wgu_hbm, bgu_ref, wdn_hbm, bdn_ref, y_ref,
                   wgu_buf, wdn_buf, slot_ref, sem):
    i = pl.program_id(0)
    n_active = nact_ref[0]
    last_tile = pl.num_programs(0) - 1

    def fetch(expert, slot):
        return (pltpu.make_async_copy(wgu_hbm.at[expert], wgu_buf.at[slot], sem.at[slot, 0]),
                pltpu.make_async_copy(wdn_hbm.at[expert], wdn_buf.at[slot], sem.at[slot, 1]))

    @pl.when(i == 0)
    def _():
        slot_ref[0] = 1
        for copy in fetch(te_ref[0], 0):
            copy.start()

    @pl.when(i < n_active)
    def _():
        expert = te_ref[i]
        first_tile_of_expert = jnp.logical_or(i == 0, expert != te_ref[jnp.maximum(i - 1, 0)])

        @pl.when(first_tile_of_expert)
        def _():
            slot = 1 - slot_ref[0]
            slot_ref[0] = slot
            for copy in fetch(expert, slot):
                copy.wait()
            nxt = lax.while_loop(
                lambda j: jnp.logical_and(j < n_active, te_ref[jnp.minimum(j, last_tile)] == expert),
                lambda j: j + 1, i + 1)

            @pl.when(nxt < n_active)
            def _():
                for copy in fetch(te_ref[jnp.minimum(nxt, last_tile)], 1 - slot):
                    copy.start()

        slot = slot_ref[0]

        def mlp(rows):
            x = _unpack_rows(x_ref[0:rows, :]).astype(BF16)
            gu = jnp.dot(x, wgu_buf[slot].astype(BF16), preferred_element_type=F32) + bgu_ref[0]
            gate = jnp.minimum(gu[:, :D_FF], SWIGLU_LIMIT)
            up = jnp.clip(gu[:, D_FF:], -SWIGLU_LIMIT, SWIGLU_LIMIT)
            hdn = (up + 1.0) * (gate * _sigmoid(SWIGLU_ALPHA * gate))
            y = jnp.dot(hdn.astype(BF16), wdn_buf[slot].astype(BF16), preferred_element_type=F32) + bdn_ref[0]
            y_ref[0:rows, :] = _pack_rows(y)

        half = EXPERT_TILE // 2
        half_empty = valid_ref[i] <= half
        pl.when(jnp.logical_not(half_empty))(lambda: mlp(EXPERT_TILE))

        @pl.when(half_empty)
        def _():
            mlp(half)
            y_ref[half:, :] = jnp.zeros((EXPERT_TILE - half, PACK_WORDS), y_ref.dtype)

    @pl.when(i >= n_active)
    def _():
        y_ref[...] = jnp.zeros_like(y_ref)


def _expert_mlp(te, nact, valid, xs, w_gu, b_gu, w_dn, b_dn):
    d = D_MODEL
    n_tiles = xs.shape[0] // EXPERT_TILE

    def tile_map(i, te_ref, nact_ref, valid_ref):
        return (jnp.minimum(i, nact_ref[0] - 1), 0)

    def exp_map(i, te_ref, nact_ref, valid_ref):
        return (te_ref[jnp.minimum(i, nact_ref[0] - 1)], 0, 0)

    grid_spec = pltpu.PrefetchScalarGridSpec(
        num_scalar_prefetch=3,
        grid=(n_tiles,),
        in_specs=[pl.BlockSpec((EXPERT_TILE, PACK_WORDS), tile_map),
                  pl.BlockSpec(memory_space=pl.ANY),
                  pl.BlockSpec((1, 1, 2 * D_FF), exp_map),
                  pl.BlockSpec(memory_space=pl.ANY),
                  pl.BlockSpec((1, 1, d), exp_map)],
        out_specs=pl.BlockSpec((EXPERT_TILE, PACK_WORDS), lambda i, te_ref, nact_ref, valid_ref: (i, 0)),
        scratch_shapes=[pltpu.VMEM((2, d, 2 * D_FF), F32), pltpu.VMEM((2, D_FF, d), F32),
                        pltpu.SMEM((1,), jnp.int32), pltpu.SemaphoreType.DMA((2, 2))],
    )
    return pl.pallas_call(
        _expert_kernel,
        grid_spec=grid_spec,
        out_shape=jax.ShapeDtypeStruct(xs.shape, jnp.uint32),
        compiler_params=_cparams(("arbitrary",)),
        name="expert_mlp",
    )(te, nact, valid, xs, w_gu, b_gu, w_dn, b_dn)


def _combine_kernel(gfin_ref, *refs):
    out_ref = refs[-1]
    for s in range(COMBINE_SUB):
        h1_ref, info_ref, yg_ref = refs[3 * s:3 * s + 3]
        info = info_ref[...]
        h2 = h1_ref[...]
        for kk in range(TOP_K):
            h2 = h2 + info[:, kk:kk + 1] * _unpack_rows(yg_ref[kk])
        ms = jnp.mean(h2 * h2, axis=-1, keepdims=True)
        out_ref[s * CHUNK:(s + 1) * CHUNK, :] = h2 * lax.rsqrt(ms + RMS_EPS) * gfin_ref[...]


def _combine(h1, info, g_final, yg, n_batch, seq):
    d = h1.shape[1]
    chunks = seq // CHUNK

    def padded(s):
        def index(j):
            c = j * COMBINE_SUB + s
            return (c // chunks) * (chunks + 1) + c % chunks + 1
        return index

    in_specs = [pl.BlockSpec((1, d), lambda j: (0, 0))]
    operands = [g_final]
    for s in range(COMBINE_SUB):
        chunk_of = padded(s)
        in_specs += [pl.BlockSpec((CHUNK, d), lambda j, f=chunk_of: (f(j), 0)),
                     pl.BlockSpec((CHUNK, LANES), lambda j, f=chunk_of: (f(j), 0)),
                     pl.BlockSpec((TOP_K, CHUNK, PACK_WORDS), lambda j, f=chunk_of: (0, f(j), 0))]
        operands += [h1, info, yg]
    return pl.pallas_call(
        _combine_kernel,
        grid=(n_batch * chunks // COMBINE_SUB,),
        in_specs=in_specs,
        out_specs=pl.BlockSpec((COMBINE_SUB * CHUNK, d), lambda j: (j, 0)),
        out_shape=jax.ShapeDtypeStruct((n_batch * seq, d), F32),
        compiler_params=_cparams(("parallel",)),
        name="combine",
    )(*operands)


def kernel(x, meta_tokens, lb_logits, g_mix, w_in, w_dw, b_dw, ln_g, ln_b, w_conv_out, b_conv_out,
           g_onorm, w_rnn_out, w_o, g_ffn, w_router, b_router, w_gate_up, b_gate_up, w_down, b_down,
           g_final):
    n_batch, seq, d = x.shape
    assert d == D_MODEL and w_in.shape[0] == 1, "single-layer block with D_MODEL features"
    assert seq % (CHUNK * COMBINE_SUB) == 0
    seq_pad = CHUNK + seq
    tp = n_batch * seq_pad
    assert tp % ROW_TILE == 0

    x2 = x.reshape(n_batch * seq, d)
    meta = meta_tokens.astype(x.dtype)
    vec = lambda a: a.reshape(1, -1).astype(F32)

    acts, logf = _in_proj(x2, meta, seq, vec(g_mix[0]), lb_logits.astype(F32), w_in[0].astype(BF16))
    w_taps = w_dw[0].astype(F32).reshape(CONV_WIDTH, d // LANES, LANES).transpose(1, 0, 2)
    ap = _conv_branch(acts, w_taps, vec(b_dw[0]), vec(ln_g[0]), vec(ln_b[0]),
                      w_conv_out[0].astype(BF16), vec(b_conv_out[0]))
    on = _hgrn2_scan(acts, logf, vec(g_onorm[0]), n_batch)
    w_router_pad = jnp.pad(w_router[0].astype(F32), ((0, 0), (0, LANES - N_EXPERTS)))
    b_router_pad = jnp.pad(vec(b_router[0]), ((0, 0), (0, LANES - N_EXPERTS)))
    h1, u2p, info, cnt = _merge_route(x2, meta, seq, on, ap, acts, w_rnn_out[0].astype(BF16), w_o[0].astype(BF16),
                                      vec(g_ffn[0]), w_router_pad, b_router_pad)

    n_tiles_max = -(-(tp * TOP_K + N_EXPERTS * (EXPERT_TILE - 1)) // EXPERT_TILE)
    dest, te = _slots(info, cnt, n_tiles_max)
    dest_flat = dest[:, :TOP_K].T.reshape(-1)
    xs = _dispatch(dest_flat, u2p, n_tiles_max * EXPERT_TILE)
    ys = _expert_mlp(te[0, :n_tiles_max], te[1, :1], te[2, :n_tiles_max], xs, w_gate_up[0],
                     b_gate_up[0].reshape(N_EXPERTS, 1, -1).astype(F32), w_down[0],
                     b_down[0].reshape(N_EXPERTS, 1, -1).astype(F32))
    yg = _gather_expert_rows(dest_flat, ys)
    out = _combine(h1, info, vec(g_final), yg, n_batch, seq)
    return out.reshape(n_batch, seq, d)
```

```python
import functools

import jax
import jax.numpy as jnp
from jax import lax
from jax.experimental import pallas as pl
from jax.experimental.pallas import tpu as pltpu
from jax.experimental.pallas import tpu_sc as plsc

F32 = jnp.float32
BF16 = jnp.bfloat16

D_MODEL = 1024
N_META = 16
CHUNK = 128
CHUNK_PAD = CHUNK - N_META
CONV_WIDTH = 31
HEAD_DIM = 128
N_HEADS = D_MODEL // HEAD_DIM
N_EXPERTS = 32
TOP_K = 4
D_FF = D_MODEL
SWIGLU_LIMIT = 7.0
SWIGLU_ALPHA = 1.702
RMS_EPS = 1e-6
LN_EPS = 1e-5

ACT_AGLU, ACT_Q, ACT_K, ACT_V, ACT_OG, ACT_SGA, ACT_SGB = range(7)
N_ACTS = 7

LANES = 128
SUBLANES = 8
SUB_BLOCK = 32
N_SUB = CHUNK // SUB_BLOCK
HALO = 32
ROW_TILE = 640
IN_TILE = 320
SCAN_CHUNKS = 5
MERGE_PARTS = 2
SLOTS_STEPS = 5
ROUTE_ROWS = 16
CONV_ROWS = 64
CONV_SUMS = 4
EXPERT_TILE = 512
SC_SCATTER_CHUNK = 104
SC_GATHER_CHUNK = 40
COMBINE_SUB = 4
NEG_BIG = -1e30
VMEM_LIMIT = 56 * 1024 * 1024


def _sigmoid(x):
    return 1.0 / (1.0 + jnp.exp(-x))


def _cparams(sem):
    return pltpu.CompilerParams(dimension_semantics=sem, vmem_limit_bytes=VMEM_LIMIT)


def _const_spec(shape):
    nd = len(shape)
    return pl.BlockSpec(shape, lambda *_: (0,) * nd)


PACK_WORDS = D_MODEL // 2
HIGH_HALF = 0xFFFF0000


def _pack_rows(x):
    lo = lax.bitcast_convert_type(x[:, :PACK_WORDS].astype(BF16).astype(F32), jnp.uint32)
    hi = lax.bitcast_convert_type(x[:, PACK_WORDS:].astype(BF16).astype(F32), jnp.uint32)
    return (lo >> 16) | (hi & jnp.uint32(HIGH_HALF))


def _unpack_rows(w):
    lo = lax.bitcast_convert_type(w << 16, F32)
    hi = lax.bitcast_convert_type(w & jnp.uint32(HIGH_HALF), F32)
    return jnp.concatenate([lo, hi], axis=1)


def _residual_tile(x_ref, meta_ref, is_first):
    x = x_ref[...]
    tm, d = x.shape
    prefix = jnp.concatenate([jnp.zeros((CHUNK_PAD, d), x.dtype), meta_ref[...]], axis=0)
    first = jnp.concatenate([prefix, x[:tm - CHUNK, :]], axis=0)
    return jnp.where(is_first, first, x)


def _residual_spec(tm, d, seq, tiles_per_batch, n_tiles):
    def start(i):
        i = jnp.minimum(i, n_tiles - 1)
        b, t = i // tiles_per_batch, i % tiles_per_batch
        return (pl.multiple_of(b * seq + jnp.maximum(t * tm - CHUNK, 0), SUBLANES), 0)
    return pl.BlockSpec((pl.Element(tm), pl.Element(d)), start)


def _in_proj_kernel(x_ref, meta_ref, g_ref, lbl_ref, w_ref, acts_ref, logf_ref, *, tiles_per_batch):
    d = D_MODEL

    def put(group, value):
        acts_ref[:, group * d:(group + 1) * d] = value.astype(acts_ref.dtype)

    h = _residual_tile(x_ref, meta_ref, pl.program_id(0) % tiles_per_batch == 0)
    ms = jnp.mean(h * h, axis=-1, keepdims=True)
    u = (h * lax.rsqrt(ms + RMS_EPS) * g_ref[...]).astype(BF16)

    def proj(j):
        return jnp.dot(u, w_ref[:, j * d:(j + 1) * d], preferred_element_type=F32)

    put(ACT_AGLU, proj(0) * _sigmoid(proj(1)))
    zq = proj(2)
    put(ACT_Q, zq * _sigmoid(zq))
    lbl = lbl_ref[...]
    e = jnp.exp(lbl - jnp.max(lbl, axis=0, keepdims=True))
    lb = e[0:1, :] / jnp.sum(e, axis=0, keepdims=True)
    s = _sigmoid(proj(3))
    logf_ref[...] = jnp.log(lb + (1.0 - lb) * s)
    put(ACT_K, (1.0 - lb) * (1.0 - s))
    put(ACT_V, proj(4))
    zg = proj(5)
    put(ACT_OG, zg * _sigmoid(zg))
    put(ACT_SGA, _sigmoid(proj(6)))
    put(ACT_SGB, _sigmoid(proj(7)))


def _in_proj(x2, meta, seq, g_mix, lb_logits, w_in_bf16):
    d = x2.shape[1]
    n_batch = x2.shape[0] // seq
    tm = IN_TILE
    tpb = (seq + CHUNK) // tm
    tp = n_batch * (seq + CHUNK)
    assert tpb * tm == seq + CHUNK and tm > CHUNK
    row = pl.BlockSpec((tm, d), lambda i: (i, 0))
    return pl.pallas_call(
        functools.partial(_in_proj_kernel, tiles_per_batch=tpb),
        grid=(tp // tm,),
        in_specs=[_residual_spec(tm, d, seq, tpb, tp // tm), _const_spec(meta.shape), _const_spec((1, d)),
                  _const_spec(lb_logits.shape),
                  pl.BlockSpec(w_in_bf16.shape, lambda i: (0, 0), pipeline_mode=pl.Buffered(1))],
        out_specs=[pl.BlockSpec((tm, N_ACTS * d), lambda i: (i, 0)), row],
        out_shape=[jax.ShapeDtypeStruct((tp, N_ACTS * d), BF16), jax.ShapeDtypeStruct((tp, d), F32)],
        compiler_params=_cparams(("parallel",)),
        name="in_proj",
    )(x2, meta, g_mix, lb_logits, w_in_bf16)


def _conv_kernel(halo_ref, cur_ref, wdw_ref, bdw_ref, lng_ref, lnb_ref, wout_ref, bout_ref, sga_ref,
                 out_ref, win_ref, acc_ref, shift_ref):
    tm = cur_ref.shape[0]
    win_ref[0:HALO, :] = halo_ref[...].astype(F32)
    win_ref[HALO:, :] = cur_ref[...].astype(F32)
    first_tap = HALO - (CONV_WIDTH - 1)
    for c in range(D_MODEL // LANES):
        lanes = slice(c * LANES, (c + 1) * LANES)
        n_rows = tm + HALO - SUBLANES
        for s in range(1, SUBLANES):
            shift_ref[s - 1, 0:n_rows, :] = win_ref[pl.ds(s, n_rows), lanes]
        bias = jnp.broadcast_to(bdw_ref[:, lanes], (CONV_ROWS, LANES))
        groups = CONV_ROWS // SUBLANES

        def chunk(r, carry, c=c, lanes=lanes, bias=bias):
            r0 = pl.multiple_of(r * CONV_ROWS, CONV_ROWS)
            accs = [bias.reshape(groups, SUBLANES, LANES)] + [None] * (CONV_SUMS - 1)
            for j in range(CONV_WIDTH):
                off = first_tap + j
                rows = pl.ds(r0 + off - off % SUBLANES, CONV_ROWS)
                w_j = wdw_ref[c, pl.ds(j, SUBLANES, stride=0), :]
                window = shift_ref[off % SUBLANES - 1, rows, :] if off % SUBLANES else win_ref[rows, lanes]
                term = w_j[None] * window.reshape(groups, SUBLANES, LANES)
                accs[j % CONV_SUMS] = term if accs[j % CONV_SUMS] is None else accs[j % CONV_SUMS] + term
            while len(accs) > 1:
                accs = [a + b for a, b in zip(accs[0::2], accs[1::2])]
            acc_ref[pl.ds(r0, CONV_ROWS), lanes] = accs[0].reshape(CONV_ROWS, LANES)
            return carry

        lax.fori_loop(0, tm // CONV_ROWS, chunk, 0)
    a = acc_ref[...]
    mu = jnp.mean(a, axis=-1, keepdims=True)
    ac = a - mu
    var = jnp.mean(ac * ac, axis=-1, keepdims=True)
    y = ac * lax.rsqrt(var + LN_EPS) * lng_ref[...] + lnb_ref[...]
    y = y * _sigmoid(y)
    o = jnp.dot(y.astype(BF16), wout_ref[...], preferred_element_type=F32) + bout_ref[...]
    out_ref[...] = (sga_ref[...].astype(F32) * o).astype(out_ref.dtype)


def _conv_branch(acts, w_dw, b_dw, ln_g, ln_b, w_out_bf16, b_out):
    tp, d = acts.shape[0], D_MODEL
    tm = ROW_TILE
    per = tm // HALO
    row = pl.BlockSpec((tm, d), lambda i: (i, 0))
    glu = pl.BlockSpec((tm, d), lambda i: (i, ACT_AGLU))
    gate = pl.BlockSpec((tm, d), lambda i: (i, ACT_SGA))
    halo = pl.BlockSpec((HALO, d), lambda i: (jnp.maximum(i * per - 1, 0), ACT_AGLU))
    vec = _const_spec((1, d))
    return pl.pallas_call(
        _conv_kernel,
        grid=(tp // tm,),
        in_specs=[halo, glu, _const_spec(w_dw.shape), vec, vec, vec, _const_spec((d, d)), vec, gate],
        out_specs=row,
        out_shape=jax.ShapeDtypeStruct((tp, d), BF16),
        scratch_shapes=[pltpu.VMEM((tm + HALO, d), F32), pltpu.VMEM((tm, d), F32),
                        pltpu.VMEM((SUBLANES - 1, tm + HALO - SUBLANES, LANES), F32)],
        compiler_params=_cparams(("parallel",)),
        name="conv_branch",
    )(acts, acts, w_dw, b_dw, ln_g, ln_b, w_out_bf16, b_out, acts)


def _split_bf16(x):
    hi = x.astype(BF16)
    return hi, (x - hi.astype(F32)).astype(BF16)


def _nt_dot(a, b):
    return lax.dot_general(a, b, (((1,), (1,)), ((), ())), preferred_element_type=F32)


def _tn_dot(a, b):
    return lax.dot_general(a, b, (((0,), (0,)), ((), ())), preferred_element_type=F32)


def _scan_kernel(q_ref, k_ref, v_ref, lf_ref, og_ref, gon_ref, o_ref, st_ref):
    @pl.when(pl.program_id(1) == 0)
    def _():
        st_ref[...] = jnp.zeros_like(st_ref)

    c = CHUNK
    row = lax.broadcasted_iota(jnp.int32, (c, c), 0)
    col = lax.broadcasted_iota(jnp.int32, (c, c), 1)
    causal = col <= row
    diag_mask = jnp.logical_and(causal, row // SUB_BLOCK == col // SUB_BLOCK)
    tri = causal.astype(BF16)

    def bcast_rows(rows):
        return jnp.concatenate([jnp.broadcast_to(r, (SUB_BLOCK, HEAD_DIM)) for r in rows], axis=0)

    heads = [slice(h * HEAD_DIM, (h + 1) * HEAD_DIM) for h in range(N_HEADS)]
    chunks = [slice(ci * c, (ci + 1) * c) for ci in range(SCAN_CHUNKS)]

    bcums = []
    for cs in chunks:
        lf = lf_ref[cs, :]
        lf_hi = lf.astype(BF16)
        lf_mid, lf_lo = _split_bf16(lf - lf_hi.astype(F32))
        bcums.append(jnp.dot(tri, lf_hi, preferred_element_type=F32)
                     + (jnp.dot(tri, lf_mid, preferred_element_type=F32)
                        + jnp.dot(tri, lf_lo, preferred_element_type=F32)))
    operands = {}
    for ci, cs in enumerate(chunks):
        for h, hs in enumerate(heads):
            b = bcums[ci][:, hs]
            q = q_ref[cs, hs].astype(F32)
            k = k_ref[cs, hs].astype(F32)
            ends = [b[i * SUB_BLOCK + SUB_BLOCK - 1:i * SUB_BLOCK + SUB_BLOCK, :] for i in range(N_SUB)]
            mids = [b[i * SUB_BLOCK + SUB_BLOCK // 2 - 1:i * SUB_BLOCK + SUB_BLOCK // 2, :] for i in range(N_SUB)]
            mid_full = bcast_rows(mids)
            b_last = ends[-1]
            zero_row = jnp.zeros_like(b_last)
            qm = q * jnp.exp(b - mid_full)
            km = k * jnp.exp(mid_full - b)
            q_parts, k_parts = [], []
            for j in range(N_SUB - 1):
                q_rows = [jnp.exp(mids[i] - ends[j]) if i > j else zero_row for i in range(N_SUB)]
                k_rows = [jnp.exp(ends[j] - mids[j]) if i == j else zero_row for i in range(N_SUB)]
                q_parts.append((qm * bcast_rows(q_rows)).astype(BF16))
                k_parts.append((km * bcast_rows(k_rows)).astype(BF16))
            q_in = (qm * bcast_rows([jnp.exp(m) for m in mids])).astype(BF16)
            k_out = (km * bcast_rows([jnp.exp(b_last - m) for m in mids])).astype(BF16)
            operands[ci, h] = (qm.astype(BF16), km.astype(BF16), jnp.concatenate(q_parts, axis=1),
                               jnp.concatenate(k_parts, axis=1), q_in, k_out, jnp.exp(b_last))
    products = {}
    for ci, cs in enumerate(chunks):
        for h, hs in enumerate(heads):
            qm, km, q_cat, k_cat, _, k_out, _ = operands[ci, h]
            products[ci, h] = (_nt_dot(qm, km), _nt_dot(q_cat, k_cat), _tn_dot(v_ref[cs, hs], k_out))
    within = {}
    for ci, cs in enumerate(chunks):
        for h, hs in enumerate(heads):
            same_block, earlier_blocks, _ = products[ci, h]
            scores = jnp.where(diag_mask, same_block, 0.0) + earlier_blocks
            within[ci, h] = jnp.dot(scores.astype(BF16), v_ref[cs, hs], preferred_element_type=F32)
    states = [st_ref[h] for h in range(N_HEADS)]
    outs = {}
    for ci in range(SCAN_CHUNKS):
        for h in range(N_HEADS):
            outs[ci, h] = within[ci, h] + _nt_dot(operands[ci, h][4], states[h].astype(BF16))
            states[h] = states[h] * operands[ci, h][6] + products[ci, h][2]
    for h in range(N_HEADS):
        st_ref[h] = states[h]
    for ci, cs in enumerate(chunks):
        for h, hs in enumerate(heads):
            o = outs[ci, h]
            ms = jnp.mean(o * o, axis=-1, keepdims=True)
            on = o * lax.rsqrt(ms + RMS_EPS) * gon_ref[:, hs]
            o_ref[cs, hs] = (on * og_ref[cs, hs].astype(F32)).astype(o_ref.dtype)


def _hgrn2_scan(acts, logf, g_onorm, n_batch):
    tp, d = logf.shape
    rows = SCAN_CHUNKS * CHUNK
    n_chunks = tp // n_batch // rows
    assert n_chunks * rows * n_batch == tp
    col = lambda group: pl.BlockSpec((rows, d), lambda b, c: (b * n_chunks + c, group))
    blk = col(0)
    return pl.pallas_call(
        _scan_kernel,
        grid=(n_batch, n_chunks),
        in_specs=[col(ACT_Q), col(ACT_K), col(ACT_V), blk, col(ACT_OG), pl.BlockSpec((1, d), lambda b, c: (0, 0))],
        out_specs=blk,
        out_shape=jax.ShapeDtypeStruct((tp, d), BF16),
        scratch_shapes=[pltpu.VMEM((N_HEADS, HEAD_DIM, HEAD_DIM), F32)],
        compiler_params=_cparams(("arbitrary", "arbitrary")),
        name="hgrn2_scan",
    )(acts, acts, acts, logf, acts, g_onorm)


def _merge_route_kernel(x_ref, meta_ref, on_ref, ap_ref, sgb_ref, wrnn_ref, wo_ref, gffn_ref, wr_ref, br_ref,
                        h1_ref, u2_ref, info_ref, route_ref, cnt_ref, tri_ref, carry_ref, colcnt_ref, logits_ref, hres_ref,
                        *, tiles_per_batch, n_tiles):
    tm = on_ref.shape[0]
    i = pl.program_id(0)
    hres_ref[...] = _residual_tile(x_ref, meta_ref, jnp.minimum(i, n_tiles - 1) % tiles_per_batch == 0)

    @pl.when(i == 0)
    def _():
        r_i = lax.broadcasted_iota(jnp.int32, (tm, tm), 0)
        c_i = lax.broadcasted_iota(jnp.int32, (tm, tm), 1)
        tri_ref[...] = (c_i < r_i).astype(BF16)
        carry_ref[...] = jnp.zeros_like(carry_ref)
        colcnt_ref[...] = jnp.zeros_like(colcnt_ref)
        logits_ref[...] = jnp.zeros_like(logits_ref)

    routed = i > 0
    lane = lax.broadcasted_iota(jnp.int32, (tm, LANES), 1)
    cur = jnp.where(lane < N_EXPERTS, logits_ref[...], NEG_BIG)
    vals, idxs, sels = [], [], []

    def topk_round(cur):
        m = jnp.max(cur, axis=-1, keepdims=True)
        idx = jnp.min(jnp.where(cur == m, lane, LANES), axis=-1, keepdims=True)
        sel = lane == idx
        vals.append(m)
        idxs.append(idx)
        sels.append(sel)
        return jnp.where(sel, 2.0 * NEG_BIG, cur)

    w_hi, w_lo = _split_bf16(wr_ref[...])
    part = tm // MERGE_PARTS
    rows = [slice(p * part, (p + 1) * part) for p in range(MERGE_PARTS)]
    rs = [jnp.dot(on_ref[rw, :], wrnn_ref[...], preferred_element_type=F32) for rw in rows]
    cur = topk_round(cur)
    ys = [(ap_ref[rw, :].astype(F32) + sgb_ref[rw, :].astype(F32) * r).astype(BF16) for rw, r in zip(rows, rs)]
    h1s = [hres_ref[rw, :] + jnp.dot(y, wo_ref[...], preferred_element_type=F32) for rw, y in zip(rows, ys)]
    cur = topk_round(cur)
    u2s = []
    for rw, h1 in zip(rows, h1s):
        h1_ref[rw, :] = h1
        ms = jnp.mean(h1 * h1, axis=-1, keepdims=True)
        u2 = h1 * lax.rsqrt(ms + RMS_EPS) * gffn_ref[...]
        u2_ref[rw, :] = _pack_rows(u2)
        u2s.append(u2)
    cur = topk_round(cur)
    new_logits = []
    for u2 in u2s:
        u_hi, u_lo = _split_bf16(u2)
        new_logits.append(jnp.dot(u_hi, w_hi, preferred_element_type=F32)
                          + (jnp.dot(u_lo, w_hi, preferred_element_type=F32)
                             + jnp.dot(u_hi, w_lo, preferred_element_type=F32))
                          + br_ref[...])
    cur = topk_round(cur)
    assert len(vals) == TOP_K
    exps = [jnp.exp(vk - vals[0]) for vk in vals]
    den = exps[0] + exps[1] + exps[2] + exps[3]
    onehot = jnp.logical_or(jnp.logical_or(sels[0], sels[1]), jnp.logical_or(sels[2], sels[3]))
    onehot = jnp.logical_and(onehot, routed)
    onehot_bf = onehot.astype(BF16)
    rank_all = jnp.dot(tri_ref[...], onehot_bf, preferred_element_type=F32) + carry_ref[...]
    carry_ref[...] += jnp.sum(onehot.astype(F32), axis=0, keepdims=True)
    colcnt_ref[...] += _tn_dot(onehot_bf, jnp.ones((tm, LANES), BF16))
    info = jnp.zeros((tm, LANES), F32)
    for kk in range(TOP_K):
        rank_k = jnp.sum(jnp.where(sels[kk], rank_all, 0.0), axis=-1, keepdims=True)
        info = jnp.where(lane == kk, exps[kk] / den, info)
        info = jnp.where(lane == TOP_K + kk, idxs[kk].astype(F32), info)
        info = jnp.where(lane == 2 * TOP_K + kk, rank_k, info)
    info_ref[...] = info
    route_ref[...] = info.T[0:ROUTE_ROWS, :]
    cnt_ref[...] = colcnt_ref[...]
    logits_ref[...] = jnp.concatenate(new_logits, axis=0)


def _merge_route(x2, meta, seq, on, ap, acts, w_rnn_bf16, w_o_bf16, g_ffn, w_router_pad, b_router_pad):
    tp, d = on.shape
    tm = ROW_TILE
    n_tiles = tp // tm
    tpb = (seq + CHUNK) // tm
    assert tpb * tm == seq + CHUNK
    this_tile = lambda i: (jnp.minimum(i, n_tiles - 1), 0)
    prev_tile = lambda i: (jnp.maximum(i - 1, 0), 0)
    row = pl.BlockSpec((tm, d), this_tile)
    vec = _const_spec((1, d))
    return pl.pallas_call(
        functools.partial(_merge_route_kernel, tiles_per_batch=tpb, n_tiles=n_tiles),
        grid=(n_tiles + 1,),
        in_specs=[_residual_spec(tm, d, seq, tpb, n_tiles), _const_spec(meta.shape), row, row,
                  pl.BlockSpec((tm, d), lambda i: (jnp.minimum(i, n_tiles - 1), ACT_SGB)),
                  _const_spec((d, d)), _const_spec((d, d)), vec, _const_spec((d, LANES)), _const_spec((1, LANES))],
        out_specs=[row, pl.BlockSpec((tm, PACK_WORDS), this_tile),
                   pl.BlockSpec((tm, LANES), prev_tile),
                   pl.BlockSpec((ROUTE_ROWS, tm), lambda i: (0, jnp.maximum(i - 1, 0))), _const_spec((LANES, LANES))],
        out_shape=[jax.ShapeDtypeStruct((tp, d), F32), jax.ShapeDtypeStruct((tp, PACK_WORDS), jnp.uint32),
                   jax.ShapeDtypeStruct((tp, LANES), F32), jax.ShapeDtypeStruct((ROUTE_ROWS, tp), F32),
                   jax.ShapeDtypeStruct((LANES, LANES), F32)],
        scratch_shapes=[pltpu.VMEM((tm, tm), BF16), pltpu.VMEM((1, LANES), F32),
                        pltpu.VMEM((LANES, LANES), F32), pltpu.VMEM((tm, LANES), F32), pltpu.VMEM((tm, d), F32)],
        compiler_params=_cparams(("arbitrary",)),
        name="merge_route",
    )(x2, meta, on, ap, acts, w_rnn_bf16, w_o_bf16, g_ffn, w_router_pad, b_router_pad)


def _slots_kernel(route_ref, cnt_ref, dest_ref, te_ref, pstart_ref):
    tl = route_ref.shape[1]
    n_tile_lanes = te_ref.shape[1]

    @pl.when(pl.program_id(0) == 0)
    def _():
        r_i = lax.broadcasted_iota(jnp.int32, (LANES, LANES), 0)
        c_i = lax.broadcasted_iota(jnp.int32, (LANES, LANES), 1)
        cnt = cnt_ref[...]
        tiles = jnp.floor((cnt + (EXPERT_TILE - 1)) / EXPERT_TILE)
        pend_col = jnp.dot((c_i <= r_i).astype(F32), tiles, preferred_element_type=F32,
                           precision=lax.Precision.HIGHEST)
        pstart_ref[...] = pend_col - tiles
        tile_id = lax.broadcasted_iota(jnp.int32, (LANES, n_tile_lanes), 1).astype(F32)
        exp_id = lax.broadcasted_iota(jnp.int32, (LANES, n_tile_lanes), 0)
        pend_wide = jnp.concatenate([pend_col] * (n_tile_lanes // LANES), axis=1)
        below = jnp.logical_and(pend_wide <= tile_id, exp_id < N_EXPERTS)
        te = jnp.minimum(jnp.sum(below.astype(F32), axis=0, keepdims=True), N_EXPERTS - 1.0)
        n_active = pend_col[N_EXPERTS - 1:N_EXPERTS, 0:1]
        owner = exp_id.astype(F32) == te
        cnt_wide = jnp.concatenate([cnt] * (n_tile_lanes // LANES), axis=1)
        first_tile = pend_wide - jnp.concatenate([tiles] * (n_tile_lanes // LANES), axis=1)
        left = jnp.sum(jnp.where(owner, cnt_wide - (tile_id - first_tile) * EXPERT_TILE, 0.0), axis=0, keepdims=True)
        valid = jnp.clip(left, 0.0, float(EXPERT_TILE))
        sub_t = lax.broadcasted_iota(jnp.int32, (SUBLANES, n_tile_lanes), 0)
        out = jnp.where(sub_t == 0, jnp.broadcast_to(te, (SUBLANES, n_tile_lanes)),
                        jnp.where(sub_t == 1, jnp.broadcast_to(n_active, (SUBLANES, n_tile_lanes)),
                                  jnp.broadcast_to(valid, (SUBLANES, n_tile_lanes))))
        te_ref[...] = out.astype(jnp.int32)

    route = route_ref[...]
    expert = lax.broadcasted_iota(jnp.int32, (N_EXPERTS, tl), 0)
    first_slot = jnp.concatenate([pstart_ref[0:N_EXPERTS, :]] * (tl // LANES), axis=1) * EXPERT_TILE
    rows = []
    for kk in range(TOP_K):
        idx_k = route[TOP_K + kk:TOP_K + kk + 1, :].astype(jnp.int32)
        start_k = jnp.sum(jnp.where(expert == idx_k, first_slot, 0.0), axis=0, keepdims=True)
        rows.append(start_k + route[2 * TOP_K + kk:2 * TOP_K + kk + 1, :])
    rows.append(jnp.zeros((SUBLANES - TOP_K, tl), F32))
    dest_ref[...] = jnp.concatenate(rows, axis=0).astype(jnp.int32)


def _slots(route, cnt, n_tiles_max):
    tp = route.shape[1]
    tl = tp // SLOTS_STEPS
    assert tl * SLOTS_STEPS == tp and tl % LANES == 0
    n_tile_lanes = -(-n_tiles_max // LANES) * LANES
    return pl.pallas_call(
        _slots_kernel,
        grid=(SLOTS_STEPS,),
        in_specs=[pl.BlockSpec((ROUTE_ROWS, tl), lambda i: (0, i)), _const_spec((LANES, LANES))],
        out_specs=[pl.BlockSpec((SUBLANES, tl), lambda i: (0, i)), _const_spec((8, n_tile_lanes))],
        out_shape=[jax.ShapeDtypeStruct((SUBLANES, tp), jnp.int32),
                   jax.ShapeDtypeStruct((8, n_tile_lanes), jnp.int32)],
        scratch_shapes=[pltpu.VMEM((LANES, LANES), F32)],
        compiler_params=_cparams(("arbitrary",)),
        name="slots",
    )(route, cnt)


def _sc_workers():
    sc = plsc.get_sparse_core_info()
    return sc.num_cores, sc.num_cores * sc.num_subcores


def _sc_token_chunks(tp, chunk, body):
    n_cores, n_workers = _sc_workers()
    per_worker = tp // n_workers
    assert per_worker * n_workers == tp and per_worker % chunk == 0 and chunk % SUBLANES == 0
    base = (lax.axis_index("s") * n_cores + lax.axis_index("c")) * per_worker

    @pl.loop(0, per_worker // chunk)
    def _(ci):
        body(pl.multiple_of(base + ci * chunk, SUBLANES))


def _sc_kernel(out_type, chunk, n_row_bufs):
    return functools.partial(
        pl.kernel, mesh=plsc.VectorSubcoreMesh(core_axis_name="c", subcore_axis_name="s"), out_type=out_type,
        scratch_types=[pltpu.VMEM((TOP_K, chunk), jnp.int32), pltpu.VMEM((n_row_bufs, chunk, PACK_WORDS), jnp.uint32),
                       pltpu.SemaphoreType.DMA, pltpu.SemaphoreType.DMA])


def _wait_all(copies):
    for c in copies:
        c.wait()


def _dispatch(dest_flat, u2p, n_slots):
    tp = u2p.shape[0]
    ch = SC_SCATTER_CHUNK

    @_sc_kernel(jax.ShapeDtypeStruct((n_slots, PACK_WORDS), jnp.uint32), ch, 1)
    def scatter_rows(u2_hbm, dest_hbm, xs_hbm, idx_v, rows_v, sem_a, sem_b):
        def chunk(off):
            loads = [pltpu.async_copy(u2_hbm.at[pl.ds(off, ch)], rows_v.at[0], sem_a)]
            loads += [pltpu.async_copy(dest_hbm.at[pl.ds(kk * tp + off, ch)], idx_v.at[kk], sem_a)
                      for kk in range(TOP_K)]
            _wait_all(loads)
            _wait_all([pltpu.async_copy(rows_v.at[0], xs_hbm.at[idx_v.at[kk]], sem_b) for kk in range(TOP_K)])

        _sc_token_chunks(tp, ch, chunk)

    return scatter_rows(u2p, dest_flat)


def _gather_expert_rows(dest_flat, ys):
    tp = dest_flat.shape[0] // TOP_K
    ch = SC_GATHER_CHUNK

    @_sc_kernel(jax.ShapeDtypeStruct((TOP_K, tp, PACK_WORDS), jnp.uint32), ch, TOP_K)
    def gather_rows(ys_hbm, dest_hbm, out_hbm, idx_v, rows_v, sem_a, sem_b):
        def chunk(off):
            _wait_all([pltpu.async_copy(dest_hbm.at[pl.ds(kk * tp + off, ch)], idx_v.at[kk], sem_a)
                       for kk in range(TOP_K)])
            _wait_all([pltpu.async_copy(ys_hbm.at[idx_v.at[kk]], rows_v.at[kk], sem_b) for kk in range(TOP_K)])
            _wait_all([pltpu.async_copy(rows_v.at[kk], out_hbm.at[kk, pl.ds(off, ch)], sem_a)
                       for kk in range(TOP_K)])

        _sc_token_chunks(tp, ch, chunk)

    return gather_rows(ys, dest_flat)


def _expert_kernel(te_ref, nact_ref, valid_ref, x_ref, wgu_hbm, bgu_ref, wdn_hbm, bdn_ref, y_ref,
                   wgu_buf, wdn_buf, slot_ref, sem):
    i = pl.program_id(0)
    n_active = nact_ref[0]
    last_tile = pl.num_programs(0) - 1

    def fetch(expert, slot):
        return (pltpu.make_async_copy(wgu_hbm.at[expert], wgu_buf.at[slot], sem.at[slot, 0]),
                pltpu.make_async_copy(wdn_hbm.at[expert], wdn_buf.at[slot], sem.at[slot, 1]))

    @pl.when(i == 0)
    def _():
        slot_ref[0] = 1
        for copy in fetch(te_ref[0], 0):
            copy.start()

    @pl.when(i < n_active)
    def _():
        expert = te_ref[i]
        first_tile_of_expert = jnp.logical_or(i == 0, expert != te_ref[jnp.maximum(i - 1, 0)])

        @pl.when(first_tile_of_expert)
        def _():
            slot = 1 - slot_ref[0]
            slot_ref[0] = slot
            for copy in fetch(expert, slot):
                copy.wait()
            nxt = lax.while_loop(
                lambda j: jnp.logical_and(j < n_active, te_ref[jnp.minimum(j, last_tile)] == expert),
                lambda j: j + 1, i + 1)

            @pl.when(nxt < n_active)
            def _():
                for copy in fetch(te_ref[jnp.minimum(nxt, last_tile)], 1 - slot):
                    copy.start()

        slot = slot_ref[0]

        def mlp(rows):
            x = _unpack_rows(x_ref[0:rows, :]).astype(BF16)
            gu = jnp.dot(x, wgu_buf[slot].astype(BF16), preferred_element_type=F32) + bgu_ref[0]
            gate = jnp.minimum(gu[:, :D_FF], SWIGLU_LIMIT)
            up = jnp.clip(gu[:, D_FF:], -SWIGLU_LIMIT, SWIGLU_LIMIT)
            hdn = (up + 1.0) * (gate * _sigmoid(SWIGLU_ALPHA * gate))
            y = jnp.dot(hdn.astype(BF16), wdn_buf[slot].astype(BF16), preferred_element_type=F32) + bdn_ref[0]
            y_ref[0:rows, :] = _pack_rows(y)

        half = EXPERT_TILE // 2
        half_empty = valid_ref[i] <= half
        pl.when(jnp.logical_not(half_empty))(lambda: mlp(EXPERT_TILE))

        @pl.when(half_empty)
        def _():
            mlp(half)
            y_ref[half:, :] = jnp.zeros((EXPERT_TILE - half, PACK_WORDS), y_ref.dtype)

    @pl.when(i >= n_active)
    def _():
        y_ref[...] = jnp.zeros_like(y_ref)


def _expert_mlp(te, nact, valid, xs, w_gu, b_gu, w_dn, b_dn):
    d = D_MODEL
    n_tiles = xs.shape[0] // EXPERT_TILE

    def tile_map(i, te_ref, nact_ref, valid_ref):
        return (jnp.minimum(i, nact_ref[0] - 1), 0)

    def exp_map(i, te_ref, nact_ref, valid_ref):
        return (te_ref[jnp.minimum(i, nact_ref[0] - 1)], 0, 0)

    grid_spec = pltpu.PrefetchScalarGridSpec(
        num_scalar_prefetch=3,
        grid=(n_tiles,),
        in_specs=[pl.BlockSpec((EXPERT_TILE, PACK_WORDS), tile_map),
                  pl.BlockSpec(memory_space=pl.ANY),
                  pl.BlockSpec((1, 1, 2 * D_FF), exp_map),
                  pl.BlockSpec(memory_space=pl.ANY),
                  pl.BlockSpec((1, 1, d), exp_map)],
        out_specs=pl.BlockSpec((EXPERT_TILE, PACK_WORDS), lambda i, te_ref, nact_ref, valid_ref: (i, 0)),
        scratch_shapes=[pltpu.VMEM((2, d, 2 * D_FF), F32), pltpu.VMEM((2, D_FF, d), F32),
                        pltpu.SMEM((1,), jnp.int32), pltpu.SemaphoreType.DMA((2, 2))],
    )
    return pl.pallas_call(
        _expert_kernel,
        grid_spec=grid_spec,
        out_shape=jax.ShapeDtypeStruct(xs.shape, jnp.uint32),
        compiler_params=_cparams(("arbitrary",)),
        name="expert_mlp",
    )(te, nact, valid, xs, w_gu, b_gu, w_dn, b_dn)


def _combine_kernel(gfin_ref, *refs):
    out_ref = refs[-1]
    for s in range(COMBINE_SUB):
        h1_ref, info_ref, yg_ref = refs[3 * s:3 * s + 3]
        info = info_ref[...]
        h2 = h1_ref[...]
        for kk in range(TOP_K):
            h2 = h2 + info[:, kk:kk + 1] * _unpack_rows(yg_ref[kk])
        ms = jnp.mean(h2 * h2, axis=-1, keepdims=True)
        out_ref[s * CHUNK:(s + 1) * CHUNK, :] = h2 * lax.rsqrt(ms + RMS_EPS) * gfin_ref[...]


def _combine(h1, info, g_final, yg, n_batch, seq):
    d = h1.shape[1]
    chunks = seq // CHUNK

    def padded(s):
        def index(j):
            c = j * COMBINE_SUB + s
            return (c // chunks) * (chunks + 1) + c % chunks + 1
        return index

    in_specs = [pl.BlockSpec((1, d), lambda j: (0, 0))]
    operands = [g_final]
    for s in range(COMBINE_SUB):
        chunk_of = padded(s)
        in_specs += [pl.BlockSpec((CHUNK, d), lambda j, f=chunk_of: (f(j), 0)),
                     pl.BlockSpec((CHUNK, LANES), lambda j, f=chunk_of: (f(j), 0)),
                     pl.BlockSpec((TOP_K, CHUNK, PACK_WORDS), lambda j, f=chunk_of: (0, f(j), 0))]
        operands += [h1, info, yg]
    return pl.pallas_call(
        _combine_kernel,
        grid=(n_batch * chunks // COMBINE_SUB,),
        in_specs=in_specs,
        out_specs=pl.BlockSpec((COMBINE_SUB * CHUNK, d), lambda j: (j, 0)),
        out_shape=jax.ShapeDtypeStruct((n_batch * seq, d), F32),
        compiler_params=_cparams(("parallel",)),
        name="combine",
    )(*operands)


def kernel(x, meta_tokens, lb_logits, g_mix, w_in, w_dw, b_dw, ln_g, ln_b, w_conv_out, b_conv_out,
           g_onorm, w_rnn_out, w_o, g_ffn, w_router, b_router, w_gate_up, b_gate_up, w_down, b_down,
           g_final):
    n_batch, seq, d = x.shape
    assert d == D_MODEL and w_in.shape[0] == 1, "single-layer block with D_MODEL features"
    assert seq % (CHUNK * COMBINE_SUB) == 0
    seq_pad = CHUNK + seq
    tp = n_batch * seq_pad
    assert tp % ROW_TILE == 0

    x2 = x.reshape(n_batch * seq, d)
    meta = meta_tokens.astype(x.dtype)
    vec = lambda a: a.reshape(1, -1).astype(F32)

    acts, logf = _in_proj(x2, meta, seq, vec(g_mix[0]), lb_logits.astype(F32), w_in[0].astype(BF16))
    w_taps = w_dw[0].astype(F32).reshape(CONV_WIDTH, d // LANES, LANES).transpose(1, 0, 2)
    ap = _conv_branch(acts, w_taps, vec(b_dw[0]), vec(ln_g[0]), vec(ln_b[0]),
                      w_conv_out[0].astype(BF16), vec(b_conv_out[0]))
    on = _hgrn2_scan(acts, logf, vec(g_onorm[0]), n_batch)
    w_router_pad = jnp.pad(w_router[0].astype(F32), ((0, 0), (0, LANES - N_EXPERTS)))
    b_router_pad = jnp.pad(vec(b_router[0]), ((0, 0), (0, LANES - N_EXPERTS)))
    h1, u2p, info, route, cnt = _merge_route(x2, meta, seq, on, ap, acts, w_rnn_out[0].astype(BF16), w_o[0].astype(BF16),
                                      vec(g_ffn[0]), w_router_pad, b_router_pad)

    n_tiles_max = -(-(tp * TOP_K + N_EXPERTS * (EXPERT_TILE - 1)) // EXPERT_TILE)
    dest, te = _slots(route, cnt, n_tiles_max)
    dest_flat = dest[:TOP_K].reshape(-1)
    xs = _dispatch(dest_flat, u2p, n_tiles_max * EXPERT_TILE)
    ys = _expert_mlp(te[0, :n_tiles_max], te[1, :1], te[2, :n_tiles_max], xs, w_gate_up[0],
                     b_gate_up[0].reshape(N_EXPERTS, 1, -1).astype(F32), w_down[0],
                     b_down[0].reshape(N_EXPERTS, 1, -1).astype(F32))
    yg = _gather_expert_rows(dest_flat, ys)
    out = _combine(h1, info, vec(g_final), yg, n_batch, seq)
    return out.reshape(n_batch, seq, d)
```

```python
import functools

import jax
import jax.numpy as jnp
from jax import lax
from jax.experimental import pallas as pl
from jax.experimental.pallas import tpu as pltpu
from jax.experimental.pallas import tpu_sc as plsc

F32 = jnp.float32
BF16 = jnp.bfloat16

D_MODEL = 1024
N_META = 16
CHUNK = 128
CHUNK_PAD = CHUNK - N_META
CONV_WIDTH = 31
HEAD_DIM = 128
N_HEADS = D_MODEL // HEAD_DIM
N_EXPERTS = 32
TOP_K = 4
D_FF = D_MODEL
SWIGLU_LIMIT = 7.0
SWIGLU_ALPHA = 1.702
RMS_EPS = 1e-6
LN_EPS = 1e-5

ACT_AGLU, ACT_Q, ACT_K, ACT_V, ACT_OG, ACT_SGA, ACT_SGB = range(7)
N_ACTS = 7

LANES = 128
SUBLANES = 8
SUB_BLOCK = 32
N_SUB = CHUNK // SUB_BLOCK
HALO = 32
ROW_TILE = 640
IN_TILE = 320
SCAN_CHUNKS = 5
MERGE_PARTS = 2
SLOTS_STEPS = 5
ROUTE_ROWS = 16
CONV_ROWS = 64
CONV_SUMS = 4
EXPERT_TILE = 512
SC_SCATTER_CHUNK = 104
SC_GATHER_CHUNK = 40
COMBINE_SUB = 4
NEG_BIG = -1e30
VMEM_LIMIT = 56 * 1024 * 1024


def _sigmoid(x):
    return 1.0 / (1.0 + jnp.exp(-x))


def _cparams(sem):
    return pltpu.CompilerParams(dimension_semantics=sem, vmem_limit_bytes=VMEM_LIMIT)


def _const_spec(shape):
    nd = len(shape)
    return pl.BlockSpec(shape, lambda *_: (0,) * nd)


PACK_WORDS = D_MODEL // 2
HIGH_HALF = 0xFFFF0000


def _pack_rows(x):
    lo = lax.bitcast_convert_type(x[:, :PACK_WORDS].astype(BF16).astype(F32), jnp.uint32)
    hi = lax.bitcast_convert_type(x[:, PACK_WORDS:].astype(BF16).astype(F32), jnp.uint32)
    return (lo >> 16) | (hi & jnp.uint32(HIGH_HALF))


def _unpack_rows(w):
    lo = lax.bitcast_convert_type(w << 16, F32)
    hi = lax.bitcast_convert_type(w & jnp.uint32(HIGH_HALF), F32)
    return jnp.concatenate([lo, hi], axis=1)


def _residual_tile(x_ref, meta_ref, is_first):
    x = x_ref[...]
    tm, d = x.shape
    prefix = jnp.concatenate([jnp.zeros((CHUNK_PAD, d), x.dtype), meta_ref[...]], axis=0)
    first = jnp.concatenate([prefix, x[:tm - CHUNK, :]], axis=0)
    return jnp.where(is_first, first, x)


def _residual_spec(tm, d, seq, tiles_per_batch, n_tiles):
    def start(i):
        i = jnp.minimum(i, n_tiles - 1)
        b, t = i // tiles_per_batch, i % tiles_per_batch
        return (pl.multiple_of(b * seq + jnp.maximum(t * tm - CHUNK, 0), SUBLANES), 0)
    return pl.BlockSpec((pl.Element(tm), pl.Element(d)), start)


def _in_proj_kernel(x_ref, meta_ref, g_ref, lbl_ref, w_ref, acts_ref, logf_ref, *, tiles_per_batch):
    d = D_MODEL

    def put(group, value):
        acts_ref[:, group * d:(group + 1) * d] = value.astype(acts_ref.dtype)

    h = _residual_tile(x_ref, meta_ref, pl.program_id(0) % tiles_per_batch == 0)
    ms = jnp.mean(h * h, axis=-1, keepdims=True)
    u = (h * lax.rsqrt(ms + RMS_EPS) * g_ref[...]).astype(BF16)

    def proj(j):
        return jnp.dot(u, w_ref[:, j * d:(j + 1) * d].astype(BF16), preferred_element_type=F32)

    put(ACT_AGLU, proj(0) * _sigmoid(proj(1)))
    zq = proj(2)
    put(ACT_Q, zq * _sigmoid(zq))
    lbl = lbl_ref[...]
    e = jnp.exp(lbl - jnp.max(lbl, axis=0, keepdims=True))
    lb = e[0:1, :] / jnp.sum(e, axis=0, keepdims=True)
    s = _sigmoid(proj(3))
    logf_ref[...] = jnp.log(lb + (1.0 - lb) * s)
    put(ACT_K, (1.0 - lb) * (1.0 - s))
    put(ACT_V, proj(4))
    zg = proj(5)
    put(ACT_OG, zg * _sigmoid(zg))
    put(ACT_SGA, _sigmoid(proj(6)))
    put(ACT_SGB, _sigmoid(proj(7)))


def _in_proj(x2, meta, seq, g_mix, lb_logits, w_in):
    d = x2.shape[1]
    n_batch = x2.shape[0] // seq
    tm = IN_TILE
    tpb = (seq + CHUNK) // tm
    tp = n_batch * (seq + CHUNK)
    assert tpb * tm == seq + CHUNK and tm > CHUNK
    row = pl.BlockSpec((tm, d), lambda i: (i, 0))
    return pl.pallas_call(
        functools.partial(_in_proj_kernel, tiles_per_batch=tpb),
        grid=(tp // tm,),
        in_specs=[_residual_spec(tm, d, seq, tpb, tp // tm), _const_spec(meta.shape), _const_spec((1, d)),
                  _const_spec(lb_logits.shape),
                  pl.BlockSpec(w_in.shape, lambda i: (0, 0), pipeline_mode=pl.Buffered(1))],
        out_specs=[pl.BlockSpec((tm, N_ACTS * d), lambda i: (i, 0)), row],
        out_shape=[jax.ShapeDtypeStruct((tp, N_ACTS * d), BF16), jax.ShapeDtypeStruct((tp, d), F32)],
        compiler_params=_cparams(("parallel",)),
        name="in_proj",
    )(x2, meta, g_mix, lb_logits, w_in)


def _conv_kernel(halo_ref, cur_ref, wdw_ref, bdw_ref, lng_ref, lnb_ref, wout_ref, bout_ref, sga_ref,
                 out_ref, win_ref, acc_ref, shift_ref):
    tm = cur_ref.shape[0]
    win_ref[0:HALO, :] = halo_ref[...].astype(F32)
    win_ref[HALO:, :] = cur_ref[...].astype(F32)
    first_tap = HALO - (CONV_WIDTH - 1)
    for c in range(D_MODEL // LANES):
        lanes = slice(c * LANES, (c + 1) * LANES)
        n_rows = tm + HALO - SUBLANES
        for s in range(1, SUBLANES):
            shift_ref[s - 1, 0:n_rows, :] = win_ref[pl.ds(s, n_rows), lanes]
        bias = jnp.broadcast_to(bdw_ref[:, lanes], (CONV_ROWS, LANES))
        groups = CONV_ROWS // SUBLANES

        def chunk(r, carry, c=c, lanes=lanes, bias=bias):
            r0 = pl.multiple_of(r * CONV_ROWS, CONV_ROWS)
            accs = [bias.reshape(groups, SUBLANES, LANES)] + [None] * (CONV_SUMS - 1)
            for j in range(CONV_WIDTH):
                off = first_tap + j
                rows = pl.ds(r0 + off - off % SUBLANES, CONV_ROWS)
                w_j = wdw_ref[c, pl.ds(j, SUBLANES, stride=0), :]
                window = shift_ref[off % SUBLANES - 1, rows, :] if off % SUBLANES else win_ref[rows, lanes]
                term = w_j[None] * window.reshape(groups, SUBLANES, LANES)
                accs[j % CONV_SUMS] = term if accs[j % CONV_SUMS] is None else accs[j % CONV_SUMS] + term
            while len(accs) > 1:
                accs = [a + b for a, b in zip(accs[0::2], accs[1::2])]
            acc_ref[pl.ds(r0, CONV_ROWS), lanes] = accs[0].reshape(CONV_ROWS, LANES)
            return carry

        lax.fori_loop(0, tm // CONV_ROWS, chunk, 0)
    a = acc_ref[...]
    mu = jnp.mean(a, axis=-1, keepdims=True)
    ac = a - mu
    var = jnp.mean(ac * ac, axis=-1, keepdims=True)
    y = ac * lax.rsqrt(var + LN_EPS) * lng_ref[...] + lnb_ref[...]
    y = y * _sigmoid(y)
    o = jnp.dot(y.astype(BF16), wout_ref[...].astype(BF16), preferred_element_type=F32) + bout_ref[...]
    out_ref[...] = (sga_ref[...].astype(F32) * o).astype(out_ref.dtype)


def _conv_branch(acts, w_dw, b_dw, ln_g, ln_b, w_out, b_out):
    tp, d = acts.shape[0], D_MODEL
    tm = ROW_TILE
    per = tm // HALO
    row = pl.BlockSpec((tm, d), lambda i: (i, 0))
    glu = pl.BlockSpec((tm, d), lambda i: (i, ACT_AGLU))
    gate = pl.BlockSpec((tm, d), lambda i: (i, ACT_SGA))
    halo = pl.BlockSpec((HALO, d), lambda i: (jnp.maximum(i * per - 1, 0), ACT_AGLU))
    vec = _const_spec((1, d))
    return pl.pallas_call(
        _conv_kernel,
        grid=(tp // tm,),
        in_specs=[halo, glu, _const_spec(w_dw.shape), vec, vec, vec, _const_spec((d, d)), vec, gate],
        out_specs=row,
        out_shape=jax.ShapeDtypeStruct((tp, d), BF16),
        scratch_shapes=[pltpu.VMEM((tm + HALO, d), F32), pltpu.VMEM((tm, d), F32),
                        pltpu.VMEM((SUBLANES - 1, tm + HALO - SUBLANES, LANES), F32)],
        compiler_params=_cparams(("parallel",)),
        name="conv_branch",
    )(acts, acts, w_dw, b_dw, ln_g, ln_b, w_out, b_out, acts)


def _split_bf16(x):
    hi = x.astype(BF16)
    return hi, (x - hi.astype(F32)).astype(BF16)


def _nt_dot(a, b):
    return lax.dot_general(a, b, (((1,), (1,)), ((), ())), preferred_element_type=F32)


def _tn_dot(a, b):
    return lax.dot_general(a, b, (((0,), (0,)), ((), ())), preferred_element_type=F32)


def _scan_kernel(q_ref, k_ref, v_ref, lf_ref, og_ref, gon_ref, o_ref, st_ref):
    @pl.when(pl.program_id(1) == 0)
    def _():
        st_ref[...] = jnp.zeros_like(st_ref)

    c = CHUNK
    row = lax.broadcasted_iota(jnp.int32, (c, c), 0)
    col = lax.broadcasted_iota(jnp.int32, (c, c), 1)
    causal = col <= row
    diag_mask = jnp.logical_and(causal, row // SUB_BLOCK == col // SUB_BLOCK)
    tri = causal.astype(BF16)

    def bcast_rows(rows):
        return jnp.concatenate([jnp.broadcast_to(r, (SUB_BLOCK, HEAD_DIM)) for r in rows], axis=0)

    heads = [slice(h * HEAD_DIM, (h + 1) * HEAD_DIM) for h in range(N_HEADS)]
    chunks = [slice(ci * c, (ci + 1) * c) for ci in range(SCAN_CHUNKS)]

    bcums = []
    for cs in chunks:
        lf = lf_ref[cs, :]
        lf_hi = lf.astype(BF16)
        lf_mid, lf_lo = _split_bf16(lf - lf_hi.astype(F32))
        bcums.append(jnp.dot(tri, lf_hi, preferred_element_type=F32)
                     + (jnp.dot(tri, lf_mid, preferred_element_type=F32)
                        + jnp.dot(tri, lf_lo, preferred_element_type=F32)))
    operands = {}
    for ci, cs in enumerate(chunks):
        for h, hs in enumerate(heads):
            b = bcums[ci][:, hs]
            q = q_ref[cs, hs].astype(F32)
            k = k_ref[cs, hs].astype(F32)
            ends = [b[i * SUB_BLOCK + SUB_BLOCK - 1:i * SUB_BLOCK + SUB_BLOCK, :] for i in range(N_SUB)]
            mids = [b[i * SUB_BLOCK + SUB_BLOCK // 2 - 1:i * SUB_BLOCK + SUB_BLOCK // 2, :] for i in range(N_SUB)]
            mid_full = bcast_rows(mids)
            b_last = ends[-1]
            zero_row = jnp.zeros_like(b_last)
            qm = q * jnp.exp(b - mid_full)
            km = k * jnp.exp(mid_full - b)
            q_parts, k_parts = [], []
            for j in range(N_SUB - 1):
                q_rows = [jnp.exp(mids[i] - ends[j]) if i > j else zero_row for i in range(N_SUB)]
                k_rows = [jnp.exp(ends[j] - mids[j]) if i == j else zero_row for i in range(N_SUB)]
                q_parts.append((qm * bcast_rows(q_rows)).astype(BF16))
                k_parts.append((km * bcast_rows(k_rows)).astype(BF16))
            q_in = (qm * bcast_rows([jnp.exp(m) for m in mids])).astype(BF16)
            k_out = (km * bcast_rows([jnp.exp(b_last - m) for m in mids])).astype(BF16)
            operands[ci, h] = (qm.astype(BF16), km.astype(BF16), jnp.concatenate(q_parts, axis=1),
                               jnp.concatenate(k_parts, axis=1), q_in, k_out, jnp.exp(b_last))
    products = {}
    for ci, cs in enumerate(chunks):
        for h, hs in enumerate(heads):
            qm, km, q_cat, k_cat, _, k_out, _ = operands[ci, h]
            products[ci, h] = (_nt_dot(qm, km), _nt_dot(q_cat, k_cat), _tn_dot(v_ref[cs, hs], k_out))
    within = {}
    for ci, cs in enumerate(chunks):
        for h, hs in enumerate(heads):
            same_block, earlier_blocks, _ = products[ci, h]
            scores = jnp.where(diag_mask, same_block, 0.0) + earlier_blocks
            within[ci, h] = jnp.dot(scores.astype(BF16), v_ref[cs, hs], preferred_element_type=F32)
    states = [st_ref[h] for h in range(N_HEADS)]
    outs = {}
    for ci in range(SCAN_CHUNKS):
        for h in range(N_HEADS):
            outs[ci, h] = within[ci, h] + _nt_dot(operands[ci, h][4], states[h].astype(BF16))
            states[h] = states[h] * operands[ci, h][6] + products[ci, h][2]
    for h in range(N_HEADS):
        st_ref[h] = states[h]
    for ci, cs in enumerate(chunks):
        for h, hs in enumerate(heads):
            o = outs[ci, h]
            ms = jnp.mean(o * o, axis=-1, keepdims=True)
            on = o * lax.rsqrt(ms + RMS_EPS) * gon_ref[:, hs]
            o_ref[cs, hs] = (on * og_ref[cs, hs].astype(F32)).astype(o_ref.dtype)


def _hgrn2_scan(acts, logf, g_onorm, n_batch):
    tp, d = logf.shape
    rows = SCAN_CHUNKS * CHUNK
    n_chunks = tp // n_batch // rows
    assert n_chunks * rows * n_batch == tp
    col = lambda group: pl.BlockSpec((rows, d), lambda b, c: (b * n_chunks + c, group))
    blk = col(0)
    return pl.pallas_call(
        _scan_kernel,
        grid=(n_batch, n_chunks),
        in_specs=[col(ACT_Q), col(ACT_K), col(ACT_V), blk, col(ACT_OG), pl.BlockSpec((1, d), lambda b, c: (0, 0))],
        out_specs=blk,
        out_shape=jax.ShapeDtypeStruct((tp, d), BF16),
        scratch_shapes=[pltpu.VMEM((N_HEADS, HEAD_DIM, HEAD_DIM), F32)],
        compiler_params=_cparams(("arbitrary", "arbitrary")),
        name="hgrn2_scan",
    )(acts, acts, acts, logf, acts, g_onorm)


def _merge_route_kernel(x_ref, meta_ref, on_ref, ap_ref, sgb_ref, wrnn_ref, wo_ref, gffn_ref, wr_ref, br_ref,
                        h1_ref, u2_ref, info_ref, route_ref, cnt_ref, tri_ref, carry_ref, colcnt_ref, logits_ref, hres_ref,
                        *, tiles_per_batch, n_tiles):
    tm = on_ref.shape[0]
    i = pl.program_id(0)
    hres_ref[...] = _residual_tile(x_ref, meta_ref, jnp.minimum(i, n_tiles - 1) % tiles_per_batch == 0)

    @pl.when(i == 0)
    def _():
        r_i = lax.broadcasted_iota(jnp.int32, (tm, tm), 0)
        c_i = lax.broadcasted_iota(jnp.int32, (tm, tm), 1)
        tri_ref[...] = (c_i < r_i).astype(BF16)
        carry_ref[...] = jnp.zeros_like(carry_ref)
        colcnt_ref[...] = jnp.zeros_like(colcnt_ref)
        logits_ref[...] = jnp.zeros_like(logits_ref)

    routed = i > 0
    lane = lax.broadcasted_iota(jnp.int32, (tm, LANES), 1)
    cur = jnp.where(lane < N_EXPERTS, logits_ref[...], NEG_BIG)
    vals, idxs, sels = [], [], []

    def topk_round(cur):
        m = jnp.max(cur, axis=-1, keepdims=True)
        idx = jnp.min(jnp.where(cur == m, lane, LANES), axis=-1, keepdims=True)
        sel = lane == idx
        vals.append(m)
        idxs.append(idx)
        sels.append(sel)
        return jnp.where(sel, 2.0 * NEG_BIG, cur)

    w_hi, w_lo = _split_bf16(wr_ref[...])
    w_hi_lo = jnp.concatenate([w_hi, w_lo], axis=1)
    part = tm // MERGE_PARTS
    rows = [slice(p * part, (p + 1) * part) for p in range(MERGE_PARTS)]
    w_rnn = wrnn_ref[...].astype(BF16)
    w_out = wo_ref[...].astype(BF16)
    rs = [jnp.dot(on_ref[rw, :], w_rnn, preferred_element_type=F32) for rw in rows]
    cur = topk_round(cur)
    ys = [(ap_ref[rw, :].astype(F32) + sgb_ref[rw, :].astype(F32) * r).astype(BF16) for rw, r in zip(rows, rs)]
    h1s = [hres_ref[rw, :] + jnp.dot(y, w_out, preferred_element_type=F32) for rw, y in zip(rows, ys)]
    cur = topk_round(cur)
    u2s = []
    for rw, h1 in zip(rows, h1s):
        h1_ref[rw, :] = h1
        ms = jnp.mean(h1 * h1, axis=-1, keepdims=True)
        u2 = h1 * lax.rsqrt(ms + RMS_EPS) * gffn_ref[...]
        u2_ref[rw, :] = _pack_rows(u2)
        u2s.append(u2)
    cur = topk_round(cur)
    new_logits = []
    for u2 in u2s:
        u_hi, u_lo = _split_bf16(u2)
        both = jnp.dot(u_hi, w_hi_lo, preferred_element_type=F32)
        new_logits.append(both[:, :LANES] + (jnp.dot(u_lo, w_hi, preferred_element_type=F32) + both[:, LANES:])
                          + br_ref[...])
    cur = topk_round(cur)
    assert len(vals) == TOP_K
    exps = [jnp.exp(vk - vals[0]) for vk in vals]
    den = exps[0] + exps[1] + exps[2] + exps[3]
    onehot = jnp.logical_or(jnp.logical_or(sels[0], sels[1]), jnp.logical_or(sels[2], sels[3]))
    onehot = jnp.logical_and(onehot, routed)
    onehot_bf = onehot.astype(BF16)
    rank_all = jnp.dot(tri_ref[...], onehot_bf, preferred_element_type=F32) + carry_ref[...]
    carry_ref[...] += jnp.sum(onehot.astype(F32), axis=0, keepdims=True)
    colcnt_ref[...] += _tn_dot(onehot_bf, jnp.ones((tm, LANES), BF16))
    info = jnp.zeros((tm, LANES), F32)
    for kk in range(TOP_K):
        rank_k = jnp.sum(jnp.where(sels[kk], rank_all, 0.0), axis=-1, keepdims=True)
        info = jnp.where(lane == kk, exps[kk] / den, info)
        info = jnp.where(lane == TOP_K + kk, idxs[kk].astype(F32), info)
        info = jnp.where(lane == 2 * TOP_K + kk, rank_k, info)
    info_ref[...] = info
    route_ref[...] = info.T[0:ROUTE_ROWS, :]
    cnt_ref[...] = colcnt_ref[...]
    logits_ref[...] = jnp.concatenate(new_logits, axis=0)


def _merge_route(x2, meta, seq, on, ap, acts, w_rnn, w_o, g_ffn, w_router_pad, b_router_pad):
    tp, d = on.shape
    tm = ROW_TILE
    n_tiles = tp // tm
    tpb = (seq + CHUNK) // tm
    assert tpb * tm == seq + CHUNK
    this_tile = lambda i: (jnp.minimum(i, n_tiles - 1), 0)
    prev_tile = lambda i: (jnp.maximum(i - 1, 0), 0)
    row = pl.BlockSpec((tm, d), this_tile)
    vec = _const_spec((1, d))
    return pl.pallas_call(
        functools.partial(_merge_route_kernel, tiles_per_batch=tpb, n_tiles=n_tiles),
        grid=(n_tiles + 1,),
        in_specs=[_residual_spec(tm, d, seq, tpb, n_tiles), _const_spec(meta.shape), row, row,
                  pl.BlockSpec((tm, d), lambda i: (jnp.minimum(i, n_tiles - 1), ACT_SGB)),
                  _const_spec((d, d)), _const_spec((d, d)), vec, _const_spec((d, LANES)), _const_spec((1, LANES))],
        out_specs=[row, pl.BlockSpec((tm, PACK_WORDS), this_tile),
                   pl.BlockSpec((tm, LANES), prev_tile),
                   pl.BlockSpec((ROUTE_ROWS, tm), lambda i: (0, jnp.maximum(i - 1, 0))), _const_spec((LANES, LANES))],
        out_shape=[jax.ShapeDtypeStruct((tp, d), F32), jax.ShapeDtypeStruct((tp, PACK_WORDS), jnp.uint32),
                   jax.ShapeDtypeStruct((tp, LANES), F32), jax.ShapeDtypeStruct((ROUTE_ROWS, tp), F32),
                   jax.ShapeDtypeStruct((LANES, LANES), F32)],
        scratch_shapes=[pltpu.VMEM((tm, tm), BF16), pltpu.VMEM((1, LANES), F32),
                        pltpu.VMEM((LANES, LANES), F32), pltpu.VMEM((tm, LANES), F32), pltpu.VMEM((tm, d), F32)],
        compiler_params=_cparams(("arbitrary",)),
        name="merge_route",
    )(x2, meta, on, ap, acts, w_rnn, w_o, g_ffn, w_router_pad, b_router_pad)


def _slots_kernel(route_ref, cnt_ref, dest_ref, te_ref, pstart_ref):
    tl = route_ref.shape[1]
    n_tile_lanes = te_ref.shape[1]

    @pl.when(pl.program_id(0) == 0)
    def _():
        r_i = lax.broadcasted_iota(jnp.int32, (LANES, LANES), 0)
        c_i = lax.broadcasted_iota(jnp.int32, (LANES, LANES), 1)
        cnt = cnt_ref[...]
        tiles = jnp.floor((cnt + (EXPERT_TILE - 1)) / EXPERT_TILE)
        pend_col = jnp.dot((c_i <= r_i).astype(F32), tiles, preferred_element_type=F32,
                           precision=lax.Precision.HIGHEST)
        pstart_ref[...] = pend_col - tiles
        tile_id = lax.broadcasted_iota(jnp.int32, (LANES, n_tile_lanes), 1).astype(F32)
        exp_id = lax.broadcasted_iota(jnp.int32, (LANES, n_tile_lanes), 0)
        pend_wide = jnp.concatenate([pend_col] * (n_tile_lanes // LANES), axis=1)
        below = jnp.logical_and(pend_wide <= tile_id, exp_id < N_EXPERTS)
        te = jnp.minimum(jnp.sum(below.astype(F32), axis=0, keepdims=True), N_EXPERTS - 1.0)
        n_active = pend_col[N_EXPERTS - 1:N_EXPERTS, 0:1]
        owner = exp_id.astype(F32) == te
        cnt_wide = jnp.concatenate([cnt] * (n_tile_lanes // LANES), axis=1)
        first_tile = pend_wide - jnp.concatenate([tiles] * (n_tile_lanes // LANES), axis=1)
        left = jnp.sum(jnp.where(owner, cnt_wide - (tile_id - first_tile) * EXPERT_TILE, 0.0), axis=0, keepdims=True)
        valid = jnp.clip(left, 0.0, float(EXPERT_TILE))
        sub_t = lax.broadcasted_iota(jnp.int32, (SUBLANES, n_tile_lanes), 0)
        out = jnp.where(sub_t == 0, jnp.broadcast_to(te, (SUBLANES, n_tile_lanes)),
                        jnp.where(sub_t == 1, jnp.broadcast_to(n_active, (SUBLANES, n_tile_lanes)),
                                  jnp.broadcast_to(valid, (SUBLANES, n_tile_lanes))))
        te_ref[...] = out.astype(jnp.int32)

    route = route_ref[...]
    expert = lax.broadcasted_iota(jnp.int32, (N_EXPERTS, tl), 0)
    first_slot = jnp.concatenate([pstart_ref[0:N_EXPERTS, :]] * (tl // LANES), axis=1) * EXPERT_TILE
    rows = []
    for kk in range(TOP_K):
        idx_k = route[TOP_K + kk:TOP_K + kk + 1, :].astype(jnp.int32)
        start_k = jnp.sum(jnp.where(expert == idx_k, first_slot, 0.0), axis=0, keepdims=True)
        rows.append(start_k + route[2 * TOP_K + kk:2 * TOP_K + kk + 1, :])
    rows.append(jnp.zeros((SUBLANES - TOP_K, tl), F32))
    dest_ref[...] = jnp.concatenate(rows, axis=0).astype(jnp.int32)


def _slots(route, cnt, n_tiles_max):
    tp = route.shape[1]
    tl = tp // SLOTS_STEPS
    assert tl * SLOTS_STEPS == tp and tl % LANES == 0
    n_tile_lanes = -(-n_tiles_max // LANES) * LANES
    return pl.pallas_call(
        _slots_kernel,
        grid=(SLOTS_STEPS,),
        in_specs=[pl.BlockSpec((ROUTE_ROWS, tl), lambda i: (0, i)), _const_spec((LANES, LANES))],
        out_specs=[pl.BlockSpec((SUBLANES, tl), lambda i: (0, i)), _const_spec((8, n_tile_lanes))],
        out_shape=[jax.ShapeDtypeStruct((SUBLANES, tp), jnp.int32),
                   jax.ShapeDtypeStruct((8, n_tile_lanes), jnp.int32)],
        scratch_shapes=[pltpu.VMEM((LANES, LANES), F32)],
        compiler_params=_cparams(("arbitrary",)),
        name="slots",
    )(route, cnt)


def _sc_workers():
    sc = plsc.get_sparse_core_info()
    return sc.num_cores, sc.num_cores * sc.num_subcores


def _sc_token_chunks(tp, chunk, body):
    n_cores, n_workers = _sc_workers()
    per_worker = tp // n_workers
    assert per_worker * n_workers == tp and per_worker % chunk == 0 and chunk % SUBLANES == 0
    base = (lax.axis_index("s") * n_cores + lax.axis_index("c")) * per_worker

    @pl.loop(0, per_worker // chunk)
    def _(ci):
        body(pl.multiple_of(base + ci * chunk, SUBLANES))


def _sc_kernel(out_type, chunk, n_row_bufs):
    return functools.partial(
        pl.kernel, mesh=plsc.VectorSubcoreMesh(core_axis_name="c", subcore_axis_name="s"), out_type=out_type,
        scratch_types=[pltpu.VMEM((TOP_K, chunk), jnp.int32), pltpu.VMEM((n_row_bufs, chunk, PACK_WORDS), jnp.uint32),
                       pltpu.SemaphoreType.DMA, pltpu.SemaphoreType.DMA])


def _wait_all(copies):
    for c in copies:
        c.wait()


def _dispatch(dest_flat, u2p, n_slots):
    tp = u2p.shape[0]
    ch = SC_SCATTER_CHUNK

    @_sc_kernel(jax.ShapeDtypeStruct((n_slots, PACK_WORDS), jnp.uint32), ch, 1)
    def scatter_rows(u2_hbm, dest_hbm, xs_hbm, idx_v, rows_v, sem_a, sem_b):
        def chunk(off):
            loads = [pltpu.async_copy(u2_hbm.at[pl.ds(off, ch)], rows_v.at[0], sem_a)]
            loads += [pltpu.async_copy(dest_hbm.at[pl.ds(kk * tp + off, ch)], idx_v.at[kk], sem_a)
                      for kk in range(TOP_K)]
            _wait_all(loads)
            _wait_all([pltpu.async_copy(rows_v.at[0], xs_hbm.at[idx_v.at[kk]], sem_b) for kk in range(TOP_K)])

        _sc_token_chunks(tp, ch, chunk)

    return scatter_rows(u2p, dest_flat)


def _gather_expert_rows(dest_flat, ys):
    tp = dest_flat.shape[0] // TOP_K
    ch = SC_GATHER_CHUNK

    @_sc_kernel(jax.ShapeDtypeStruct((TOP_K, tp, PACK_WORDS), jnp.uint32), ch, TOP_K)
    def gather_rows(ys_hbm, dest_hbm, out_hbm, idx_v, rows_v, sem_a, sem_b):
        def chunk(off):
            _wait_all([pltpu.async_copy(dest_hbm.at[pl.ds(kk * tp + off, ch)], idx_v.at[kk], sem_a)
                       for kk in range(TOP_K)])
            _wait_all([pltpu.async_copy(ys_hbm.at[idx_v.at[kk]], rows_v.at[kk], sem_b) for kk in range(TOP_K)])
            _wait_all([pltpu.async_copy(rows_v.at[kk], out_hbm.at[kk, pl.ds(off, ch)], sem_a)
                       for kk in range(TOP_K)])

        _sc_token_chunks(tp, ch, chunk)

    return gather_rows(ys, dest_flat)


def _expert_kernel(te_ref, nact_ref, valid_ref, x_ref, wgu_hbm, bgu_ref, wdn_hbm, bdn_ref, y_ref,
                   wgu_buf, wdn_buf, slot_ref, sem):
    i = pl.program_id(0)
    n_active = nact_ref[0]
    last_tile = pl.num_programs(0) - 1

    def fetch(expert, slot):
        return (pltpu.make_async_copy(wgu_hbm.at[expert], wgu_buf.at[slot], sem.at[slot, 0]),
                pltpu.make_async_copy(wdn_hbm.at[expert], wdn_buf.at[slot], sem.at[slot, 1]))

    @pl.when(i == 0)
    def _():
        slot_ref[0] = 1
        for copy in fetch(te_ref[0], 0):
            copy.start()

    @pl.when(i < n_active)
    def _():
        expert = te_ref[i]
        first_tile_of_expert = jnp.logical_or(i == 0, expert != te_ref[jnp.maximum(i - 1, 0)])

        @pl.when(first_tile_of_expert)
        def _():
            slot = 1 - slot_ref[0]
            slot_ref[0] = slot
            for copy in fetch(expert, slot):
                copy.wait()
            nxt = lax.while_loop(
                lambda j: jnp.logical_and(j < n_active, te_ref[jnp.minimum(j, last_tile)] == expert),
                lambda j: j + 1, i + 1)

            @pl.when(nxt < n_active)
            def _():
                for copy in fetch(te_ref[jnp.minimum(nxt, last_tile)], 1 - slot):
                    copy.start()

        slot = slot_ref[0]

        def mlp(rows):
            x = _unpack_rows(x_ref[0:rows, :]).astype(BF16)
            gu = jnp.dot(x, wgu_buf[slot].astype(BF16), preferred_element_type=F32) + bgu_ref[0]
            gate = jnp.minimum(gu[:, :D_FF], SWIGLU_LIMIT)
            up = jnp.clip(gu[:, D_FF:], -SWIGLU_LIMIT, SWIGLU_LIMIT)
            hdn = (up + 1.0) * (gate * _sigmoid(SWIGLU_ALPHA * gate))
            y = jnp.dot(hdn.astype(BF16), wdn_buf[slot].astype(BF16), preferred_element_type=F32) + bdn_ref[0]
            y_ref[0:rows, :] = _pack_rows(y)

        half = EXPERT_TILE // 2
        half_empty = valid_ref[i] <= half
        pl.when(jnp.logical_not(half_empty))(lambda: mlp(EXPERT_TILE))

        @pl.when(half_empty)
        def _():
            mlp(half)
            y_ref[half:, :] = jnp.zeros((EXPERT_TILE - half, PACK_WORDS), y_ref.dtype)

    @pl.when(i >= n_active)
    def _():
        y_ref[...] = jnp.zeros_like(y_ref)


def _expert_mlp(te, nact, valid, xs, w_gu, b_gu, w_dn, b_dn):
    d = D_MODEL
    n_tiles = xs.shape[0] // EXPERT_TILE

    def tile_map(i, te_ref, nact_ref, valid_ref):
        return (jnp.minimum(i, nact_ref[0] - 1), 0)

    def exp_map(i, te_ref, nact_ref, valid_ref):
        return (te_ref[jnp.minimum(i, nact_ref[0] - 1)], 0, 0)

    grid_spec = pltpu.PrefetchScalarGridSpec(
        num_scalar_prefetch=3,
        grid=(n_tiles,),
        in_specs=[pl.BlockSpec((EXPERT_TILE, PACK_WORDS), tile_map),
                  pl.BlockSpec(memory_space=pl.ANY),
                  pl.BlockSpec((1, 1, 2 * D_FF), exp_map),
                  pl.BlockSpec(memory_space=pl.ANY),
                  pl.BlockSpec((1, 1, d), exp_map)],
        out_specs=pl.BlockSpec((EXPERT_TILE, PACK_WORDS), lambda i, te_ref, nact_ref, valid_ref: (i, 0)),
        scratch_shapes=[pltpu.VMEM((2, d, 2 * D_FF), F32), pltpu.VMEM((2, D_FF, d), F32),
                        pltpu.SMEM((1,), jnp.int32), pltpu.SemaphoreType.DMA((2, 2))],
    )
    return pl.pallas_call(
        _expert_kernel,
        grid_spec=grid_spec,
        out_shape=jax.ShapeDtypeStruct(xs.shape, jnp.uint32),
        compiler_params=_cparams(("arbitrary",)),
        name="expert_mlp",
    )(te, nact, valid, xs, w_gu, b_gu, w_dn, b_dn)


def _combine_kernel(gfin_ref, *refs):
    out_ref = refs[-1]
    for s in range(COMBINE_SUB):
        h1_ref, info_ref, yg_ref = refs[3 * s:3 * s + 3]
        info = info_ref[...]
        h2 = h1_ref[...]
        for kk in range(TOP_K):
            h2 = h2 + info[:, kk:kk + 1] * _unpack_rows(yg_ref[kk])
        ms = jnp.mean(h2 * h2, axis=-1, keepdims=True)
        out_ref[s * CHUNK:(s + 1) * CHUNK, :] = h2 * lax.rsqrt(ms + RMS_EPS) * gfin_ref[...]


def _combine(h1, info, g_final, yg, n_batch, seq):
    d = h1.shape[1]
    chunks = seq // CHUNK

    def padded(s):
        def index(j):
            c = j * COMBINE_SUB + s
            return (c // chunks) * (chunks + 1) + c % chunks + 1
        return index

    in_specs = [pl.BlockSpec((1, d), lambda j: (0, 0))]
    operands = [g_final]
    for s in range(COMBINE_SUB):
        chunk_of = padded(s)
        in_specs += [pl.BlockSpec((CHUNK, d), lambda j, f=chunk_of: (f(j), 0)),
                     pl.BlockSpec((CHUNK, LANES), lambda j, f=chunk_of: (f(j), 0)),
                     pl.BlockSpec((TOP_K, CHUNK, PACK_WORDS), lambda j, f=chunk_of: (0, f(j), 0))]
        operands += [h1, info, yg]
    return pl.pallas_call(
        _combine_kernel,
        grid=(n_batch * chunks // COMBINE_SUB,),
        in_specs=in_specs,
        out_specs=pl.BlockSpec((COMBINE_SUB * CHUNK, d), lambda j: (j, 0)),
        out_shape=jax.ShapeDtypeStruct((n_batch * seq, d), F32),
        compiler_params=_cparams(("parallel",)),
        name="combine",
    )(*operands)


def kernel(x, meta_tokens, lb_logits, g_mix, w_in, w_dw, b_dw, ln_g, ln_b, w_conv_out, b_conv_out,
           g_onorm, w_rnn_out, w_o, g_ffn, w_router, b_router, w_gate_up, b_gate_up, w_down, b_down,
           g_final):
    n_batch, seq, d = x.shape
    assert d == D_MODEL and w_in.shape[0] == 1, "single-layer block with D_MODEL features"
    assert seq % (CHUNK * COMBINE_SUB) == 0
    seq_pad = CHUNK + seq
    tp = n_batch * seq_pad
    assert tp % ROW_TILE == 0

    x2 = x.reshape(n_batch * seq, d)
    meta = meta_tokens.astype(x.dtype)
    vec = lambda a: a.reshape(1, -1).astype(F32)

    acts, logf = _in_proj(x2, meta, seq, vec(g_mix[0]), lb_logits.astype(F32), w_in[0])
    w_taps = w_dw[0].astype(F32).reshape(CONV_WIDTH, d // LANES, LANES).transpose(1, 0, 2)
    ap = _conv_branch(acts, w_taps, vec(b_dw[0]), vec(ln_g[0]), vec(ln_b[0]),
                      w_conv_out[0], vec(b_conv_out[0]))
    on = _hgrn2_scan(acts, logf, vec(g_onorm[0]), n_batch)
    w_router_pad = jnp.pad(w_router[0].astype(F32), ((0, 0), (0, LANES - N_EXPERTS)))
    b_router_pad = jnp.pad(vec(b_router[0]), ((0, 0), (0, LANES - N_EXPERTS)))
    h1, u2p, info, route, cnt = _merge_route(x2, meta, seq, on, ap, acts, w_rnn_out[0], w_o[0],
                                      vec(g_ffn[0]), w_router_pad, b_router_pad)

    n_tiles_max = -(-(tp * TOP_K + N_EXPERTS * (EXPERT_TILE - 1)) // EXPERT_TILE)
    dest, te = _slots(route, cnt, n_tiles_max)
    dest_flat = dest[:TOP_K].reshape(-1)
    xs = _dispatch(dest_flat, u2p, n_tiles_max * EXPERT_TILE)
    ys = _expert_mlp(te[0, :n_tiles_max], te[1, :1], te[2, :n_tiles_max], xs, w_gate_up[0],
                     b_gate_up[0].reshape(N_EXPERTS, 1, -1).astype(F32), w_down[0],
                     b_down[0].reshape(N_EXPERTS, 1, -1).astype(F32))
    yg = _gather_expert_rows(dest_flat, ys)
    out = _combine(h1, info, vec(g_final), yg, n_batch, seq)
    return out.reshape(n_batch, seq, d)
```

```python
import functools

import jax
import jax.numpy as jnp
from jax import lax
from jax.experimental import pallas as pl
from jax.experimental.pallas import tpu as pltpu
from jax.experimental.pallas import tpu_sc as plsc

F32 = jnp.float32
BF16 = jnp.bfloat16

D_MODEL = 1024
N_META = 16
CHUNK = 128
CHUNK_PAD = CHUNK - N_META
CONV_WIDTH = 31
HEAD_DIM = 128
N_HEADS = D_MODEL // HEAD_DIM
N_EXPERTS = 32
TOP_K = 4
D_FF = D_MODEL
SWIGLU_LIMIT = 7.0
SWIGLU_ALPHA = 1.702
RMS_EPS = 1e-6
LN_EPS = 1e-5

ACT_AGLU, ACT_Q, ACT_K, ACT_V, ACT_OG, ACT_SGA, ACT_SGB = range(7)
N_ACTS = 7

LANES = 128
SUBLANES = 8
SUB_BLOCK = 32
N_SUB = CHUNK // SUB_BLOCK
HALO = 32
ROW_TILE = 640
IN_TILE = 320
SCAN_CHUNKS = 5
MERGE_PARTS = 2
SLOTS_STEPS = 5
ROUTE_ROWS = 16
CONV_ROWS = 64
CONV_SUMS = 4
EXPERT_TILE = 512
EXPERT_PART = 128
SC_MAX_INDICES = 128
SC_ROW_BUFFER_BYTES = 384 * 1024
COMBINE_SUB = 4
NEG_BIG = -1e30
VMEM_LIMIT = 56 * 1024 * 1024


def _sigmoid(x):
    return 1.0 / (1.0 + jnp.exp(-x))


def _cparams(sem):
    return pltpu.CompilerParams(dimension_semantics=sem, vmem_limit_bytes=VMEM_LIMIT)


def _const_spec(shape):
    nd = len(shape)
    return pl.BlockSpec(shape, lambda *_: (0,) * nd)


PACK_WORDS = D_MODEL // 2
HIGH_HALF = 0xFFFF0000


def _pack_rows(x):
    lo = lax.bitcast_convert_type(x[:, :PACK_WORDS].astype(BF16).astype(F32), jnp.uint32)
    hi = lax.bitcast_convert_type(x[:, PACK_WORDS:].astype(BF16).astype(F32), jnp.uint32)
    return (lo >> 16) | (hi & jnp.uint32(HIGH_HALF))


def _unpack_rows(w):
    lo = lax.bitcast_convert_type(w << 16, F32)
    hi = lax.bitcast_convert_type(w & jnp.uint32(HIGH_HALF), F32)
    return jnp.concatenate([lo, hi], axis=1)


def _residual_tile(x_ref, meta_ref, is_first):
    x = x_ref[...]
    tm, d = x.shape
    prefix = jnp.concatenate([jnp.zeros((CHUNK_PAD, d), x.dtype), meta_ref[...]], axis=0)
    first = jnp.concatenate([prefix, x[:tm - CHUNK, :]], axis=0)
    return jnp.where(is_first, first, x)


def _residual_spec(tm, d, seq, tiles_per_batch, n_tiles):
    def start(i):
        i = jnp.minimum(i, n_tiles - 1)
        b, t = i // tiles_per_batch, i % tiles_per_batch
        return (pl.multiple_of(b * seq + jnp.maximum(t * tm - CHUNK, 0), SUBLANES), 0)
    return pl.BlockSpec((pl.Element(tm), pl.Element(d)), start)


def _in_proj_kernel(x_ref, meta_ref, g_ref, lbl_ref, w_ref, acts_ref, logf_ref, *, tiles_per_batch):
    d = D_MODEL

    def put(group, value):
        acts_ref[:, group * d:(group + 1) * d] = value.astype(acts_ref.dtype)

    h = _residual_tile(x_ref, meta_ref, pl.program_id(0) % tiles_per_batch == 0)
    ms = jnp.mean(h * h, axis=-1, keepdims=True)
    u = (h * lax.rsqrt(ms + RMS_EPS) * g_ref[...]).astype(BF16)

    def proj(j):
        return jnp.dot(u, w_ref[:, j * d:(j + 1) * d].astype(BF16), preferred_element_type=F32)

    put(ACT_AGLU, proj(0) * _sigmoid(proj(1)))
    zq = proj(2)
    put(ACT_Q, zq * _sigmoid(zq))
    lbl = lbl_ref[...]
    e = jnp.exp(lbl - jnp.max(lbl, axis=0, keepdims=True))
    lb = e[0:1, :] / jnp.sum(e, axis=0, keepdims=True)
    s = _sigmoid(proj(3))
    logf_ref[...] = jnp.log(lb + (1.0 - lb) * s)
    put(ACT_K, (1.0 - lb) * (1.0 - s))
    put(ACT_V, proj(4))
    zg = proj(5)
    put(ACT_OG, zg * _sigmoid(zg))
    put(ACT_SGA, _sigmoid(proj(6)))
    put(ACT_SGB, _sigmoid(proj(7)))


def _in_proj(x2, meta, seq, g_mix, lb_logits, w_in):
    d = x2.shape[1]
    n_batch = x2.shape[0] // seq
    tm = IN_TILE
    tpb = (seq + CHUNK) // tm
    tp = n_batch * (seq + CHUNK)
    assert tpb * tm == seq + CHUNK and tm > CHUNK
    row = pl.BlockSpec((tm, d), lambda i: (i, 0))
    return pl.pallas_call(
        functools.partial(_in_proj_kernel, tiles_per_batch=tpb),
        grid=(tp // tm,),
        in_specs=[_residual_spec(tm, d, seq, tpb, tp // tm), _const_spec(meta.shape), _const_spec((1, d)),
                  _const_spec(lb_logits.shape),
                  pl.BlockSpec(w_in.shape, lambda i: (0, 0), pipeline_mode=pl.Buffered(1))],
        out_specs=[pl.BlockSpec((tm, N_ACTS * d), lambda i: (i, 0)), row],
        out_shape=[jax.ShapeDtypeStruct((tp, N_ACTS * d), BF16), jax.ShapeDtypeStruct((tp, d), F32)],
        compiler_params=_cparams(("parallel",)),
        name="in_proj",
    )(x2, meta, g_mix, lb_logits, w_in)


def _conv_kernel(halo_ref, cur_ref, wdw_ref, bdw_ref, lng_ref, lnb_ref, wout_ref, bout_ref, sga_ref,
                 out_ref, win_ref, acc_ref, shift_ref):
    tm = cur_ref.shape[0]
    win_ref[0:HALO, :] = halo_ref[...].astype(F32)
    win_ref[HALO:, :] = cur_ref[...].astype(F32)
    first_tap = HALO - (CONV_WIDTH - 1)
    for c in range(D_MODEL // LANES):
        lanes = slice(c * LANES, (c + 1) * LANES)
        n_rows = tm + HALO - SUBLANES
        for s in range(1, SUBLANES):
            shift_ref[s - 1, 0:n_rows, :] = win_ref[pl.ds(s, n_rows), lanes]
        bias = jnp.broadcast_to(bdw_ref[:, lanes], (CONV_ROWS, LANES))
        groups = CONV_ROWS // SUBLANES

        def chunk(r, carry, c=c, lanes=lanes, bias=bias):
            r0 = pl.multiple_of(r * CONV_ROWS, CONV_ROWS)
            accs = [bias.reshape(groups, SUBLANES, LANES)] + [None] * (CONV_SUMS - 1)
            for j in range(CONV_WIDTH):
                off = first_tap + j
                rows = pl.ds(r0 + off - off % SUBLANES, CONV_ROWS)
                w_j = wdw_ref[c, pl.ds(j, SUBLANES, stride=0), :]
                window = shift_ref[off % SUBLANES - 1, rows, :] if off % SUBLANES else win_ref[rows, lanes]
                term = w_j[None] * window.reshape(groups, SUBLANES, LANES)
                accs[j % CONV_SUMS] = term if accs[j % CONV_SUMS] is None else accs[j % CONV_SUMS] + term
            while len(accs) > 1:
                accs = [a + b for a, b in zip(accs[0::2], accs[1::2])]
            acc_ref[pl.ds(r0, CONV_ROWS), lanes] = accs[0].reshape(CONV_ROWS, LANES)
            return carry

        lax.fori_loop(0, tm // CONV_ROWS, chunk, 0)
    a = acc_ref[...]
    mu = jnp.mean(a, axis=-1, keepdims=True)
    ac = a - mu
    var = jnp.mean(ac * ac, axis=-1, keepdims=True)
    y = ac * lax.rsqrt(var + LN_EPS) * lng_ref[...] + lnb_ref[...]
    y = y * _sigmoid(y)
    o = jnp.dot(y.astype(BF16), wout_ref[...].astype(BF16), preferred_element_type=F32) + bout_ref[...]
    out_ref[...] = (sga_ref[...].astype(F32) * o).astype(out_ref.dtype)


def _conv_branch(acts, w_dw, b_dw, ln_g, ln_b, w_out, b_out):
    tp, d = acts.shape[0], D_MODEL
    tm = ROW_TILE
    per = tm // HALO
    row = pl.BlockSpec((tm, d), lambda i: (i, 0))
    glu = pl.BlockSpec((tm, d), lambda i: (i, ACT_AGLU))
    gate = pl.BlockSpec((tm, d), lambda i: (i, ACT_SGA))
    halo = pl.BlockSpec((HALO, d), lambda i: (jnp.maximum(i * per - 1, 0), ACT_AGLU))
    vec = _const_spec((1, d))
    return pl.pallas_call(
        _conv_kernel,
        grid=(tp // tm,),
        in_specs=[halo, glu, _const_spec(w_dw.shape), vec, vec, vec, _const_spec((d, d)), vec, gate],
        out_specs=row,
        out_shape=jax.ShapeDtypeStruct((tp, d), BF16),
        scratch_shapes=[pltpu.VMEM((tm + HALO, d), F32), pltpu.VMEM((tm, d), F32),
                        pltpu.VMEM((SUBLANES - 1, tm + HALO - SUBLANES, LANES), F32)],
        compiler_params=_cparams(("parallel",)),
        name="conv_branch",
    )(acts, acts, w_dw, b_dw, ln_g, ln_b, w_out, b_out, acts)


def _split_bf16(x):
    hi = x.astype(BF16)
    return hi, (x - hi.astype(F32)).astype(BF16)


def _nt_dot(a, b):
    return lax.dot_general(a, b, (((1,), (1,)), ((), ())), preferred_element_type=F32)


def _tn_dot(a, b):
    return lax.dot_general(a, b, (((0,), (0,)), ((), ())), preferred_element_type=F32)


def _scan_kernel(q_ref, k_ref, v_ref, lf_ref, og_ref, gon_ref, o_ref, st_ref):
    @pl.when(pl.program_id(1) == 0)
    def _():
        st_ref[...] = jnp.zeros_like(st_ref)

    c = CHUNK
    row = lax.broadcasted_iota(jnp.int32, (c, c), 0)
    col = lax.broadcasted_iota(jnp.int32, (c, c), 1)
    causal = col <= row
    diag_mask = jnp.logical_and(causal, row // SUB_BLOCK == col // SUB_BLOCK)
    tri = causal.astype(BF16)

    def bcast_rows(rows):
        return jnp.concatenate([jnp.broadcast_to(r, (SUB_BLOCK, HEAD_DIM)) for r in rows], axis=0)

    heads = [slice(h * HEAD_DIM, (h + 1) * HEAD_DIM) for h in range(N_HEADS)]
    chunks = [slice(ci * c, (ci + 1) * c) for ci in range(SCAN_CHUNKS)]

    bcums = []
    for cs in chunks:
        lf = lf_ref[cs, :]
        lf_hi = lf.astype(BF16)
        lf_mid, lf_lo = _split_bf16(lf - lf_hi.astype(F32))
        bcums.append(jnp.dot(tri, lf_hi, preferred_element_type=F32)
                     + (jnp.dot(tri, lf_mid, preferred_element_type=F32)
                        + jnp.dot(tri, lf_lo, preferred_element_type=F32)))
    operands = {}
    for ci, cs in enumerate(chunks):
        for h, hs in enumerate(heads):
            b = bcums[ci][:, hs]
            q = q_ref[cs, hs].astype(F32)
            k = k_ref[cs, hs].astype(F32)
            ends = [b[i * SUB_BLOCK + SUB_BLOCK - 1:i * SUB_BLOCK + SUB_BLOCK, :] for i in range(N_SUB)]
            mids = [b[i * SUB_BLOCK + SUB_BLOCK // 2 - 1:i * SUB_BLOCK + SUB_BLOCK // 2, :] for i in range(N_SUB)]
            mid_full = bcast_rows(mids)
            b_last = ends[-1]
            zero_row = jnp.zeros_like(b_last)
            qm = q * jnp.exp(b - mid_full)
            km = k * jnp.exp(mid_full - b)
            q_parts, k_parts = [], []
            for j in range(N_SUB - 1):
                q_rows = [jnp.exp(mids[i] - ends[j]) if i > j else zero_row for i in range(N_SUB)]
                k_rows = [jnp.exp(ends[j] - mids[j]) if i == j else zero_row for i in range(N_SUB)]
                q_parts.append((qm * bcast_rows(q_rows)).astype(BF16))
                k_parts.append((km * bcast_rows(k_rows)).astype(BF16))
            q_in = (qm * bcast_rows([jnp.exp(m) for m in mids])).astype(BF16)
            k_out = (km * bcast_rows([jnp.exp(b_last - m) for m in mids])).astype(BF16)
            operands[ci, h] = (qm.astype(BF16), km.astype(BF16), jnp.concatenate(q_parts, axis=1),
                               jnp.concatenate(k_parts, axis=1), q_in, k_out, jnp.exp(b_last))
    products = {}
    for ci, cs in enumerate(chunks):
        for h, hs in enumerate(heads):
            qm, km, q_cat, k_cat, _, k_out, _ = operands[ci, h]
            products[ci, h] = (_nt_dot(qm, km), _nt_dot(q_cat, k_cat), _tn_dot(v_ref[cs, hs], k_out))
    within = {}
    for ci, cs in enumerate(chunks):
        for h, hs in enumerate(heads):
            same_block, earlier_blocks, _ = products[ci, h]
            scores = jnp.where(diag_mask, same_block, 0.0) + earlier_blocks
            within[ci, h] = jnp.dot(scores.astype(BF16), v_ref[cs, hs], preferred_element_type=F32)
    states = [st_ref[h] for h in range(N_HEADS)]
    outs = {}
    for ci in range(SCAN_CHUNKS):
        for h in range(N_HEADS):
            outs[ci, h] = within[ci, h] + _nt_dot(operands[ci, h][4], states[h].astype(BF16))
            states[h] = states[h] * operands[ci, h][6] + products[ci, h][2]
    for h in range(N_HEADS):
        st_ref[h] = states[h]
    for ci, cs in enumerate(chunks):
        for h, hs in enumerate(heads):
            o = outs[ci, h]
            ms = jnp.mean(o * o, axis=-1, keepdims=True)
            on = o * lax.rsqrt(ms + RMS_EPS) * gon_ref[:, hs]
            o_ref[cs, hs] = (on * og_ref[cs, hs].astype(F32)).astype(o_ref.dtype)


def _hgrn2_scan(acts, logf, g_onorm, n_batch):
    tp, d = logf.shape
    rows = SCAN_CHUNKS * CHUNK
    n_chunks = tp // n_batch // rows
    assert n_chunks * rows * n_batch == tp
    col = lambda group: pl.BlockSpec((rows, d), lambda b, c: (b * n_chunks + c, group))
    blk = col(0)
    return pl.pallas_call(
        _scan_kernel,
        grid=(n_batch, n_chunks),
        in_specs=[col(ACT_Q), col(ACT_K), col(ACT_V), blk, col(ACT_OG), pl.BlockSpec((1, d), lambda b, c: (0, 0))],
        out_specs=blk,
        out_shape=jax.ShapeDtypeStruct((tp, d), BF16),
        scratch_shapes=[pltpu.VMEM((N_HEADS, HEAD_DIM, HEAD_DIM), F32)],
        compiler_params=_cparams(("arbitrary", "arbitrary")),
        name="hgrn2_scan",
    )(acts, acts, acts, logf, acts, g_onorm)


def _merge_route_kernel(x_ref, meta_ref, on_ref, ap_ref, sgb_ref, wrnn_ref, wo_ref, gffn_ref, wr_ref, br_ref,
                        h1_ref, u2_ref, info_ref, route_ref, cnt_ref, tri_ref, carry_ref, colcnt_ref, logits_ref, hres_ref,
                        *, tiles_per_batch, n_tiles):
    tm = on_ref.shape[0]
    i = pl.program_id(0)
    hres_ref[...] = _residual_tile(x_ref, meta_ref, jnp.minimum(i, n_tiles - 1) % tiles_per_batch == 0)

    @pl.when(i == 0)
    def _():
        r_i = lax.broadcasted_iota(jnp.int32, (tm, tm), 0)
        c_i = lax.broadcasted_iota(jnp.int32, (tm, tm), 1)
        tri_ref[...] = (c_i < r_i).astype(BF16)
        carry_ref[...] = jnp.zeros_like(carry_ref)
        colcnt_ref[...] = jnp.zeros_like(colcnt_ref)
        logits_ref[...] = jnp.zeros_like(logits_ref)

    routed = i > 0
    lane = lax.broadcasted_iota(jnp.int32, (tm, LANES), 1)
    cur = jnp.where(lane < N_EXPERTS, logits_ref[...], NEG_BIG)
    vals, idxs, sels = [], [], []

    def topk_round(cur):
        m = jnp.max(cur, axis=-1, keepdims=True)
        idx = jnp.min(jnp.where(cur == m, lane, LANES), axis=-1, keepdims=True)
        sel = lane == idx
        vals.append(m)
        idxs.append(idx)
        sels.append(sel)
        return jnp.where(sel, 2.0 * NEG_BIG, cur)

    w_hi, w_lo = _split_bf16(wr_ref[...])
    w_hi_lo = jnp.concatenate([w_hi, w_lo], axis=1)
    part = tm // MERGE_PARTS
    rows = [slice(p * part, (p + 1) * part) for p in range(MERGE_PARTS)]
    w_rnn = wrnn_ref[...].astype(BF16)
    w_out = wo_ref[...].astype(BF16)
    rs = [jnp.dot(on_ref[rw, :], w_rnn, preferred_element_type=F32) for rw in rows]
    cur = topk_round(cur)
    ys = [(ap_ref[rw, :].astype(F32) + sgb_ref[rw, :].astype(F32) * r).astype(BF16) for rw, r in zip(rows, rs)]
    h1s = [hres_ref[rw, :] + jnp.dot(y, w_out, preferred_element_type=F32) for rw, y in zip(rows, ys)]
    cur = topk_round(cur)
    u2s = []
    for rw, h1 in zip(rows, h1s):
        h1_ref[rw, :] = h1
        ms = jnp.mean(h1 * h1, axis=-1, keepdims=True)
        u2 = h1 * lax.rsqrt(ms + RMS_EPS) * gffn_ref[...]
        u2_ref[rw, :] = _pack_rows(u2)
        u2s.append(u2)
    cur = topk_round(cur)
    new_logits = []
    for u2 in u2s:
        u_hi, u_lo = _split_bf16(u2)
        both = jnp.dot(u_hi, w_hi_lo, preferred_element_type=F32)
        new_logits.append(both[:, :LANES] + (jnp.dot(u_lo, w_hi, preferred_element_type=F32) + both[:, LANES:])
                          + br_ref[...])
    cur = topk_round(cur)
    assert len(vals) == TOP_K
    exps = [jnp.exp(vk - vals[0]) for vk in vals]
    den = exps[0] + exps[1] + exps[2] + exps[3]
    onehot = jnp.logical_or(jnp.logical_or(sels[0], sels[1]), jnp.logical_or(sels[2], sels[3]))
    onehot = jnp.logical_and(onehot, routed)
    onehot_bf = onehot.astype(BF16)
    rank_all = jnp.dot(tri_ref[...], onehot_bf, preferred_element_type=F32) + carry_ref[...]
    carry_ref[...] += jnp.sum(onehot.astype(F32), axis=0, keepdims=True)
    colcnt_ref[...] += _tn_dot(onehot_bf, jnp.ones((tm, LANES), BF16))
    info = jnp.zeros((tm, LANES), F32)
    for kk in range(TOP_K):
        rank_k = jnp.sum(jnp.where(sels[kk], rank_all, 0.0), axis=-1, keepdims=True)
        info = jnp.where(lane == kk, exps[kk] / den, info)
        info = jnp.where(lane == TOP_K + kk, idxs[kk].astype(F32), info)
        info = jnp.where(lane == 2 * TOP_K + kk, rank_k, info)
    info_ref[...] = info
    route_ref[...] = info.T[0:ROUTE_ROWS, :]
    cnt_ref[...] = colcnt_ref[...]
    logits_ref[...] = jnp.concatenate(new_logits, axis=0)


def _merge_route(x2, meta, seq, on, ap, acts, w_rnn, w_o, g_ffn, w_router_pad, b_router_pad):
    tp, d = on.shape
    tm = ROW_TILE
    n_tiles = tp // tm
    tpb = (seq + CHUNK) // tm
    assert tpb * tm == seq + CHUNK
    this_tile = lambda i: (jnp.minimum(i, n_tiles - 1), 0)
    prev_tile = lambda i: (jnp.maximum(i - 1, 0), 0)
    row = pl.BlockSpec((tm, d), this_tile)
    vec = _const_spec((1, d))
    return pl.pallas_call(
        functools.partial(_merge_route_kernel, tiles_per_batch=tpb, n_tiles=n_tiles),
        grid=(n_tiles + 1,),
        in_specs=[_residual_spec(tm, d, seq, tpb, n_tiles), _const_spec(meta.shape), row, row,
                  pl.BlockSpec((tm, d), lambda i: (jnp.minimum(i, n_tiles - 1), ACT_SGB)),
                  _const_spec((d, d)), _const_spec((d, d)), vec, _const_spec((d, LANES)), _const_spec((1, LANES))],
        out_specs=[row, pl.BlockSpec((tm, PACK_WORDS), this_tile),
                   pl.BlockSpec((tm, LANES), prev_tile),
                   pl.BlockSpec((ROUTE_ROWS, tm), lambda i: (0, jnp.maximum(i - 1, 0))), _const_spec((LANES, LANES))],
        out_shape=[jax.ShapeDtypeStruct((tp, d), F32), jax.ShapeDtypeStruct((tp, PACK_WORDS), jnp.uint32),
                   jax.ShapeDtypeStruct((tp, LANES), F32), jax.ShapeDtypeStruct((ROUTE_ROWS, tp), F32),
                   jax.ShapeDtypeStruct((LANES, LANES), F32)],
        scratch_shapes=[pltpu.VMEM((tm, tm), BF16), pltpu.VMEM((1, LANES), F32),
                        pltpu.VMEM((LANES, LANES), F32), pltpu.VMEM((tm, LANES), F32), pltpu.VMEM((tm, d), F32)],
        compiler_params=_cparams(("arbitrary",)),
        name="merge_route",
    )(x2, meta, on, ap, acts, w_rnn, w_o, g_ffn, w_router_pad, b_router_pad)


def _slots_kernel(route_ref, cnt_ref, dest_ref, te_ref, pstart_ref):
    tl = route_ref.shape[1]
    n_tile_lanes = te_ref.shape[1]

    @pl.when(pl.program_id(0) == 0)
    def _():
        r_i = lax.broadcasted_iota(jnp.int32, (LANES, LANES), 0)
        c_i = lax.broadcasted_iota(jnp.int32, (LANES, LANES), 1)
        cnt = cnt_ref[...]
        tiles = jnp.floor((cnt + (EXPERT_TILE - 1)) / EXPERT_TILE)
        pend_col = jnp.dot((c_i <= r_i).astype(F32), tiles, preferred_element_type=F32,
                           precision=lax.Precision.HIGHEST)
        pstart_ref[...] = pend_col - tiles
        tile_id = lax.broadcasted_iota(jnp.int32, (LANES, n_tile_lanes), 1).astype(F32)
        exp_id = lax.broadcasted_iota(jnp.int32, (LANES, n_tile_lanes), 0)
        pend_wide = jnp.concatenate([pend_col] * (n_tile_lanes // LANES), axis=1)
        below = jnp.logical_and(pend_wide <= tile_id, exp_id < N_EXPERTS)
        te = jnp.minimum(jnp.sum(below.astype(F32), axis=0, keepdims=True), N_EXPERTS - 1.0)
        n_active = pend_col[N_EXPERTS - 1:N_EXPERTS, 0:1]
        owner = exp_id.astype(F32) == te
        cnt_wide = jnp.concatenate([cnt] * (n_tile_lanes // LANES), axis=1)
        first_tile = pend_wide - jnp.concatenate([tiles] * (n_tile_lanes // LANES), axis=1)
        left = jnp.sum(jnp.where(owner, cnt_wide - (tile_id - first_tile) * EXPERT_TILE, 0.0), axis=0, keepdims=True)
        valid = jnp.clip(left, 0.0, float(EXPERT_TILE))
        sub_t = lax.broadcasted_iota(jnp.int32, (SUBLANES, n_tile_lanes), 0)
        out = jnp.where(sub_t == 0, jnp.broadcast_to(te, (SUBLANES, n_tile_lanes)),
                        jnp.where(sub_t == 1, jnp.broadcast_to(n_active, (SUBLANES, n_tile_lanes)),
                                  jnp.broadcast_to(valid, (SUBLANES, n_tile_lanes))))
        te_ref[...] = out.astype(jnp.int32)

    route = route_ref[...]
    expert = lax.broadcasted_iota(jnp.int32, (N_EXPERTS, tl), 0)
    first_slot = jnp.concatenate([pstart_ref[0:N_EXPERTS, :]] * (tl // LANES), axis=1) * EXPERT_TILE
    rows = []
    for kk in range(TOP_K):
        idx_k = route[TOP_K + kk:TOP_K + kk + 1, :].astype(jnp.int32)
        start_k = jnp.sum(jnp.where(expert == idx_k, first_slot, 0.0), axis=0, keepdims=True)
        rows.append(start_k + route[2 * TOP_K + kk:2 * TOP_K + kk + 1, :])
    rows.append(jnp.zeros((SUBLANES - TOP_K, tl), F32))
    dest_ref[...] = jnp.concatenate(rows, axis=0).astype(jnp.int32)


def _slots(route, cnt, n_tiles_max):
    tp = route.shape[1]
    tl = tp // SLOTS_STEPS
    assert tl * SLOTS_STEPS == tp and tl % LANES == 0
    n_tile_lanes = -(-n_tiles_max // LANES) * LANES
    return pl.pallas_call(
        _slots_kernel,
        grid=(SLOTS_STEPS,),
        in_specs=[pl.BlockSpec((ROUTE_ROWS, tl), lambda i: (0, i)), _const_spec((LANES, LANES))],
        out_specs=[pl.BlockSpec((SUBLANES, tl), lambda i: (0, i)), _const_spec((8, n_tile_lanes))],
        out_shape=[jax.ShapeDtypeStruct((SUBLANES, tp), jnp.int32),
                   jax.ShapeDtypeStruct((8, n_tile_lanes), jnp.int32)],
        scratch_shapes=[pltpu.VMEM((LANES, LANES), F32)],
        compiler_params=_cparams(("arbitrary",)),
        name="slots",
    )(route, cnt)


def _sc_workers():
    sc = plsc.get_sparse_core_info()
    return sc.num_cores, sc.num_cores * sc.num_subcores


def _sc_chunk(tp, n_row_bufs):
    per_worker = tp // _sc_workers()[1]
    limit = min(SC_MAX_INDICES, SC_ROW_BUFFER_BYTES // (n_row_bufs * PACK_WORDS * 4))
    return max(c for c in range(SUBLANES, limit + 1, SUBLANES) if per_worker % c == 0)


def _sc_token_chunks(tp, chunk, body):
    n_cores, n_workers = _sc_workers()
    per_worker = tp // n_workers
    assert per_worker * n_workers == tp and per_worker % chunk == 0 and chunk % SUBLANES == 0
    base = (lax.axis_index("s") * n_cores + lax.axis_index("c")) * per_worker

    @pl.loop(0, per_worker // chunk)
    def _(ci):
        body(pl.multiple_of(base + ci * chunk, SUBLANES))


def _sc_kernel(out_type, chunk, n_row_bufs):
    return functools.partial(
        pl.kernel, mesh=plsc.VectorSubcoreMesh(core_axis_name="c", subcore_axis_name="s"), out_type=out_type,
        scratch_types=[pltpu.VMEM((TOP_K, chunk), jnp.int32), pltpu.VMEM((n_row_bufs, chunk, PACK_WORDS), jnp.uint32),
                       pltpu.SemaphoreType.DMA, pltpu.SemaphoreType.DMA])


def _wait_all(copies):
    for c in copies:
        c.wait()


def _dispatch(dest_flat, u2p, n_slots):
    tp = u2p.shape[0]
    ch = _sc_chunk(tp, 1)

    @_sc_kernel(jax.ShapeDtypeStruct((n_slots, PACK_WORDS), jnp.uint32), ch, 1)
    def scatter_rows(u2_hbm, dest_hbm, xs_hbm, idx_v, rows_v, sem_a, sem_b):
        def chunk(off):
            loads = [pltpu.async_copy(u2_hbm.at[pl.ds(off, ch)], rows_v.at[0], sem_a)]
            loads += [pltpu.async_copy(dest_hbm.at[pl.ds(kk * tp + off, ch)], idx_v.at[kk], sem_a)
                      for kk in range(TOP_K)]
            _wait_all(loads)
            _wait_all([pltpu.async_copy(rows_v.at[0], xs_hbm.at[idx_v.at[kk]], sem_b) for kk in range(TOP_K)])

        _sc_token_chunks(tp, ch, chunk)

    return scatter_rows(u2p, dest_flat)


def _gather_expert_rows(dest_flat, ys):
    tp = dest_flat.shape[0] // TOP_K
    ch = _sc_chunk(tp, TOP_K)

    @_sc_kernel(jax.ShapeDtypeStruct((TOP_K, tp, PACK_WORDS), jnp.uint32), ch, TOP_K)
    def gather_rows(ys_hbm, dest_hbm, out_hbm, idx_v, rows_v, sem_a, sem_b):
        def chunk(off):
            _wait_all([pltpu.async_copy(dest_hbm.at[pl.ds(kk * tp + off, ch)], idx_v.at[kk], sem_a)
                       for kk in range(TOP_K)])
            _wait_all([pltpu.async_copy(ys_hbm.at[idx_v.at[kk]], rows_v.at[kk], sem_b) for kk in range(TOP_K)])
            _wait_all([pltpu.async_copy(rows_v.at[kk], out_hbm.at[kk, pl.ds(off, ch)], sem_a)
                       for kk in range(TOP_K)])

        _sc_token_chunks(tp, ch, chunk)

    return gather_rows(ys, dest_flat)


def _expert_kernel(te_ref, nact_ref, valid_ref, x_ref, wgu_hbm, bgu_ref, wdn_hbm, bdn_ref, y_ref,
                   wgu_buf, wdn_buf, slot_ref, sem):
    i = pl.program_id(0)
    n_active = nact_ref[0]
    last_tile = pl.num_programs(0) - 1

    def fetch(expert, slot):
        return (pltpu.make_async_copy(wgu_hbm.at[expert], wgu_buf.at[slot], sem.at[slot, 0]),
                pltpu.make_async_copy(wdn_hbm.at[expert], wdn_buf.at[slot], sem.at[slot, 1]))

    @pl.when(i == 0)
    def _():
        slot_ref[0] = 1
        for copy in fetch(te_ref[0], 0):
            copy.start()

    @pl.when(i < n_active)
    def _():
        expert = te_ref[i]
        first_tile_of_expert = jnp.logical_or(i == 0, expert != te_ref[jnp.maximum(i - 1, 0)])

        @pl.when(first_tile_of_expert)
        def _():
            slot = 1 - slot_ref[0]
            slot_ref[0] = slot
            for copy in fetch(expert, slot):
                copy.wait()
            nxt = lax.while_loop(
                lambda j: jnp.logical_and(j < n_active, te_ref[jnp.minimum(j, last_tile)] == expert),
                lambda j: j + 1, i + 1)

            @pl.when(nxt < n_active)
            def _():
                for copy in fetch(te_ref[jnp.minimum(nxt, last_tile)], 1 - slot):
                    copy.start()

        slot = slot_ref[0]

        def mlp(rows):
            x = _unpack_rows(x_ref[0:rows, :]).astype(BF16)
            gu = jnp.dot(x, wgu_buf[slot].astype(BF16), preferred_element_type=F32) + bgu_ref[0]
            gate = jnp.minimum(gu[:, :D_FF], SWIGLU_LIMIT)
            up = jnp.clip(gu[:, D_FF:], -SWIGLU_LIMIT, SWIGLU_LIMIT)
            hdn = (up + 1.0) * (gate * _sigmoid(SWIGLU_ALPHA * gate))
            y = jnp.dot(hdn.astype(BF16), wdn_buf[slot].astype(BF16), preferred_element_type=F32) + bdn_ref[0]
            y_ref[0:rows, :] = _pack_rows(y)

        parts_used = jnp.clip((valid_ref[i] + EXPERT_PART - 1) // EXPERT_PART, 1, EXPERT_TILE // EXPERT_PART)
        for parts in range(1, EXPERT_TILE // EXPERT_PART + 1):
            @pl.when(parts_used == parts)
            def _(rows=parts * EXPERT_PART):
                mlp(rows)
                if rows < EXPERT_TILE:
                    y_ref[rows:, :] = jnp.zeros((EXPERT_TILE - rows, PACK_WORDS), y_ref.dtype)

    @pl.when(i >= n_active)
    def _():
        y_ref[...] = jnp.zeros_like(y_ref)


def _expert_mlp(te, nact, valid, xs, w_gu, b_gu, w_dn, b_dn):
    d = D_MODEL
    n_tiles = xs.shape[0] // EXPERT_TILE

    def tile_map(i, te_ref, nact_ref, valid_ref):
        return (jnp.minimum(i, nact_ref[0] - 1), 0)

    def exp_map(i, te_ref, nact_ref, valid_ref):
        return (te_ref[jnp.minimum(i, nact_ref[0] - 1)], 0, 0)

    grid_spec = pltpu.PrefetchScalarGridSpec(
        num_scalar_prefetch=3,
        grid=(n_tiles,),
        in_specs=[pl.BlockSpec((EXPERT_TILE, PACK_WORDS), tile_map),
                  pl.BlockSpec(memory_space=pl.ANY),
                  pl.BlockSpec((1, 1, 2 * D_FF), exp_map),
                  pl.BlockSpec(memory_space=pl.ANY),
                  pl.BlockSpec((1, 1, d), exp_map)],
        out_specs=pl.BlockSpec((EXPERT_TILE, PACK_WORDS), lambda i, te_ref, nact_ref, valid_ref: (i, 0)),
        scratch_shapes=[pltpu.VMEM((2, d, 2 * D_FF), F32), pltpu.VMEM((2, D_FF, d), F32),
                        pltpu.SMEM((1,), jnp.int32), pltpu.SemaphoreType.DMA((2, 2))],
    )
    return pl.pallas_call(
        _expert_kernel,
        grid_spec=grid_spec,
        out_shape=jax.ShapeDtypeStruct(xs.shape, jnp.uint32),
        compiler_params=_cparams(("arbitrary",)),
        name="expert_mlp",
    )(te, nact, valid, xs, w_gu, b_gu, w_dn, b_dn)


def _combine_kernel(gfin_ref, *refs):
    out_ref = refs[-1]
    for s in range(COMBINE_SUB):
        h1_ref, info_ref, yg_ref = refs[3 * s:3 * s + 3]
        info = info_ref[...]
        h2 = h1_ref[...]
        for kk in range(TOP_K):
            h2 = h2 + info[:, kk:kk + 1] * _unpack_rows(yg_ref[kk])
        ms = jnp.mean(h2 * h2, axis=-1, keepdims=True)
        out_ref[s * CHUNK:(s + 1) * CHUNK, :] = h2 * lax.rsqrt(ms + RMS_EPS) * gfin_ref[...]


def _combine(h1, info, g_final, yg, n_batch, seq):
    d = h1.shape[1]
    chunks = seq // CHUNK

    def padded(s):
        def index(j):
            c = j * COMBINE_SUB + s
            return (c // chunks) * (chunks + 1) + c % chunks + 1
        return index

    in_specs = [pl.BlockSpec((1, d), lambda j: (0, 0))]
    operands = [g_final]
    for s in range(COMBINE_SUB):
        chunk_of = padded(s)
        in_specs += [pl.BlockSpec((CHUNK, d), lambda j, f=chunk_of: (f(j), 0)),
                     pl.BlockSpec((CHUNK, LANES), lambda j, f=chunk_of: (f(j), 0)),
                     pl.BlockSpec((TOP_K, CHUNK, PACK_WORDS), lambda j, f=chunk_of: (0, f(j), 0))]
        operands += [h1, info, yg]
    return pl.pallas_call(
        _combine_kernel,
        grid=(n_batch * chunks // COMBINE_SUB,),
        in_specs=in_specs,
        out_specs=pl.BlockSpec((COMBINE_SUB * CHUNK, d), lambda j: (j, 0)),
        out_shape=jax.ShapeDtypeStruct((n_batch * seq, d), F32),
        compiler_params=_cparams(("parallel",)),
        name="combine",
    )(*operands)


def kernel(x, meta_tokens, lb_logits, g_mix, w_in, w_dw, b_dw, ln_g, ln_b, w_conv_out, b_conv_out,
           g_onorm, w_rnn_out, w_o, g_ffn, w_router, b_router, w_gate_up, b_gate_up, w_down, b_down,
           g_final):
    n_batch, seq, d = x.shape
    assert d == D_MODEL and w_in.shape[0] == 1, "single-layer block with D_MODEL features"
    assert seq % (CHUNK * COMBINE_SUB) == 0
    seq_pad = CHUNK + seq
    tp = n_batch * seq_pad
    assert tp % ROW_TILE == 0

    x2 = x.reshape(n_batch * seq, d)
    meta = meta_tokens.astype(x.dtype)
    vec = lambda a: a.reshape(1, -1).astype(F32)

    acts, logf = _in_proj(x2, meta, seq, vec(g_mix[0]), lb_logits.astype(F32), w_in[0])
    w_taps = w_dw[0].astype(F32).reshape(CONV_WIDTH, d // LANES, LANES).transpose(1, 0, 2)
    ap = _conv_branch(acts, w_taps, vec(b_dw[0]), vec(ln_g[0]), vec(ln_b[0]),
                      w_conv_out[0], vec(b_conv_out[0]))
    on = _hgrn2_scan(acts, logf, vec(g_onorm[0]), n_batch)
    w_router_pad = jnp.pad(w_router[0].astype(F32), ((0, 0), (0, LANES - N_EXPERTS)))
    b_router_pad = jnp.pad(vec(b_router[0]), ((0, 0), (0, LANES - N_EXPERTS)))
    h1, u2p, info, route, cnt = _merge_route(x2, meta, seq, on, ap, acts, w_rnn_out[0], w_o[0],
                                      vec(g_ffn[0]), w_router_pad, b_router_pad)

    n_tiles_max = -(-(tp * TOP_K + N_EXPERTS * (EXPERT_TILE - 1)) // EXPERT_TILE)
    dest, te = _slots(route, cnt, n_tiles_max)
    dest_flat = dest[:TOP_K].reshape(-1)
    xs = _dispatch(dest_flat, u2p, n_tiles_max * EXPERT_TILE)
    ys = _expert_mlp(te[0, :n_tiles_max], te[1, :1], te[2, :n_tiles_max], xs, w_gate_up[0],
                     b_gate_up[0].reshape(N_EXPERTS, 1, -1).astype(F32), w_down[0],
                     b_down[0].reshape(N_EXPERTS, 1, -1).astype(F32))
    yg = _gather_expert_rows(dest_flat, ys)
    out = _combine(h1, info, vec(g_final), yg, n_batch, seq)
    return out.reshape(n_batch, seq, d)
```

```python
import functools

import jax
import jax.numpy as jnp
from jax import lax
from jax.experimental import pallas as pl
from jax.experimental.pallas import tpu as pltpu
from jax.experimental.pallas import tpu_sc as plsc

F32 = jnp.float32
BF16 = jnp.bfloat16

D_MODEL = 1024
N_META = 16
CHUNK = 128
CHUNK_PAD = CHUNK - N_META
CONV_WIDTH = 31
HEAD_DIM = 128
N_HEADS = D_MODEL // HEAD_DIM
N_EXPERTS = 32
TOP_K = 4
D_FF = D_MODEL
SWIGLU_LIMIT = 7.0
SWIGLU_ALPHA = 1.702
RMS_EPS = 1e-6
LN_EPS = 1e-5

ACT_AGLU, ACT_Q, ACT_K, ACT_V, ACT_OG, ACT_SGA, ACT_SGB = range(7)
N_ACTS = 7

LANES = 128
SUBLANES = 8
SUB_BLOCK = 32
N_SUB = CHUNK // SUB_BLOCK
HALO = 32
ROW_TILE = 640
IN_TILE = 320
SCAN_CHUNKS = 5
MERGE_PARTS = 2
SLOTS_STEPS = 5
ROUTE_ROWS = 16
CONV_ROWS = 64
CONV_SUMS = 4
EXPERT_TILE = 512
EXPERT_PART = 128
SC_MAX_INDICES = 128
SC_ROW_BUFFER_BYTES = 384 * 1024
COMBINE_SUB = 8
NEG_BIG = -1e30
VMEM_LIMIT = 56 * 1024 * 1024


def _sigmoid(x):
    return 1.0 / (1.0 + jnp.exp(-x))


def _cparams(sem):
    return pltpu.CompilerParams(dimension_semantics=sem, vmem_limit_bytes=VMEM_LIMIT)


def _const_spec(shape):
    nd = len(shape)
    return pl.BlockSpec(shape, lambda *_: (0,) * nd)


PACK_WORDS = D_MODEL // 2
HIGH_HALF = 0xFFFF0000


def _pack_rows(x):
    lo = lax.bitcast_convert_type(x[:, :PACK_WORDS].astype(BF16).astype(F32), jnp.uint32)
    hi = lax.bitcast_convert_type(x[:, PACK_WORDS:].astype(BF16).astype(F32), jnp.uint32)
    return (lo >> 16) | (hi & jnp.uint32(HIGH_HALF))


def _unpack_rows(w):
    lo = lax.bitcast_convert_type(w << 16, F32)
    hi = lax.bitcast_convert_type(w & jnp.uint32(HIGH_HALF), F32)
    return jnp.concatenate([lo, hi], axis=1)


def _residual_tile(x_ref, meta_ref, is_first):
    x = x_ref[...]
    tm, d = x.shape
    prefix = jnp.concatenate([jnp.zeros((CHUNK_PAD, d), x.dtype), meta_ref[...]], axis=0)
    first = jnp.concatenate([prefix, x[:tm - CHUNK, :]], axis=0)
    return jnp.where(is_first, first, x)


def _residual_spec(tm, d, seq, tiles_per_batch, n_tiles):
    def start(i):
        i = jnp.minimum(i, n_tiles - 1)
        b, t = i // tiles_per_batch, i % tiles_per_batch
        return (pl.multiple_of(b * seq + jnp.maximum(t * tm - CHUNK, 0), SUBLANES), 0)
    return pl.BlockSpec((pl.Element(tm), pl.Element(d)), start)


def _in_proj_kernel(x_ref, meta_ref, g_ref, lbl_ref, w_ref, acts_ref, logf_ref, *, tiles_per_batch):
    d = D_MODEL

    def put(group, value):
        acts_ref[:, group * d:(group + 1) * d] = value.astype(acts_ref.dtype)

    h = _residual_tile(x_ref, meta_ref, pl.program_id(0) % tiles_per_batch == 0)
    ms = jnp.mean(h * h, axis=-1, keepdims=True)
    u = (h * lax.rsqrt(ms + RMS_EPS) * g_ref[...]).astype(BF16)

    def proj(j):
        return jnp.dot(u, w_ref[:, j * d:(j + 1) * d].astype(BF16), preferred_element_type=F32)

    put(ACT_AGLU, proj(0) * _sigmoid(proj(1)))
    zq = proj(2)
    put(ACT_Q, zq * _sigmoid(zq))
    lbl = lbl_ref[...]
    e = jnp.exp(lbl - jnp.max(lbl, axis=0, keepdims=True))
    lb = e[0:1, :] / jnp.sum(e, axis=0, keepdims=True)
    s = _sigmoid(proj(3))
    logf_ref[...] = jnp.log(lb + (1.0 - lb) * s)
    put(ACT_K, (1.0 - lb) * (1.0 - s))
    put(ACT_V, proj(4))
    zg = proj(5)
    put(ACT_OG, zg * _sigmoid(zg))
    put(ACT_SGA, _sigmoid(proj(6)))
    put(ACT_SGB, _sigmoid(proj(7)))


def _in_proj(x2, meta, seq, g_mix, lb_logits, w_in):
    d = x2.shape[1]
    n_batch = x2.shape[0] // seq
    tm = IN_TILE
    tpb = (seq + CHUNK) // tm
    tp = n_batch * (seq + CHUNK)
    assert tpb * tm == seq + CHUNK and tm > CHUNK
    row = pl.BlockSpec((tm, d), lambda i: (i, 0))
    return pl.pallas_call(
        functools.partial(_in_proj_kernel, tiles_per_batch=tpb),
        grid=(tp // tm,),
        in_specs=[_residual_spec(tm, d, seq, tpb, tp // tm), _const_spec(meta.shape), _const_spec((1, d)),
                  _const_spec(lb_logits.shape),
                  pl.BlockSpec(w_in.shape, lambda i: (0, 0), pipeline_mode=pl.Buffered(1))],
        out_specs=[pl.BlockSpec((tm, N_ACTS * d), lambda i: (i, 0)), row],
        out_shape=[jax.ShapeDtypeStruct((tp, N_ACTS * d), BF16), jax.ShapeDtypeStruct((tp, d), F32)],
        compiler_params=_cparams(("parallel",)),
        name="in_proj",
    )(x2, meta, g_mix, lb_logits, w_in)


def _conv_kernel(halo_ref, cur_ref, wdw_ref, bdw_ref, lng_ref, lnb_ref, wout_ref, bout_ref, sga_ref,
                 out_ref, win_ref, acc_ref, shift_ref):
    tm = cur_ref.shape[0]
    win_ref[0:HALO, :] = halo_ref[...].astype(F32)
    win_ref[HALO:, :] = cur_ref[...].astype(F32)
    first_tap = HALO - (CONV_WIDTH - 1)
    for c in range(D_MODEL // LANES):
        lanes = slice(c * LANES, (c + 1) * LANES)
        n_rows = tm + HALO - SUBLANES
        for s in range(1, SUBLANES):
            shift_ref[s - 1, 0:n_rows, :] = win_ref[pl.ds(s, n_rows), lanes]
        bias = jnp.broadcast_to(bdw_ref[:, lanes], (CONV_ROWS, LANES))
        groups = CONV_ROWS // SUBLANES

        def chunk(r, carry, c=c, lanes=lanes, bias=bias):
            r0 = pl.multiple_of(r * CONV_ROWS, CONV_ROWS)
            accs = [bias.reshape(groups, SUBLANES, LANES)] + [None] * (CONV_SUMS - 1)
            for j in range(CONV_WIDTH):
                off = first_tap + j
                rows = pl.ds(r0 + off - off % SUBLANES, CONV_ROWS)
                w_j = wdw_ref[c, pl.ds(j, SUBLANES, stride=0), :]
                window = shift_ref[off % SUBLANES - 1, rows, :] if off % SUBLANES else win_ref[rows, lanes]
                term = w_j[None] * window.reshape(groups, SUBLANES, LANES)
                accs[j % CONV_SUMS] = term if accs[j % CONV_SUMS] is None else accs[j % CONV_SUMS] + term
            while len(accs) > 1:
                accs = [a + b for a, b in zip(accs[0::2], accs[1::2])]
            acc_ref[pl.ds(r0, CONV_ROWS), lanes] = accs[0].reshape(CONV_ROWS, LANES)
            return carry

        lax.fori_loop(0, tm // CONV_ROWS, chunk, 0)
    a = acc_ref[...]
    mu = jnp.mean(a, axis=-1, keepdims=True)
    ac = a - mu
    var = jnp.mean(ac * ac, axis=-1, keepdims=True)
    y = ac * lax.rsqrt(var + LN_EPS) * lng_ref[...] + lnb_ref[...]
    y = y * _sigmoid(y)
    o = jnp.dot(y.astype(BF16), wout_ref[...].astype(BF16), preferred_element_type=F32) + bout_ref[...]
    out_ref[...] = (sga_ref[...].astype(F32) * o).astype(out_ref.dtype)


def _conv_branch(acts, w_dw, b_dw, ln_g, ln_b, w_out, b_out):
    tp, d = acts.shape[0], D_MODEL
    tm = ROW_TILE
    per = tm // HALO
    row = pl.BlockSpec((tm, d), lambda i: (i, 0))
    glu = pl.BlockSpec((tm, d), lambda i: (i, ACT_AGLU))
    gate = pl.BlockSpec((tm, d), lambda i: (i, ACT_SGA))
    halo = pl.BlockSpec((HALO, d), lambda i: (jnp.maximum(i * per - 1, 0), ACT_AGLU))
    vec = _const_spec((1, d))
    return pl.pallas_call(
        _conv_kernel,
        grid=(tp // tm,),
        in_specs=[halo, glu, _const_spec(w_dw.shape), vec, vec, vec, _const_spec((d, d)), vec, gate],
        out_specs=row,
        out_shape=jax.ShapeDtypeStruct((tp, d), BF16),
        scratch_shapes=[pltpu.VMEM((tm + HALO, d), F32), pltpu.VMEM((tm, d), F32),
                        pltpu.VMEM((SUBLANES - 1, tm + HALO - SUBLANES, LANES), F32)],
        compiler_params=_cparams(("parallel",)),
        name="conv_branch",
    )(acts, acts, w_dw, b_dw, ln_g, ln_b, w_out, b_out, acts)


def _split_bf16(x):
    hi = x.astype(BF16)
    return hi, (x - hi.astype(F32)).astype(BF16)


def _nt_dot(a, b):
    return lax.dot_general(a, b, (((1,), (1,)), ((), ())), preferred_element_type=F32)


def _tn_dot(a, b):
    return lax.dot_general(a, b, (((0,), (0,)), ((), ())), preferred_element_type=F32)


def _scan_kernel(q_ref, k_ref, v_ref, lf_ref, og_ref, gon_ref, o_ref, st_ref):
    @pl.when(pl.program_id(1) == 0)
    def _():
        st_ref[...] = jnp.zeros_like(st_ref)

    c = CHUNK
    row = lax.broadcasted_iota(jnp.int32, (c, c), 0)
    col = lax.broadcasted_iota(jnp.int32, (c, c), 1)
    causal = col <= row
    diag_mask = jnp.logical_and(causal, row // SUB_BLOCK == col // SUB_BLOCK)
    tri = causal.astype(BF16)

    def bcast_rows(rows):
        return jnp.concatenate([jnp.broadcast_to(r, (SUB_BLOCK, HEAD_DIM)) for r in rows], axis=0)

    heads = [slice(h * HEAD_DIM, (h + 1) * HEAD_DIM) for h in range(N_HEADS)]
    chunks = [slice(ci * c, (ci + 1) * c) for ci in range(SCAN_CHUNKS)]

    bcums = []
    for cs in chunks:
        lf = lf_ref[cs, :]
        lf_hi = lf.astype(BF16)
        lf_mid, lf_lo = _split_bf16(lf - lf_hi.astype(F32))
        bcums.append(jnp.dot(tri, lf_hi, preferred_element_type=F32)
                     + (jnp.dot(tri, lf_mid, preferred_element_type=F32)
                        + jnp.dot(tri, lf_lo, preferred_element_type=F32)))
    operands = {}
    for ci, cs in enumerate(chunks):
        for h, hs in enumerate(heads):
            b = bcums[ci][:, hs]
            q = q_ref[cs, hs].astype(F32)
            k = k_ref[cs, hs].astype(F32)
            ends = [b[i * SUB_BLOCK + SUB_BLOCK - 1:i * SUB_BLOCK + SUB_BLOCK, :] for i in range(N_SUB)]
            mids = [b[i * SUB_BLOCK + SUB_BLOCK // 2 - 1:i * SUB_BLOCK + SUB_BLOCK // 2, :] for i in range(N_SUB)]
            mid_full = bcast_rows(mids)
            b_last = ends[-1]
            zero_row = jnp.zeros_like(b_last)
            qm = q * jnp.exp(b - mid_full)
            km = k * jnp.exp(mid_full - b)
            q_parts, k_parts = [], []
            for j in range(N_SUB - 1):
                q_rows = [jnp.exp(mids[i] - ends[j]) if i > j else zero_row for i in range(N_SUB)]
                k_rows = [jnp.exp(ends[j] - mids[j]) if i == j else zero_row for i in range(N_SUB)]
                q_parts.append((qm * bcast_rows(q_rows)).astype(BF16))
                k_parts.append((km * bcast_rows(k_rows)).astype(BF16))
            q_in = (qm * bcast_rows([jnp.exp(m) for m in mids])).astype(BF16)
            k_out = (km * bcast_rows([jnp.exp(b_last - m) for m in mids])).astype(BF16)
            operands[ci, h] = (qm.astype(BF16), km.astype(BF16), jnp.concatenate(q_parts, axis=1),
                               jnp.concatenate(k_parts, axis=1), q_in, k_out, jnp.exp(b_last))
    products = {}
    for ci, cs in enumerate(chunks):
        for h, hs in enumerate(heads):
            qm, km, q_cat, k_cat, _, k_out, _ = operands[ci, h]
            products[ci, h] = (_nt_dot(qm, km), _nt_dot(q_cat, k_cat), _tn_dot(v_ref[cs, hs], k_out))
    within = {}
    for ci, cs in enumerate(chunks):
        for h, hs in enumerate(heads):
            same_block, earlier_blocks, _ = products[ci, h]
            scores = jnp.where(diag_mask, same_block, 0.0) + earlier_blocks
            within[ci, h] = jnp.dot(scores.astype(BF16), v_ref[cs, hs], preferred_element_type=F32)
    states = [st_ref[h] for h in range(N_HEADS)]
    outs = {}
    for ci in range(SCAN_CHUNKS):
        for h in range(N_HEADS):
            outs[ci, h] = within[ci, h] + _nt_dot(operands[ci, h][4], states[h].astype(BF16))
            states[h] = states[h] * operands[ci, h][6] + products[ci, h][2]
    for h in range(N_HEADS):
        st_ref[h] = states[h]
    for ci, cs in enumerate(chunks):
        for h, hs in enumerate(heads):
            o = outs[ci, h]
            ms = jnp.mean(o * o, axis=-1, keepdims=True)
            on = o * lax.rsqrt(ms + RMS_EPS) * gon_ref[:, hs]
            o_ref[cs, hs] = (on * og_ref[cs, hs].astype(F32)).astype(o_ref.dtype)


def _hgrn2_scan(acts, logf, g_onorm, n_batch):
    tp, d = logf.shape
    rows = SCAN_CHUNKS * CHUNK
    n_chunks = tp // n_batch // rows
    assert n_chunks * rows * n_batch == tp
    col = lambda group: pl.BlockSpec((rows, d), lambda b, c: (b * n_chunks + c, group))
    blk = col(0)
    return pl.pallas_call(
        _scan_kernel,
        grid=(n_batch, n_chunks),
        in_specs=[col(ACT_Q), col(ACT_K), col(ACT_V), blk, col(ACT_OG), pl.BlockSpec((1, d), lambda b, c: (0, 0))],
        out_specs=blk,
        out_shape=jax.ShapeDtypeStruct((tp, d), BF16),
        scratch_shapes=[pltpu.VMEM((N_HEADS, HEAD_DIM, HEAD_DIM), F32)],
        compiler_params=_cparams(("arbitrary", "arbitrary")),
        name="hgrn2_scan",
    )(acts, acts, acts, logf, acts, g_onorm)


def _merge_route_kernel(x_ref, meta_ref, on_ref, ap_ref, sgb_ref, wrnn_ref, wo_ref, gffn_ref, wr_ref, br_ref,
                        h1_ref, u2_ref, info_ref, route_ref, cnt_ref, tri_ref, carry_ref, colcnt_ref, logits_ref, hres_ref,
                        *, tiles_per_batch, n_tiles):
    tm = on_ref.shape[0]
    i = pl.program_id(0)
    hres_ref[...] = _residual_tile(x_ref, meta_ref, jnp.minimum(i, n_tiles - 1) % tiles_per_batch == 0)

    @pl.when(i == 0)
    def _():
        r_i = lax.broadcasted_iota(jnp.int32, (tm, tm), 0)
        c_i = lax.broadcasted_iota(jnp.int32, (tm, tm), 1)
        tri_ref[...] = (c_i < r_i).astype(BF16)
        carry_ref[...] = jnp.zeros_like(carry_ref)
        colcnt_ref[...] = jnp.zeros_like(colcnt_ref)
        logits_ref[...] = jnp.zeros_like(logits_ref)

    routed = i > 0
    lane = lax.broadcasted_iota(jnp.int32, (tm, LANES), 1)
    cur = jnp.where(lane < N_EXPERTS, logits_ref[...], NEG_BIG)
    vals, idxs, sels = [], [], []

    def topk_round(cur):
        m = jnp.max(cur, axis=-1, keepdims=True)
        idx = jnp.min(jnp.where(cur == m, lane, LANES), axis=-1, keepdims=True)
        sel = lane == idx
        vals.append(m)
        idxs.append(idx)
        sels.append(sel)
        return jnp.where(sel, 2.0 * NEG_BIG, cur)

    w_hi, w_lo = _split_bf16(wr_ref[...])
    w_hi_lo = jnp.concatenate([w_hi, w_lo], axis=1)
    part = tm // MERGE_PARTS
    rows = [slice(p * part, (p + 1) * part) for p in range(MERGE_PARTS)]
    w_rnn = wrnn_ref[...].astype(BF16)
    w_out = wo_ref[...].astype(BF16)
    rs = [jnp.dot(on_ref[rw, :], w_rnn, preferred_element_type=F32) for rw in rows]
    cur = topk_round(cur)
    ys = [(ap_ref[rw, :].astype(F32) + sgb_ref[rw, :].astype(F32) * r).astype(BF16) for rw, r in zip(rows, rs)]
    h1s = [hres_ref[rw, :] + jnp.dot(y, w_out, preferred_element_type=F32) for rw, y in zip(rows, ys)]
    cur = topk_round(cur)
    u2s = []
    for rw, h1 in zip(rows, h1s):
        h1_ref[rw, :] = h1
        ms = jnp.mean(h1 * h1, axis=-1, keepdims=True)
        u2 = h1 * lax.rsqrt(ms + RMS_EPS) * gffn_ref[...]
        u2_ref[rw, :] = _pack_rows(u2)
        u2s.append(u2)
    cur = topk_round(cur)
    new_logits = []
    for u2 in u2s:
        u_hi, u_lo = _split_bf16(u2)
        both = jnp.dot(u_hi, w_hi_lo, preferred_element_type=F32)
        new_logits.append(both[:, :LANES] + (jnp.dot(u_lo, w_hi, preferred_element_type=F32) + both[:, LANES:])
                          + br_ref[...])
    cur = topk_round(cur)
    assert len(vals) == TOP_K
    exps = [jnp.exp(vk - vals[0]) for vk in vals]
    den = exps[0] + exps[1] + exps[2] + exps[3]
    onehot = jnp.logical_or(jnp.logical_or(sels[0], sels[1]), jnp.logical_or(sels[2], sels[3]))
    onehot = jnp.logical_and(onehot, routed)
    onehot_bf = onehot.astype(BF16)
    rank_all = jnp.dot(tri_ref[...], onehot_bf, preferred_element_type=F32) + carry_ref[...]
    carry_ref[...] += jnp.sum(onehot.astype(F32), axis=0, keepdims=True)
    colcnt_ref[...] += _tn_dot(onehot_bf, jnp.ones((tm, LANES), BF16))
    info = jnp.zeros((tm, LANES), F32)
    for kk in range(TOP_K):
        rank_k = jnp.sum(jnp.where(sels[kk], rank_all, 0.0), axis=-1, keepdims=True)
        info = jnp.where(lane == kk, exps[kk] / den, info)
        info = jnp.where(lane == TOP_K + kk, idxs[kk].astype(F32), info)
        info = jnp.where(lane == 2 * TOP_K + kk, rank_k, info)
    info_ref[...] = info
    route_ref[...] = info.T[0:ROUTE_ROWS, :]
    cnt_ref[...] = colcnt_ref[...]
    logits_ref[...] = jnp.concatenate(new_logits, axis=0)


def _merge_route(x2, meta, seq, on, ap, acts, w_rnn, w_o, g_ffn, w_router_pad, b_router_pad):
    tp, d = on.shape
    tm = ROW_TILE
    n_tiles = tp // tm
    tpb = (seq + CHUNK) // tm
    assert tpb * tm == seq + CHUNK
    this_tile = lambda i: (jnp.minimum(i, n_tiles - 1), 0)
    prev_tile = lambda i: (jnp.maximum(i - 1, 0), 0)
    row = pl.BlockSpec((tm, d), this_tile)
    vec = _const_spec((1, d))
    return pl.pallas_call(
        functools.partial(_merge_route_kernel, tiles_per_batch=tpb, n_tiles=n_tiles),
        grid=(n_tiles + 1,),
        in_specs=[_residual_spec(tm, d, seq, tpb, n_tiles), _const_spec(meta.shape), row, row,
                  pl.BlockSpec((tm, d), lambda i: (jnp.minimum(i, n_tiles - 1), ACT_SGB)),
                  _const_spec((d, d)), _const_spec((d, d)), vec, _const_spec((d, LANES)), _const_spec((1, LANES))],
        out_specs=[row, pl.BlockSpec((tm, PACK_WORDS), this_tile),
                   pl.BlockSpec((tm, LANES), prev_tile),
                   pl.BlockSpec((ROUTE_ROWS, tm), lambda i: (0, jnp.maximum(i - 1, 0))), _const_spec((LANES, LANES))],
        out_shape=[jax.ShapeDtypeStruct((tp, d), F32), jax.ShapeDtypeStruct((tp, PACK_WORDS), jnp.uint32),
                   jax.ShapeDtypeStruct((tp, LANES), F32), jax.ShapeDtypeStruct((ROUTE_ROWS, tp), F32),
                   jax.ShapeDtypeStruct((LANES, LANES), F32)],
        scratch_shapes=[pltpu.VMEM((tm, tm), BF16), pltpu.VMEM((1, LANES), F32),
                        pltpu.VMEM((LANES, LANES), F32), pltpu.VMEM((tm, LANES), F32), pltpu.VMEM((tm, d), F32)],
        compiler_params=_cparams(("arbitrary",)),
        name="merge_route",
    )(x2, meta, on, ap, acts, w_rnn, w_o, g_ffn, w_router_pad, b_router_pad)


def _slots_kernel(route_ref, cnt_ref, dest_ref, te_ref, pstart_ref):
    tl = route_ref.shape[1]
    n_tile_lanes = te_ref.shape[1]

    @pl.when(pl.program_id(0) == 0)
    def _():
        r_i = lax.broadcasted_iota(jnp.int32, (LANES, LANES), 0)
        c_i = lax.broadcasted_iota(jnp.int32, (LANES, LANES), 1)
        cnt = cnt_ref[...]
        tiles = jnp.floor((cnt + (EXPERT_TILE - 1)) / EXPERT_TILE)
        pend_col = jnp.dot((c_i <= r_i).astype(F32), tiles, preferred_element_type=F32,
                           precision=lax.Precision.HIGHEST)
        pstart_ref[...] = pend_col - tiles
        tile_id = lax.broadcasted_iota(jnp.int32, (LANES, n_tile_lanes), 1).astype(F32)
        exp_id = lax.broadcasted_iota(jnp.int32, (LANES, n_tile_lanes), 0)
        pend_wide = jnp.concatenate([pend_col] * (n_tile_lanes // LANES), axis=1)
        below = jnp.logical_and(pend_wide <= tile_id, exp_id < N_EXPERTS)
        te = jnp.minimum(jnp.sum(below.astype(F32), axis=0, keepdims=True), N_EXPERTS - 1.0)
        n_active = pend_col[N_EXPERTS - 1:N_EXPERTS, 0:1]
        owner = exp_id.astype(F32) == te
        cnt_wide = jnp.concatenate([cnt] * (n_tile_lanes // LANES), axis=1)
        first_tile = pend_wide - jnp.concatenate([tiles] * (n_tile_lanes // LANES), axis=1)
        left = jnp.sum(jnp.where(owner, cnt_wide - (tile_id - first_tile) * EXPERT_TILE, 0.0), axis=0, keepdims=True)
        valid = jnp.clip(left, 0.0, float(EXPERT_TILE))
        sub_t = lax.broadcasted_iota(jnp.int32, (SUBLANES, n_tile_lanes), 0)
        out = jnp.where(sub_t == 0, jnp.broadcast_to(te, (SUBLANES, n_tile_lanes)),
                        jnp.where(sub_t == 1, jnp.broadcast_to(n_active, (SUBLANES, n_tile_lanes)),
                                  jnp.broadcast_to(valid, (SUBLANES, n_tile_lanes))))
        te_ref[...] = out.astype(jnp.int32)

    route = route_ref[...]
    expert = lax.broadcasted_iota(jnp.int32, (N_EXPERTS, tl), 0)
    first_slot = jnp.concatenate([pstart_ref[0:N_EXPERTS, :]] * (tl // LANES), axis=1) * EXPERT_TILE
    rows = []
    for kk in range(TOP_K):
        idx_k = route[TOP_K + kk:TOP_K + kk + 1, :].astype(jnp.int32)
        start_k = jnp.sum(jnp.where(expert == idx_k, first_slot, 0.0), axis=0, keepdims=True)
        rows.append(start_k + route[2 * TOP_K + kk:2 * TOP_K + kk + 1, :])
    rows.append(jnp.zeros((SUBLANES - TOP_K, tl), F32))
    dest_ref[...] = jnp.concatenate(rows, axis=0).astype(jnp.int32)


def _slots(route, cnt, n_tiles_max):
    tp = route.shape[1]
    tl = tp // SLOTS_STEPS
    assert tl * SLOTS_STEPS == tp and tl % LANES == 0
    n_tile_lanes = -(-n_tiles_max // LANES) * LANES
    return pl.pallas_call(
        _slots_kernel,
        grid=(SLOTS_STEPS,),
        in_specs=[pl.BlockSpec((ROUTE_ROWS, tl), lambda i: (0, i)), _const_spec((LANES, LANES))],
        out_specs=[pl.BlockSpec((SUBLANES, tl), lambda i: (0, i)), _const_spec((8, n_tile_lanes))],
        out_shape=[jax.ShapeDtypeStruct((SUBLANES, tp), jnp.int32),
                   jax.ShapeDtypeStruct((8, n_tile_lanes), jnp.int32)],
        scratch_shapes=[pltpu.VMEM((LANES, LANES), F32)],
        compiler_params=_cparams(("arbitrary",)),
        name="slots",
    )(route, cnt)


def _sc_workers():
    sc = plsc.get_sparse_core_info()
    return sc.num_cores, sc.num_cores * sc.num_subcores


def _sc_chunk(tp, n_row_bufs):
    per_worker = tp // _sc_workers()[1]
    limit = min(SC_MAX_INDICES, SC_ROW_BUFFER_BYTES // (n_row_bufs * PACK_WORDS * 4))
    return max(c for c in range(SUBLANES, limit + 1, SUBLANES) if per_worker % c == 0)


def _sc_token_chunks(tp, chunk, body):
    n_cores, n_workers = _sc_workers()
    per_worker = tp // n_workers
    assert per_worker * n_workers == tp and per_worker % chunk == 0 and chunk % SUBLANES == 0
    base = (lax.axis_index("s") * n_cores + lax.axis_index("c")) * per_worker

    @pl.loop(0, per_worker // chunk)
    def _(ci):
        body(pl.multiple_of(base + ci * chunk, SUBLANES))


def _sc_kernel(out_type, chunk, n_row_bufs):
    return functools.partial(
        pl.kernel, mesh=plsc.VectorSubcoreMesh(core_axis_name="c", subcore_axis_name="s"), out_type=out_type,
        scratch_types=[pltpu.VMEM((TOP_K, chunk), jnp.int32), pltpu.VMEM((n_row_bufs, chunk, PACK_WORDS), jnp.uint32),
                       pltpu.SemaphoreType.DMA, pltpu.SemaphoreType.DMA])


def _wait_all(copies):
    for c in copies:
        c.wait()


def _dispatch(dest_flat, u2p, n_slots):
    tp = u2p.shape[0]
    ch = _sc_chunk(tp, 1)

    @_sc_kernel(jax.ShapeDtypeStruct((n_slots, PACK_WORDS), jnp.uint32), ch, 1)
    def scatter_rows(u2_hbm, dest_hbm, xs_hbm, idx_v, rows_v, sem_a, sem_b):
        def chunk(off):
            loads = [pltpu.async_copy(u2_hbm.at[pl.ds(off, ch)], rows_v.at[0], sem_a)]
            loads += [pltpu.async_copy(dest_hbm.at[pl.ds(kk * tp + off, ch)], idx_v.at[kk], sem_a)
                      for kk in range(TOP_K)]
            _wait_all(loads)
            _wait_all([pltpu.async_copy(rows_v.at[0], xs_hbm.at[idx_v.at[kk]], sem_b) for kk in range(TOP_K)])

        _sc_token_chunks(tp, ch, chunk)

    return scatter_rows(u2p, dest_flat)


def _gather_expert_rows(dest_flat, ys):
    tp = dest_flat.shape[0] // TOP_K
    ch = _sc_chunk(tp, TOP_K)

    @_sc_kernel(jax.ShapeDtypeStruct((TOP_K, tp, PACK_WORDS), jnp.uint32), ch, TOP_K)
    def gather_rows(ys_hbm, dest_hbm, out_hbm, idx_v, rows_v, sem_a, sem_b):
        def chunk(off):
            _wait_all([pltpu.async_copy(dest_hbm.at[pl.ds(kk * tp + off, ch)], idx_v.at[kk], sem_a)
                       for kk in range(TOP_K)])
            _wait_all([pltpu.async_copy(ys_hbm.at[idx_v.at[kk]], rows_v.at[kk], sem_b) for kk in range(TOP_K)])
            _wait_all([pltpu.async_copy(rows_v.at[kk], out_hbm.at[kk, pl.ds(off, ch)], sem_a)
                       for kk in range(TOP_K)])

        _sc_token_chunks(tp, ch, chunk)

    return gather_rows(ys, dest_flat)


def _expert_kernel(te_ref, nact_ref, valid_ref, x_ref, wgu_hbm, bgu_ref, wdn_hbm, bdn_ref, y_ref,
                   wgu_buf, wdn_buf, slot_ref, sem):
    i = pl.program_id(0)
    n_active = nact_ref[0]
    last_tile = pl.num_programs(0) - 1

    def fetch(expert, slot):
        return (pltpu.make_async_copy(wgu_hbm.at[expert], wgu_buf.at[slot], sem.at[slot, 0]),
                pltpu.make_async_copy(wdn_hbm.at[expert], wdn_buf.at[slot], sem.at[slot, 1]))

    @pl.when(i == 0)
    def _():
        slot_ref[0] = 1
        for copy in fetch(te_ref[0], 0):
            copy.start()

    @pl.when(i < n_active)
    def _():
        expert = te_ref[i]
        first_tile_of_expert = jnp.logical_or(i == 0, expert != te_ref[jnp.maximum(i - 1, 0)])

        @pl.when(first_tile_of_expert)
        def _():
            slot = 1 - slot_ref[0]
            slot_ref[0] = slot
            for copy in fetch(expert, slot):
                copy.wait()
            nxt = lax.while_loop(
                lambda j: jnp.logical_and(j < n_active, te_ref[jnp.minimum(j, last_tile)] == expert),
                lambda j: j + 1, i + 1)

            @pl.when(nxt < n_active)
            def _():
                for copy in fetch(te_ref[jnp.minimum(nxt, last_tile)], 1 - slot):
                    copy.start()

        slot = slot_ref[0]

        def mlp(rows):
            x = _unpack_rows(x_ref[0:rows, :]).astype(BF16)
            gu = jnp.dot(x, wgu_buf[slot].astype(BF16), preferred_element_type=F32) + bgu_ref[0]
            gate = jnp.minimum(gu[:, :D_FF], SWIGLU_LIMIT)
            up = jnp.clip(gu[:, D_FF:], -SWIGLU_LIMIT, SWIGLU_LIMIT)
            hdn = (up + 1.0) * (gate * _sigmoid(SWIGLU_ALPHA * gate))
            y = jnp.dot(hdn.astype(BF16), wdn_buf[slot].astype(BF16), preferred_element_type=F32) + bdn_ref[0]
            y_ref[0:rows, :] = _pack_rows(y)

        parts_used = jnp.clip((valid_ref[i] + EXPERT_PART - 1) // EXPERT_PART, 1, EXPERT_TILE // EXPERT_PART)
        for parts in range(1, EXPERT_TILE // EXPERT_PART + 1):
            @pl.when(parts_used == parts)
            def _(rows=parts * EXPERT_PART):
                mlp(rows)
                if rows < EXPERT_TILE:
                    y_ref[rows:, :] = jnp.zeros((EXPERT_TILE - rows, PACK_WORDS), y_ref.dtype)

    @pl.when(i >= n_active)
    def _():
        y_ref[...] = jnp.zeros_like(y_ref)


def _expert_mlp(te, nact, valid, xs, w_gu, b_gu, w_dn, b_dn):
    d = D_MODEL
    n_tiles = xs.shape[0] // EXPERT_TILE

    def tile_map(i, te_ref, nact_ref, valid_ref):
        return (jnp.minimum(i, nact_ref[0] - 1), 0)

    def exp_map(i, te_ref, nact_ref, valid_ref):
        return (te_ref[jnp.minimum(i, nact_ref[0] - 1)], 0, 0)

    grid_spec = pltpu.PrefetchScalarGridSpec(
        num_scalar_prefetch=3,
        grid=(n_tiles,),
        in_specs=[pl.BlockSpec((EXPERT_TILE, PACK_WORDS), tile_map),
                  pl.BlockSpec(memory_space=pl.ANY),
                  pl.BlockSpec((1, 1, 2 * D_FF), exp_map),
                  pl.BlockSpec(memory_space=pl.ANY),
                  pl.BlockSpec((1, 1, d), exp_map)],
        out_specs=pl.BlockSpec((EXPERT_TILE, PACK_WORDS), lambda i, te_ref, nact_ref, valid_ref: (i, 0)),
        scratch_shapes=[pltpu.VMEM((2, d, 2 * D_FF), F32), pltpu.VMEM((2, D_FF, d), F32),
                        pltpu.SMEM((1,), jnp.int32), pltpu.SemaphoreType.DMA((2, 2))],
    )
    return pl.pallas_call(
        _expert_kernel,
        grid_spec=grid_spec,
        out_shape=jax.ShapeDtypeStruct(xs.shape, jnp.uint32),
        compiler_params=_cparams(("arbitrary",)),
        name="expert_mlp",
    )(te, nact, valid, xs, w_gu, b_gu, w_dn, b_dn)


def _combine_kernel(gfin_ref, *refs):
    out_ref = refs[-1]
    for s in range(COMBINE_SUB):
        h1_ref, info_ref, yg_ref = refs[3 * s:3 * s + 3]
        info = info_ref[...]
        h2 = h1_ref[...]
        for kk in range(TOP_K):
            h2 = h2 + info[:, kk:kk + 1] * _unpack_rows(yg_ref[kk])
        ms = jnp.mean(h2 * h2, axis=-1, keepdims=True)
        out_ref[s * CHUNK:(s + 1) * CHUNK, :] = h2 * lax.rsqrt(ms + RMS_EPS) * gfin_ref[...]


def _combine(h1, info, g_final, yg, n_batch, seq):
    d = h1.shape[1]
    chunks = seq // CHUNK

    def padded(s):
        def index(j):
            c = j * COMBINE_SUB + s
            return (c // chunks) * (chunks + 1) + c % chunks + 1
        return index

    in_specs = [pl.BlockSpec((1, d), lambda j: (0, 0))]
    operands = [g_final]
    for s in range(COMBINE_SUB):
        chunk_of = padded(s)
        in_specs += [pl.BlockSpec((CHUNK, d), lambda j, f=chunk_of: (f(j), 0)),
                     pl.BlockSpec((CHUNK, LANES), lambda j, f=chunk_of: (f(j), 0)),
                     pl.BlockSpec((TOP_K, CHUNK, PACK_WORDS), lambda j, f=chunk_of: (0, f(j), 0))]
        operands += [h1, info, yg]
    return pl.pallas_call(
        _combine_kernel,
        grid=(n_batch * chunks // COMBINE_SUB,),
        in_specs=in_specs,
        out_specs=pl.BlockSpec((COMBINE_SUB * CHUNK, d), lambda j: (j, 0)),
        out_shape=jax.ShapeDtypeStruct((n_batch * seq, d), F32),
        compiler_params=_cparams(("parallel",)),
        name="combine",
    )(*operands)


def kernel(x, meta_tokens, lb_logits, g_mix, w_in, w_dw, b_dw, ln_g, ln_b, w_conv_out, b_conv_out,
           g_onorm, w_rnn_out, w_o, g_ffn, w_router, b_router, w_gate_up, b_gate_up, w_down, b_down,
           g_final):
    n_batch, seq, d = x.shape
    assert d == D_MODEL and w_in.shape[0] == 1, "single-layer block with D_MODEL features"
    assert seq % (CHUNK * COMBINE_SUB) == 0
    seq_pad = CHUNK + seq
    tp = n_batch * seq_pad
    assert tp % ROW_TILE == 0

    x2 = x.reshape(n_batch * seq, d)
    meta = meta_tokens.astype(x.dtype)
    vec = lambda a: a.reshape(1, -1).astype(F32)

    acts, logf = _in_proj(x2, meta, seq, vec(g_mix[0]), lb_logits.astype(F32), w_in[0])
    w_taps = w_dw[0].astype(F32).reshape(CONV_WIDTH, d // LANES, LANES).transpose(1, 0, 2)
    ap = _conv_branch(acts, w_taps, vec(b_dw[0]), vec(ln_g[0]), vec(ln_b[0]),
                      w_conv_out[0], vec(b_conv_out[0]))
    on = _hgrn2_scan(acts, logf, vec(g_onorm[0]), n_batch)
    w_router_pad = jnp.pad(w_router[0].astype(F32), ((0, 0), (0, LANES - N_EXPERTS)))
    b_router_pad = jnp.pad(vec(b_router[0]), ((0, 0), (0, LANES - N_EXPERTS)))
    h1, u2p, info, route, cnt = _merge_route(x2, meta, seq, on, ap, acts, w_rnn_out[0], w_o[0],
                                      vec(g_ffn[0]), w_router_pad, b_router_pad)

    n_tiles_max = -(-(tp * TOP_K + N_EXPERTS * (EXPERT_TILE - 1)) // EXPERT_TILE)
    dest, te = _slots(route, cnt, n_tiles_max)
    dest_flat = dest[:TOP_K].reshape(-1)
    xs = _dispatch(dest_flat, u2p, n_tiles_max * EXPERT_TILE)
    ys = _expert_mlp(te[0, :n_tiles_max], te[1, :1], te[2, :n_tiles_max], xs, w_gate_up[0],
                     b_gate_up[0].reshape(N_EXPERTS, 1, -1).astype(F32), w_down[0],
                     b_down[0].reshape(N_EXPERTS, 1, -1).astype(F32))
    yg = _gather_expert_rows(dest_flat, ys)
    out = _combine(h1, info, vec(g_final), yg, n_batch, seq)
    return out.reshape(n_batch, seq, d)
```

```python
import functools

import jax
import jax.numpy as jnp
from jax import lax
from jax.experimental import pallas as pl
from jax.experimental.pallas import tpu as pltpu
from jax.experimental.pallas import tpu_sc as plsc

F32 = jnp.float32
BF16 = jnp.bfloat16

D_MODEL = 1024
N_META = 16
CHUNK = 128
CHUNK_PAD = CHUNK - N_META
CONV_WIDTH = 31
HEAD_DIM = 128
N_HEADS = D_MODEL // HEAD_DIM
N_EXPERTS = 32
TOP_K = 4
D_FF = D_MODEL
SWIGLU_LIMIT = 7.0
SWIGLU_ALPHA = 1.702
RMS_EPS = 1e-6
LN_EPS = 1e-5

ACT_AGLU, ACT_Q, ACT_K, ACT_V, ACT_OG, ACT_SGA, ACT_SGB = range(7)
N_ACTS = 7

LANES = 128
SUBLANES = 8
SUB_BLOCK = 32
N_SUB = CHUNK // SUB_BLOCK
HALO = 32
ROW_TILE = 640
IN_TILE = 320
SCAN_CHUNKS = 5
MERGE_PARTS = 2
SLOTS_STEPS = 5
ROUTE_ROWS = 16
CONV_ROWS = 64
CONV_SUMS = 4
EXPERT_TILE = 1024
EXPERT_PATH_ROWS = tuple(EXPERT_TILE >> k for k in (3, 2, 1, 0))
SC_MAX_INDICES = 128
SC_ROW_BUFFER_BYTES = 384 * 1024
COMBINE_SUB = 8
NEG_BIG = -1e30
VMEM_LIMIT = 56 * 1024 * 1024


def _sigmoid(x):
    return 1.0 / (1.0 + jnp.exp(-x))


def _cparams(sem):
    return pltpu.CompilerParams(dimension_semantics=sem, vmem_limit_bytes=VMEM_LIMIT)


def _const_spec(shape):
    nd = len(shape)
    return pl.BlockSpec(shape, lambda *_: (0,) * nd)


PACK_WORDS = D_MODEL // 2
HIGH_HALF = 0xFFFF0000


def _pack_rows(x):
    lo = lax.bitcast_convert_type(x[:, :PACK_WORDS].astype(BF16).astype(F32), jnp.uint32)
    hi = lax.bitcast_convert_type(x[:, PACK_WORDS:].astype(BF16).astype(F32), jnp.uint32)
    return (lo >> 16) | (hi & jnp.uint32(HIGH_HALF))


def _unpack_rows(w):
    lo = lax.bitcast_convert_type(w << 16, F32)
    hi = lax.bitcast_convert_type(w & jnp.uint32(HIGH_HALF), F32)
    return jnp.concatenate([lo, hi], axis=1)


def _residual_tile(x_ref, meta_ref, is_first):
    x = x_ref[...]
    tm, d = x.shape
    prefix = jnp.concatenate([jnp.zeros((CHUNK_PAD, d), x.dtype), meta_ref[...]], axis=0)
    first = jnp.concatenate([prefix, x[:tm - CHUNK, :]], axis=0)
    return jnp.where(is_first, first, x)


def _residual_spec(tm, d, seq, tiles_per_batch, n_tiles):
    def start(i):
        i = jnp.minimum(i, n_tiles - 1)
        b, t = i // tiles_per_batch, i % tiles_per_batch
        return (pl.multiple_of(b * seq + jnp.maximum(t * tm - CHUNK, 0), SUBLANES), 0)
    return pl.BlockSpec((pl.Element(tm), pl.Element(d)), start)


def _in_proj_kernel(x_ref, meta_ref, g_ref, lbl_ref, w_ref, acts_ref, logf_ref, *, tiles_per_batch):
    d = D_MODEL

    def put(group, value):
        acts_ref[:, group * d:(group + 1) * d] = value.astype(acts_ref.dtype)

    h = _residual_tile(x_ref, meta_ref, pl.program_id(0) % tiles_per_batch == 0)
    ms = jnp.mean(h * h, axis=-1, keepdims=True)
    u = (h * lax.rsqrt(ms + RMS_EPS) * g_ref[...]).astype(BF16)

    def proj(j):
        return jnp.dot(u, w_ref[:, j * d:(j + 1) * d].astype(BF16), preferred_element_type=F32)

    put(ACT_AGLU, proj(0) * _sigmoid(proj(1)))
    zq = proj(2)
    put(ACT_Q, zq * _sigmoid(zq))
    lbl = lbl_ref[...]
    e = jnp.exp(lbl - jnp.max(lbl, axis=0, keepdims=True))
    lb = e[0:1, :] / jnp.sum(e, axis=0, keepdims=True)
    s = _sigmoid(proj(3))
    logf_ref[...] = jnp.log(lb + (1.0 - lb) * s)
    put(ACT_K, (1.0 - lb) * (1.0 - s))
    put(ACT_V, proj(4))
    zg = proj(5)
    put(ACT_OG, zg * _sigmoid(zg))
    put(ACT_SGA, _sigmoid(proj(6)))
    put(ACT_SGB, _sigmoid(proj(7)))


def _in_proj(x2, meta, seq, g_mix, lb_logits, w_in):
    d = x2.shape[1]
    n_batch = x2.shape[0] // seq
    tm = IN_TILE
    tpb = (seq + CHUNK) // tm
    tp = n_batch * (seq + CHUNK)
    assert tpb * tm == seq + CHUNK and tm > CHUNK
    row = pl.BlockSpec((tm, d), lambda i: (i, 0))
    return pl.pallas_call(
        functools.partial(_in_proj_kernel, tiles_per_batch=tpb),
        grid=(tp // tm,),
        in_specs=[_residual_spec(tm, d, seq, tpb, tp // tm), _const_spec(meta.shape), _const_spec((1, d)),
                  _const_spec(lb_logits.shape),
                  pl.BlockSpec(w_in.shape, lambda i: (0, 0), pipeline_mode=pl.Buffered(1))],
        out_specs=[pl.BlockSpec((tm, N_ACTS * d), lambda i: (i, 0)), row],
        out_shape=[jax.ShapeDtypeStruct((tp, N_ACTS * d), BF16), jax.ShapeDtypeStruct((tp, d), F32)],
        compiler_params=_cparams(("parallel",)),
        name="in_proj",
    )(x2, meta, g_mix, lb_logits, w_in)


def _conv_kernel(halo_ref, cur_ref, wdw_ref, bdw_ref, lng_ref, lnb_ref, wout_ref, bout_ref, sga_ref,
                 out_ref, win_ref, acc_ref, shift_ref):
    tm = cur_ref.shape[0]
    win_ref[0:HALO, :] = halo_ref[...].astype(F32)
    win_ref[HALO:, :] = cur_ref[...].astype(F32)
    first_tap = HALO - (CONV_WIDTH - 1)
    for c in range(D_MODEL // LANES):
        lanes = slice(c * LANES, (c + 1) * LANES)
        n_rows = tm + HALO - SUBLANES
        for s in range(1, SUBLANES):
            shift_ref[s - 1, 0:n_rows, :] = win_ref[pl.ds(s, n_rows), lanes]
        bias = jnp.broadcast_to(bdw_ref[:, lanes], (CONV_ROWS, LANES))
        groups = CONV_ROWS // SUBLANES

        def chunk(r, carry, c=c, lanes=lanes, bias=bias):
            r0 = pl.multiple_of(r * CONV_ROWS, CONV_ROWS)
            accs = [bias.reshape(groups, SUBLANES, LANES)] + [None] * (CONV_SUMS - 1)
            for j in range(CONV_WIDTH):
                off = first_tap + j
                rows = pl.ds(r0 + off - off % SUBLANES, CONV_ROWS)
                w_j = wdw_ref[c, pl.ds(j, SUBLANES, stride=0), :]
                window = shift_ref[off % SUBLANES - 1, rows, :] if off % SUBLANES else win_ref[rows, lanes]
                term = w_j[None] * window.reshape(groups, SUBLANES, LANES)
                accs[j % CONV_SUMS] = term if accs[j % CONV_SUMS] is None else accs[j % CONV_SUMS] + term
            while len(accs) > 1:
                accs = [a + b for a, b in zip(accs[0::2], accs[1::2])]
            acc_ref[pl.ds(r0, CONV_ROWS), lanes] = accs[0].reshape(CONV_ROWS, LANES)
            return carry

        lax.fori_loop(0, tm // CONV_ROWS, chunk, 0)
    a = acc_ref[...]
    mu = jnp.mean(a, axis=-1, keepdims=True)
    ac = a - mu
    var = jnp.mean(ac * ac, axis=-1, keepdims=True)
    y = ac * lax.rsqrt(var + LN_EPS) * lng_ref[...] + lnb_ref[...]
    y = y * _sigmoid(y)
    o = jnp.dot(y.astype(BF16), wout_ref[...].astype(BF16), preferred_element_type=F32) + bout_ref[...]
    out_ref[...] = (sga_ref[...].astype(F32) * o).astype(out_ref.dtype)


def _conv_branch(acts, w_dw, b_dw, ln_g, ln_b, w_out, b_out):
    tp, d = acts.shape[0], D_MODEL
    tm = ROW_TILE
    per = tm // HALO
    row = pl.BlockSpec((tm, d), lambda i: (i, 0))
    glu = pl.BlockSpec((tm, d), lambda i: (i, ACT_AGLU))
    gate = pl.BlockSpec((tm, d), lambda i: (i, ACT_SGA))
    halo = pl.BlockSpec((HALO, d), lambda i: (jnp.maximum(i * per - 1, 0), ACT_AGLU))
    vec = _const_spec((1, d))
    return pl.pallas_call(
        _conv_kernel,
        grid=(tp // tm,),
        in_specs=[halo, glu, _const_spec(w_dw.shape), vec, vec, vec, _const_spec((d, d)), vec, gate],
        out_specs=row,
        out_shape=jax.ShapeDtypeStruct((tp, d), BF16),
        scratch_shapes=[pltpu.VMEM((tm + HALO, d), F32), pltpu.VMEM((tm, d), F32),
                        pltpu.VMEM((SUBLANES - 1, tm + HALO - SUBLANES, LANES), F32)],
        compiler_params=_cparams(("parallel",)),
        name="conv_branch",
    )(acts, acts, w_dw, b_dw, ln_g, ln_b, w_out, b_out, acts)


def _split_bf16(x):
    hi = x.astype(BF16)
    return hi, (x - hi.astype(F32)).astype(BF16)


def _nt_dot(a, b):
    return lax.dot_general(a, b, (((1,), (1,)), ((), ())), preferred_element_type=F32)


def _tn_dot(a, b):
    return lax.dot_general(a, b, (((0,), (0,)), ((), ())), preferred_element_type=F32)


def _scan_kernel(q_ref, k_ref, v_ref, lf_ref, og_ref, gon_ref, o_ref, st_ref):
    @pl.when(pl.program_id(1) == 0)
    def _():
        st_ref[...] = jnp.zeros_like(st_ref)

    c = CHUNK
    row = lax.broadcasted_iota(jnp.int32, (c, c), 0)
    col = lax.broadcasted_iota(jnp.int32, (c, c), 1)
    causal = col <= row
    diag_mask = jnp.logical_and(causal, row // SUB_BLOCK == col // SUB_BLOCK)
    tri = causal.astype(BF16)

    def bcast_rows(rows):
        return jnp.concatenate([jnp.broadcast_to(r, (SUB_BLOCK, HEAD_DIM)) for r in rows], axis=0)

    heads = [slice(h * HEAD_DIM, (h + 1) * HEAD_DIM) for h in range(N_HEADS)]
    chunks = [slice(ci * c, (ci + 1) * c) for ci in range(SCAN_CHUNKS)]

    bcums = []
    for cs in chunks:
        lf = lf_ref[cs, :]
        lf_hi = lf.astype(BF16)
        lf_mid, lf_lo = _split_bf16(lf - lf_hi.astype(F32))
        bcums.append(jnp.dot(tri, lf_hi, preferred_element_type=F32)
                     + (jnp.dot(tri, lf_mid, preferred_element_type=F32)
                        + jnp.dot(tri, lf_lo, preferred_element_type=F32)))
    operands = {}
    for ci, cs in enumerate(chunks):
        for h, hs in enumerate(heads):
            b = bcums[ci][:, hs]
            q = q_ref[cs, hs].astype(F32)
            k = k_ref[cs, hs].astype(F32)
            ends = [b[i * SUB_BLOCK + SUB_BLOCK - 1:i * SUB_BLOCK + SUB_BLOCK, :] for i in range(N_SUB)]
            mids = [b[i * SUB_BLOCK + SUB_BLOCK // 2 - 1:i * SUB_BLOCK + SUB_BLOCK // 2, :] for i in range(N_SUB)]
            mid_full = bcast_rows(mids)
            b_last = ends[-1]
            zero_row = jnp.zeros_like(b_last)
            qm = q * jnp.exp(b - mid_full)
            km = k * jnp.exp(mid_full - b)
            q_parts, k_parts = [], []
            for j in range(N_SUB - 1):
                q_rows = [jnp.exp(mids[i] - ends[j]) if i > j else zero_row for i in range(N_SUB)]
                k_rows = [jnp.exp(ends[j] - mids[j]) if i == j else zero_row for i in range(N_SUB)]
                q_parts.append((qm * bcast_rows(q_rows)).astype(BF16))
                k_parts.append((km * bcast_rows(k_rows)).astype(BF16))
            q_in = (qm * bcast_rows([jnp.exp(m) for m in mids])).astype(BF16)
            k_out = (km * bcast_rows([jnp.exp(b_last - m) for m in mids])).astype(BF16)
            operands[ci, h] = (qm.astype(BF16), km.astype(BF16), jnp.concatenate(q_parts, axis=1),
                               jnp.concatenate(k_parts, axis=1), q_in, k_out, jnp.exp(b_last))
    products = {}
    for ci, cs in enumerate(chunks):
        for h, hs in enumerate(heads):
            qm, km, q_cat, k_cat, _, k_out, _ = operands[ci, h]
            products[ci, h] = (_nt_dot(qm, km), _nt_dot(q_cat, k_cat), _tn_dot(v_ref[cs, hs], k_out))
    within = {}
    for ci, cs in enumerate(chunks):
        for h, hs in enumerate(heads):
            same_block, earlier_blocks, _ = products[ci, h]
            scores = jnp.where(diag_mask, same_block, 0.0) + earlier_blocks
            within[ci, h] = jnp.dot(scores.astype(BF16), v_ref[cs, hs], preferred_element_type=F32)
    states = [st_ref[h] for h in range(N_HEADS)]
    outs = {}
    for ci in range(SCAN_CHUNKS):
        for h in range(N_HEADS):
            outs[ci, h] = within[ci, h] + _nt_dot(operands[ci, h][4], states[h].astype(BF16))
            states[h] = states[h] * operands[ci, h][6] + products[ci, h][2]
    for h in range(N_HEADS):
        st_ref[h] = states[h]
    for ci, cs in enumerate(chunks):
        for h, hs in enumerate(heads):
            o = outs[ci, h]
            ms = jnp.mean(o * o, axis=-1, keepdims=True)
            on = o * lax.rsqrt(ms + RMS_EPS) * gon_ref[:, hs]
            o_ref[cs, hs] = (on * og_ref[cs, hs].astype(F32)).astype(o_ref.dtype)


def _hgrn2_scan(acts, logf, g_onorm, n_batch):
    tp, d = logf.shape
    rows = SCAN_CHUNKS * CHUNK
    n_chunks = tp // n_batch // rows
    assert n_chunks * rows * n_batch == tp
    col = lambda group: pl.BlockSpec((rows, d), lambda b, c: (b * n_chunks + c, group))
    blk = col(0)
    return pl.pallas_call(
        _scan_kernel,
        grid=(n_batch, n_chunks),
        in_specs=[col(ACT_Q), col(ACT_K), col(ACT_V), blk, col(ACT_OG), pl.BlockSpec((1, d), lambda b, c: (0, 0))],
        out_specs=blk,
        out_shape=jax.ShapeDtypeStruct((tp, d), BF16),
        scratch_shapes=[pltpu.VMEM((N_HEADS, HEAD_DIM, HEAD_DIM), F32)],
        compiler_params=_cparams(("arbitrary", "arbitrary")),
        name="hgrn2_scan",
    )(acts, acts, acts, logf, acts, g_onorm)


def _merge_route_kernel(x_ref, meta_ref, on_ref, ap_ref, sgb_ref, wrnn_ref, wo_ref, gffn_ref, wr_ref, br_ref,
                        h1_ref, u2_ref, info_ref, route_ref, cnt_ref, tri_ref, carry_ref, colcnt_ref, logits_ref, hres_ref,
                        *, tiles_per_batch, n_tiles):
    tm = on_ref.shape[0]
    i = pl.program_id(0)
    hres_ref[...] = _residual_tile(x_ref, meta_ref, jnp.minimum(i, n_tiles - 1) % tiles_per_batch == 0)

    @pl.when(i == 0)
    def _():
        r_i = lax.broadcasted_iota(jnp.int32, (tm, tm), 0)
        c_i = lax.broadcasted_iota(jnp.int32, (tm, tm), 1)
        tri_ref[...] = (c_i < r_i).astype(BF16)
        carry_ref[...] = jnp.zeros_like(carry_ref)
        colcnt_ref[...] = jnp.zeros_like(colcnt_ref)
        logits_ref[...] = jnp.zeros_like(logits_ref)

    routed = i > 0
    lane = lax.broadcasted_iota(jnp.int32, (tm, LANES), 1)
    cur = jnp.where(lane < N_EXPERTS, logits_ref[...], NEG_BIG)
    vals, idxs, sels = [], [], []

    def topk_round(cur):
        m = jnp.max(cur, axis=-1, keepdims=True)
        idx = jnp.min(jnp.where(cur == m, lane, LANES), axis=-1, keepdims=True)
        sel = lane == idx
        vals.append(m)
        idxs.append(idx)
        sels.append(sel)
        return jnp.where(sel, 2.0 * NEG_BIG, cur)

    w_hi, w_lo = _split_bf16(wr_ref[...])
    w_hi_lo = jnp.concatenate([w_hi, w_lo], axis=1)
    part = tm // MERGE_PARTS
    rows = [slice(p * part, (p + 1) * part) for p in range(MERGE_PARTS)]
    w_rnn = wrnn_ref[...].astype(BF16)
    w_out = wo_ref[...].astype(BF16)
    rs = [jnp.dot(on_ref[rw, :], w_rnn, preferred_element_type=F32) for rw in rows]
    cur = topk_round(cur)
    ys = [(ap_ref[rw, :].astype(F32) + sgb_ref[rw, :].astype(F32) * r).astype(BF16) for rw, r in zip(rows, rs)]
    h1s = [hres_ref[rw, :] + jnp.dot(y, w_out, preferred_element_type=F32) for rw, y in zip(rows, ys)]
    cur = topk_round(cur)
    u2s = []
    for rw, h1 in zip(rows, h1s):
        h1_ref[rw, :] = h1
        ms = jnp.mean(h1 * h1, axis=-1, keepdims=True)
        u2 = h1 * lax.rsqrt(ms + RMS_EPS) * gffn_ref[...]
        u2_ref[rw, :] = _pack_rows(u2)
        u2s.append(u2)
    cur = topk_round(cur)
    new_logits = []
    for u2 in u2s:
        u_hi, u_lo = _split_bf16(u2)
        both = jnp.dot(u_hi, w_hi_lo, preferred_element_type=F32)
        new_logits.append(both[:, :LANES] + (jnp.dot(u_lo, w_hi, preferred_element_type=F32) + both[:, LANES:])
                          + br_ref[...])
    cur = topk_round(cur)
    assert len(vals) == TOP_K
    exps = [jnp.exp(vk - vals[0]) for vk in vals]
    den = exps[0] + exps[1] + exps[2] + exps[3]
    onehot = jnp.logical_or(jnp.logical_or(sels[0], sels[1]), jnp.logical_or(sels[2], sels[3]))
    onehot = jnp.logical_and(onehot, routed)
    onehot_bf = onehot.astype(BF16)
    rank_all = jnp.dot(tri_ref[...], onehot_bf, preferred_element_type=F32) + carry_ref[...]
    carry_ref[...] += jnp.sum(onehot.astype(F32), axis=0, keepdims=True)
    colcnt_ref[...] += _tn_dot(onehot_bf, jnp.ones((tm, LANES), BF16))
    info = jnp.zeros((tm, LANES), F32)
    for kk in range(TOP_K):
        rank_k = jnp.sum(jnp.where(sels[kk], rank_all, 0.0), axis=-1, keepdims=True)
        info = jnp.where(lane == kk, exps[kk] / den, info)
        info = jnp.where(lane == TOP_K + kk, idxs[kk].astype(F32), info)
        info = jnp.where(lane == 2 * TOP_K + kk, rank_k, info)
    info_ref[...] = info
    route_ref[...] = info.T[0:ROUTE_ROWS, :]
    cnt_ref[...] = colcnt_ref[...]
    logits_ref[...] = jnp.concatenate(new_logits, axis=0)


def _merge_route(x2, meta, seq, on, ap, acts, w_rnn, w_o, g_ffn, w_router_pad, b_router_pad):
    tp, d = on.shape
    tm = ROW_TILE
    n_tiles = tp // tm
    tpb = (seq + CHUNK) // tm
    assert tpb * tm == seq + CHUNK
    this_tile = lambda i: (jnp.minimum(i, n_tiles - 1), 0)
    prev_tile = lambda i: (jnp.maximum(i - 1, 0), 0)
    row = pl.BlockSpec((tm, d), this_tile)
    vec = _const_spec((1, d))
    return pl.pallas_call(
        functools.partial(_merge_route_kernel, tiles_per_batch=tpb, n_tiles=n_tiles),
        grid=(n_tiles + 1,),
        in_specs=[_residual_spec(tm, d, seq, tpb, n_tiles), _const_spec(meta.shape), row, row,
                  pl.BlockSpec((tm, d), lambda i: (jnp.minimum(i, n_tiles - 1), ACT_SGB)),
                  _const_spec((d, d)), _const_spec((d, d)), vec, _const_spec((d, LANES)), _const_spec((1, LANES))],
        out_specs=[row, pl.BlockSpec((tm, PACK_WORDS), this_tile),
                   pl.BlockSpec((tm, LANES), prev_tile),
                   pl.BlockSpec((ROUTE_ROWS, tm), lambda i: (0, jnp.maximum(i - 1, 0))), _const_spec((LANES, LANES))],
        out_shape=[jax.ShapeDtypeStruct((tp, d), F32), jax.ShapeDtypeStruct((tp, PACK_WORDS), jnp.uint32),
                   jax.ShapeDtypeStruct((tp, LANES), F32), jax.ShapeDtypeStruct((ROUTE_ROWS, tp), F32),
                   jax.ShapeDtypeStruct((LANES, LANES), F32)],
        scratch_shapes=[pltpu.VMEM((tm, tm), BF16), pltpu.VMEM((1, LANES), F32),
                        pltpu.VMEM((LANES, LANES), F32), pltpu.VMEM((tm, LANES), F32), pltpu.VMEM((tm, d), F32)],
        compiler_params=_cparams(("arbitrary",)),
        name="merge_route",
    )(x2, meta, on, ap, acts, w_rnn, w_o, g_ffn, w_router_pad, b_router_pad)


def _slots_kernel(route_ref, cnt_ref, dest_ref, te_ref, pstart_ref):
    tl = route_ref.shape[1]
    n_tile_lanes = te_ref.shape[1]

    @pl.when(pl.program_id(0) == 0)
    def _():
        r_i = lax.broadcasted_iota(jnp.int32, (LANES, LANES), 0)
        c_i = lax.broadcasted_iota(jnp.int32, (LANES, LANES), 1)
        cnt = cnt_ref[...]
        tiles = jnp.floor((cnt + (EXPERT_TILE - 1)) / EXPERT_TILE)
        pend_col = jnp.dot((c_i <= r_i).astype(F32), tiles, preferred_element_type=F32,
                           precision=lax.Precision.HIGHEST)
        pstart_ref[...] = pend_col - tiles
        tile_id = lax.broadcasted_iota(jnp.int32, (LANES, n_tile_lanes), 1).astype(F32)
        exp_id = lax.broadcasted_iota(jnp.int32, (LANES, n_tile_lanes), 0)
        pend_wide = jnp.concatenate([pend_col] * (n_tile_lanes // LANES), axis=1)
        below = jnp.logical_and(pend_wide <= tile_id, exp_id < N_EXPERTS)
        te = jnp.minimum(jnp.sum(below.astype(F32), axis=0, keepdims=True), N_EXPERTS - 1.0)
        n_active = pend_col[N_EXPERTS - 1:N_EXPERTS, 0:1]
        owner = exp_id.astype(F32) == te
        cnt_wide = jnp.concatenate([cnt] * (n_tile_lanes // LANES), axis=1)
        first_tile = pend_wide - jnp.concatenate([tiles] * (n_tile_lanes // LANES), axis=1)
        left = jnp.sum(jnp.where(owner, cnt_wide - (tile_id - first_tile) * EXPERT_TILE, 0.0), axis=0, keepdims=True)
        valid = jnp.clip(left, 0.0, float(EXPERT_TILE))
        sub_t = lax.broadcasted_iota(jnp.int32, (SUBLANES, n_tile_lanes), 0)
        out = jnp.where(sub_t == 0, jnp.broadcast_to(te, (SUBLANES, n_tile_lanes)),
                        jnp.where(sub_t == 1, jnp.broadcast_to(n_active, (SUBLANES, n_tile_lanes)),
                                  jnp.broadcast_to(valid, (SUBLANES, n_tile_lanes))))
        te_ref[...] = out.astype(jnp.int32)

    route = route_ref[...]
    expert = lax.broadcasted_iota(jnp.int32, (N_EXPERTS, tl), 0)
    first_slot = jnp.concatenate([pstart_ref[0:N_EXPERTS, :]] * (tl // LANES), axis=1) * EXPERT_TILE
    rows = []
    for kk in range(TOP_K):
        idx_k = route[TOP_K + kk:TOP_K + kk + 1, :].astype(jnp.int32)
        start_k = jnp.sum(jnp.where(expert == idx_k, first_slot, 0.0), axis=0, keepdims=True)
        rows.append(start_k + route[2 * TOP_K + kk:2 * TOP_K + kk + 1, :])
    rows.append(jnp.zeros((SUBLANES - TOP_K, tl), F32))
    dest_ref[...] = jnp.concatenate(rows, axis=0).astype(jnp.int32)


def _slots(route, cnt, n_tiles_max):
    tp = route.shape[1]
    tl = tp // SLOTS_STEPS
    assert tl * SLOTS_STEPS == tp and tl % LANES == 0
    n_tile_lanes = -(-n_tiles_max // LANES) * LANES
    return pl.pallas_call(
        _slots_kernel,
        grid=(SLOTS_STEPS,),
        in_specs=[pl.BlockSpec((ROUTE_ROWS, tl), lambda i: (0, i)), _const_spec((LANES, LANES))],
        out_specs=[pl.BlockSpec((SUBLANES, tl), lambda i: (0, i)), _const_spec((8, n_tile_lanes))],
        out_shape=[jax.ShapeDtypeStruct((SUBLANES, tp), jnp.int32),
                   jax.ShapeDtypeStruct((8, n_tile_lanes), jnp.int32)],
        scratch_shapes=[pltpu.VMEM((LANES, LANES), F32)],
        compiler_params=_cparams(("arbitrary",)),
        name="slots",
    )(route, cnt)


def _sc_workers():
    sc = plsc.get_sparse_core_info()
    return sc.num_cores, sc.num_cores * sc.num_subcores


def _sc_chunk(tp, n_row_bufs):
    per_worker = tp // _sc_workers()[1]
    limit = min(SC_MAX_INDICES, SC_ROW_BUFFER_BYTES // (n_row_bufs * PACK_WORDS * 4))
    return max(c for c in range(SUBLANES, limit + 1, SUBLANES) if per_worker % c == 0)


def _sc_token_chunks(tp, chunk, body):
    n_cores, n_workers = _sc_workers()
    per_worker = tp // n_workers
    assert per_worker * n_workers == tp and per_worker % chunk == 0 and chunk % SUBLANES == 0
    base = (lax.axis_index("s") * n_cores + lax.axis_index("c")) * per_worker

    @pl.loop(0, per_worker // chunk)
    def _(ci):
        body(pl.multiple_of(base + ci * chunk, SUBLANES))


def _sc_kernel(out_type, chunk, n_row_bufs):
    return functools.partial(
        pl.kernel, mesh=plsc.VectorSubcoreMesh(core_axis_name="c", subcore_axis_name="s"), out_type=out_type,
        scratch_types=[pltpu.VMEM((TOP_K, chunk), jnp.int32), pltpu.VMEM((n_row_bufs, chunk, PACK_WORDS), jnp.uint32),
                       pltpu.SemaphoreType.DMA, pltpu.SemaphoreType.DMA])


def _wait_all(copies):
    for c in copies:
        c.wait()


def _dispatch(dest_flat, u2p, n_slots):
    tp = u2p.shape[0]
    ch = _sc_chunk(tp, 1)

    @_sc_kernel(jax.ShapeDtypeStruct((n_slots, PACK_WORDS), jnp.uint32), ch, 1)
    def scatter_rows(u2_hbm, dest_hbm, xs_hbm, idx_v, rows_v, sem_a, sem_b):
        def chunk(off):
            loads = [pltpu.async_copy(u2_hbm.at[pl.ds(off, ch)], rows_v.at[0], sem_a)]
            loads += [pltpu.async_copy(dest_hbm.at[pl.ds(kk * tp + off, ch)], idx_v.at[kk], sem_a)
                      for kk in range(TOP_K)]
            _wait_all(loads)
            _wait_all([pltpu.async_copy(rows_v.at[0], xs_hbm.at[idx_v.at[kk]], sem_b) for kk in range(TOP_K)])

        _sc_token_chunks(tp, ch, chunk)

    return scatter_rows(u2p, dest_flat)


def _gather_expert_rows(dest_flat, ys):
    tp = dest_flat.shape[0] // TOP_K
    ch = _sc_chunk(tp, TOP_K)

    @_sc_kernel(jax.ShapeDtypeStruct((TOP_K, tp, PACK_WORDS), jnp.uint32), ch, TOP_K)
    def gather_rows(ys_hbm, dest_hbm, out_hbm, idx_v, rows_v, sem_a, sem_b):
        def chunk(off):
            _wait_all([pltpu.async_copy(dest_hbm.at[pl.ds(kk * tp + off, ch)], idx_v.at[kk], sem_a)
                       for kk in range(TOP_K)])
            _wait_all([pltpu.async_copy(ys_hbm.at[idx_v.at[kk]], rows_v.at[kk], sem_b) for kk in range(TOP_K)])
            _wait_all([pltpu.async_copy(rows_v.at[kk], out_hbm.at[kk, pl.ds(off, ch)], sem_a)
                       for kk in range(TOP_K)])

        _sc_token_chunks(tp, ch, chunk)

    return gather_rows(ys, dest_flat)


def _expert_kernel(te_ref, nact_ref, valid_ref, x_ref, wgu_hbm, bgu_ref, wdn_hbm, bdn_ref, y_ref,
                   wgu_buf, wdn_buf, slot_ref, sem):
    i = pl.program_id(0)
    n_active = nact_ref[0]
    last_tile = pl.num_programs(0) - 1

    def fetch(expert, slot):
        return (pltpu.make_async_copy(wgu_hbm.at[expert], wgu_buf.at[slot], sem.at[slot, 0]),
                pltpu.make_async_copy(wdn_hbm.at[expert], wdn_buf.at[slot], sem.at[slot, 1]))

    @pl.when(i == 0)
    def _():
        slot_ref[0] = 1
        for copy in fetch(te_ref[0], 0):
            copy.start()

    @pl.when(i < n_active)
    def _():
        expert = te_ref[i]
        first_tile_of_expert = jnp.logical_or(i == 0, expert != te_ref[jnp.maximum(i - 1, 0)])

        @pl.when(first_tile_of_expert)
        def _():
            slot = 1 - slot_ref[0]
            slot_ref[0] = slot
            for copy in fetch(expert, slot):
                copy.wait()
            nxt = lax.while_loop(
                lambda j: jnp.logical_and(j < n_active, te_ref[jnp.minimum(j, last_tile)] == expert),
                lambda j: j + 1, i + 1)

            @pl.when(nxt < n_active)
            def _():
                for copy in fetch(te_ref[jnp.minimum(nxt, last_tile)], 1 - slot):
                    copy.start()

        slot = slot_ref[0]

        def mlp(rows):
            x = _unpack_rows(x_ref[0:rows, :]).astype(BF16)
            gu = jnp.dot(x, wgu_buf[slot].astype(BF16), preferred_element_type=F32) + bgu_ref[0]
            gate = jnp.minimum(gu[:, :D_FF], SWIGLU_LIMIT)
            up = jnp.clip(gu[:, D_FF:], -SWIGLU_LIMIT, SWIGLU_LIMIT)
            hdn = (up + 1.0) * (gate * _sigmoid(SWIGLU_ALPHA * gate))
            y = jnp.dot(hdn.astype(BF16), wdn_buf[slot].astype(BF16), preferred_element_type=F32) + bdn_ref[0]
            y_ref[0:rows, :] = _pack_rows(y)

        path = sum((valid_ref[i] > rows).astype(jnp.int32) for rows in EXPERT_PATH_ROWS[:-1])
        for k, rows in enumerate(EXPERT_PATH_ROWS):
            @pl.when(path == k)
            def _(rows=rows):
                mlp(rows)
                if rows < EXPERT_TILE:
                    y_ref[rows:, :] = jnp.zeros((EXPERT_TILE - rows, PACK_WORDS), y_ref.dtype)

    @pl.when(i >= n_active)
    def _():
        y_ref[...] = jnp.zeros_like(y_ref)


def _expert_mlp(te, nact, valid, xs, w_gu, b_gu, w_dn, b_dn):
    d = D_MODEL
    n_tiles = xs.shape[0] // EXPERT_TILE

    def tile_map(i, te_ref, nact_ref, valid_ref):
        return (jnp.minimum(i, nact_ref[0] - 1), 0)

    def exp_map(i, te_ref, nact_ref, valid_ref):
        return (te_ref[jnp.minimum(i, nact_ref[0] - 1)], 0, 0)

    grid_spec = pltpu.PrefetchScalarGridSpec(
        num_scalar_prefetch=3,
        grid=(n_tiles,),
        in_specs=[pl.BlockSpec((EXPERT_TILE, PACK_WORDS), tile_map),
                  pl.BlockSpec(memory_space=pl.ANY),
                  pl.BlockSpec((1, 1, 2 * D_FF), exp_map),
                  pl.BlockSpec(memory_space=pl.ANY),
                  pl.BlockSpec((1, 1, d), exp_map)],
        out_specs=pl.BlockSpec((EXPERT_TILE, PACK_WORDS), lambda i, te_ref, nact_ref, valid_ref: (i, 0)),
        scratch_shapes=[pltpu.VMEM((2, d, 2 * D_FF), F32), pltpu.VMEM((2, D_FF, d), F32),
                        pltpu.SMEM((1,), jnp.int32), pltpu.SemaphoreType.DMA((2, 2))],
    )
    return pl.pallas_call(
        _expert_kernel,
        grid_spec=grid_spec,
        out_shape=jax.ShapeDtypeStruct(xs.shape, jnp.uint32),
        compiler_params=_cparams(("arbitrary",)),
        name="expert_mlp",
    )(te, nact, valid, xs, w_gu, b_gu, w_dn, b_dn)


def _combine_kernel(gfin_ref, *refs):
    out_ref = refs[-1]
    for s in range(COMBINE_SUB):
        h1_ref, info_ref, yg_ref = refs[3 * s:3 * s + 3]
        info = info_ref[...]
        h2 = h1_ref[...]
        for kk in range(TOP_K):
            h2 = h2 + info[:, kk:kk + 1] * _unpack_rows(yg_ref[kk])
        ms = jnp.mean(h2 * h2, axis=-1, keepdims=True)
        out_ref[s * CHUNK:(s + 1) * CHUNK, :] = h2 * lax.rsqrt(ms + RMS_EPS) * gfin_ref[...]


def _combine(h1, info, g_final, yg, n_batch, seq):
    d = h1.shape[1]
    chunks = seq // CHUNK

    def padded(s):
        def index(j):
            c = j * COMBINE_SUB + s
            return (c // chunks) * (chunks + 1) + c % chunks + 1
        return index

    in_specs = [pl.BlockSpec((1, d), lambda j: (0, 0))]
    operands = [g_final]
    for s in range(COMBINE_SUB):
        chunk_of = padded(s)
        in_specs += [pl.BlockSpec((CHUNK, d), lambda j, f=chunk_of: (f(j), 0)),
                     pl.BlockSpec((CHUNK, LANES), lambda j, f=chunk_of: (f(j), 0)),
                     pl.BlockSpec((TOP_K, CHUNK, PACK_WORDS), lambda j, f=chunk_of: (0, f(j), 0))]
        operands += [h1, info, yg]
    return pl.pallas_call(
        _combine_kernel,
        grid=(n_batch * chunks // COMBINE_SUB,),
        in_specs=in_specs,
        out_specs=pl.BlockSpec((COMBINE_SUB * CHUNK, d), lambda j: (j, 0)),
        out_shape=jax.ShapeDtypeStruct((n_batch * seq, d), F32),
        compiler_params=_cparams(("parallel",)),
        name="combine",
    )(*operands)


def kernel(x, meta_tokens, lb_logits, g_mix, w_in, w_dw, b_dw, ln_g, ln_b, w_conv_out, b_conv_out,
           g_onorm, w_rnn_out, w_o, g_ffn, w_router, b_router, w_gate_up, b_gate_up, w_down, b_down,
           g_final):
    n_batch, seq, d = x.shape
    assert d == D_MODEL and w_in.shape[0] == 1, "single-layer block with D_MODEL features"
    assert seq % (CHUNK * COMBINE_SUB) == 0
    seq_pad = CHUNK + seq
    tp = n_batch * seq_pad
    assert tp % ROW_TILE == 0

    x2 = x.reshape(n_batch * seq, d)
    meta = meta_tokens.astype(x.dtype)
    vec = lambda a: a.reshape(1, -1).astype(F32)

    acts, logf = _in_proj(x2, meta, seq, vec(g_mix[0]), lb_logits.astype(F32), w_in[0])
    w_taps = w_dw[0].astype(F32).reshape(CONV_WIDTH, d // LANES, LANES).transpose(1, 0, 2)
    ap = _conv_branch(acts, w_taps, vec(b_dw[0]), vec(ln_g[0]), vec(ln_b[0]),
                      w_conv_out[0], vec(b_conv_out[0]))
    on = _hgrn2_scan(acts, logf, vec(g_onorm[0]), n_batch)
    w_router_pad = jnp.pad(w_router[0].astype(F32), ((0, 0), (0, LANES - N_EXPERTS)))
    b_router_pad = jnp.pad(vec(b_router[0]), ((0, 0), (0, LANES - N_EXPERTS)))
    h1, u2p, info, route, cnt = _merge_route(x2, meta, seq, on, ap, acts, w_rnn_out[0], w_o[0],
                                      vec(g_ffn[0]), w_router_pad, b_router_pad)

    n_tiles_max = -(-(tp * TOP_K + N_EXPERTS * (EXPERT_TILE - 1)) // EXPERT_TILE)
    dest, te = _slots(route, cnt, n_tiles_max)
    dest_flat = dest[:TOP_K].reshape(-1)
    xs = _dispatch(dest_flat, u2p, n_tiles_max * EXPERT_TILE)
    ys = _expert_mlp(te[0, :n_tiles_max], te[1, :1], te[2, :n_tiles_max], xs, w_gate_up[0],
                     b_gate_up[0].reshape(N_EXPERTS, 1, -1).astype(F32), w_down[0],
                     b_down[0].reshape(N_EXPERTS, 1, -1).astype(F32))
    yg = _gather_expert_rows(dest_flat, ys)
    out = _combine(h1, info, vec(g_final), yg, n_batch, seq)
    return out.reshape(n_batch, seq, d)
```

```python
import functools

import jax
import jax.numpy as jnp
from jax import lax
from jax.experimental import pallas as pl
from jax.experimental.pallas import tpu as pltpu
from jax.experimental.pallas import tpu_sc as plsc

F32 = jnp.float32
BF16 = jnp.bfloat16

D_MODEL = 1024
N_META = 16
CHUNK = 128
CHUNK_PAD = CHUNK - N_META
CONV_WIDTH = 31
HEAD_DIM = 128
N_HEADS = D_MODEL // HEAD_DIM
N_EXPERTS = 32
TOP_K = 4
D_FF = D_MODEL
SWIGLU_LIMIT = 7.0
SWIGLU_ALPHA = 1.702
RMS_EPS = 1e-6
LN_EPS = 1e-5

ACT_AGLU, ACT_Q, ACT_K, ACT_V, ACT_OG, ACT_SGA, ACT_SGB = range(7)
N_ACTS = 7

LANES = 128
SUBLANES = 8
SUB_BLOCK = 32
N_SUB = CHUNK // SUB_BLOCK
HALO = 32
ROW_TILE = 640
IN_TILE = 320
SCAN_CHUNKS = 5
MERGE_PARTS = 2
SLOTS_STEPS = 5
ROUTE_ROWS = 16
CONV_ROWS = 64
CONV_SUMS = 4
EXPERT_TILE = 1024
EXPERT_PATH_ROWS = tuple(EXPERT_TILE >> k for k in (3, 2, 1, 0))
SC_MAX_INDICES = 128
SC_ROW_BUFFER_BYTES = 384 * 1024
GATHER_RING = 4
GATHER_LAG = 2
COMBINE_SUB = 8
NEG_BIG = -1e30
VMEM_LIMIT = 56 * 1024 * 1024


def _sigmoid(x):
    return 1.0 / (1.0 + jnp.exp(-x))


def _cparams(sem):
    return pltpu.CompilerParams(dimension_semantics=sem, vmem_limit_bytes=VMEM_LIMIT)


def _const_spec(shape):
    nd = len(shape)
    return pl.BlockSpec(shape, lambda *_: (0,) * nd)


PACK_WORDS = D_MODEL // 2
HIGH_HALF = 0xFFFF0000


def _pack_rows(x):
    lo = lax.bitcast_convert_type(x[:, :PACK_WORDS].astype(BF16).astype(F32), jnp.uint32)
    hi = lax.bitcast_convert_type(x[:, PACK_WORDS:].astype(BF16).astype(F32), jnp.uint32)
    return (lo >> 16) | (hi & jnp.uint32(HIGH_HALF))


def _unpack_rows(w):
    lo = lax.bitcast_convert_type(w << 16, F32)
    hi = lax.bitcast_convert_type(w & jnp.uint32(HIGH_HALF), F32)
    return jnp.concatenate([lo, hi], axis=1)


def _residual_tile(x_ref, meta_ref, is_first):
    x = x_ref[...]
    tm, d = x.shape
    prefix = jnp.concatenate([jnp.zeros((CHUNK_PAD, d), x.dtype), meta_ref[...]], axis=0)
    first = jnp.concatenate([prefix, x[:tm - CHUNK, :]], axis=0)
    return jnp.where(is_first, first, x)


def _residual_spec(tm, d, seq, tiles_per_batch, n_tiles):
    def start(i):
        i = jnp.minimum(i, n_tiles - 1)
        b, t = i // tiles_per_batch, i % tiles_per_batch
        return (pl.multiple_of(b * seq + jnp.maximum(t * tm - CHUNK, 0), SUBLANES), 0)
    return pl.BlockSpec((pl.Element(tm), pl.Element(d)), start)


def _in_proj_kernel(x_ref, meta_ref, g_ref, lbl_ref, w_ref, acts_ref, logf_ref, *, tiles_per_batch):
    d = D_MODEL

    def put(group, value):
        acts_ref[:, group * d:(group + 1) * d] = value.astype(acts_ref.dtype)

    h = _residual_tile(x_ref, meta_ref, pl.program_id(0) % tiles_per_batch == 0)
    ms = jnp.mean(h * h, axis=-1, keepdims=True)
    u = (h * lax.rsqrt(ms + RMS_EPS) * g_ref[...]).astype(BF16)

    def proj(j):
        return jnp.dot(u, w_ref[:, j * d:(j + 1) * d].astype(BF16), preferred_element_type=F32)

    put(ACT_AGLU, proj(0) * _sigmoid(proj(1)))
    zq = proj(2)
    put(ACT_Q, zq * _sigmoid(zq))
    lbl = lbl_ref[...]
    e = jnp.exp(lbl - jnp.max(lbl, axis=0, keepdims=True))
    lb = e[0:1, :] / jnp.sum(e, axis=0, keepdims=True)
    s = _sigmoid(proj(3))
    logf_ref[...] = jnp.log(lb + (1.0 - lb) * s)
    put(ACT_K, (1.0 - lb) * (1.0 - s))
    put(ACT_V, proj(4))
    zg = proj(5)
    put(ACT_OG, zg * _sigmoid(zg))
    put(ACT_SGA, _sigmoid(proj(6)))
    put(ACT_SGB, _sigmoid(proj(7)))


def _in_proj(x2, meta, seq, g_mix, lb_logits, w_in):
    d = x2.shape[1]
    n_batch = x2.shape[0] // seq
    tm = IN_TILE
    tpb = (seq + CHUNK) // tm
    tp = n_batch * (seq + CHUNK)
    assert tpb * tm == seq + CHUNK and tm > CHUNK
    row = pl.BlockSpec((tm, d), lambda i: (i, 0))
    return pl.pallas_call(
        functools.partial(_in_proj_kernel, tiles_per_batch=tpb),
        grid=(tp // tm,),
        in_specs=[_residual_spec(tm, d, seq, tpb, tp // tm), _const_spec(meta.shape), _const_spec((1, d)),
                  _const_spec(lb_logits.shape),
                  pl.BlockSpec(w_in.shape, lambda i: (0, 0), pipeline_mode=pl.Buffered(1))],
        out_specs=[pl.BlockSpec((tm, N_ACTS * d), lambda i: (i, 0)), row],
        out_shape=[jax.ShapeDtypeStruct((tp, N_ACTS * d), BF16), jax.ShapeDtypeStruct((tp, d), F32)],
        compiler_params=_cparams(("parallel",)),
        name="in_proj",
    )(x2, meta, g_mix, lb_logits, w_in)


def _conv_kernel(halo_ref, cur_ref, wdw_ref, bdw_ref, lng_ref, lnb_ref, wout_ref, bout_ref, sga_ref,
                 out_ref, win_ref, acc_ref, shift_ref):
    tm = cur_ref.shape[0]
    win_ref[0:HALO, :] = halo_ref[...].astype(F32)
    win_ref[HALO:, :] = cur_ref[...].astype(F32)
    first_tap = HALO - (CONV_WIDTH - 1)
    for c in range(D_MODEL // LANES):
        lanes = slice(c * LANES, (c + 1) * LANES)
        n_rows = tm + HALO - SUBLANES
        for s in range(1, SUBLANES):
            shift_ref[s - 1, 0:n_rows, :] = win_ref[pl.ds(s, n_rows), lanes]
        bias = jnp.broadcast_to(bdw_ref[:, lanes], (CONV_ROWS, LANES))
        groups = CONV_ROWS // SUBLANES

        def chunk(r, carry, c=c, lanes=lanes, bias=bias):
            r0 = pl.multiple_of(r * CONV_ROWS, CONV_ROWS)
            accs = [bias.reshape(groups, SUBLANES, LANES)] + [None] * (CONV_SUMS - 1)
            for j in range(CONV_WIDTH):
                off = first_tap + j
                rows = pl.ds(r0 + off - off % SUBLANES, CONV_ROWS)
                w_j = wdw_ref[c, pl.ds(j, SUBLANES, stride=0), :]
                window = shift_ref[off % SUBLANES - 1, rows, :] if off % SUBLANES else win_ref[rows, lanes]
                term = w_j[None] * window.reshape(groups, SUBLANES, LANES)
                accs[j % CONV_SUMS] = term if accs[j % CONV_SUMS] is None else accs[j % CONV_SUMS] + term
            while len(accs) > 1:
                accs = [a + b for a, b in zip(accs[0::2], accs[1::2])]
            acc_ref[pl.ds(r0, CONV_ROWS), lanes] = accs[0].reshape(CONV_ROWS, LANES)
            return carry

        lax.fori_loop(0, tm // CONV_ROWS, chunk, 0)
    a = acc_ref[...]
    mu = jnp.mean(a, axis=-1, keepdims=True)
    ac = a - mu
    var = jnp.mean(ac * ac, axis=-1, keepdims=True)
    y = ac * lax.rsqrt(var + LN_EPS) * lng_ref[...] + lnb_ref[...]
    y = y * _sigmoid(y)
    o = jnp.dot(y.astype(BF16), wout_ref[...].astype(BF16), preferred_element_type=F32) + bout_ref[...]
    out_ref[...] = (sga_ref[...].astype(F32) * o).astype(out_ref.dtype)


def _conv_branch(acts, w_dw, b_dw, ln_g, ln_b, w_out, b_out):
    tp, d = acts.shape[0], D_MODEL
    tm = ROW_TILE
    per = tm // HALO
    row = pl.BlockSpec((tm, d), lambda i: (i, 0))
    glu = pl.BlockSpec((tm, d), lambda i: (i, ACT_AGLU))
    gate = pl.BlockSpec((tm, d), lambda i: (i, ACT_SGA))
    halo = pl.BlockSpec((HALO, d), lambda i: (jnp.maximum(i * per - 1, 0), ACT_AGLU))
    vec = _const_spec((1, d))
    return pl.pallas_call(
        _conv_kernel,
        grid=(tp // tm,),
        in_specs=[halo, glu, _const_spec(w_dw.shape), vec, vec, vec, _const_spec((d, d)), vec, gate],
        out_specs=row,
        out_shape=jax.ShapeDtypeStruct((tp, d), BF16),
        scratch_shapes=[pltpu.VMEM((tm + HALO, d), F32), pltpu.VMEM((tm, d), F32),
                        pltpu.VMEM((SUBLANES - 1, tm + HALO - SUBLANES, LANES), F32)],
        compiler_params=_cparams(("parallel",)),
        name="conv_branch",
    )(acts, acts, w_dw, b_dw, ln_g, ln_b, w_out, b_out, acts)


def _split_bf16(x):
    hi = x.astype(BF16)
    return hi, (x - hi.astype(F32)).astype(BF16)


def _nt_dot(a, b):
    return lax.dot_general(a, b, (((1,), (1,)), ((), ())), preferred_element_type=F32)


def _tn_dot(a, b):
    return lax.dot_general(a, b, (((0,), (0,)), ((), ())), preferred_element_type=F32)


def _scan_kernel(q_ref, k_ref, v_ref, lf_ref, og_ref, gon_ref, o_ref, st_ref):
    @pl.when(pl.program_id(1) == 0)
    def _():
        st_ref[...] = jnp.zeros_like(st_ref)

    c = CHUNK
    row = lax.broadcasted_iota(jnp.int32, (c, c), 0)
    col = lax.broadcasted_iota(jnp.int32, (c, c), 1)
    causal = col <= row
    diag_mask = jnp.logical_and(causal, row // SUB_BLOCK == col // SUB_BLOCK)
    tri = causal.astype(BF16)

    def bcast_rows(rows):
        return jnp.concatenate([jnp.broadcast_to(r, (SUB_BLOCK, HEAD_DIM)) for r in rows], axis=0)

    heads = [slice(h * HEAD_DIM, (h + 1) * HEAD_DIM) for h in range(N_HEADS)]
    chunks = [slice(ci * c, (ci + 1) * c) for ci in range(SCAN_CHUNKS)]

    bcums = []
    for cs in chunks:
        lf = lf_ref[cs, :]
        lf_hi = lf.astype(BF16)
        lf_mid, lf_lo = _split_bf16(lf - lf_hi.astype(F32))
        bcums.append(jnp.dot(tri, lf_hi, preferred_element_type=F32)
                     + (jnp.dot(tri, lf_mid, preferred_element_type=F32)
                        + jnp.dot(tri, lf_lo, preferred_element_type=F32)))
    operands = {}
    for ci, cs in enumerate(chunks):
        for h, hs in enumerate(heads):
            b = bcums[ci][:, hs]
            q = q_ref[cs, hs].astype(F32)
            k = k_ref[cs, hs].astype(F32)
            ends = [b[i * SUB_BLOCK + SUB_BLOCK - 1:i * SUB_BLOCK + SUB_BLOCK, :] for i in range(N_SUB)]
            mids = [b[i * SUB_BLOCK + SUB_BLOCK // 2 - 1:i * SUB_BLOCK + SUB_BLOCK // 2, :] for i in range(N_SUB)]
            mid_full = bcast_rows(mids)
            b_last = ends[-1]
            zero_row = jnp.zeros_like(b_last)
            qm = q * jnp.exp(b - mid_full)
            km = k * jnp.exp(mid_full - b)
            q_parts, k_parts = [], []
            for j in range(N_SUB - 1):
                q_rows = [jnp.exp(mids[i] - ends[j]) if i > j else zero_row for i in range(N_SUB)]
                k_rows = [jnp.exp(ends[j] - mids[j]) if i == j else zero_row for i in range(N_SUB)]
                q_parts.append((qm * bcast_rows(q_rows)).astype(BF16))
                k_parts.append((km * bcast_rows(k_rows)).astype(BF16))
            q_in = (qm * bcast_rows([jnp.exp(m) for m in mids])).astype(BF16)
            k_out = (km * bcast_rows([jnp.exp(b_last - m) for m in mids])).astype(BF16)
            operands[ci, h] = (qm.astype(BF16), km.astype(BF16), jnp.concatenate(q_parts, axis=1),
                               jnp.concatenate(k_parts, axis=1), q_in, k_out, jnp.exp(b_last))
    products = {}
    for ci, cs in enumerate(chunks):
        for h, hs in enumerate(heads):
            qm, km, q_cat, k_cat, _, k_out, _ = operands[ci, h]
            products[ci, h] = (_nt_dot(qm, km), _nt_dot(q_cat, k_cat), _tn_dot(v_ref[cs, hs], k_out))
    within = {}
    for ci, cs in enumerate(chunks):
        for h, hs in enumerate(heads):
            same_block, earlier_blocks, _ = products[ci, h]
            scores = jnp.where(diag_mask, same_block, 0.0) + earlier_blocks
            within[ci, h] = jnp.dot(scores.astype(BF16), v_ref[cs, hs], preferred_element_type=F32)
    states = [st_ref[h] for h in range(N_HEADS)]
    outs = {}
    for ci in range(SCAN_CHUNKS):
        for h in range(N_HEADS):
            outs[ci, h] = within[ci, h] + _nt_dot(operands[ci, h][4], states[h].astype(BF16))
            states[h] = states[h] * operands[ci, h][6] + products[ci, h][2]
    for h in range(N_HEADS):
        st_ref[h] = states[h]
    for ci, cs in enumerate(chunks):
        for h, hs in enumerate(heads):
            o = outs[ci, h]
            ms = jnp.mean(o * o, axis=-1, keepdims=True)
            on = o * lax.rsqrt(ms + RMS_EPS) * gon_ref[:, hs]
            o_ref[cs, hs] = (on * og_ref[cs, hs].astype(F32)).astype(o_ref.dtype)


def _hgrn2_scan(acts, logf, g_onorm, n_batch):
    tp, d = logf.shape
    rows = SCAN_CHUNKS * CHUNK
    n_chunks = tp // n_batch // rows
    assert n_chunks * rows * n_batch == tp
    col = lambda group: pl.BlockSpec((rows, d), lambda b, c: (b * n_chunks + c, group))
    blk = col(0)
    return pl.pallas_call(
        _scan_kernel,
        grid=(n_batch, n_chunks),
        in_specs=[col(ACT_Q), col(ACT_K), col(ACT_V), blk, col(ACT_OG), pl.BlockSpec((1, d), lambda b, c: (0, 0))],
        out_specs=blk,
        out_shape=jax.ShapeDtypeStruct((tp, d), BF16),
        scratch_shapes=[pltpu.VMEM((N_HEADS, HEAD_DIM, HEAD_DIM), F32)],
        compiler_params=_cparams(("arbitrary", "arbitrary")),
        name="hgrn2_scan",
    )(acts, acts, acts, logf, acts, g_onorm)


def _merge_route_kernel(x_ref, meta_ref, on_ref, ap_ref, sgb_ref, wrnn_ref, wo_ref, gffn_ref, wr_ref, br_ref,
                        h1_ref, u2_ref, info_ref, route_ref, cnt_ref, tri_ref, carry_ref, colcnt_ref, logits_ref, hres_ref,
                        *, tiles_per_batch, n_tiles):
    tm = on_ref.shape[0]
    i = pl.program_id(0)
    hres_ref[...] = _residual_tile(x_ref, meta_ref, jnp.minimum(i, n_tiles - 1) % tiles_per_batch == 0)

    @pl.when(i == 0)
    def _():
        r_i = lax.broadcasted_iota(jnp.int32, (tm, tm), 0)
        c_i = lax.broadcasted_iota(jnp.int32, (tm, tm), 1)
        tri_ref[...] = (c_i < r_i).astype(BF16)
        carry_ref[...] = jnp.zeros_like(carry_ref)
        colcnt_ref[...] = jnp.zeros_like(colcnt_ref)
        logits_ref[...] = jnp.zeros_like(logits_ref)

    routed = i > 0
    lane = lax.broadcasted_iota(jnp.int32, (tm, LANES), 1)
    cur = jnp.where(lane < N_EXPERTS, logits_ref[...], NEG_BIG)
    vals, idxs, sels = [], [], []

    def topk_round(cur):
        m = jnp.max(cur, axis=-1, keepdims=True)
        idx = jnp.min(jnp.where(cur == m, lane, LANES), axis=-1, keepdims=True)
        sel = lane == idx
        vals.append(m)
        idxs.append(idx)
        sels.append(sel)
        return jnp.where(sel, 2.0 * NEG_BIG, cur)

    w_hi, w_lo = _split_bf16(wr_ref[...])
    w_hi_lo = jnp.concatenate([w_hi, w_lo], axis=1)
    part = tm // MERGE_PARTS
    rows = [slice(p * part, (p + 1) * part) for p in range(MERGE_PARTS)]
    w_rnn = wrnn_ref[...].astype(BF16)
    w_out = wo_ref[...].astype(BF16)
    rs = [jnp.dot(on_ref[rw, :], w_rnn, preferred_element_type=F32) for rw in rows]
    cur = topk_round(cur)
    ys = [(ap_ref[rw, :].astype(F32) + sgb_ref[rw, :].astype(F32) * r).astype(BF16) for rw, r in zip(rows, rs)]
    h1s = [hres_ref[rw, :] + jnp.dot(y, w_out, preferred_element_type=F32) for rw, y in zip(rows, ys)]
    cur = topk_round(cur)
    u2s = []
    for rw, h1 in zip(rows, h1s):
        h1_ref[rw, :] = h1
        ms = jnp.mean(h1 * h1, axis=-1, keepdims=True)
        u2 = h1 * lax.rsqrt(ms + RMS_EPS) * gffn_ref[...]
        u2_ref[rw, :] = _pack_rows(u2)
        u2s.append(u2)
    cur = topk_round(cur)
    new_logits = []
    for u2 in u2s:
        u_hi, u_lo = _split_bf16(u2)
        both = jnp.dot(u_hi, w_hi_lo, preferred_element_type=F32)
        new_logits.append(both[:, :LANES] + (jnp.dot(u_lo, w_hi, preferred_element_type=F32) + both[:, LANES:])
                          + br_ref[...])
    cur = topk_round(cur)
    assert len(vals) == TOP_K
    exps = [jnp.exp(vk - vals[0]) for vk in vals]
    den = exps[0] + exps[1] + exps[2] + exps[3]
    onehot = jnp.logical_or(jnp.logical_or(sels[0], sels[1]), jnp.logical_or(sels[2], sels[3]))
    onehot = jnp.logical_and(onehot, routed)
    onehot_bf = onehot.astype(BF16)
    rank_all = jnp.dot(tri_ref[...], onehot_bf, preferred_element_type=F32) + carry_ref[...]
    carry_ref[...] += jnp.sum(onehot.astype(F32), axis=0, keepdims=True)
    colcnt_ref[...] += _tn_dot(onehot_bf, jnp.ones((tm, LANES), BF16))
    info = jnp.zeros((tm, LANES), F32)
    for kk in range(TOP_K):
        rank_k = jnp.sum(jnp.where(sels[kk], rank_all, 0.0), axis=-1, keepdims=True)
        info = jnp.where(lane == kk, exps[kk] / den, info)
        info = jnp.where(lane == TOP_K + kk, idxs[kk].astype(F32), info)
        info = jnp.where(lane == 2 * TOP_K + kk, rank_k, info)
    info_ref[...] = info
    route_ref[...] = info.T[0:ROUTE_ROWS, :]
    cnt_ref[...] = colcnt_ref[...]
    logits_ref[...] = jnp.concatenate(new_logits, axis=0)


def _merge_route(x2, meta, seq, on, ap, acts, w_rnn, w_o, g_ffn, w_router_pad, b_router_pad):
    tp, d = on.shape
    tm = ROW_TILE
    n_tiles = tp // tm
    tpb = (seq + CHUNK) // tm
    assert tpb * tm == seq + CHUNK
    this_tile = lambda i: (jnp.minimum(i, n_tiles - 1), 0)
    prev_tile = lambda i: (jnp.maximum(i - 1, 0), 0)
    row = pl.BlockSpec((tm, d), this_tile)
    vec = _const_spec((1, d))
    return pl.pallas_call(
        functools.partial(_merge_route_kernel, tiles_per_batch=tpb, n_tiles=n_tiles),
        grid=(n_tiles + 1,),
        in_specs=[_residual_spec(tm, d, seq, tpb, n_tiles), _const_spec(meta.shape), row, row,
                  pl.BlockSpec((tm, d), lambda i: (jnp.minimum(i, n_tiles - 1), ACT_SGB)),
                  _const_spec((d, d)), _const_spec((d, d)), vec, _const_spec((d, LANES)), _const_spec((1, LANES))],
        out_specs=[row, pl.BlockSpec((tm, PACK_WORDS), this_tile),
                   pl.BlockSpec((tm, LANES), prev_tile),
                   pl.BlockSpec((ROUTE_ROWS, tm), lambda i: (0, jnp.maximum(i - 1, 0))), _const_spec((LANES, LANES))],
        out_shape=[jax.ShapeDtypeStruct((tp, d), F32), jax.ShapeDtypeStruct((tp, PACK_WORDS), jnp.uint32),
                   jax.ShapeDtypeStruct((tp, LANES), F32), jax.ShapeDtypeStruct((ROUTE_ROWS, tp), F32),
                   jax.ShapeDtypeStruct((LANES, LANES), F32)],
        scratch_shapes=[pltpu.VMEM((tm, tm), BF16), pltpu.VMEM((1, LANES), F32),
                        pltpu.VMEM((LANES, LANES), F32), pltpu.VMEM((tm, LANES), F32), pltpu.VMEM((tm, d), F32)],
        compiler_params=_cparams(("arbitrary",)),
        name="merge_route",
    )(x2, meta, on, ap, acts, w_rnn, w_o, g_ffn, w_router_pad, b_router_pad)


def _slots_kernel(route_ref, cnt_ref, dest_ref, te_ref, pstart_ref):
    tl = route_ref.shape[1]
    n_tile_lanes = te_ref.shape[1]

    @pl.when(pl.program_id(0) == 0)
    def _():
        r_i = lax.broadcasted_iota(jnp.int32, (LANES, LANES), 0)
        c_i = lax.broadcasted_iota(jnp.int32, (LANES, LANES), 1)
        cnt = cnt_ref[...]
        tiles = jnp.floor((cnt + (EXPERT_TILE - 1)) / EXPERT_TILE)
        pend_col = jnp.dot((c_i <= r_i).astype(F32), tiles, preferred_element_type=F32,
                           precision=lax.Precision.HIGHEST)
        pstart_ref[...] = pend_col - tiles
        tile_id = lax.broadcasted_iota(jnp.int32, (LANES, n_tile_lanes), 1).astype(F32)
        exp_id = lax.broadcasted_iota(jnp.int32, (LANES, n_tile_lanes), 0)
        pend_wide = jnp.concatenate([pend_col] * (n_tile_lanes // LANES), axis=1)
        below = jnp.logical_and(pend_wide <= tile_id, exp_id < N_EXPERTS)
        te = jnp.minimum(jnp.sum(below.astype(F32), axis=0, keepdims=True), N_EXPERTS - 1.0)
        n_active = pend_col[N_EXPERTS - 1:N_EXPERTS, 0:1]
        owner = exp_id.astype(F32) == te
        cnt_wide = jnp.concatenate([cnt] * (n_tile_lanes // LANES), axis=1)
        first_tile = pend_wide - jnp.concatenate([tiles] * (n_tile_lanes // LANES), axis=1)
        left = jnp.sum(jnp.where(owner, cnt_wide - (tile_id - first_tile) * EXPERT_TILE, 0.0), axis=0, keepdims=True)
        valid = jnp.clip(left, 0.0, float(EXPERT_TILE))
        sub_t = lax.broadcasted_iota(jnp.int32, (SUBLANES, n_tile_lanes), 0)
        out = jnp.where(sub_t == 0, jnp.broadcast_to(te, (SUBLANES, n_tile_lanes)),
                        jnp.where(sub_t == 1, jnp.broadcast_to(n_active, (SUBLANES, n_tile_lanes)),
                                  jnp.broadcast_to(valid, (SUBLANES, n_tile_lanes))))
        te_ref[...] = out.astype(jnp.int32)

    route = route_ref[...]
    expert = lax.broadcasted_iota(jnp.int32, (N_EXPERTS, tl), 0)
    first_slot = jnp.concatenate([pstart_ref[0:N_EXPERTS, :]] * (tl // LANES), axis=1) * EXPERT_TILE
    rows = []
    for kk in range(TOP_K):
        idx_k = route[TOP_K + kk:TOP_K + kk + 1, :].astype(jnp.int32)
        start_k = jnp.sum(jnp.where(expert == idx_k, first_slot, 0.0), axis=0, keepdims=True)
        rows.append(start_k + route[2 * TOP_K + kk:2 * TOP_K + kk + 1, :])
    rows.append(jnp.zeros((SUBLANES - TOP_K, tl), F32))
    dest_ref[...] = jnp.concatenate(rows, axis=0).astype(jnp.int32)


def _slots(route, cnt, n_tiles_max):
    tp = route.shape[1]
    tl = tp // SLOTS_STEPS
    assert tl * SLOTS_STEPS == tp and tl % LANES == 0
    n_tile_lanes = -(-n_tiles_max // LANES) * LANES
    return pl.pallas_call(
        _slots_kernel,
        grid=(SLOTS_STEPS,),
        in_specs=[pl.BlockSpec((ROUTE_ROWS, tl), lambda i: (0, i)), _const_spec((LANES, LANES))],
        out_specs=[pl.BlockSpec((SUBLANES, tl), lambda i: (0, i)), _const_spec((8, n_tile_lanes))],
        out_shape=[jax.ShapeDtypeStruct((SUBLANES, tp), jnp.int32),
                   jax.ShapeDtypeStruct((8, n_tile_lanes), jnp.int32)],
        scratch_shapes=[pltpu.VMEM((LANES, LANES), F32)],
        compiler_params=_cparams(("arbitrary",)),
        name="slots",
    )(route, cnt)


def _sc_workers():
    sc = plsc.get_sparse_core_info()
    return sc.num_cores, sc.num_cores * sc.num_subcores


def _sc_chunk(tp, n_row_bufs):
    per_worker = tp // _sc_workers()[1]
    limit = min(SC_MAX_INDICES, SC_ROW_BUFFER_BYTES // (n_row_bufs * PACK_WORDS * 4))
    return max(c for c in range(SUBLANES, limit + 1, SUBLANES) if per_worker % c == 0)


def _sc_token_chunks(tp, chunk, body):
    n_cores, n_workers = _sc_workers()
    per_worker = tp // n_workers
    assert per_worker * n_workers == tp and per_worker % chunk == 0 and chunk % SUBLANES == 0
    base = (lax.axis_index("s") * n_cores + lax.axis_index("c")) * per_worker

    @pl.loop(0, per_worker // chunk)
    def _(ci):
        body(pl.multiple_of(base + ci * chunk, SUBLANES))


def _sc_kernel(out_type, chunk, n_row_bufs):
    return functools.partial(
        pl.kernel, mesh=plsc.VectorSubcoreMesh(core_axis_name="c", subcore_axis_name="s"), out_type=out_type,
        scratch_types=[pltpu.VMEM((TOP_K, chunk), jnp.int32), pltpu.VMEM((n_row_bufs, chunk, PACK_WORDS), jnp.uint32),
                       pltpu.SemaphoreType.DMA, pltpu.SemaphoreType.DMA])


def _wait_all(copies):
    for c in copies:
        c.wait()


def _dispatch(dest_flat, u2p, n_slots):
    tp = u2p.shape[0]
    ch = _sc_chunk(tp, 1)

    @_sc_kernel(jax.ShapeDtypeStruct((n_slots, PACK_WORDS), jnp.uint32), ch, 1)
    def scatter_rows(u2_hbm, dest_hbm, xs_hbm, idx_v, rows_v, sem_a, sem_b):
        def chunk(off):
            loads = [pltpu.async_copy(u2_hbm.at[pl.ds(off, ch)], rows_v.at[0], sem_a)]
            loads += [pltpu.async_copy(dest_hbm.at[pl.ds(kk * tp + off, ch)], idx_v.at[kk], sem_a)
                      for kk in range(TOP_K)]
            _wait_all(loads)
            _wait_all([pltpu.async_copy(rows_v.at[0], xs_hbm.at[idx_v.at[kk]], sem_b) for kk in range(TOP_K)])

        _sc_token_chunks(tp, ch, chunk)

    return scatter_rows(u2p, dest_flat)


def _gather_expert_rows(dest_flat, ys):
    tp = dest_flat.shape[0] // TOP_K
    n_cores, n_workers = _sc_workers()
    per_worker = tp // n_workers
    ch = _sc_chunk(tp, GATHER_RING)
    assert per_worker * n_workers == tp and per_worker % ch == 0
    units = [(c, kk) for c in range(per_worker // ch) for kk in range(TOP_K)]

    @functools.partial(
        pl.kernel, mesh=plsc.VectorSubcoreMesh(core_axis_name="c", subcore_axis_name="s"),
        out_type=jax.ShapeDtypeStruct((TOP_K, tp, PACK_WORDS), jnp.uint32),
        scratch_types=[pltpu.VMEM((len(units), ch), jnp.int32), pltpu.VMEM((GATHER_RING, ch, PACK_WORDS), jnp.uint32),
                       pltpu.SemaphoreType.DMA, pltpu.SemaphoreType.DMA((GATHER_RING,)),
                       pltpu.SemaphoreType.DMA((GATHER_RING,))])
    def gather_rows(ys_hbm, dest_hbm, out_hbm, idx_v, rows_v, sem_idx, sem_gather, sem_write):
        base = pl.multiple_of((lax.axis_index("s") * n_cores + lax.axis_index("c")) * per_worker, SUBLANES)
        _wait_all([pltpu.async_copy(dest_hbm.at[pl.ds(kk * tp + base + c * ch, ch)], idx_v.at[u], sem_idx)
                   for u, (c, kk) in enumerate(units)])

        def gather(u):
            slot = u % GATHER_RING
            return pltpu.async_copy(ys_hbm.at[idx_v.at[u]], rows_v.at[slot], sem_gather.at[slot])

        def write(u):
            c, kk = units[u]
            slot = u % GATHER_RING
            return pltpu.async_copy(rows_v.at[slot], out_hbm.at[kk, pl.ds(base + c * ch, ch)], sem_write.at[slot])

        gathers, writes = {}, {}
        for u in range(len(units) + GATHER_LAG):
            if u < len(units):
                if u >= GATHER_RING:
                    writes.pop(u - GATHER_RING).wait()
                gathers[u] = gather(u)
            if u >= GATHER_LAG:
                gathers.pop(u - GATHER_LAG).wait()
                writes[u - GATHER_LAG] = write(u - GATHER_LAG)
        _wait_all(writes.values())

    return gather_rows(ys, dest_flat)


def _expert_kernel(te_ref, nact_ref, valid_ref, x_ref, wgu_hbm, bgu_ref, wdn_hbm, bdn_ref, y_ref,
                   wgu_buf, wdn_buf, slot_ref, sem):
    i = pl.program_id(0)
    n_active = nact_ref[0]
    last_tile = pl.num_programs(0) - 1

    def fetch(expert, slot):
        return (pltpu.make_async_copy(wgu_hbm.at[expert], wgu_buf.at[slot], sem.at[slot, 0]),
                pltpu.make_async_copy(wdn_hbm.at[expert], wdn_buf.at[slot], sem.at[slot, 1]))

    @pl.when(i == 0)
    def _():
        slot_ref[0] = 1
        for copy in fetch(te_ref[0], 0):
            copy.start()

    @pl.when(i < n_active)
    def _():
        expert = te_ref[i]
        first_tile_of_expert = jnp.logical_or(i == 0, expert != te_ref[jnp.maximum(i - 1, 0)])

        @pl.when(first_tile_of_expert)
        def _():
            slot = 1 - slot_ref[0]
            slot_ref[0] = slot
            for copy in fetch(expert, slot):
                copy.wait()
            nxt = lax.while_loop(
                lambda j: jnp.logical_and(j < n_active, te_ref[jnp.minimum(j, last_tile)] == expert),
                lambda j: j + 1, i + 1)

            @pl.when(nxt < n_active)
            def _():
                for copy in fetch(te_ref[jnp.minimum(nxt, last_tile)], 1 - slot):
                    copy.start()

        slot = slot_ref[0]

        def mlp(rows):
            x = _unpack_rows(x_ref[0:rows, :]).astype(BF16)
            gu = jnp.dot(x, wgu_buf[slot].astype(BF16), preferred_element_type=F32) + bgu_ref[0]
            gate = jnp.minimum(gu[:, :D_FF], SWIGLU_LIMIT)
            up = jnp.clip(gu[:, D_FF:], -SWIGLU_LIMIT, SWIGLU_LIMIT)
            hdn = (up + 1.0) * (gate * _sigmoid(SWIGLU_ALPHA * gate))
            y = jnp.dot(hdn.astype(BF16), wdn_buf[slot].astype(BF16), preferred_element_type=F32) + bdn_ref[0]
            y_ref[0:rows, :] = _pack_rows(y)

        path = sum((valid_ref[i] > rows).astype(jnp.int32) for rows in EXPERT_PATH_ROWS[:-1])
        for k, rows in enumerate(EXPERT_PATH_ROWS):
            @pl.when(path == k)
            def _(rows=rows):
                mlp(rows)
                if rows < EXPERT_TILE:
                    y_ref[rows:, :] = jnp.zeros((EXPERT_TILE - rows, PACK_WORDS), y_ref.dtype)

    @pl.when(i >= n_active)
    def _():
        y_ref[...] = jnp.zeros_like(y_ref)


def _expert_mlp(te, nact, valid, xs, w_gu, b_gu, w_dn, b_dn):
    d = D_MODEL
    n_tiles = xs.shape[0] // EXPERT_TILE

    def tile_map(i, te_ref, nact_ref, valid_ref):
        return (jnp.minimum(i, nact_ref[0] - 1), 0)

    def exp_map(i, te_ref, nact_ref, valid_ref):
        return (te_ref[jnp.minimum(i, nact_ref[0] - 1)], 0, 0)

    grid_spec = pltpu.PrefetchScalarGridSpec(
        num_scalar_prefetch=3,
        grid=(n_tiles,),
        in_specs=[pl.BlockSpec((EXPERT_TILE, PACK_WORDS), tile_map),
                  pl.BlockSpec(memory_space=pl.ANY),
                  pl.BlockSpec((1, 1, 2 * D_FF), exp_map),
                  pl.BlockSpec(memory_space=pl.ANY),
                  pl.BlockSpec((1, 1, d), exp_map)],
        out_specs=pl.BlockSpec((EXPERT_TILE, PACK_WORDS), lambda i, te_ref, nact_ref, valid_ref: (i, 0)),
        scratch_shapes=[pltpu.VMEM((2, d, 2 * D_FF), F32), pltpu.VMEM((2, D_FF, d), F32),
                        pltpu.SMEM((1,), jnp.int32), pltpu.SemaphoreType.DMA((2, 2))],
    )
    return pl.pallas_call(
        _expert_kernel,
        grid_spec=grid_spec,
        out_shape=jax.ShapeDtypeStruct(xs.shape, jnp.uint32),
        compiler_params=_cparams(("arbitrary",)),
        name="expert_mlp",
    )(te, nact, valid, xs, w_gu, b_gu, w_dn, b_dn)


def _combine_kernel(gfin_ref, *refs):
    out_ref = refs[-1]
    for s in range(COMBINE_SUB):
        h1_ref, info_ref, yg_ref = refs[3 * s:3 * s + 3]
        info = info_ref[...]
        h2 = h1_ref[...]
        for kk in range(TOP_K):
            h2 = h2 + info[:, kk:kk + 1] * _unpack_rows(yg_ref[kk])
        ms = jnp.mean(h2 * h2, axis=-1, keepdims=True)
        out_ref[s * CHUNK:(s + 1) * CHUNK, :] = h2 * lax.rsqrt(ms + RMS_EPS) * gfin_ref[...]


def _combine(h1, info, g_final, yg, n_batch, seq):
    d = h1.shape[1]
    chunks = seq // CHUNK

    def padded(s):
        def index(j):
            c = j * COMBINE_SUB + s
            return (c // chunks) * (chunks + 1) + c % chunks + 1
        return index

    in_specs = [pl.BlockSpec((1, d), lambda j: (0, 0))]
    operands = [g_final]
    for s in range(COMBINE_SUB):
        chunk_of = padded(s)
        in_specs += [pl.BlockSpec((CHUNK, d), lambda j, f=chunk_of: (f(j), 0)),
                     pl.BlockSpec((CHUNK, LANES), lambda j, f=chunk_of: (f(j), 0)),
                     pl.BlockSpec((TOP_K, CHUNK, PACK_WORDS), lambda j, f=chunk_of: (0, f(j), 0))]
        operands += [h1, info, yg]
    return pl.pallas_call(
        _combine_kernel,
        grid=(n_batch * chunks // COMBINE_SUB,),
        in_specs=in_specs,
        out_specs=pl.BlockSpec((COMBINE_SUB * CHUNK, d), lambda j: (j, 0)),
        out_shape=jax.ShapeDtypeStruct((n_batch * seq, d), F32),
        compiler_params=_cparams(("parallel",)),
        name="combine",
    )(*operands)


def kernel(x, meta_tokens, lb_logits, g_mix, w_in, w_dw, b_dw, ln_g, ln_b, w_conv_out, b_conv_out,
           g_onorm, w_rnn_out, w_o, g_ffn, w_router, b_router, w_gate_up, b_gate_up, w_down, b_down,
           g_final):
    n_batch, seq, d = x.shape
    assert d == D_MODEL and w_in.shape[0] == 1, "single-layer block with D_MODEL features"
    assert seq % (CHUNK * COMBINE_SUB) == 0
    seq_pad = CHUNK + seq
    tp = n_batch * seq_pad
    assert tp % ROW_TILE == 0

    x2 = x.reshape(n_batch * seq, d)
    meta = meta_tokens.astype(x.dtype)
    vec = lambda a: a.reshape(1, -1).astype(F32)

    acts, logf = _in_proj(x2, meta, seq, vec(g_mix[0]), lb_logits.astype(F32), w_in[0])
    w_taps = w_dw[0].astype(F32).reshape(CONV_WIDTH, d // LANES, LANES).transpose(1, 0, 2)
    ap = _conv_branch(acts, w_taps, vec(b_dw[0]), vec(ln_g[0]), vec(ln_b[0]),
                      w_conv_out[0], vec(b_conv_out[0]))
    on = _hgrn2_scan(acts, logf, vec(g_onorm[0]), n_batch)
    w_router_pad = jnp.pad(w_router[0].astype(F32), ((0, 0), (0, LANES - N_EXPERTS)))
    b_router_pad = jnp.pad(vec(b_router[0]), ((0, 0), (0, LANES - N_EXPERTS)))
    h1, u2p, info, route, cnt = _merge_route(x2, meta, seq, on, ap, acts, w_rnn_out[0], w_o[0],
                                      vec(g_ffn[0]), w_router_pad, b_router_pad)

    n_tiles_max = -(-(tp * TOP_K + N_EXPERTS * (EXPERT_TILE - 1)) // EXPERT_TILE)
    dest, te = _slots(route, cnt, n_tiles_max)
    dest_flat = dest[:TOP_K].reshape(-1)
    xs = _dispatch(dest_flat, u2p, n_tiles_max * EXPERT_TILE)
    ys = _expert_mlp(te[0, :n_tiles_max], te[1, :1], te[2, :n_tiles_max], xs, w_gate_up[0],
                     b_gate_up[0].reshape(N_EXPERTS, 1, -1).astype(F32), w_down[0],
                     b_down[0].reshape(N_EXPERTS, 1, -1).astype(F32))
    yg = _gather_expert_rows(dest_flat, ys)
    out = _combine(h1, info, vec(g_final), yg, n_batch, seq)
    return out.reshape(n_batch, seq, d)
```

```python
import functools

import jax
import jax.numpy as jnp
from jax import lax
from jax.experimental import pallas as pl
from jax.experimental.pallas import tpu as pltpu
from jax.experimental.pallas import tpu_sc as plsc

F32 = jnp.float32
BF16 = jnp.bfloat16

D_MODEL = 1024
N_META = 16
CHUNK = 128
CHUNK_PAD = CHUNK - N_META
CONV_WIDTH = 31
HEAD_DIM = 128
N_HEADS = D_MODEL // HEAD_DIM
N_EXPERTS = 32
TOP_K = 4
D_FF = D_MODEL
SWIGLU_LIMIT = 7.0
SWIGLU_ALPHA = 1.702
RMS_EPS = 1e-6
LN_EPS = 1e-5

ACT_AGLU, ACT_Q, ACT_K, ACT_V, ACT_OG, ACT_SGA, ACT_SGB = range(7)
N_ACTS = 7

LANES = 128
SUBLANES = 8
SUB_BLOCK = 32
N_SUB = CHUNK // SUB_BLOCK
HALO = 32
ROW_TILE = 640
IN_TILE = 320
SCAN_CHUNKS = 5
MERGE_PARTS = 2
SLOTS_STEPS = 5
ROUTE_ROWS = 16
CONV_ROWS = 64
CONV_SUMS = 4
EXPERT_TILE = 1024
EXPERT_PATH_ROWS = tuple(EXPERT_TILE >> k for k in (3, 2, 1, 0))
SC_MAX_INDICES = 128
SC_ROW_BUFFER_BYTES = 384 * 1024
SC_RING = 4
GATHER_LAG = 2
COMBINE_SUB = 8
NEG_BIG = -1e30
VMEM_LIMIT = 56 * 1024 * 1024


def _sigmoid(x):
    return 1.0 / (1.0 + jnp.exp(-x))


def _cparams(sem):
    return pltpu.CompilerParams(dimension_semantics=sem, vmem_limit_bytes=VMEM_LIMIT)


def _const_spec(shape):
    nd = len(shape)
    return pl.BlockSpec(shape, lambda *_: (0,) * nd)


PACK_WORDS = D_MODEL // 2
HIGH_HALF = 0xFFFF0000


def _pack_rows(x):
    lo = lax.bitcast_convert_type(x[:, :PACK_WORDS].astype(BF16).astype(F32), jnp.uint32)
    hi = lax.bitcast_convert_type(x[:, PACK_WORDS:].astype(BF16).astype(F32), jnp.uint32)
    return (lo >> 16) | (hi & jnp.uint32(HIGH_HALF))


def _unpack_rows(w):
    lo = lax.bitcast_convert_type(w << 16, F32)
    hi = lax.bitcast_convert_type(w & jnp.uint32(HIGH_HALF), F32)
    return jnp.concatenate([lo, hi], axis=1)


def _residual_tile(x_ref, meta_ref, is_first):
    x = x_ref[...]
    tm, d = x.shape
    prefix = jnp.concatenate([jnp.zeros((CHUNK_PAD, d), x.dtype), meta_ref[...]], axis=0)
    first = jnp.concatenate([prefix, x[:tm - CHUNK, :]], axis=0)
    return jnp.where(is_first, first, x)


def _residual_spec(tm, d, seq, tiles_per_batch, n_tiles):
    def start(i):
        i = jnp.minimum(i, n_tiles - 1)
        b, t = i // tiles_per_batch, i % tiles_per_batch
        return (pl.multiple_of(b * seq + jnp.maximum(t * tm - CHUNK, 0), SUBLANES), 0)
    return pl.BlockSpec((pl.Element(tm), pl.Element(d)), start)


def _in_proj_kernel(x_ref, meta_ref, g_ref, lbl_ref, w_ref, acts_ref, logf_ref, *, tiles_per_batch):
    d = D_MODEL

    def put(group, value):
        acts_ref[:, group * d:(group + 1) * d] = value.astype(acts_ref.dtype)

    h = _residual_tile(x_ref, meta_ref, pl.program_id(0) % tiles_per_batch == 0)
    ms = jnp.mean(h * h, axis=-1, keepdims=True)
    u = (h * lax.rsqrt(ms + RMS_EPS) * g_ref[...]).astype(BF16)

    def proj(j):
        return jnp.dot(u, w_ref[:, j * d:(j + 1) * d].astype(BF16), preferred_element_type=F32)

    put(ACT_AGLU, proj(0) * _sigmoid(proj(1)))
    zq = proj(2)
    put(ACT_Q, zq * _sigmoid(zq))
    lbl = lbl_ref[...]
    e = jnp.exp(lbl - jnp.max(lbl, axis=0, keepdims=True))
    lb = e[0:1, :] / jnp.sum(e, axis=0, keepdims=True)
    s = _sigmoid(proj(3))
    logf_ref[...] = jnp.log(lb + (1.0 - lb) * s)
    put(ACT_K, (1.0 - lb) * (1.0 - s))
    zg = proj(5)
    put(ACT_OG, zg * _sigmoid(zg))
    put(ACT_SGA, _sigmoid(proj(6)))
    put(ACT_SGB, _sigmoid(proj(7)))
    put(ACT_V, proj(4))


def _in_proj(x2, meta, seq, g_mix, lb_logits, w_in):
    d = x2.shape[1]
    n_batch = x2.shape[0] // seq
    tm = IN_TILE
    tpb = (seq + CHUNK) // tm
    tp = n_batch * (seq + CHUNK)
    assert tpb * tm == seq + CHUNK and tm > CHUNK
    row = pl.BlockSpec((tm, d), lambda i: (i, 0))
    return pl.pallas_call(
        functools.partial(_in_proj_kernel, tiles_per_batch=tpb),
        grid=(tp // tm,),
        in_specs=[_residual_spec(tm, d, seq, tpb, tp // tm), _const_spec(meta.shape), _const_spec((1, d)),
                  _const_spec(lb_logits.shape),
                  pl.BlockSpec(w_in.shape, lambda i: (0, 0), pipeline_mode=pl.Buffered(1))],
        out_specs=[pl.BlockSpec((tm, N_ACTS * d), lambda i: (i, 0)), row],
        out_shape=[jax.ShapeDtypeStruct((tp, N_ACTS * d), BF16), jax.ShapeDtypeStruct((tp, d), F32)],
        compiler_params=_cparams(("parallel",)),
        name="in_proj",
    )(x2, meta, g_mix, lb_logits, w_in)


def _conv_kernel(halo_ref, cur_ref, wdw_ref, bdw_ref, lng_ref, lnb_ref, wout_ref, bout_ref, sga_ref,
                 out_ref, win_ref, acc_ref, shift_ref):
    tm = cur_ref.shape[0]
    win_ref[0:HALO, :] = halo_ref[...].astype(F32)
    win_ref[HALO:, :] = cur_ref[...].astype(F32)
    first_tap = HALO - (CONV_WIDTH - 1)
    for c in range(D_MODEL // LANES):
        lanes = slice(c * LANES, (c + 1) * LANES)
        n_rows = tm + HALO - SUBLANES
        for s in range(1, SUBLANES):
            shift_ref[s - 1, 0:n_rows, :] = win_ref[pl.ds(s, n_rows), lanes]
        bias = jnp.broadcast_to(bdw_ref[:, lanes], (CONV_ROWS, LANES))
        groups = CONV_ROWS // SUBLANES

        def chunk(r, carry, c=c, lanes=lanes, bias=bias):
            r0 = pl.multiple_of(r * CONV_ROWS, CONV_ROWS)
            accs = [bias.reshape(groups, SUBLANES, LANES)] + [None] * (CONV_SUMS - 1)
            for j in range(CONV_WIDTH):
                off = first_tap + j
                rows = pl.ds(r0 + off - off % SUBLANES, CONV_ROWS)
                w_j = wdw_ref[c, pl.ds(j, SUBLANES, stride=0), :]
                window = shift_ref[off % SUBLANES - 1, rows, :] if off % SUBLANES else win_ref[rows, lanes]
                term = w_j[None] * window.reshape(groups, SUBLANES, LANES)
                accs[j % CONV_SUMS] = term if accs[j % CONV_SUMS] is None else accs[j % CONV_SUMS] + term
            while len(accs) > 1:
                accs = [a + b for a, b in zip(accs[0::2], accs[1::2])]
            acc_ref[pl.ds(r0, CONV_ROWS), lanes] = accs[0].reshape(CONV_ROWS, LANES)
            return carry

        lax.fori_loop(0, tm // CONV_ROWS, chunk, 0)
    a = acc_ref[...]
    mu = jnp.mean(a, axis=-1, keepdims=True)
    ac = a - mu
    var = jnp.mean(ac * ac, axis=-1, keepdims=True)
    y = ac * lax.rsqrt(var + LN_EPS) * lng_ref[...] + lnb_ref[...]
    y = y * _sigmoid(y)
    o = jnp.dot(y.astype(BF16), wout_ref[...].astype(BF16), preferred_element_type=F32) + bout_ref[...]
    out_ref[...] = (sga_ref[...].astype(F32) * o).astype(out_ref.dtype)


def _conv_branch(acts, w_dw, b_dw, ln_g, ln_b, w_out, b_out):
    tp, d = acts.shape[0], D_MODEL
    tm = ROW_TILE
    per = tm // HALO
    row = pl.BlockSpec((tm, d), lambda i: (i, 0))
    glu = pl.BlockSpec((tm, d), lambda i: (i, ACT_AGLU))
    gate = pl.BlockSpec((tm, d), lambda i: (i, ACT_SGA))
    halo = pl.BlockSpec((HALO, d), lambda i: (jnp.maximum(i * per - 1, 0), ACT_AGLU))
    vec = _const_spec((1, d))
    return pl.pallas_call(
        _conv_kernel,
        grid=(tp // tm,),
        in_specs=[halo, glu, _const_spec(w_dw.shape), vec, vec, vec, _const_spec((d, d)), vec, gate],
        out_specs=row,
        out_shape=jax.ShapeDtypeStruct((tp, d), BF16),
        scratch_shapes=[pltpu.VMEM((tm + HALO, d), F32), pltpu.VMEM((tm, d), F32),
                        pltpu.VMEM((SUBLANES - 1, tm + HALO - SUBLANES, LANES), F32)],
        compiler_params=_cparams(("parallel",)),
        name="conv_branch",
    )(acts, acts, w_dw, b_dw, ln_g, ln_b, w_out, b_out, acts)


def _split_bf16(x):
    hi = x.astype(BF16)
    return hi, (x - hi.astype(F32)).astype(BF16)


def _nt_dot(a, b):
    return lax.dot_general(a, b, (((1,), (1,)), ((), ())), preferred_element_type=F32)


def _tn_dot(a, b):
    return lax.dot_general(a, b, (((0,), (0,)), ((), ())), preferred_element_type=F32)


def _scan_kernel(q_ref, k_ref, v_ref, lf_ref, og_ref, gon_ref, o_ref, st_ref):
    @pl.when(pl.program_id(1) == 0)
    def _():
        st_ref[...] = jnp.zeros_like(st_ref)

    c = CHUNK
    row = lax.broadcasted_iota(jnp.int32, (c, c), 0)
    col = lax.broadcasted_iota(jnp.int32, (c, c), 1)
    causal = col <= row
    diag_mask = jnp.logical_and(causal, row // SUB_BLOCK == col // SUB_BLOCK)
    tri = causal.astype(BF16)

    def bcast_rows(rows):
        return jnp.concatenate([jnp.broadcast_to(r, (SUB_BLOCK, HEAD_DIM)) for r in rows], axis=0)

    heads = [slice(h * HEAD_DIM, (h + 1) * HEAD_DIM) for h in range(N_HEADS)]
    chunks = [slice(ci * c, (ci + 1) * c) for ci in range(SCAN_CHUNKS)]

    bcums = []
    for cs in chunks:
        lf = lf_ref[cs, :]
        lf_hi = lf.astype(BF16)
        lf_mid, lf_lo = _split_bf16(lf - lf_hi.astype(F32))
        bcums.append(jnp.dot(tri, lf_hi, preferred_element_type=F32)
                     + (jnp.dot(tri, lf_mid, preferred_element_type=F32)
                        + jnp.dot(tri, lf_lo, preferred_element_type=F32)))
    operands = {}
    for ci, cs in enumerate(chunks):
        for h, hs in enumerate(heads):
            b = bcums[ci][:, hs]
            q = q_ref[cs, hs].astype(F32)
            k = k_ref[cs, hs].astype(F32)
            ends = [b[i * SUB_BLOCK + SUB_BLOCK - 1:i * SUB_BLOCK + SUB_BLOCK, :] for i in range(N_SUB)]
            mids = [b[i * SUB_BLOCK + SUB_BLOCK // 2 - 1:i * SUB_BLOCK + SUB_BLOCK // 2, :] for i in range(N_SUB)]
            mid_full = bcast_rows(mids)
            b_last = ends[-1]
            zero_row = jnp.zeros_like(b_last)
            qm = q * jnp.exp(b - mid_full)
            km = k * jnp.exp(mid_full - b)
            q_parts, k_parts = [], []
            for j in range(N_SUB - 1):
                q_rows = [jnp.exp(mids[i] - ends[j]) if i > j else zero_row for i in range(N_SUB)]
                k_rows = [jnp.exp(ends[j] - mids[j]) if i == j else zero_row for i in range(N_SUB)]
                q_parts.append((qm * bcast_rows(q_rows)).astype(BF16))
                k_parts.append((km * bcast_rows(k_rows)).astype(BF16))
            q_in = (qm * bcast_rows([jnp.exp(m) for m in mids])).astype(BF16)
            k_out = (km * bcast_rows([jnp.exp(b_last - m) for m in mids])).astype(BF16)
            operands[ci, h] = (qm.astype(BF16), km.astype(BF16), jnp.concatenate(q_parts, axis=1),
                               jnp.concatenate(k_parts, axis=1), q_in, k_out, jnp.exp(b_last))
    products = {}
    for ci, cs in enumerate(chunks):
        for h, hs in enumerate(heads):
            qm, km, q_cat, k_cat, _, k_out, _ = operands[ci, h]
            products[ci, h] = (_nt_dot(qm, km), _nt_dot(q_cat, k_cat), _tn_dot(v_ref[cs, hs], k_out))
    within = {}
    for ci, cs in enumerate(chunks):
        for h, hs in enumerate(heads):
            same_block, earlier_blocks, _ = products[ci, h]
            scores = jnp.where(diag_mask, same_block, 0.0) + earlier_blocks
            within[ci, h] = jnp.dot(scores.astype(BF16), v_ref[cs, hs], preferred_element_type=F32)
    states = [st_ref[h] for h in range(N_HEADS)]
    outs = {}
    for ci in range(SCAN_CHUNKS):
        for h in range(N_HEADS):
            outs[ci, h] = within[ci, h] + _nt_dot(operands[ci, h][4], states[h].astype(BF16))
            states[h] = states[h] * operands[ci, h][6] + products[ci, h][2]
    for h in range(N_HEADS):
        st_ref[h] = states[h]
    for ci, cs in enumerate(chunks):
        for h, hs in enumerate(heads):
            o = outs[ci, h]
            ms = jnp.mean(o * o, axis=-1, keepdims=True)
            on = o * lax.rsqrt(ms + RMS_EPS) * gon_ref[:, hs]
            o_ref[cs, hs] = (on * og_ref[cs, hs].astype(F32)).astype(o_ref.dtype)


def _hgrn2_scan(acts, logf, g_onorm, n_batch):
    tp, d = logf.shape
    rows = SCAN_CHUNKS * CHUNK
    n_chunks = tp // n_batch // rows
    assert n_chunks * rows * n_batch == tp
    col = lambda group: pl.BlockSpec((rows, d), lambda b, c: (b * n_chunks + c, group))
    blk = col(0)
    return pl.pallas_call(
        _scan_kernel,
        grid=(n_batch, n_chunks),
        in_specs=[col(ACT_Q), col(ACT_K), col(ACT_V), blk, col(ACT_OG), pl.BlockSpec((1, d), lambda b, c: (0, 0))],
        out_specs=blk,
        out_shape=jax.ShapeDtypeStruct((tp, d), BF16),
        scratch_shapes=[pltpu.VMEM((N_HEADS, HEAD_DIM, HEAD_DIM), F32)],
        compiler_params=_cparams(("arbitrary", "arbitrary")),
        name="hgrn2_scan",
    )(acts, acts, acts, logf, acts, g_onorm)


def _merge_route_kernel(x_ref, meta_ref, on_ref, ap_ref, sgb_ref, wrnn_ref, wo_ref, gffn_ref, wr_ref, br_ref,
                        h1_ref, u2_ref, info_ref, route_ref, cnt_ref, tri_ref, carry_ref, colcnt_ref, logits_ref, hres_ref,
                        *, tiles_per_batch, n_tiles):
    tm = on_ref.shape[0]
    i = pl.program_id(0)
    hres_ref[...] = _residual_tile(x_ref, meta_ref, jnp.minimum(i, n_tiles - 1) % tiles_per_batch == 0)

    @pl.when(i == 0)
    def _():
        r_i = lax.broadcasted_iota(jnp.int32, (tm, tm), 0)
        c_i = lax.broadcasted_iota(jnp.int32, (tm, tm), 1)
        tri_ref[...] = (c_i < r_i).astype(BF16)
        carry_ref[...] = jnp.zeros_like(carry_ref)
        colcnt_ref[...] = jnp.zeros_like(colcnt_ref)
        logits_ref[...] = jnp.zeros_like(logits_ref)

    routed = i > 0
    lane = lax.broadcasted_iota(jnp.int32, (tm, LANES), 1)
    cur = jnp.where(lane < N_EXPERTS, logits_ref[...], NEG_BIG)
    vals, idxs, sels = [], [], []

    def topk_round(cur):
        m = jnp.max(cur, axis=-1, keepdims=True)
        idx = jnp.min(jnp.where(cur == m, lane, LANES), axis=-1, keepdims=True)
        sel = lane == idx
        vals.append(m)
        idxs.append(idx)
        sels.append(sel)
        return jnp.where(sel, 2.0 * NEG_BIG, cur)

    w_hi, w_lo = _split_bf16(wr_ref[...])
    w_hi_lo = jnp.concatenate([w_hi, w_lo], axis=1)
    part = tm // MERGE_PARTS
    rows = [slice(p * part, (p + 1) * part) for p in range(MERGE_PARTS)]
    w_rnn = wrnn_ref[...].astype(BF16)
    w_out = wo_ref[...].astype(BF16)
    rs = [jnp.dot(on_ref[rw, :], w_rnn, preferred_element_type=F32) for rw in rows]
    cur = topk_round(cur)
    ys = [(ap_ref[rw, :].astype(F32) + sgb_ref[rw, :].astype(F32) * r).astype(BF16) for rw, r in zip(rows, rs)]
    h1s = [hres_ref[rw, :] + jnp.dot(y, w_out, preferred_element_type=F32) for rw, y in zip(rows, ys)]
    cur = topk_round(cur)
    u2s = []
    for rw, h1 in zip(rows, h1s):
        h1_ref[rw, :] = h1
        ms = jnp.mean(h1 * h1, axis=-1, keepdims=True)
        u2 = h1 * lax.rsqrt(ms + RMS_EPS) * gffn_ref[...]
        u2_ref[rw, :] = _pack_rows(u2)
        u2s.append(u2)
    cur = topk_round(cur)
    new_logits = []
    for u2 in u2s:
        u_hi, u_lo = _split_bf16(u2)
        both = jnp.dot(u_hi, w_hi_lo, preferred_element_type=F32)
        new_logits.append(both[:, :LANES] + (jnp.dot(u_lo, w_hi, preferred_element_type=F32) + both[:, LANES:])
                          + br_ref[...])
    cur = topk_round(cur)
    assert len(vals) == TOP_K
    exps = [jnp.exp(vk - vals[0]) for vk in vals]
    den = exps[0] + exps[1] + exps[2] + exps[3]
    onehot = jnp.logical_or(jnp.logical_or(sels[0], sels[1]), jnp.logical_or(sels[2], sels[3]))
    onehot = jnp.logical_and(onehot, routed)
    onehot_bf = onehot.astype(BF16)
    rank_all = jnp.dot(tri_ref[...], onehot_bf, preferred_element_type=F32) + carry_ref[...]
    carry_ref[...] += jnp.sum(onehot.astype(F32), axis=0, keepdims=True)
    colcnt_ref[...] += _tn_dot(onehot_bf, jnp.ones((tm, LANES), BF16))
    info = jnp.zeros((tm, LANES), F32)
    for kk in range(TOP_K):
        rank_k = jnp.sum(jnp.where(sels[kk], rank_all, 0.0), axis=-1, keepdims=True)
        info = jnp.where(lane == kk, exps[kk] / den, info)
        info = jnp.where(lane == TOP_K + kk, idxs[kk].astype(F32), info)
        info = jnp.where(lane == 2 * TOP_K + kk, rank_k, info)
    info_ref[...] = info
    route_ref[...] = info.T[0:ROUTE_ROWS, :]
    cnt_ref[...] = colcnt_ref[...]
    logits_ref[...] = jnp.concatenate(new_logits, axis=0)


def _merge_route(x2, meta, seq, on, ap, acts, w_rnn, w_o, g_ffn, w_router_pad, b_router_pad):
    tp, d = on.shape
    tm = ROW_TILE
    n_tiles = tp // tm
    tpb = (seq + CHUNK) // tm
    assert tpb * tm == seq + CHUNK
    this_tile = lambda i: (jnp.minimum(i, n_tiles - 1), 0)
    prev_tile = lambda i: (jnp.maximum(i - 1, 0), 0)
    row = pl.BlockSpec((tm, d), this_tile)
    vec = _const_spec((1, d))
    return pl.pallas_call(
        functools.partial(_merge_route_kernel, tiles_per_batch=tpb, n_tiles=n_tiles),
        grid=(n_tiles + 1,),
        in_specs=[_residual_spec(tm, d, seq, tpb, n_tiles), _const_spec(meta.shape), row, row,
                  pl.BlockSpec((tm, d), lambda i: (jnp.minimum(i, n_tiles - 1), ACT_SGB)),
                  _const_spec((d, d)), _const_spec((d, d)), vec, _const_spec((d, LANES)), _const_spec((1, LANES))],
        out_specs=[row, pl.BlockSpec((tm, PACK_WORDS), this_tile),
                   pl.BlockSpec((tm, LANES), prev_tile),
                   pl.BlockSpec((ROUTE_ROWS, tm), lambda i: (0, jnp.maximum(i - 1, 0))), _const_spec((LANES, LANES))],
        out_shape=[jax.ShapeDtypeStruct((tp, d), F32), jax.ShapeDtypeStruct((tp, PACK_WORDS), jnp.uint32),
                   jax.ShapeDtypeStruct((tp, LANES), F32), jax.ShapeDtypeStruct((ROUTE_ROWS, tp), F32),
                   jax.ShapeDtypeStruct((LANES, LANES), F32)],
        scratch_shapes=[pltpu.VMEM((tm, tm), BF16), pltpu.VMEM((1, LANES), F32),
                        pltpu.VMEM((LANES, LANES), F32), pltpu.VMEM((tm, LANES), F32), pltpu.VMEM((tm, d), F32)],
        compiler_params=_cparams(("arbitrary",)),
        name="merge_route",
    )(x2, meta, on, ap, acts, w_rnn, w_o, g_ffn, w_router_pad, b_router_pad)


def _slots_kernel(route_ref, cnt_ref, dest_ref, te_ref, pstart_ref):
    tl = route_ref.shape[1]
    n_tile_lanes = te_ref.shape[1]

    @pl.when(pl.program_id(0) == 0)
    def _():
        r_i = lax.broadcasted_iota(jnp.int32, (LANES, LANES), 0)
        c_i = lax.broadcasted_iota(jnp.int32, (LANES, LANES), 1)
        cnt = cnt_ref[...]
        tiles = jnp.floor((cnt + (EXPERT_TILE - 1)) / EXPERT_TILE)
        pend_col = jnp.dot((c_i <= r_i).astype(F32), tiles, preferred_element_type=F32,
                           precision=lax.Precision.HIGHEST)
        pstart_ref[...] = pend_col - tiles
        tile_id = lax.broadcasted_iota(jnp.int32, (LANES, n_tile_lanes), 1).astype(F32)
        exp_id = lax.broadcasted_iota(jnp.int32, (LANES, n_tile_lanes), 0)
        pend_wide = jnp.concatenate([pend_col] * (n_tile_lanes // LANES), axis=1)
        below = jnp.logical_and(pend_wide <= tile_id, exp_id < N_EXPERTS)
        te = jnp.minimum(jnp.sum(below.astype(F32), axis=0, keepdims=True), N_EXPERTS - 1.0)
        n_active = pend_col[N_EXPERTS - 1:N_EXPERTS, 0:1]
        owner = exp_id.astype(F32) == te
        cnt_wide = jnp.concatenate([cnt] * (n_tile_lanes // LANES), axis=1)
        first_tile = pend_wide - jnp.concatenate([tiles] * (n_tile_lanes // LANES), axis=1)
        left = jnp.sum(jnp.where(owner, cnt_wide - (tile_id - first_tile) * EXPERT_TILE, 0.0), axis=0, keepdims=True)
        valid = jnp.clip(left, 0.0, float(EXPERT_TILE))
        sub_t = lax.broadcasted_iota(jnp.int32, (SUBLANES, n_tile_lanes), 0)
        out = jnp.where(sub_t == 0, jnp.broadcast_to(te, (SUBLANES, n_tile_lanes)),
                        jnp.where(sub_t == 1, jnp.broadcast_to(n_active, (SUBLANES, n_tile_lanes)),
                                  jnp.broadcast_to(valid, (SUBLANES, n_tile_lanes))))
        te_ref[...] = out.astype(jnp.int32)

    route = route_ref[...]
    expert = lax.broadcasted_iota(jnp.int32, (N_EXPERTS, tl), 0)
    first_slot = jnp.concatenate([pstart_ref[0:N_EXPERTS, :]] * (tl // LANES), axis=1) * EXPERT_TILE
    rows = []
    for kk in range(TOP_K):
        idx_k = route[TOP_K + kk:TOP_K + kk + 1, :].astype(jnp.int32)
        start_k = jnp.sum(jnp.where(expert == idx_k, first_slot, 0.0), axis=0, keepdims=True)
        rows.append(start_k + route[2 * TOP_K + kk:2 * TOP_K + kk + 1, :])
    rows.append(jnp.zeros((SUBLANES - TOP_K, tl), F32))
    dest_ref[...] = jnp.concatenate(rows, axis=0).astype(jnp.int32)


def _slots(route, cnt, n_tiles_max):
    tp = route.shape[1]
    tl = tp // SLOTS_STEPS
    assert tl * SLOTS_STEPS == tp and tl % LANES == 0
    n_tile_lanes = -(-n_tiles_max // LANES) * LANES
    return pl.pallas_call(
        _slots_kernel,
        grid=(SLOTS_STEPS,),
        in_specs=[pl.BlockSpec((ROUTE_ROWS, tl), lambda i: (0, i)), _const_spec((LANES, LANES))],
        out_specs=[pl.BlockSpec((SUBLANES, tl), lambda i: (0, i)), _const_spec((8, n_tile_lanes))],
        out_shape=[jax.ShapeDtypeStruct((SUBLANES, tp), jnp.int32),
                   jax.ShapeDtypeStruct((8, n_tile_lanes), jnp.int32)],
        scratch_shapes=[pltpu.VMEM((LANES, LANES), F32)],
        compiler_params=_cparams(("arbitrary",)),
        name="slots",
    )(route, cnt)


def _sc_workers():
    sc = plsc.get_sparse_core_info()
    return sc.num_cores, sc.num_cores * sc.num_subcores


def _sc_split(tp):
    n_workers = _sc_workers()[1]
    per_worker = tp // n_workers
    assert per_worker * n_workers == tp
    limit = min(SC_MAX_INDICES, SC_ROW_BUFFER_BYTES // (SC_RING * PACK_WORDS * 4))
    return per_worker, max(c for c in range(SUBLANES, limit + 1, SUBLANES) if per_worker % c == 0)


def _sc_first_token(per_worker):
    return pl.multiple_of((lax.axis_index("s") * _sc_workers()[0] + lax.axis_index("c")) * per_worker, SUBLANES)


def _sc_ring_kernel(out_type, n_index_rows, chunk):
    return functools.partial(
        pl.kernel, mesh=plsc.VectorSubcoreMesh(core_axis_name="c", subcore_axis_name="s"), out_type=out_type,
        scratch_types=[pltpu.VMEM((n_index_rows, chunk), jnp.int32), pltpu.VMEM((SC_RING, chunk, PACK_WORDS), jnp.uint32),
                       pltpu.SemaphoreType.DMA, pltpu.SemaphoreType.DMA((SC_RING,)), pltpu.SemaphoreType.DMA((SC_RING,))])


def _wait_all(copies):
    for c in copies:
        c.wait()


def _dispatch(dest_flat, u2p, n_slots):
    tp = u2p.shape[0]
    per_worker, ch = _sc_split(tp)
    n_chunks = per_worker // ch

    @_sc_ring_kernel(jax.ShapeDtypeStruct((n_slots, PACK_WORDS), jnp.uint32), n_chunks * TOP_K, ch)
    def scatter_rows(u2_hbm, dest_hbm, xs_hbm, idx_v, rows_v, sem_idx, sem_load, sem_scatter):
        base = _sc_first_token(per_worker)
        _wait_all([pltpu.async_copy(dest_hbm.at[pl.ds(kk * tp + base + c * ch, ch)], idx_v.at[c * TOP_K + kk], sem_idx)
                   for c in range(n_chunks) for kk in range(TOP_K)])

        def load(c):
            slot = c % SC_RING
            return pltpu.async_copy(u2_hbm.at[pl.ds(base + c * ch, ch)], rows_v.at[slot], sem_load.at[slot])

        def scatter(c):
            slot = c % SC_RING
            return [pltpu.async_copy(rows_v.at[slot], xs_hbm.at[idx_v.at[c * TOP_K + kk]], sem_scatter.at[slot])
                    for kk in range(TOP_K)]

        loads, scatters = {}, {}
        for c in range(n_chunks + 1):
            if c < n_chunks:
                if c >= SC_RING:
                    _wait_all(scatters.pop(c - SC_RING))
                loads[c] = load(c)
            if c >= 1:
                loads.pop(c - 1).wait()
                scatters[c - 1] = scatter(c - 1)
        for copies in scatters.values():
            _wait_all(copies)

    return scatter_rows(u2p, dest_flat)


def _gather_expert_rows(dest_flat, ys):
    tp = dest_flat.shape[0] // TOP_K
    per_worker, ch = _sc_split(tp)
    units = [(c, kk) for c in range(per_worker // ch) for kk in range(TOP_K)]

    @_sc_ring_kernel(jax.ShapeDtypeStruct((TOP_K, tp, PACK_WORDS), jnp.uint32), len(units), ch)
    def gather_rows(ys_hbm, dest_hbm, out_hbm, idx_v, rows_v, sem_idx, sem_gather, sem_write):
        base = _sc_first_token(per_worker)
        _wait_all([pltpu.async_copy(dest_hbm.at[pl.ds(kk * tp + base + c * ch, ch)], idx_v.at[u], sem_idx)
                   for u, (c, kk) in enumerate(units)])

        def gather(u):
            slot = u % SC_RING
            return pltpu.async_copy(ys_hbm.at[idx_v.at[u]], rows_v.at[slot], sem_gather.at[slot])

        def write(u):
            c, kk = units[u]
            slot = u % SC_RING
            return pltpu.async_copy(rows_v.at[slot], out_hbm.at[kk, pl.ds(base + c * ch, ch)], sem_write.at[slot])

        gathers, writes = {}, {}
        for u in range(len(units) + GATHER_LAG):
            if u < len(units):
                if u >= SC_RING:
                    writes.pop(u - SC_RING).wait()
                gathers[u] = gather(u)
            if u >= GATHER_LAG:
                gathers.pop(u - GATHER_LAG).wait()
                writes[u - GATHER_LAG] = write(u - GATHER_LAG)
        _wait_all(writes.values())

    return gather_rows(ys, dest_flat)


def _expert_kernel(te_ref, nact_ref, valid_ref, x_ref, wgu_hbm, bgu_ref, wdn_hbm, bdn_ref, y_ref,
                   wgu_buf, wdn_buf, slot_ref, sem):
    i = pl.program_id(0)
    n_active = nact_ref[0]
    last_tile = pl.num_programs(0) - 1

    def fetch(expert, slot):
        return (pltpu.make_async_copy(wgu_hbm.at[expert], wgu_buf.at[slot], sem.at[slot, 0]),
                pltpu.make_async_copy(wdn_hbm.at[expert], wdn_buf.at[slot], sem.at[slot, 1]))

    @pl.when(i == 0)
    def _():
        slot_ref[0] = 1
        for copy in fetch(te_ref[0], 0):
            copy.start()

    @pl.when(i < n_active)
    def _():
        expert = te_ref[i]
        first_tile_of_expert = jnp.logical_or(i == 0, expert != te_ref[jnp.maximum(i - 1, 0)])

        @pl.when(first_tile_of_expert)
        def _():
            slot = 1 - slot_ref[0]
            slot_ref[0] = slot
            for copy in fetch(expert, slot):
                copy.wait()
            nxt = lax.while_loop(
                lambda j: jnp.logical_and(j < n_active, te_ref[jnp.minimum(j, last_tile)] == expert),
                lambda j: j + 1, i + 1)

            @pl.when(nxt < n_active)
            def _():
                for copy in fetch(te_ref[jnp.minimum(nxt, last_tile)], 1 - slot):
                    copy.start()

        slot = slot_ref[0]

        def mlp(rows):
            x = _unpack_rows(x_ref[0:rows, :]).astype(BF16)
            gu = jnp.dot(x, wgu_buf[slot].astype(BF16), preferred_element_type=F32) + bgu_ref[0]
            gate = jnp.minimum(gu[:, :D_FF], SWIGLU_LIMIT)
            up = jnp.clip(gu[:, D_FF:], -SWIGLU_LIMIT, SWIGLU_LIMIT)
            hdn = (up + 1.0) * (gate * _sigmoid(SWIGLU_ALPHA * gate))
            y = jnp.dot(hdn.astype(BF16), wdn_buf[slot].astype(BF16), preferred_element_type=F32) + bdn_ref[0]
            y_ref[0:rows, :] = _pack_rows(y)

        path = sum((valid_ref[i] > rows).astype(jnp.int32) for rows in EXPERT_PATH_ROWS[:-1])
        for k, rows in enumerate(EXPERT_PATH_ROWS):
            @pl.when(path == k)
            def _(rows=rows):
                mlp(rows)
                if rows < EXPERT_TILE:
                    y_ref[rows:, :] = jnp.zeros((EXPERT_TILE - rows, PACK_WORDS), y_ref.dtype)

    @pl.when(i >= n_active)
    def _():
        y_ref[...] = jnp.zeros_like(y_ref)


def _expert_mlp(te, nact, valid, xs, w_gu, b_gu, w_dn, b_dn):
    d = D_MODEL
    n_tiles = xs.shape[0] // EXPERT_TILE

    def tile_map(i, te_ref, nact_ref, valid_ref):
        return (jnp.minimum(i, nact_ref[0] - 1), 0)

    def exp_map(i, te_ref, nact_ref, valid_ref):
        return (te_ref[jnp.minimum(i, nact_ref[0] - 1)], 0, 0)

    grid_spec = pltpu.PrefetchScalarGridSpec(
        num_scalar_prefetch=3,
        grid=(n_tiles,),
        in_specs=[pl.BlockSpec((EXPERT_TILE, PACK_WORDS), tile_map),
                  pl.BlockSpec(memory_space=pl.ANY),
                  pl.BlockSpec((1, 1, 2 * D_FF), exp_map),
                  pl.BlockSpec(memory_space=pl.ANY),
                  pl.BlockSpec((1, 1, d), exp_map)],
        out_specs=pl.BlockSpec((EXPERT_TILE, PACK_WORDS), lambda i, te_ref, nact_ref, valid_ref: (i, 0)),
        scratch_shapes=[pltpu.VMEM((2, d, 2 * D_FF), F32), pltpu.VMEM((2, D_FF, d), F32),
                        pltpu.SMEM((1,), jnp.int32), pltpu.SemaphoreType.DMA((2, 2))],
    )
    return pl.pallas_call(
        _expert_kernel,
        grid_spec=grid_spec,
        out_shape=jax.ShapeDtypeStruct(xs.shape, jnp.uint32),
        compiler_params=_cparams(("arbitrary",)),
        name="expert_mlp",
    )(te, nact, valid, xs, w_gu, b_gu, w_dn, b_dn)


def _combine_kernel(gfin_ref, *refs):
    out_ref = refs[-1]
    for s in range(COMBINE_SUB):
        h1_ref, info_ref, yg_ref = refs[3 * s:3 * s + 3]
        info = info_ref[...]
        h2 = h1_ref[...]
        for kk in range(TOP_K):
            h2 = h2 + info[:, kk:kk + 1] * _unpack_rows(yg_ref[kk])
        ms = jnp.mean(h2 * h2, axis=-1, keepdims=True)
        out_ref[s * CHUNK:(s + 1) * CHUNK, :] = h2 * lax.rsqrt(ms + RMS_EPS) * gfin_ref[...]


def _combine(h1, info, g_final, yg, n_batch, seq):
    d = h1.shape[1]
    chunks = seq // CHUNK

    def padded(s):
        def index(j):
            c = j * COMBINE_SUB + s
            return (c // chunks) * (chunks + 1) + c % chunks + 1
        return index

    in_specs = [pl.BlockSpec((1, d), lambda j: (0, 0))]
    operands = [g_final]
    for s in range(COMBINE_SUB):
        chunk_of = padded(s)
        in_specs += [pl.BlockSpec((CHUNK, d), lambda j, f=chunk_of: (f(j), 0)),
                     pl.BlockSpec((CHUNK, LANES), lambda j, f=chunk_of: (f(j), 0)),
                     pl.BlockSpec((TOP_K, CHUNK, PACK_WORDS), lambda j, f=chunk_of: (0, f(j), 0))]
        operands += [h1, info, yg]
    return pl.pallas_call(
        _combine_kernel,
        grid=(n_batch * chunks // COMBINE_SUB,),
        in_specs=in_specs,
        out_specs=pl.BlockSpec((COMBINE_SUB * CHUNK, d), lambda j: (j, 0)),
        out_shape=jax.ShapeDtypeStruct((n_batch * seq, d), F32),
        compiler_params=_cparams(("parallel",)),
        name="combine",
    )(*operands)


def kernel(x, meta_tokens, lb_logits, g_mix, w_in, w_dw, b_dw, ln_g, ln_b, w_conv_out, b_conv_out,
           g_onorm, w_rnn_out, w_o, g_ffn, w_router, b_router, w_gate_up, b_gate_up, w_down, b_down,
           g_final):
    n_batch, seq, d = x.shape
    assert d == D_MODEL and w_in.shape[0] == 1, "single-layer block with D_MODEL features"
    assert seq % (CHUNK * COMBINE_SUB) == 0
    seq_pad = CHUNK + seq
    tp = n_batch * seq_pad
    assert tp % ROW_TILE == 0

    x2 = x.reshape(n_batch * seq, d)
    meta = meta_tokens.astype(x.dtype)
    vec = lambda a: a.reshape(1, -1).astype(F32)

    acts, logf = _in_proj(x2, meta, seq, vec(g_mix[0]), lb_logits.astype(F32), w_in[0])
    w_taps = w_dw[0].astype(F32).reshape(CONV_WIDTH, d // LANES, LANES).transpose(1, 0, 2)
    ap = _conv_branch(acts, w_taps, vec(b_dw[0]), vec(ln_g[0]), vec(ln_b[0]),
                      w_conv_out[0], vec(b_conv_out[0]))
    on = _hgrn2_scan(acts, logf, vec(g_onorm[0]), n_batch)
    w_router_pad = jnp.pad(w_router[0].astype(F32), ((0, 0), (0, LANES - N_EXPERTS)))
    b_router_pad = jnp.pad(vec(b_router[0]), ((0, 0), (0, LANES - N_EXPERTS)))
    h1, u2p, info, route, cnt = _merge_route(x2, meta, seq, on, ap, acts, w_rnn_out[0], w_o[0],
                                      vec(g_ffn[0]), w_router_pad, b_router_pad)

    n_tiles_max = -(-(tp * TOP_K + N_EXPERTS * (EXPERT_TILE - 1)) // EXPERT_TILE)
    dest, te = _slots(route, cnt, n_tiles_max)
    dest_flat = dest[:TOP_K].reshape(-1)
    xs = _dispatch(dest_flat, u2p, n_tiles_max * EXPERT_TILE)
    ys = _expert_mlp(te[0, :n_tiles_max], te[1, :1], te[2, :n_tiles_max], xs, w_gate_up[0],
                     b_gate_up[0].reshape(N_EXPERTS, 1, -1).astype(F32), w_down[0],
                     b_down[0].reshape(N_EXPERTS, 1, -1).astype(F32))
    yg = _gather_expert_rows(dest_flat, ys)
    out = _combine(h1, info, vec(g_final), yg, n_batch, seq)
    return out.reshape(n_batch, seq, d)
```

```python
import functools

import jax
import jax.numpy as jnp
from jax import lax
from jax.experimental import pallas as pl
from jax.experimental.pallas import tpu as pltpu
from jax.experimental.pallas import tpu_sc as plsc

F32 = jnp.float32
BF16 = jnp.bfloat16

D_MODEL = 1024
N_META = 16
CHUNK = 128
CHUNK_PAD = CHUNK - N_META
CONV_WIDTH = 31
HEAD_DIM = 128
N_HEADS = D_MODEL // HEAD_DIM
N_EXPERTS = 32
TOP_K = 4
D_FF = D_MODEL
SWIGLU_LIMIT = 7.0
SWIGLU_ALPHA = 1.702
RMS_EPS = 1e-6
LN_EPS = 1e-5

ACT_AGLU, ACT_Q, ACT_K, ACT_V, ACT_OG, ACT_SGA, ACT_SGB = range(7)
N_ACTS = 7

LANES = 128
SUBLANES = 8
SUB_BLOCK = 32
N_SUB = CHUNK // SUB_BLOCK
HALO = 32
ROW_TILE = 640
IN_TILE = 320
SCAN_CHUNKS = 5
MERGE_PARTS = 2
SLOTS_STEPS = 5
ROUTE_ROWS = 16
CONV_ROWS = 64
CONV_SUMS = 4
EXPERT_TILE = 1024
EXPERT_PATH_ROWS = tuple(EXPERT_TILE >> k for k in (3, 2, 1, 0))
SC_MAX_INDICES = 128
SC_ROW_BUFFER_BYTES = 448 * 1024
SCATTER_RING = 2
GATHER_RING = 4
GATHER_LAG = 2
COMBINE_SUB = 8
NEG_BIG = -1e30
VMEM_LIMIT = 56 * 1024 * 1024


def _sigmoid(x):
    return 1.0 / (1.0 + jnp.exp(-x))


def _cparams(sem):
    return pltpu.CompilerParams(dimension_semantics=sem, vmem_limit_bytes=VMEM_LIMIT)


def _const_spec(shape):
    nd = len(shape)
    return pl.BlockSpec(shape, lambda *_: (0,) * nd)


PACK_WORDS = D_MODEL // 2
HIGH_HALF = 0xFFFF0000


def _pack_rows(x):
    lo = lax.bitcast_convert_type(x[:, :PACK_WORDS].astype(BF16).astype(F32), jnp.uint32)
    hi = lax.bitcast_convert_type(x[:, PACK_WORDS:].astype(BF16).astype(F32), jnp.uint32)
    return (lo >> 16) | (hi & jnp.uint32(HIGH_HALF))


def _unpack_rows(w):
    lo = lax.bitcast_convert_type(w << 16, F32)
    hi = lax.bitcast_convert_type(w & jnp.uint32(HIGH_HALF), F32)
    return jnp.concatenate([lo, hi], axis=1)


def _residual_tile(x_ref, meta_ref, is_first):
    x = x_ref[...]
    tm, d = x.shape
    prefix = jnp.concatenate([jnp.zeros((CHUNK_PAD, d), x.dtype), meta_ref[...]], axis=0)
    first = jnp.concatenate([prefix, x[:tm - CHUNK, :]], axis=0)
    return jnp.where(is_first, first, x)


def _residual_spec(tm, d, seq, tiles_per_batch, n_tiles):
    def start(i):
        i = jnp.minimum(i, n_tiles - 1)
        b, t = i // tiles_per_batch, i % tiles_per_batch
        return (pl.multiple_of(b * seq + jnp.maximum(t * tm - CHUNK, 0), SUBLANES), 0)
    return pl.BlockSpec((pl.Element(tm), pl.Element(d)), start)


def _in_proj_kernel(x_ref, meta_ref, g_ref, lbl_ref, w_ref, acts_ref, logf_ref, *, tiles_per_batch):
    d = D_MODEL

    def put(group, value):
        acts_ref[:, group * d:(group + 1) * d] = value.astype(acts_ref.dtype)

    h = _residual_tile(x_ref, meta_ref, pl.program_id(0) % tiles_per_batch == 0)
    ms = jnp.mean(h * h, axis=-1, keepdims=True)
    u = (h * lax.rsqrt(ms + RMS_EPS) * g_ref[...]).astype(BF16)

    def proj(j):
        return jnp.dot(u, w_ref[:, j * d:(j + 1) * d].astype(BF16), preferred_element_type=F32)

    put(ACT_AGLU, proj(0) * _sigmoid(proj(1)))
    zq = proj(2)
    put(ACT_Q, zq * _sigmoid(zq))
    lbl = lbl_ref[...]
    e = jnp.exp(lbl - jnp.max(lbl, axis=0, keepdims=True))
    lb = e[0:1, :] / jnp.sum(e, axis=0, keepdims=True)
    s = _sigmoid(proj(3))
    logf_ref[...] = jnp.log(lb + (1.0 - lb) * s)
    put(ACT_K, (1.0 - lb) * (1.0 - s))
    zg = proj(5)
    put(ACT_OG, zg * _sigmoid(zg))
    put(ACT_SGA, _sigmoid(proj(6)))
    put(ACT_SGB, _sigmoid(proj(7)))
    put(ACT_V, proj(4))


def _in_proj(x2, meta, seq, g_mix, lb_logits, w_in):
    d = x2.shape[1]
    n_batch = x2.shape[0] // seq
    tm = IN_TILE
    tpb = (seq + CHUNK) // tm
    tp = n_batch * (seq + CHUNK)
    assert tpb * tm == seq + CHUNK and tm > CHUNK
    row = pl.BlockSpec((tm, d), lambda i: (i, 0))
    return pl.pallas_call(
        functools.partial(_in_proj_kernel, tiles_per_batch=tpb),
        grid=(tp // tm,),
        in_specs=[_residual_spec(tm, d, seq, tpb, tp // tm), _const_spec(meta.shape), _const_spec((1, d)),
                  _const_spec(lb_logits.shape),
                  pl.BlockSpec(w_in.shape, lambda i: (0, 0), pipeline_mode=pl.Buffered(1))],
        out_specs=[pl.BlockSpec((tm, N_ACTS * d), lambda i: (i, 0)), row],
        out_shape=[jax.ShapeDtypeStruct((tp, N_ACTS * d), BF16), jax.ShapeDtypeStruct((tp, d), F32)],
        compiler_params=_cparams(("parallel",)),
        name="in_proj",
    )(x2, meta, g_mix, lb_logits, w_in)


def _conv_kernel(halo_ref, cur_ref, wdw_ref, bdw_ref, lng_ref, lnb_ref, wout_ref, bout_ref, sga_ref,
                 out_ref, win_ref, acc_ref, shift_ref):
    tm = cur_ref.shape[0]
    win_ref[0:HALO, :] = halo_ref[...].astype(F32)
    win_ref[HALO:, :] = cur_ref[...].astype(F32)
    first_tap = HALO - (CONV_WIDTH - 1)
    for c in range(D_MODEL // LANES):
        lanes = slice(c * LANES, (c + 1) * LANES)
        n_rows = tm + HALO - SUBLANES
        for s in range(1, SUBLANES):
            shift_ref[s - 1, 0:n_rows, :] = win_ref[pl.ds(s, n_rows), lanes]
        bias = jnp.broadcast_to(bdw_ref[:, lanes], (CONV_ROWS, LANES))
        groups = CONV_ROWS // SUBLANES

        def chunk(r, carry, c=c, lanes=lanes, bias=bias):
            r0 = pl.multiple_of(r * CONV_ROWS, CONV_ROWS)
            accs = [bias.reshape(groups, SUBLANES, LANES)] + [None] * (CONV_SUMS - 1)
            for j in range(CONV_WIDTH):
                off = first_tap + j
                rows = pl.ds(r0 + off - off % SUBLANES, CONV_ROWS)
                w_j = wdw_ref[c, pl.ds(j, SUBLANES, stride=0), :]
                window = shift_ref[off % SUBLANES - 1, rows, :] if off % SUBLANES else win_ref[rows, lanes]
                term = w_j[None] * window.reshape(groups, SUBLANES, LANES)
                accs[j % CONV_SUMS] = term if accs[j % CONV_SUMS] is None else accs[j % CONV_SUMS] + term
            while len(accs) > 1:
                accs = [a + b for a, b in zip(accs[0::2], accs[1::2])]
            acc_ref[pl.ds(r0, CONV_ROWS), lanes] = accs[0].reshape(CONV_ROWS, LANES)
            return carry

        lax.fori_loop(0, tm // CONV_ROWS, chunk, 0)
    a = acc_ref[...]
    mu = jnp.mean(a, axis=-1, keepdims=True)
    ac = a - mu
    var = jnp.mean(ac * ac, axis=-1, keepdims=True)
    y = ac * lax.rsqrt(var + LN_EPS) * lng_ref[...] + lnb_ref[...]
    y = y * _sigmoid(y)
    o = jnp.dot(y.astype(BF16), wout_ref[...].astype(BF16), preferred_element_type=F32) + bout_ref[...]
    out_ref[...] = (sga_ref[...].astype(F32) * o).astype(out_ref.dtype)


def _conv_branch(acts, w_dw, b_dw, ln_g, ln_b, w_out, b_out):
    tp, d = acts.shape[0], D_MODEL
    tm = ROW_TILE
    per = tm // HALO
    row = pl.BlockSpec((tm, d), lambda i: (i, 0))
    glu = pl.BlockSpec((tm, d), lambda i: (i, ACT_AGLU))
    gate = pl.BlockSpec((tm, d), lambda i: (i, ACT_SGA))
    halo = pl.BlockSpec((HALO, d), lambda i: (jnp.maximum(i * per - 1, 0), ACT_AGLU))
    vec = _const_spec((1, d))
    return pl.pallas_call(
        _conv_kernel,
        grid=(tp // tm,),
        in_specs=[halo, glu, _const_spec(w_dw.shape), vec, vec, vec, _const_spec((d, d)), vec, gate],
        out_specs=row,
        out_shape=jax.ShapeDtypeStruct((tp, d), BF16),
        scratch_shapes=[pltpu.VMEM((tm + HALO, d), F32), pltpu.VMEM((tm, d), F32),
                        pltpu.VMEM((SUBLANES - 1, tm + HALO - SUBLANES, LANES), F32)],
        compiler_params=_cparams(("parallel",)),
        name="conv_branch",
    )(acts, acts, w_dw, b_dw, ln_g, ln_b, w_out, b_out, acts)


def _split_bf16(x):
    hi = x.astype(BF16)
    return hi, (x - hi.astype(F32)).astype(BF16)


def _nt_dot(a, b):
    return lax.dot_general(a, b, (((1,), (1,)), ((), ())), preferred_element_type=F32)


def _tn_dot(a, b):
    return lax.dot_general(a, b, (((0,), (0,)), ((), ())), preferred_element_type=F32)


def _scan_kernel(q_ref, k_ref, v_ref, lf_ref, og_ref, gon_ref, o_ref, st_ref):
    @pl.when(pl.program_id(1) == 0)
    def _():
        st_ref[...] = jnp.zeros_like(st_ref)

    c = CHUNK
    row = lax.broadcasted_iota(jnp.int32, (c, c), 0)
    col = lax.broadcasted_iota(jnp.int32, (c, c), 1)
    causal = col <= row
    diag_mask = jnp.logical_and(causal, row // SUB_BLOCK == col // SUB_BLOCK)
    tri = causal.astype(BF16)

    def bcast_rows(rows):
        return jnp.concatenate([jnp.broadcast_to(r, (SUB_BLOCK, HEAD_DIM)) for r in rows], axis=0)

    heads = [slice(h * HEAD_DIM, (h + 1) * HEAD_DIM) for h in range(N_HEADS)]
    chunks = [slice(ci * c, (ci + 1) * c) for ci in range(SCAN_CHUNKS)]

    bcums = []
    for cs in chunks:
        lf = lf_ref[cs, :]
        lf_hi = lf.astype(BF16)
        lf_mid, lf_lo = _split_bf16(lf - lf_hi.astype(F32))
        bcums.append(jnp.dot(tri, lf_hi, preferred_element_type=F32)
                     + (jnp.dot(tri, lf_mid, preferred_element_type=F32)
                        + jnp.dot(tri, lf_lo, preferred_element_type=F32)))
    operands = {}
    for ci, cs in enumerate(chunks):
        for h, hs in enumerate(heads):
            b = bcums[ci][:, hs]
            q = q_ref[cs, hs].astype(F32)
            k = k_ref[cs, hs].astype(F32)
            ends = [b[i * SUB_BLOCK + SUB_BLOCK - 1:i * SUB_BLOCK + SUB_BLOCK, :] for i in range(N_SUB)]
            mids = [b[i * SUB_BLOCK + SUB_BLOCK // 2 - 1:i * SUB_BLOCK + SUB_BLOCK // 2, :] for i in range(N_SUB)]
            mid_full = bcast_rows(mids)
            b_last = ends[-1]
            zero_row = jnp.zeros_like(b_last)
            qm = q * jnp.exp(b - mid_full)
            km = k * jnp.exp(mid_full - b)
            q_parts, k_parts = [], []
            for j in range(N_SUB - 1):
                q_rows = [jnp.exp(mids[i] - ends[j]) if i > j else zero_row for i in range(N_SUB)]
                k_rows = [jnp.exp(ends[j] - mids[j]) if i == j else zero_row for i in range(N_SUB)]
                q_parts.append((qm * bcast_rows(q_rows)).astype(BF16))
                k_parts.append((km * bcast_rows(k_rows)).astype(BF16))
            q_in = (qm * bcast_rows([jnp.exp(m) for m in mids])).astype(BF16)
            k_out = (km * bcast_rows([jnp.exp(b_last - m) for m in mids])).astype(BF16)
            operands[ci, h] = (qm.astype(BF16), km.astype(BF16), jnp.concatenate(q_parts, axis=1),
                               jnp.concatenate(k_parts, axis=1), q_in, k_out, jnp.exp(b_last))
    products = {}
    for ci, cs in enumerate(chunks):
        for h, hs in enumerate(heads):
            qm, km, q_cat, k_cat, _, k_out, _ = operands[ci, h]
            products[ci, h] = (_nt_dot(qm, km), _nt_dot(q_cat, k_cat), _tn_dot(v_ref[cs, hs], k_out))
    within = {}
    for ci, cs in enumerate(chunks):
        for h, hs in enumerate(heads):
            same_block, earlier_blocks, _ = products[ci, h]
            scores = jnp.where(diag_mask, same_block, 0.0) + earlier_blocks
            within[ci, h] = jnp.dot(scores.astype(BF16), v_ref[cs, hs], preferred_element_type=F32)
    states = [st_ref[h] for h in range(N_HEADS)]
    outs = {}
    for ci in range(SCAN_CHUNKS):
        for h in range(N_HEADS):
            outs[ci, h] = within[ci, h] + _nt_dot(operands[ci, h][4], states[h].astype(BF16))
            states[h] = states[h] * operands[ci, h][6] + products[ci, h][2]
    for h in range(N_HEADS):
        st_ref[h] = states[h]
    for ci, cs in enumerate(chunks):
        for h, hs in enumerate(heads):
            o = outs[ci, h]
            ms = jnp.mean(o * o, axis=-1, keepdims=True)
            on = o * lax.rsqrt(ms + RMS_EPS) * gon_ref[:, hs]
            o_ref[cs, hs] = (on * og_ref[cs, hs].astype(F32)).astype(o_ref.dtype)


def _hgrn2_scan(acts, logf, g_onorm, n_batch):
    tp, d = logf.shape
    rows = SCAN_CHUNKS * CHUNK
    n_chunks = tp // n_batch // rows
    assert n_chunks * rows * n_batch == tp
    col = lambda group: pl.BlockSpec((rows, d), lambda b, c: (b * n_chunks + c, group))
    blk = col(0)
    return pl.pallas_call(
        _scan_kernel,
        grid=(n_batch, n_chunks),
        in_specs=[col(ACT_Q), col(ACT_K), col(ACT_V), blk, col(ACT_OG), pl.BlockSpec((1, d), lambda b, c: (0, 0))],
        out_specs=blk,
        out_shape=jax.ShapeDtypeStruct((tp, d), BF16),
        scratch_shapes=[pltpu.VMEM((N_HEADS, HEAD_DIM, HEAD_DIM), F32)],
        compiler_params=_cparams(("arbitrary", "arbitrary")),
        name="hgrn2_scan",
    )(acts, acts, acts, logf, acts, g_onorm)


def _merge_route_kernel(x_ref, meta_ref, on_ref, ap_ref, sgb_ref, wrnn_ref, wo_ref, gffn_ref, wr_ref, br_ref,
                        h1_ref, u2_ref, info_ref, route_ref, cnt_ref, tri_ref, carry_ref, colcnt_ref, logits_ref, hres_ref,
                        *, tiles_per_batch, n_tiles):
    tm = on_ref.shape[0]
    i = pl.program_id(0)
    hres_ref[...] = _residual_tile(x_ref, meta_ref, jnp.minimum(i, n_tiles - 1) % tiles_per_batch == 0)

    @pl.when(i == 0)
    def _():
        r_i = lax.broadcasted_iota(jnp.int32, (tm, tm), 0)
        c_i = lax.broadcasted_iota(jnp.int32, (tm, tm), 1)
        tri_ref[...] = (c_i < r_i).astype(BF16)
        carry_ref[...] = jnp.zeros_like(carry_ref)
        colcnt_ref[...] = jnp.zeros_like(colcnt_ref)
        logits_ref[...] = jnp.zeros_like(logits_ref)

    routed = i > 0
    lane = lax.broadcasted_iota(jnp.int32, (tm, LANES), 1)
    cur = jnp.where(lane < N_EXPERTS, logits_ref[...], NEG_BIG)
    vals, idxs, sels = [], [], []

    def topk_round(cur):
        m = jnp.max(cur, axis=-1, keepdims=True)
        idx = jnp.min(jnp.where(cur == m, lane, LANES), axis=-1, keepdims=True)
        sel = lane == idx
        vals.append(m)
        idxs.append(idx)
        sels.append(sel)
        return jnp.where(sel, 2.0 * NEG_BIG, cur)

    w_hi, w_lo = _split_bf16(wr_ref[...])
    w_hi_lo = jnp.concatenate([w_hi, w_lo], axis=1)
    part = tm // MERGE_PARTS
    rows = [slice(p * part, (p + 1) * part) for p in range(MERGE_PARTS)]
    w_rnn = wrnn_ref[...].astype(BF16)
    w_out = wo_ref[...].astype(BF16)
    rs = [jnp.dot(on_ref[rw, :], w_rnn, preferred_element_type=F32) for rw in rows]
    cur = topk_round(cur)
    ys = [(ap_ref[rw, :].astype(F32) + sgb_ref[rw, :].astype(F32) * r).astype(BF16) for rw, r in zip(rows, rs)]
    h1s = [hres_ref[rw, :] + jnp.dot(y, w_out, preferred_element_type=F32) for rw, y in zip(rows, ys)]
    cur = topk_round(cur)
    u2s = []
    for rw, h1 in zip(rows, h1s):
        h1_ref[rw, :] = h1
        ms = jnp.mean(h1 * h1, axis=-1, keepdims=True)
        u2 = h1 * lax.rsqrt(ms + RMS_EPS) * gffn_ref[...]
        u2_ref[rw, :] = _pack_rows(u2)
        u2s.append(u2)
    cur = topk_round(cur)
    new_logits = []
    for u2 in u2s:
        u_hi, u_lo = _split_bf16(u2)
        both = jnp.dot(u_hi, w_hi_lo, preferred_element_type=F32)
        new_logits.append(both[:, :LANES] + (jnp.dot(u_lo, w_hi, preferred_element_type=F32) + both[:, LANES:])
                          + br_ref[...])
    cur = topk_round(cur)
    assert len(vals) == TOP_K
    exps = [jnp.exp(vk - vals[0]) for vk in vals]
    den = exps[0] + exps[1] + exps[2] + exps[3]
    onehot = jnp.logical_or(jnp.logical_or(sels[0], sels[1]), jnp.logical_or(sels[2], sels[3]))
    onehot = jnp.logical_and(onehot, routed)
    onehot_bf = onehot.astype(BF16)
    rank_all = jnp.dot(tri_ref[...], onehot_bf, preferred_element_type=F32) + carry_ref[...]
    carry_ref[...] += jnp.sum(onehot.astype(F32), axis=0, keepdims=True)
    colcnt_ref[...] += _tn_dot(onehot_bf, jnp.ones((tm, LANES), BF16))
    info = jnp.zeros((tm, LANES), F32)
    for kk in range(TOP_K):
        rank_k = jnp.sum(jnp.where(sels[kk], rank_all, 0.0), axis=-1, keepdims=True)
        info = jnp.where(lane == kk, exps[kk] / den, info)
        info = jnp.where(lane == TOP_K + kk, idxs[kk].astype(F32), info)
        info = jnp.where(lane == 2 * TOP_K + kk, rank_k, info)
    info_ref[...] = info
    route_ref[...] = info.T[0:ROUTE_ROWS, :]
    cnt_ref[...] = colcnt_ref[...]
    logits_ref[...] = jnp.concatenate(new_logits, axis=0)


def _merge_route(x2, meta, seq, on, ap, acts, w_rnn, w_o, g_ffn, w_router_pad, b_router_pad):
    tp, d = on.shape
    tm = ROW_TILE
    n_tiles = tp // tm
    tpb = (seq + CHUNK) // tm
    assert tpb * tm == seq + CHUNK
    this_tile = lambda i: (jnp.minimum(i, n_tiles - 1), 0)
    prev_tile = lambda i: (jnp.maximum(i - 1, 0), 0)
    row = pl.BlockSpec((tm, d), this_tile)
    vec = _const_spec((1, d))
    return pl.pallas_call(
        functools.partial(_merge_route_kernel, tiles_per_batch=tpb, n_tiles=n_tiles),
        grid=(n_tiles + 1,),
        in_specs=[_residual_spec(tm, d, seq, tpb, n_tiles), _const_spec(meta.shape), row, row,
                  pl.BlockSpec((tm, d), lambda i: (jnp.minimum(i, n_tiles - 1), ACT_SGB)),
                  _const_spec((d, d)), _const_spec((d, d)), vec, _const_spec((d, LANES)), _const_spec((1, LANES))],
        out_specs=[row, pl.BlockSpec((tm, PACK_WORDS), this_tile),
                   pl.BlockSpec((tm, LANES), prev_tile),
                   pl.BlockSpec((ROUTE_ROWS, tm), lambda i: (0, jnp.maximum(i - 1, 0))), _const_spec((LANES, LANES))],
        out_shape=[jax.ShapeDtypeStruct((tp, d), F32), jax.ShapeDtypeStruct((tp, PACK_WORDS), jnp.uint32),
                   jax.ShapeDtypeStruct((tp, LANES), F32), jax.ShapeDtypeStruct((ROUTE_ROWS, tp), F32),
                   jax.ShapeDtypeStruct((LANES, LANES), F32)],
        scratch_shapes=[pltpu.VMEM((tm, tm), BF16), pltpu.VMEM((1, LANES), F32),
                        pltpu.VMEM((LANES, LANES), F32), pltpu.VMEM((tm, LANES), F32), pltpu.VMEM((tm, d), F32)],
        compiler_params=_cparams(("arbitrary",)),
        name="merge_route",
    )(x2, meta, on, ap, acts, w_rnn, w_o, g_ffn, w_router_pad, b_router_pad)


def _slots_kernel(route_ref, cnt_ref, dest_ref, te_ref, pstart_ref):
    tl = route_ref.shape[1]
    n_tile_lanes = te_ref.shape[1]

    @pl.when(pl.program_id(0) == 0)
    def _():
        r_i = lax.broadcasted_iota(jnp.int32, (LANES, LANES), 0)
        c_i = lax.broadcasted_iota(jnp.int32, (LANES, LANES), 1)
        cnt = cnt_ref[...]
        tiles = jnp.floor((cnt + (EXPERT_TILE - 1)) / EXPERT_TILE)
        pend_col = jnp.dot((c_i <= r_i).astype(F32), tiles, preferred_element_type=F32,
                           precision=lax.Precision.HIGHEST)
        pstart_ref[...] = pend_col - tiles
        tile_id = lax.broadcasted_iota(jnp.int32, (LANES, n_tile_lanes), 1).astype(F32)
        exp_id = lax.broadcasted_iota(jnp.int32, (LANES, n_tile_lanes), 0)
        pend_wide = jnp.concatenate([pend_col] * (n_tile_lanes // LANES), axis=1)
        below = jnp.logical_and(pend_wide <= tile_id, exp_id < N_EXPERTS)
        te = jnp.minimum(jnp.sum(below.astype(F32), axis=0, keepdims=True), N_EXPERTS - 1.0)
        n_active = pend_col[N_EXPERTS - 1:N_EXPERTS, 0:1]
        owner = exp_id.astype(F32) == te
        cnt_wide = jnp.concatenate([cnt] * (n_tile_lanes // LANES), axis=1)
        first_tile = pend_wide - jnp.concatenate([tiles] * (n_tile_lanes // LANES), axis=1)
        left = jnp.sum(jnp.where(owner, cnt_wide - (tile_id - first_tile) * EXPERT_TILE, 0.0), axis=0, keepdims=True)
        valid = jnp.clip(left, 0.0, float(EXPERT_TILE))
        sub_t = lax.broadcasted_iota(jnp.int32, (SUBLANES, n_tile_lanes), 0)
        out = jnp.where(sub_t == 0, jnp.broadcast_to(te, (SUBLANES, n_tile_lanes)),
                        jnp.where(sub_t == 1, jnp.broadcast_to(n_active, (SUBLANES, n_tile_lanes)),
                                  jnp.broadcast_to(valid, (SUBLANES, n_tile_lanes))))
        te_ref[...] = out.astype(jnp.int32)

    route = route_ref[...]
    expert = lax.broadcasted_iota(jnp.int32, (N_EXPERTS, tl), 0)
    first_slot = jnp.concatenate([pstart_ref[0:N_EXPERTS, :]] * (tl // LANES), axis=1) * EXPERT_TILE
    rows = []
    for kk in range(TOP_K):
        idx_k = route[TOP_K + kk:TOP_K + kk + 1, :].astype(jnp.int32)
        start_k = jnp.sum(jnp.where(expert == idx_k, first_slot, 0.0), axis=0, keepdims=True)
        rows.append(start_k + route[2 * TOP_K + kk:2 * TOP_K + kk + 1, :])
    rows.append(jnp.zeros((SUBLANES - TOP_K, tl), F32))
    dest_ref[...] = jnp.concatenate(rows, axis=0).astype(jnp.int32)


def _slots(route, cnt, n_tiles_max):
    tp = route.shape[1]
    tl = tp // SLOTS_STEPS
    assert tl * SLOTS_STEPS == tp and tl % LANES == 0
    n_tile_lanes = -(-n_tiles_max // LANES) * LANES
    return pl.pallas_call(
        _slots_kernel,
        grid=(SLOTS_STEPS,),
        in_specs=[pl.BlockSpec((ROUTE_ROWS, tl), lambda i: (0, i)), _const_spec((LANES, LANES))],
        out_specs=[pl.BlockSpec((SUBLANES, tl), lambda i: (0, i)), _const_spec((8, n_tile_lanes))],
        out_shape=[jax.ShapeDtypeStruct((SUBLANES, tp), jnp.int32),
                   jax.ShapeDtypeStruct((8, n_tile_lanes), jnp.int32)],
        scratch_shapes=[pltpu.VMEM((LANES, LANES), F32)],
        compiler_params=_cparams(("arbitrary",)),
        name="slots",
    )(route, cnt)


def _sc_workers():
    sc = plsc.get_sparse_core_info()
    return sc.num_cores, sc.num_cores * sc.num_subcores


def _sc_split(tp, ring):
    n_workers = _sc_workers()[1]
    per_worker = tp // n_workers
    assert per_worker * n_workers == tp
    limit = min(SC_MAX_INDICES, SC_ROW_BUFFER_BYTES // (ring * PACK_WORDS * 4))
    return per_worker, max(c for c in range(SUBLANES, limit + 1, SUBLANES) if per_worker % c == 0)


def _sc_first_token(per_worker):
    return pl.multiple_of((lax.axis_index("s") * _sc_workers()[0] + lax.axis_index("c")) * per_worker, SUBLANES)


def _sc_ring_kernel(out_type, n_index_rows, chunk, ring):
    return functools.partial(
        pl.kernel, mesh=plsc.VectorSubcoreMesh(core_axis_name="c", subcore_axis_name="s"), out_type=out_type,
        scratch_types=[pltpu.VMEM((n_index_rows, chunk), jnp.int32), pltpu.VMEM((ring, chunk, PACK_WORDS), jnp.uint32),
                       pltpu.SemaphoreType.DMA, pltpu.SemaphoreType.DMA((ring,)), pltpu.SemaphoreType.DMA((ring,))])


def _wait_all(copies):
    for c in copies:
        c.wait()


def _dispatch(dest_flat, u2p, n_slots):
    tp = u2p.shape[0]
    per_worker, ch = _sc_split(tp, SCATTER_RING)
    n_chunks = per_worker // ch

    @_sc_ring_kernel(jax.ShapeDtypeStruct((n_slots, PACK_WORDS), jnp.uint32), n_chunks * TOP_K, ch, SCATTER_RING)
    def scatter_rows(u2_hbm, dest_hbm, xs_hbm, idx_v, rows_v, sem_idx, sem_load, sem_scatter):
        base = _sc_first_token(per_worker)
        _wait_all([pltpu.async_copy(dest_hbm.at[pl.ds(kk * tp + base + c * ch, ch)], idx_v.at[c * TOP_K + kk], sem_idx)
                   for c in range(n_chunks) for kk in range(TOP_K)])

        def load(c):
            slot = c % SCATTER_RING
            return pltpu.async_copy(u2_hbm.at[pl.ds(base + c * ch, ch)], rows_v.at[slot], sem_load.at[slot])

        def scatter(c):
            slot = c % SCATTER_RING
            return [pltpu.async_copy(rows_v.at[slot], xs_hbm.at[idx_v.at[c * TOP_K + kk]], sem_scatter.at[slot])
                    for kk in range(TOP_K)]

        loads, scatters = {}, {}
        for c in range(n_chunks + 1):
            if c < n_chunks:
                if c >= SCATTER_RING:
                    _wait_all(scatters.pop(c - SCATTER_RING))
                loads[c] = load(c)
            if c >= 1:
                loads.pop(c - 1).wait()
                scatters[c - 1] = scatter(c - 1)
        for copies in scatters.values():
            _wait_all(copies)

    return scatter_rows(u2p, dest_flat)


def _gather_expert_rows(dest_flat, ys):
    tp = dest_flat.shape[0] // TOP_K
    per_worker, ch = _sc_split(tp, GATHER_RING)
    units = [(c, kk) for c in range(per_worker // ch) for kk in range(TOP_K)]

    @_sc_ring_kernel(jax.ShapeDtypeStruct((TOP_K, tp, PACK_WORDS), jnp.uint32), len(units), ch, GATHER_RING)
    def gather_rows(ys_hbm, dest_hbm, out_hbm, idx_v, rows_v, sem_idx, sem_gather, sem_write):
        base = _sc_first_token(per_worker)
        _wait_all([pltpu.async_copy(dest_hbm.at[pl.ds(kk * tp + base + c * ch, ch)], idx_v.at[u], sem_idx)
                   for u, (c, kk) in enumerate(units)])

        def gather(u):
            slot = u % GATHER_RING
            return pltpu.async_copy(ys_hbm.at[idx_v.at[u]], rows_v.at[slot], sem_gather.at[slot])

        def write(u):
            c, kk = units[u]
            slot = u % GATHER_RING
            return pltpu.async_copy(rows_v.at[slot], out_hbm.at[kk, pl.ds(base + c * ch, ch)], sem_write.at[slot])

        gathers, writes = {}, {}
        for u in range(len(units) + GATHER_LAG):
            if u < len(units):
                if u >= GATHER_RING:
                    writes.pop(u - GATHER_RING).wait()
                gathers[u] = gather(u)
            if u >= GATHER_LAG:
                gathers.pop(u - GATHER_LAG).wait()
                writes[u - GATHER_LAG] = write(u - GATHER_LAG)
        _wait_all(writes.values())

    return gather_rows(ys, dest_flat)


def _expert_kernel(te_ref, nact_ref, valid_ref, x_ref, wgu_hbm, bgu_ref, wdn_hbm, bdn_ref, y_ref,
                   wgu_buf, wdn_buf, slot_ref, sem):
    i = pl.program_id(0)
    n_active = nact_ref[0]
    last_tile = pl.num_programs(0) - 1

    def fetch(expert, slot):
        return (pltpu.make_async_copy(wgu_hbm.at[expert], wgu_buf.at[slot], sem.at[slot, 0]),
                pltpu.make_async_copy(wdn_hbm.at[expert], wdn_buf.at[slot], sem.at[slot, 1]))

    @pl.when(i == 0)
    def _():
        slot_ref[0] = 1
        for copy in fetch(te_ref[0], 0):
            copy.start()

    @pl.when(i < n_active)
    def _():
        expert = te_ref[i]
        first_tile_of_expert = jnp.logical_or(i == 0, expert != te_ref[jnp.maximum(i - 1, 0)])

        @pl.when(first_tile_of_expert)
        def _():
            slot = 1 - slot_ref[0]
            slot_ref[0] = slot
            for copy in fetch(expert, slot):
                copy.wait()
            nxt = lax.while_loop(
                lambda j: jnp.logical_and(j < n_active, te_ref[jnp.minimum(j, last_tile)] == expert),
                lambda j: j + 1, i + 1)

            @pl.when(nxt < n_active)
            def _():
                for copy in fetch(te_ref[jnp.minimum(nxt, last_tile)], 1 - slot):
                    copy.start()

        slot = slot_ref[0]

        def mlp(rows):
            x = _unpack_rows(x_ref[0:rows, :]).astype(BF16)
            gu = jnp.dot(x, wgu_buf[slot].astype(BF16), preferred_element_type=F32) + bgu_ref[0]
            gate = jnp.minimum(gu[:, :D_FF], SWIGLU_LIMIT)
            up = jnp.clip(gu[:, D_FF:], -SWIGLU_LIMIT, SWIGLU_LIMIT)
            hdn = (up + 1.0) * (gate * _sigmoid(SWIGLU_ALPHA * gate))
            y = jnp.dot(hdn.astype(BF16), wdn_buf[slot].astype(BF16), preferred_element_type=F32) + bdn_ref[0]
            y_ref[0:rows, :] = _pack_rows(y)

        path = sum((valid_ref[i] > rows).astype(jnp.int32) for rows in EXPERT_PATH_ROWS[:-1])
        for k, rows in enumerate(EXPERT_PATH_ROWS):
            @pl.when(path == k)
            def _(rows=rows):
                mlp(rows)
                if rows < EXPERT_TILE:
                    y_ref[rows:, :] = jnp.zeros((EXPERT_TILE - rows, PACK_WORDS), y_ref.dtype)

    @pl.when(i >= n_active)
    def _():
        y_ref[...] = jnp.zeros_like(y_ref)


def _expert_mlp(te, nact, valid, xs, w_gu, b_gu, w_dn, b_dn):
    d = D_MODEL
    n_tiles = xs.shape[0] // EXPERT_TILE

    def tile_map(i, te_ref, nact_ref, valid_ref):
        return (jnp.minimum(i, nact_ref[0] - 1), 0)

    def exp_map(i, te_ref, nact_ref, valid_ref):
        return (te_ref[jnp.minimum(i, nact_ref[0] - 1)], 0, 0)

    grid_spec = pltpu.PrefetchScalarGridSpec(
        num_scalar_prefetch=3,
        grid=(n_tiles,),
        in_specs=[pl.BlockSpec((EXPERT_TILE, PACK_WORDS), tile_map),
                  pl.BlockSpec(memory_space=pl.ANY),
                  pl.BlockSpec((1, 1, 2 * D_FF), exp_map),
                  pl.BlockSpec(memory_space=pl.ANY),
                  pl.BlockSpec((1, 1, d), exp_map)],
        out_specs=pl.BlockSpec((EXPERT_TILE, PACK_WORDS), lambda i, te_ref, nact_ref, valid_ref: (i, 0)),
        scratch_shapes=[pltpu.VMEM((2, d, 2 * D_FF), F32), pltpu.VMEM((2, D_FF, d), F32),
                        pltpu.SMEM((1,), jnp.int32), pltpu.SemaphoreType.DMA((2, 2))],
    )
    return pl.pallas_call(
        _expert_kernel,
        grid_spec=grid_spec,
        out_shape=jax.ShapeDtypeStruct(xs.shape, jnp.uint32),
        compiler_params=_cparams(("arbitrary",)),
        name="expert_mlp",
    )(te, nact, valid, xs, w_gu, b_gu, w_dn, b_dn)


def _combine_kernel(gfin_ref, *refs):
    out_ref = refs[-1]
    for s in range(COMBINE_SUB):
        h1_ref, info_ref, yg_ref = refs[3 * s:3 * s + 3]
        info = info_ref[...]
        h2 = h1_ref[...]
        for kk in range(TOP_K):
            h2 = h2 + info[:, kk:kk + 1] * _unpack_rows(yg_ref[kk])
        ms = jnp.mean(h2 * h2, axis=-1, keepdims=True)
        out_ref[s * CHUNK:(s + 1) * CHUNK, :] = h2 * lax.rsqrt(ms + RMS_EPS) * gfin_ref[...]


def _combine(h1, info, g_final, yg, n_batch, seq):
    d = h1.shape[1]
    chunks = seq // CHUNK

    def padded(s):
        def index(j):
            c = j * COMBINE_SUB + s
            return (c // chunks) * (chunks + 1) + c % chunks + 1
        return index

    in_specs = [pl.BlockSpec((1, d), lambda j: (0, 0))]
    operands = [g_final]
    for s in range(COMBINE_SUB):
        chunk_of = padded(s)
        in_specs += [pl.BlockSpec((CHUNK, d), lambda j, f=chunk_of: (f(j), 0)),
                     pl.BlockSpec((CHUNK, LANES), lambda j, f=chunk_of: (f(j), 0)),
                     pl.BlockSpec((TOP_K, CHUNK, PACK_WORDS), lambda j, f=chunk_of: (0, f(j), 0))]
        operands += [h1, info, yg]
    return pl.pallas_call(
        _combine_kernel,
        grid=(n_batch * chunks // COMBINE_SUB,),
        in_specs=in_specs,
        out_specs=pl.BlockSpec((COMBINE_SUB * CHUNK, d), lambda j: (j, 0)),
        out_shape=jax.ShapeDtypeStruct((n_batch * seq, d), F32),
        compiler_params=_cparams(("parallel",)),
        name="combine",
    )(*operands)


def kernel(x, meta_tokens, lb_logits, g_mix, w_in, w_dw, b_dw, ln_g, ln_b, w_conv_out, b_conv_out,
           g_onorm, w_rnn_out, w_o, g_ffn, w_router, b_router, w_gate_up, b_gate_up, w_down, b_down,
           g_final):
    n_batch, seq, d = x.shape
    assert d == D_MODEL and w_in.shape[0] == 1, "single-layer block with D_MODEL features"
    assert seq % (CHUNK * COMBINE_SUB) == 0
    seq_pad = CHUNK + seq
    tp = n_batch * seq_pad
    assert tp % ROW_TILE == 0

    x2 = x.reshape(n_batch * seq, d)
    meta = meta_tokens.astype(x.dtype)
    vec = lambda a: a.reshape(1, -1).astype(F32)

    acts, logf = _in_proj(x2, meta, seq, vec(g_mix[0]), lb_logits.astype(F32), w_in[0])
    w_taps = w_dw[0].astype(F32).reshape(CONV_WIDTH, d // LANES, LANES).transpose(1, 0, 2)
    ap = _conv_branch(acts, w_taps, vec(b_dw[0]), vec(ln_g[0]), vec(ln_b[0]),
                      w_conv_out[0], vec(b_conv_out[0]))
    on = _hgrn2_scan(acts, logf, vec(g_onorm[0]), n_batch)
    w_router_pad = jnp.pad(w_router[0].astype(F32), ((0, 0), (0, LANES - N_EXPERTS)))
    b_router_pad = jnp.pad(vec(b_router[0]), ((0, 0), (0, LANES - N_EXPERTS)))
    h1, u2p, info, route, cnt = _merge_route(x2, meta, seq, on, ap, acts, w_rnn_out[0], w_o[0],
                                      vec(g_ffn[0]), w_router_pad, b_router_pad)

    n_tiles_max = -(-(tp * TOP_K + N_EXPERTS * (EXPERT_TILE - 1)) // EXPERT_TILE)
    dest, te = _slots(route, cnt, n_tiles_max)
    dest_flat = dest[:TOP_K].reshape(-1)
    xs = _dispatch(dest_flat, u2p, n_tiles_max * EXPERT_TILE)
    ys = _expert_mlp(te[0, :n_tiles_max], te[1, :1], te[2, :n_tiles_max], xs, w_gate_up[0],
                     b_gate_up[0].reshape(N_EXPERTS, 1, -1).astype(F32), w_down[0],
                     b_down[0].reshape(N_EXPERTS, 1, -1).astype(F32))
    yg = _gather_expert_rows(dest_flat, ys)
    out = _combine(h1, info, vec(g_final), yg, n_batch, seq)
    return out.reshape(n_batch, seq, d)
```

```python
import functools

import jax
import jax.numpy as jnp
from jax import lax
from jax.experimental import pallas as pl
from jax.experimental.pallas import tpu as pltpu
from jax.experimental.pallas import tpu_sc as plsc

F32 = jnp.float32
BF16 = jnp.bfloat16

D_MODEL = 1024
N_META = 16
CHUNK = 128
CHUNK_PAD = CHUNK - N_META
CONV_WIDTH = 31
HEAD_DIM = 128
N_HEADS = D_MODEL // HEAD_DIM
N_EXPERTS = 32
TOP_K = 4
D_FF = D_MODEL
SWIGLU_LIMIT = 7.0
SWIGLU_ALPHA = 1.702
RMS_EPS = 1e-6
LN_EPS = 1e-5

ACT_AGLU, ACT_Q, ACT_K, ACT_V, ACT_OG, ACT_SGA, ACT_SGB = range(7)
N_ACTS = 7

LANES = 128
SUBLANES = 8
SUB_BLOCK = 32
N_SUB = CHUNK // SUB_BLOCK
HALO = 32
ROW_TILE = 640
IN_TILE = 320
SCAN_CHUNKS = 5
MERGE_PARTS = 2
SLOTS_STEPS = 5
ROUTE_ROWS = 16
CONV_ROWS = 64
CONV_SUMS = 4
EXPERT_TILE = 1024
EXPERT_PATH_ROWS = tuple(EXPERT_TILE >> k for k in (3, 2, 1, 0))
SC_MAX_INDICES = 128
SC_ROW_BUFFER_BYTES = 448 * 1024
SCATTER_RING = 2
GATHER_RING = 2
GATHER_LAG = 1
COMBINE_SUB = 8
NEG_BIG = -1e30
VMEM_LIMIT = 56 * 1024 * 1024


def _sigmoid(x):
    return 1.0 / (1.0 + jnp.exp(-x))


def _cparams(sem):
    return pltpu.CompilerParams(dimension_semantics=sem, vmem_limit_bytes=VMEM_LIMIT)


def _const_spec(shape):
    nd = len(shape)
    return pl.BlockSpec(shape, lambda *_: (0,) * nd)


PACK_WORDS = D_MODEL // 2
HIGH_HALF = 0xFFFF0000


def _pack_rows(x):
    lo = lax.bitcast_convert_type(x[:, :PACK_WORDS].astype(BF16).astype(F32), jnp.uint32)
    hi = lax.bitcast_convert_type(x[:, PACK_WORDS:].astype(BF16).astype(F32), jnp.uint32)
    return (lo >> 16) | (hi & jnp.uint32(HIGH_HALF))


def _unpack_rows(w):
    lo = lax.bitcast_convert_type(w << 16, F32)
    hi = lax.bitcast_convert_type(w & jnp.uint32(HIGH_HALF), F32)
    return jnp.concatenate([lo, hi], axis=1)


def _residual_tile(x_ref, meta_ref, is_first):
    x = x_ref[...]
    tm, d = x.shape
    prefix = jnp.concatenate([jnp.zeros((CHUNK_PAD, d), x.dtype), meta_ref[...]], axis=0)
    first = jnp.concatenate([prefix, x[:tm - CHUNK, :]], axis=0)
    return jnp.where(is_first, first, x)


def _residual_spec(tm, d, seq, tiles_per_batch, n_tiles):
    def start(i):
        i = jnp.minimum(i, n_tiles - 1)
        b, t = i // tiles_per_batch, i % tiles_per_batch
        return (pl.multiple_of(b * seq + jnp.maximum(t * tm - CHUNK, 0), SUBLANES), 0)
    return pl.BlockSpec((pl.Element(tm), pl.Element(d)), start)


def _in_proj_kernel(x_ref, meta_ref, g_ref, lbl_ref, w_ref, acts_ref, logf_ref, *, tiles_per_batch):
    d = D_MODEL

    def put(group, value):
        acts_ref[:, group * d:(group + 1) * d] = value.astype(acts_ref.dtype)

    h = _residual_tile(x_ref, meta_ref, pl.program_id(0) % tiles_per_batch == 0)
    ms = jnp.mean(h * h, axis=-1, keepdims=True)
    u = (h * lax.rsqrt(ms + RMS_EPS) * g_ref[...]).astype(BF16)

    def proj(j):
        return jnp.dot(u, w_ref[:, j * d:(j + 1) * d].astype(BF16), preferred_element_type=F32)

    put(ACT_AGLU, proj(0) * _sigmoid(proj(1)))
    zq = proj(2)
    put(ACT_Q, zq * _sigmoid(zq))
    lbl = lbl_ref[...]
    e = jnp.exp(lbl - jnp.max(lbl, axis=0, keepdims=True))
    lb = e[0:1, :] / jnp.sum(e, axis=0, keepdims=True)
    s = _sigmoid(proj(3))
    logf_ref[...] = jnp.log(lb + (1.0 - lb) * s)
    put(ACT_K, (1.0 - lb) * (1.0 - s))
    zg = proj(5)
    put(ACT_OG, zg * _sigmoid(zg))
    put(ACT_SGA, _sigmoid(proj(6)))
    put(ACT_SGB, _sigmoid(proj(7)))
    put(ACT_V, proj(4))


def _in_proj(x2, meta, seq, g_mix, lb_logits, w_in):
    d = x2.shape[1]
    n_batch = x2.shape[0] // seq
    tm = IN_TILE
    tpb = (seq + CHUNK) // tm
    tp = n_batch * (seq + CHUNK)
    assert tpb * tm == seq + CHUNK and tm > CHUNK
    row = pl.BlockSpec((tm, d), lambda i: (i, 0))
    return pl.pallas_call(
        functools.partial(_in_proj_kernel, tiles_per_batch=tpb),
        grid=(tp // tm,),
        in_specs=[_residual_spec(tm, d, seq, tpb, tp // tm), _const_spec(meta.shape), _const_spec((1, d)),
                  _const_spec(lb_logits.shape),
                  pl.BlockSpec(w_in.shape, lambda i: (0, 0), pipeline_mode=pl.Buffered(1))],
        out_specs=[pl.BlockSpec((tm, N_ACTS * d), lambda i: (i, 0)), row],
        out_shape=[jax.ShapeDtypeStruct((tp, N_ACTS * d), BF16), jax.ShapeDtypeStruct((tp, d), F32)],
        compiler_params=_cparams(("parallel",)),
        name="in_proj",
    )(x2, meta, g_mix, lb_logits, w_in)


def _conv_kernel(halo_ref, cur_ref, wdw_ref, bdw_ref, lng_ref, lnb_ref, wout_ref, bout_ref, sga_ref,
                 out_ref, win_ref, acc_ref, shift_ref):
    tm = cur_ref.shape[0]
    win_ref[0:HALO, :] = halo_ref[...].astype(F32)
    win_ref[HALO:, :] = cur_ref[...].astype(F32)
    first_tap = HALO - (CONV_WIDTH - 1)
    for c in range(D_MODEL // LANES):
        lanes = slice(c * LANES, (c + 1) * LANES)
        n_rows = tm + HALO - SUBLANES
        for s in range(1, SUBLANES):
            shift_ref[s - 1, 0:n_rows, :] = win_ref[pl.ds(s, n_rows), lanes]
        bias = jnp.broadcast_to(bdw_ref[:, lanes], (CONV_ROWS, LANES))
        groups = CONV_ROWS // SUBLANES

        def chunk(r, carry, c=c, lanes=lanes, bias=bias):
            r0 = pl.multiple_of(r * CONV_ROWS, CONV_ROWS)
            accs = [bias.reshape(groups, SUBLANES, LANES)] + [None] * (CONV_SUMS - 1)
            for j in range(CONV_WIDTH):
                off = first_tap + j
                rows = pl.ds(r0 + off - off % SUBLANES, CONV_ROWS)
                w_j = wdw_ref[c, pl.ds(j, SUBLANES, stride=0), :]
                window = shift_ref[off % SUBLANES - 1, rows, :] if off % SUBLANES else win_ref[rows, lanes]
                term = w_j[None] * window.reshape(groups, SUBLANES, LANES)
                accs[j % CONV_SUMS] = term if accs[j % CONV_SUMS] is None else accs[j % CONV_SUMS] + term
            while len(accs) > 1:
                accs = [a + b for a, b in zip(accs[0::2], accs[1::2])]
            acc_ref[pl.ds(r0, CONV_ROWS), lanes] = accs[0].reshape(CONV_ROWS, LANES)
            return carry

        lax.fori_loop(0, tm // CONV_ROWS, chunk, 0)
    a = acc_ref[...]
    mu = jnp.mean(a, axis=-1, keepdims=True)
    ac = a - mu
    var = jnp.mean(ac * ac, axis=-1, keepdims=True)
    y = ac * lax.rsqrt(var + LN_EPS) * lng_ref[...] + lnb_ref[...]
    y = y * _sigmoid(y)
    o = jnp.dot(y.astype(BF16), wout_ref[...].astype(BF16), preferred_element_type=F32) + bout_ref[...]
    out_ref[...] = (sga_ref[...].astype(F32) * o).astype(out_ref.dtype)


def _conv_branch(acts, w_dw, b_dw, ln_g, ln_b, w_out, b_out):
    tp, d = acts.shape[0], D_MODEL
    tm = ROW_TILE
    per = tm // HALO
    row = pl.BlockSpec((tm, d), lambda i: (i, 0))
    glu = pl.BlockSpec((tm, d), lambda i: (i, ACT_AGLU))
    gate = pl.BlockSpec((tm, d), lambda i: (i, ACT_SGA))
    halo = pl.BlockSpec((HALO, d), lambda i: (jnp.maximum(i * per - 1, 0), ACT_AGLU))
    vec = _const_spec((1, d))
    return pl.pallas_call(
        _conv_kernel,
        grid=(tp // tm,),
        in_specs=[halo, glu, _const_spec(w_dw.shape), vec, vec, vec, _const_spec((d, d)), vec, gate],
        out_specs=row,
        out_shape=jax.ShapeDtypeStruct((tp, d), BF16),
        scratch_shapes=[pltpu.VMEM((tm + HALO, d), F32), pltpu.VMEM((tm, d), F32),
                        pltpu.VMEM((SUBLANES - 1, tm + HALO - SUBLANES, LANES), F32)],
        compiler_params=_cparams(("parallel",)),
        name="conv_branch",
    )(acts, acts, w_dw, b_dw, ln_g, ln_b, w_out, b_out, acts)


def _split_bf16(x):
    hi = x.astype(BF16)
    return hi, (x - hi.astype(F32)).astype(BF16)


def _nt_dot(a, b):
    return lax.dot_general(a, b, (((1,), (1,)), ((), ())), preferred_element_type=F32)


def _tn_dot(a, b):
    return lax.dot_general(a, b, (((0,), (0,)), ((), ())), preferred_element_type=F32)


def _scan_kernel(q_ref, k_ref, v_ref, lf_ref, og_ref, gon_ref, o_ref, st_ref):
    @pl.when(pl.program_id(1) == 0)
    def _():
        st_ref[...] = jnp.zeros_like(st_ref)

    c = CHUNK
    row = lax.broadcasted_iota(jnp.int32, (c, c), 0)
    col = lax.broadcasted_iota(jnp.int32, (c, c), 1)
    causal = col <= row
    diag_mask = jnp.logical_and(causal, row // SUB_BLOCK == col // SUB_BLOCK)
    tri = causal.astype(BF16)

    def bcast_rows(rows):
        return jnp.concatenate([jnp.broadcast_to(r, (SUB_BLOCK, HEAD_DIM)) for r in rows], axis=0)

    heads = [slice(h * HEAD_DIM, (h + 1) * HEAD_DIM) for h in range(N_HEADS)]
    chunks = [slice(ci * c, (ci + 1) * c) for ci in range(SCAN_CHUNKS)]

    bcums = []
    for cs in chunks:
        lf = lf_ref[cs, :]
        lf_hi = lf.astype(BF16)
        lf_mid, lf_lo = _split_bf16(lf - lf_hi.astype(F32))
        bcums.append(jnp.dot(tri, lf_hi, preferred_element_type=F32)
                     + (jnp.dot(tri, lf_mid, preferred_element_type=F32)
                        + jnp.dot(tri, lf_lo, preferred_element_type=F32)))
    operands = {}
    for ci, cs in enumerate(chunks):
        for h, hs in enumerate(heads):
            b = bcums[ci][:, hs]
            q = q_ref[cs, hs].astype(F32)
            k = k_ref[cs, hs].astype(F32)
            ends = [b[i * SUB_BLOCK + SUB_BLOCK - 1:i * SUB_BLOCK + SUB_BLOCK, :] for i in range(N_SUB)]
            mids = [b[i * SUB_BLOCK + SUB_BLOCK // 2 - 1:i * SUB_BLOCK + SUB_BLOCK // 2, :] for i in range(N_SUB)]
            mid_full = bcast_rows(mids)
            b_last = ends[-1]
            zero_row = jnp.zeros_like(b_last)
            qm = q * jnp.exp(b - mid_full)
            km = k * jnp.exp(mid_full - b)
            q_parts, k_parts = [], []
            for j in range(N_SUB - 1):
                q_rows = [jnp.exp(mids[i] - ends[j]) if i > j else zero_row for i in range(N_SUB)]
                k_rows = [jnp.exp(ends[j] - mids[j]) if i == j else zero_row for i in range(N_SUB)]
                q_parts.append((qm * bcast_rows(q_rows)).astype(BF16))
                k_parts.append((km * bcast_rows(k_rows)).astype(BF16))
            q_in = (qm * bcast_rows([jnp.exp(m) for m in mids])).astype(BF16)
            k_out = (km * bcast_rows([jnp.exp(b_last - m) for m in mids])).astype(BF16)
            operands[ci, h] = (qm.astype(BF16), km.astype(BF16), jnp.concatenate(q_parts, axis=1),
                               jnp.concatenate(k_parts, axis=1), q_in, k_out, jnp.exp(b_last))
    products = {}
    for ci, cs in enumerate(chunks):
        for h, hs in enumerate(heads):
            qm, km, q_cat, k_cat, _, k_out, _ = operands[ci, h]
            products[ci, h] = (_nt_dot(qm, km), _nt_dot(q_cat, k_cat), _tn_dot(v_ref[cs, hs], k_out))
    within = {}
    for ci, cs in enumerate(chunks):
        for h, hs in enumerate(heads):
            same_block, earlier_blocks, _ = products[ci, h]
            scores = jnp.where(diag_mask, same_block, 0.0) + earlier_blocks
            within[ci, h] = jnp.dot(scores.astype(BF16), v_ref[cs, hs], preferred_element_type=F32)
    states = [st_ref[h] for h in range(N_HEADS)]
    outs = {}
    for ci in range(SCAN_CHUNKS):
        for h in range(N_HEADS):
            outs[ci, h] = within[ci, h] + _nt_dot(operands[ci, h][4], states[h].astype(BF16))
            states[h] = states[h] * operands[ci, h][6] + products[ci, h][2]
    for h in range(N_HEADS):
        st_ref[h] = states[h]
    for ci, cs in enumerate(chunks):
        for h, hs in enumerate(heads):
            o = outs[ci, h]
            ms = jnp.mean(o * o, axis=-1, keepdims=True)
            on = o * lax.rsqrt(ms + RMS_EPS) * gon_ref[:, hs]
            o_ref[cs, hs] = (on * og_ref[cs, hs].astype(F32)).astype(o_ref.dtype)


def _hgrn2_scan(acts, logf, g_onorm, n_batch):
    tp, d = logf.shape
    rows = SCAN_CHUNKS * CHUNK
    n_chunks = tp // n_batch // rows
    assert n_chunks * rows * n_batch == tp
    col = lambda group: pl.BlockSpec((rows, d), lambda b, c: (b * n_chunks + c, group))
    blk = col(0)
    return pl.pallas_call(
        _scan_kernel,
        grid=(n_batch, n_chunks),
        in_specs=[col(ACT_Q), col(ACT_K), col(ACT_V), blk, col(ACT_OG), pl.BlockSpec((1, d), lambda b, c: (0, 0))],
        out_specs=blk,
        out_shape=jax.ShapeDtypeStruct((tp, d), BF16),
        scratch_shapes=[pltpu.VMEM((N_HEADS, HEAD_DIM, HEAD_DIM), F32)],
        compiler_params=_cparams(("arbitrary", "arbitrary")),
        name="hgrn2_scan",
    )(acts, acts, acts, logf, acts, g_onorm)


def _merge_route_kernel(x_ref, meta_ref, on_ref, ap_ref, sgb_ref, wrnn_ref, wo_ref, gffn_ref, wr_ref, br_ref,
                        h1_ref, u2_ref, info_ref, route_ref, cnt_ref, tri_ref, carry_ref, colcnt_ref, logits_ref, hres_ref,
                        *, tiles_per_batch, n_tiles):
    tm = on_ref.shape[0]
    i = pl.program_id(0)
    hres_ref[...] = _residual_tile(x_ref, meta_ref, jnp.minimum(i, n_tiles - 1) % tiles_per_batch == 0)

    @pl.when(i == 0)
    def _():
        r_i = lax.broadcasted_iota(jnp.int32, (tm, tm), 0)
        c_i = lax.broadcasted_iota(jnp.int32, (tm, tm), 1)
        tri_ref[...] = (c_i < r_i).astype(BF16)
        carry_ref[...] = jnp.zeros_like(carry_ref)
        colcnt_ref[...] = jnp.zeros_like(colcnt_ref)
        logits_ref[...] = jnp.zeros_like(logits_ref)

    routed = i > 0
    lane = lax.broadcasted_iota(jnp.int32, (tm, LANES), 1)
    cur = jnp.where(lane < N_EXPERTS, logits_ref[...], NEG_BIG)
    vals, idxs, sels = [], [], []

    def topk_round(cur):
        m = jnp.max(cur, axis=-1, keepdims=True)
        idx = jnp.min(jnp.where(cur == m, lane, LANES), axis=-1, keepdims=True)
        sel = lane == idx
        vals.append(m)
        idxs.append(idx)
        sels.append(sel)
        return jnp.where(sel, 2.0 * NEG_BIG, cur)

    w_hi, w_lo = _split_bf16(wr_ref[...])
    w_hi_lo = jnp.concatenate([w_hi, w_lo], axis=1)
    part = tm // MERGE_PARTS
    rows = [slice(p * part, (p + 1) * part) for p in range(MERGE_PARTS)]
    w_rnn = wrnn_ref[...].astype(BF16)
    w_out = wo_ref[...].astype(BF16)
    rs = [jnp.dot(on_ref[rw, :], w_rnn, preferred_element_type=F32) for rw in rows]
    cur = topk_round(cur)
    ys = [(ap_ref[rw, :].astype(F32) + sgb_ref[rw, :].astype(F32) * r).astype(BF16) for rw, r in zip(rows, rs)]
    h1s = [hres_ref[rw, :] + jnp.dot(y, w_out, preferred_element_type=F32) for rw, y in zip(rows, ys)]
    cur = topk_round(cur)
    u2s = []
    for rw, h1 in zip(rows, h1s):
        h1_ref[rw, :] = h1
        ms = jnp.mean(h1 * h1, axis=-1, keepdims=True)
        u2 = h1 * lax.rsqrt(ms + RMS_EPS) * gffn_ref[...]
        u2_ref[rw, :] = _pack_rows(u2)
        u2s.append(u2)
    cur = topk_round(cur)
    new_logits = []
    for u2 in u2s:
        u_hi, u_lo = _split_bf16(u2)
        both = jnp.dot(u_hi, w_hi_lo, preferred_element_type=F32)
        new_logits.append(both[:, :LANES] + (jnp.dot(u_lo, w_hi, preferred_element_type=F32) + both[:, LANES:])
                          + br_ref[...])
    cur = topk_round(cur)
    assert len(vals) == TOP_K
    exps = [jnp.exp(vk - vals[0]) for vk in vals]
    den = exps[0] + exps[1] + exps[2] + exps[3]
    onehot = jnp.logical_or(jnp.logical_or(sels[0], sels[1]), jnp.logical_or(sels[2], sels[3]))
    onehot = jnp.logical_and(onehot, routed)
    onehot_bf = onehot.astype(BF16)
    rank_all = jnp.dot(tri_ref[...], onehot_bf, preferred_element_type=F32) + carry_ref[...]
    carry_ref[...] += jnp.sum(onehot.astype(F32), axis=0, keepdims=True)
    colcnt_ref[...] += _tn_dot(onehot_bf, jnp.ones((tm, LANES), BF16))
    info = jnp.zeros((tm, LANES), F32)
    for kk in range(TOP_K):
        rank_k = jnp.sum(jnp.where(sels[kk], rank_all, 0.0), axis=-1, keepdims=True)
        info = jnp.where(lane == kk, exps[kk] / den, info)
        info = jnp.where(lane == TOP_K + kk, idxs[kk].astype(F32), info)
        info = jnp.where(lane == 2 * TOP_K + kk, rank_k, info)
    info_ref[...] = info
    route_ref[...] = info.T[0:ROUTE_ROWS, :]
    cnt_ref[...] = colcnt_ref[...]
    logits_ref[...] = jnp.concatenate(new_logits, axis=0)


def _merge_route(x2, meta, seq, on, ap, acts, w_rnn, w_o, g_ffn, w_router_pad, b_router_pad):
    tp, d = on.shape
    tm = ROW_TILE
    n_tiles = tp // tm
    tpb = (seq + CHUNK) // tm
    assert tpb * tm == seq + CHUNK
    this_tile = lambda i: (jnp.minimum(i, n_tiles - 1), 0)
    prev_tile = lambda i: (jnp.maximum(i - 1, 0), 0)
    row = pl.BlockSpec((tm, d), this_tile)
    vec = _const_spec((1, d))
    return pl.pallas_call(
        functools.partial(_merge_route_kernel, tiles_per_batch=tpb, n_tiles=n_tiles),
        grid=(n_tiles + 1,),
        in_specs=[_residual_spec(tm, d, seq, tpb, n_tiles), _const_spec(meta.shape), row, row,
                  pl.BlockSpec((tm, d), lambda i: (jnp.minimum(i, n_tiles - 1), ACT_SGB)),
                  _const_spec((d, d)), _const_spec((d, d)), vec, _const_spec((d, LANES)), _const_spec((1, LANES))],
        out_specs=[row, pl.BlockSpec((tm, PACK_WORDS), this_tile),
                   pl.BlockSpec((tm, LANES), prev_tile),
                   pl.BlockSpec((ROUTE_ROWS, tm), lambda i: (0, jnp.maximum(i - 1, 0))), _const_spec((LANES, LANES))],
        out_shape=[jax.ShapeDtypeStruct((tp, d), F32), jax.ShapeDtypeStruct((tp, PACK_WORDS), jnp.uint32),
                   jax.ShapeDtypeStruct((tp, LANES), F32), jax.ShapeDtypeStruct((ROUTE_ROWS, tp), F32),
                   jax.ShapeDtypeStruct((LANES, LANES), F32)],
        scratch_shapes=[pltpu.VMEM((tm, tm), BF16), pltpu.VMEM((1, LANES), F32),
                        pltpu.VMEM((LANES, LANES), F32), pltpu.VMEM((tm, LANES), F32), pltpu.VMEM((tm, d), F32)],
        compiler_params=_cparams(("arbitrary",)),
        name="merge_route",
    )(x2, meta, on, ap, acts, w_rnn, w_o, g_ffn, w_router_pad, b_router_pad)


def _slots_kernel(route_ref, cnt_ref, dest_ref, te_ref, pstart_ref):
    tl = route_ref.shape[1]
    n_tile_lanes = te_ref.shape[1]

    @pl.when(pl.program_id(0) == 0)
    def _():
        r_i = lax.broadcasted_iota(jnp.int32, (LANES, LANES), 0)
        c_i = lax.broadcasted_iota(jnp.int32, (LANES, LANES), 1)
        cnt = cnt_ref[...]
        tiles = jnp.floor((cnt + (EXPERT_TILE - 1)) / EXPERT_TILE)
        pend_col = jnp.dot((c_i <= r_i).astype(F32), tiles, preferred_element_type=F32,
                           precision=lax.Precision.HIGHEST)
        pstart_ref[...] = pend_col - tiles
        tile_id = lax.broadcasted_iota(jnp.int32, (LANES, n_tile_lanes), 1).astype(F32)
        exp_id = lax.broadcasted_iota(jnp.int32, (LANES, n_tile_lanes), 0)
        pend_wide = jnp.concatenate([pend_col] * (n_tile_lanes // LANES), axis=1)
        below = jnp.logical_and(pend_wide <= tile_id, exp_id < N_EXPERTS)
        te = jnp.minimum(jnp.sum(below.astype(F32), axis=0, keepdims=True), N_EXPERTS - 1.0)
        n_active = pend_col[N_EXPERTS - 1:N_EXPERTS, 0:1]
        owner = exp_id.astype(F32) == te
        cnt_wide = jnp.concatenate([cnt] * (n_tile_lanes // LANES), axis=1)
        first_tile = pend_wide - jnp.concatenate([tiles] * (n_tile_lanes // LANES), axis=1)
        left = jnp.sum(jnp.where(owner, cnt_wide - (tile_id - first_tile) * EXPERT_TILE, 0.0), axis=0, keepdims=True)
        valid = jnp.clip(left, 0.0, float(EXPERT_TILE))
        sub_t = lax.broadcasted_iota(jnp.int32, (SUBLANES, n_tile_lanes), 0)
        out = jnp.where(sub_t == 0, jnp.broadcast_to(te, (SUBLANES, n_tile_lanes)),
                        jnp.where(sub_t == 1, jnp.broadcast_to(n_active, (SUBLANES, n_tile_lanes)),
                                  jnp.broadcast_to(valid, (SUBLANES, n_tile_lanes))))
        te_ref[...] = out.astype(jnp.int32)

    route = route_ref[...]
    expert = lax.broadcasted_iota(jnp.int32, (N_EXPERTS, tl), 0)
    first_slot = jnp.concatenate([pstart_ref[0:N_EXPERTS, :]] * (tl // LANES), axis=1) * EXPERT_TILE
    rows = []
    for kk in range(TOP_K):
        idx_k = route[TOP_K + kk:TOP_K + kk + 1, :].astype(jnp.int32)
        start_k = jnp.sum(jnp.where(expert == idx_k, first_slot, 0.0), axis=0, keepdims=True)
        rows.append(start_k + route[2 * TOP_K + kk:2 * TOP_K + kk + 1, :])
    rows.append(jnp.zeros((SUBLANES - TOP_K, tl), F32))
    dest_ref[...] = jnp.concatenate(rows, axis=0).astype(jnp.int32)


def _slots(route, cnt, n_tiles_max):
    tp = route.shape[1]
    tl = tp // SLOTS_STEPS
    assert tl * SLOTS_STEPS == tp and tl % LANES == 0
    n_tile_lanes = -(-n_tiles_max // LANES) * LANES
    return pl.pallas_call(
        _slots_kernel,
        grid=(SLOTS_STEPS,),
        in_specs=[pl.BlockSpec((ROUTE_ROWS, tl), lambda i: (0, i)), _const_spec((LANES, LANES))],
        out_specs=[pl.BlockSpec((SUBLANES, tl), lambda i: (0, i)), _const_spec((8, n_tile_lanes))],
        out_shape=[jax.ShapeDtypeStruct((SUBLANES, tp), jnp.int32),
                   jax.ShapeDtypeStruct((8, n_tile_lanes), jnp.int32)],
        scratch_shapes=[pltpu.VMEM((LANES, LANES), F32)],
        compiler_params=_cparams(("arbitrary",)),
        name="slots",
    )(route, cnt)


def _sc_workers():
    sc = plsc.get_sparse_core_info()
    return sc.num_cores, sc.num_cores * sc.num_subcores


def _sc_split(tp, ring):
    n_workers = _sc_workers()[1]
    per_worker = tp // n_workers
    assert per_worker * n_workers == tp
    limit = min(SC_MAX_INDICES, SC_ROW_BUFFER_BYTES // (ring * PACK_WORDS * 4))
    return per_worker, max(c for c in range(SUBLANES, limit + 1, SUBLANES) if per_worker % c == 0)


def _sc_first_token(per_worker):
    return pl.multiple_of((lax.axis_index("s") * _sc_workers()[0] + lax.axis_index("c")) * per_worker, SUBLANES)


def _sc_ring_kernel(out_type, n_index_rows, chunk, ring):
    return functools.partial(
        pl.kernel, mesh=plsc.VectorSubcoreMesh(core_axis_name="c", subcore_axis_name="s"), out_type=out_type,
        scratch_types=[pltpu.VMEM((n_index_rows, chunk), jnp.int32), pltpu.VMEM((ring, chunk, PACK_WORDS), jnp.uint32),
                       pltpu.SemaphoreType.DMA, pltpu.SemaphoreType.DMA((ring,)), pltpu.SemaphoreType.DMA((ring,))])


def _wait_all(copies):
    for c in copies:
        c.wait()


def _dispatch(dest_flat, u2p, n_slots):
    tp = u2p.shape[0]
    per_worker, ch = _sc_split(tp, SCATTER_RING)
    n_chunks = per_worker // ch

    @_sc_ring_kernel(jax.ShapeDtypeStruct((n_slots, PACK_WORDS), jnp.uint32), n_chunks * TOP_K, ch, SCATTER_RING)
    def scatter_rows(u2_hbm, dest_hbm, xs_hbm, idx_v, rows_v, sem_idx, sem_load, sem_scatter):
        base = _sc_first_token(per_worker)
        _wait_all([pltpu.async_copy(dest_hbm.at[pl.ds(kk * tp + base + c * ch, ch)], idx_v.at[c * TOP_K + kk], sem_idx)
                   for c in range(n_chunks) for kk in range(TOP_K)])

        def load(c):
            slot = c % SCATTER_RING
            return pltpu.async_copy(u2_hbm.at[pl.ds(base + c * ch, ch)], rows_v.at[slot], sem_load.at[slot])

        def scatter(c):
            slot = c % SCATTER_RING
            return [pltpu.async_copy(rows_v.at[slot], xs_hbm.at[idx_v.at[c * TOP_K + kk]], sem_scatter.at[slot])
                    for kk in range(TOP_K)]

        loads, scatters = {}, {}
        for c in range(n_chunks + 1):
            if c < n_chunks:
                if c >= SCATTER_RING:
                    _wait_all(scatters.pop(c - SCATTER_RING))
                loads[c] = load(c)
            if c >= 1:
                loads.pop(c - 1).wait()
                scatters[c - 1] = scatter(c - 1)
        for copies in scatters.values():
            _wait_all(copies)

    return scatter_rows(u2p, dest_flat)


def _gather_expert_rows(dest_flat, ys):
    tp = dest_flat.shape[0] // TOP_K
    per_worker, ch = _sc_split(tp, GATHER_RING)
    units = [(c, kk) for c in range(per_worker // ch) for kk in range(TOP_K)]

    @_sc_ring_kernel(jax.ShapeDtypeStruct((TOP_K, tp, PACK_WORDS), jnp.uint32), len(units), ch, GATHER_RING)
    def gather_rows(ys_hbm, dest_hbm, out_hbm, idx_v, rows_v, sem_idx, sem_gather, sem_write):
        base = _sc_first_token(per_worker)
        _wait_all([pltpu.async_copy(dest_hbm.at[pl.ds(kk * tp + base + c * ch, ch)], idx_v.at[u], sem_idx)
                   for u, (c, kk) in enumerate(units)])

        def gather(u):
            slot = u % GATHER_RING
            return pltpu.async_copy(ys_hbm.at[idx_v.at[u]], rows_v.at[slot], sem_gather.at[slot])

        def write(u):
            c, kk = units[u]
            slot = u % GATHER_RING
            return pltpu.async_copy(rows_v.at[slot], out_hbm.at[kk, pl.ds(base + c * ch, ch)], sem_write.at[slot])

        gathers, writes = {}, {}
        for u in range(len(units) + GATHER_LAG):
            if u < len(units):
                if u >= GATHER_RING:
                    writes.pop(u - GATHER_RING).wait()
                gathers[u] = gather(u)
            if u >= GATHER_LAG:
                gathers.pop(u - GATHER_LAG).wait()
                writes[u - GATHER_LAG] = write(u - GATHER_LAG)
        _wait_all(writes.values())

    return gather_rows(ys, dest_flat)


def _expert_kernel(te_ref, nact_ref, valid_ref, x_ref, wgu_hbm, bgu_ref, wdn_hbm, bdn_ref, y_ref,
                   wgu_buf, wdn_buf, slot_ref, sem):
    i = pl.program_id(0)
    n_active = nact_ref[0]
    last_tile = pl.num_programs(0) - 1

    def fetch(expert, slot):
        return (pltpu.make_async_copy(wgu_hbm.at[expert], wgu_buf.at[slot], sem.at[slot, 0]),
                pltpu.make_async_copy(wdn_hbm.at[expert], wdn_buf.at[slot], sem.at[slot, 1]))

    @pl.when(i == 0)
    def _():
        slot_ref[0] = 1
        for copy in fetch(te_ref[0], 0):
            copy.start()

    @pl.when(i < n_active)
    def _():
        expert = te_ref[i]
        first_tile_of_expert = jnp.logical_or(i == 0, expert != te_ref[jnp.maximum(i - 1, 0)])

        @pl.when(first_tile_of_expert)
        def _():
            slot = 1 - slot_ref[0]
            slot_ref[0] = slot
            for copy in fetch(expert, slot):
                copy.wait()
            nxt = lax.while_loop(
                lambda j: jnp.logical_and(j < n_active, te_ref[jnp.minimum(j, last_tile)] == expert),
                lambda j: j + 1, i + 1)

            @pl.when(nxt < n_active)
            def _():
                for copy in fetch(te_ref[jnp.minimum(nxt, last_tile)], 1 - slot):
                    copy.start()

        slot = slot_ref[0]

        def mlp(rows):
            x = _unpack_rows(x_ref[0:rows, :]).astype(BF16)
            gu = jnp.dot(x, wgu_buf[slot].astype(BF16), preferred_element_type=F32) + bgu_ref[0]
            gate = jnp.minimum(gu[:, :D_FF], SWIGLU_LIMIT)
            up = jnp.clip(gu[:, D_FF:], -SWIGLU_LIMIT, SWIGLU_LIMIT)
            hdn = (up + 1.0) * (gate * _sigmoid(SWIGLU_ALPHA * gate))
            y = jnp.dot(hdn.astype(BF16), wdn_buf[slot].astype(BF16), preferred_element_type=F32) + bdn_ref[0]
            y_ref[0:rows, :] = _pack_rows(y)

        path = sum((valid_ref[i] > rows).astype(jnp.int32) for rows in EXPERT_PATH_ROWS[:-1])
        for k, rows in enumerate(EXPERT_PATH_ROWS):
            @pl.when(path == k)
            def _(rows=rows):
                mlp(rows)
                if rows < EXPERT_TILE:
                    y_ref[rows:, :] = jnp.zeros((EXPERT_TILE - rows, PACK_WORDS), y_ref.dtype)

    @pl.when(i >= n_active)
    def _():
        y_ref[...] = jnp.zeros_like(y_ref)


def _expert_mlp(te, nact, valid, xs, w_gu, b_gu, w_dn, b_dn):
    d = D_MODEL
    n_tiles = xs.shape[0] // EXPERT_TILE

    def tile_map(i, te_ref, nact_ref, valid_ref):
        return (jnp.minimum(i, nact_ref[0] - 1), 0)

    def exp_map(i, te_ref, nact_ref, valid_ref):
        return (te_ref[jnp.minimum(i, nact_ref[0] - 1)], 0, 0)

    grid_spec = pltpu.PrefetchScalarGridSpec(
        num_scalar_prefetch=3,
        grid=(n_tiles,),
        in_specs=[pl.BlockSpec((EXPERT_TILE, PACK_WORDS), tile_map),
                  pl.BlockSpec(memory_space=pl.ANY),
                  pl.BlockSpec((1, 1, 2 * D_FF), exp_map),
                  pl.BlockSpec(memory_space=pl.ANY),
                  pl.BlockSpec((1, 1, d), exp_map)],
        out_specs=pl.BlockSpec((EXPERT_TILE, PACK_WORDS), lambda i, te_ref, nact_ref, valid_ref: (i, 0)),
        scratch_shapes=[pltpu.VMEM((2, d, 2 * D_FF), F32), pltpu.VMEM((2, D_FF, d), F32),
                        pltpu.SMEM((1,), jnp.int32), pltpu.SemaphoreType.DMA((2, 2))],
    )
    return pl.pallas_call(
        _expert_kernel,
        grid_spec=grid_spec,
        out_shape=jax.ShapeDtypeStruct(xs.shape, jnp.uint32),
        compiler_params=_cparams(("arbitrary",)),
        name="expert_mlp",
    )(te, nact, valid, xs, w_gu, b_gu, w_dn, b_dn)


def _combine_kernel(gfin_ref, *refs):
    out_ref = refs[-1]
    for s in range(COMBINE_SUB):
        h1_ref, info_ref, yg_ref = refs[3 * s:3 * s + 3]
        info = info_ref[...]
        h2 = h1_ref[...]
        for kk in range(TOP_K):
            h2 = h2 + info[:, kk:kk + 1] * _unpack_rows(yg_ref[kk])
        ms = jnp.mean(h2 * h2, axis=-1, keepdims=True)
        out_ref[s * CHUNK:(s + 1) * CHUNK, :] = h2 * lax.rsqrt(ms + RMS_EPS) * gfin_ref[...]


def _combine(h1, info, g_final, yg, n_batch, seq):
    d = h1.shape[1]
    chunks = seq // CHUNK

    def padded(s):
        def index(j):
            c = j * COMBINE_SUB + s
            return (c // chunks) * (chunks + 1) + c % chunks + 1
        return index

    in_specs = [pl.BlockSpec((1, d), lambda j: (0, 0))]
    operands = [g_final]
    for s in range(COMBINE_SUB):
        chunk_of = padded(s)
        in_specs += [pl.BlockSpec((CHUNK, d), lambda j, f=chunk_of: (f(j), 0)),
                     pl.BlockSpec((CHUNK, LANES), lambda j, f=chunk_of: (f(j), 0)),
                     pl.BlockSpec((TOP_K, CHUNK, PACK_WORDS), lambda j, f=chunk_of: (0, f(j), 0))]
        operands += [h1, info, yg]
    return pl.pallas_call(
        _combine_kernel,
        grid=(n_batch * chunks // COMBINE_SUB,),
        in_specs=in_specs,
        out_specs=pl.BlockSpec((COMBINE_SUB * CHUNK, d), lambda j: (j, 0)),
        out_shape=jax.ShapeDtypeStruct((n_batch * seq, d), F32),
        compiler_params=_cparams(("parallel",)),
        name="combine",
    )(*operands)


def kernel(x, meta_tokens, lb_logits, g_mix, w_in, w_dw, b_dw, ln_g, ln_b, w_conv_out, b_conv_out,
           g_onorm, w_rnn_out, w_o, g_ffn, w_router, b_router, w_gate_up, b_gate_up, w_down, b_down,
           g_final):
    n_batch, seq, d = x.shape
    assert d == D_MODEL and w_in.shape[0] == 1, "single-layer block with D_MODEL features"
    assert seq % (CHUNK * COMBINE_SUB) == 0
    seq_pad = CHUNK + seq
    tp = n_batch * seq_pad
    assert tp % ROW_TILE == 0

    x2 = x.reshape(n_batch * seq, d)
    meta = meta_tokens.astype(x.dtype)
    vec = lambda a: a.reshape(1, -1).astype(F32)

    acts, logf = _in_proj(x2, meta, seq, vec(g_mix[0]), lb_logits.astype(F32), w_in[0])
    w_taps = w_dw[0].astype(F32).reshape(CONV_WIDTH, d // LANES, LANES).transpose(1, 0, 2)
    ap = _conv_branch(acts, w_taps, vec(b_dw[0]), vec(ln_g[0]), vec(ln_b[0]),
                      w_conv_out[0], vec(b_conv_out[0]))
    on = _hgrn2_scan(acts, logf, vec(g_onorm[0]), n_batch)
    w_router_pad = jnp.pad(w_router[0].astype(F32), ((0, 0), (0, LANES - N_EXPERTS)))
    b_router_pad = jnp.pad(vec(b_router[0]), ((0, 0), (0, LANES - N_EXPERTS)))
    h1, u2p, info, route, cnt = _merge_route(x2, meta, seq, on, ap, acts, w_rnn_out[0], w_o[0],
                                      vec(g_ffn[0]), w_router_pad, b_router_pad)

    n_tiles_max = -(-(tp * TOP_K + N_EXPERTS * (EXPERT_TILE - 1)) // EXPERT_TILE)
    dest, te = _slots(route, cnt, n_tiles_max)
    dest_flat = dest[:TOP_K].reshape(-1)
    xs = _dispatch(dest_flat, u2p, n_tiles_max * EXPERT_TILE)
    ys = _expert_mlp(te[0, :n_tiles_max], te[1, :1], te[2, :n_tiles_max], xs, w_gate_up[0],
                     b_gate_up[0].reshape(N_EXPERTS, 1, -1).astype(F32), w_down[0],
                     b_down[0].reshape(N_EXPERTS, 1, -1).astype(F32))
    yg = _gather_expert_rows(dest_flat, ys)
    out = _combine(h1, info, vec(g_final), yg, n_batch, seq)
    return out.reshape(n_batch, seq, d)
```

```python
import functools

import jax
import jax.numpy as jnp
from jax import lax
from jax.experimental import pallas as pl
from jax.experimental.pallas import tpu as pltpu
from jax.experimental.pallas import tpu_sc as plsc

F32 = jnp.float32
BF16 = jnp.bfloat16

D_MODEL = 1024
N_META = 16
CHUNK = 128
CHUNK_PAD = CHUNK - N_META
CONV_WIDTH = 31
HEAD_DIM = 128
N_HEADS = D_MODEL // HEAD_DIM
N_EXPERTS = 32
TOP_K = 4
D_FF = D_MODEL
SWIGLU_LIMIT = 7.0
SWIGLU_ALPHA = 1.702
RMS_EPS = 1e-6
LN_EPS = 1e-5

ACT_AGLU, ACT_Q, ACT_K, ACT_V, ACT_OG, ACT_SGA, ACT_SGB = range(7)
N_ACTS = 7

LANES = 128
SUBLANES = 8
SUB_BLOCK = 32
N_SUB = CHUNK // SUB_BLOCK
HALO = 32
ROW_TILE = 640
IN_TILE = 320
SCAN_CHUNKS = 5
NORM_PARTS = 2
MERGE_PARTS = 4
SLOTS_STEPS = 5
ROUTE_ROWS = 16
CONV_ROWS = 64
CONV_SUMS = 4
EXPERT_TILE = 1024
EXPERT_PATH_ROWS = tuple(EXPERT_TILE >> k for k in (3, 2, 1, 0))
SC_MAX_INDICES = 128
SC_ROW_BUFFER_BYTES = 448 * 1024
SCATTER_RING = 2
GATHER_RING = 4
GATHER_LAG = 2
COMBINE_SUB = 8
NEG_BIG = -1e30
VMEM_LIMIT = 56 * 1024 * 1024


def _sigmoid(x):
    return 1.0 / (1.0 + jnp.exp(-x))


def _cparams(sem):
    return pltpu.CompilerParams(dimension_semantics=sem, vmem_limit_bytes=VMEM_LIMIT)


def _const_spec(shape):
    nd = len(shape)
    return pl.BlockSpec(shape, lambda *_: (0,) * nd)


PACK_WORDS = D_MODEL // 2
HIGH_HALF = 0xFFFF0000


def _pack_rows(x):
    lo = lax.bitcast_convert_type(x[:, :PACK_WORDS].astype(BF16).astype(F32), jnp.uint32)
    hi = lax.bitcast_convert_type(x[:, PACK_WORDS:].astype(BF16).astype(F32), jnp.uint32)
    return (lo >> 16) | (hi & jnp.uint32(HIGH_HALF))


def _unpack_rows(w):
    lo = lax.bitcast_convert_type(w << 16, F32)
    hi = lax.bitcast_convert_type(w & jnp.uint32(HIGH_HALF), F32)
    return jnp.concatenate([lo, hi], axis=1)


def _residual_tile(x_ref, meta_ref, is_first):
    x = x_ref[...]
    tm, d = x.shape
    prefix = jnp.concatenate([jnp.zeros((CHUNK_PAD, d), x.dtype), meta_ref[...]], axis=0)
    first = jnp.concatenate([prefix, x[:tm - CHUNK, :]], axis=0)
    return jnp.where(is_first, first, x)


def _residual_spec(tm, d, seq, tiles_per_batch, n_tiles):
    def start(i):
        i = jnp.minimum(i, n_tiles - 1)
        b, t = i // tiles_per_batch, i % tiles_per_batch
        return (pl.multiple_of(b * seq + jnp.maximum(t * tm - CHUNK, 0), SUBLANES), 0)
    return pl.BlockSpec((pl.Element(tm), pl.Element(d)), start)


def _in_proj_kernel(x_ref, meta_ref, g_ref, lbl_ref, w_ref, acts_ref, logf_ref, *, tiles_per_batch):
    d = D_MODEL

    def put(group, value):
        acts_ref[:, group * d:(group + 1) * d] = value.astype(acts_ref.dtype)

    h = _residual_tile(x_ref, meta_ref, pl.program_id(0) % tiles_per_batch == 0)
    rows = h.shape[0] // NORM_PARTS
    u_parts = []
    for p in range(NORM_PARTS):
        hp = h[p * rows:(p + 1) * rows]
        ms = jnp.mean(hp * hp, axis=-1, keepdims=True)
        u_parts.append((hp * lax.rsqrt(ms + RMS_EPS) * g_ref[...]).astype(BF16))

    def proj(j):
        w = w_ref[:, j * d:(j + 1) * d].astype(BF16)
        return jnp.concatenate([jnp.dot(u, w, preferred_element_type=F32) for u in u_parts], axis=0)

    put(ACT_AGLU, proj(0) * _sigmoid(proj(1)))
    zq = proj(2)
    put(ACT_Q, zq * _sigmoid(zq))
    lbl = lbl_ref[...]
    e = jnp.exp(lbl - jnp.max(lbl, axis=0, keepdims=True))
    lb = e[0:1, :] / jnp.sum(e, axis=0, keepdims=True)
    s = _sigmoid(proj(3))
    logf_ref[...] = jnp.log(lb + (1.0 - lb) * s)
    put(ACT_K, (1.0 - lb) * (1.0 - s))
    zg = proj(5)
    put(ACT_OG, zg * _sigmoid(zg))
    put(ACT_SGA, _sigmoid(proj(6)))
    put(ACT_SGB, _sigmoid(proj(7)))
    put(ACT_V, proj(4))


def _in_proj(x2, meta, seq, g_mix, lb_logits, w_in):
    d = x2.shape[1]
    n_batch = x2.shape[0] // seq
    tm = IN_TILE
    tpb = (seq + CHUNK) // tm
    tp = n_batch * (seq + CHUNK)
    assert tpb * tm == seq + CHUNK and tm > CHUNK
    row = pl.BlockSpec((tm, d), lambda i: (i, 0))
    return pl.pallas_call(
        functools.partial(_in_proj_kernel, tiles_per_batch=tpb),
        grid=(tp // tm,),
        in_specs=[_residual_spec(tm, d, seq, tpb, tp // tm), _const_spec(meta.shape), _const_spec((1, d)),
                  _const_spec(lb_logits.shape),
                  pl.BlockSpec(w_in.shape, lambda i: (0, 0), pipeline_mode=pl.Buffered(1))],
        out_specs=[pl.BlockSpec((tm, N_ACTS * d), lambda i: (i, 0)), row],
        out_shape=[jax.ShapeDtypeStruct((tp, N_ACTS * d), BF16), jax.ShapeDtypeStruct((tp, d), F32)],
        compiler_params=_cparams(("parallel",)),
        name="in_proj",
    )(x2, meta, g_mix, lb_logits, w_in)


def _conv_kernel(halo_ref, cur_ref, wdw_ref, bdw_ref, lng_ref, lnb_ref, wout_ref, bout_ref, sga_ref,
                 out_ref, win_ref, acc_ref, shift_ref):
    tm = cur_ref.shape[0]
    win_ref[0:HALO, :] = halo_ref[...].astype(F32)
    win_ref[HALO:, :] = cur_ref[...].astype(F32)
    first_tap = HALO - (CONV_WIDTH - 1)
    for c in range(D_MODEL // LANES):
        lanes = slice(c * LANES, (c + 1) * LANES)
        n_rows = tm + HALO - SUBLANES
        for s in range(1, SUBLANES):
            shift_ref[s - 1, 0:n_rows, :] = win_ref[pl.ds(s, n_rows), lanes]
        bias = jnp.broadcast_to(bdw_ref[:, lanes], (CONV_ROWS, LANES))
        groups = CONV_ROWS // SUBLANES

        def chunk(r, carry, c=c, lanes=lanes, bias=bias):
            r0 = pl.multiple_of(r * CONV_ROWS, CONV_ROWS)
            accs = [bias.reshape(groups, SUBLANES, LANES)] + [None] * (CONV_SUMS - 1)
            for j in range(CONV_WIDTH):
                off = first_tap + j
                rows = pl.ds(r0 + off - off % SUBLANES, CONV_ROWS)
                w_j = wdw_ref[c, pl.ds(j, SUBLANES, stride=0), :]
                window = shift_ref[off % SUBLANES - 1, rows, :] if off % SUBLANES else win_ref[rows, lanes]
                term = w_j[None] * window.reshape(groups, SUBLANES, LANES)
                accs[j % CONV_SUMS] = term if accs[j % CONV_SUMS] is None else accs[j % CONV_SUMS] + term
            while len(accs) > 1:
                accs = [a + b for a, b in zip(accs[0::2], accs[1::2])]
            acc_ref[pl.ds(r0, CONV_ROWS), lanes] = accs[0].reshape(CONV_ROWS, LANES)
            return carry

        lax.fori_loop(0, tm // CONV_ROWS, chunk, 0)
    rows = tm // NORM_PARTS
    w_out = wout_ref[...].astype(BF16)
    ys = []
    for p in range(NORM_PARTS):
        a = acc_ref[p * rows:(p + 1) * rows, :]
        mu = jnp.mean(a, axis=-1, keepdims=True)
        ac = a - mu
        var = jnp.mean(ac * ac, axis=-1, keepdims=True)
        y = ac * lax.rsqrt(var + LN_EPS) * lng_ref[...] + lnb_ref[...]
        ys.append((y * _sigmoid(y)).astype(BF16))
    outs = [jnp.dot(y, w_out, preferred_element_type=F32) + bout_ref[...] for y in ys]
    for p in range(NORM_PARTS):
        part = slice(p * rows, (p + 1) * rows)
        out_ref[part, :] = (sga_ref[part, :].astype(F32) * outs[p]).astype(out_ref.dtype)


def _conv_branch(acts, w_dw, b_dw, ln_g, ln_b, w_out, b_out):
    tp, d = acts.shape[0], D_MODEL
    tm = ROW_TILE
    per = tm // HALO
    row = pl.BlockSpec((tm, d), lambda i: (i, 0))
    glu = pl.BlockSpec((tm, d), lambda i: (i, ACT_AGLU))
    gate = pl.BlockSpec((tm, d), lambda i: (i, ACT_SGA))
    halo = pl.BlockSpec((HALO, d), lambda i: (jnp.maximum(i * per - 1, 0), ACT_AGLU))
    vec = _const_spec((1, d))
    return pl.pallas_call(
        _conv_kernel,
        grid=(tp // tm,),
        in_specs=[halo, glu, _const_spec(w_dw.shape), vec, vec, vec, _const_spec((d, d)), vec, gate],
        out_specs=row,
        out_shape=jax.ShapeDtypeStruct((tp, d), BF16),
        scratch_shapes=[pltpu.VMEM((tm + HALO, d), F32), pltpu.VMEM((tm, d), F32),
                        pltpu.VMEM((SUBLANES - 1, tm + HALO - SUBLANES, LANES), F32)],
        compiler_params=_cparams(("parallel",)),
        name="conv_branch",
    )(acts, acts, w_dw, b_dw, ln_g, ln_b, w_out, b_out, acts)


def _split_bf16(x):
    hi = x.astype(BF16)
    return hi, (x - hi.astype(F32)).astype(BF16)


def _nt_dot(a, b):
    return lax.dot_general(a, b, (((1,), (1,)), ((), ())), preferred_element_type=F32)


def _tn_dot(a, b):
    return lax.dot_general(a, b, (((0,), (0,)), ((), ())), preferred_element_type=F32)


def _scan_kernel(q_ref, k_ref, v_ref, lf_ref, og_ref, gon_ref, o_ref, st_ref):
    @pl.when(pl.program_id(1) == 0)
    def _():
        st_ref[...] = jnp.zeros_like(st_ref)

    c = CHUNK
    row = lax.broadcasted_iota(jnp.int32, (c, c), 0)
    col = lax.broadcasted_iota(jnp.int32, (c, c), 1)
    causal = col <= row
    diag_mask = jnp.logical_and(causal, row // SUB_BLOCK == col // SUB_BLOCK)
    tri = causal.astype(BF16)

    def bcast_rows(rows):
        return jnp.concatenate([jnp.broadcast_to(r, (SUB_BLOCK, HEAD_DIM)) for r in rows], axis=0)

    heads = [slice(h * HEAD_DIM, (h + 1) * HEAD_DIM) for h in range(N_HEADS)]
    chunks = [slice(ci * c, (ci + 1) * c) for ci in range(SCAN_CHUNKS)]

    bcums = []
    for cs in chunks:
        lf = lf_ref[cs, :]
        lf_hi = lf.astype(BF16)
        lf_mid, lf_lo = _split_bf16(lf - lf_hi.astype(F32))
        bcums.append(jnp.dot(tri, lf_hi, preferred_element_type=F32)
                     + (jnp.dot(tri, lf_mid, preferred_element_type=F32)
                        + jnp.dot(tri, lf_lo, preferred_element_type=F32)))
    operands = {}
    for ci, cs in enumerate(chunks):
        for h, hs in enumerate(heads):
            b = bcums[ci][:, hs]
            q = q_ref[cs, hs].astype(F32)
            k = k_ref[cs, hs].astype(F32)
            ends = [b[i * SUB_BLOCK + SUB_BLOCK - 1:i * SUB_BLOCK + SUB_BLOCK, :] for i in range(N_SUB)]
            mids = [b[i * SUB_BLOCK + SUB_BLOCK // 2 - 1:i * SUB_BLOCK + SUB_BLOCK // 2, :] for i in range(N_SUB)]
            mid_full = bcast_rows(mids)
            b_last = ends[-1]
            zero_row = jnp.zeros_like(b_last)
            qm = q * jnp.exp(b - mid_full)
            km = k * jnp.exp(mid_full - b)
            q_parts, k_parts = [], []
            for j in range(N_SUB - 1):
                q_rows = [jnp.exp(mids[i] - ends[j]) if i > j else zero_row for i in range(N_SUB)]
                k_rows = [jnp.exp(ends[j] - mids[j]) if i == j else zero_row for i in range(N_SUB)]
                q_parts.append((qm * bcast_rows(q_rows)).astype(BF16))
                k_parts.append((km * bcast_rows(k_rows)).astype(BF16))
            q_in = (qm * bcast_rows([jnp.exp(m) for m in mids])).astype(BF16)
            k_out = (km * bcast_rows([jnp.exp(b_last - m) for m in mids])).astype(BF16)
            operands[ci, h] = (qm.astype(BF16), km.astype(BF16), jnp.concatenate(q_parts, axis=1),
                               jnp.concatenate(k_parts, axis=1), q_in, k_out, jnp.exp(b_last))
    products = {}
    for ci, cs in enumerate(chunks):
        for h, hs in enumerate(heads):
            qm, km, q_cat, k_cat, _, k_out, _ = operands[ci, h]
            products[ci, h] = (_nt_dot(qm, km), _nt_dot(q_cat, k_cat), _tn_dot(v_ref[cs, hs], k_out))
    within = {}
    for ci, cs in enumerate(chunks):
        for h, hs in enumerate(heads):
            same_block, earlier_blocks, _ = products[ci, h]
            scores = jnp.where(diag_mask, same_block, 0.0) + earlier_blocks
            within[ci, h] = jnp.dot(scores.astype(BF16), v_ref[cs, hs], preferred_element_type=F32)
    states = [st_ref[h] for h in range(N_HEADS)]
    outs = {}
    for ci in range(SCAN_CHUNKS):
        for h in range(N_HEADS):
            outs[ci, h] = within[ci, h] + _nt_dot(operands[ci, h][4], states[h].astype(BF16))
            states[h] = states[h] * operands[ci, h][6] + products[ci, h][2]
    for h in range(N_HEADS):
        st_ref[h] = states[h]
    for ci, cs in enumerate(chunks):
        for h, hs in enumerate(heads):
            o = outs[ci, h]
            ms = jnp.mean(o * o, axis=-1, keepdims=True)
            on = o * lax.rsqrt(ms + RMS_EPS) * gon_ref[:, hs]
            o_ref[cs, hs] = (on * og_ref[cs, hs].astype(F32)).astype(o_ref.dtype)


def _hgrn2_scan(acts, logf, g_onorm, n_batch):
    tp, d = logf.shape
    rows = SCAN_CHUNKS * CHUNK
    n_chunks = tp // n_batch // rows
    assert n_chunks * rows * n_batch == tp
    col = lambda group: pl.BlockSpec((rows, d), lambda b, c: (b * n_chunks + c, group))
    blk = col(0)
    return pl.pallas_call(
        _scan_kernel,
        grid=(n_batch, n_chunks),
        in_specs=[col(ACT_Q), col(ACT_K), col(ACT_V), blk, col(ACT_OG), pl.BlockSpec((1, d), lambda b, c: (0, 0))],
        out_specs=blk,
        out_shape=jax.ShapeDtypeStruct((tp, d), BF16),
        scratch_shapes=[pltpu.VMEM((N_HEADS, HEAD_DIM, HEAD_DIM), F32)],
        compiler_params=_cparams(("arbitrary", "arbitrary")),
        name="hgrn2_scan",
    )(acts, acts, acts, logf, acts, g_onorm)


def _merge_route_kernel(x_ref, meta_ref, on_ref, ap_ref, sgb_ref, wrnn_ref, wo_ref, gffn_ref, wr_ref, br_ref,
                        h1_ref, u2_ref, info_ref, route_ref, cnt_ref, tri_ref, carry_ref, colcnt_ref, logits_ref, hres_ref,
                        *, tiles_per_batch, n_tiles):
    tm = on_ref.shape[0]
    i = pl.program_id(0)
    hres_ref[...] = _residual_tile(x_ref, meta_ref, jnp.minimum(i, n_tiles - 1) % tiles_per_batch == 0)

    @pl.when(i == 0)
    def _():
        r_i = lax.broadcasted_iota(jnp.int32, (tm, tm), 0)
        c_i = lax.broadcasted_iota(jnp.int32, (tm, tm), 1)
        tri_ref[...] = (c_i < r_i).astype(BF16)
        carry_ref[...] = jnp.zeros_like(carry_ref)
        colcnt_ref[...] = jnp.zeros_like(colcnt_ref)
        logits_ref[...] = jnp.zeros_like(logits_ref)

    routed = i > 0
    lane = lax.broadcasted_iota(jnp.int32, (tm, LANES), 1)
    cur = jnp.where(lane < N_EXPERTS, logits_ref[...], NEG_BIG)
    vals, idxs, sels = [], [], []

    def topk_round(cur):
        m = jnp.max(cur, axis=-1, keepdims=True)
        idx = jnp.min(jnp.where(cur == m, lane, LANES), axis=-1, keepdims=True)
        sel = lane == idx
        vals.append(m)
        idxs.append(idx)
        sels.append(sel)
        return jnp.where(sel, 2.0 * NEG_BIG, cur)

    w_hi, w_lo = _split_bf16(wr_ref[...])
    w_hi_lo = jnp.concatenate([w_hi, w_lo], axis=1)
    part = tm // MERGE_PARTS
    rows = [slice(p * part, (p + 1) * part) for p in range(MERGE_PARTS)]
    w_rnn = wrnn_ref[...].astype(BF16)
    w_out = wo_ref[...].astype(BF16)
    rs = [jnp.dot(on_ref[rw, :], w_rnn, preferred_element_type=F32) for rw in rows]
    cur = topk_round(cur)
    ys = [(ap_ref[rw, :].astype(F32) + sgb_ref[rw, :].astype(F32) * r).astype(BF16) for rw, r in zip(rows, rs)]
    h1s = [hres_ref[rw, :] + jnp.dot(y, w_out, preferred_element_type=F32) for rw, y in zip(rows, ys)]
    cur = topk_round(cur)
    u2s = []
    for rw, h1 in zip(rows, h1s):
        h1_ref[rw, :] = h1
        ms = jnp.mean(h1 * h1, axis=-1, keepdims=True)
        u2 = h1 * lax.rsqrt(ms + RMS_EPS) * gffn_ref[...]
        u2_ref[rw, :] = _pack_rows(u2)
        u2s.append(u2)
    cur = topk_round(cur)
    new_logits = []
    for u2 in u2s:
        u_hi, u_lo = _split_bf16(u2)
        both = jnp.dot(u_hi, w_hi_lo, preferred_element_type=F32)
        new_logits.append(both[:, :LANES] + (jnp.dot(u_lo, w_hi, preferred_element_type=F32) + both[:, LANES:])
                          + br_ref[...])
    cur = topk_round(cur)
    assert len(vals) == TOP_K
    exps = [jnp.exp(vk - vals[0]) for vk in vals]
    den = exps[0] + exps[1] + exps[2] + exps[3]
    onehot = jnp.logical_or(jnp.logical_or(sels[0], sels[1]), jnp.logical_or(sels[2], sels[3]))
    onehot = jnp.logical_and(onehot, routed)
    onehot_bf = onehot.astype(BF16)
    rank_all = jnp.dot(tri_ref[...], onehot_bf, preferred_element_type=F32) + carry_ref[...]
    carry_ref[...] += jnp.sum(onehot.astype(F32), axis=0, keepdims=True)
    colcnt_ref[...] += _tn_dot(onehot_bf, jnp.ones((tm, LANES), BF16))
    info = jnp.zeros((tm, LANES), F32)
    for kk in range(TOP_K):
        rank_k = jnp.sum(jnp.where(sels[kk], rank_all, 0.0), axis=-1, keepdims=True)
        info = jnp.where(lane == kk, exps[kk] / den, info)
        info = jnp.where(lane == TOP_K + kk, idxs[kk].astype(F32), info)
        info = jnp.where(lane == 2 * TOP_K + kk, rank_k, info)
    info_ref[...] = info
    route_ref[...] = info.T[0:ROUTE_ROWS, :]
    cnt_ref[...] = colcnt_ref[...]
    logits_ref[...] = jnp.concatenate(new_logits, axis=0)


def _merge_route(x2, meta, seq, on, ap, acts, w_rnn, w_o, g_ffn, w_router_pad, b_router_pad):
    tp, d = on.shape
    tm = ROW_TILE
    n_tiles = tp // tm
    tpb = (seq + CHUNK) // tm
    assert tpb * tm == seq + CHUNK
    this_tile = lambda i: (jnp.minimum(i, n_tiles - 1), 0)
    prev_tile = lambda i: (jnp.maximum(i - 1, 0), 0)
    row = pl.BlockSpec((tm, d), this_tile)
    vec = _const_spec((1, d))
    return pl.pallas_call(
        functools.partial(_merge_route_kernel, tiles_per_batch=tpb, n_tiles=n_tiles),
        grid=(n_tiles + 1,),
        in_specs=[_residual_spec(tm, d, seq, tpb, n_tiles), _const_spec(meta.shape), row, row,
                  pl.BlockSpec((tm, d), lambda i: (jnp.minimum(i, n_tiles - 1), ACT_SGB)),
                  _const_spec((d, d)), _const_spec((d, d)), vec, _const_spec((d, LANES)), _const_spec((1, LANES))],
        out_specs=[row, pl.BlockSpec((tm, PACK_WORDS), this_tile),
                   pl.BlockSpec((tm, LANES), prev_tile),
                   pl.BlockSpec((ROUTE_ROWS, tm), lambda i: (0, jnp.maximum(i - 1, 0))), _const_spec((LANES, LANES))],
        out_shape=[jax.ShapeDtypeStruct((tp, d), F32), jax.ShapeDtypeStruct((tp, PACK_WORDS), jnp.uint32),
                   jax.ShapeDtypeStruct((tp, LANES), F32), jax.ShapeDtypeStruct((ROUTE_ROWS, tp), F32),
                   jax.ShapeDtypeStruct((LANES, LANES), F32)],
        scratch_shapes=[pltpu.VMEM((tm, tm), BF16), pltpu.VMEM((1, LANES), F32),
                        pltpu.VMEM((LANES, LANES), F32), pltpu.VMEM((tm, LANES), F32), pltpu.VMEM((tm, d), F32)],
        compiler_params=_cparams(("arbitrary",)),
        name="merge_route",
    )(x2, meta, on, ap, acts, w_rnn, w_o, g_ffn, w_router_pad, b_router_pad)


def _slots_kernel(route_ref, cnt_ref, dest_ref, te_ref, pstart_ref):
    tl = route_ref.shape[1]
    n_tile_lanes = te_ref.shape[1]

    @pl.when(pl.program_id(0) == 0)
    def _():
        r_i = lax.broadcasted_iota(jnp.int32, (LANES, LANES), 0)
        c_i = lax.broadcasted_iota(jnp.int32, (LANES, LANES), 1)
        cnt = cnt_ref[...]
        tiles = jnp.floor((cnt + (EXPERT_TILE - 1)) / EXPERT_TILE)
        pend_col = jnp.dot((c_i <= r_i).astype(F32), tiles, preferred_element_type=F32,
                           precision=lax.Precision.HIGHEST)
        pstart_ref[...] = pend_col - tiles
        tile_id = lax.broadcasted_iota(jnp.int32, (LANES, n_tile_lanes), 1).astype(F32)
        exp_id = lax.broadcasted_iota(jnp.int32, (LANES, n_tile_lanes), 0)
        pend_wide = jnp.concatenate([pend_col] * (n_tile_lanes // LANES), axis=1)
        below = jnp.logical_and(pend_wide <= tile_id, exp_id < N_EXPERTS)
        te = jnp.minimum(jnp.sum(below.astype(F32), axis=0, keepdims=True), N_EXPERTS - 1.0)
        n_active = pend_col[N_EXPERTS - 1:N_EXPERTS, 0:1]
        owner = exp_id.astype(F32) == te
        cnt_wide = jnp.concatenate([cnt] * (n_tile_lanes // LANES), axis=1)
        first_tile = pend_wide - jnp.concatenate([tiles] * (n_tile_lanes // LANES), axis=1)
        left = jnp.sum(jnp.where(owner, cnt_wide - (tile_id - first_tile) * EXPERT_TILE, 0.0), axis=0, keepdims=True)
        valid = jnp.clip(left, 0.0, float(EXPERT_TILE))
        sub_t = lax.broadcasted_iota(jnp.int32, (SUBLANES, n_tile_lanes), 0)
        out = jnp.where(sub_t == 0, jnp.broadcast_to(te, (SUBLANES, n_tile_lanes)),
                        jnp.where(sub_t == 1, jnp.broadcast_to(n_active, (SUBLANES, n_tile_lanes)),
                                  jnp.broadcast_to(valid, (SUBLANES, n_tile_lanes))))
        te_ref[...] = out.astype(jnp.int32)

    route = route_ref[...]
    expert = lax.broadcasted_iota(jnp.int32, (N_EXPERTS, tl), 0)
    first_slot = jnp.concatenate([pstart_ref[0:N_EXPERTS, :]] * (tl // LANES), axis=1) * EXPERT_TILE
    rows = []
    for kk in range(TOP_K):
        idx_k = route[TOP_K + kk:TOP_K + kk + 1, :].astype(jnp.int32)
        start_k = jnp.sum(jnp.where(expert == idx_k, first_slot, 0.0), axis=0, keepdims=True)
        rows.append(start_k + route[2 * TOP_K + kk:2 * TOP_K + kk + 1, :])
    rows.append(jnp.zeros((SUBLANES - TOP_K, tl), F32))
    dest_ref[...] = jnp.concatenate(rows, axis=0).astype(jnp.int32)


def _slots(route, cnt, n_tiles_max):
    tp = route.shape[1]
    tl = tp // SLOTS_STEPS
    assert tl * SLOTS_STEPS == tp and tl % LANES == 0
    n_tile_lanes = -(-n_tiles_max // LANES) * LANES
    return pl.pallas_call(
        _slots_kernel,
        grid=(SLOTS_STEPS,),
        in_specs=[pl.BlockSpec((ROUTE_ROWS, tl), lambda i: (0, i)), _const_spec((LANES, LANES))],
        out_specs=[pl.BlockSpec((SUBLANES, tl), lambda i: (0, i)), _const_spec((8, n_tile_lanes))],
        out_shape=[jax.ShapeDtypeStruct((SUBLANES, tp), jnp.int32),
                   jax.ShapeDtypeStruct((8, n_tile_lanes), jnp.int32)],
        scratch_shapes=[pltpu.VMEM((LANES, LANES), F32)],
        compiler_params=_cparams(("arbitrary",)),
        name="slots",
    )(route, cnt)


def _sc_workers():
    sc = plsc.get_sparse_core_info()
    return sc.num_cores, sc.num_cores * sc.num_subcores


def _sc_split(tp, ring):
    n_workers = _sc_workers()[1]
    per_worker = tp // n_workers
    assert per_worker * n_workers == tp
    limit = min(SC_MAX_INDICES, SC_ROW_BUFFER_BYTES // (ring * PACK_WORDS * 4))
    return per_worker, max(c for c in range(SUBLANES, limit + 1, SUBLANES) if per_worker % c == 0)


def _sc_first_token(per_worker):
    return pl.multiple_of((lax.axis_index("s") * _sc_workers()[0] + lax.axis_index("c")) * per_worker, SUBLANES)


def _sc_ring_kernel(out_type, n_index_rows, chunk, ring):
    return functools.partial(
        pl.kernel, mesh=plsc.VectorSubcoreMesh(core_axis_name="c", subcore_axis_name="s"), out_type=out_type,
        scratch_types=[pltpu.VMEM((n_index_rows, chunk), jnp.int32), pltpu.VMEM((ring, chunk, PACK_WORDS), jnp.uint32),
                       pltpu.SemaphoreType.DMA, pltpu.SemaphoreType.DMA((ring,)), pltpu.SemaphoreType.DMA((ring,))])


def _wait_all(copies):
    for c in copies:
        c.wait()


def _dispatch(dest_flat, u2p, n_slots):
    tp = u2p.shape[0]
    per_worker, ch = _sc_split(tp, SCATTER_RING)
    n_chunks = per_worker // ch

    @_sc_ring_kernel(jax.ShapeDtypeStruct((n_slots, PACK_WORDS), jnp.uint32), n_chunks * TOP_K, ch, SCATTER_RING)
    def scatter_rows(u2_hbm, dest_hbm, xs_hbm, idx_v, rows_v, sem_idx, sem_load, sem_scatter):
        base = _sc_first_token(per_worker)
        _wait_all([pltpu.async_copy(dest_hbm.at[pl.ds(kk * tp + base + c * ch, ch)], idx_v.at[c * TOP_K + kk], sem_idx)
                   for c in range(n_chunks) for kk in range(TOP_K)])

        def load(c):
            slot = c % SCATTER_RING
            return pltpu.async_copy(u2_hbm.at[pl.ds(base + c * ch, ch)], rows_v.at[slot], sem_load.at[slot])

        def scatter(c):
            slot = c % SCATTER_RING
            return [pltpu.async_copy(rows_v.at[slot], xs_hbm.at[idx_v.at[c * TOP_K + kk]], sem_scatter.at[slot])
                    for kk in range(TOP_K)]

        loads, scatters = {}, {}
        for c in range(n_chunks + 1):
            if c < n_chunks:
                if c >= SCATTER_RING:
                    _wait_all(scatters.pop(c - SCATTER_RING))
                loads[c] = load(c)
            if c >= 1:
                loads.pop(c - 1).wait()
                scatters[c - 1] = scatter(c - 1)
        for copies in scatters.values():
            _wait_all(copies)

    return scatter_rows(u2p, dest_flat)


def _gather_expert_rows(dest_flat, ys):
    tp = dest_flat.shape[0] // TOP_K
    per_worker, ch = _sc_split(tp, GATHER_RING)
    units = [(c, kk) for c in range(per_worker // ch) for kk in range(TOP_K)]

    @_sc_ring_kernel(jax.ShapeDtypeStruct((TOP_K, tp, PACK_WORDS), jnp.uint32), len(units), ch, GATHER_RING)
    def gather_rows(ys_hbm, dest_hbm, out_hbm, idx_v, rows_v, sem_idx, sem_gather, sem_write):
        base = _sc_first_token(per_worker)
        _wait_all([pltpu.async_copy(dest_hbm.at[pl.ds(kk * tp + base + c * ch, ch)], idx_v.at[u], sem_idx)
                   for u, (c, kk) in enumerate(units)])

        def gather(u):
            slot = u % GATHER_RING
            return pltpu.async_copy(ys_hbm.at[idx_v.at[u]], rows_v.at[slot], sem_gather.at[slot])

        def write(u):
            c, kk = units[u]
            slot = u % GATHER_RING
            return pltpu.async_copy(rows_v.at[slot], out_hbm.at[kk, pl.ds(base + c * ch, ch)], sem_write.at[slot])

        gathers, writes = {}, {}
        for u in range(len(units) + GATHER_LAG):
            if u < len(units):
                if u >= GATHER_RING:
                    writes.pop(u - GATHER_RING).wait()
                gathers[u] = gather(u)
            if u >= GATHER_LAG:
                gathers.pop(u - GATHER_LAG).wait()
                writes[u - GATHER_LAG] = write(u - GATHER_LAG)
        _wait_all(writes.values())

    return gather_rows(ys, dest_flat)


def _expert_kernel(te_ref, nact_ref, valid_ref, x_ref, wgu_hbm, bgu_ref, wdn_hbm, bdn_ref, y_ref,
                   wgu_buf, wdn_buf, slot_ref, sem):
    i = pl.program_id(0)
    n_active = nact_ref[0]
    last_tile = pl.num_programs(0) - 1

    def fetch(expert, slot):
        return (pltpu.make_async_copy(wgu_hbm.at[expert], wgu_buf.at[slot], sem.at[slot, 0]),
                pltpu.make_async_copy(wdn_hbm.at[expert], wdn_buf.at[slot], sem.at[slot, 1]))

    @pl.when(i == 0)
    def _():
        slot_ref[0] = 1
        for copy in fetch(te_ref[0], 0):
            copy.start()

    @pl.when(i < n_active)
    def _():
        expert = te_ref[i]
        first_tile_of_expert = jnp.logical_or(i == 0, expert != te_ref[jnp.maximum(i - 1, 0)])

        @pl.when(first_tile_of_expert)
        def _():
            slot = 1 - slot_ref[0]
            slot_ref[0] = slot
            for copy in fetch(expert, slot):
                copy.wait()
            nxt = lax.while_loop(
                lambda j: jnp.logical_and(j < n_active, te_ref[jnp.minimum(j, last_tile)] == expert),
                lambda j: j + 1, i + 1)

            @pl.when(nxt < n_active)
            def _():
                for copy in fetch(te_ref[jnp.minimum(nxt, last_tile)], 1 - slot):
                    copy.start()

        slot = slot_ref[0]

        def mlp(rows):
            x = _unpack_rows(x_ref[0:rows, :]).astype(BF16)
            gu = jnp.dot(x, wgu_buf[slot].astype(BF16), preferred_element_type=F32) + bgu_ref[0]
            gate = jnp.minimum(gu[:, :D_FF], SWIGLU_LIMIT)
            up = jnp.clip(gu[:, D_FF:], -SWIGLU_LIMIT, SWIGLU_LIMIT)
            hdn = (up + 1.0) * (gate * _sigmoid(SWIGLU_ALPHA * gate))
            y = jnp.dot(hdn.astype(BF16), wdn_buf[slot].astype(BF16), preferred_element_type=F32) + bdn_ref[0]
            y_ref[0:rows, :] = _pack_rows(y)

        path = sum((valid_ref[i] > rows).astype(jnp.int32) for rows in EXPERT_PATH_ROWS[:-1])
        for k, rows in enumerate(EXPERT_PATH_ROWS):
            @pl.when(path == k)
            def _(rows=rows):
                mlp(rows)
                if rows < EXPERT_TILE:
                    y_ref[rows:, :] = jnp.zeros((EXPERT_TILE - rows, PACK_WORDS), y_ref.dtype)

    @pl.when(i >= n_active)
    def _():
        y_ref[...] = jnp.zeros_like(y_ref)


def _expert_mlp(te, nact, valid, xs, w_gu, b_gu, w_dn, b_dn):
    d = D_MODEL
    n_tiles = xs.shape[0] // EXPERT_TILE

    def tile_map(i, te_ref, nact_ref, valid_ref):
        return (jnp.minimum(i, nact_ref[0] - 1), 0)

    def exp_map(i, te_ref, nact_ref, valid_ref):
        return (te_ref[jnp.minimum(i, nact_ref[0] - 1)], 0, 0)

    grid_spec = pltpu.PrefetchScalarGridSpec(
        num_scalar_prefetch=3,
        grid=(n_tiles,),
        in_specs=[pl.BlockSpec((EXPERT_TILE, PACK_WORDS), tile_map),
                  pl.BlockSpec(memory_space=pl.ANY),
                  pl.BlockSpec((1, 1, 2 * D_FF), exp_map),
                  pl.BlockSpec(memory_space=pl.ANY),
                  pl.BlockSpec((1, 1, d), exp_map)],
        out_specs=pl.BlockSpec((EXPERT_TILE, PACK_WORDS), lambda i, te_ref, nact_ref, valid_ref: (i, 0)),
        scratch_shapes=[pltpu.VMEM((2, d, 2 * D_FF), F32), pltpu.VMEM((2, D_FF, d), F32),
                        pltpu.SMEM((1,), jnp.int32), pltpu.SemaphoreType.DMA((2, 2))],
    )
    return pl.pallas_call(
        _expert_kernel,
        grid_spec=grid_spec,
        out_shape=jax.ShapeDtypeStruct(xs.shape, jnp.uint32),
        compiler_params=_cparams(("arbitrary",)),
        name="expert_mlp",
    )(te, nact, valid, xs, w_gu, b_gu, w_dn, b_dn)


def _combine_kernel(gfin_ref, *refs):
    out_ref = refs[-1]
    for s in range(COMBINE_SUB):
        h1_ref, info_ref, yg_ref = refs[3 * s:3 * s + 3]
        info = info_ref[...]
        h2 = h1_ref[...]
        for kk in range(TOP_K):
            h2 = h2 + info[:, kk:kk + 1] * _unpack_rows(yg_ref[kk])
        ms = jnp.mean(h2 * h2, axis=-1, keepdims=True)
        out_ref[s * CHUNK:(s + 1) * CHUNK, :] = h2 * lax.rsqrt(ms + RMS_EPS) * gfin_ref[...]


def _combine(h1, info, g_final, yg, n_batch, seq):
    d = h1.shape[1]
    chunks = seq // CHUNK

    def padded(s):
        def index(j):
            c = j * COMBINE_SUB + s
            return (c // chunks) * (chunks + 1) + c % chunks + 1
        return index

    in_specs = [pl.BlockSpec((1, d), lambda j: (0, 0))]
    operands = [g_final]
    for s in range(COMBINE_SUB):
        chunk_of = padded(s)
        in_specs += [pl.BlockSpec((CHUNK, d), lambda j, f=chunk_of: (f(j), 0)),
                     pl.BlockSpec((CHUNK, LANES), lambda j, f=chunk_of: (f(j), 0)),
                     pl.BlockSpec((TOP_K, CHUNK, PACK_WORDS), lambda j, f=chunk_of: (0, f(j), 0))]
        operands += [h1, info, yg]
    return pl.pallas_call(
        _combine_kernel,
        grid=(n_batch * chunks // COMBINE_SUB,),
        in_specs=in_specs,
        out_specs=pl.BlockSpec((COMBINE_SUB * CHUNK, d), lambda j: (j, 0)),
        out_shape=jax.ShapeDtypeStruct((n_batch * seq, d), F32),
        compiler_params=_cparams(("parallel",)),
        name="combine",
    )(*operands)


def kernel(x, meta_tokens, lb_logits, g_mix, w_in, w_dw, b_dw, ln_g, ln_b, w_conv_out, b_conv_out,
           g_onorm, w_rnn_out, w_o, g_ffn, w_router, b_router, w_gate_up, b_gate_up, w_down, b_down,
           g_final):
    n_batch, seq, d = x.shape
    assert d == D_MODEL and w_in.shape[0] == 1, "single-layer block with D_MODEL features"
    assert seq % (CHUNK * COMBINE_SUB) == 0
    seq_pad = CHUNK + seq
    tp = n_batch * seq_pad
    assert tp % ROW_TILE == 0

    x2 = x.reshape(n_batch * seq, d)
    meta = meta_tokens.astype(x.dtype)
    vec = lambda a: a.reshape(1, -1).astype(F32)

    acts, logf = _in_proj(x2, meta, seq, vec(g_mix[0]), lb_logits.astype(F32), w_in[0])
    w_taps = w_dw[0].astype(F32).reshape(CONV_WIDTH, d // LANES, LANES).transpose(1, 0, 2)
    ap = _conv_branch(acts, w_taps, vec(b_dw[0]), vec(ln_g[0]), vec(ln_b[0]),
                      w_conv_out[0], vec(b_conv_out[0]))
    on = _hgrn2_scan(acts, logf, vec(g_onorm[0]), n_batch)
    w_router_pad = jnp.pad(w_router[0].astype(F32), ((0, 0), (0, LANES - N_EXPERTS)))
    b_router_pad = jnp.pad(vec(b_router[0]), ((0, 0), (0, LANES - N_EXPERTS)))
    h1, u2p, info, route, cnt = _merge_route(x2, meta, seq, on, ap, acts, w_rnn_out[0], w_o[0],
                                      vec(g_ffn[0]), w_router_pad, b_router_pad)

    n_tiles_max = -(-(tp * TOP_K + N_EXPERTS * (EXPERT_TILE - 1)) // EXPERT_TILE)
    dest, te = _slots(route, cnt, n_tiles_max)
    dest_flat = dest[:TOP_K].reshape(-1)
    xs = _dispatch(dest_flat, u2p, n_tiles_max * EXPERT_TILE)
    ys = _expert_mlp(te[0, :n_tiles_max], te[1, :1], te[2, :n_tiles_max], xs, w_gate_up[0],
                     b_gate_up[0].reshape(N_EXPERTS, 1, -1).astype(F32), w_down[0],
                     b_down[0].reshape(N_EXPERTS, 1, -1).astype(F32))
    yg = _gather_expert_rows(dest_flat, ys)
    out = _combine(h1, info, vec(g_final), yg, n_batch, seq)
    return out.reshape(n_batch, seq, d)
```

```python
import functools

import jax
import jax.numpy as jnp
from jax import lax
from jax.experimental import pallas as pl
from jax.experimental.pallas import tpu as pltpu
from jax.experimental.pallas import tpu_sc as plsc

F32 = jnp.float32
BF16 = jnp.bfloat16

D_MODEL = 1024
N_META = 16
CHUNK = 128
CHUNK_PAD = CHUNK - N_META
CONV_WIDTH = 31
HEAD_DIM = 128
N_HEADS = D_MODEL // HEAD_DIM
N_EXPERTS = 32
TOP_K = 4
D_FF = D_MODEL
SWIGLU_LIMIT = 7.0
SWIGLU_ALPHA = 1.702
RMS_EPS = 1e-6
LN_EPS = 1e-5

ACT_AGLU, ACT_Q, ACT_K, ACT_V, ACT_OG, ACT_SGA, ACT_SGB = range(7)
N_ACTS = 7

LANES = 128
SUBLANES = 8
SUB_BLOCK = 32
N_SUB = CHUNK // SUB_BLOCK
HALO = 32
ROW_TILE = 640
IN_TILE = 320
SCAN_CHUNKS = 5
NORM_PARTS = 2
MERGE_PARTS = 4
SLOTS_STEPS = 5
ROUTE_ROWS = 16
CONV_ROWS = 64
CONV_SUMS = 4
EXPERT_TILE = 1024
EXPERT_PATH_ROWS = tuple(EXPERT_TILE >> k for k in (3, 2, 1, 0))
SC_MAX_INDICES = 128
SC_ROW_BUFFER_BYTES = 448 * 1024
SCATTER_RING = 2
GATHER_RING = 4
GATHER_LAG = 2
COMBINE_SUB = 8
NEG_BIG = -1e30
VMEM_LIMIT = 56 * 1024 * 1024


def _sigmoid(x):
    return 1.0 / (1.0 + jnp.exp(-x))


def _cparams(sem):
    return pltpu.CompilerParams(dimension_semantics=sem, vmem_limit_bytes=VMEM_LIMIT)


def _const_spec(shape):
    nd = len(shape)
    return pl.BlockSpec(shape, lambda *_: (0,) * nd)


PACK_WORDS = D_MODEL // 2
HIGH_HALF = 0xFFFF0000


def _pack_rows(x):
    lo = lax.bitcast_convert_type(x[:, :PACK_WORDS].astype(BF16).astype(F32), jnp.uint32)
    hi = lax.bitcast_convert_type(x[:, PACK_WORDS:].astype(BF16).astype(F32), jnp.uint32)
    return (lo >> 16) | (hi & jnp.uint32(HIGH_HALF))


def _unpack_rows(w):
    lo = lax.bitcast_convert_type(w << 16, F32)
    hi = lax.bitcast_convert_type(w & jnp.uint32(HIGH_HALF), F32)
    return jnp.concatenate([lo, hi], axis=1)


def _residual_tile(x_ref, meta_ref, is_first):
    x = x_ref[...]
    tm, d = x.shape
    prefix = jnp.concatenate([jnp.zeros((CHUNK_PAD, d), x.dtype), meta_ref[...]], axis=0)
    first = jnp.concatenate([prefix, x[:tm - CHUNK, :]], axis=0)
    return jnp.where(is_first, first, x)


def _residual_spec(tm, d, seq, tiles_per_batch, n_tiles):
    def start(i):
        i = jnp.minimum(i, n_tiles - 1)
        b, t = i // tiles_per_batch, i % tiles_per_batch
        return (pl.multiple_of(b * seq + jnp.maximum(t * tm - CHUNK, 0), SUBLANES), 0)
    return pl.BlockSpec((pl.Element(tm), pl.Element(d)), start)


IN_PROJ_ORDER = (0, 1, 2, 3, 5, 6, 7, 4)


def _in_proj_kernel(x_ref, meta_ref, g_ref, lbl_ref, w_hbm, acts_ref, logf_ref, w_ref, w_sem, *, tiles_per_batch):
    d = D_MODEL
    first_step = pl.program_id(0) == 0
    body = functools.partial(_in_proj_body, x_ref, meta_ref, g_ref, lbl_ref, w_ref, acts_ref, logf_ref, tiles_per_batch)

    def fetch(j):
        cols = pl.ds(j * d, d)
        return pltpu.make_async_copy(w_hbm.at[:, cols], w_ref.at[:, cols], w_sem.at[j])

    @pl.when(first_step)
    def _():
        for j in IN_PROJ_ORDER:
            fetch(j).start()
        body(lambda j: fetch(j).wait())

    @pl.when(jnp.logical_not(first_step))
    def _():
        body(lambda j: None)


def _in_proj_body(x_ref, meta_ref, g_ref, lbl_ref, w_ref, acts_ref, logf_ref, tiles_per_batch, wait_for):
    d = D_MODEL

    def put(group, value):
        acts_ref[:, group * d:(group + 1) * d] = value.astype(acts_ref.dtype)

    h = _residual_tile(x_ref, meta_ref, pl.program_id(0) % tiles_per_batch == 0)
    rows = h.shape[0] // NORM_PARTS
    u_parts = []
    for p in range(NORM_PARTS):
        hp = h[p * rows:(p + 1) * rows]
        ms = jnp.mean(hp * hp, axis=-1, keepdims=True)
        u_parts.append((hp * lax.rsqrt(ms + RMS_EPS) * g_ref[...]).astype(BF16))

    def proj(j):
        wait_for(j)
        w = w_ref[:, j * d:(j + 1) * d].astype(BF16)
        return jnp.concatenate([jnp.dot(u, w, preferred_element_type=F32) for u in u_parts], axis=0)

    put(ACT_AGLU, proj(0) * _sigmoid(proj(1)))
    zq = proj(2)
    put(ACT_Q, zq * _sigmoid(zq))
    lbl = lbl_ref[...]
    e = jnp.exp(lbl - jnp.max(lbl, axis=0, keepdims=True))
    lb = e[0:1, :] / jnp.sum(e, axis=0, keepdims=True)
    s = _sigmoid(proj(3))
    logf_ref[...] = jnp.log(lb + (1.0 - lb) * s)
    put(ACT_K, (1.0 - lb) * (1.0 - s))
    zg = proj(5)
    put(ACT_OG, zg * _sigmoid(zg))
    put(ACT_SGA, _sigmoid(proj(6)))
    put(ACT_SGB, _sigmoid(proj(7)))
    put(ACT_V, proj(4))


def _in_proj(x2, meta, seq, g_mix, lb_logits, w_in):
    d = x2.shape[1]
    n_batch = x2.shape[0] // seq
    tm = IN_TILE
    tpb = (seq + CHUNK) // tm
    tp = n_batch * (seq + CHUNK)
    assert tpb * tm == seq + CHUNK and tm > CHUNK
    row = pl.BlockSpec((tm, d), lambda i: (i, 0))
    return pl.pallas_call(
        functools.partial(_in_proj_kernel, tiles_per_batch=tpb),
        grid=(tp // tm,),
        in_specs=[_residual_spec(tm, d, seq, tpb, tp // tm), _const_spec(meta.shape), _const_spec((1, d)),
                  _const_spec(lb_logits.shape), pl.BlockSpec(memory_space=pl.ANY)],
        out_specs=[pl.BlockSpec((tm, N_ACTS * d), lambda i: (i, 0)), row],
        out_shape=[jax.ShapeDtypeStruct((tp, N_ACTS * d), BF16), jax.ShapeDtypeStruct((tp, d), F32)],
        scratch_shapes=[pltpu.VMEM(w_in.shape, w_in.dtype), pltpu.SemaphoreType.DMA((len(IN_PROJ_ORDER),))],
        compiler_params=_cparams(("arbitrary",)),
        name="in_proj",
    )(x2, meta, g_mix, lb_logits, w_in)


def _conv_kernel(halo_ref, cur_ref, wdw_ref, bdw_ref, lng_ref, lnb_ref, wout_ref, bout_ref, sga_ref,
                 out_ref, win_ref, acc_ref, shift_ref):
    tm = cur_ref.shape[0]
    win_ref[0:HALO, :] = halo_ref[...].astype(F32)
    win_ref[HALO:, :] = cur_ref[...].astype(F32)
    first_tap = HALO - (CONV_WIDTH - 1)
    for c in range(D_MODEL // LANES):
        lanes = slice(c * LANES, (c + 1) * LANES)
        n_rows = tm + HALO - SUBLANES
        for s in range(1, SUBLANES):
            shift_ref[s - 1, 0:n_rows, :] = win_ref[pl.ds(s, n_rows), lanes]
        bias = jnp.broadcast_to(bdw_ref[:, lanes], (CONV_ROWS, LANES))
        groups = CONV_ROWS // SUBLANES

        def chunk(r, carry, c=c, lanes=lanes, bias=bias):
            r0 = pl.multiple_of(r * CONV_ROWS, CONV_ROWS)
            accs = [bias.reshape(groups, SUBLANES, LANES)] + [None] * (CONV_SUMS - 1)
            for j in range(CONV_WIDTH):
                off = first_tap + j
                rows = pl.ds(r0 + off - off % SUBLANES, CONV_ROWS)
                w_j = wdw_ref[c, pl.ds(j, SUBLANES, stride=0), :]
                window = shift_ref[off % SUBLANES - 1, rows, :] if off % SUBLANES else win_ref[rows, lanes]
                term = w_j[None] * window.reshape(groups, SUBLANES, LANES)
                accs[j % CONV_SUMS] = term if accs[j % CONV_SUMS] is None else accs[j % CONV_SUMS] + term
            while len(accs) > 1:
                accs = [a + b for a, b in zip(accs[0::2], accs[1::2])]
            acc_ref[pl.ds(r0, CONV_ROWS), lanes] = accs[0].reshape(CONV_ROWS, LANES)
            return carry

        lax.fori_loop(0, tm // CONV_ROWS, chunk, 0)
    rows = tm // NORM_PARTS
    w_out = wout_ref[...].astype(BF16)
    ys = []
    for p in range(NORM_PARTS):
        a = acc_ref[p * rows:(p + 1) * rows, :]
        mu = jnp.mean(a, axis=-1, keepdims=True)
        ac = a - mu
        var = jnp.mean(ac * ac, axis=-1, keepdims=True)
        y = ac * lax.rsqrt(var + LN_EPS) * lng_ref[...] + lnb_ref[...]
        ys.append((y * _sigmoid(y)).astype(BF16))
    outs = [jnp.dot(y, w_out, preferred_element_type=F32) + bout_ref[...] for y in ys]
    for p in range(NORM_PARTS):
        part = slice(p * rows, (p + 1) * rows)
        out_ref[part, :] = (sga_ref[part, :].astype(F32) * outs[p]).astype(out_ref.dtype)


def _conv_branch(acts, w_dw, b_dw, ln_g, ln_b, w_out, b_out):
    tp, d = acts.shape[0], D_MODEL
    tm = ROW_TILE
    per = tm // HALO
    row = pl.BlockSpec((tm, d), lambda i: (i, 0))
    glu = pl.BlockSpec((tm, d), lambda i: (i, ACT_AGLU))
    gate = pl.BlockSpec((tm, d), lambda i: (i, ACT_SGA))
    halo = pl.BlockSpec((HALO, d), lambda i: (jnp.maximum(i * per - 1, 0), ACT_AGLU))
    vec = _const_spec((1, d))
    return pl.pallas_call(
        _conv_kernel,
        grid=(tp // tm,),
        in_specs=[halo, glu, _const_spec(w_dw.shape), vec, vec, vec, _const_spec((d, d)), vec, gate],
        out_specs=row,
        out_shape=jax.ShapeDtypeStruct((tp, d), BF16),
        scratch_shapes=[pltpu.VMEM((tm + HALO, d), F32), pltpu.VMEM((tm, d), F32),
                        pltpu.VMEM((SUBLANES - 1, tm + HALO - SUBLANES, LANES), F32)],
        compiler_params=_cparams(("parallel",)),
        name="conv_branch",
    )(acts, acts, w_dw, b_dw, ln_g, ln_b, w_out, b_out, acts)


def _split_bf16(x):
    hi = x.astype(BF16)
    return hi, (x - hi.astype(F32)).astype(BF16)


def _nt_dot(a, b):
    return lax.dot_general(a, b, (((1,), (1,)), ((), ())), preferred_element_type=F32)


def _tn_dot(a, b):
    return lax.dot_general(a, b, (((0,), (0,)), ((), ())), preferred_element_type=F32)


def _scan_kernel(q_ref, k_ref, v_ref, lf_ref, og_ref, gon_ref, o_ref, st_ref):
    @pl.when(pl.program_id(1) == 0)
    def _():
        st_ref[...] = jnp.zeros_like(st_ref)

    c = CHUNK
    row = lax.broadcasted_iota(jnp.int32, (c, c), 0)
    col = lax.broadcasted_iota(jnp.int32, (c, c), 1)
    causal = col <= row
    diag_mask = jnp.logical_and(causal, row // SUB_BLOCK == col // SUB_BLOCK)
    tri = causal.astype(BF16)

    def bcast_rows(rows):
        return jnp.concatenate([jnp.broadcast_to(r, (SUB_BLOCK, HEAD_DIM)) for r in rows], axis=0)

    heads = [slice(h * HEAD_DIM, (h + 1) * HEAD_DIM) for h in range(N_HEADS)]
    chunks = [slice(ci * c, (ci + 1) * c) for ci in range(SCAN_CHUNKS)]

    bcums = []
    for cs in chunks:
        lf = lf_ref[cs, :]
        lf_hi = lf.astype(BF16)
        lf_mid, lf_lo = _split_bf16(lf - lf_hi.astype(F32))
        bcums.append(jnp.dot(tri, lf_hi, preferred_element_type=F32)
                     + (jnp.dot(tri, lf_mid, preferred_element_type=F32)
                        + jnp.dot(tri, lf_lo, preferred_element_type=F32)))
    operands = {}
    for ci, cs in enumerate(chunks):
        for h, hs in enumerate(heads):
            b = bcums[ci][:, hs]
            q = q_ref[cs, hs].astype(F32)
            k = k_ref[cs, hs].astype(F32)
            ends = [b[i * SUB_BLOCK + SUB_BLOCK - 1:i * SUB_BLOCK + SUB_BLOCK, :] for i in range(N_SUB)]
            mids = [b[i * SUB_BLOCK + SUB_BLOCK // 2 - 1:i * SUB_BLOCK + SUB_BLOCK // 2, :] for i in range(N_SUB)]
            mid_full = bcast_rows(mids)
            b_last = ends[-1]
            zero_row = jnp.zeros_like(b_last)
            qm = q * jnp.exp(b - mid_full)
            km = k * jnp.exp(mid_full - b)
            q_parts, k_parts = [], []
            for j in range(N_SUB - 1):
                q_rows = [jnp.exp(mids[i] - ends[j]) if i > j else zero_row for i in range(N_SUB)]
                k_rows = [jnp.exp(ends[j] - mids[j]) if i == j else zero_row for i in range(N_SUB)]
                q_parts.append((qm * bcast_rows(q_rows)).astype(BF16))
                k_parts.append((km * bcast_rows(k_rows)).astype(BF16))
            q_in = (qm * bcast_rows([jnp.exp(m) for m in mids])).astype(BF16)
            k_out = (km * bcast_rows([jnp.exp(b_last - m) for m in mids])).astype(BF16)
            operands[ci, h] = (qm.astype(BF16), km.astype(BF16), jnp.concatenate(q_parts, axis=1),
                               jnp.concatenate(k_parts, axis=1), q_in, k_out, jnp.exp(b_last))
    products = {}
    for ci, cs in enumerate(chunks):
        for h, hs in enumerate(heads):
            qm, km, q_cat, k_cat, _, k_out, _ = operands[ci, h]
            products[ci, h] = (_nt_dot(qm, km), _nt_dot(q_cat, k_cat), _tn_dot(v_ref[cs, hs], k_out))
    within = {}
    for ci, cs in enumerate(chunks):
        for h, hs in enumerate(heads):
            same_block, earlier_blocks, _ = products[ci, h]
            scores = jnp.where(diag_mask, same_block, 0.0) + earlier_blocks
            within[ci, h] = jnp.dot(scores.astype(BF16), v_ref[cs, hs], preferred_element_type=F32)
    states = [st_ref[h] for h in range(N_HEADS)]
    outs = {}
    for ci in range(SCAN_CHUNKS):
        for h in range(N_HEADS):
            outs[ci, h] = within[ci, h] + _nt_dot(operands[ci, h][4], states[h].astype(BF16))
            states[h] = states[h] * operands[ci, h][6] + products[ci, h][2]
    for h in range(N_HEADS):
        st_ref[h] = states[h]
    for ci, cs in enumerate(chunks):
        for h, hs in enumerate(heads):
            o = outs[ci, h]
            ms = jnp.mean(o * o, axis=-1, keepdims=True)
            on = o * lax.rsqrt(ms + RMS_EPS) * gon_ref[:, hs]
            o_ref[cs, hs] = (on * og_ref[cs, hs].astype(F32)).astype(o_ref.dtype)


def _hgrn2_scan(acts, logf, g_onorm, n_batch):
    tp, d = logf.shape
    rows = SCAN_CHUNKS * CHUNK
    n_chunks = tp // n_batch // rows
    assert n_chunks * rows * n_batch == tp
    col = lambda group: pl.BlockSpec((rows, d), lambda b, c: (b * n_chunks + c, group))
    blk = col(0)
    return pl.pallas_call(
        _scan_kernel,
        grid=(n_batch, n_chunks),
        in_specs=[col(ACT_Q), col(ACT_K), col(ACT_V), blk, col(ACT_OG), pl.BlockSpec((1, d), lambda b, c: (0, 0))],
        out_specs=blk,
        out_shape=jax.ShapeDtypeStruct((tp, d), BF16),
        scratch_shapes=[pltpu.VMEM((N_HEADS, HEAD_DIM, HEAD_DIM), F32)],
        compiler_params=_cparams(("arbitrary", "arbitrary")),
        name="hgrn2_scan",
    )(acts, acts, acts, logf, acts, g_onorm)


def _merge_route_kernel(x_ref, meta_ref, on_ref, ap_ref, sgb_ref, wrnn_ref, wo_ref, gffn_ref, wr_ref, br_ref,
                        h1_ref, u2_ref, info_ref, route_ref, cnt_ref, tri_ref, carry_ref, colcnt_ref, logits_ref, hres_ref,
                        *, tiles_per_batch, n_tiles):
    tm = on_ref.shape[0]
    i = pl.program_id(0)
    hres_ref[...] = _residual_tile(x_ref, meta_ref, jnp.minimum(i, n_tiles - 1) % tiles_per_batch == 0)

    @pl.when(i == 0)
    def _():
        r_i = lax.broadcasted_iota(jnp.int32, (tm, tm), 0)
        c_i = lax.broadcasted_iota(jnp.int32, (tm, tm), 1)
        tri_ref[...] = (c_i < r_i).astype(BF16)
        carry_ref[...] = jnp.zeros_like(carry_ref)
        colcnt_ref[...] = jnp.zeros_like(colcnt_ref)
        logits_ref[...] = jnp.zeros_like(logits_ref)

    routed = i > 0
    lane = lax.broadcasted_iota(jnp.int32, (tm, LANES), 1)
    cur = jnp.where(lane < N_EXPERTS, logits_ref[...], NEG_BIG)
    vals, idxs, sels = [], [], []

    def topk_round(cur):
        m = jnp.max(cur, axis=-1, keepdims=True)
        idx = jnp.min(jnp.where(cur == m, lane, LANES), axis=-1, keepdims=True)
        sel = lane == idx
        vals.append(m)
        idxs.append(idx)
        sels.append(sel)
        return jnp.where(sel, 2.0 * NEG_BIG, cur)

    w_hi, w_lo = _split_bf16(wr_ref[...])
    w_hi_lo = jnp.concatenate([w_hi, w_lo], axis=1)
    part = tm // MERGE_PARTS
    rows = [slice(p * part, (p + 1) * part) for p in range(MERGE_PARTS)]
    w_rnn = wrnn_ref[...].astype(BF16)
    w_out = wo_ref[...].astype(BF16)
    rs = [jnp.dot(on_ref[rw, :], w_rnn, preferred_element_type=F32) for rw in rows]
    cur = topk_round(cur)
    ys = [(ap_ref[rw, :].astype(F32) + sgb_ref[rw, :].astype(F32) * r).astype(BF16) for rw, r in zip(rows, rs)]
    h1s = [hres_ref[rw, :] + jnp.dot(y, w_out, preferred_element_type=F32) for rw, y in zip(rows, ys)]
    cur = topk_round(cur)
    u2s = []
    for rw, h1 in zip(rows, h1s):
        h1_ref[rw, :] = h1
        ms = jnp.mean(h1 * h1, axis=-1, keepdims=True)
        u2 = h1 * lax.rsqrt(ms + RMS_EPS) * gffn_ref[...]
        u2_ref[rw, :] = _pack_rows(u2)
        u2s.append(u2)
    cur = topk_round(cur)
    new_logits = []
    for u2 in u2s:
        u_hi, u_lo = _split_bf16(u2)
        both = jnp.dot(u_hi, w_hi_lo, preferred_element_type=F32)
        new_logits.append(both[:, :LANES] + (jnp.dot(u_lo, w_hi, preferred_element_type=F32) + both[:, LANES:])
                          + br_ref[...])
    cur = topk_round(cur)
    assert len(vals) == TOP_K
    exps = [jnp.exp(vk - vals[0]) for vk in vals]
    den = exps[0] + exps[1] + exps[2] + exps[3]
    onehot = jnp.logical_or(jnp.logical_or(sels[0], sels[1]), jnp.logical_or(sels[2], sels[3]))
    onehot = jnp.logical_and(onehot, routed)
    onehot_bf = onehot.astype(BF16)
    rank_all = jnp.dot(tri_ref[...], onehot_bf, preferred_element_type=F32) + carry_ref[...]
    carry_ref[...] += jnp.sum(onehot.astype(F32), axis=0, keepdims=True)
    colcnt_ref[...] += _tn_dot(onehot_bf, jnp.ones((tm, LANES), BF16))
    info = jnp.zeros((tm, LANES), F32)
    for kk in range(TOP_K):
        rank_k = jnp.sum(jnp.where(sels[kk], rank_all, 0.0), axis=-1, keepdims=True)
        info = jnp.where(lane == kk, exps[kk] / den, info)
        info = jnp.where(lane == TOP_K + kk, idxs[kk].astype(F32), info)
        info = jnp.where(lane == 2 * TOP_K + kk, rank_k, info)
    info_ref[...] = info
    route_ref[...] = info.T[0:ROUTE_ROWS, :]
    cnt_ref[...] = colcnt_ref[...]
    logits_ref[...] = jnp.concatenate(new_logits, axis=0)


def _merge_route(x2, meta, seq, on, ap, acts, w_rnn, w_o, g_ffn, w_router_pad, b_router_pad):
    tp, d = on.shape
    tm = ROW_TILE
    n_tiles = tp // tm
    tpb = (seq + CHUNK) // tm
    assert tpb * tm == seq + CHUNK
    this_tile = lambda i: (jnp.minimum(i, n_tiles - 1), 0)
    prev_tile = lambda i: (jnp.maximum(i - 1, 0), 0)
    row = pl.BlockSpec((tm, d), this_tile)
    vec = _const_spec((1, d))
    return pl.pallas_call(
        functools.partial(_merge_route_kernel, tiles_per_batch=tpb, n_tiles=n_tiles),
        grid=(n_tiles + 1,),
        in_specs=[_residual_spec(tm, d, seq, tpb, n_tiles), _const_spec(meta.shape), row, row,
                  pl.BlockSpec((tm, d), lambda i: (jnp.minimum(i, n_tiles - 1), ACT_SGB)),
                  _const_spec((d, d)), _const_spec((d, d)), vec, _const_spec((d, LANES)), _const_spec((1, LANES))],
        out_specs=[row, pl.BlockSpec((tm, PACK_WORDS), this_tile),
                   pl.BlockSpec((tm, LANES), prev_tile),
                   pl.BlockSpec((ROUTE_ROWS, tm), lambda i: (0, jnp.maximum(i - 1, 0))), _const_spec((LANES, LANES))],
        out_shape=[jax.ShapeDtypeStruct((tp, d), F32), jax.ShapeDtypeStruct((tp, PACK_WORDS), jnp.uint32),
                   jax.ShapeDtypeStruct((tp, LANES), F32), jax.ShapeDtypeStruct((ROUTE_ROWS, tp), F32),
                   jax.ShapeDtypeStruct((LANES, LANES), F32)],
        scratch_shapes=[pltpu.VMEM((tm, tm), BF16), pltpu.VMEM((1, LANES), F32),
                        pltpu.VMEM((LANES, LANES), F32), pltpu.VMEM((tm, LANES), F32), pltpu.VMEM((tm, d), F32)],
        compiler_params=_cparams(("arbitrary",)),
        name="merge_route",
    )(x2, meta, on, ap, acts, w_rnn, w_o, g_ffn, w_router_pad, b_router_pad)


def _slots_kernel(route_ref, cnt_ref, dest_ref, te_ref, pstart_ref):
    tl = route_ref.shape[1]
    n_tile_lanes = te_ref.shape[1]

    @pl.when(pl.program_id(0) == 0)
    def _():
        r_i = lax.broadcasted_iota(jnp.int32, (LANES, LANES), 0)
        c_i = lax.broadcasted_iota(jnp.int32, (LANES, LANES), 1)
        cnt = cnt_ref[...]
        tiles = jnp.floor((cnt + (EXPERT_TILE - 1)) / EXPERT_TILE)
        pend_col = jnp.dot((c_i <= r_i).astype(F32), tiles, preferred_element_type=F32,
                           precision=lax.Precision.HIGHEST)
        pstart_ref[...] = pend_col - tiles
        tile_id = lax.broadcasted_iota(jnp.int32, (LANES, n_tile_lanes), 1).astype(F32)
        exp_id = lax.broadcasted_iota(jnp.int32, (LANES, n_tile_lanes), 0)
        pend_wide = jnp.concatenate([pend_col] * (n_tile_lanes // LANES), axis=1)
        below = jnp.logical_and(pend_wide <= tile_id, exp_id < N_EXPERTS)
        te = jnp.minimum(jnp.sum(below.astype(F32), axis=0, keepdims=True), N_EXPERTS - 1.0)
        n_active = pend_col[N_EXPERTS - 1:N_EXPERTS, 0:1]
        owner = exp_id.astype(F32) == te
        cnt_wide = jnp.concatenate([cnt] * (n_tile_lanes // LANES), axis=1)
        first_tile = pend_wide - jnp.concatenate([tiles] * (n_tile_lanes // LANES), axis=1)
        left = jnp.sum(jnp.where(owner, cnt_wide - (tile_id - first_tile) * EXPERT_TILE, 0.0), axis=0, keepdims=True)
        valid = jnp.clip(left, 0.0, float(EXPERT_TILE))
        sub_t = lax.broadcasted_iota(jnp.int32, (SUBLANES, n_tile_lanes), 0)
        out = jnp.where(sub_t == 0, jnp.broadcast_to(te, (SUBLANES, n_tile_lanes)),
                        jnp.where(sub_t == 1, jnp.broadcast_to(n_active, (SUBLANES, n_tile_lanes)),
                                  jnp.broadcast_to(valid, (SUBLANES, n_tile_lanes))))
        te_ref[...] = out.astype(jnp.int32)

    route = route_ref[...]
    expert = lax.broadcasted_iota(jnp.int32, (N_EXPERTS, tl), 0)
    first_slot = jnp.concatenate([pstart_ref[0:N_EXPERTS, :]] * (tl // LANES), axis=1) * EXPERT_TILE
    rows = []
    for kk in range(TOP_K):
        idx_k = route[TOP_K + kk:TOP_K + kk + 1, :].astype(jnp.int32)
        start_k = jnp.sum(jnp.where(expert == idx_k, first_slot, 0.0), axis=0, keepdims=True)
        rows.append(start_k + route[2 * TOP_K + kk:2 * TOP_K + kk + 1, :])
    rows.append(jnp.zeros((SUBLANES - TOP_K, tl), F32))
    dest_ref[...] = jnp.concatenate(rows, axis=0).astype(jnp.int32)


def _slots(route, cnt, n_tiles_max):
    tp = route.shape[1]
    tl = tp // SLOTS_STEPS
    assert tl * SLOTS_STEPS == tp and tl % LANES == 0
    n_tile_lanes = -(-n_tiles_max // LANES) * LANES
    return pl.pallas_call(
        _slots_kernel,
        grid=(SLOTS_STEPS,),
        in_specs=[pl.BlockSpec((ROUTE_ROWS, tl), lambda i: (0, i)), _const_spec((LANES, LANES))],
        out_specs=[pl.BlockSpec((SUBLANES, tl), lambda i: (0, i)), _const_spec((8, n_tile_lanes))],
        out_shape=[jax.ShapeDtypeStruct((SUBLANES, tp), jnp.int32),
                   jax.ShapeDtypeStruct((8, n_tile_lanes), jnp.int32)],
        scratch_shapes=[pltpu.VMEM((LANES, LANES), F32)],
        compiler_params=_cparams(("arbitrary",)),
        name="slots",
    )(route, cnt)


def _sc_workers():
    sc = plsc.get_sparse_core_info()
    return sc.num_cores, sc.num_cores * sc.num_subcores


def _sc_split(tp, ring):
    n_workers = _sc_workers()[1]
    per_worker = tp // n_workers
    assert per_worker * n_workers == tp
    limit = min(SC_MAX_INDICES, SC_ROW_BUFFER_BYTES // (ring * PACK_WORDS * 4))
    return per_worker, max(c for c in range(SUBLANES, limit + 1, SUBLANES) if per_worker % c == 0)


def _sc_first_token(per_worker):
    return pl.multiple_of((lax.axis_index("s") * _sc_workers()[0] + lax.axis_index("c")) * per_worker, SUBLANES)


def _sc_ring_kernel(out_type, n_index_rows, chunk, ring):
    return functools.partial(
        pl.kernel, mesh=plsc.VectorSubcoreMesh(core_axis_name="c", subcore_axis_name="s"), out_type=out_type,
        scratch_types=[pltpu.VMEM((n_index_rows, chunk), jnp.int32), pltpu.VMEM((ring, chunk, PACK_WORDS), jnp.uint32),
                       pltpu.SemaphoreType.DMA, pltpu.SemaphoreType.DMA((ring,)), pltpu.SemaphoreType.DMA((ring,))])


def _wait_all(copies):
    for c in copies:
        c.wait()


def _dispatch(dest_flat, u2p, n_slots):
    tp = u2p.shape[0]
    per_worker, ch = _sc_split(tp, SCATTER_RING)
    n_chunks = per_worker // ch

    @_sc_ring_kernel(jax.ShapeDtypeStruct((n_slots, PACK_WORDS), jnp.uint32), n_chunks * TOP_K, ch, SCATTER_RING)
    def scatter_rows(u2_hbm, dest_hbm, xs_hbm, idx_v, rows_v, sem_idx, sem_load, sem_scatter):
        base = _sc_first_token(per_worker)
        _wait_all([pltpu.async_copy(dest_hbm.at[pl.ds(kk * tp + base + c * ch, ch)], idx_v.at[c * TOP_K + kk], sem_idx)
                   for c in range(n_chunks) for kk in range(TOP_K)])

        def load(c):
            slot = c % SCATTER_RING
            return pltpu.async_copy(u2_hbm.at[pl.ds(base + c * ch, ch)], rows_v.at[slot], sem_load.at[slot])

        def scatter(c):
            slot = c % SCATTER_RING
            return [pltpu.async_copy(rows_v.at[slot], xs_hbm.at[idx_v.at[c * TOP_K + kk]], sem_scatter.at[slot])
                    for kk in range(TOP_K)]

        loads, scatters = {}, {}
        for c in range(n_chunks + 1):
            if c < n_chunks:
                if c >= SCATTER_RING:
                    _wait_all(scatters.pop(c - SCATTER_RING))
                loads[c] = load(c)
            if c >= 1:
                loads.pop(c - 1).wait()
                scatters[c - 1] = scatter(c - 1)
        for copies in scatters.values():
            _wait_all(copies)

    return scatter_rows(u2p, dest_flat)


def _gather_expert_rows(dest_flat, ys):
    tp = dest_flat.shape[0] // TOP_K
    per_worker, ch = _sc_split(tp, GATHER_RING)
    units = [(c, kk) for c in range(per_worker // ch) for kk in range(TOP_K)]

    @_sc_ring_kernel(jax.ShapeDtypeStruct((TOP_K, tp, PACK_WORDS), jnp.uint32), len(units), ch, GATHER_RING)
    def gather_rows(ys_hbm, dest_hbm, out_hbm, idx_v, rows_v, sem_idx, sem_gather, sem_write):
        base = _sc_first_token(per_worker)
        _wait_all([pltpu.async_copy(dest_hbm.at[pl.ds(kk * tp + base + c * ch, ch)], idx_v.at[u], sem_idx)
                   for u, (c, kk) in enumerate(units)])

        def gather(u):
            slot = u % GATHER_RING
            return pltpu.async_copy(ys_hbm.at[idx_v.at[u]], rows_v.at[slot], sem_gather.at[slot])

        def write(u):
            c, kk = units[u]
            slot = u % GATHER_RING
            return pltpu.async_copy(rows_v.at[slot], out_hbm.at[kk, pl.ds(base + c * ch, ch)], sem_write.at[slot])

        gathers, writes = {}, {}
        for u in range(len(units) + GATHER_LAG):
            if u < len(units):
                if u >= GATHER_RING:
                    writes.pop(u - GATHER_RING).wait()
                gathers[u] = gather(u)
            if u >= GATHER_LAG:
                gathers.pop(u - GATHER_LAG).wait()
                writes[u - GATHER_LAG] = write(u - GATHER_LAG)
        _wait_all(writes.values())

    return gather_rows(ys, dest_flat)


def _expert_kernel(te_ref, nact_ref, valid_ref, x_ref, wgu_hbm, bgu_ref, wdn_hbm, bdn_ref, y_ref,
                   wgu_buf, wdn_buf, slot_ref, sem):
    i = pl.program_id(0)
    n_active = nact_ref[0]
    last_tile = pl.num_programs(0) - 1

    def fetch(expert, slot):
        return (pltpu.make_async_copy(wgu_hbm.at[expert], wgu_buf.at[slot], sem.at[slot, 0]),
                pltpu.make_async_copy(wdn_hbm.at[expert], wdn_buf.at[slot], sem.at[slot, 1]))

    @pl.when(i == 0)
    def _():
        slot_ref[0] = 1
        for copy in fetch(te_ref[0], 0):
            copy.start()

    @pl.when(i < n_active)
    def _():
        expert = te_ref[i]
        first_tile_of_expert = jnp.logical_or(i == 0, expert != te_ref[jnp.maximum(i - 1, 0)])

        @pl.when(first_tile_of_expert)
        def _():
            slot = 1 - slot_ref[0]
            slot_ref[0] = slot
            for copy in fetch(expert, slot):
                copy.wait()
            nxt = lax.while_loop(
                lambda j: jnp.logical_and(j < n_active, te_ref[jnp.minimum(j, last_tile)] == expert),
                lambda j: j + 1, i + 1)

            @pl.when(nxt < n_active)
            def _():
                for copy in fetch(te_ref[jnp.minimum(nxt, last_tile)], 1 - slot):
                    copy.start()

        slot = slot_ref[0]

        def mlp(rows):
            x = _unpack_rows(x_ref[0:rows, :]).astype(BF16)
            gu = jnp.dot(x, wgu_buf[slot].astype(BF16), preferred_element_type=F32) + bgu_ref[0]
            gate = jnp.minimum(gu[:, :D_FF], SWIGLU_LIMIT)
            up = jnp.clip(gu[:, D_FF:], -SWIGLU_LIMIT, SWIGLU_LIMIT)
            hdn = (up + 1.0) * (gate * _sigmoid(SWIGLU_ALPHA * gate))
            y = jnp.dot(hdn.astype(BF16), wdn_buf[slot].astype(BF16), preferred_element_type=F32) + bdn_ref[0]
            y_ref[0:rows, :] = _pack_rows(y)

        path = sum((valid_ref[i] > rows).astype(jnp.int32) for rows in EXPERT_PATH_ROWS[:-1])
        for k, rows in enumerate(EXPERT_PATH_ROWS):
            @pl.when(path == k)
            def _(rows=rows):
                mlp(rows)
                if rows < EXPERT_TILE:
                    y_ref[rows:, :] = jnp.zeros((EXPERT_TILE - rows, PACK_WORDS), y_ref.dtype)

    @pl.when(i >= n_active)
    def _():
        y_ref[...] = jnp.zeros_like(y_ref)


def _expert_mlp(te, nact, valid, xs, w_gu, b_gu, w_dn, b_dn):
    d = D_MODEL
    n_tiles = xs.shape[0] // EXPERT_TILE

    def tile_map(i, te_ref, nact_ref, valid_ref):
        return (jnp.minimum(i, nact_ref[0] - 1), 0)

    def exp_map(i, te_ref, nact_ref, valid_ref):
        return (te_ref[jnp.minimum(i, nact_ref[0] - 1)], 0, 0)

    grid_spec = pltpu.PrefetchScalarGridSpec(
        num_scalar_prefetch=3,
        grid=(n_tiles,),
        in_specs=[pl.BlockSpec((EXPERT_TILE, PACK_WORDS), tile_map),
                  pl.BlockSpec(memory_space=pl.ANY),
                  pl.BlockSpec((1, 1, 2 * D_FF), exp_map),
                  pl.BlockSpec(memory_space=pl.ANY),
                  pl.BlockSpec((1, 1, d), exp_map)],
        out_specs=pl.BlockSpec((EXPERT_TILE, PACK_WORDS), lambda i, te_ref, nact_ref, valid_ref: (i, 0)),
        scratch_shapes=[pltpu.VMEM((2, d, 2 * D_FF), F32), pltpu.VMEM((2, D_FF, d), F32),
                        pltpu.SMEM((1,), jnp.int32), pltpu.SemaphoreType.DMA((2, 2))],
    )
    return pl.pallas_call(
        _expert_kernel,
        grid_spec=grid_spec,
        out_shape=jax.ShapeDtypeStruct(xs.shape, jnp.uint32),
        compiler_params=_cparams(("arbitrary",)),
        name="expert_mlp",
    )(te, nact, valid, xs, w_gu, b_gu, w_dn, b_dn)


def _combine_kernel(gfin_ref, *refs):
    out_ref = refs[-1]
    for s in range(COMBINE_SUB):
        h1_ref, info_ref, yg_ref = refs[3 * s:3 * s + 3]
        info = info_ref[...]
        h2 = h1_ref[...]
        for kk in range(TOP_K):
            h2 = h2 + info[:, kk:kk + 1] * _unpack_rows(yg_ref[kk])
        ms = jnp.mean(h2 * h2, axis=-1, keepdims=True)
        out_ref[s * CHUNK:(s + 1) * CHUNK, :] = h2 * lax.rsqrt(ms + RMS_EPS) * gfin_ref[...]


def _combine(h1, info, g_final, yg, n_batch, seq):
    d = h1.shape[1]
    chunks = seq // CHUNK

    def padded(s):
        def index(j):
            c = j * COMBINE_SUB + s
            return (c // chunks) * (chunks + 1) + c % chunks + 1
        return index

    in_specs = [pl.BlockSpec((1, d), lambda j: (0, 0))]
    operands = [g_final]
    for s in range(COMBINE_SUB):
        chunk_of = padded(s)
        in_specs += [pl.BlockSpec((CHUNK, d), lambda j, f=chunk_of: (f(j), 0)),
                     pl.BlockSpec((CHUNK, LANES), lambda j, f=chunk_of: (f(j), 0)),
                     pl.BlockSpec((TOP_K, CHUNK, PACK_WORDS), lambda j, f=chunk_of: (0, f(j), 0))]
        operands += [h1, info, yg]
    return pl.pallas_call(
        _combine_kernel,
        grid=(n_batch * chunks // COMBINE_SUB,),
        in_specs=in_specs,
        out_specs=pl.BlockSpec((COMBINE_SUB * CHUNK, d), lambda j: (j, 0)),
        out_shape=jax.ShapeDtypeStruct((n_batch * seq, d), F32),
        compiler_params=_cparams(("parallel",)),
        name="combine",
    )(*operands)


def kernel(x, meta_tokens, lb_logits, g_mix, w_in, w_dw, b_dw, ln_g, ln_b, w_conv_out, b_conv_out,
           g_onorm, w_rnn_out, w_o, g_ffn, w_router, b_router, w_gate_up, b_gate_up, w_down, b_down,
           g_final):
    n_batch, seq, d = x.shape
    assert d == D_MODEL and w_in.shape[0] == 1, "single-layer block with D_MODEL features"
    assert seq % (CHUNK * COMBINE_SUB) == 0
    seq_pad = CHUNK + seq
    tp = n_batch * seq_pad
    assert tp % ROW_TILE == 0

    x2 = x.reshape(n_batch * seq, d)
    meta = meta_tokens.astype(x.dtype)
    vec = lambda a: a.reshape(1, -1).astype(F32)

    acts, logf = _in_proj(x2, meta, seq, vec(g_mix[0]), lb_logits.astype(F32), w_in[0])
    w_taps = w_dw[0].astype(F32).reshape(CONV_WIDTH, d // LANES, LANES).transpose(1, 0, 2)
    ap = _conv_branch(acts, w_taps, vec(b_dw[0]), vec(ln_g[0]), vec(ln_b[0]),
                      w_conv_out[0], vec(b_conv_out[0]))
    on = _hgrn2_scan(acts, logf, vec(g_onorm[0]), n_batch)
    w_router_pad = jnp.pad(w_router[0].astype(F32), ((0, 0), (0, LANES - N_EXPERTS)))
    b_router_pad = jnp.pad(vec(b_router[0]), ((0, 0), (0, LANES - N_EXPERTS)))
    h1, u2p, info, route, cnt = _merge_route(x2, meta, seq, on, ap, acts, w_rnn_out[0], w_o[0],
                                      vec(g_ffn[0]), w_router_pad, b_router_pad)

    n_tiles_max = -(-(tp * TOP_K + N_EXPERTS * (EXPERT_TILE - 1)) // EXPERT_TILE)
    dest, te = _slots(route, cnt, n_tiles_max)
    dest_flat = dest[:TOP_K].reshape(-1)
    xs = _dispatch(dest_flat, u2p, n_tiles_max * EXPERT_TILE)
    ys = _expert_mlp(te[0, :n_tiles_max], te[1, :1], te[2, :n_tiles_max], xs, w_gate_up[0],
                     b_gate_up[0].reshape(N_EXPERTS, 1, -1).astype(F32), w_down[0],
                     b_down[0].reshape(N_EXPERTS, 1, -1).astype(F32))
    yg = _gather_expert_rows(dest_flat, ys)
    out = _combine(h1, info, vec(g_final), yg, n_batch, seq)
    return out.reshape(n_batch, seq, d)
```

```python
import functools

import jax
import jax.numpy as jnp
from jax import lax
from jax.experimental import pallas as pl
from jax.experimental.pallas import tpu as pltpu
from jax.experimental.pallas import tpu_sc as plsc

F32 = jnp.float32
BF16 = jnp.bfloat16

D_MODEL = 1024
N_META = 16
CHUNK = 128
CHUNK_PAD = CHUNK - N_META
CONV_WIDTH = 31
HEAD_DIM = 128
N_HEADS = D_MODEL // HEAD_DIM
N_EXPERTS = 32
TOP_K = 4
D_FF = D_MODEL
SWIGLU_LIMIT = 7.0
SWIGLU_ALPHA = 1.702
RMS_EPS = 1e-6
LN_EPS = 1e-5

ACT_AGLU, ACT_Q, ACT_K, ACT_V, ACT_OG, ACT_SGA, ACT_SGB = range(7)
N_ACTS = 7

LANES = 128
SUBLANES = 8
SUB_BLOCK = 32
N_SUB = CHUNK // SUB_BLOCK
HALO = 32
ROW_TILE = 640
IN_TILE = 320
SCAN_CHUNKS = 5
NORM_PARTS = 2
MERGE_PARTS = 4
SLOTS_STEPS = 5
ROUTE_ROWS = 16
CONV_ROWS = 64
CONV_SUMS = 4
EXPERT_TILE = 1024
EXPERT_PATH_ROWS = tuple(EXPERT_TILE >> k for k in (3, 2, 1, 0))
SC_MAX_INDICES = 128
SC_ROW_BUFFER_BYTES = 448 * 1024
SCATTER_RING = 2
GATHER_RING = 4
GATHER_LAG = 2
COMBINE_SUB = 8
NEG_BIG = -1e30
VMEM_LIMIT = 56 * 1024 * 1024


def _sigmoid(x):
    return 1.0 / (1.0 + jnp.exp(-x))


def _cparams(sem):
    return pltpu.CompilerParams(dimension_semantics=sem, vmem_limit_bytes=VMEM_LIMIT)


def _const_spec(shape):
    nd = len(shape)
    return pl.BlockSpec(shape, lambda *_: (0,) * nd)


PACK_WORDS = D_MODEL // 2
HIGH_HALF = 0xFFFF0000


def _pack_rows(x):
    lo = lax.bitcast_convert_type(x[:, :PACK_WORDS].astype(BF16).astype(F32), jnp.uint32)
    hi = lax.bitcast_convert_type(x[:, PACK_WORDS:].astype(BF16).astype(F32), jnp.uint32)
    return (lo >> 16) | (hi & jnp.uint32(HIGH_HALF))


def _unpack_rows(w):
    lo = lax.bitcast_convert_type(w << 16, F32)
    hi = lax.bitcast_convert_type(w & jnp.uint32(HIGH_HALF), F32)
    return jnp.concatenate([lo, hi], axis=1)


def _residual_tile(x_ref, meta_ref, is_first):
    x = x_ref[...]
    tm, d = x.shape
    prefix = jnp.concatenate([jnp.zeros((CHUNK_PAD, d), x.dtype), meta_ref[...]], axis=0)
    first = jnp.concatenate([prefix, x[:tm - CHUNK, :]], axis=0)
    return jnp.where(is_first, first, x)


def _residual_spec(tm, d, seq, tiles_per_batch, n_tiles):
    def start(i):
        i = jnp.minimum(i, n_tiles - 1)
        b, t = i // tiles_per_batch, i % tiles_per_batch
        return (pl.multiple_of(b * seq + jnp.maximum(t * tm - CHUNK, 0), SUBLANES), 0)
    return pl.BlockSpec((pl.Element(tm), pl.Element(d)), start)


def _in_proj_kernel(x_ref, meta_ref, g_ref, lbl_ref, w_ref, acts_ref, logf_ref, *, tiles_per_batch):
    d = D_MODEL

    def put(group, value):
        acts_ref[:, group * d:(group + 1) * d] = value.astype(acts_ref.dtype)

    h = _residual_tile(x_ref, meta_ref, pl.program_id(0) % tiles_per_batch == 0)
    rows = h.shape[0] // NORM_PARTS
    u_parts = []
    for p in range(NORM_PARTS):
        hp = h[p * rows:(p + 1) * rows]
        ms = jnp.mean(hp * hp, axis=-1, keepdims=True)
        u_parts.append((hp * lax.rsqrt(ms + RMS_EPS) * g_ref[...]).astype(BF16))

    def proj(j):
        w = w_ref[:, j * d:(j + 1) * d].astype(BF16)
        return jnp.concatenate([jnp.dot(u, w, preferred_element_type=F32) for u in u_parts], axis=0)

    put(ACT_AGLU, proj(0) * _sigmoid(proj(1)))
    zq = proj(2)
    put(ACT_Q, zq * _sigmoid(zq))
    lbl = lbl_ref[...]
    e = jnp.exp(lbl - jnp.max(lbl, axis=0, keepdims=True))
    lb = e[0:1, :] / jnp.sum(e, axis=0, keepdims=True)
    s = _sigmoid(proj(3))
    logf_ref[...] = jnp.log(lb + (1.0 - lb) * s)
    put(ACT_K, (1.0 - lb) * (1.0 - s))
    zg = proj(5)
    put(ACT_OG, zg * _sigmoid(zg))
    put(ACT_SGA, _sigmoid(proj(6)))
    put(ACT_SGB, _sigmoid(proj(7)))
    put(ACT_V, proj(4))


def _in_proj(x2, meta, seq, g_mix, lb_logits, w_in):
    d = x2.shape[1]
    n_batch = x2.shape[0] // seq
    tm = IN_TILE
    tpb = (seq + CHUNK) // tm
    tp = n_batch * (seq + CHUNK)
    assert tpb * tm == seq + CHUNK and tm > CHUNK
    row = pl.BlockSpec((tm, d), lambda i: (i, 0))
    return pl.pallas_call(
        functools.partial(_in_proj_kernel, tiles_per_batch=tpb),
        grid=(tp // tm,),
        in_specs=[_residual_spec(tm, d, seq, tpb, tp // tm), _const_spec(meta.shape), _const_spec((1, d)),
                  _const_spec(lb_logits.shape),
                  pl.BlockSpec(w_in.shape, lambda i: (0, 0), pipeline_mode=pl.Buffered(1))],
        out_specs=[pl.BlockSpec((tm, N_ACTS * d), lambda i: (i, 0)), row],
        out_shape=[jax.ShapeDtypeStruct((tp, N_ACTS * d), BF16), jax.ShapeDtypeStruct((tp, d), F32)],
        compiler_params=_cparams(("parallel",)),
        name="in_proj",
    )(x2, meta, g_mix, lb_logits, w_in)


def _conv_kernel(halo_ref, cur_ref, wdw_ref, bdw_ref, lng_ref, lnb_ref, wout_ref, bout_ref, sga_ref,
                 out_ref, win_ref, acc_ref, shift_ref):
    tm = cur_ref.shape[0]
    win_ref[0:HALO, :] = halo_ref[...].astype(F32)
    win_ref[HALO:, :] = cur_ref[...].astype(F32)
    first_tap = HALO - (CONV_WIDTH - 1)
    for c in range(D_MODEL // LANES):
        lanes = slice(c * LANES, (c + 1) * LANES)
        n_rows = tm + HALO - SUBLANES
        for s in range(1, SUBLANES):
            shift_ref[s - 1, 0:n_rows, :] = win_ref[pl.ds(s, n_rows), lanes]
        bias = jnp.broadcast_to(bdw_ref[:, lanes], (CONV_ROWS, LANES))
        groups = CONV_ROWS // SUBLANES

        def chunk(r, carry, c=c, lanes=lanes, bias=bias):
            r0 = pl.multiple_of(r * CONV_ROWS, CONV_ROWS)
            accs = [bias.reshape(groups, SUBLANES, LANES)] + [None] * (CONV_SUMS - 1)
            for j in range(CONV_WIDTH):
                off = first_tap + j
                rows = pl.ds(r0 + off - off % SUBLANES, CONV_ROWS)
                w_j = wdw_ref[c, pl.ds(j, SUBLANES, stride=0), :]
                window = shift_ref[off % SUBLANES - 1, rows, :] if off % SUBLANES else win_ref[rows, lanes]
                term = w_j[None] * window.reshape(groups, SUBLANES, LANES)
                accs[j % CONV_SUMS] = term if accs[j % CONV_SUMS] is None else accs[j % CONV_SUMS] + term
            while len(accs) > 1:
                accs = [a + b for a, b in zip(accs[0::2], accs[1::2])]
            acc_ref[pl.ds(r0, CONV_ROWS), lanes] = accs[0].reshape(CONV_ROWS, LANES)
            return carry

        lax.fori_loop(0, tm // CONV_ROWS, chunk, 0)
    rows = tm // NORM_PARTS
    w_out = wout_ref[...].astype(BF16)
    ys = []
    for p in range(NORM_PARTS):
        a = acc_ref[p * rows:(p + 1) * rows, :]
        mu = jnp.mean(a, axis=-1, keepdims=True)
        ac = a - mu
        var = jnp.mean(ac * ac, axis=-1, keepdims=True)
        y = ac * lax.rsqrt(var + LN_EPS) * lng_ref[...] + lnb_ref[...]
        ys.append((y * _sigmoid(y)).astype(BF16))
    outs = [jnp.dot(y, w_out, preferred_element_type=F32) + bout_ref[...] for y in ys]
    for p in range(NORM_PARTS):
        part = slice(p * rows, (p + 1) * rows)
        out_ref[part, :] = (sga_ref[part, :].astype(F32) * outs[p]).astype(out_ref.dtype)


def _conv_branch(acts, w_dw, b_dw, ln_g, ln_b, w_out, b_out):
    tp, d = acts.shape[0], D_MODEL
    tm = ROW_TILE
    per = tm // HALO
    row = pl.BlockSpec((tm, d), lambda i: (i, 0))
    glu = pl.BlockSpec((tm, d), lambda i: (i, ACT_AGLU))
    gate = pl.BlockSpec((tm, d), lambda i: (i, ACT_SGA))
    halo = pl.BlockSpec((HALO, d), lambda i: (jnp.maximum(i * per - 1, 0), ACT_AGLU))
    vec = _const_spec((1, d))
    return pl.pallas_call(
        _conv_kernel,
        grid=(tp // tm,),
        in_specs=[halo, glu, _const_spec(w_dw.shape), vec, vec, vec, _const_spec((d, d)), vec, gate],
        out_specs=row,
        out_shape=jax.ShapeDtypeStruct((tp, d), BF16),
        scratch_shapes=[pltpu.VMEM((tm + HALO, d), F32), pltpu.VMEM((tm, d), F32),
                        pltpu.VMEM((SUBLANES - 1, tm + HALO - SUBLANES, LANES), F32)],
        compiler_params=_cparams(("parallel",)),
        name="conv_branch",
    )(acts, acts, w_dw, b_dw, ln_g, ln_b, w_out, b_out, acts)


def _split_bf16(x):
    hi = x.astype(BF16)
    return hi, (x - hi.astype(F32)).astype(BF16)


def _nt_dot(a, b):
    return lax.dot_general(a, b, (((1,), (1,)), ((), ())), preferred_element_type=F32)


def _tn_dot(a, b):
    return lax.dot_general(a, b, (((0,), (0,)), ((), ())), preferred_element_type=F32)


def _scan_kernel(q_ref, k_ref, v_ref, lf_ref, og_ref, gon_ref, o_ref, st_ref):
    @pl.when(pl.program_id(1) == 0)
    def _():
        st_ref[...] = jnp.zeros_like(st_ref)

    c = CHUNK
    row = lax.broadcasted_iota(jnp.int32, (c, c), 0)
    col = lax.broadcasted_iota(jnp.int32, (c, c), 1)
    causal = col <= row
    diag_mask = jnp.logical_and(causal, row // SUB_BLOCK == col // SUB_BLOCK)
    tri = causal.astype(BF16)

    def bcast_rows(rows):
        return jnp.concatenate([jnp.broadcast_to(r, (SUB_BLOCK, HEAD_DIM)) for r in rows], axis=0)

    heads = [slice(h * HEAD_DIM, (h + 1) * HEAD_DIM) for h in range(N_HEADS)]
    chunks = [slice(ci * c, (ci + 1) * c) for ci in range(SCAN_CHUNKS)]

    bcums = []
    for cs in chunks:
        lf = lf_ref[cs, :]
        lf_hi = lf.astype(BF16)
        lf_mid, lf_lo = _split_bf16(lf - lf_hi.astype(F32))
        bcums.append(jnp.dot(tri, lf_hi, preferred_element_type=F32)
                     + (jnp.dot(tri, lf_mid, preferred_element_type=F32)
                        + jnp.dot(tri, lf_lo, preferred_element_type=F32)))
    operands = {}
    for ci, cs in enumerate(chunks):
        for h, hs in enumerate(heads):
            b = bcums[ci][:, hs]
            q = q_ref[cs, hs].astype(F32)
            k = k_ref[cs, hs].astype(F32)
            ends = [b[i * SUB_BLOCK + SUB_BLOCK - 1:i * SUB_BLOCK + SUB_BLOCK, :] for i in range(N_SUB)]
            mids = [b[i * SUB_BLOCK + SUB_BLOCK // 2 - 1:i * SUB_BLOCK + SUB_BLOCK // 2, :] for i in range(N_SUB)]
            mid_full = bcast_rows(mids)
            b_last = ends[-1]
            zero_row = jnp.zeros_like(b_last)
            qm = q * jnp.exp(b - mid_full)
            km = k * jnp.exp(mid_full - b)
            q_parts, k_parts = [], []
            for j in range(N_SUB - 1):
                q_rows = [jnp.exp(mids[i] - ends[j]) if i > j else zero_row for i in range(N_SUB)]
                k_rows = [jnp.exp(ends[j] - mids[j]) if i == j else zero_row for i in range(N_SUB)]
                q_parts.append((qm * bcast_rows(q_rows)).astype(BF16))
                k_parts.append((km * bcast_rows(k_rows)).astype(BF16))
            q_in = (qm * bcast_rows([jnp.exp(m) for m in mids])).astype(BF16)
            k_out = (km * bcast_rows([jnp.exp(b_last - m) for m in mids])).astype(BF16)
            operands[ci, h] = (qm.astype(BF16), km.astype(BF16), jnp.concatenate(q_parts, axis=1),
                               jnp.concatenate(k_parts, axis=1), q_in, k_out, jnp.exp(b_last))
    products = {}
    for ci, cs in enumerate(chunks):
        for h, hs in enumerate(heads):
            qm, km, q_cat, k_cat, _, k_out, _ = operands[ci, h]
            products[ci, h] = (_nt_dot(qm, km), _nt_dot(q_cat, k_cat), _tn_dot(v_ref[cs, hs], k_out))
    within = {}
    for ci, cs in enumerate(chunks):
        for h, hs in enumerate(heads):
            same_block, earlier_blocks, _ = products[ci, h]
            scores = jnp.where(diag_mask, same_block, 0.0) + earlier_blocks
            within[ci, h] = jnp.dot(scores.astype(BF16), v_ref[cs, hs], preferred_element_type=F32)
    states = [st_ref[h] for h in range(N_HEADS)]
    outs = {}
    for ci in range(SCAN_CHUNKS):
        for h in range(N_HEADS):
            outs[ci, h] = within[ci, h] + _nt_dot(operands[ci, h][4], states[h].astype(BF16))
            states[h] = states[h] * operands[ci, h][6] + products[ci, h][2]
    for h in range(N_HEADS):
        st_ref[h] = states[h]
    for ci, cs in enumerate(chunks):
        for h, hs in enumerate(heads):
            o = outs[ci, h]
            ms = jnp.mean(o * o, axis=-1, keepdims=True)
            on = o * lax.rsqrt(ms + RMS_EPS) * gon_ref[:, hs]
            o_ref[cs, hs] = (on * og_ref[cs, hs].astype(F32)).astype(o_ref.dtype)


def _hgrn2_scan(acts, logf, g_onorm, n_batch):
    tp, d = logf.shape
    rows = SCAN_CHUNKS * CHUNK
    n_chunks = tp // n_batch // rows
    assert n_chunks * rows * n_batch == tp
    col = lambda group: pl.BlockSpec((rows, d), lambda b, c: (b * n_chunks + c, group))
    blk = col(0)
    return pl.pallas_call(
        _scan_kernel,
        grid=(n_batch, n_chunks),
        in_specs=[col(ACT_Q), col(ACT_K), col(ACT_V), blk, col(ACT_OG), pl.BlockSpec((1, d), lambda b, c: (0, 0))],
        out_specs=blk,
        out_shape=jax.ShapeDtypeStruct((tp, d), BF16),
        scratch_shapes=[pltpu.VMEM((N_HEADS, HEAD_DIM, HEAD_DIM), F32)],
        compiler_params=_cparams(("arbitrary", "arbitrary")),
        name="hgrn2_scan",
    )(acts, acts, acts, logf, acts, g_onorm)


def _merge_route_kernel(x_ref, meta_ref, on_ref, ap_ref, sgb_ref, wrnn_ref, wo_ref, gffn_ref, wr_ref, br_ref,
                        h1_ref, u2_ref, info_ref, route_ref, cnt_ref, tri_ref, carry_ref, colcnt_ref, logits_ref, hres_ref,
                        *, tiles_per_batch, n_tiles):
    i = pl.program_id(0)

    @pl.when(i == 0)
    def _():
        tm = on_ref.shape[0]
        r_i = lax.broadcasted_iota(jnp.int32, (tm, tm), 0)
        c_i = lax.broadcasted_iota(jnp.int32, (tm, tm), 1)
        tri_ref[...] = (c_i < r_i).astype(BF16)
        carry_ref[...] = jnp.zeros_like(carry_ref)
        colcnt_ref[...] = jnp.zeros_like(colcnt_ref)
        logits_ref[...] = jnp.zeros_like(logits_ref)

    refs = (x_ref, meta_ref, on_ref, ap_ref, sgb_ref, wrnn_ref, wo_ref, gffn_ref, wr_ref, br_ref,
            h1_ref, u2_ref, info_ref, route_ref, cnt_ref, tri_ref, carry_ref, colcnt_ref, logits_ref, hres_ref)

    @pl.when(i < n_tiles)
    def _():
        _merge_route_step(*refs, first_of_batch=i % tiles_per_batch == 0, routed=i > 0, with_matmuls=True)

    @pl.when(i == n_tiles)
    def _():
        _merge_route_step(*refs, first_of_batch=None, routed=True, with_matmuls=False)


def _merge_route_step(x_ref, meta_ref, on_ref, ap_ref, sgb_ref, wrnn_ref, wo_ref, gffn_ref, wr_ref, br_ref,
                      h1_ref, u2_ref, info_ref, route_ref, cnt_ref, tri_ref, carry_ref, colcnt_ref, logits_ref, hres_ref,
                      *, first_of_batch, routed, with_matmuls):
    tm = on_ref.shape[0]
    if with_matmuls:
        hres_ref[...] = _residual_tile(x_ref, meta_ref, first_of_batch)
    lane = lax.broadcasted_iota(jnp.int32, (tm, LANES), 1)
    cur = jnp.where(lane < N_EXPERTS, logits_ref[...], NEG_BIG)
    vals, idxs, sels = [], [], []

    def topk_round(cur):
        m = jnp.max(cur, axis=-1, keepdims=True)
        idx = jnp.min(jnp.where(cur == m, lane, LANES), axis=-1, keepdims=True)
        sel = lane == idx
        vals.append(m)
        idxs.append(idx)
        sels.append(sel)
        return jnp.where(sel, 2.0 * NEG_BIG, cur)

    if with_matmuls:
        w_hi, w_lo = _split_bf16(wr_ref[...])
        w_hi_lo = jnp.concatenate([w_hi, w_lo], axis=1)
        part = tm // MERGE_PARTS
        rows = [slice(p * part, (p + 1) * part) for p in range(MERGE_PARTS)]
        w_rnn = wrnn_ref[...].astype(BF16)
        w_out = wo_ref[...].astype(BF16)
        rs = [jnp.dot(on_ref[rw, :], w_rnn, preferred_element_type=F32) for rw in rows]
    cur = topk_round(cur)
    if with_matmuls:
        ys = [(ap_ref[rw, :].astype(F32) + sgb_ref[rw, :].astype(F32) * r).astype(BF16) for rw, r in zip(rows, rs)]
        h1s = [hres_ref[rw, :] + jnp.dot(y, w_out, preferred_element_type=F32) for rw, y in zip(rows, ys)]
    cur = topk_round(cur)
    if with_matmuls:
        u2s = []
        for rw, h1 in zip(rows, h1s):
            h1_ref[rw, :] = h1
            ms = jnp.mean(h1 * h1, axis=-1, keepdims=True)
            u2 = h1 * lax.rsqrt(ms + RMS_EPS) * gffn_ref[...]
            u2_ref[rw, :] = _pack_rows(u2)
            u2s.append(u2)
    cur = topk_round(cur)
    if with_matmuls:
        new_logits = []
        for u2 in u2s:
            u_hi, u_lo = _split_bf16(u2)
            both = jnp.dot(u_hi, w_hi_lo, preferred_element_type=F32)
            new_logits.append(both[:, :LANES] + (jnp.dot(u_lo, w_hi, preferred_element_type=F32) + both[:, LANES:])
                              + br_ref[...])
    cur = topk_round(cur)
    assert len(vals) == TOP_K
    exps = [jnp.exp(vk - vals[0]) for vk in vals]
    den = exps[0] + exps[1] + exps[2] + exps[3]
    onehot = jnp.logical_or(jnp.logical_or(sels[0], sels[1]), jnp.logical_or(sels[2], sels[3]))
    onehot = jnp.logical_and(onehot, routed)
    onehot_bf = onehot.astype(BF16)
    rank_all = jnp.dot(tri_ref[...], onehot_bf, preferred_element_type=F32) + carry_ref[...]
    carry_ref[...] += jnp.sum(onehot.astype(F32), axis=0, keepdims=True)
    colcnt_ref[...] += _tn_dot(onehot_bf, jnp.ones((tm, LANES), BF16))
    info = jnp.zeros((tm, LANES), F32)
    for kk in range(TOP_K):
        rank_k = jnp.sum(jnp.where(sels[kk], rank_all, 0.0), axis=-1, keepdims=True)
        info = jnp.where(lane == kk, exps[kk] / den, info)
        info = jnp.where(lane == TOP_K + kk, idxs[kk].astype(F32), info)
        info = jnp.where(lane == 2 * TOP_K + kk, rank_k, info)
    info_ref[...] = info
    route_ref[...] = info.T[0:ROUTE_ROWS, :]
    cnt_ref[...] = colcnt_ref[...]
    if with_matmuls:
        logits_ref[...] = jnp.concatenate(new_logits, axis=0)


def _merge_route(x2, meta, seq, on, ap, acts, w_rnn, w_o, g_ffn, w_router_pad, b_router_pad):
    tp, d = on.shape
    tm = ROW_TILE
    n_tiles = tp // tm
    tpb = (seq + CHUNK) // tm
    assert tpb * tm == seq + CHUNK
    this_tile = lambda i: (jnp.minimum(i, n_tiles - 1), 0)
    prev_tile = lambda i: (jnp.maximum(i - 1, 0), 0)
    row = pl.BlockSpec((tm, d), this_tile)
    vec = _const_spec((1, d))
    return pl.pallas_call(
        functools.partial(_merge_route_kernel, tiles_per_batch=tpb, n_tiles=n_tiles),
        grid=(n_tiles + 1,),
        in_specs=[_residual_spec(tm, d, seq, tpb, n_tiles), _const_spec(meta.shape), row, row,
                  pl.BlockSpec((tm, d), lambda i: (jnp.minimum(i, n_tiles - 1), ACT_SGB)),
                  _const_spec((d, d)), _const_spec((d, d)), vec, _const_spec((d, LANES)), _const_spec((1, LANES))],
        out_specs=[row, pl.BlockSpec((tm, PACK_WORDS), this_tile),
                   pl.BlockSpec((tm, LANES), prev_tile),
                   pl.BlockSpec((ROUTE_ROWS, tm), lambda i: (0, jnp.maximum(i - 1, 0))), _const_spec((LANES, LANES))],
        out_shape=[jax.ShapeDtypeStruct((tp, d), F32), jax.ShapeDtypeStruct((tp, PACK_WORDS), jnp.uint32),
                   jax.ShapeDtypeStruct((tp, LANES), F32), jax.ShapeDtypeStruct((ROUTE_ROWS, tp), F32),
                   jax.ShapeDtypeStruct((LANES, LANES), F32)],
        scratch_shapes=[pltpu.VMEM((tm, tm), BF16), pltpu.VMEM((1, LANES), F32),
                        pltpu.VMEM((LANES, LANES), F32), pltpu.VMEM((tm, LANES), F32), pltpu.VMEM((tm, d), F32)],
        compiler_params=_cparams(("arbitrary",)),
        name="merge_route",
    )(x2, meta, on, ap, acts, w_rnn, w_o, g_ffn, w_router_pad, b_router_pad)


def _slots_kernel(route_ref, cnt_ref, dest_ref, te_ref, pstart_ref):
    tl = route_ref.shape[1]
    n_tile_lanes = te_ref.shape[1]

    @pl.when(pl.program_id(0) == 0)
    def _():
        r_i = lax.broadcasted_iota(jnp.int32, (LANES, LANES), 0)
        c_i = lax.broadcasted_iota(jnp.int32, (LANES, LANES), 1)
        cnt = cnt_ref[...]
        tiles = jnp.floor((cnt + (EXPERT_TILE - 1)) / EXPERT_TILE)
        pend_col = jnp.dot((c_i <= r_i).astype(F32), tiles, preferred_element_type=F32,
                           precision=lax.Precision.HIGHEST)
        pstart_ref[...] = pend_col - tiles
        tile_id = lax.broadcasted_iota(jnp.int32, (LANES, n_tile_lanes), 1).astype(F32)
        exp_id = lax.broadcasted_iota(jnp.int32, (LANES, n_tile_lanes), 0)
        pend_wide = jnp.concatenate([pend_col] * (n_tile_lanes // LANES), axis=1)
        below = jnp.logical_and(pend_wide <= tile_id, exp_id < N_EXPERTS)
        te = jnp.minimum(jnp.sum(below.astype(F32), axis=0, keepdims=True), N_EXPERTS - 1.0)
        n_active = pend_col[N_EXPERTS - 1:N_EXPERTS, 0:1]
        owner = exp_id.astype(F32) == te
        cnt_wide = jnp.concatenate([cnt] * (n_tile_lanes // LANES), axis=1)
        first_tile = pend_wide - jnp.concatenate([tiles] * (n_tile_lanes // LANES), axis=1)
        left = jnp.sum(jnp.where(owner, cnt_wide - (tile_id - first_tile) * EXPERT_TILE, 0.0), axis=0, keepdims=True)
        valid = jnp.clip(left, 0.0, float(EXPERT_TILE))
        sub_t = lax.broadcasted_iota(jnp.int32, (SUBLANES, n_tile_lanes), 0)
        out = jnp.where(sub_t == 0, jnp.broadcast_to(te, (SUBLANES, n_tile_lanes)),
                        jnp.where(sub_t == 1, jnp.broadcast_to(n_active, (SUBLANES, n_tile_lanes)),
                                  jnp.broadcast_to(valid, (SUBLANES, n_tile_lanes))))
        te_ref[...] = out.astype(jnp.int32)

    route = route_ref[...]
    expert = lax.broadcasted_iota(jnp.int32, (N_EXPERTS, tl), 0)
    first_slot = jnp.concatenate([pstart_ref[0:N_EXPERTS, :]] * (tl // LANES), axis=1) * EXPERT_TILE
    rows = []
    for kk in range(TOP_K):
        idx_k = route[TOP_K + kk:TOP_K + kk + 1, :].astype(jnp.int32)
        start_k = jnp.sum(jnp.where(expert == idx_k, first_slot, 0.0), axis=0, keepdims=True)
        rows.append(start_k + route[2 * TOP_K + kk:2 * TOP_K + kk + 1, :])
    rows.append(jnp.zeros((SUBLANES - TOP_K, tl), F32))
    dest_ref[...] = jnp.concatenate(rows, axis=0).astype(jnp.int32)


def _slots(route, cnt, n_tiles_max):
    tp = route.shape[1]
    tl = tp // SLOTS_STEPS
    assert tl * SLOTS_STEPS == tp and tl % LANES == 0
    n_tile_lanes = -(-n_tiles_max // LANES) * LANES
    return pl.pallas_call(
        _slots_kernel,
        grid=(SLOTS_STEPS,),
        in_specs=[pl.BlockSpec((ROUTE_ROWS, tl), lambda i: (0, i)), _const_spec((LANES, LANES))],
        out_specs=[pl.BlockSpec((SUBLANES, tl), lambda i: (0, i)), _const_spec((8, n_tile_lanes))],
        out_shape=[jax.ShapeDtypeStruct((SUBLANES, tp), jnp.int32),
                   jax.ShapeDtypeStruct((8, n_tile_lanes), jnp.int32)],
        scratch_shapes=[pltpu.VMEM((LANES, LANES), F32)],
        compiler_params=_cparams(("arbitrary",)),
        name="slots",
    )(route, cnt)


def _sc_workers():
    sc = plsc.get_sparse_core_info()
    return sc.num_cores, sc.num_cores * sc.num_subcores


def _sc_split(tp, ring):
    n_workers = _sc_workers()[1]
    per_worker = tp // n_workers
    assert per_worker * n_workers == tp
    limit = min(SC_MAX_INDICES, SC_ROW_BUFFER_BYTES // (ring * PACK_WORDS * 4))
    return per_worker, max(c for c in range(SUBLANES, limit + 1, SUBLANES) if per_worker % c == 0)


def _sc_first_token(per_worker):
    return pl.multiple_of((lax.axis_index("s") * _sc_workers()[0] + lax.axis_index("c")) * per_worker, SUBLANES)


def _sc_ring_kernel(out_type, n_index_rows, chunk, ring):
    return functools.partial(
        pl.kernel, mesh=plsc.VectorSubcoreMesh(core_axis_name="c", subcore_axis_name="s"), out_type=out_type,
        scratch_types=[pltpu.VMEM((n_index_rows, chunk), jnp.int32), pltpu.VMEM((ring, chunk, PACK_WORDS), jnp.uint32),
                       pltpu.SemaphoreType.DMA, pltpu.SemaphoreType.DMA((ring,)), pltpu.SemaphoreType.DMA((ring,))])


def _wait_all(copies):
    for c in copies:
        c.wait()


def _dispatch(dest_flat, u2p, n_slots):
    tp = u2p.shape[0]
    per_worker, ch = _sc_split(tp, SCATTER_RING)
    n_chunks = per_worker // ch

    @_sc_ring_kernel(jax.ShapeDtypeStruct((n_slots, PACK_WORDS), jnp.uint32), n_chunks * TOP_K, ch, SCATTER_RING)
    def scatter_rows(u2_hbm, dest_hbm, xs_hbm, idx_v, rows_v, sem_idx, sem_load, sem_scatter):
        base = _sc_first_token(per_worker)
        _wait_all([pltpu.async_copy(dest_hbm.at[pl.ds(kk * tp + base + c * ch, ch)], idx_v.at[c * TOP_K + kk], sem_idx)
                   for c in range(n_chunks) for kk in range(TOP_K)])

        def load(c):
            slot = c % SCATTER_RING
            return pltpu.async_copy(u2_hbm.at[pl.ds(base + c * ch, ch)], rows_v.at[slot], sem_load.at[slot])

        def scatter(c):
            slot = c % SCATTER_RING
            return [pltpu.async_copy(rows_v.at[slot], xs_hbm.at[idx_v.at[c * TOP_K + kk]], sem_scatter.at[slot])
                    for kk in range(TOP_K)]

        loads, scatters = {}, {}
        for c in range(n_chunks + 1):
            if c < n_chunks:
                if c >= SCATTER_RING:
                    _wait_all(scatters.pop(c - SCATTER_RING))
                loads[c] = load(c)
            if c >= 1:
                loads.pop(c - 1).wait()
                scatters[c - 1] = scatter(c - 1)
        for copies in scatters.values():
            _wait_all(copies)

    return scatter_rows(u2p, dest_flat)


def _gather_expert_rows(dest_flat, ys):
    tp = dest_flat.shape[0] // TOP_K
    per_worker, ch = _sc_split(tp, GATHER_RING)
    units = [(c, kk) for c in range(per_worker // ch) for kk in range(TOP_K)]

    @_sc_ring_kernel(jax.ShapeDtypeStruct((TOP_K, tp, PACK_WORDS), jnp.uint32), len(units), ch, GATHER_RING)
    def gather_rows(ys_hbm, dest_hbm, out_hbm, idx_v, rows_v, sem_idx, sem_gather, sem_write):
        base = _sc_first_token(per_worker)
        _wait_all([pltpu.async_copy(dest_hbm.at[pl.ds(kk * tp + base + c * ch, ch)], idx_v.at[u], sem_idx)
                   for u, (c, kk) in enumerate(units)])

        def gather(u):
            slot = u % GATHER_RING
            return pltpu.async_copy(ys_hbm.at[idx_v.at[u]], rows_v.at[slot], sem_gather.at[slot])

        def write(u):
            c, kk = units[u]
            slot = u % GATHER_RING
            return pltpu.async_copy(rows_v.at[slot], out_hbm.at[kk, pl.ds(base + c * ch, ch)], sem_write.at[slot])

        gathers, writes = {}, {}
        for u in range(len(units) + GATHER_LAG):
            if u < len(units):
                if u >= GATHER_RING:
                    writes.pop(u - GATHER_RING).wait()
                gathers[u] = gather(u)
            if u >= GATHER_LAG:
                gathers.pop(u - GATHER_LAG).wait()
                writes[u - GATHER_LAG] = write(u - GATHER_LAG)
        _wait_all(writes.values())

    return gather_rows(ys, dest_flat)


def _expert_kernel(te_ref, nact_ref, valid_ref, x_ref, wgu_hbm, bgu_ref, wdn_hbm, bdn_ref, y_ref,
                   wgu_buf, wdn_buf, slot_ref, sem):
    i = pl.program_id(0)
    n_active = nact_ref[0]
    last_tile = pl.num_programs(0) - 1

    def fetch(expert, slot):
        return (pltpu.make_async_copy(wgu_hbm.at[expert], wgu_buf.at[slot], sem.at[slot, 0]),
                pltpu.make_async_copy(wdn_hbm.at[expert], wdn_buf.at[slot], sem.at[slot, 1]))

    @pl.when(i == 0)
    def _():
        slot_ref[0] = 1
        for copy in fetch(te_ref[0], 0):
            copy.start()

    @pl.when(i < n_active)
    def _():
        expert = te_ref[i]
        first_tile_of_expert = jnp.logical_or(i == 0, expert != te_ref[jnp.maximum(i - 1, 0)])

        @pl.when(first_tile_of_expert)
        def _():
            slot = 1 - slot_ref[0]
            slot_ref[0] = slot
            for copy in fetch(expert, slot):
                copy.wait()
            nxt = lax.while_loop(
                lambda j: jnp.logical_and(j < n_active, te_ref[jnp.minimum(j, last_tile)] == expert),
                lambda j: j + 1, i + 1)

            @pl.when(nxt < n_active)
            def _():
                for copy in fetch(te_ref[jnp.minimum(nxt, last_tile)], 1 - slot):
                    copy.start()

        slot = slot_ref[0]

        def mlp(rows):
            x = _unpack_rows(x_ref[0:rows, :]).astype(BF16)
            gu = jnp.dot(x, wgu_buf[slot].astype(BF16), preferred_element_type=F32) + bgu_ref[0]
            gate = jnp.minimum(gu[:, :D_FF], SWIGLU_LIMIT)
            up = jnp.clip(gu[:, D_FF:], -SWIGLU_LIMIT, SWIGLU_LIMIT)
            hdn = (up + 1.0) * (gate * _sigmoid(SWIGLU_ALPHA * gate))
            y = jnp.dot(hdn.astype(BF16), wdn_buf[slot].astype(BF16), preferred_element_type=F32) + bdn_ref[0]
            y_ref[0:rows, :] = _pack_rows(y)

        path = sum((valid_ref[i] > rows).astype(jnp.int32) for rows in EXPERT_PATH_ROWS[:-1])
        for k, rows in enumerate(EXPERT_PATH_ROWS):
            @pl.when(path == k)
            def _(rows=rows):
                mlp(rows)
                if rows < EXPERT_TILE:
                    y_ref[rows:, :] = jnp.zeros((EXPERT_TILE - rows, PACK_WORDS), y_ref.dtype)

    @pl.when(i >= n_active)
    def _():
        y_ref[...] = jnp.zeros_like(y_ref)


def _expert_mlp(te, nact, valid, xs, w_gu, b_gu, w_dn, b_dn):
    d = D_MODEL
    n_tiles = xs.shape[0] // EXPERT_TILE

    def tile_map(i, te_ref, nact_ref, valid_ref):
        return (jnp.minimum(i, nact_ref[0] - 1), 0)

    def exp_map(i, te_ref, nact_ref, valid_ref):
        return (te_ref[jnp.minimum(i, nact_ref[0] - 1)], 0, 0)

    grid_spec = pltpu.PrefetchScalarGridSpec(
        num_scalar_prefetch=3,
        grid=(n_tiles,),
        in_specs=[pl.BlockSpec((EXPERT_TILE, PACK_WORDS), tile_map),
                  pl.BlockSpec(memory_space=pl.ANY),
                  pl.BlockSpec((1, 1, 2 * D_FF), exp_map),
                  pl.BlockSpec(memory_space=pl.ANY),
                  pl.BlockSpec((1, 1, d), exp_map)],
        out_specs=pl.BlockSpec((EXPERT_TILE, PACK_WORDS), lambda i, te_ref, nact_ref, valid_ref: (i, 0)),
        scratch_shapes=[pltpu.VMEM((2, d, 2 * D_FF), F32), pltpu.VMEM((2, D_FF, d), F32),
                        pltpu.SMEM((1,), jnp.int32), pltpu.SemaphoreType.DMA((2, 2))],
    )
    return pl.pallas_call(
        _expert_kernel,
        grid_spec=grid_spec,
        out_shape=jax.ShapeDtypeStruct(xs.shape, jnp.uint32),
        compiler_params=_cparams(("arbitrary",)),
        name="expert_mlp",
    )(te, nact, valid, xs, w_gu, b_gu, w_dn, b_dn)


def _combine_kernel(gfin_ref, *refs):
    out_ref = refs[-1]
    for s in range(COMBINE_SUB):
        h1_ref, info_ref, yg_ref = refs[3 * s:3 * s + 3]
        info = info_ref[...]
        h2 = h1_ref[...]
        for kk in range(TOP_K):
            h2 = h2 + info[:, kk:kk + 1] * _unpack_rows(yg_ref[kk])
        ms = jnp.mean(h2 * h2, axis=-1, keepdims=True)
        out_ref[s * CHUNK:(s + 1) * CHUNK, :] = h2 * lax.rsqrt(ms + RMS_EPS) * gfin_ref[...]


def _combine(h1, info, g_final, yg, n_batch, seq):
    d = h1.shape[1]
    chunks = seq // CHUNK

    def padded(s):
        def index(j):
            c = j * COMBINE_SUB + s
            return (c // chunks) * (chunks + 1) + c % chunks + 1
        return index

    in_specs = [pl.BlockSpec((1, d), lambda j: (0, 0))]
    operands = [g_final]
    for s in range(COMBINE_SUB):
        chunk_of = padded(s)
        in_specs += [pl.BlockSpec((CHUNK, d), lambda j, f=chunk_of: (f(j), 0)),
                     pl.BlockSpec((CHUNK, LANES), lambda j, f=chunk_of: (f(j), 0)),
                     pl.BlockSpec((TOP_K, CHUNK, PACK_WORDS), lambda j, f=chunk_of: (0, f(j), 0))]
        operands += [h1, info, yg]
    return pl.pallas_call(
        _combine_kernel,
        grid=(n_batch * chunks // COMBINE_SUB,),
        in_specs=in_specs,
        out_specs=pl.BlockSpec((COMBINE_SUB * CHUNK, d), lambda j: (j, 0)),
        out_shape=jax.ShapeDtypeStruct((n_batch * seq, d), F32),
        compiler_params=_cparams(("parallel",)),
        name="combine",
    )(*operands)


def kernel(x, meta_tokens, lb_logits, g_mix, w_in, w_dw, b_dw, ln_g, ln_b, w_conv_out, b_conv_out,
           g_onorm, w_rnn_out, w_o, g_ffn, w_router, b_router, w_gate_up, b_gate_up, w_down, b_down,
           g_final):
    n_batch, seq, d = x.shape
    assert d == D_MODEL and w_in.shape[0] == 1, "single-layer block with D_MODEL features"
    assert seq % (CHUNK * COMBINE_SUB) == 0
    seq_pad = CHUNK + seq
    tp = n_batch * seq_pad
    assert tp % ROW_TILE == 0

    x2 = x.reshape(n_batch * seq, d)
    meta = meta_tokens.astype(x.dtype)
    vec = lambda a: a.reshape(1, -1).astype(F32)

    acts, logf = _in_proj(x2, meta, seq, vec(g_mix[0]), lb_logits.astype(F32), w_in[0])
    w_taps = w_dw[0].astype(F32).reshape(CONV_WIDTH, d // LANES, LANES).transpose(1, 0, 2)
    ap = _conv_branch(acts, w_taps, vec(b_dw[0]), vec(ln_g[0]), vec(ln_b[0]),
                      w_conv_out[0], vec(b_conv_out[0]))
    on = _hgrn2_scan(acts, logf, vec(g_onorm[0]), n_batch)
    w_router_pad = jnp.pad(w_router[0].astype(F32), ((0, 0), (0, LANES - N_EXPERTS)))
    b_router_pad = jnp.pad(vec(b_router[0]), ((0, 0), (0, LANES - N_EXPERTS)))
    h1, u2p, info, route, cnt = _merge_route(x2, meta, seq, on, ap, acts, w_rnn_out[0], w_o[0],
                                      vec(g_ffn[0]), w_router_pad, b_router_pad)

    n_tiles_max = -(-(tp * TOP_K + N_EXPERTS * (EXPERT_TILE - 1)) // EXPERT_TILE)
    dest, te = _slots(route, cnt, n_tiles_max)
    dest_flat = dest[:TOP_K].reshape(-1)
    xs = _dispatch(dest_flat, u2p, n_tiles_max * EXPERT_TILE)
    ys = _expert_mlp(te[0, :n_tiles_max], te[1, :1], te[2, :n_tiles_max], xs, w_gate_up[0],
                     b_gate_up[0].reshape(N_EXPERTS, 1, -1).astype(F32), w_down[0],
                     b_down[0].reshape(N_EXPERTS, 1, -1).astype(F32))
    yg = _gather_expert_rows(dest_flat, ys)
    out = _combine(h1, info, vec(g_final), yg, n_batch, seq)
    return out.reshape(n_batch, seq, d)
```

```python
import functools

import jax
import jax.numpy as jnp
from jax import lax
from jax.experimental import pallas as pl
from jax.experimental.pallas import tpu as pltpu
from jax.experimental.pallas import tpu_sc as plsc

F32 = jnp.float32
BF16 = jnp.bfloat16

D_MODEL = 1024
N_META = 16
CHUNK = 128
CHUNK_PAD = CHUNK - N_META
CONV_WIDTH = 31
HEAD_DIM = 128
N_HEADS = D_MODEL // HEAD_DIM
N_EXPERTS = 32
TOP_K = 4
D_FF = D_MODEL
SWIGLU_LIMIT = 7.0
SWIGLU_ALPHA = 1.702
RMS_EPS = 1e-6
LN_EPS = 1e-5

ACT_AGLU, ACT_Q, ACT_K, ACT_V, ACT_OG, ACT_SGA, ACT_SGB = range(7)
N_ACTS = 7

LANES = 128
SUBLANES = 8
SUB_BLOCK = 32
N_SUB = CHUNK // SUB_BLOCK
HALO = 32
ROW_TILE = 640
IN_TILE = 320
SCAN_CHUNKS = 5
NORM_PARTS = 2
MERGE_PARTS = 4
SLOTS_STEPS = 5
ROUTE_ROWS = 16
CONV_ROWS = 64
CONV_SUMS = 4
EXPERT_TILE = 1024
EXPERT_PATH_ROWS = tuple(EXPERT_TILE >> k for k in (3, 2, 1, 0))
SC_MAX_INDICES = 128
SC_ROW_BUFFER_BYTES = 448 * 1024
SCATTER_RING = 2
GATHER_RING = 4
GATHER_LAG = 2
COMBINE_SUB = 8
NEG_BIG = -1e30
VMEM_LIMIT = 56 * 1024 * 1024


def _sigmoid(x):
    return 1.0 / (1.0 + jnp.exp(-x))


def _cparams(sem):
    return pltpu.CompilerParams(dimension_semantics=sem, vmem_limit_bytes=VMEM_LIMIT)


def _const_spec(shape):
    nd = len(shape)
    return pl.BlockSpec(shape, lambda *_: (0,) * nd)


PACK_WORDS = D_MODEL // 2
HIGH_HALF = 0xFFFF0000


def _pack_rows(x):
    lo = lax.bitcast_convert_type(x[:, :PACK_WORDS].astype(BF16).astype(F32), jnp.uint32)
    hi = lax.bitcast_convert_type(x[:, PACK_WORDS:].astype(BF16).astype(F32), jnp.uint32)
    return (lo >> 16) | (hi & jnp.uint32(HIGH_HALF))


def _unpack_rows(w):
    lo = lax.bitcast_convert_type(w << 16, F32)
    hi = lax.bitcast_convert_type(w & jnp.uint32(HIGH_HALF), F32)
    return jnp.concatenate([lo, hi], axis=1)


def _residual_tile(x_ref, meta_ref, is_first):
    x = x_ref[...]
    if is_first is False:
        return x
    tm, d = x.shape
    prefix = jnp.concatenate([jnp.zeros((CHUNK_PAD, d), x.dtype), meta_ref[...]], axis=0)
    first = jnp.concatenate([prefix, x[:tm - CHUNK, :]], axis=0)
    return first if is_first is True else jnp.where(is_first, first, x)


def _residual_spec(tm, d, seq, tiles_per_batch, n_tiles):
    def start(i):
        i = jnp.minimum(i, n_tiles - 1)
        b, t = i // tiles_per_batch, i % tiles_per_batch
        return (pl.multiple_of(b * seq + jnp.maximum(t * tm - CHUNK, 0), SUBLANES), 0)
    return pl.BlockSpec((pl.Element(tm), pl.Element(d)), start)


def _in_proj_kernel(x_ref, meta_ref, g_ref, lbl_ref, w_ref, acts_ref, logf_ref, *, tiles_per_batch):
    first_of_batch = pl.program_id(0) % tiles_per_batch == 0
    for is_first in (True, False):
        pl.when(first_of_batch if is_first else jnp.logical_not(first_of_batch))(
            functools.partial(_in_proj_body, x_ref, meta_ref, g_ref, lbl_ref, w_ref, acts_ref, logf_ref, is_first))


def _in_proj_body(x_ref, meta_ref, g_ref, lbl_ref, w_ref, acts_ref, logf_ref, is_first):
    d = D_MODEL

    def put(group, value):
        acts_ref[:, group * d:(group + 1) * d] = value.astype(acts_ref.dtype)

    h = _residual_tile(x_ref, meta_ref, is_first)
    rows = h.shape[0] // NORM_PARTS
    u_parts = []
    for p in range(NORM_PARTS):
        hp = h[p * rows:(p + 1) * rows]
        ms = jnp.mean(hp * hp, axis=-1, keepdims=True)
        u_parts.append((hp * lax.rsqrt(ms + RMS_EPS) * g_ref[...]).astype(BF16))

    def proj(j):
        w = w_ref[:, j * d:(j + 1) * d].astype(BF16)
        return jnp.concatenate([jnp.dot(u, w, preferred_element_type=F32) for u in u_parts], axis=0)

    put(ACT_AGLU, proj(0) * _sigmoid(proj(1)))
    zq = proj(2)
    put(ACT_Q, zq * _sigmoid(zq))
    lbl = lbl_ref[...]
    e = jnp.exp(lbl - jnp.max(lbl, axis=0, keepdims=True))
    lb = e[0:1, :] / jnp.sum(e, axis=0, keepdims=True)
    s = _sigmoid(proj(3))
    logf_ref[...] = jnp.log(lb + (1.0 - lb) * s)
    put(ACT_K, (1.0 - lb) * (1.0 - s))
    zg = proj(5)
    put(ACT_OG, zg * _sigmoid(zg))
    put(ACT_SGA, _sigmoid(proj(6)))
    put(ACT_SGB, _sigmoid(proj(7)))
    put(ACT_V, proj(4))


def _in_proj(x2, meta, seq, g_mix, lb_logits, w_in):
    d = x2.shape[1]
    n_batch = x2.shape[0] // seq
    tm = IN_TILE
    tpb = (seq + CHUNK) // tm
    tp = n_batch * (seq + CHUNK)
    assert tpb * tm == seq + CHUNK and tm > CHUNK
    row = pl.BlockSpec((tm, d), lambda i: (i, 0))
    return pl.pallas_call(
        functools.partial(_in_proj_kernel, tiles_per_batch=tpb),
        grid=(tp // tm,),
        in_specs=[_residual_spec(tm, d, seq, tpb, tp // tm), _const_spec(meta.shape), _const_spec((1, d)),
                  _const_spec(lb_logits.shape),
                  pl.BlockSpec(w_in.shape, lambda i: (0, 0), pipeline_mode=pl.Buffered(1))],
        out_specs=[pl.BlockSpec((tm, N_ACTS * d), lambda i: (i, 0)), row],
        out_shape=[jax.ShapeDtypeStruct((tp, N_ACTS * d), BF16), jax.ShapeDtypeStruct((tp, d), F32)],
        compiler_params=_cparams(("parallel",)),
        name="in_proj",
    )(x2, meta, g_mix, lb_logits, w_in)


def _conv_kernel(halo_ref, cur_ref, wdw_ref, bdw_ref, lng_ref, lnb_ref, wout_ref, bout_ref, sga_ref,
                 out_ref, win_ref, acc_ref, shift_ref):
    tm = cur_ref.shape[0]
    win_ref[0:HALO, :] = halo_ref[...].astype(F32)
    win_ref[HALO:, :] = cur_ref[...].astype(F32)
    first_tap = HALO - (CONV_WIDTH - 1)
    for c in range(D_MODEL // LANES):
        lanes = slice(c * LANES, (c + 1) * LANES)
        n_rows = tm + HALO - SUBLANES
        for s in range(1, SUBLANES):
            shift_ref[s - 1, 0:n_rows, :] = win_ref[pl.ds(s, n_rows), lanes]
        bias = jnp.broadcast_to(bdw_ref[:, lanes], (CONV_ROWS, LANES))
        groups = CONV_ROWS // SUBLANES

        def chunk(r, carry, c=c, lanes=lanes, bias=bias):
            r0 = pl.multiple_of(r * CONV_ROWS, CONV_ROWS)
            accs = [bias.reshape(groups, SUBLANES, LANES)] + [None] * (CONV_SUMS - 1)
            for j in range(CONV_WIDTH):
                off = first_tap + j
                rows = pl.ds(r0 + off - off % SUBLANES, CONV_ROWS)
                w_j = wdw_ref[c, pl.ds(j, SUBLANES, stride=0), :]
                window = shift_ref[off % SUBLANES - 1, rows, :] if off % SUBLANES else win_ref[rows, lanes]
                term = w_j[None] * window.reshape(groups, SUBLANES, LANES)
                accs[j % CONV_SUMS] = term if accs[j % CONV_SUMS] is None else accs[j % CONV_SUMS] + term
            while len(accs) > 1:
                accs = [a + b for a, b in zip(accs[0::2], accs[1::2])]
            acc_ref[pl.ds(r0, CONV_ROWS), lanes] = accs[0].reshape(CONV_ROWS, LANES)
            return carry

        lax.fori_loop(0, tm // CONV_ROWS, chunk, 0)
    rows = tm // NORM_PARTS
    w_out = wout_ref[...].astype(BF16)
    ys = []
    for p in range(NORM_PARTS):
        a = acc_ref[p * rows:(p + 1) * rows, :]
        mu = jnp.mean(a, axis=-1, keepdims=True)
        ac = a - mu
        var = jnp.mean(ac * ac, axis=-1, keepdims=True)
        y = ac * lax.rsqrt(var + LN_EPS) * lng_ref[...] + lnb_ref[...]
        ys.append((y * _sigmoid(y)).astype(BF16))
    outs = [jnp.dot(y, w_out, preferred_element_type=F32) + bout_ref[...] for y in ys]
    for p in range(NORM_PARTS):
        part = slice(p * rows, (p + 1) * rows)
        out_ref[part, :] = (sga_ref[part, :].astype(F32) * outs[p]).astype(out_ref.dtype)


def _conv_branch(acts, w_dw, b_dw, ln_g, ln_b, w_out, b_out):
    tp, d = acts.shape[0], D_MODEL
    tm = ROW_TILE
    per = tm // HALO
    row = pl.BlockSpec((tm, d), lambda i: (i, 0))
    glu = pl.BlockSpec((tm, d), lambda i: (i, ACT_AGLU))
    gate = pl.BlockSpec((tm, d), lambda i: (i, ACT_SGA))
    halo = pl.BlockSpec((HALO, d), lambda i: (jnp.maximum(i * per - 1, 0), ACT_AGLU))
    vec = _const_spec((1, d))
    return pl.pallas_call(
        _conv_kernel,
        grid=(tp // tm,),
        in_specs=[halo, glu, _const_spec(w_dw.shape), vec, vec, vec, _const_spec((d, d)), vec, gate],
        out_specs=row,
        out_shape=jax.ShapeDtypeStruct((tp, d), BF16),
        scratch_shapes=[pltpu.VMEM((tm + HALO, d), F32), pltpu.VMEM((tm, d), F32),
                        pltpu.VMEM((SUBLANES - 1, tm + HALO - SUBLANES, LANES), F32)],
        compiler_params=_cparams(("parallel",)),
        name="conv_branch",
    )(acts, acts, w_dw, b_dw, ln_g, ln_b, w_out, b_out, acts)


def _split_bf16(x):
    hi = x.astype(BF16)
    return hi, (x - hi.astype(F32)).astype(BF16)


def _nt_dot(a, b):
    return lax.dot_general(a, b, (((1,), (1,)), ((), ())), preferred_element_type=F32)


def _tn_dot(a, b):
    return lax.dot_general(a, b, (((0,), (0,)), ((), ())), preferred_element_type=F32)


def _scan_kernel(q_ref, k_ref, v_ref, lf_ref, og_ref, gon_ref, o_ref, st_ref):
    @pl.when(pl.program_id(1) == 0)
    def _():
        st_ref[...] = jnp.zeros_like(st_ref)

    c = CHUNK
    row = lax.broadcasted_iota(jnp.int32, (c, c), 0)
    col = lax.broadcasted_iota(jnp.int32, (c, c), 1)
    causal = col <= row
    diag_mask = jnp.logical_and(causal, row // SUB_BLOCK == col // SUB_BLOCK)
    tri = causal.astype(BF16)

    def bcast_rows(rows):
        return jnp.concatenate([jnp.broadcast_to(r, (SUB_BLOCK, HEAD_DIM)) for r in rows], axis=0)

    heads = [slice(h * HEAD_DIM, (h + 1) * HEAD_DIM) for h in range(N_HEADS)]
    chunks = [slice(ci * c, (ci + 1) * c) for ci in range(SCAN_CHUNKS)]

    bcums = []
    for cs in chunks:
        lf = lf_ref[cs, :]
        lf_hi = lf.astype(BF16)
        lf_mid, lf_lo = _split_bf16(lf - lf_hi.astype(F32))
        bcums.append(jnp.dot(tri, lf_hi, preferred_element_type=F32)
                     + (jnp.dot(tri, lf_mid, preferred_element_type=F32)
                        + jnp.dot(tri, lf_lo, preferred_element_type=F32)))
    operands = {}
    for ci, cs in enumerate(chunks):
        for h, hs in enumerate(heads):
            b = bcums[ci][:, hs]
            q = q_ref[cs, hs].astype(F32)
            k = k_ref[cs, hs].astype(F32)
            ends = [b[i * SUB_BLOCK + SUB_BLOCK - 1:i * SUB_BLOCK + SUB_BLOCK, :] for i in range(N_SUB)]
            mids = [b[i * SUB_BLOCK + SUB_BLOCK // 2 - 1:i * SUB_BLOCK + SUB_BLOCK // 2, :] for i in range(N_SUB)]
            mid_full = bcast_rows(mids)
            b_last = ends[-1]
            zero_row = jnp.zeros_like(b_last)
            qm = q * jnp.exp(b - mid_full)
            km = k * jnp.exp(mid_full - b)
            q_parts, k_parts = [], []
            for j in range(N_SUB - 1):
                q_rows = [jnp.exp(mids[i] - ends[j]) if i > j else zero_row for i in range(N_SUB)]
                k_rows = [jnp.exp(ends[j] - mids[j]) if i == j else zero_row for i in range(N_SUB)]
                q_parts.append((qm * bcast_rows(q_rows)).astype(BF16))
                k_parts.append((km * bcast_rows(k_rows)).astype(BF16))
            q_in = (qm * bcast_rows([jnp.exp(m) for m in mids])).astype(BF16)
            k_out = (km * bcast_rows([jnp.exp(b_last - m) for m in mids])).astype(BF16)
            operands[ci, h] = (qm.astype(BF16), km.astype(BF16), jnp.concatenate(q_parts, axis=1),
                               jnp.concatenate(k_parts, axis=1), q_in, k_out, jnp.exp(b_last))
    products = {}
    for ci, cs in enumerate(chunks):
        for h, hs in enumerate(heads):
            qm, km, q_cat, k_cat, _, k_out, _ = operands[ci, h]
            products[ci, h] = (_nt_dot(qm, km), _nt_dot(q_cat, k_cat), _tn_dot(v_ref[cs, hs], k_out))
    within = {}
    for ci, cs in enumerate(chunks):
        for h, hs in enumerate(heads):
            same_block, earlier_blocks, _ = products[ci, h]
            scores = jnp.where(diag_mask, same_block, 0.0) + earlier_blocks
            within[ci, h] = jnp.dot(scores.astype(BF16), v_ref[cs, hs], preferred_element_type=F32)
    states = [st_ref[h] for h in range(N_HEADS)]
    outs = {}
    for ci in range(SCAN_CHUNKS):
        for h in range(N_HEADS):
            outs[ci, h] = within[ci, h] + _nt_dot(operands[ci, h][4], states[h].astype(BF16))
            states[h] = states[h] * operands[ci, h][6] + products[ci, h][2]
    for h in range(N_HEADS):
        st_ref[h] = states[h]
    for ci, cs in enumerate(chunks):
        for h, hs in enumerate(heads):
            o = outs[ci, h]
            ms = jnp.mean(o * o, axis=-1, keepdims=True)
            on = o * lax.rsqrt(ms + RMS_EPS) * gon_ref[:, hs]
            o_ref[cs, hs] = (on * og_ref[cs, hs].astype(F32)).astype(o_ref.dtype)


def _hgrn2_scan(acts, logf, g_onorm, n_batch):
    tp, d = logf.shape
    rows = SCAN_CHUNKS * CHUNK
    n_chunks = tp // n_batch // rows
    assert n_chunks * rows * n_batch == tp
    col = lambda group: pl.BlockSpec((rows, d), lambda b, c: (b * n_chunks + c, group))
    blk = col(0)
    return pl.pallas_call(
        _scan_kernel,
        grid=(n_batch, n_chunks),
        in_specs=[col(ACT_Q), col(ACT_K), col(ACT_V), blk, col(ACT_OG), pl.BlockSpec((1, d), lambda b, c: (0, 0))],
        out_specs=blk,
        out_shape=jax.ShapeDtypeStruct((tp, d), BF16),
        scratch_shapes=[pltpu.VMEM((N_HEADS, HEAD_DIM, HEAD_DIM), F32)],
        compiler_params=_cparams(("arbitrary", "arbitrary")),
        name="hgrn2_scan",
    )(acts, acts, acts, logf, acts, g_onorm)


def _merge_route_kernel(x_ref, meta_ref, on_ref, ap_ref, sgb_ref, wrnn_ref, wo_ref, gffn_ref, wr_ref, br_ref,
                        h1_ref, u2_ref, info_ref, route_ref, cnt_ref, tri_ref, carry_ref, colcnt_ref, logits_ref, hres_ref,
                        *, tiles_per_batch, n_tiles):
    i = pl.program_id(0)

    @pl.when(i == 0)
    def _():
        tm = on_ref.shape[0]
        r_i = lax.broadcasted_iota(jnp.int32, (tm, tm), 0)
        c_i = lax.broadcasted_iota(jnp.int32, (tm, tm), 1)
        tri_ref[...] = (c_i < r_i).astype(BF16)
        carry_ref[...] = jnp.zeros_like(carry_ref)
        colcnt_ref[...] = jnp.zeros_like(colcnt_ref)
        logits_ref[...] = jnp.zeros_like(logits_ref)

    refs = (x_ref, meta_ref, on_ref, ap_ref, sgb_ref, wrnn_ref, wo_ref, gffn_ref, wr_ref, br_ref,
            h1_ref, u2_ref, info_ref, route_ref, cnt_ref, tri_ref, carry_ref, colcnt_ref, logits_ref, hres_ref)

    @pl.when(i < n_tiles)
    def _():
        _merge_route_step(*refs, first_of_batch=i % tiles_per_batch == 0, routed=i > 0, with_matmuls=True)

    @pl.when(i == n_tiles)
    def _():
        _merge_route_step(*refs, first_of_batch=None, routed=True, with_matmuls=False)


def _merge_route_step(x_ref, meta_ref, on_ref, ap_ref, sgb_ref, wrnn_ref, wo_ref, gffn_ref, wr_ref, br_ref,
                      h1_ref, u2_ref, info_ref, route_ref, cnt_ref, tri_ref, carry_ref, colcnt_ref, logits_ref, hres_ref,
                      *, first_of_batch, routed, with_matmuls):
    tm = on_ref.shape[0]
    if with_matmuls:
        hres_ref[...] = _residual_tile(x_ref, meta_ref, first_of_batch)
    lane = lax.broadcasted_iota(jnp.int32, (tm, LANES), 1)
    cur = jnp.where(lane < N_EXPERTS, logits_ref[...], NEG_BIG)
    vals, idxs, sels = [], [], []

    def topk_round(cur):
        m = jnp.max(cur, axis=-1, keepdims=True)
        idx = jnp.min(jnp.where(cur == m, lane, LANES), axis=-1, keepdims=True)
        sel = lane == idx
        vals.append(m)
        idxs.append(idx)
        sels.append(sel)
        return jnp.where(sel, 2.0 * NEG_BIG, cur)

    if with_matmuls:
        w_hi, w_lo = _split_bf16(wr_ref[...])
        w_hi_lo = jnp.concatenate([w_hi, w_lo], axis=1)
        part = tm // MERGE_PARTS
        rows = [slice(p * part, (p + 1) * part) for p in range(MERGE_PARTS)]
        w_rnn = wrnn_ref[...].astype(BF16)
        w_out = wo_ref[...].astype(BF16)
        rs = [jnp.dot(on_ref[rw, :], w_rnn, preferred_element_type=F32) for rw in rows]
    cur = topk_round(cur)
    if with_matmuls:
        ys = [(ap_ref[rw, :].astype(F32) + sgb_ref[rw, :].astype(F32) * r).astype(BF16) for rw, r in zip(rows, rs)]
        h1s = [hres_ref[rw, :] + jnp.dot(y, w_out, preferred_element_type=F32) for rw, y in zip(rows, ys)]
    cur = topk_round(cur)
    if with_matmuls:
        u2s = []
        for rw, h1 in zip(rows, h1s):
            h1_ref[rw, :] = h1
            ms = jnp.mean(h1 * h1, axis=-1, keepdims=True)
            u2 = h1 * lax.rsqrt(ms + RMS_EPS) * gffn_ref[...]
            u2_ref[rw, :] = _pack_rows(u2)
            u2s.append(u2)
    cur = topk_round(cur)
    if with_matmuls:
        new_logits = []
        for u2 in u2s:
            u_hi, u_lo = _split_bf16(u2)
            both = jnp.dot(u_hi, w_hi_lo, preferred_element_type=F32)
            new_logits.append(both[:, :LANES] + (jnp.dot(u_lo, w_hi, preferred_element_type=F32) + both[:, LANES:])
                              + br_ref[...])
    cur = topk_round(cur)
    assert len(vals) == TOP_K
    exps = [jnp.exp(vk - vals[0]) for vk in vals]
    den = exps[0] + exps[1] + exps[2] + exps[3]
    onehot = jnp.logical_or(jnp.logical_or(sels[0], sels[1]), jnp.logical_or(sels[2], sels[3]))
    onehot = jnp.logical_and(onehot, routed)
    onehot_bf = onehot.astype(BF16)
    rank_all = jnp.dot(tri_ref[...], onehot_bf, preferred_element_type=F32) + carry_ref[...]
    carry_ref[...] += jnp.sum(onehot.astype(F32), axis=0, keepdims=True)
    colcnt_ref[...] += _tn_dot(onehot_bf, jnp.ones((tm, LANES), BF16))
    info = jnp.zeros((tm, LANES), F32)
    for kk in range(TOP_K):
        rank_k = jnp.sum(jnp.where(sels[kk], rank_all, 0.0), axis=-1, keepdims=True)
        info = jnp.where(lane == kk, exps[kk] / den, info)
        info = jnp.where(lane == TOP_K + kk, idxs[kk].astype(F32), info)
        info = jnp.where(lane == 2 * TOP_K + kk, rank_k, info)
    info_ref[...] = info
    route_ref[...] = info.T[0:ROUTE_ROWS, :]
    cnt_ref[...] = colcnt_ref[...]
    if with_matmuls:
        logits_ref[...] = jnp.concatenate(new_logits, axis=0)


def _merge_route(x2, meta, seq, on, ap, acts, w_rnn, w_o, g_ffn, w_router_pad, b_router_pad):
    tp, d = on.shape
    tm = ROW_TILE
    n_tiles = tp // tm
    tpb = (seq + CHUNK) // tm
    assert tpb * tm == seq + CHUNK
    this_tile = lambda i: (jnp.minimum(i, n_tiles - 1), 0)
    prev_tile = lambda i: (jnp.maximum(i - 1, 0), 0)
    row = pl.BlockSpec((tm, d), this_tile)
    vec = _const_spec((1, d))
    return pl.pallas_call(
        functools.partial(_merge_route_kernel, tiles_per_batch=tpb, n_tiles=n_tiles),
        grid=(n_tiles + 1,),
        in_specs=[_residual_spec(tm, d, seq, tpb, n_tiles), _const_spec(meta.shape), row, row,
                  pl.BlockSpec((tm, d), lambda i: (jnp.minimum(i, n_tiles - 1), ACT_SGB)),
                  _const_spec((d, d)), _const_spec((d, d)), vec, _const_spec((d, LANES)), _const_spec((1, LANES))],
        out_specs=[row, pl.BlockSpec((tm, PACK_WORDS), this_tile),
                   pl.BlockSpec((tm, LANES), prev_tile),
                   pl.BlockSpec((ROUTE_ROWS, tm), lambda i: (0, jnp.maximum(i - 1, 0))), _const_spec((LANES, LANES))],
        out_shape=[jax.ShapeDtypeStruct((tp, d), F32), jax.ShapeDtypeStruct((tp, PACK_WORDS), jnp.uint32),
                   jax.ShapeDtypeStruct((tp, LANES), F32), jax.ShapeDtypeStruct((ROUTE_ROWS, tp), F32),
                   jax.ShapeDtypeStruct((LANES, LANES), F32)],
        scratch_shapes=[pltpu.VMEM((tm, tm), BF16), pltpu.VMEM((1, LANES), F32),
                        pltpu.VMEM((LANES, LANES), F32), pltpu.VMEM((tm, LANES), F32), pltpu.VMEM((tm, d), F32)],
        compiler_params=_cparams(("arbitrary",)),
        name="merge_route",
    )(x2, meta, on, ap, acts, w_rnn, w_o, g_ffn, w_router_pad, b_router_pad)


def _slots_kernel(route_ref, cnt_ref, dest_ref, te_ref, pstart_ref):
    tl = route_ref.shape[1]
    n_tile_lanes = te_ref.shape[1]

    @pl.when(pl.program_id(0) == 0)
    def _():
        r_i = lax.broadcasted_iota(jnp.int32, (LANES, LANES), 0)
        c_i = lax.broadcasted_iota(jnp.int32, (LANES, LANES), 1)
        cnt = cnt_ref[...]
        tiles = jnp.floor((cnt + (EXPERT_TILE - 1)) / EXPERT_TILE)
        pend_col = jnp.dot((c_i <= r_i).astype(F32), tiles, preferred_element_type=F32,
                           precision=lax.Precision.HIGHEST)
        pstart_ref[...] = pend_col - tiles
        tile_id = lax.broadcasted_iota(jnp.int32, (LANES, n_tile_lanes), 1).astype(F32)
        exp_id = lax.broadcasted_iota(jnp.int32, (LANES, n_tile_lanes), 0)
        pend_wide = jnp.concatenate([pend_col] * (n_tile_lanes // LANES), axis=1)
        below = jnp.logical_and(pend_wide <= tile_id, exp_id < N_EXPERTS)
        te = jnp.minimum(jnp.sum(below.astype(F32), axis=0, keepdims=True), N_EXPERTS - 1.0)
        n_active = pend_col[N_EXPERTS - 1:N_EXPERTS, 0:1]
        owner = exp_id.astype(F32) == te
        cnt_wide = jnp.concatenate([cnt] * (n_tile_lanes // LANES), axis=1)
        first_tile = pend_wide - jnp.concatenate([tiles] * (n_tile_lanes // LANES), axis=1)
        left = jnp.sum(jnp.where(owner, cnt_wide - (tile_id - first_tile) * EXPERT_TILE, 0.0), axis=0, keepdims=True)
        valid = jnp.clip(left, 0.0, float(EXPERT_TILE))
        sub_t = lax.broadcasted_iota(jnp.int32, (SUBLANES, n_tile_lanes), 0)
        out = jnp.where(sub_t == 0, jnp.broadcast_to(te, (SUBLANES, n_tile_lanes)),
                        jnp.where(sub_t == 1, jnp.broadcast_to(n_active, (SUBLANES, n_tile_lanes)),
                                  jnp.broadcast_to(valid, (SUBLANES, n_tile_lanes))))
        te_ref[...] = out.astype(jnp.int32)

    route = route_ref[...]
    expert = lax.broadcasted_iota(jnp.int32, (N_EXPERTS, tl), 0)
    first_slot = jnp.concatenate([pstart_ref[0:N_EXPERTS, :]] * (tl // LANES), axis=1) * EXPERT_TILE
    rows = []
    for kk in range(TOP_K):
        idx_k = route[TOP_K + kk:TOP_K + kk + 1, :].astype(jnp.int32)
        start_k = jnp.sum(jnp.where(expert == idx_k, first_slot, 0.0), axis=0, keepdims=True)
        rows.append(start_k + route[2 * TOP_K + kk:2 * TOP_K + kk + 1, :])
    rows.append(jnp.zeros((SUBLANES - TOP_K, tl), F32))
    dest_ref[...] = jnp.concatenate(rows, axis=0).astype(jnp.int32)


def _slots(route, cnt, n_tiles_max):
    tp = route.shape[1]
    tl = tp // SLOTS_STEPS
    assert tl * SLOTS_STEPS == tp and tl % LANES == 0
    n_tile_lanes = -(-n_tiles_max // LANES) * LANES
    return pl.pallas_call(
        _slots_kernel,
        grid=(SLOTS_STEPS,),
        in_specs=[pl.BlockSpec((ROUTE_ROWS, tl), lambda i: (0, i)), _const_spec((LANES, LANES))],
        out_specs=[pl.BlockSpec((SUBLANES, tl), lambda i: (0, i)), _const_spec((8, n_tile_lanes))],
        out_shape=[jax.ShapeDtypeStruct((SUBLANES, tp), jnp.int32),
                   jax.ShapeDtypeStruct((8, n_tile_lanes), jnp.int32)],
        scratch_shapes=[pltpu.VMEM((LANES, LANES), F32)],
        compiler_params=_cparams(("arbitrary",)),
        name="slots",
    )(route, cnt)


def _sc_workers():
    sc = plsc.get_sparse_core_info()
    return sc.num_cores, sc.num_cores * sc.num_subcores


def _sc_split(tp, ring):
    n_workers = _sc_workers()[1]
    per_worker = tp // n_workers
    assert per_worker * n_workers == tp
    limit = min(SC_MAX_INDICES, SC_ROW_BUFFER_BYTES // (ring * PACK_WORDS * 4))
    return per_worker, max(c for c in range(SUBLANES, limit + 1, SUBLANES) if per_worker % c == 0)


def _sc_first_token(per_worker):
    return pl.multiple_of((lax.axis_index("s") * _sc_workers()[0] + lax.axis_index("c")) * per_worker, SUBLANES)


def _sc_ring_kernel(out_type, n_index_rows, chunk, ring):
    return functools.partial(
        pl.kernel, mesh=plsc.VectorSubcoreMesh(core_axis_name="c", subcore_axis_name="s"), out_type=out_type,
        scratch_types=[pltpu.VMEM((n_index_rows, chunk), jnp.int32), pltpu.VMEM((ring, chunk, PACK_WORDS), jnp.uint32),
                       pltpu.SemaphoreType.DMA, pltpu.SemaphoreType.DMA((ring,)), pltpu.SemaphoreType.DMA((ring,))])


def _wait_all(copies):
    for c in copies:
        c.wait()


def _dispatch(dest_flat, u2p, n_slots):
    tp = u2p.shape[0]
    per_worker, ch = _sc_split(tp, SCATTER_RING)
    n_chunks = per_worker // ch

    @_sc_ring_kernel(jax.ShapeDtypeStruct((n_slots, PACK_WORDS), jnp.uint32), n_chunks * TOP_K, ch, SCATTER_RING)
    def scatter_rows(u2_hbm, dest_hbm, xs_hbm, idx_v, rows_v, sem_idx, sem_load, sem_scatter):
        base = _sc_first_token(per_worker)
        _wait_all([pltpu.async_copy(dest_hbm.at[pl.ds(kk * tp + base + c * ch, ch)], idx_v.at[c * TOP_K + kk], sem_idx)
                   for c in range(n_chunks) for kk in range(TOP_K)])

        def load(c):
            slot = c % SCATTER_RING
            return pltpu.async_copy(u2_hbm.at[pl.ds(base + c * ch, ch)], rows_v.at[slot], sem_load.at[slot])

        def scatter(c):
            slot = c % SCATTER_RING
            return [pltpu.async_copy(rows_v.at[slot], xs_hbm.at[idx_v.at[c * TOP_K + kk]], sem_scatter.at[slot])
                    for kk in range(TOP_K)]

        loads, scatters = {}, {}
        for c in range(n_chunks + 1):
            if c < n_chunks:
                if c >= SCATTER_RING:
                    _wait_all(scatters.pop(c - SCATTER_RING))
                loads[c] = load(c)
            if c >= 1:
                loads.pop(c - 1).wait()
                scatters[c - 1] = scatter(c - 1)
        for copies in scatters.values():
            _wait_all(copies)

    return scatter_rows(u2p, dest_flat)


def _gather_expert_rows(dest_flat, ys):
    tp = dest_flat.shape[0] // TOP_K
    per_worker, ch = _sc_split(tp, GATHER_RING)
    units = [(c, kk) for c in range(per_worker // ch) for kk in range(TOP_K)]

    @_sc_ring_kernel(jax.ShapeDtypeStruct((TOP_K, tp, PACK_WORDS), jnp.uint32), len(units), ch, GATHER_RING)
    def gather_rows(ys_hbm, dest_hbm, out_hbm, idx_v, rows_v, sem_idx, sem_gather, sem_write):
        base = _sc_first_token(per_worker)
        _wait_all([pltpu.async_copy(dest_hbm.at[pl.ds(kk * tp + base + c * ch, ch)], idx_v.at[u], sem_idx)
                   for u, (c, kk) in enumerate(units)])

        def gather(u):
            slot = u % GATHER_RING
            return pltpu.async_copy(ys_hbm.at[idx_v.at[u]], rows_v.at[slot], sem_gather.at[slot])

        def write(u):
            c, kk = units[u]
            slot = u % GATHER_RING
            return pltpu.async_copy(rows_v.at[slot], out_hbm.at[kk, pl.ds(base + c * ch, ch)], sem_write.at[slot])

        gathers, writes = {}, {}
        for u in range(len(units) + GATHER_LAG):
            if u < len(units):
                if u >= GATHER_RING:
                    writes.pop(u - GATHER_RING).wait()
                gathers[u] = gather(u)
            if u >= GATHER_LAG:
                gathers.pop(u - GATHER_LAG).wait()
                writes[u - GATHER_LAG] = write(u - GATHER_LAG)
        _wait_all(writes.values())

    return gather_rows(ys, dest_flat)


def _expert_kernel(te_ref, nact_ref, valid_ref, x_ref, wgu_hbm, bgu_ref, wdn_hbm, bdn_ref, y_ref,
                   wgu_buf, wdn_buf, slot_ref, sem):
    i = pl.program_id(0)
    n_active = nact_ref[0]
    last_tile = pl.num_programs(0) - 1

    def fetch(expert, slot):
        return (pltpu.make_async_copy(wgu_hbm.at[expert], wgu_buf.at[slot], sem.at[slot, 0]),
                pltpu.make_async_copy(wdn_hbm.at[expert], wdn_buf.at[slot], sem.at[slot, 1]))

    @pl.when(i == 0)
    def _():
        slot_ref[0] = 1
        for copy in fetch(te_ref[0], 0):
            copy.start()

    @pl.when(i < n_active)
    def _():
        expert = te_ref[i]
        first_tile_of_expert = jnp.logical_or(i == 0, expert != te_ref[jnp.maximum(i - 1, 0)])

        @pl.when(first_tile_of_expert)
        def _():
            slot = 1 - slot_ref[0]
            slot_ref[0] = slot
            for copy in fetch(expert, slot):
                copy.wait()
            nxt = lax.while_loop(
                lambda j: jnp.logical_and(j < n_active, te_ref[jnp.minimum(j, last_tile)] == expert),
                lambda j: j + 1, i + 1)

            @pl.when(nxt < n_active)
            def _():
                for copy in fetch(te_ref[jnp.minimum(nxt, last_tile)], 1 - slot):
                    copy.start()

        slot = slot_ref[0]

        def mlp(rows):
            x = _unpack_rows(x_ref[0:rows, :]).astype(BF16)
            gu = jnp.dot(x, wgu_buf[slot].astype(BF16), preferred_element_type=F32) + bgu_ref[0]
            gate = jnp.minimum(gu[:, :D_FF], SWIGLU_LIMIT)
            up = jnp.clip(gu[:, D_FF:], -SWIGLU_LIMIT, SWIGLU_LIMIT)
            hdn = (up + 1.0) * (gate * _sigmoid(SWIGLU_ALPHA * gate))
            y = jnp.dot(hdn.astype(BF16), wdn_buf[slot].astype(BF16), preferred_element_type=F32) + bdn_ref[0]
            y_ref[0:rows, :] = _pack_rows(y)

        path = sum((valid_ref[i] > rows).astype(jnp.int32) for rows in EXPERT_PATH_ROWS[:-1])
        for k, rows in enumerate(EXPERT_PATH_ROWS):
            @pl.when(path == k)
            def _(rows=rows):
                mlp(rows)
                if rows < EXPERT_TILE:
                    y_ref[rows:, :] = jnp.zeros((EXPERT_TILE - rows, PACK_WORDS), y_ref.dtype)

    @pl.when(i >= n_active)
    def _():
        y_ref[...] = jnp.zeros_like(y_ref)


def _expert_mlp(te, nact, valid, xs, w_gu, b_gu, w_dn, b_dn):
    d = D_MODEL
    n_tiles = xs.shape[0] // EXPERT_TILE

    def tile_map(i, te_ref, nact_ref, valid_ref):
        return (jnp.minimum(i, nact_ref[0] - 1), 0)

    def exp_map(i, te_ref, nact_ref, valid_ref):
        return (te_ref[jnp.minimum(i, nact_ref[0] - 1)], 0, 0)

    grid_spec = pltpu.PrefetchScalarGridSpec(
        num_scalar_prefetch=3,
        grid=(n_tiles,),
        in_specs=[pl.BlockSpec((EXPERT_TILE, PACK_WORDS), tile_map),
                  pl.BlockSpec(memory_space=pl.ANY),
                  pl.BlockSpec((1, 1, 2 * D_FF), exp_map),
                  pl.BlockSpec(memory_space=pl.ANY),
                  pl.BlockSpec((1, 1, d), exp_map)],
        out_specs=pl.BlockSpec((EXPERT_TILE, PACK_WORDS), lambda i, te_ref, nact_ref, valid_ref: (i, 0)),
        scratch_shapes=[pltpu.VMEM((2, d, 2 * D_FF), F32), pltpu.VMEM((2, D_FF, d), F32),
                        pltpu.SMEM((1,), jnp.int32), pltpu.SemaphoreType.DMA((2, 2))],
    )
    return pl.pallas_call(
        _expert_kernel,
        grid_spec=grid_spec,
        out_shape=jax.ShapeDtypeStruct(xs.shape, jnp.uint32),
        compiler_params=_cparams(("arbitrary",)),
        name="expert_mlp",
    )(te, nact, valid, xs, w_gu, b_gu, w_dn, b_dn)


def _combine_kernel(gfin_ref, *refs):
    out_ref = refs[-1]
    for s in range(COMBINE_SUB):
        h1_ref, info_ref, yg_ref = refs[3 * s:3 * s + 3]
        info = info_ref[...]
        h2 = h1_ref[...]
        for kk in range(TOP_K):
            h2 = h2 + info[:, kk:kk + 1] * _unpack_rows(yg_ref[kk])
        ms = jnp.mean(h2 * h2, axis=-1, keepdims=True)
        out_ref[s * CHUNK:(s + 1) * CHUNK, :] = h2 * lax.rsqrt(ms + RMS_EPS) * gfin_ref[...]


def _combine(h1, info, g_final, yg, n_batch, seq):
    d = h1.shape[1]
    chunks = seq // CHUNK

    def padded(s):
        def index(j):
            c = j * COMBINE_SUB + s
            return (c // chunks) * (chunks + 1) + c % chunks + 1
        return index

    in_specs = [pl.BlockSpec((1, d), lambda j: (0, 0))]
    operands = [g_final]
    for s in range(COMBINE_SUB):
        chunk_of = padded(s)
        in_specs += [pl.BlockSpec((CHUNK, d), lambda j, f=chunk_of: (f(j), 0)),
                     pl.BlockSpec((CHUNK, LANES), lambda j, f=chunk_of: (f(j), 0)),
                     pl.BlockSpec((TOP_K, CHUNK, PACK_WORDS), lambda j, f=chunk_of: (0, f(j), 0))]
        operands += [h1, info, yg]
    return pl.pallas_call(
        _combine_kernel,
        grid=(n_batch * chunks // COMBINE_SUB,),
        in_specs=in_specs,
        out_specs=pl.BlockSpec((COMBINE_SUB * CHUNK, d), lambda j: (j, 0)),
        out_shape=jax.ShapeDtypeStruct((n_batch * seq, d), F32),
        compiler_params=_cparams(("parallel",)),
        name="combine",
    )(*operands)


def kernel(x, meta_tokens, lb_logits, g_mix, w_in, w_dw, b_dw, ln_g, ln_b, w_conv_out, b_conv_out,
           g_onorm, w_rnn_out, w_o, g_ffn, w_router, b_router, w_gate_up, b_gate_up, w_down, b_down,
           g_final):
    n_batch, seq, d = x.shape
    assert d == D_MODEL and w_in.shape[0] == 1, "single-layer block with D_MODEL features"
    assert seq % (CHUNK * COMBINE_SUB) == 0
    seq_pad = CHUNK + seq
    tp = n_batch * seq_pad
    assert tp % ROW_TILE == 0

    x2 = x.reshape(n_batch * seq, d)
    meta = meta_tokens.astype(x.dtype)
    vec = lambda a: a.reshape(1, -1).astype(F32)

    acts, logf = _in_proj(x2, meta, seq, vec(g_mix[0]), lb_logits.astype(F32), w_in[0])
    w_taps = w_dw[0].astype(F32).reshape(CONV_WIDTH, d // LANES, LANES).transpose(1, 0, 2)
    ap = _conv_branch(acts, w_taps, vec(b_dw[0]), vec(ln_g[0]), vec(ln_b[0]),
                      w_conv_out[0], vec(b_conv_out[0]))
    on = _hgrn2_scan(acts, logf, vec(g_onorm[0]), n_batch)
    w_router_pad = jnp.pad(w_router[0].astype(F32), ((0, 0), (0, LANES - N_EXPERTS)))
    b_router_pad = jnp.pad(vec(b_router[0]), ((0, 0), (0, LANES - N_EXPERTS)))
    h1, u2p, info, route, cnt = _merge_route(x2, meta, seq, on, ap, acts, w_rnn_out[0], w_o[0],
                                      vec(g_ffn[0]), w_router_pad, b_router_pad)

    n_tiles_max = -(-(tp * TOP_K + N_EXPERTS * (EXPERT_TILE - 1)) // EXPERT_TILE)
    dest, te = _slots(route, cnt, n_tiles_max)
    dest_flat = dest[:TOP_K].reshape(-1)
    xs = _dispatch(dest_flat, u2p, n_tiles_max * EXPERT_TILE)
    ys = _expert_mlp(te[0, :n_tiles_max], te[1, :1], te[2, :n_tiles_max], xs, w_gate_up[0],
                     b_gate_up[0].reshape(N_EXPERTS, 1, -1).astype(F32), w_down[0],
                     b_down[0].reshape(N_EXPERTS, 1, -1).astype(F32))
    yg = _gather_expert_rows(dest_flat, ys)
    out = _combine(h1, info, vec(g_final), yg, n_batch, seq)
    return out.reshape(n_batch, seq, d)
```
